```python
import jax, jax.numpy as jnp
from jax import lax
import numpy as np

D_MODEL = 2048
BATCH = 8
SEQ = 4096
DEPTH = 2

N_A = DEPTH // 2
N_B = DEPTH - N_A
CHUNK = 128
D_A = D_MODEL
G_A = 16
GD_A = D_A // G_A
N_HEADS = 16
HEAD_DIM = 128
D_B = N_HEADS * HEAD_DIM
BLOCK_Q = 128
D_FF = 5632
CONV_W = 3
EPS = 1e-6

kernel_name = "yoco_gmlp_stickbreaking_hybrid"


def rms_norm(x, g):
    xf = x.astype(jnp.float32)
    ms = jnp.mean(xf * xf, axis=-1, keepdims=True)
    return (xf * lax.rsqrt(ms + EPS) * g.astype(jnp.float32)).astype(x.dtype)


def chunked_gmlp(h, w_in, v_g, w_s, b_s, w_out):
    B, S, _ = h.shape
    uv = jax.nn.gelu(h @ w_in)
    u, v = jnp.split(uv, 2, axis=-1)
    v = rms_norm(v.reshape(B, S, G_A, GD_A), v_g.reshape(G_A, GD_A))
    v = v.reshape(B, S // CHUNK, CHUNK, G_A, GD_A)
    causal = jnp.tril(jnp.ones((CHUNK, CHUNK), dtype=w_s.dtype))
    w = w_s * causal[None]
    mixed = jnp.einsum('gts,bnsgd->bntgd', w, v) + b_s.T[None, None, :, :, None]
    out = u * mixed.reshape(B, S, D_A)
    return out @ w_out


def stick_breaking_attention(q, k, v):
    B, S, H, dh = q.shape
    nb = S // BLOCK_Q
    scale = 1.0 / np.sqrt(dh).astype(np.float32)
    qb = q.reshape(B, nb, BLOCK_Q, H, dh).transpose(1, 0, 2, 3, 4)
    kf = k.astype(jnp.float32)
    vf = v.astype(jnp.float32)
    kpos = jnp.arange(S)

    def one_block(args):
        qi, i = args
        z = jnp.einsum('bqhd,bkhd->bhqk', qi.astype(jnp.float32), kf) * scale
        qpos = i * BLOCK_Q + jnp.arange(BLOCK_Q)
        mask = kpos[None, :] < qpos[:, None]
        log_beta = jax.nn.log_sigmoid(z)
        log_1m = jnp.where(mask, jax.nn.log_sigmoid(-z), 0.0)
        suffix = jnp.flip(jnp.cumsum(jnp.flip(log_1m, -1), axis=-1), -1) - log_1m
        a = jnp.where(mask, jnp.exp(log_beta + suffix), 0.0)
        return jnp.einsum('bhqk,bkhd->bqhd', a, vf)

    out = lax.map(one_block, (qb, jnp.arange(nb)))
    return out.transpose(1, 0, 2, 3, 4).reshape(B, S, H * dh).astype(q.dtype)


def conv_ffn(h, w_up, conv_w, conv_b, w_down):
    S = h.shape[1]
    a = h @ w_up
    ap = jnp.pad(a, ((0, 0), (CONV_W - 1, 0), (0, 0)))
    c = conv_b + sum(ap[:, tap:tap + S] * conv_w[tap] for tap in range(CONV_W))
    gate, val = jnp.split(c, 2, axis=-1)
    return (jax.nn.silu(gate) * val) @ w_down


def _fwd_setup_inputs(seed: int = 0) -> dict:
    key = jax.random.key(seed)
    ks = jax.random.split(key, 24)
    f32 = jnp.float32

    def nrm(k, shape, scale):
        return jax.random.normal(k, shape, f32) * scale

    def gain(k, shape):
        return 1.0 + 0.02 * jax.random.normal(k, shape, f32)

    return {
        "x": jax.random.normal(ks[0], (BATCH, SEQ, D_MODEL), f32),
        "pre_mix_g": gain(ks[1], (DEPTH, D_MODEL)),
        "post_mix_g": gain(ks[2], (DEPTH, D_MODEL)),
        "pre_ffn_g": gain(ks[3], (DEPTH, D_MODEL)),
        "post_ffn_g": gain(ks[4], (DEPTH, D_MODEL)),
        "a_w_in": nrm(ks[5], (N_A, D_MODEL, 2 * D_A), D_MODEL ** -0.5),
        "a_v_norm_g": gain(ks[6], (N_A, D_A)),
        "a_w_spatial": nrm(ks[7], (N_A, G_A, CHUNK, CHUNK), CHUNK ** -0.5),
        "a_b_spatial": gain(ks[8], (N_A, G_A, CHUNK)),
        "a_w_out": nrm(ks[9], (N_A, D_A, D_MODEL), D_A ** -0.5),
        "kv_norm_g": gain(ks[10], (D_MODEL,)),
        "w_k": nrm(ks[11], (D_MODEL, D_B), D_MODEL ** -0.5),
        "w_v": nrm(ks[12], (D_MODEL, D_B), D_MODEL ** -0.5),
        "b_w_q": nrm(ks[13], (N_B, D_MODEL, D_B), D_MODEL ** -0.5),
        "b_w_o": nrm(ks[14], (N_B, D_B, D_MODEL), D_B ** -0.5),
        "ffn_w_up": nrm(ks[15], (DEPTH, D_MODEL, 2 * D_FF), D_MODEL ** -0.5),
        "ffn_conv_w": nrm(ks[16], (DEPTH, CONV_W, 2 * D_FF), CONV_W ** -0.5),
        "ffn_conv_b": nrm(ks[17], (DEPTH, 2 * D_FF), 0.02),
        "ffn_w_down": nrm(ks[18], (DEPTH, D_FF, D_MODEL), D_FF ** -0.5),
    }


def _fwd_reference(x, pre_mix_g, post_mix_g, pre_ffn_g, post_ffn_g,
              a_w_in, a_v_norm_g, a_w_spatial, a_b_spatial, a_w_out,
              kv_norm_g, w_k, w_v, b_w_q, b_w_o,
              ffn_w_up, ffn_conv_w, ffn_conv_b, ffn_w_down):
    B, S, _ = x.shape
    h = x
    k_shared = None
    v_shared = None
    for layer in range(DEPTH):
        hn = rms_norm(h, pre_mix_g[layer])
        if layer < N_A:
            mix = chunked_gmlp(hn, a_w_in[layer], a_v_norm_g[layer],
                               a_w_spatial[layer], a_b_spatial[layer], a_w_out[layer])
        else:
            j = layer - N_A
            if j == 0:
                kvn = rms_norm(h, kv_norm_g)
                k_shared = (kvn @ w_k).reshape(B, S, N_HEADS, HEAD_DIM)
                v_shared = (kvn @ w_v).reshape(B, S, N_HEADS, HEAD_DIM)
            q = (hn @ b_w_q[j]).reshape(B, S, N_HEADS, HEAD_DIM)
            mix = stick_breaking_attention(q, k_shared, v_shared) @ b_w_o[j]
        h = h + rms_norm(mix, post_mix_g[layer])
        f = conv_ffn(rms_norm(h, pre_ffn_g[layer]), ffn_w_up[layer],
                     ffn_conv_w[layer], ffn_conv_b[layer], ffn_w_down[layer])
        h = h + rms_norm(f, post_ffn_g[layer])
    return h


import jax as _jax
import jax.numpy as _jnp

TWIN_FORMAT = 'train_step'
FWD_PARAMS = ['x', 'pre_mix_g', 'post_mix_g', 'pre_ffn_g', 'post_ffn_g', 'a_w_in', 'a_v_norm_g', 'a_w_spatial', 'a_b_spatial', 'a_w_out', 'kv_norm_g', 'w_k', 'w_v', 'b_w_q', 'b_w_o', 'ffn_w_up', 'ffn_conv_w', 'ffn_conv_b', 'ffn_w_down']
TWIN_WEIGHTS = ['pre_mix_g', 'post_mix_g', 'pre_ffn_g', 'post_ffn_g', 'a_w_in', 'a_v_norm_g', 'a_w_spatial', 'a_b_spatial', 'a_w_out', 'kv_norm_g', 'w_k', 'w_v', 'b_w_q', 'b_w_o', 'ffn_w_up', 'ffn_conv_w', 'ffn_conv_b', 'ffn_w_down']
TWIN_DIFF_INPUT = 'x'
TWIN_INPUTS = ['x', 'pre_mix_g', 'post_mix_g', 'pre_ffn_g', 'post_ffn_g', 'a_w_in', 'a_v_norm_g', 'a_w_spatial', 'a_b_spatial', 'a_w_out', 'kv_norm_g', 'w_k', 'w_v', 'b_w_q', 'b_w_o', 'ffn_w_up', 'ffn_conv_w', 'ffn_conv_b', 'ffn_w_down', 'loss_target', 'm_pre_mix_g', 'm_post_mix_g', 'm_pre_ffn_g', 'm_post_ffn_g', 'm_a_w_in', 'm_a_v_norm_g', 'm_a_w_spatial', 'm_a_b_spatial', 'm_a_w_out', 'm_kv_norm_g', 'm_w_k', 'm_w_v', 'm_b_w_q', 'm_b_w_o', 'm_ffn_w_up', 'm_ffn_conv_w', 'm_ffn_conv_b', 'm_ffn_w_down', 'v_pre_mix_g', 'v_post_mix_g', 'v_pre_ffn_g', 'v_post_ffn_g', 'v_a_w_in', 'v_a_v_norm_g', 'v_a_w_spatial', 'v_a_b_spatial', 'v_a_w_out', 'v_kv_norm_g', 'v_w_k', 'v_w_v', 'v_b_w_q', 'v_b_w_o', 'v_ffn_w_up', 'v_ffn_conv_w', 'v_ffn_conv_b', 'v_ffn_w_down']
TWIN_OUTPUTS = ['loss', 'grad_x', 'grad_pre_mix_g', 'grad_post_mix_g', 'grad_pre_ffn_g', 'grad_post_ffn_g', 'grad_a_w_in', 'grad_a_v_norm_g', 'grad_a_w_spatial', 'grad_a_b_spatial', 'grad_a_w_out', 'grad_kv_norm_g', 'grad_w_k', 'grad_w_v', 'grad_b_w_q', 'grad_b_w_o', 'grad_ffn_w_up', 'grad_ffn_conv_w', 'grad_ffn_conv_b', 'grad_ffn_w_down', 'delta_pre_mix_g', 'delta_post_mix_g', 'delta_pre_ffn_g', 'delta_post_ffn_g', 'delta_a_w_in', 'delta_a_v_norm_g', 'delta_a_w_spatial', 'delta_a_b_spatial', 'delta_a_w_out', 'delta_kv_norm_g', 'delta_w_k', 'delta_w_v', 'delta_b_w_q', 'delta_b_w_o', 'delta_ffn_w_up', 'delta_ffn_conv_w', 'delta_ffn_conv_b', 'delta_ffn_w_down', 'new_m_pre_mix_g', 'new_m_post_mix_g', 'new_m_pre_ffn_g', 'new_m_post_ffn_g', 'new_m_a_w_in', 'new_m_a_v_norm_g', 'new_m_a_w_spatial', 'new_m_a_b_spatial', 'new_m_a_w_out', 'new_m_kv_norm_g', 'new_m_w_k', 'new_m_w_v', 'new_m_b_w_q', 'new_m_b_w_o', 'new_m_ffn_w_up', 'new_m_ffn_conv_w', 'new_m_ffn_conv_b', 'new_m_ffn_w_down', 'new_v_pre_mix_g', 'new_v_post_mix_g', 'new_v_pre_ffn_g', 'new_v_post_ffn_g', 'new_v_a_w_in', 'new_v_a_v_norm_g', 'new_v_a_w_spatial', 'new_v_a_b_spatial', 'new_v_a_w_out', 'new_v_kv_norm_g', 'new_v_w_k', 'new_v_w_v', 'new_v_b_w_q', 'new_v_b_w_o', 'new_v_ffn_w_up', 'new_v_ffn_conv_w', 'new_v_ffn_conv_b', 'new_v_ffn_w_down']
TWIN_LEAF_KINDS = {'loss': 'loss', 'grad_x': 'grad_x', 'grad_pre_mix_g': 'grad_w', 'grad_post_mix_g': 'grad_w', 'grad_pre_ffn_g': 'grad_w', 'grad_post_ffn_g': 'grad_w', 'grad_a_w_in': 'grad_w', 'grad_a_v_norm_g': 'grad_w', 'grad_a_w_spatial': 'grad_w', 'grad_a_b_spatial': 'grad_w', 'grad_a_w_out': 'grad_w', 'grad_kv_norm_g': 'grad_w', 'grad_w_k': 'grad_w', 'grad_w_v': 'grad_w', 'grad_b_w_q': 'grad_w', 'grad_b_w_o': 'grad_w', 'grad_ffn_w_up': 'grad_w', 'grad_ffn_conv_w': 'grad_w', 'grad_ffn_conv_b': 'grad_w', 'grad_ffn_w_down': 'grad_w', 'delta_pre_mix_g': 'delta_w', 'delta_post_mix_g': 'delta_w', 'delta_pre_ffn_g': 'delta_w', 'delta_post_ffn_g': 'delta_w', 'delta_a_w_in': 'delta_w', 'delta_a_v_norm_g': 'delta_w', 'delta_a_w_spatial': 'delta_w', 'delta_a_b_spatial': 'delta_w', 'delta_a_w_out': 'delta_w', 'delta_kv_norm_g': 'delta_w', 'delta_w_k': 'delta_w', 'delta_w_v': 'delta_w', 'delta_b_w_q': 'delta_w', 'delta_b_w_o': 'delta_w', 'delta_ffn_w_up': 'delta_w', 'delta_ffn_conv_w': 'delta_w', 'delta_ffn_conv_b': 'delta_w', 'delta_ffn_w_down': 'delta_w', 'new_m_pre_mix_g': 'new_m', 'new_m_post_mix_g': 'new_m', 'new_m_pre_ffn_g': 'new_m', 'new_m_post_ffn_g': 'new_m', 'new_m_a_w_in': 'new_m', 'new_m_a_v_norm_g': 'new_m', 'new_m_a_w_spatial': 'new_m', 'new_m_a_b_spatial': 'new_m', 'new_m_a_w_out': 'new_m', 'new_m_kv_norm_g': 'new_m', 'new_m_w_k': 'new_m', 'new_m_w_v': 'new_m', 'new_m_b_w_q': 'new_m', 'new_m_b_w_o': 'new_m', 'new_m_ffn_w_up': 'new_m', 'new_m_ffn_conv_w': 'new_m', 'new_m_ffn_conv_b': 'new_m', 'new_m_ffn_w_down': 'new_m', 'new_v_pre_mix_g': 'new_v', 'new_v_post_mix_g': 'new_v', 'new_v_pre_ffn_g': 'new_v', 'new_v_post_ffn_g': 'new_v', 'new_v_a_w_in': 'new_v', 'new_v_a_v_norm_g': 'new_v', 'new_v_a_w_spatial': 'new_v', 'new_v_a_b_spatial': 'new_v', 'new_v_a_w_out': 'new_v', 'new_v_kv_norm_g': 'new_v', 'new_v_w_k': 'new_v', 'new_v_w_v': 'new_v', 'new_v_b_w_q': 'new_v', 'new_v_b_w_o': 'new_v', 'new_v_ffn_w_up': 'new_v', 'new_v_ffn_conv_w': 'new_v', 'new_v_ffn_conv_b': 'new_v', 'new_v_ffn_w_down': 'new_v'}


def _forward(args):
    return _fwd_reference(*[args[k] for k in FWD_PARAMS])


def _output_shape():
    def fwd():
        inp = _fwd_setup_inputs(0)
        return _fwd_reference(*[inp[k] for k in FWD_PARAMS])
    out = _jax.eval_shape(fwd)
    return out.shape, out.dtype

N_MICROBATCH = 1
ADAM_LR = 0.001
ADAM_B1 = 0.9
ADAM_B2 = 0.999
ADAM_EPS = 1e-08
ADAM_WD = 0.01
ADAM_STEP = 10
PER_EXAMPLE_BATCH_AXIS = {'x': 0, 'loss_target': 0}
SHARED_INPUTS = []
_WEIGHT_DTYPES = {'pre_mix_g': _jnp.float32, 'post_mix_g': _jnp.float32, 'pre_ffn_g': _jnp.float32, 'post_ffn_g': _jnp.float32, 'a_w_in': _jnp.float32, 'a_v_norm_g': _jnp.float32, 'a_w_spatial': _jnp.float32, 'a_b_spatial': _jnp.float32, 'a_w_out': _jnp.float32, 'kv_norm_g': _jnp.float32, 'w_k': _jnp.float32, 'w_v': _jnp.float32, 'b_w_q': _jnp.float32, 'b_w_o': _jnp.float32, 'ffn_w_up': _jnp.float32, 'ffn_conv_w': _jnp.float32, 'ffn_conv_b': _jnp.float32, 'ffn_w_down': _jnp.float32}
MOMENT_SCALE = {'pre_mix_g': 3.849094e-01, 'post_mix_g': 1.655153e+01, 'pre_ffn_g': 1.045882e+00, 'post_ffn_g': 1.603761e+01, 'a_w_in': 3.791638e-01, 'a_v_norm_g': 2.457648e-01, 'a_w_spatial': 2.398877e-01, 'a_b_spatial': 3.865832e-01, 'a_w_out': 3.146457e+00, 'kv_norm_g': 2.039469e+00, 'w_k': 1.083299e-01, 'w_v': 2.029028e+00, 'b_w_q': 1.080447e-01, 'b_w_o': 2.044484e+00, 'ffn_w_up': 4.378972e-01, 'ffn_conv_w': 5.383217e-01, 'ffn_conv_b': 1.869437e+00, 'ffn_w_down': 9.268776e-01}


def _to_microbatches(a, axis):
    t = _jnp.moveaxis(a, axis, 0)
    t = t.reshape((N_MICROBATCH, t.shape[0] // N_MICROBATCH) + t.shape[1:])
    return _jnp.moveaxis(t, 1, axis + 1)


def setup_inputs(seed: int = 0) -> dict:
    inp = _fwd_setup_inputs(seed)
    key = _jax.random.fold_in(_jax.random.key(seed), 7919)
    shape, _ = _output_shape()
    out = dict(inp)
    out["loss_target"] = _jax.random.normal(_jax.random.fold_in(key, 0), shape, _jnp.float32)
    for i, name in enumerate(TWIN_WEIGHTS):
        w = inp[name].astype(_jnp.float32)
        if MOMENT_SCALE is None:
            s = _jnp.sqrt(_jnp.mean(_jnp.square(w)) + 1e-30)
        else:
            s = MOMENT_SCALE[name]
        km, kv = _jax.random.split(_jax.random.fold_in(key, i + 1))
        out[name] = w
        out["m_" + name] = s * _jax.random.normal(km, w.shape, _jnp.float32)
        out["v_" + name] = (s * s) * _jax.random.uniform(kv, w.shape, _jnp.float32, 0.5, 1.5)
    if N_MICROBATCH > 1:
        for name, axis in PER_EXAMPLE_BATCH_AXIS.items():
            out[name] = _to_microbatches(out[name], axis)
    return {'x': out['x'], 'pre_mix_g': out['pre_mix_g'], 'post_mix_g': out['post_mix_g'], 'pre_ffn_g': out['pre_ffn_g'], 'post_ffn_g': out['post_ffn_g'], 'a_w_in': out['a_w_in'], 'a_v_norm_g': out['a_v_norm_g'], 'a_w_spatial': out['a_w_spatial'], 'a_b_spatial': out['a_b_spatial'], 'a_w_out': out['a_w_out'], 'kv_norm_g': out['kv_norm_g'], 'w_k': out['w_k'], 'w_v': out['w_v'], 'b_w_q': out['b_w_q'], 'b_w_o': out['b_w_o'], 'ffn_w_up': out['ffn_w_up'], 'ffn_conv_w': out['ffn_conv_w'], 'ffn_conv_b': out['ffn_conv_b'], 'ffn_w_down': out['ffn_w_down'], 'loss_target': out['loss_target'], 'm_pre_mix_g': out['m_pre_mix_g'], 'm_post_mix_g': out['m_post_mix_g'], 'm_pre_ffn_g': out['m_pre_ffn_g'], 'm_post_ffn_g': out['m_post_ffn_g'], 'm_a_w_in': out['m_a_w_in'], 'm_a_v_norm_g': out['m_a_v_norm_g'], 'm_a_w_spatial': out['m_a_w_spatial'], 'm_a_b_spatial': out['m_a_b_spatial'], 'm_a_w_out': out['m_a_w_out'], 'm_kv_norm_g': out['m_kv_norm_g'], 'm_w_k': out['m_w_k'], 'm_w_v': out['m_w_v'], 'm_b_w_q': out['m_b_w_q'], 'm_b_w_o': out['m_b_w_o'], 'm_ffn_w_up': out['m_ffn_w_up'], 'm_ffn_conv_w': out['m_ffn_conv_w'], 'm_ffn_conv_b': out['m_ffn_conv_b'], 'm_ffn_w_down': out['m_ffn_w_down'], 'v_pre_mix_g': out['v_pre_mix_g'], 'v_post_mix_g': out['v_post_mix_g'], 'v_pre_ffn_g': out['v_pre_ffn_g'], 'v_post_ffn_g': out['v_post_ffn_g'], 'v_a_w_in': out['v_a_w_in'], 'v_a_v_norm_g': out['v_a_v_norm_g'], 'v_a_w_spatial': out['v_a_w_spatial'], 'v_a_b_spatial': out['v_a_b_spatial'], 'v_a_w_out': out['v_a_w_out'], 'v_kv_norm_g': out['v_kv_norm_g'], 'v_w_k': out['v_w_k'], 'v_w_v': out['v_w_v'], 'v_b_w_q': out['v_b_w_q'], 'v_b_w_o': out['v_b_w_o'], 'v_ffn_w_up': out['v_ffn_w_up'], 'v_ffn_conv_w': out['v_ffn_conv_w'], 'v_ffn_conv_b': out['v_ffn_conv_b'], 'v_ffn_w_down': out['v_ffn_w_down']}


def _loss(weights, diff, rest, loss_target):
    with _jax.named_scope("forward"):
        args = {**rest, TWIN_DIFF_INPUT: diff, **{k: w.astype(_WEIGHT_DTYPES[k]) for k, w in weights.items()}}
        y = _forward(args)
    with _jax.named_scope("loss_head"):
        err = _jnp.square(y.astype(_jnp.float32) - loss_target)
        return 0.5 * _jnp.sum(_jnp.mean(err, axis=-1)) if err.ndim else 0.5 * err


def _adamw(w, g, m, v):
    m = ADAM_B1 * m + (1.0 - ADAM_B1) * g
    v = ADAM_B2 * v + (1.0 - ADAM_B2) * _jnp.square(g)
    m_hat = m / (1.0 - ADAM_B1 ** ADAM_STEP)
    v_hat = v / (1.0 - ADAM_B2 ** ADAM_STEP)
    delta = -ADAM_LR * (m_hat / (_jnp.sqrt(v_hat) + ADAM_EPS) + ADAM_WD * w)
    return delta, m, v


def reference(x, pre_mix_g, post_mix_g, pre_ffn_g, post_ffn_g, a_w_in, a_v_norm_g, a_w_spatial, a_b_spatial, a_w_out, kv_norm_g, w_k, w_v, b_w_q, b_w_o, ffn_w_up, ffn_conv_w, ffn_conv_b, ffn_w_down, loss_target, m_pre_mix_g, m_post_mix_g, m_pre_ffn_g, m_post_ffn_g, m_a_w_in, m_a_v_norm_g, m_a_w_spatial, m_a_b_spatial, m_a_w_out, m_kv_norm_g, m_w_k, m_w_v, m_b_w_q, m_b_w_o, m_ffn_w_up, m_ffn_conv_w, m_ffn_conv_b, m_ffn_w_down, v_pre_mix_g, v_post_mix_g, v_pre_ffn_g, v_post_ffn_g, v_a_w_in, v_a_v_norm_g, v_a_w_spatial, v_a_b_spatial, v_a_w_out, v_kv_norm_g, v_w_k, v_w_v, v_b_w_q, v_b_w_o, v_ffn_w_up, v_ffn_conv_w, v_ffn_conv_b, v_ffn_w_down):
    given = dict(x=x, pre_mix_g=pre_mix_g, post_mix_g=post_mix_g, pre_ffn_g=pre_ffn_g, post_ffn_g=post_ffn_g, a_w_in=a_w_in, a_v_norm_g=a_v_norm_g, a_w_spatial=a_w_spatial, a_b_spatial=a_b_spatial, a_w_out=a_w_out, kv_norm_g=kv_norm_g, w_k=w_k, w_v=w_v, b_w_q=b_w_q, b_w_o=b_w_o, ffn_w_up=ffn_w_up, ffn_conv_w=ffn_conv_w, ffn_conv_b=ffn_conv_b, ffn_w_down=ffn_w_down, loss_target=loss_target, m_pre_mix_g=m_pre_mix_g, m_post_mix_g=m_post_mix_g, m_pre_ffn_g=m_pre_ffn_g, m_post_ffn_g=m_post_ffn_g, m_a_w_in=m_a_w_in, m_a_v_norm_g=m_a_v_norm_g, m_a_w_spatial=m_a_w_spatial, m_a_b_spatial=m_a_b_spatial, m_a_w_out=m_a_w_out, m_kv_norm_g=m_kv_norm_g, m_w_k=m_w_k, m_w_v=m_w_v, m_b_w_q=m_b_w_q, m_b_w_o=m_b_w_o, m_ffn_w_up=m_ffn_w_up, m_ffn_conv_w=m_ffn_conv_w, m_ffn_conv_b=m_ffn_conv_b, m_ffn_w_down=m_ffn_w_down, v_pre_mix_g=v_pre_mix_g, v_post_mix_g=v_post_mix_g, v_pre_ffn_g=v_pre_ffn_g, v_post_ffn_g=v_post_ffn_g, v_a_w_in=v_a_w_in, v_a_v_norm_g=v_a_v_norm_g, v_a_w_spatial=v_a_w_spatial, v_a_b_spatial=v_a_b_spatial, v_a_w_out=v_a_w_out, v_kv_norm_g=v_kv_norm_g, v_w_k=v_w_k, v_w_v=v_w_v, v_b_w_q=v_b_w_q, v_b_w_o=v_b_w_o, v_ffn_w_up=v_ffn_w_up, v_ffn_conv_w=v_ffn_conv_w, v_ffn_conv_b=v_ffn_conv_b, v_ffn_w_down=v_ffn_w_down)
    weights = {n: given[n] for n in TWIN_WEIGHTS}
    shared = {n: given[n] for n in SHARED_INPUTS}
    per_example = {n: given[n] for n in ['x']}
    grad_fn = _jax.value_and_grad(_loss, argnums=(0, 1))

    def one_microbatch(ex, loss_target):
        ex = dict(ex)
        diff = ex.pop(TWIN_DIFF_INPUT)
        return grad_fn(weights, diff, {**shared, **ex}, loss_target)

    if N_MICROBATCH == 1:
        loss, (grad_w, grad_x) = one_microbatch(per_example, given["loss_target"])
    else:
        def body(carry, xs):
            loss_sum, grad_sum = carry
            l_k, (gw_k, gx_k) = one_microbatch(xs[0], xs[1])
            with _jax.named_scope("update"):
                return (loss_sum + l_k, _jax.tree.map(_jnp.add, grad_sum, gw_k)), gx_k

        init = (_jnp.zeros((), _jnp.float32), _jax.tree.map(_jnp.zeros_like, weights))
        (loss, grad_w), grad_x = _jax.lax.scan(body, init, (per_example, given["loss_target"]))
    with _jax.named_scope("update"):
        delta_w, new_m, new_v = {}, {}, {}
        for n in TWIN_WEIGHTS:
            delta_w[n], new_m[n], new_v[n] = _adamw(weights[n], grad_w[n], given["m_" + n], given["v_" + n])
    return (loss, grad_x, *[grad_w[n] for n in TWIN_WEIGHTS], *[delta_w[n] for n in TWIN_WEIGHTS],
            *[new_m[n] for n in TWIN_WEIGHTS], *[new_v[n] for n in TWIN_WEIGHTS])
```

```python
import functools
import math

import jax
import jax.numpy as jnp
from jax import lax
from jax.experimental import pallas as pl
from jax.experimental.pallas import tpu as pltpu

F32 = jnp.float32
BF16 = jnp.bfloat16
EPS = 1e-6
ADAM_LR = 0.001
ADAM_B1 = 0.9
ADAM_B2 = 0.999
ADAM_EPS = 1e-08
ADAM_WD = 0.01
ADAM_STEP = 10

LANE = 128
SUBLANE = 8
ROWS = 16
TILE = 128
N_CHIPS = 4
N_DEV = 8
VMEM_LIMIT = 56 * 1024 * 1024
MESH = pl.DeviceIdType.MESH
ANY = pl.BlockSpec(memory_space=pl.ANY)
VMEM_SPEC = pl.BlockSpec(memory_space=pltpu.VMEM)


def _cp(*sem):
    return pltpu.CompilerParams(dimension_semantics=sem, vmem_limit_bytes=VMEM_LIMIT)


def _pick(dim, pref, align=LANE):
    if dim <= pref:
        return dim
    best = None
    for d in range(align, pref + 1, align):
        if dim % d == 0:
            best = d
    assert best is not None, (dim, pref)
    return best


_DIMS = {
    "nn": (((1,), (0,)), ((), ())),
    "nt": (((1,), (1,)), ((), ())),
    "tn": (((0,), (0,)), ((), ())),
}


def _as3(a):
    return a if a.ndim == 3 else a[None]


def _spec3(br, bc, cols_j, rc):
    per = cols_j // bc

    def imap(m, n, k):
        r, c = rc(m, n, k)
        return (c // per, r, c % per)

    return pl.BlockSpec((None, br, bc), imap)


def _mm(a, b, mode, name, out_dtype=F32, out_split=1):
    a, b = _as3(a), _as3(b)
    ja, ra, caj = a.shape
    jb, rb, cbj = b.shape
    if mode == "nn":
        m, k, n = ra, ja * caj, jb * cbj
        assert rb == k
        m_ext, k_ext, n_ext = [ra], [caj, rb], [cbj]
    elif mode == "nt":
        m, k, n = ra, ja * caj, rb
        assert jb * cbj == k
        m_ext, k_ext, n_ext = [ra], [caj, cbj], [rb]
    else:
        m, k, n = ja * caj, ra, jb * cbj
        assert rb == k
        m_ext, k_ext, n_ext = [caj], [ra], [cbj]
    assert n % out_split == 0
    n_ext.append(n // out_split)
    bm = _pick(math.gcd(*m_ext), 1536)
    bn = _pick(math.gcd(*n_ext), 1536)
    bk = _pick(math.gcd(*k_ext), 1536)
    nk = k // bk
    if mode == "nn":
        a_spec = _spec3(bm, bk, caj, lambda mi, ni, ki: (mi, ki))
        b_spec = _spec3(bk, bn, cbj, lambda mi, ni, ki: (ki, ni))
    elif mode == "nt":
        a_spec = _spec3(bm, bk, caj, lambda mi, ni, ki: (mi, ki))
        b_spec = _spec3(bn, bk, cbj, lambda mi, ni, ki: (ni, ki))
    else:
        a_spec = _spec3(bk, bm, caj, lambda mi, ni, ki: (ki, mi))
        b_spec = _spec3(bk, bn, cbj, lambda mi, ni, ki: (ki, ni))
    o_spec = _spec3(bm, bn, n // out_split, lambda mi, ni, ki: (mi, ni))
    dims = _DIMS[mode]

    def body(a_ref, b_ref, o_ref, acc_ref):
        ki = pl.program_id(2)
        part = lax.dot_general(a_ref[...].astype(BF16), b_ref[...].astype(BF16), dims, preferred_element_type=F32)

        @pl.when(ki == 0)
        def _():
            acc_ref[...] = part

        @pl.when(ki > 0)
        def _():
            acc_ref[...] += part

        @pl.when(ki == nk - 1)
        def _():
            o_ref[...] = acc_ref[...].astype(o_ref.dtype)

    return pl.pallas_call(
        body,
        name=name,
        grid=(m // bm, n // bn, nk),
        in_specs=[a_spec, b_spec],
        out_specs=o_spec,
        out_shape=jax.ShapeDtypeStruct((out_split, m, n // out_split), out_dtype),
        scratch_shapes=[pltpu.VMEM((bm, bn), F32)],
        compiler_params=_cp("parallel", "parallel", "arbitrary"),
    )(a, b)


def _rms(x, g):
    r = lax.rsqrt(jnp.mean(x * x, axis=-1, keepdims=True) + EPS)
    return x * r * g


def _rms_bwd(x, g, dy):
    r = lax.rsqrt(jnp.mean(x * x, axis=-1, keepdims=True) + EPS)
    xh = x * r
    gy = dy * g
    dx = r * (gy - xh * jnp.mean(gy * xh, axis=-1, keepdims=True))
    return dx, jnp.sum(dy * xh, axis=0, keepdims=True)


def _row_block(s):
    return _pick(s, 256, ROWS)


def _rms_fwd(h, g, name):
    s, d = h.shape
    br = _row_block(s)

    def body(h_ref, g_ref, o_ref):
        o_ref[...] = _rms(h_ref[...], g_ref[...]).astype(BF16)

    row = pl.BlockSpec((br, d), lambda i: (i, 0))
    vec = pl.BlockSpec((1, d), lambda i: (0, 0))
    return pl.pallas_call(
        body, name=name, grid=(s // br,), in_specs=[row, vec], out_specs=row,
        out_shape=jax.ShapeDtypeStruct((s, d), BF16), compiler_params=_cp("parallel"),
    )(h, g)


def _resid_rms(h_in, f, g_post, g_next, name):
    s, d = h_in.shape
    br = _row_block(s)
    n_next = len(g_next)

    def body(h_ref, f_ref, gp_ref, *refs):
        gn_refs, ho_ref, hn_refs = refs[:n_next], refs[n_next], refs[n_next + 1:]
        h = h_ref[...] + _rms(f_ref[...], gp_ref[...])
        ho_ref[...] = h
        for gn_ref, hn_ref in zip(gn_refs, hn_refs):
            hn_ref[...] = _rms(h, gn_ref[...]).astype(BF16)

    row = pl.BlockSpec((br, d), lambda i: (i, 0))
    vec = pl.BlockSpec((1, d), lambda i: (0, 0))
    return pl.pallas_call(
        body, name=name, grid=(s // br,),
        in_specs=[row, row, vec] + [vec] * n_next,
        out_specs=[row] * (1 + n_next),
        out_shape=[jax.ShapeDtypeStruct((s, d), F32)] + [jax.ShapeDtypeStruct((s, d), BF16)] * n_next,
        compiler_params=_cp("parallel"),
    )(h_in, f, g_post, *g_next)


def _loss_head(h_in, f, g_post, target, name):
    s, d = h_in.shape
    br = _row_block(s)

    def body(h_ref, f_ref, gp_ref, t_ref, dh_ref, loss_ref):
        @pl.when(pl.program_id(0) == 0)
        def _():
            loss_ref[...] = jnp.zeros_like(loss_ref)

        diff = h_ref[...] + _rms(f_ref[...], gp_ref[...]) - t_ref[...]
        dh_ref[...] = diff * (1.0 / d)
        loss_ref[...] += 0.5 * jnp.sum(jnp.mean(diff * diff, axis=-1, keepdims=True))

    row = pl.BlockSpec((br, d), lambda i: (i, 0))
    vec = pl.BlockSpec((1, d), lambda i: (0, 0))
    return pl.pallas_call(
        body, name=name, grid=(s // br,),
        in_specs=[row, row, vec, row],
        out_specs=[row, pl.BlockSpec((SUBLANE, LANE), lambda i: (0, 0))],
        out_shape=[jax.ShapeDtypeStruct((s, d), F32), jax.ShapeDtypeStruct((SUBLANE, LANE), F32)],
        compiler_params=_cp("arbitrary"),
    )(h_in, f, g_post, target)


def _rms_bwd_out(dy, f, g, name):
    s, d = f.shape
    br = _row_block(s)

    def body(dy_ref, f_ref, g_ref, df_ref, dg_ref):
        @pl.when(pl.program_id(0) == 0)
        def _():
            dg_ref[...] = jnp.zeros_like(dg_ref)

        dx, dg = _rms_bwd(f_ref[...], g_ref[...], dy_ref[...])
        df_ref[...] = dx.astype(BF16)
        dg_ref[...] += dg

    row = pl.BlockSpec((br, d), lambda i: (i, 0))
    vec = pl.BlockSpec((1, d), lambda i: (0, 0))
    return pl.pallas_call(
        body, name=name, grid=(s // br,), in_specs=[row, row, vec], out_specs=[row, vec],
        out_shape=[jax.ShapeDtypeStruct((s, d), BF16), jax.ShapeDtypeStruct((1, d), F32)],
        compiler_params=_cp("arbitrary"),
    )(dy, f, g)


def _rms_bwd_in(dh_out, h_in, branches, name):
    s, d = h_in.shape
    br = _row_block(s)
    counts = [len(ds) for ds, _ in branches]
    n_d = sum(counts)
    n_b = len(branches)

    def body(dho_ref, h_ref, *refs):
        d_refs, g_refs = refs[:n_d], refs[n_d:n_d + n_b]
        dh_ref, dg_refs = refs[n_d + n_b], refs[n_d + n_b + 1:]

        @pl.when(pl.program_id(0) == 0)
        def _():
            for r in dg_refs:
                r[...] = jnp.zeros_like(r)

        h = h_ref[...]
        acc = dho_ref[...]
        at = 0
        for bi, cnt in enumerate(counts):
            dn = d_refs[at][...]
            for r in d_refs[at + 1:at + cnt]:
                dn = dn + r[...]
            at += cnt
            dx, dg = _rms_bwd(h, g_refs[bi][...], dn)
            acc = acc + dx
            dg_refs[bi][...] += dg
        dh_ref[...] = acc

    row = pl.BlockSpec((br, d), lambda i: (i, 0))
    vec = pl.BlockSpec((1, d), lambda i: (0, 0))
    flat_d = [x for ds, _ in branches for x in ds]
    outs = pl.pallas_call(
        body, name=name, grid=(s // br,),
        in_specs=[row, row] + [row] * n_d + [vec] * n_b,
        out_specs=[row] + [vec] * n_b,
        out_shape=[jax.ShapeDtypeStruct((s, d), F32)] + [jax.ShapeDtypeStruct((1, d), F32)] * n_b,
        compiler_params=_cp("arbitrary"),
    )(dh_out, h_in, *flat_d, *[g for _, g in branches])
    return outs[0], list(outs[1:])


def _split3(x):
    x0 = x.astype(BF16)
    r1 = x - x0.astype(F32)
    x1 = r1.astype(BF16)
    x2 = (r1 - x1.astype(F32)).astype(BF16)
    return x0, x1, x2


def _dot_sel(x, sel, dims):
    out = None
    for t in _split3(x):
        p = lax.dot_general(t, sel, dims, preferred_element_type=F32)
        out = p if out is None else out + p
    return out


def _tri(n, kind):
    r = lax.broadcasted_iota(jnp.int32, (n, n), 0)
    c = lax.broadcasted_iota(jnp.int32, (n, n), 1)
    m = {"lt": r < c, "le": r <= c, "gt": r > c}[kind]
    return jnp.where(m, 1.0, 0.0).astype(BF16)


_GELU_C = math.sqrt(2.0 / math.pi)
_GELU_A = 0.044715


def _gelu(x):
    return 0.5 * x * (1.0 + jnp.tanh(_GELU_C * (x + _GELU_A * (x * x * x))))


def _gelu_grad(x):
    t = jnp.tanh(_GELU_C * (x + _GELU_A * (x * x * x)))
    return 0.5 * (1.0 + t) + 0.5 * x * (1.0 - t * t) * (_GELU_C * (1.0 + 3.0 * _GELU_A * (x * x)))


def _causal_w(w):
    r = lax.broadcasted_iota(jnp.int32, (TILE, TILE), 0)
    c = lax.broadcasted_iota(jnp.int32, (TILE, TILE), 1)
    return jnp.where(c <= r, w, 0.0)


def _uv_tiles(uv_ref, g, d_a, dq):
    cu, cv = g * TILE, d_a + g * TILE
    u = uv_ref[cu // dq, :, pl.ds(cu % dq, TILE)]
    v = uv_ref[cv // dq, :, pl.ds(cv % dq, TILE)]
    return u, v


def _gmlp_fwd(uv, v_g, w_s, bias, name):
    _, s, dq = uv.shape
    d_a = 2 * dq
    n_g = d_a // TILE

    def body(uv_ref, vg_ref, ws_ref, b_ref, o_ref):
        for g in range(n_g):
            up, vp = _uv_tiles(uv_ref, g, d_a, dq)
            cols = pl.ds(g * TILE, TILE)
            vn = _rms(_gelu(vp), vg_ref[:, cols])
            mixed = jnp.dot(_causal_w(ws_ref[g]).astype(BF16), vn.astype(BF16), preferred_element_type=F32) + b_ref[:, cols]
            o_ref[:, cols] = (_gelu(up) * mixed).astype(BF16)

    return pl.pallas_call(
        body, name=name, grid=(s // TILE,),
        in_specs=[
            pl.BlockSpec((4, TILE, dq), lambda i: (0, i, 0)),
            pl.BlockSpec((1, d_a), lambda i: (0, 0)),
            pl.BlockSpec((n_g, TILE, TILE), lambda i: (0, 0, 0)),
            pl.BlockSpec((TILE, d_a), lambda i: (0, 0)),
        ],
        out_specs=pl.BlockSpec((TILE, d_a), lambda i: (i, 0)),
        out_shape=jax.ShapeDtypeStruct((s, d_a), BF16),
        compiler_params=_cp("parallel"),
    )(uv, v_g, w_s, bias)


def _gmlp_bwd(uv, dgm, v_g, w_s, bias, name):
    _, s, dq = uv.shape
    d_a = 2 * dq
    n_g = d_a // TILE
    n_c = s // TILE

    def body(uv_ref, d_ref, vg_ref, ws_ref, b_ref, duv_ref, dws_ref, dbs_ref, dvg_ref, dbias_acc):
        i = pl.program_id(0)

        @pl.when(i == 0)
        def _():
            dws_ref[...] = jnp.zeros_like(dws_ref)
            dvg_ref[...] = jnp.zeros_like(dvg_ref)
            dbias_acc[...] = jnp.zeros_like(dbias_acc)

        for g in range(n_g):
            up, vp = _uv_tiles(uv_ref, g, d_a, dq)
            cols = pl.ds(g * TILE, TILE)
            vg = vg_ref[:, cols]
            u = _gelu(up)
            v = _gelu(vp)
            r = lax.rsqrt(jnp.mean(v * v, axis=-1, keepdims=True) + EPS)
            vh = v * r
            vn = (vh * vg).astype(BF16)
            wc = _causal_w(ws_ref[g]).astype(BF16)
            mixed = jnp.dot(wc, vn, preferred_element_type=F32) + b_ref[:, cols]
            d_out = d_ref[:, cols]
            du = d_out * mixed
            dmixed = d_out * u
            dmb = dmixed.astype(BF16)
            dvn = lax.dot_general(wc, dmb, _DIMS["tn"], preferred_element_type=F32)
            dws_ref[g] += lax.dot_general(dmb, vn, _DIMS["nt"], preferred_element_type=F32)
            dbias_acc[:, cols] += dmixed
            dvg_ref[:, cols] += jnp.sum(dvn * vh, axis=0, keepdims=True)
            gv = dvn * vg
            dv = r * (gv - vh * jnp.mean(gv * vh, axis=-1, keepdims=True))
            cu, cv = g * TILE, d_a + g * TILE
            duv_ref[cu // dq, :, pl.ds(cu % dq, TILE)] = (du * _gelu_grad(up)).astype(BF16)
            duv_ref[cv // dq, :, pl.ds(cv % dq, TILE)] = (dv * _gelu_grad(vp)).astype(BF16)

        @pl.when(i == n_c - 1)
        def _():
            ones = jnp.ones((SUBLANE, TILE), BF16)
            for g in range(n_g):
                dws_ref[g] = _causal_w(dws_ref[g])
                cols = pl.ds(g * TILE, TILE)
                out = None
                for t in _split3(dbias_acc[:, cols]):
                    p = lax.dot_general(ones, t, _DIMS["nt"], preferred_element_type=F32)
                    out = p if out is None else out + p
                dbs_ref[pl.ds(g * SUBLANE, SUBLANE), :] = out

    return pl.pallas_call(
        body, name=name, grid=(n_c,),
        in_specs=[
            pl.BlockSpec((4, TILE, dq), lambda i: (0, i, 0)),
            pl.BlockSpec((TILE, d_a), lambda i: (i, 0)),
            pl.BlockSpec((1, d_a), lambda i: (0, 0)),
            pl.BlockSpec((n_g, TILE, TILE), lambda i: (0, 0, 0)),
            pl.BlockSpec((TILE, d_a), lambda i: (0, 0)),
        ],
        out_specs=[
            pl.BlockSpec((4, TILE, dq), lambda i: (0, i, 0)),
            pl.BlockSpec((n_g, TILE, TILE), lambda i: (0, 0, 0)),
            pl.BlockSpec((n_g * SUBLANE, TILE), lambda i: (0, 0)),
            pl.BlockSpec((1, d_a), lambda i: (0, 0)),
        ],
        out_shape=[
            jax.ShapeDtypeStruct((4, s, dq), BF16),
            jax.ShapeDtypeStruct((n_g, TILE, TILE), F32),
            jax.ShapeDtypeStruct((n_g * SUBLANE, TILE), F32),
            jax.ShapeDtypeStruct((1, d_a), F32),
        ],
        scratch_shapes=[pltpu.VMEM((TILE, d_a), F32)],
        compiler_params=_cp("arbitrary"),
    )(uv, dgm, v_g, w_s, bias)


def _sigmoid(x):
    return 1.0 / (1.0 + jnp.exp(-x))


def _conv3(ext, w, b):
    return b + ((w[0:1] * pltpu.roll(ext, 2, 0) + w[1:2] * pltpu.roll(ext, 1, 0)) + w[2:3] * ext)


def _act_blocks(s, ns):
    return _pick(s, 512, ROWS), _pick(ns, 256)


def _ffn_act_fwd(a, cw, cb, name):
    _, s, ns = a.shape
    bs, cb_w = _act_blocks(s, ns)
    hb = bs // SUBLANE

    def body(a_ref, prev_ref, cw_ref, cb_ref, o_ref):
        first = pl.program_id(0) == 0

        def conv(comp):
            prev = jnp.where(first, 0.0, prev_ref[comp])
            ext = jnp.concatenate([prev, a_ref[comp]], axis=0)
            return _conv3(ext, cw_ref[comp], cb_ref[comp])[SUBLANE:]

        for p in range(2):
            cg = conv(p)
            o_ref[p] = (cg * _sigmoid(cg) * conv(2 + p)).astype(BF16)

    return pl.pallas_call(
        body, name=name, grid=(s // bs, ns // cb_w),
        in_specs=[
            pl.BlockSpec((4, bs, cb_w), lambda i, j: (0, i, j)),
            pl.BlockSpec((4, SUBLANE, cb_w), lambda i, j: (0, jnp.maximum(i * hb - 1, 0), j)),
            pl.BlockSpec((4, 3, cb_w), lambda i, j: (0, 0, j)),
            pl.BlockSpec((4, 1, cb_w), lambda i, j: (0, 0, j)),
        ],
        out_specs=pl.BlockSpec((2, bs, cb_w), lambda i, j: (0, i, j)),
        out_shape=jax.ShapeDtypeStruct((2, s, ns), BF16),
        compiler_params=_cp("parallel", "parallel"),
    )(a, a, cw, cb)


def _ffn_act_bwd(a, dhm, cw, cb, name):
    _, s, ns = a.shape
    bs, cb_w = _act_blocks(s, ns)
    hb = bs // SUBLANE
    n_i = s // bs
    n_ext = bs + 2 * SUBLANE
    cur = slice(SUBLANE, SUBLANE + bs)

    def body(a_ref, prev_ref, next_ref, d_ref, dnext_ref, cw_ref, cb_ref, da_ref, dcw_ref, dcb_ref):
        i = pl.program_id(1)
        first, last = i == 0, i == n_i - 1

        @pl.when(first)
        def _():
            dcw_ref[...] = jnp.zeros_like(dcw_ref)
            dcb_ref[...] = jnp.zeros_like(dcb_ref)

        def ext_of(comp):
            return jnp.concatenate([jnp.where(first, 0.0, prev_ref[comp]), a_ref[comp], next_ref[comp]], axis=0)

        def back(comp, a_ext, dc):
            w = cw_ref[comp]
            da = (w[2:3] * dc + w[1:2] * pltpu.roll(dc, n_ext - 1, 0)) + w[0:1] * pltpu.roll(dc, n_ext - 2, 0)
            da_ref[comp] = da[cur].astype(BF16)
            dcc = dc[cur]
            dcw_ref[comp, 0:1, :] += jnp.sum(dcc * pltpu.roll(a_ext, 2, 0)[cur], axis=0, keepdims=True)
            dcw_ref[comp, 1:2, :] += jnp.sum(dcc * pltpu.roll(a_ext, 1, 0)[cur], axis=0, keepdims=True)
            dcw_ref[comp, 2:3, :] += jnp.sum(dcc * a_ext[cur], axis=0, keepdims=True)
            dcb_ref[comp] += jnp.sum(dcc, axis=0, keepdims=True)

        for p in range(2):
            ag, av = ext_of(p), ext_of(2 + p)
            cg = _conv3(ag, cw_ref[p], cb_ref[p])
            cv = _conv3(av, cw_ref[2 + p], cb_ref[2 + p])
            d = jnp.concatenate(
                [jnp.zeros((SUBLANE, cb_w), F32), d_ref[p], jnp.where(last, 0.0, dnext_ref[p])], axis=0)
            sg = _sigmoid(cg)
            back(2 + p, av, d * (cg * sg))
            back(p, ag, d * cv * (sg * (1.0 + cg * (1.0 - sg))))

    return pl.pallas_call(
        body, name=name, grid=(ns // cb_w, n_i),
        in_specs=[
            pl.BlockSpec((4, bs, cb_w), lambda j, i: (0, i, j)),
            pl.BlockSpec((4, SUBLANE, cb_w), lambda j, i: (0, jnp.maximum(i * hb - 1, 0), j)),
            pl.BlockSpec((4, SUBLANE, cb_w), lambda j, i: (0, jnp.minimum((i + 1) * hb, n_i * hb - 1), j)),
            pl.BlockSpec((2, bs, cb_w), lambda j, i: (0, i, j)),
            pl.BlockSpec((2, SUBLANE, cb_w), lambda j, i: (0, jnp.minimum((i + 1) * hb, n_i * hb - 1), j)),
            pl.BlockSpec((4, 3, cb_w), lambda j, i: (0, 0, j)),
            pl.BlockSpec((4, 1, cb_w), lambda j, i: (0, 0, j)),
        ],
        out_specs=[
            pl.BlockSpec((4, bs, cb_w), lambda j, i: (0, i, j)),
            pl.BlockSpec((4, 3, cb_w), lambda j, i: (0, 0, j)),
            pl.BlockSpec((4, 1, cb_w), lambda j, i: (0, 0, j)),
        ],
        out_shape=[
            jax.ShapeDtypeStruct((4, s, ns), BF16),
            jax.ShapeDtypeStruct((4, 3, ns), F32),
            jax.ShapeDtypeStruct((4, 1, ns), F32),
        ],
        compiler_params=_cp("parallel", "arbitrary"),
    )(a, a, a, dhm, dhm, cw, cb)


def _sb_logits(q, k_blk, scale, row0, col0):
    z = lax.dot_general(q, k_blk, _DIMS["nt"], preferred_element_type=F32) * scale
    rows = row0 + lax.broadcasted_iota(jnp.int32, (TILE, TILE), 0)
    cols = col0 + lax.broadcasted_iota(jnp.int32, (TILE, TILE), 1)
    mask = cols < rows
    lp = jnp.log(1.0 + jnp.exp(-jnp.abs(z)))
    lb = jnp.minimum(z, 0.0) - lp
    l1m = jnp.where(mask, lb - z, 0.0)
    return z, lb, l1m, mask


def _attn_fwd(q, k, v, name):
    s, hd = q.shape
    n_h, n_q = hd // TILE, s // TILE
    scale = 1.0 / math.sqrt(TILE)

    def body(q_ref, k_ref, v_ref, o_ref, l_ref):
        i = pl.program_id(1)
        qb = q_ref[...]
        later = _tri(TILE, "gt")

        def step(t, carry):
            acc, suf = carry
            j = i - t
            rows = pl.ds(pl.multiple_of(j * TILE, TILE), TILE)
            _, lb, l1m, mask = _sb_logits(qb, k_ref[rows, :], scale, i * TILE, j * TILE)
            inner = _dot_sel(l1m, later, _DIMS["nn"])
            a = jnp.where(mask, jnp.exp(lb + inner + suf), 0.0)
            acc = acc + jnp.dot(a.astype(BF16), v_ref[rows, :], preferred_element_type=F32)
            return acc, suf + jnp.sum(l1m, axis=1, keepdims=True)

        acc, suf = lax.fori_loop(0, i + 1, step, (jnp.zeros((TILE, TILE), F32), jnp.zeros((TILE, 1), F32)))
        o_ref[...] = acc.astype(BF16)
        l_ref[...] = jnp.broadcast_to(suf, (TILE, TILE))

    blk = pl.BlockSpec((TILE, TILE), lambda h, i: (i, h))
    head = pl.BlockSpec((s, TILE), lambda h, i: (0, h))
    return pl.pallas_call(
        body, name=name, grid=(n_h, n_q), in_specs=[blk, head, head], out_specs=[blk, blk],
        out_shape=[jax.ShapeDtypeStruct((s, hd), BF16), jax.ShapeDtypeStruct((s, hd), F32)],
        compiler_params=_cp("parallel", "parallel"),
    )(q, k, v)


def _attn_bwd(q, k, v, do, lsum, name):
    s, hd = q.shape
    n_h, n_q = hd // TILE, s // TILE
    scale = 1.0 / math.sqrt(TILE)

    def body(q_ref, k_ref, v_ref, do_ref, l_ref, dq_ref, dk_ref, dv_ref):
        i = pl.program_id(1)

        @pl.when(i == 0)
        def _():
            dk_ref[...] = jnp.zeros_like(dk_ref)
            dv_ref[...] = jnp.zeros_like(dv_ref)

        qb = q_ref[...]
        dob = do_ref[...]
        total = l_ref[:, 0:1]
        upto = _tri(TILE, "le")
        before = _tri(TILE, "lt")

        def step(j, carry):
            dq, pre, cp = carry
            rows = pl.ds(pl.multiple_of(j * TILE, TILE), TILE)
            kb, vb = k_ref[rows, :], v_ref[rows, :]
            z, lb, l1m, mask = _sb_logits(qb, kb, scale, i * TILE, j * TILE)
            suffix = (total - pre) - _dot_sel(l1m, upto, _DIMS["nn"])
            a = jnp.where(mask, jnp.exp(lb + suffix), 0.0)
            p = a * lax.dot_general(dob, vb, _DIMS["nt"], preferred_element_type=F32)
            cprev = cp + _dot_sel(p, before, _DIMS["nn"])
            sg = _sigmoid(z)
            dz = (jnp.where(mask, p * (1.0 - sg) - cprev * sg, 0.0) * scale).astype(BF16)
            dq = dq + jnp.dot(dz, kb, preferred_element_type=F32)
            dk_ref[rows, :] += lax.dot_general(dz, qb, _DIMS["tn"], preferred_element_type=F32)
            dv_ref[rows, :] += lax.dot_general(a.astype(BF16), dob, _DIMS["tn"], preferred_element_type=F32)
            return dq, pre + jnp.sum(l1m, axis=1, keepdims=True), cp + jnp.sum(p, axis=1, keepdims=True)

        zero = jnp.zeros((TILE, 1), F32)
        dq, _, _ = lax.fori_loop(0, i + 1, step, (jnp.zeros((TILE, TILE), F32), zero, zero))
        dq_ref[...] = dq.astype(BF16)

    blk = pl.BlockSpec((TILE, TILE), lambda h, i: (i, h))
    head = pl.BlockSpec((s, TILE), lambda h, i: (0, h))
    return pl.pallas_call(
        body, name=name, grid=(n_h, n_q), in_specs=[blk, head, head, blk, blk], out_specs=[blk, head, head],
        out_shape=[jax.ShapeDtypeStruct((s, hd), BF16), jax.ShapeDtypeStruct((s, hd), F32),
                   jax.ShapeDtypeStruct((s, hd), F32)],
        compiler_params=_cp("parallel", "arbitrary"),
    )(q, k, v, do, lsum)


def _cast_bf16(w, layer, name):
    _, r, c = w.shape
    br, bc = _pick(r, 256, ROWS), _pick(c, 1024)

    def body(w_ref, o_ref):
        o_ref[...] = w_ref[...].astype(BF16)

    return pl.pallas_call(
        body, name=name, grid=(r // br, c // bc),
        in_specs=[pl.BlockSpec((None, br, bc), lambda i, j: (layer, i, j))],
        out_specs=pl.BlockSpec((br, bc), lambda i, j: (i, j)),
        out_shape=jax.ShapeDtypeStruct((r, c), BF16), compiler_params=_cp("parallel", "parallel"),
    )(w)


def _pair_add(dw, recv, c_idx, name):
    _, r, c = dw.shape
    hr = r // 2
    br, bc = _pick(hr, 256, ROWS), _pick(c, 1024)
    nb = hr // br

    def body(c_ref, a_ref, b_ref, o_ref):
        o_ref[...] = (a_ref[...].astype(F32) + b_ref[...].astype(F32)).astype(BF16)

    return pl.pallas_call(
        body, name=name,
        grid_spec=pltpu.PrefetchScalarGridSpec(
            num_scalar_prefetch=1, grid=(N_CHIPS, nb, c // bc),
            in_specs=[
                pl.BlockSpec((None, br, bc), lambda s, i, j, c_ref: (s, c_ref[0] * nb + i, j)),
                pl.BlockSpec((None, br, bc), lambda s, i, j, c_ref: (s, i, j)),
            ],
            out_specs=pl.BlockSpec((None, br, bc), lambda s, i, j, c_ref: (s, i, j)),
        ),
        out_shape=jax.ShapeDtypeStruct((N_CHIPS, hr, c), BF16),
        compiler_params=_cp("parallel", "parallel", "parallel"),
    )(c_idx, dw, recv)


def _chip_sum(parts, name):
    _, r, c = parts.shape
    br, bc = _pick(r, 256, ROWS), _pick(c, 1024)

    def body(p_ref, o_ref):
        acc = p_ref[0].astype(F32)
        for s in range(1, N_CHIPS):
            acc = acc + p_ref[s].astype(F32)
        o_ref[...] = acc

    return pl.pallas_call(
        body, name=name, grid=(r // br, c // bc),
        in_specs=[pl.BlockSpec((N_CHIPS, br, bc), lambda i, j: (0, i, j))],
        out_specs=pl.BlockSpec((br, bc), lambda i, j: (i, j)),
        out_shape=jax.ShapeDtypeStruct((r, c), F32), compiler_params=_cp("parallel", "parallel"),
    )(parts)


def _adamw(w, g, m, v, name):
    n_l, r, c = w.shape
    br, bc = _pick(r, 256, ROWS), _pick(c, 1024)

    def body(w_ref, g_ref, m_ref, v_ref, d_ref, mo_ref, vo_ref):
        g = g_ref[...]
        m = ADAM_B1 * m_ref[...] + (1.0 - ADAM_B1) * g
        v = ADAM_B2 * v_ref[...] + (1.0 - ADAM_B2) * (g * g)
        m_hat = m / (1.0 - ADAM_B1 ** ADAM_STEP)
        v_hat = v / (1.0 - ADAM_B2 ** ADAM_STEP)
        d_ref[...] = -ADAM_LR * (m_hat / (jnp.sqrt(v_hat) + ADAM_EPS) + ADAM_WD * w_ref[...])
        mo_ref[...] = m
        vo_ref[...] = v

    blk = pl.BlockSpec((None, br, bc), lambda l, i, j: (l, i, j))
    return pl.pallas_call(
        body, name=name, grid=(n_l, r // br, c // bc), in_specs=[blk] * 4, out_specs=[blk] * 3,
        out_shape=[jax.ShapeDtypeStruct(w.shape, F32)] * 3, compiler_params=_cp("parallel", "parallel", "parallel"),
    )(w, g, m, v)


def _place():
    x, y, c = lax.axis_index("x"), lax.axis_index("y"), lax.axis_index("c")
    chips = [(1 - x, y), (x, 1 - y), (1 - x, 1 - y)]
    return x, y, c, chips


def _gather_shards(shards, name):
    n = len(shards)
    halves = [a.shape[0] // 2 for a in shards]
    for a, hr in zip(shards, halves):
        assert a.shape[0] == 2 * hr and hr % (2 * SUBLANE) == 0, a.shape

    def body(*refs):
        ins, outs = refs[:n], refs[n:2 * n]
        send_sems, recv_sems, local_sems = refs[2 * n:]
        x, y, c, chips = _place()
        me = 2 * x + y

        def half_of(i, core):
            return pl.ds(pl.multiple_of(core * halves[i], 2 * SUBLANE), halves[i])

        def remote(i, k, slot, core, to, src=None):
            dst = outs[i].at[slot, half_of(i, core)]
            return pltpu.make_async_remote_copy(
                src_ref=dst if src is None else src, dst_ref=dst,
                send_sem=send_sems.at[i * 6 + k], recv_sem=recv_sems.at[i * 6 + k],
                device_id=to, device_id_type=MESH)

        own = [pltpu.make_async_copy(ins[i], outs[i].at[me], local_sems.at[i]) for i in range(n)]
        for cp in own:
            cp.start()
        sends = []
        for i in range(n):
            for k, (px, py) in enumerate(chips):
                sends.append(remote(i, k, me, c, (px, py, c), src=ins[i].at[half_of(i, c)]))
        for cp in sends:
            cp.start()
        passed = []
        for i in range(n):
            for k, (px, py) in enumerate(chips):
                remote(i, k, 2 * px + py, c, (x, y, c)).wait_recv()
                fwd = remote(i, 3 + k, 2 * px + py, c, (x, y, 1 - c))
                fwd.start()
                passed.append(fwd)
        for i in range(n):
            for k, (px, py) in enumerate(chips):
                remote(i, 3 + k, 2 * px + py, 1 - c, (x, y, c)).wait_recv()
        for cp in sends + passed:
            cp.wait_send()
        for cp in own:
            cp.wait()

    return pl.pallas_call(
        body, name=name,
        in_specs=[ANY] * n, out_specs=[ANY] * n,
        out_shape=[jax.ShapeDtypeStruct((N_CHIPS,) + a.shape, a.dtype) for a in shards],
        scratch_shapes=[pltpu.SemaphoreType.DMA((6 * n,)), pltpu.SemaphoreType.DMA((6 * n,)),
                        pltpu.SemaphoreType.DMA((n,))],
    )(*shards)


def _swap_halves(grads, name):
    n = len(grads)
    halves = [a.shape[1] // 2 for a in grads]

    def body(*refs):
        ins, outs = refs[:n], refs[n:2 * n]
        send_sems, recv_sems = refs[2 * n:]
        x, y, c, _ = _place()
        copies = []
        for i in range(n):
            rows = pl.ds(pl.multiple_of((1 - c) * halves[i], 2 * SUBLANE), halves[i])
            copies.append(pltpu.make_async_remote_copy(
                src_ref=ins[i].at[:, rows, :], dst_ref=outs[i], send_sem=send_sems.at[i], recv_sem=recv_sems.at[i],
                device_id=(x, y, 1 - c), device_id_type=MESH))
        for cp in copies:
            cp.start()
        for cp in copies:
            cp.wait()

    return pl.pallas_call(
        body, name=name, in_specs=[ANY] * n, out_specs=[ANY] * n,
        out_shape=[jax.ShapeDtypeStruct((N_CHIPS, hr, a.shape[2]), a.dtype) for a, hr in zip(grads, halves)],
        scratch_shapes=[pltpu.SemaphoreType.DMA((n,)), pltpu.SemaphoreType.DMA((n,))],
    )(*grads)


def _scatter_to_owners(parts, name):
    n = len(parts)

    def body(*refs):
        ins, outs = refs[:n], refs[n:2 * n]
        send_sems, recv_sems, local_sems = refs[2 * n:]
        x, y, c, chips = _place()
        me = 2 * x + y
        own = [pltpu.make_async_copy(ins[i].at[me], outs[i].at[me], local_sems.at[i]) for i in range(n)]
        for cp in own:
            cp.start()
        sends = []
        for i in range(n):
            for k, (px, py) in enumerate(chips):
                sends.append(pltpu.make_async_remote_copy(
                    src_ref=ins[i].at[2 * px + py], dst_ref=outs[i].at[me],
                    send_sem=send_sems.at[3 * i + k], recv_sem=recv_sems.at[3 * i + k],
                    device_id=(px, py, c), device_id_type=MESH))
        for cp in sends:
            cp.start()
        for i in range(n):
            for k, (px, py) in enumerate(chips):
                slot = outs[i].at[2 * px + py]
                pltpu.make_async_remote_copy(
                    src_ref=slot, dst_ref=slot, send_sem=send_sems.at[3 * i + k], recv_sem=recv_sems.at[3 * i + k],
                    device_id=(x, y, c), device_id_type=MESH).wait_recv()
        for cp in sends:
            cp.wait_send()
        for cp in own:
            cp.wait()

    return pl.pallas_call(
        body, name=name, in_specs=[ANY] * n, out_specs=[ANY] * n,
        out_shape=[jax.ShapeDtypeStruct(a.shape, a.dtype) for a in parts],
        scratch_shapes=[pltpu.SemaphoreType.DMA((3 * n,)), pltpu.SemaphoreType.DMA((3 * n,)),
                        pltpu.SemaphoreType.DMA((n,))],
    )(*parts)


def _join_halves(units, shapes, name):
    flat = [(t, layer, tot) for t, lst in enumerate(units) for layer, tot in lst]
    n = len(flat)
    n_out = len(shapes)

    def body(*refs):
        ins, outs = refs[:n], refs[n:n + n_out]
        send_sems, recv_sems, local_sems = refs[n + n_out:]
        x, y, c, _ = _place()

        def dst(i, core):
            t, layer, tot = flat[i]
            hr = tot.shape[0]
            return outs[t].at[layer, pl.ds(pl.multiple_of(core * hr, SUBLANE), hr), :]

        own = [pltpu.make_async_copy(ins[i], dst(i, c), local_sems.at[i]) for i in range(n)]
        sends = [pltpu.make_async_remote_copy(
            src_ref=ins[i], dst_ref=dst(i, c), send_sem=send_sems.at[i], recv_sem=recv_sems.at[i],
            device_id=(x, y, 1 - c), device_id_type=MESH) for i in range(n)]
        for cp in own + sends:
            cp.start()
        for i in range(n):
            other = dst(i, 1 - c)
            pltpu.make_async_remote_copy(
                src_ref=other, dst_ref=other, send_sem=send_sems.at[i], recv_sem=recv_sems.at[i],
                device_id=(x, y, c), device_id_type=MESH).wait_recv()
        for cp in sends:
            cp.wait_send()
        for cp in own:
            cp.wait()

    return pl.pallas_call(
        body, name=name, in_specs=[ANY] * n, out_specs=[ANY] * n_out,
        out_shape=[jax.ShapeDtypeStruct(s, F32) for s in shapes],
        scratch_shapes=[pltpu.SemaphoreType.DMA((n,)), pltpu.SemaphoreType.DMA((n,)), pltpu.SemaphoreType.DMA((n,))],
    )(*[tot for _, _, tot in flat])


def _all_reduce_small(packed, name):
    r, c = packed.shape
    chunk = _pick(r, 256, ROWS)

    def body(x_ref, out_ref, gath, send_sems, recv_sems, local_sem):
        x, y, cc, chips = _place()
        me, sibling = (x, y, cc), (x, y, 1 - cc)

        def slot(px, py, pc):
            return gath.at[4 * px + 2 * py + pc]

        def copy(k, block, to, src=None):
            return pltpu.make_async_remote_copy(
                src_ref=slot(*block) if src is None else src, dst_ref=slot(*block),
                send_sem=send_sems.at[k], recv_sem=recv_sems.at[k], device_id=to, device_id_type=MESH)

        mine = pltpu.make_async_copy(x_ref, slot(*me), local_sem)
        mine.start()
        first = [copy(0, me, sibling, src=x_ref)]
        first += [copy(1 + j, me, (*chip, cc), src=x_ref) for j, chip in enumerate(chips)]
        for cp in first:
            cp.start()
        passed = [copy(4 + j, (*chip, cc), sibling) for j, chip in enumerate(chips)]
        for j, chip in enumerate(chips):
            copy(1 + j, (*chip, cc), me).wait_recv()
            passed[j].start()
        copy(0, sibling, me).wait_recv()
        for j, chip in enumerate(chips):
            copy(4 + j, (*chip, 1 - cc), me).wait_recv()
        for cp in first + passed:
            cp.wait_send()
        mine.wait()

        def add(i, carry):
            rows = pl.ds(pl.multiple_of(i * chunk, SUBLANE), chunk)
            acc = gath[0, rows, :]
            for dev in range(1, N_DEV):
                acc = acc + gath[dev, rows, :]
            out_ref[rows, :] = acc
            return carry

        lax.fori_loop(0, r // chunk, add, 0)

    return pl.pallas_call(
        body, name=name, in_specs=[VMEM_SPEC], out_specs=VMEM_SPEC,
        out_shape=jax.ShapeDtypeStruct((r, c), F32),
        scratch_shapes=[pltpu.VMEM((N_DEV, r, c), F32), pltpu.SemaphoreType.DMA((7,)),
                        pltpu.SemaphoreType.DMA((7,)), pltpu.SemaphoreType.DMA],
        compiler_params=pltpu.CompilerParams(vmem_limit_bytes=VMEM_LIMIT),
    )(packed)


_PACK_ROWS = 256


def _pack(arrays):
    flat = jnp.concatenate([a.reshape(-1).astype(F32) for a in arrays])
    unit = _PACK_ROWS * LANE
    total = -(-flat.shape[0] // unit) * unit
    return jnp.pad(flat, (0, total - flat.shape[0])).reshape(-1, LANE)


def _unpack(packed, shapes, lead=()):
    flat = packed.reshape(lead + (-1,))
    out, at = [], 0
    for s in shapes:
        size = math.prod(s)
        out.append(flat[..., at:at + size].reshape(lead + tuple(s)))
        at += size
    return out


def kernel(x, pre_mix_g, post_mix_g, pre_ffn_g, post_ffn_g, a_w_in, a_v_norm_g, a_w_spatial, a_b_spatial, a_w_out, kv_norm_g, w_k, w_v, b_w_q, b_w_o, ffn_w_up, ffn_conv_w, ffn_conv_b, ffn_w_down, loss_target, m_pre_mix_g, m_post_mix_g, m_pre_ffn_g, m_post_ffn_g, m_a_w_in, m_a_v_norm_g, m_a_w_spatial, m_a_b_spatial, m_a_w_out, m_kv_norm_g, m_w_k, m_w_v, m_b_w_q, m_b_w_o, m_ffn_w_up, m_ffn_conv_w, m_ffn_conv_b, m_ffn_w_down, v_pre_mix_g, v_post_mix_g, v_pre_ffn_g, v_post_ffn_g, v_a_w_in, v_a_v_norm_g, v_a_w_spatial, v_a_b_spatial, v_a_w_out, v_kv_norm_g, v_w_k, v_w_v, v_b_w_q, v_b_w_o, v_ffn_w_up, v_ffn_conv_w, v_ffn_conv_b, v_ffn_w_down):
    xi, yi, ci = lax.axis_index("x"), lax.axis_index("y"), lax.axis_index("c")
    chip = 2 * xi + yi
    c_idx = jnp.reshape(ci, (1,)).astype(jnp.int32)
    _, s, d = x.shape
    n_layers = pre_mix_g.shape[0]
    assert n_layers == 2 and a_w_in.shape[0] == 1 and b_w_q.shape[0] == 1
    d_a = a_w_out.shape[1] * N_CHIPS
    n_g = a_w_spatial.shape[1]
    ns = ffn_w_up.shape[2]
    assert a_w_spatial.shape[2] == TILE and d_a == n_g * TILE and s % TILE == 0
    h0 = x[0]
    target = loss_target[0]

    big = {
        "win": (a_w_in, m_a_w_in, v_a_w_in),
        "wout": (a_w_out, m_a_w_out, v_a_w_out),
        "wk": (w_k[None], m_w_k[None], v_w_k[None]),
        "wv": (w_v[None], m_w_v[None], v_w_v[None]),
        "wq": (b_w_q, m_b_w_q, v_b_w_q),
        "wo": (b_w_o, m_b_w_o, v_b_w_o),
        "wup": (ffn_w_up, m_ffn_w_up, v_ffn_w_up),
        "wdn": (ffn_w_down, m_ffn_w_down, v_ffn_w_down),
    }
    units = [(nm, layer) for nm in big for layer in range(big[nm][0].shape[0])]
    shards = [_cast_bf16(big[nm][0], layer, f"cast_{nm}{layer}") for nm, layer in units]
    small_sharded = _pack([a_v_norm_g, ffn_conv_w])
    gathered = _gather_shards(shards + [small_sharded], "gather_weights")
    full = dict(zip(units, gathered[:-1]))
    vg_parts, cw_parts = _unpack(gathered[-1], [a_v_norm_g.shape, ffn_conv_w.shape], lead=(N_CHIPS,))
    v_g = jnp.transpose(vg_parts, (1, 0, 2)).reshape(1, d_a)

    def rows(nm, layer=0):
        w = full[(nm, layer)]
        return w.reshape(w.shape[0] * w.shape[1], w.shape[2])

    gains = lambda g, layer: g[layer:layer + 1]
    bias = jnp.repeat(a_b_spatial[0].T, TILE, axis=1)
    w_s = a_w_spatial[0]
    kv_g = kv_norm_g[None]
    conv_w = [cw_parts[:, layer] for layer in range(n_layers)]
    conv_b = [ffn_conv_b[layer].reshape(N_CHIPS, 1, ns) for layer in range(n_layers)]

    def ffn_fwd(hn, layer):
        a = _mm(hn, full[("wup", layer)], "nn", f"ffn_up{layer}", out_split=N_CHIPS)
        hm = _ffn_act_fwd(a, conv_w[layer], conv_b[layer], f"ffn_act{layer}")
        return a, hm, _mm(hm, rows("wdn", layer), "nn", f"ffn_down{layer}")[0]

    hn0 = _rms_fwd(h0, gains(pre_mix_g, 0), "norm_in")
    uv = _mm(hn0, full[("win", 0)], "nn", "gmlp_in", out_split=N_CHIPS)
    gm = _gmlp_fwd(uv, v_g, w_s, bias, "gmlp_gate")
    mix0 = _mm(gm, rows("wout"), "nn", "gmlp_out")[0]
    h1, hn1 = _resid_rms(h0, mix0, gains(post_mix_g, 0), [gains(pre_ffn_g, 0)], "resid_mix0")
    a0, hm0, f0 = ffn_fwd(hn1, 0)
    h2, hn2, kvn = _resid_rms(h1, f0, gains(post_ffn_g, 0), [gains(pre_mix_g, 1), kv_g], "resid_ffn0")
    q = _mm(hn2, rows("wq"), "nn", "proj_q", out_dtype=BF16)[0]
    k = _mm(kvn, rows("wk"), "nn", "proj_k", out_dtype=BF16)[0]
    v = _mm(kvn, rows("wv"), "nn", "proj_v", out_dtype=BF16)[0]
    att, lsum = _attn_fwd(q, k, v, "attn_fwd")
    mix1 = _mm(att, rows("wo"), "nn", "proj_o")[0]
    h3, hn3 = _resid_rms(h2, mix1, gains(post_mix_g, 1), [gains(pre_ffn_g, 1)], "resid_mix1")
    a1, hm1, f1 = ffn_fwd(hn3, 1)
    dh4, loss_tile = _loss_head(h3, f1, gains(post_ffn_g, 1), target, "loss_head")
    loss = lax.psum(loss_tile[0, 0], ("x", "y", "c"))

    dw = {}
    dg = {}

    def ffn_bwd(dh_out, h_in, hn, a, hm, f, layer):
        df, dg[("post_ffn", layer)] = _rms_bwd_out(dh_out, f, gains(post_ffn_g, layer), f"d_norm_ffn_out{layer}")
        dwd = _mm(hm, df, "tn", f"d_w_down{layer}", out_dtype=BF16)[0]
        dw[("wdn", layer)] = dwd.reshape(N_CHIPS, dwd.shape[0] // N_CHIPS, d)
        dhm = _mm(df, rows("wdn", layer), "nt", f"d_ffn_mid{layer}", out_split=2)
        da, dg[("conv_w", layer)], dg[("conv_b", layer)] = _ffn_act_bwd(
            a, dhm, conv_w[layer], conv_b[layer], f"d_ffn_act{layer}")
        dw[("wup", layer)] = _mm(hn, da, "tn", f"d_w_up{layer}", out_dtype=BF16, out_split=N_CHIPS)
        dhn = _mm(da, full[("wup", layer)], "nt", f"d_ffn_in{layer}")[0]
        return dhn

    dhn3 = ffn_bwd(dh4, h3, hn3, a1, hm1, f1, 1)
    dh3, (dg[("pre_ffn", 1)],) = _rms_bwd_in(dh4, h3, [([dhn3], gains(pre_ffn_g, 1))], "d_norm_ffn_in1")
    dmix1, dg[("post_mix", 1)] = _rms_bwd_out(dh3, mix1, gains(post_mix_g, 1), "d_norm_mix_out1")
    dwo = _mm(att, dmix1, "tn", "d_w_o", out_dtype=BF16)[0]
    dw[("wo", 0)] = dwo.reshape(N_CHIPS, dwo.shape[0] // N_CHIPS, d)
    datt = _mm(dmix1, rows("wo"), "nt", "d_attn_out", out_dtype=BF16)[0]
    dq, dk, dv = _attn_bwd(q, k, v, datt, lsum, "attn_bwd")
    for nm, act, dact in (("wq", hn2, dq), ("wk", kvn, dk), ("wv", kvn, dv)):
        g = _mm(act, dact, "tn", f"d_{nm}", out_dtype=BF16)[0]
        dw[(nm, 0)] = g.reshape(N_CHIPS, g.shape[0] // N_CHIPS, g.shape[1])
    dhn2 = _mm(dq, rows("wq"), "nt", "d_q_in")[0]
    dkvn_k = _mm(dk, rows("wk"), "nt", "d_k_in")[0]
    dkvn_v = _mm(dv, rows("wv"), "nt", "d_v_in")[0]
    dh2, (dg[("pre_mix", 1)], dg["kv"]) = _rms_bwd_in(
        dh3, h2, [([dhn2], gains(pre_mix_g, 1)), ([dkvn_k, dkvn_v], kv_g)], "d_norm_mix_in1")
    dhn1 = ffn_bwd(dh2, h1, hn1, a0, hm0, f0, 0)
    dh1, (dg[("pre_ffn", 0)],) = _rms_bwd_in(dh2, h1, [([dhn1], gains(pre_ffn_g, 0))], "d_norm_ffn_in0")
    dmix0, dg[("post_mix", 0)] = _rms_bwd_out(dh1, mix0, gains(post_mix_g, 0), "d_norm_mix_out0")
    dwout = _mm(gm, dmix0, "tn", "d_w_out", out_dtype=BF16)[0]
    dw[("wout", 0)] = dwout.reshape(N_CHIPS, dwout.shape[0] // N_CHIPS, d)
    dgm = _mm(dmix0, rows("wout"), "nt", "d_gmlp_gate")[0]
    duv, d_ws, d_bs, d_vg = _gmlp_bwd(uv, dgm, v_g, w_s, bias, "d_gmlp")
    dw[("win", 0)] = _mm(hn0, duv, "tn", "d_w_in", out_dtype=BF16, out_split=N_CHIPS)
    dhn0 = _mm(duv, full[("win", 0)], "nt", "d_gmlp_in")[0]
    dx, (dg[("pre_mix", 0)],) = _rms_bwd_in(dh1, h0, [([dhn0], gains(pre_mix_g, 0))], "d_norm_in")

    dws = [dw[u] for u in units]
    from_sibling = _swap_halves(dws, "grads_to_sibling")
    pair = [_pair_add(g, r, c_idx, f"pair_add_{nm}{layer}") for g, r, (nm, layer) in zip(dws, from_sibling, units)]
    at_owner = _scatter_to_owners(pair, "grads_to_owner")
    totals = [_chip_sum(p, f"chip_sum_{nm}{layer}") for p, (nm, layer) in zip(at_owner, units)]
    per_tensor = [[(layer, tot) for tot, (unm, layer) in zip(totals, units) if unm == nm] for nm in big]
    grads_big = dict(zip(big, _join_halves(per_tensor, [big[nm][0].shape for nm in big], "grads_join")))

    stack = lambda key: jnp.concatenate([dg[(key, layer)] for layer in range(n_layers)], axis=0)
    small_parts = [
        stack("pre_mix"), stack("post_mix"), stack("pre_ffn"), stack("post_ffn"),
        d_vg, d_ws, d_bs[::SUBLANE], dg["kv"],
        jnp.stack([dg[("conv_w", layer)] for layer in range(n_layers)]),
        jnp.stack([dg[("conv_b", layer)] for layer in range(n_layers)]),
    ]
    summed = _all_reduce_small(_pack(small_parts), "small_grads_sum")
    (g_pre_mix, g_post_mix, g_pre_ffn, g_post_ffn, g_vg, g_ws, g_bs, g_kv, g_cw, g_cb) = _unpack(
        summed, [p.shape for p in small_parts])
    g_vg = lax.dynamic_index_in_dim(g_vg.reshape(N_CHIPS, 1, d_a // N_CHIPS), chip, 0, keepdims=False)
    g_cw = lax.dynamic_index_in_dim(g_cw, chip, 1, keepdims=False)
    g_cb = g_cb.reshape(n_layers, N_CHIPS * ns)
    small = [
        (pre_mix_g, g_pre_mix, m_pre_mix_g, v_pre_mix_g),
        (post_mix_g, g_post_mix, m_post_mix_g, v_post_mix_g),
        (pre_ffn_g, g_pre_ffn, m_pre_ffn_g, v_pre_ffn_g),
        (post_ffn_g, g_post_ffn, m_post_ffn_g, v_post_ffn_g),
        (a_v_norm_g, g_vg, m_a_v_norm_g, v_a_v_norm_g),
        (a_w_spatial, g_ws[None], m_a_w_spatial, v_a_w_spatial),
        (a_b_spatial, g_bs[None], m_a_b_spatial, v_a_b_spatial),
        (kv_norm_g, g_kv.reshape(d), m_kv_norm_g, v_kv_norm_g),
        (ffn_conv_w, g_cw, m_ffn_conv_w, v_ffn_conv_w),
        (ffn_conv_b, g_cb, m_ffn_conv_b, v_ffn_conv_b),
    ]
    small = [(w, g.reshape(w.shape), m, v) for w, g, m, v in small]
    packed = [_pack([t[i] for t in small])[None] for i in range(4)]
    small_new = [_unpack(p[0], [t[0].shape for t in small]) for p in _adamw(*packed, "adamw_small")]

    new_big = {nm: _adamw(big[nm][0], grads_big[nm], big[nm][1], big[nm][2], f"adamw_{nm}") for nm in big}

    def big_out(nm, which):
        ref_shape = {"wk": w_k.shape, "wv": w_v.shape}.get(nm, big[nm][0].shape)
        arr = grads_big[nm] if which == 0 else new_big[nm][which - 1]
        return arr.reshape(ref_shape)

    order = ["pre_mix", "post_mix", "pre_ffn", "post_ffn", "win", "vg", "ws", "bs", "wout", "kv", "wk", "wv", "wq",
             "wo", "wup", "cw", "cb", "wdn"]
    small_at = {"pre_mix": 0, "post_mix": 1, "pre_ffn": 2, "post_ffn": 3, "vg": 4, "ws": 5, "bs": 6, "kv": 7,
                "cw": 8, "cb": 9}
    outs = [loss, dx[None]]
    for which in range(4):
        for nm in order:
            if nm in small_at:
                outs.append(small[small_at[nm]][1] if which == 0 else small_new[which - 1][small_at[nm]])
            else:
                outs.append(big_out(nm, which))
    return tuple(outs)
```

```python
import functools
import math

import jax
import jax.numpy as jnp
from jax import lax
from jax.experimental import pallas as pl
from jax.experimental.pallas import tpu as pltpu

F32 = jnp.float32
BF16 = jnp.bfloat16
EPS = 1e-6
ADAM_LR = 0.001
ADAM_B1 = 0.9
ADAM_B2 = 0.999
ADAM_EPS = 1e-08
ADAM_WD = 0.01
ADAM_STEP = 10

LANE = 128
SUBLANE = 8
ROWS = 16
TILE = 128
N_CHIPS = 4
N_DEV = 8
VMEM_LIMIT = 56 * 1024 * 1024
MESH = pl.DeviceIdType.MESH
ANY = pl.BlockSpec(memory_space=pl.ANY)
VMEM_SPEC = pl.BlockSpec(memory_space=pltpu.VMEM)


def _cp(*sem):
    return pltpu.CompilerParams(dimension_semantics=sem, vmem_limit_bytes=VMEM_LIMIT)


def _pick(dim, pref, align=LANE):
    if dim <= pref:
        return dim
    best = None
    for d in range(align, pref + 1, align):
        if dim % d == 0:
            best = d
    assert best is not None, (dim, pref)
    return best


_DIMS = {
    "nn": (((1,), (0,)), ((), ())),
    "nt": (((1,), (1,)), ((), ())),
    "tn": (((0,), (0,)), ((), ())),
}


def _as3(a):
    return a if a.ndim == 3 else a[None]


def _spec3(br, bc, cols_j, rc):
    per = cols_j // bc

    def imap(m, n, k):
        r, c = rc(m, n, k)
        return (c // per, r, c % per)

    return pl.BlockSpec((None, br, bc), imap)


def _mm(a, b, mode, name, out_dtype=F32, out_split=1):
    a, b = _as3(a), _as3(b)
    ja, ra, caj = a.shape
    jb, rb, cbj = b.shape
    if mode == "nn":
        m, k, n = ra, ja * caj, jb * cbj
        assert rb == k
        m_ext, k_ext, n_ext = [ra], [caj, rb], [cbj]
    elif mode == "nt":
        m, k, n = ra, ja * caj, rb
        assert jb * cbj == k
        m_ext, k_ext, n_ext = [ra], [caj, cbj], [rb]
    else:
        m, k, n = ja * caj, ra, jb * cbj
        assert rb == k
        m_ext, k_ext, n_ext = [caj], [ra], [cbj]
    assert n % out_split == 0
    n_ext.append(n // out_split)
    bm = _pick(math.gcd(*m_ext), 1536)
    bn = _pick(math.gcd(*n_ext), 1536)
    bk = _pick(math.gcd(*k_ext), 1536)
    nk = k // bk
    if mode == "nn":
        a_spec = _spec3(bm, bk, caj, lambda mi, ni, ki: (mi, ki))
        b_spec = _spec3(bk, bn, cbj, lambda mi, ni, ki: (ki, ni))
    elif mode == "nt":
        a_spec = _spec3(bm, bk, caj, lambda mi, ni, ki: (mi, ki))
        b_spec = _spec3(bn, bk, cbj, lambda mi, ni, ki: (ni, ki))
    else:
        a_spec = _spec3(bk, bm, caj, lambda mi, ni, ki: (ki, mi))
        b_spec = _spec3(bk, bn, cbj, lambda mi, ni, ki: (ki, ni))
    o_spec = _spec3(bm, bn, n // out_split, lambda mi, ni, ki: (mi, ni))
    dims = _DIMS[mode]

    def body(a_ref, b_ref, o_ref, acc_ref):
        ki = pl.program_id(2)
        part = lax.dot_general(a_ref[...].astype(BF16), b_ref[...].astype(BF16), dims, preferred_element_type=F32)

        @pl.when(ki == 0)
        def _():
            acc_ref[...] = part

        @pl.when(ki > 0)
        def _():
            acc_ref[...] += part

        @pl.when(ki == nk - 1)
        def _():
            o_ref[...] = acc_ref[...].astype(o_ref.dtype)

    return pl.pallas_call(
        body,
        name=name,
        grid=(m // bm, n // bn, nk),
        in_specs=[a_spec, b_spec],
        out_specs=o_spec,
        out_shape=jax.ShapeDtypeStruct((out_split, m, n // out_split), out_dtype),
        scratch_shapes=[pltpu.VMEM((bm, bn), F32)],
        compiler_params=_cp("parallel", "parallel", "arbitrary"),
    )(a, b)


def _rms(x, g):
    r = lax.rsqrt(jnp.mean(x * x, axis=-1, keepdims=True) + EPS)
    return x * r * g


def _rms_bwd(x, g, dy):
    r = lax.rsqrt(jnp.mean(x * x, axis=-1, keepdims=True) + EPS)
    xh = x * r
    gy = dy * g
    dx = r * (gy - xh * jnp.mean(gy * xh, axis=-1, keepdims=True))
    return dx, jnp.sum(dy * xh, axis=0, keepdims=True)


def _row_block(s):
    return _pick(s, 256, ROWS)


def _rms_fwd(h, g, name):
    s, d = h.shape
    br = _row_block(s)

    def body(h_ref, g_ref, o_ref):
        o_ref[...] = _rms(h_ref[...], g_ref[...]).astype(BF16)

    row = pl.BlockSpec((br, d), lambda i: (i, 0))
    vec = pl.BlockSpec((1, d), lambda i: (0, 0))
    return pl.pallas_call(
        body, name=name, grid=(s // br,), in_specs=[row, vec], out_specs=row,
        out_shape=jax.ShapeDtypeStruct((s, d), BF16), compiler_params=_cp("parallel"),
    )(h, g)


def _resid_rms(h_in, f, g_post, g_next, name):
    s, d = h_in.shape
    br = _row_block(s)
    n_next = len(g_next)

    def body(h_ref, f_ref, gp_ref, *refs):
        gn_refs, ho_ref, hn_refs = refs[:n_next], refs[n_next], refs[n_next + 1:]
        h = h_ref[...] + _rms(f_ref[...], gp_ref[...])
        ho_ref[...] = h
        for gn_ref, hn_ref in zip(gn_refs, hn_refs):
            hn_ref[...] = _rms(h, gn_ref[...]).astype(BF16)

    row = pl.BlockSpec((br, d), lambda i: (i, 0))
    vec = pl.BlockSpec((1, d), lambda i: (0, 0))
    return pl.pallas_call(
        body, name=name, grid=(s // br,),
        in_specs=[row, row, vec] + [vec] * n_next,
        out_specs=[row] * (1 + n_next),
        out_shape=[jax.ShapeDtypeStruct((s, d), F32)] + [jax.ShapeDtypeStruct((s, d), BF16)] * n_next,
        compiler_params=_cp("parallel"),
    )(h_in, f, g_post, *g_next)


def _loss_head(h_in, f, g_post, target, name):
    s, d = h_in.shape
    br = _row_block(s)

    def body(h_ref, f_ref, gp_ref, t_ref, dh_ref, loss_ref):
        @pl.when(pl.program_id(0) == 0)
        def _():
            loss_ref[...] = jnp.zeros_like(loss_ref)

        diff = h_ref[...] + _rms(f_ref[...], gp_ref[...]) - t_ref[...]
        dh_ref[...] = diff * (1.0 / d)
        loss_ref[...] += 0.5 * jnp.sum(jnp.mean(diff * diff, axis=-1, keepdims=True))

    row = pl.BlockSpec((br, d), lambda i: (i, 0))
    vec = pl.BlockSpec((1, d), lambda i: (0, 0))
    return pl.pallas_call(
        body, name=name, grid=(s // br,),
        in_specs=[row, row, vec, row],
        out_specs=[row, pl.BlockSpec((SUBLANE, LANE), lambda i: (0, 0))],
        out_shape=[jax.ShapeDtypeStruct((s, d), F32), jax.ShapeDtypeStruct((SUBLANE, LANE), F32)],
        compiler_params=_cp("arbitrary"),
    )(h_in, f, g_post, target)


def _rms_bwd_out(dy, f, g, name):
    s, d = f.shape
    br = _row_block(s)

    def body(dy_ref, f_ref, g_ref, df_ref, dg_ref):
        @pl.when(pl.program_id(0) == 0)
        def _():
            dg_ref[...] = jnp.zeros_like(dg_ref)

        dx, dg = _rms_bwd(f_ref[...], g_ref[...], dy_ref[...])
        df_ref[...] = dx.astype(BF16)
        dg_ref[...] += dg

    row = pl.BlockSpec((br, d), lambda i: (i, 0))
    vec = pl.BlockSpec((1, d), lambda i: (0, 0))
    return pl.pallas_call(
        body, name=name, grid=(s // br,), in_specs=[row, row, vec], out_specs=[row, vec],
        out_shape=[jax.ShapeDtypeStruct((s, d), BF16), jax.ShapeDtypeStruct((1, d), F32)],
        compiler_params=_cp("arbitrary"),
    )(dy, f, g)


def _rms_bwd_in(dh_out, h_in, branches, name):
    s, d = h_in.shape
    br = _row_block(s)
    counts = [len(ds) for ds, _ in branches]
    n_d = sum(counts)
    n_b = len(branches)

    def body(dho_ref, h_ref, *refs):
        d_refs, g_refs = refs[:n_d], refs[n_d:n_d + n_b]
        dh_ref, dg_refs = refs[n_d + n_b], refs[n_d + n_b + 1:]

        @pl.when(pl.program_id(0) == 0)
        def _():
            for r in dg_refs:
                r[...] = jnp.zeros_like(r)

        h = h_ref[...]
        acc = dho_ref[...]
        at = 0
        for bi, cnt in enumerate(counts):
            dn = d_refs[at][...]
            for r in d_refs[at + 1:at + cnt]:
                dn = dn + r[...]
            at += cnt
            dx, dg = _rms_bwd(h, g_refs[bi][...], dn)
            acc = acc + dx
            dg_refs[bi][...] += dg
        dh_ref[...] = acc

    row = pl.BlockSpec((br, d), lambda i: (i, 0))
    vec = pl.BlockSpec((1, d), lambda i: (0, 0))
    flat_d = [x for ds, _ in branches for x in ds]
    outs = pl.pallas_call(
        body, name=name, grid=(s // br,),
        in_specs=[row, row] + [row] * n_d + [vec] * n_b,
        out_specs=[row] + [vec] * n_b,
        out_shape=[jax.ShapeDtypeStruct((s, d), F32)] + [jax.ShapeDtypeStruct((1, d), F32)] * n_b,
        compiler_params=_cp("arbitrary"),
    )(dh_out, h_in, *flat_d, *[g for _, g in branches])
    return outs[0], list(outs[1:])


def _split3(x):
    x0 = x.astype(BF16)
    r1 = x - x0.astype(F32)
    x1 = r1.astype(BF16)
    x2 = (r1 - x1.astype(F32)).astype(BF16)
    return x0, x1, x2


def _dot_sel(x, sel, dims):
    out = None
    for t in _split3(x):
        p = lax.dot_general(t, sel, dims, preferred_element_type=F32)
        out = p if out is None else out + p
    return out


def _tri(n, kind):
    r = lax.broadcasted_iota(jnp.int32, (n, n), 0)
    c = lax.broadcasted_iota(jnp.int32, (n, n), 1)
    m = {"lt": r < c, "le": r <= c, "gt": r > c}[kind]
    return jnp.where(m, 1.0, 0.0).astype(BF16)


_GELU_C = math.sqrt(2.0 / math.pi)
_GELU_A = 0.044715


def _gelu(x):
    return 0.5 * x * (1.0 + jnp.tanh(_GELU_C * (x + _GELU_A * (x * x * x))))


def _gelu_grad(x):
    t = jnp.tanh(_GELU_C * (x + _GELU_A * (x * x * x)))
    return 0.5 * (1.0 + t) + 0.5 * x * (1.0 - t * t) * (_GELU_C * (1.0 + 3.0 * _GELU_A * (x * x)))


def _causal_w(w):
    r = lax.broadcasted_iota(jnp.int32, (TILE, TILE), 0)
    c = lax.broadcasted_iota(jnp.int32, (TILE, TILE), 1)
    return jnp.where(c <= r, w, 0.0)


def _uv_tiles(uv_ref, g, d_a, dq):
    cu, cv = g * TILE, d_a + g * TILE
    u = uv_ref[cu // dq, :, pl.ds(cu % dq, TILE)]
    v = uv_ref[cv // dq, :, pl.ds(cv % dq, TILE)]
    return u, v


def _gmlp_fwd(uv, v_g, w_s, bias, name):
    _, s, dq = uv.shape
    d_a = 2 * dq
    n_g = d_a // TILE

    def body(uv_ref, vg_ref, ws_ref, b_ref, o_ref):
        for g in range(n_g):
            up, vp = _uv_tiles(uv_ref, g, d_a, dq)
            cols = pl.ds(g * TILE, TILE)
            vn = _rms(_gelu(vp), vg_ref[:, cols])
            mixed = jnp.dot(_causal_w(ws_ref[g]).astype(BF16), vn.astype(BF16), preferred_element_type=F32) + b_ref[:, cols]
            o_ref[:, cols] = (_gelu(up) * mixed).astype(BF16)

    return pl.pallas_call(
        body, name=name, grid=(s // TILE,),
        in_specs=[
            pl.BlockSpec((4, TILE, dq), lambda i: (0, i, 0)),
            pl.BlockSpec((1, d_a), lambda i: (0, 0)),
            pl.BlockSpec((n_g, TILE, TILE), lambda i: (0, 0, 0)),
            pl.BlockSpec((TILE, d_a), lambda i: (0, 0)),
        ],
        out_specs=pl.BlockSpec((TILE, d_a), lambda i: (i, 0)),
        out_shape=jax.ShapeDtypeStruct((s, d_a), BF16),
        compiler_params=_cp("parallel"),
    )(uv, v_g, w_s, bias)


def _gmlp_bwd(uv, dgm, v_g, w_s, bias, name):
    _, s, dq = uv.shape
    d_a = 2 * dq
    n_g = d_a // TILE
    n_c = s // TILE

    def body(uv_ref, d_ref, vg_ref, ws_ref, b_ref, duv_ref, dws_ref, dbs_ref, dvg_ref, dbias_acc):
        i = pl.program_id(0)

        @pl.when(i == 0)
        def _():
            dws_ref[...] = jnp.zeros_like(dws_ref)
            dvg_ref[...] = jnp.zeros_like(dvg_ref)
            dbias_acc[...] = jnp.zeros_like(dbias_acc)

        for g in range(n_g):
            up, vp = _uv_tiles(uv_ref, g, d_a, dq)
            cols = pl.ds(g * TILE, TILE)
            vg = vg_ref[:, cols]
            u = _gelu(up)
            v = _gelu(vp)
            r = lax.rsqrt(jnp.mean(v * v, axis=-1, keepdims=True) + EPS)
            vh = v * r
            vn = (vh * vg).astype(BF16)
            wc = _causal_w(ws_ref[g]).astype(BF16)
            mixed = jnp.dot(wc, vn, preferred_element_type=F32) + b_ref[:, cols]
            d_out = d_ref[:, cols]
            du = d_out * mixed
            dmixed = d_out * u
            dmb = dmixed.astype(BF16)
            dvn = lax.dot_general(wc, dmb, _DIMS["tn"], preferred_element_type=F32)
            dws_ref[g] += lax.dot_general(dmb, vn, _DIMS["nt"], preferred_element_type=F32)
            dbias_acc[:, cols] += dmixed
            dvg_ref[:, cols] += jnp.sum(dvn * vh, axis=0, keepdims=True)
            gv = dvn * vg
            dv = r * (gv - vh * jnp.mean(gv * vh, axis=-1, keepdims=True))
            cu, cv = g * TILE, d_a + g * TILE
            duv_ref[cu // dq, :, pl.ds(cu % dq, TILE)] = (du * _gelu_grad(up)).astype(BF16)
            duv_ref[cv // dq, :, pl.ds(cv % dq, TILE)] = (dv * _gelu_grad(vp)).astype(BF16)

        @pl.when(i == n_c - 1)
        def _():
            ones = jnp.ones((SUBLANE, TILE), BF16)
            for g in range(n_g):
                dws_ref[g] = _causal_w(dws_ref[g])
                cols = pl.ds(g * TILE, TILE)
                out = None
                for t in _split3(dbias_acc[:, cols]):
                    p = lax.dot_general(ones, t, _DIMS["nt"], preferred_element_type=F32)
                    out = p if out is None else out + p
                dbs_ref[pl.ds(g * SUBLANE, SUBLANE), :] = out

    return pl.pallas_call(
        body, name=name, grid=(n_c,),
        in_specs=[
            pl.BlockSpec((4, TILE, dq), lambda i: (0, i, 0)),
            pl.BlockSpec((TILE, d_a), lambda i: (i, 0)),
            pl.BlockSpec((1, d_a), lambda i: (0, 0)),
            pl.BlockSpec((n_g, TILE, TILE), lambda i: (0, 0, 0)),
            pl.BlockSpec((TILE, d_a), lambda i: (0, 0)),
        ],
        out_specs=[
            pl.BlockSpec((4, TILE, dq), lambda i: (0, i, 0)),
            pl.BlockSpec((n_g, TILE, TILE), lambda i: (0, 0, 0)),
            pl.BlockSpec((n_g * SUBLANE, TILE), lambda i: (0, 0)),
            pl.BlockSpec((1, d_a), lambda i: (0, 0)),
        ],
        out_shape=[
            jax.ShapeDtypeStruct((4, s, dq), BF16),
            jax.ShapeDtypeStruct((n_g, TILE, TILE), F32),
            jax.ShapeDtypeStruct((n_g * SUBLANE, TILE), F32),
            jax.ShapeDtypeStruct((1, d_a), F32),
        ],
        scratch_shapes=[pltpu.VMEM((TILE, d_a), F32)],
        compiler_params=_cp("arbitrary"),
    )(uv, dgm, v_g, w_s, bias)


def _sigmoid(x):
    return 1.0 / (1.0 + jnp.exp(-x))


def _conv3(ext, w, b):
    return b + ((w[0:1] * pltpu.roll(ext, 2, 0) + w[1:2] * pltpu.roll(ext, 1, 0)) + w[2:3] * ext)


def _act_blocks(s, ns):
    return _pick(s, 512, ROWS), _pick(ns, 256)


def _ffn_act_fwd(a, cw, cb, name):
    _, s, ns = a.shape
    bs, cb_w = _act_blocks(s, ns)
    hb = bs // SUBLANE

    def body(a_ref, prev_ref, cw_ref, cb_ref, o_ref):
        first = pl.program_id(0) == 0

        def conv(comp):
            prev = jnp.where(first, 0.0, prev_ref[comp])
            ext = jnp.concatenate([prev, a_ref[comp]], axis=0)
            return _conv3(ext, cw_ref[comp], cb_ref[comp])[SUBLANE:]

        for p in range(2):
            cg = conv(p)
            o_ref[p] = (cg * _sigmoid(cg) * conv(2 + p)).astype(BF16)

    return pl.pallas_call(
        body, name=name, grid=(s // bs, ns // cb_w),
        in_specs=[
            pl.BlockSpec((4, bs, cb_w), lambda i, j: (0, i, j)),
            pl.BlockSpec((4, SUBLANE, cb_w), lambda i, j: (0, jnp.maximum(i * hb - 1, 0), j)),
            pl.BlockSpec((4, 3, cb_w), lambda i, j: (0, 0, j)),
            pl.BlockSpec((4, 1, cb_w), lambda i, j: (0, 0, j)),
        ],
        out_specs=pl.BlockSpec((2, bs, cb_w), lambda i, j: (0, i, j)),
        out_shape=jax.ShapeDtypeStruct((2, s, ns), BF16),
        compiler_params=_cp("parallel", "parallel"),
    )(a, a, cw, cb)


def _ffn_act_bwd(a, dhm, cw, cb, name):
    _, s, ns = a.shape
    bs, cb_w = _act_blocks(s, ns)
    hb = bs // SUBLANE
    n_i = s // bs
    n_ext = bs + 2 * SUBLANE
    cur = slice(SUBLANE, SUBLANE + bs)

    def body(a_ref, prev_ref, next_ref, d_ref, dnext_ref, cw_ref, cb_ref, da_ref, dcw_ref, dcb_ref):
        i = pl.program_id(1)
        first, last = i == 0, i == n_i - 1

        @pl.when(first)
        def _():
            dcw_ref[...] = jnp.zeros_like(dcw_ref)
            dcb_ref[...] = jnp.zeros_like(dcb_ref)

        def ext_of(comp):
            return jnp.concatenate([jnp.where(first, 0.0, prev_ref[comp]), a_ref[comp], next_ref[comp]], axis=0)

        def back(comp, a_ext, dc):
            w = cw_ref[comp]
            da = (w[2:3] * dc + w[1:2] * pltpu.roll(dc, n_ext - 1, 0)) + w[0:1] * pltpu.roll(dc, n_ext - 2, 0)
            da_ref[comp] = da[cur].astype(BF16)
            dcc = dc[cur]
            dcw_ref[comp, 0:1, :] += jnp.sum(dcc * pltpu.roll(a_ext, 2, 0)[cur], axis=0, keepdims=True)
            dcw_ref[comp, 1:2, :] += jnp.sum(dcc * pltpu.roll(a_ext, 1, 0)[cur], axis=0, keepdims=True)
            dcw_ref[comp, 2:3, :] += jnp.sum(dcc * a_ext[cur], axis=0, keepdims=True)
            dcb_ref[comp] += jnp.sum(dcc, axis=0, keepdims=True)

        for p in range(2):
            ag, av = ext_of(p), ext_of(2 + p)
            cg = _conv3(ag, cw_ref[p], cb_ref[p])
            cv = _conv3(av, cw_ref[2 + p], cb_ref[2 + p])
            d = jnp.concatenate(
                [jnp.zeros((SUBLANE, cb_w), F32), d_ref[p], jnp.where(last, 0.0, dnext_ref[p])], axis=0)
            sg = _sigmoid(cg)
            back(2 + p, av, d * (cg * sg))
            back(p, ag, d * cv * (sg * (1.0 + cg * (1.0 - sg))))

    return pl.pallas_call(
        body, name=name, grid=(ns // cb_w, n_i),
        in_specs=[
            pl.BlockSpec((4, bs, cb_w), lambda j, i: (0, i, j)),
            pl.BlockSpec((4, SUBLANE, cb_w), lambda j, i: (0, jnp.maximum(i * hb - 1, 0), j)),
            pl.BlockSpec((4, SUBLANE, cb_w), lambda j, i: (0, jnp.minimum((i + 1) * hb, n_i * hb - 1), j)),
            pl.BlockSpec((2, bs, cb_w), lambda j, i: (0, i, j)),
            pl.BlockSpec((2, SUBLANE, cb_w), lambda j, i: (0, jnp.minimum((i + 1) * hb, n_i * hb - 1), j)),
            pl.BlockSpec((4, 3, cb_w), lambda j, i: (0, 0, j)),
            pl.BlockSpec((4, 1, cb_w), lambda j, i: (0, 0, j)),
        ],
        out_specs=[
            pl.BlockSpec((4, bs, cb_w), lambda j, i: (0, i, j)),
            pl.BlockSpec((4, 3, cb_w), lambda j, i: (0, 0, j)),
            pl.BlockSpec((4, 1, cb_w), lambda j, i: (0, 0, j)),
        ],
        out_shape=[
            jax.ShapeDtypeStruct((4, s, ns), BF16),
            jax.ShapeDtypeStruct((4, 3, ns), F32),
            jax.ShapeDtypeStruct((4, 1, ns), F32),
        ],
        compiler_params=_cp("parallel", "arbitrary"),
    )(a, a, a, dhm, dhm, cw, cb)


ATT_BQ = 512
ATT_BK = 256


def _att_blocks(s):
    bq = _pick(s, ATT_BQ)
    bk = min(ATT_BK, bq)
    assert bq % bk == 0
    return bq, bk


def _dot_sel2(x, sel):
    hi = x.astype(BF16)
    lo = (x - hi.astype(F32)).astype(BF16)
    n = x.shape[0]
    both = jnp.dot(jnp.concatenate([hi, lo], axis=0), sel, preferred_element_type=F32)
    return both[:n] + both[n:]


def _causal_mask(bq, bk, row0, col0):
    rows = row0 + lax.broadcasted_iota(jnp.int32, (bq, bk), 0)
    cols = col0 + lax.broadcasted_iota(jnp.int32, (bq, bk), 1)
    return cols < rows


def _sb_tile(qb, kb, scale, mask):
    z = lax.dot_general(qb, kb, _DIMS["nt"], preferred_element_type=F32) * scale
    e = jnp.exp(-jnp.abs(z))
    lb = jnp.minimum(z, 0.0) - jnp.log(1.0 + e)
    l1m = lb - z
    if mask is not None:
        l1m = jnp.where(mask, l1m, 0.0)
    return z, e, lb, l1m


def _attn_fwd(q, k, v, name):
    s, hd = q.shape
    bq, bk = _att_blocks(s)
    r = bq // bk
    n_h, n_q = hd // TILE, s // bq
    scale = 1.0 / math.sqrt(TILE)

    def body(q_ref, k_ref, v_ref, o_ref, l_ref, acc_ref, suf_ref):
        i = pl.program_id(1)
        qb = q_ref[...]
        later = _tri(bk, "gt")
        acc_ref[...] = jnp.zeros_like(acc_ref)
        suf_ref[...] = jnp.zeros_like(suf_ref)

        def tile(j, masked):
            rows = pl.ds(pl.multiple_of(j * bk, bk), bk)
            mask = _causal_mask(bq, bk, i * bq, j * bk) if masked else None
            _, _, lb, l1m = _sb_tile(qb, k_ref[rows, :], scale, mask)
            a = jnp.exp(lb + _dot_sel2(l1m, later) + suf_ref[...])
            if masked:
                a = jnp.where(mask, a, 0.0)
            acc_ref[...] += jnp.dot(a.astype(BF16), v_ref[rows, :], preferred_element_type=F32)
            suf_ref[...] += jnp.sum(l1m, axis=1, keepdims=True)

        for dgl in range(r - 1, -1, -1):
            tile(r * i + dgl, True)

        def step(t, carry):
            tile(r * i - 1 - t, False)
            return carry

        lax.fori_loop(0, r * i, step, 0)
        o_ref[...] = acc_ref[...].astype(BF16)
        l_ref[...] = jnp.broadcast_to(suf_ref[...], (bq, TILE))

    blk = pl.BlockSpec((bq, TILE), lambda h, i: (i, h))
    head = pl.BlockSpec((s, TILE), lambda h, i: (0, h))
    return pl.pallas_call(
        body, name=name, grid=(n_h, n_q), in_specs=[blk, head, head], out_specs=[blk, blk],
        out_shape=[jax.ShapeDtypeStruct((s, hd), BF16), jax.ShapeDtypeStruct((s, hd), F32)],
        scratch_shapes=[pltpu.VMEM((bq, TILE), F32), pltpu.VMEM((bq, 1), F32)],
        compiler_params=_cp("parallel", "parallel"),
    )(q, k, v)


def _attn_bwd(q, k, v, do, lsum, name):
    s, hd = q.shape
    bq, bk = _att_blocks(s)
    r = bq // bk
    n_h, n_q = hd // TILE, s // bq
    scale = 1.0 / math.sqrt(TILE)

    def body(q_ref, k_ref, v_ref, do_ref, l_ref, dq_ref, dk_ref, dv_ref, dq_acc, pre_ref, cp_ref):
        i = pl.program_id(1)

        @pl.when(i == 0)
        def _():
            dk_ref[...] = jnp.zeros_like(dk_ref)
            dv_ref[...] = jnp.zeros_like(dv_ref)

        qb = q_ref[...]
        dob = do_ref[...]
        upto = _tri(bk, "le")
        before = _tri(bk, "lt")
        dq_acc[...] = jnp.zeros_like(dq_acc)
        pre_ref[...] = jnp.zeros_like(pre_ref)
        cp_ref[...] = jnp.zeros_like(cp_ref)

        def tile(j, masked):
            rows = pl.ds(pl.multiple_of(j * bk, bk), bk)
            kb, vb = k_ref[rows, :], v_ref[rows, :]
            mask = _causal_mask(bq, bk, i * bq, j * bk) if masked else None
            z, e, lb, l1m = _sb_tile(qb, kb, scale, mask)
            suffix = (l_ref[:, 0:1] - pre_ref[...]) - _dot_sel2(l1m, upto)
            a = jnp.exp(lb + suffix)
            if masked:
                a = jnp.where(mask, a, 0.0)
            p = a * lax.dot_general(dob, vb, _DIMS["nt"], preferred_element_type=F32)
            cprev = cp_ref[...] + _dot_sel2(p, before)
            inv = pl.reciprocal(1.0 + e, approx=True)
            pos = z >= 0.0
            dz = p * (jnp.where(pos, e, 1.0) * inv) - cprev * (jnp.where(pos, 1.0, e) * inv)
            if masked:
                dz = jnp.where(mask, dz, 0.0)
            dz = (dz * scale).astype(BF16)
            dq_acc[...] += jnp.dot(dz, kb, preferred_element_type=F32)
            dk_ref[rows, :] += lax.dot_general(dz, qb, _DIMS["tn"], preferred_element_type=F32)
            dv_ref[rows, :] += lax.dot_general(a.astype(BF16), dob, _DIMS["tn"], preferred_element_type=F32)
            pre_ref[...] += jnp.sum(l1m, axis=1, keepdims=True)
            cp_ref[...] += jnp.sum(p, axis=1, keepdims=True)

        def step(j, carry):
            tile(j, False)
            return carry

        lax.fori_loop(0, r * i, step, 0)
        for dgl in range(r):
            tile(r * i + dgl, True)
        dq_ref[...] = dq_acc[...].astype(BF16)

    blk = pl.BlockSpec((bq, TILE), lambda h, i: (i, h))
    head = pl.BlockSpec((s, TILE), lambda h, i: (0, h))
    return pl.pallas_call(
        body, name=name, grid=(n_h, n_q), in_specs=[blk, head, head, blk, blk], out_specs=[blk, head, head],
        out_shape=[jax.ShapeDtypeStruct((s, hd), BF16), jax.ShapeDtypeStruct((s, hd), F32),
                   jax.ShapeDtypeStruct((s, hd), F32)],
        scratch_shapes=[pltpu.VMEM((bq, TILE), F32), pltpu.VMEM((bq, 1), F32), pltpu.VMEM((bq, 1), F32)],
        compiler_params=_cp("parallel", "arbitrary"),
    )(q, k, v, do, lsum)


def _cast_bf16(w, layer, name):
    _, r, c = w.shape
    br, bc = _pick(r, 256, ROWS), _pick(c, 1024)

    def body(w_ref, o_ref):
        o_ref[...] = w_ref[...].astype(BF16)

    return pl.pallas_call(
        body, name=name, grid=(r // br, c // bc),
        in_specs=[pl.BlockSpec((None, br, bc), lambda i, j: (layer, i, j))],
        out_specs=pl.BlockSpec((br, bc), lambda i, j: (i, j)),
        out_shape=jax.ShapeDtypeStruct((r, c), BF16), compiler_params=_cp("parallel", "parallel"),
    )(w)


def _pair_add(dw, recv, c_idx, name):
    _, r, c = dw.shape
    hr = r // 2
    br, bc = _pick(hr, 256, ROWS), _pick(c, 1024)
    nb = hr // br

    def body(c_ref, a_ref, b_ref, o_ref):
        o_ref[...] = (a_ref[...].astype(F32) + b_ref[...].astype(F32)).astype(BF16)

    return pl.pallas_call(
        body, name=name,
        grid_spec=pltpu.PrefetchScalarGridSpec(
            num_scalar_prefetch=1, grid=(N_CHIPS, nb, c // bc),
            in_specs=[
                pl.BlockSpec((None, br, bc), lambda s, i, j, c_ref: (s, c_ref[0] * nb + i, j)),
                pl.BlockSpec((None, br, bc), lambda s, i, j, c_ref: (s, i, j)),
            ],
            out_specs=pl.BlockSpec((None, br, bc), lambda s, i, j, c_ref: (s, i, j)),
        ),
        out_shape=jax.ShapeDtypeStruct((N_CHIPS, hr, c), BF16),
        compiler_params=_cp("parallel", "parallel", "parallel"),
    )(c_idx, dw, recv)


def _chip_sum(parts, name):
    _, r, c = parts.shape
    br, bc = _pick(r, 256, ROWS), _pick(c, 1024)

    def body(p_ref, o_ref):
        acc = p_ref[0].astype(F32)
        for s in range(1, N_CHIPS):
            acc = acc + p_ref[s].astype(F32)
        o_ref[...] = acc

    return pl.pallas_call(
        body, name=name, grid=(r // br, c // bc),
        in_specs=[pl.BlockSpec((N_CHIPS, br, bc), lambda i, j: (0, i, j))],
        out_specs=pl.BlockSpec((br, bc), lambda i, j: (i, j)),
        out_shape=jax.ShapeDtypeStruct((r, c), F32), compiler_params=_cp("parallel", "parallel"),
    )(parts)


def _adamw(w, g, m, v, name):
    n_l, r, c = w.shape
    br, bc = _pick(r, 256, ROWS), _pick(c, 1024)

    def body(w_ref, g_ref, m_ref, v_ref, d_ref, mo_ref, vo_ref):
        g = g_ref[...]
        m = ADAM_B1 * m_ref[...] + (1.0 - ADAM_B1) * g
        v = ADAM_B2 * v_ref[...] + (1.0 - ADAM_B2) * (g * g)
        m_hat = m / (1.0 - ADAM_B1 ** ADAM_STEP)
        v_hat = v / (1.0 - ADAM_B2 ** ADAM_STEP)
        d_ref[...] = -ADAM_LR * (m_hat / (jnp.sqrt(v_hat) + ADAM_EPS) + ADAM_WD * w_ref[...])
        mo_ref[...] = m
        vo_ref[...] = v

    blk = pl.BlockSpec((None, br, bc), lambda l, i, j: (l, i, j))
    return pl.pallas_call(
        body, name=name, grid=(n_l, r // br, c // bc), in_specs=[blk] * 4, out_specs=[blk] * 3,
        out_shape=[jax.ShapeDtypeStruct(w.shape, F32)] * 3, compiler_params=_cp("parallel", "parallel", "parallel"),
    )(w, g, m, v)


def _place():
    x, y, c = lax.axis_index("x"), lax.axis_index("y"), lax.axis_index("c")
    chips = [(1 - x, y), (x, 1 - y), (1 - x, 1 - y)]
    return x, y, c, chips


def _gather_shards(shards, name):
    n = len(shards)
    halves = [a.shape[0] // 2 for a in shards]
    for a, hr in zip(shards, halves):
        assert a.shape[0] == 2 * hr and hr % (2 * SUBLANE) == 0, a.shape

    def body(*refs):
        ins, outs = refs[:n], refs[n:2 * n]
        send_sems, recv_sems, local_sems = refs[2 * n:]
        x, y, c, chips = _place()
        me = 2 * x + y

        def half_of(i, core):
            return pl.ds(pl.multiple_of(core * halves[i], 2 * SUBLANE), halves[i])

        def remote(i, k, slot, core, to, src=None):
            dst = outs[i].at[slot, half_of(i, core)]
            return pltpu.make_async_remote_copy(
                src_ref=dst if src is None else src, dst_ref=dst,
                send_sem=send_sems.at[i * 6 + k], recv_sem=recv_sems.at[i * 6 + k],
                device_id=to, device_id_type=MESH)

        own = [pltpu.make_async_copy(ins[i], outs[i].at[me], local_sems.at[i]) for i in range(n)]
        for cp in own:
            cp.start()
        sends = []
        for i in range(n):
            for k, (px, py) in enumerate(chips):
                sends.append(remote(i, k, me, c, (px, py, c), src=ins[i].at[half_of(i, c)]))
        for cp in sends:
            cp.start()
        passed = []
        for i in range(n):
            for k, (px, py) in enumerate(chips):
                remote(i, k, 2 * px + py, c, (x, y, c)).wait_recv()
                fwd = remote(i, 3 + k, 2 * px + py, c, (x, y, 1 - c))
                fwd.start()
                passed.append(fwd)
        for i in range(n):
            for k, (px, py) in enumerate(chips):
                remote(i, 3 + k, 2 * px + py, 1 - c, (x, y, c)).wait_recv()
        for cp in sends + passed:
            cp.wait_send()
        for cp in own:
            cp.wait()

    return pl.pallas_call(
        body, name=name,
        in_specs=[ANY] * n, out_specs=[ANY] * n,
        out_shape=[jax.ShapeDtypeStruct((N_CHIPS,) + a.shape, a.dtype) for a in shards],
        scratch_shapes=[pltpu.SemaphoreType.DMA((6 * n,)), pltpu.SemaphoreType.DMA((6 * n,)),
                        pltpu.SemaphoreType.DMA((n,))],
    )(*shards)


def _swap_halves(grads, name):
    n = len(grads)
    halves = [a.shape[1] // 2 for a in grads]

    def body(*refs):
        ins, outs = refs[:n], refs[n:2 * n]
        send_sems, recv_sems = refs[2 * n:]
        x, y, c, _ = _place()
        copies = []
        for i in range(n):
            rows = pl.ds(pl.multiple_of((1 - c) * halves[i], 2 * SUBLANE), halves[i])
            copies.append(pltpu.make_async_remote_copy(
                src_ref=ins[i].at[:, rows, :], dst_ref=outs[i], send_sem=send_sems.at[i], recv_sem=recv_sems.at[i],
                device_id=(x, y, 1 - c), device_id_type=MESH))
        for cp in copies:
            cp.start()
        for cp in copies:
            cp.wait()

    return pl.pallas_call(
        body, name=name, in_specs=[ANY] * n, out_specs=[ANY] * n,
        out_shape=[jax.ShapeDtypeStruct((N_CHIPS, hr, a.shape[2]), a.dtype) for a, hr in zip(grads, halves)],
        scratch_shapes=[pltpu.SemaphoreType.DMA((n,)), pltpu.SemaphoreType.DMA((n,))],
    )(*grads)


def _scatter_to_owners(parts, name):
    n = len(parts)

    def body(*refs):
        ins, outs = refs[:n], refs[n:2 * n]
        send_sems, recv_sems, local_sems = refs[2 * n:]
        x, y, c, chips = _place()
        me = 2 * x + y
        own = [pltpu.make_async_copy(ins[i].at[me], outs[i].at[me], local_sems.at[i]) for i in range(n)]
        for cp in own:
            cp.start()
        sends = []
        for i in range(n):
            for k, (px, py) in enumerate(chips):
                sends.append(pltpu.make_async_remote_copy(
                    src_ref=ins[i].at[2 * px + py], dst_ref=outs[i].at[me],
                    send_sem=send_sems.at[3 * i + k], recv_sem=recv_sems.at[3 * i + k],
                    device_id=(px, py, c), device_id_type=MESH))
        for cp in sends:
            cp.start()
        for i in range(n):
            for k, (px, py) in enumerate(chips):
                slot = outs[i].at[2 * px + py]
                pltpu.make_async_remote_copy(
                    src_ref=slot, dst_ref=slot, send_sem=send_sems.at[3 * i + k], recv_sem=recv_sems.at[3 * i + k],
                    device_id=(x, y, c), device_id_type=MESH).wait_recv()
        for cp in sends:
            cp.wait_send()
        for cp in own:
            cp.wait()

    return pl.pallas_call(
        body, name=name, in_specs=[ANY] * n, out_specs=[ANY] * n,
        out_shape=[jax.ShapeDtypeStruct(a.shape, a.dtype) for a in parts],
        scratch_shapes=[pltpu.SemaphoreType.DMA((3 * n,)), pltpu.SemaphoreType.DMA((3 * n,)),
                        pltpu.SemaphoreType.DMA((n,))],
    )(*parts)


def _join_halves(units, shapes, name):
    flat = [(t, layer, tot) for t, lst in enumerate(units) for layer, tot in lst]
    n = len(flat)
    n_out = len(shapes)

    def body(*refs):
        ins, outs = refs[:n], refs[n:n + n_out]
        send_sems, recv_sems, local_sems = refs[n + n_out:]
        x, y, c, _ = _place()

        def dst(i, core):
            t, layer, tot = flat[i]
            hr = tot.shape[0]
            return outs[t].at[layer, pl.ds(pl.multiple_of(core * hr, SUBLANE), hr), :]

        own = [pltpu.make_async_copy(ins[i], dst(i, c), local_sems.at[i]) for i in range(n)]
        sends = [pltpu.make_async_remote_copy(
            src_ref=ins[i], dst_ref=dst(i, c), send_sem=send_sems.at[i], recv_sem=recv_sems.at[i],
            device_id=(x, y, 1 - c), device_id_type=MESH) for i in range(n)]
        for cp in own + sends:
            cp.start()
        for i in range(n):
            other = dst(i, 1 - c)
            pltpu.make_async_remote_copy(
                src_ref=other, dst_ref=other, send_sem=send_sems.at[i], recv_sem=recv_sems.at[i],
                device_id=(x, y, c), device_id_type=MESH).wait_recv()
        for cp in sends:
            cp.wait_send()
        for cp in own:
            cp.wait()

    return pl.pallas_call(
        body, name=name, in_specs=[ANY] * n, out_specs=[ANY] * n_out,
        out_shape=[jax.ShapeDtypeStruct(s, F32) for s in shapes],
        scratch_shapes=[pltpu.SemaphoreType.DMA((n,)), pltpu.SemaphoreType.DMA((n,)), pltpu.SemaphoreType.DMA((n,))],
    )(*[tot for _, _, tot in flat])


def _all_reduce_small(packed, name):
    r, c = packed.shape
    chunk = _pick(r, 256, ROWS)

    def body(x_ref, out_ref, gath, send_sems, recv_sems, local_sem):
        x, y, cc, chips = _place()
        me, sibling = (x, y, cc), (x, y, 1 - cc)

        def slot(px, py, pc):
            return gath.at[4 * px + 2 * py + pc]

        def copy(k, block, to, src=None):
            return pltpu.make_async_remote_copy(
                src_ref=slot(*block) if src is None else src, dst_ref=slot(*block),
                send_sem=send_sems.at[k], recv_sem=recv_sems.at[k], device_id=to, device_id_type=MESH)

        mine = pltpu.make_async_copy(x_ref, slot(*me), local_sem)
        mine.start()
        first = [copy(0, me, sibling, src=x_ref)]
        first += [copy(1 + j, me, (*chip, cc), src=x_ref) for j, chip in enumerate(chips)]
        for cp in first:
            cp.start()
        passed = [copy(4 + j, (*chip, cc), sibling) for j, chip in enumerate(chips)]
        for j, chip in enumerate(chips):
            copy(1 + j, (*chip, cc), me).wait_recv()
            passed[j].start()
        copy(0, sibling, me).wait_recv()
        for j, chip in enumerate(chips):
            copy(4 + j, (*chip, 1 - cc), me).wait_recv()
        for cp in first + passed:
            cp.wait_send()
        mine.wait()

        def add(i, carry):
            rows = pl.ds(pl.multiple_of(i * chunk, SUBLANE), chunk)
            acc = gath[0, rows, :]
            for dev in range(1, N_DEV):
                acc = acc + gath[dev, rows, :]
            out_ref[rows, :] = acc
            return carry

        lax.fori_loop(0, r // chunk, add, 0)

    return pl.pallas_call(
        body, name=name, in_specs=[VMEM_SPEC], out_specs=VMEM_SPEC,
        out_shape=jax.ShapeDtypeStruct((r, c), F32),
        scratch_shapes=[pltpu.VMEM((N_DEV, r, c), F32), pltpu.SemaphoreType.DMA((7,)),
                        pltpu.SemaphoreType.DMA((7,)), pltpu.SemaphoreType.DMA],
        compiler_params=pltpu.CompilerParams(vmem_limit_bytes=VMEM_LIMIT),
    )(packed)


_PACK_ROWS = 256


def _pack(arrays):
    flat = jnp.concatenate([a.reshape(-1).astype(F32) for a in arrays])
    unit = _PACK_ROWS * LANE
    total = -(-flat.shape[0] // unit) * unit
    return jnp.pad(flat, (0, total - flat.shape[0])).reshape(-1, LANE)


def _unpack(packed, shapes, lead=()):
    flat = packed.reshape(lead + (-1,))
    out, at = [], 0
    for s in shapes:
        size = math.prod(s)
        out.append(flat[..., at:at + size].reshape(lead + tuple(s)))
        at += size
    return out


def kernel(x, pre_mix_g, post_mix_g, pre_ffn_g, post_ffn_g, a_w_in, a_v_norm_g, a_w_spatial, a_b_spatial, a_w_out, kv_norm_g, w_k, w_v, b_w_q, b_w_o, ffn_w_up, ffn_conv_w, ffn_conv_b, ffn_w_down, loss_target, m_pre_mix_g, m_post_mix_g, m_pre_ffn_g, m_post_ffn_g, m_a_w_in, m_a_v_norm_g, m_a_w_spatial, m_a_b_spatial, m_a_w_out, m_kv_norm_g, m_w_k, m_w_v, m_b_w_q, m_b_w_o, m_ffn_w_up, m_ffn_conv_w, m_ffn_conv_b, m_ffn_w_down, v_pre_mix_g, v_post_mix_g, v_pre_ffn_g, v_post_ffn_g, v_a_w_in, v_a_v_norm_g, v_a_w_spatial, v_a_b_spatial, v_a_w_out, v_kv_norm_g, v_w_k, v_w_v, v_b_w_q, v_b_w_o, v_ffn_w_up, v_ffn_conv_w, v_ffn_conv_b, v_ffn_w_down):
    xi, yi, ci = lax.axis_index("x"), lax.axis_index("y"), lax.axis_index("c")
    chip = 2 * xi + yi
    c_idx = jnp.reshape(ci, (1,)).astype(jnp.int32)
    _, s, d = x.shape
    n_layers = pre_mix_g.shape[0]
    assert n_layers == 2 and a_w_in.shape[0] == 1 and b_w_q.shape[0] == 1
    d_a = a_w_out.shape[1] * N_CHIPS
    n_g = a_w_spatial.shape[1]
    ns = ffn_w_up.shape[2]
    assert a_w_spatial.shape[2] == TILE and d_a == n_g * TILE and s % TILE == 0
    h0 = x[0]
    target = loss_target[0]

    big = {
        "win": (a_w_in, m_a_w_in, v_a_w_in),
        "wout": (a_w_out, m_a_w_out, v_a_w_out),
        "wk": (w_k[None], m_w_k[None], v_w_k[None]),
        "wv": (w_v[None], m_w_v[None], v_w_v[None]),
        "wq": (b_w_q, m_b_w_q, v_b_w_q),
        "wo": (b_w_o, m_b_w_o, v_b_w_o),
        "wup": (ffn_w_up, m_ffn_w_up, v_ffn_w_up),
        "wdn": (ffn_w_down, m_ffn_w_down, v_ffn_w_down),
    }
    units = [(nm, layer) for nm in big for layer in range(big[nm][0].shape[0])]
    shards = [_cast_bf16(big[nm][0], layer, f"cast_{nm}{layer}") for nm, layer in units]
    small_sharded = _pack([a_v_norm_g, ffn_conv_w])
    gathered = _gather_shards(shards + [small_sharded], "gather_weights")
    full = dict(zip(units, gathered[:-1]))
    vg_parts, cw_parts = _unpack(gathered[-1], [a_v_norm_g.shape, ffn_conv_w.shape], lead=(N_CHIPS,))
    v_g = jnp.transpose(vg_parts, (1, 0, 2)).reshape(1, d_a)

    def rows(nm, layer=0):
        w = full[(nm, layer)]
        return w.reshape(w.shape[0] * w.shape[1], w.shape[2])

    gains = lambda g, layer: g[layer:layer + 1]
    bias = jnp.repeat(a_b_spatial[0].T, TILE, axis=1)
    w_s = a_w_spatial[0]
    kv_g = kv_norm_g[None]
    conv_w = [cw_parts[:, layer] for layer in range(n_layers)]
    conv_b = [ffn_conv_b[layer].reshape(N_CHIPS, 1, ns) for layer in range(n_layers)]

    def ffn_fwd(hn, layer):
        a = _mm(hn, full[("wup", layer)], "nn", f"ffn_up{layer}", out_split=N_CHIPS)
        hm = _ffn_act_fwd(a, conv_w[layer], conv_b[layer], f"ffn_act{layer}")
        return a, hm, _mm(hm, rows("wdn", layer), "nn", f"ffn_down{layer}")[0]

    hn0 = _rms_fwd(h0, gains(pre_mix_g, 0), "norm_in")
    uv = _mm(hn0, full[("win", 0)], "nn", "gmlp_in", out_split=N_CHIPS)
    gm = _gmlp_fwd(uv, v_g, w_s, bias, "gmlp_gate")
    mix0 = _mm(gm, rows("wout"), "nn", "gmlp_out")[0]
    h1, hn1 = _resid_rms(h0, mix0, gains(post_mix_g, 0), [gains(pre_ffn_g, 0)], "resid_mix0")
    a0, hm0, f0 = ffn_fwd(hn1, 0)
    h2, hn2, kvn = _resid_rms(h1, f0, gains(post_ffn_g, 0), [gains(pre_mix_g, 1), kv_g], "resid_ffn0")
    q = _mm(hn2, rows("wq"), "nn", "proj_q", out_dtype=BF16)[0]
    k = _mm(kvn, rows("wk"), "nn", "proj_k", out_dtype=BF16)[0]
    v = _mm(kvn, rows("wv"), "nn", "proj_v", out_dtype=BF16)[0]
    att, lsum = _attn_fwd(q, k, v, "attn_fwd")
    mix1 = _mm(att, rows("wo"), "nn", "proj_o")[0]
    h3, hn3 = _resid_rms(h2, mix1, gains(post_mix_g, 1), [gains(pre_ffn_g, 1)], "resid_mix1")
    a1, hm1, f1 = ffn_fwd(hn3, 1)
    dh4, loss_tile = _loss_head(h3, f1, gains(post_ffn_g, 1), target, "loss_head")
    loss = lax.psum(loss_tile[0, 0], ("x", "y", "c"))

    dw = {}
    dg = {}

    def ffn_bwd(dh_out, h_in, hn, a, hm, f, layer):
        df, dg[("post_ffn", layer)] = _rms_bwd_out(dh_out, f, gains(post_ffn_g, layer), f"d_norm_ffn_out{layer}")
        dwd = _mm(hm, df, "tn", f"d_w_down{layer}", out_dtype=BF16)[0]
        dw[("wdn", layer)] = dwd.reshape(N_CHIPS, dwd.shape[0] // N_CHIPS, d)
        dhm = _mm(df, rows("wdn", layer), "nt", f"d_ffn_mid{layer}", out_split=2)
        da, dg[("conv_w", layer)], dg[("conv_b", layer)] = _ffn_act_bwd(
            a, dhm, conv_w[layer], conv_b[layer], f"d_ffn_act{layer}")
        dw[("wup", layer)] = _mm(hn, da, "tn", f"d_w_up{layer}", out_dtype=BF16, out_split=N_CHIPS)
        dhn = _mm(da, full[("wup", layer)], "nt", f"d_ffn_in{layer}")[0]
        return dhn

    dhn3 = ffn_bwd(dh4, h3, hn3, a1, hm1, f1, 1)
    dh3, (dg[("pre_ffn", 1)],) = _rms_bwd_in(dh4, h3, [([dhn3], gains(pre_ffn_g, 1))], "d_norm_ffn_in1")
    dmix1, dg[("post_mix", 1)] = _rms_bwd_out(dh3, mix1, gains(post_mix_g, 1), "d_norm_mix_out1")
    dwo = _mm(att, dmix1, "tn", "d_w_o", out_dtype=BF16)[0]
    dw[("wo", 0)] = dwo.reshape(N_CHIPS, dwo.shape[0] // N_CHIPS, d)
    datt = _mm(dmix1, rows("wo"), "nt", "d_attn_out", out_dtype=BF16)[0]
    dq, dk, dv = _attn_bwd(q, k, v, datt, lsum, "attn_bwd")
    for nm, act, dact in (("wq", hn2, dq), ("wk", kvn, dk), ("wv", kvn, dv)):
        g = _mm(act, dact, "tn", f"d_{nm}", out_dtype=BF16)[0]
        dw[(nm, 0)] = g.reshape(N_CHIPS, g.shape[0] // N_CHIPS, g.shape[1])
    dhn2 = _mm(dq, rows("wq"), "nt", "d_q_in")[0]
    dkvn_k = _mm(dk, rows("wk"), "nt", "d_k_in")[0]
    dkvn_v = _mm(dv, rows("wv"), "nt", "d_v_in")[0]
    dh2, (dg[("pre_mix", 1)], dg["kv"]) = _rms_bwd_in(
        dh3, h2, [([dhn2], gains(pre_mix_g, 1)), ([dkvn_k, dkvn_v], kv_g)], "d_norm_mix_in1")
    dhn1 = ffn_bwd(dh2, h1, hn1, a0, hm0, f0, 0)
    dh1, (dg[("pre_ffn", 0)],) = _rms_bwd_in(dh2, h1, [([dhn1], gains(pre_ffn_g, 0))], "d_norm_ffn_in0")
    dmix0, dg[("post_mix", 0)] = _rms_bwd_out(dh1, mix0, gains(post_mix_g, 0), "d_norm_mix_out0")
    dwout = _mm(gm, dmix0, "tn", "d_w_out", out_dtype=BF16)[0]
    dw[("wout", 0)] = dwout.reshape(N_CHIPS, dwout.shape[0] // N_CHIPS, d)
    dgm = _mm(dmix0, rows("wout"), "nt", "d_gmlp_gate")[0]
    duv, d_ws, d_bs, d_vg = _gmlp_bwd(uv, dgm, v_g, w_s, bias, "d_gmlp")
    dw[("win", 0)] = _mm(hn0, duv, "tn", "d_w_in", out_dtype=BF16, out_split=N_CHIPS)
    dhn0 = _mm(duv, full[("win", 0)], "nt", "d_gmlp_in")[0]
    dx, (dg[("pre_mix", 0)],) = _rms_bwd_in(dh1, h0, [([dhn0], gains(pre_mix_g, 0))], "d_norm_in")

    dws = [dw[u] for u in units]
    from_sibling = _swap_halves(dws, "grads_to_sibling")
    pair = [_pair_add(g, r, c_idx, f"pair_add_{nm}{layer}") for g, r, (nm, layer) in zip(dws, from_sibling, units)]
    at_owner = _scatter_to_owners(pair, "grads_to_owner")
    totals = [_chip_sum(p, f"chip_sum_{nm}{layer}") for p, (nm, layer) in zip(at_owner, units)]
    per_tensor = [[(layer, tot) for tot, (unm, layer) in zip(totals, units) if unm == nm] for nm in big]
    grads_big = dict(zip(big, _join_halves(per_tensor, [big[nm][0].shape for nm in big], "grads_join")))

    stack = lambda key: jnp.concatenate([dg[(key, layer)] for layer in range(n_layers)], axis=0)
    small_parts = [
        stack("pre_mix"), stack("post_mix"), stack("pre_ffn"), stack("post_ffn"),
        d_vg, d_ws, d_bs[::SUBLANE], dg["kv"],
        jnp.stack([dg[("conv_w", layer)] for layer in range(n_layers)]),
        jnp.stack([dg[("conv_b", layer)] for layer in range(n_layers)]),
    ]
    summed = _all_reduce_small(_pack(small_parts), "small_grads_sum")
    (g_pre_mix, g_post_mix, g_pre_ffn, g_post_ffn, g_vg, g_ws, g_bs, g_kv, g_cw, g_cb) = _unpack(
        summed, [p.shape for p in small_parts])
    g_vg = lax.dynamic_index_in_dim(g_vg.reshape(N_CHIPS, 1, d_a // N_CHIPS), chip, 0, keepdims=False)
    g_cw = lax.dynamic_index_in_dim(g_cw, chip, 1, keepdims=False)
    g_cb = g_cb.reshape(n_layers, N_CHIPS * ns)
    small = [
        (pre_mix_g, g_pre_mix, m_pre_mix_g, v_pre_mix_g),
        (post_mix_g, g_post_mix, m_post_mix_g, v_post_mix_g),
        (pre_ffn_g, g_pre_ffn, m_pre_ffn_g, v_pre_ffn_g),
        (post_ffn_g, g_post_ffn, m_post_ffn_g, v_post_ffn_g),
        (a_v_norm_g, g_vg, m_a_v_norm_g, v_a_v_norm_g),
        (a_w_spatial, g_ws[None], m_a_w_spatial, v_a_w_spatial),
        (a_b_spatial, g_bs[None], m_a_b_spatial, v_a_b_spatial),
        (kv_norm_g, g_kv.reshape(d), m_kv_norm_g, v_kv_norm_g),
        (ffn_conv_w, g_cw, m_ffn_conv_w, v_ffn_conv_w),
        (ffn_conv_b, g_cb, m_ffn_conv_b, v_ffn_conv_b),
    ]
    small = [(w, g.reshape(w.shape), m, v) for w, g, m, v in small]
    packed = [_pack([t[i] for t in small])[None] for i in range(4)]
    small_new = [_unpack(p[0], [t[0].shape for t in small]) for p in _adamw(*packed, "adamw_small")]

    new_big = {nm: _adamw(big[nm][0], grads_big[nm], big[nm][1], big[nm][2], f"adamw_{nm}") for nm in big}

    def big_out(nm, which):
        ref_shape = {"wk": w_k.shape, "wv": w_v.shape}.get(nm, big[nm][0].shape)
        arr = grads_big[nm] if which == 0 else new_big[nm][which - 1]
        return arr.reshape(ref_shape)

    order = ["pre_mix", "post_mix", "pre_ffn", "post_ffn", "win", "vg", "ws", "bs", "wout", "kv", "wk", "wv", "wq",
             "wo", "wup", "cw", "cb", "wdn"]
    small_at = {"pre_mix": 0, "post_mix": 1, "pre_ffn": 2, "post_ffn": 3, "vg": 4, "ws": 5, "bs": 6, "kv": 7,
                "cw": 8, "cb": 9}
    outs = [loss, dx[None]]
    for which in range(4):
        for nm in order:
            if nm in small_at:
                outs.append(small[small_at[nm]][1] if which == 0 else small_new[which - 1][small_at[nm]])
            else:
                outs.append(big_out(nm, which))
    return tuple(outs)
```

```python
import functools
import math

import jax
import jax.numpy as jnp
from jax import lax
from jax.experimental import pallas as pl
from jax.experimental.pallas import tpu as pltpu

F32 = jnp.float32
BF16 = jnp.bfloat16
EPS = 1e-6
ADAM_LR = 0.001
ADAM_B1 = 0.9
ADAM_B2 = 0.999
ADAM_EPS = 1e-08
ADAM_WD = 0.01
ADAM_STEP = 10

LANE = 128
SUBLANE = 8
ROWS = 16
TILE = 128
N_CHIPS = 4
N_DEV = 8
VMEM_LIMIT = 56 * 1024 * 1024
MM_VMEM = 40 * 1024 * 1024
MESH = pl.DeviceIdType.MESH
ANY = pl.BlockSpec(memory_space=pl.ANY)
VMEM_SPEC = pl.BlockSpec(memory_space=pltpu.VMEM)


def _cp(*sem):
    return pltpu.CompilerParams(dimension_semantics=sem, vmem_limit_bytes=VMEM_LIMIT)


def _pick(dim, pref, align=LANE):
    if dim <= pref:
        return dim
    best = None
    for d in range(align, pref + 1, align):
        if dim % d == 0:
            best = d
    assert best is not None, (dim, pref)
    return best


_DIMS = {
    "nn": (((1,), (0,)), ((), ())),
    "nt": (((1,), (1,)), ((), ())),
    "tn": (((0,), (0,)), ((), ())),
}


def _as3(a):
    return a if a.ndim == 3 else a[None]


def _spec3(br, bc, cols_j, rc):
    per = cols_j // bc

    def imap(m, n, k):
        r, c = rc(m, n, k)
        return (c // per, r, c % per)

    return pl.BlockSpec((None, br, bc), imap)


def _mm(a, b, mode, name, out_dtype=F32, out_split=1):
    a, b = _as3(a), _as3(b)
    ja, ra, caj = a.shape
    jb, rb, cbj = b.shape
    if mode == "nn":
        m, k, n = ra, ja * caj, jb * cbj
        assert rb == k
        m_ext, k_ext, n_ext = [ra], [caj, rb], [cbj]
    elif mode == "nt":
        m, k, n = ra, ja * caj, rb
        assert jb * cbj == k
        m_ext, k_ext, n_ext = [ra], [caj, cbj], [rb]
    else:
        m, k, n = ja * caj, ra, jb * cbj
        assert rb == k
        m_ext, k_ext, n_ext = [caj], [ra], [cbj]
    assert n % out_split == 0
    n_ext.append(n // out_split)
    bm = _pick(math.gcd(*m_ext), 1536)
    bn = _pick(math.gcd(*n_ext), 1536)
    k_unit = math.gcd(*k_ext)
    o_bytes = jnp.dtype(out_dtype).itemsize

    def vmem_need(bk):
        tiles = bm * bk * a.dtype.itemsize + bk * bn * b.dtype.itemsize + bm * bn * o_bytes
        return 2 * tiles + (bm * bn * 4 if bk < k else 0)

    bk = max(d for d in range(LANE, k_unit + 1, LANE) if k_unit % d == 0 and (d == LANE or vmem_need(d) <= MM_VMEM))
    nk = k // bk
    if mode == "nn":
        a_spec = _spec3(bm, bk, caj, lambda mi, ni, ki: (mi, ki))
        b_spec = _spec3(bk, bn, cbj, lambda mi, ni, ki: (ki, ni))
    elif mode == "nt":
        a_spec = _spec3(bm, bk, caj, lambda mi, ni, ki: (mi, ki))
        b_spec = _spec3(bn, bk, cbj, lambda mi, ni, ki: (ni, ki))
    else:
        a_spec = _spec3(bk, bm, caj, lambda mi, ni, ki: (ki, mi))
        b_spec = _spec3(bk, bn, cbj, lambda mi, ni, ki: (ki, ni))
    o_spec = _spec3(bm, bn, n // out_split, lambda mi, ni, ki: (mi, ni))
    dims = _DIMS[mode]

    def body(a_ref, b_ref, o_ref, *acc):
        def part():
            return lax.dot_general(a_ref[...].astype(BF16), b_ref[...].astype(BF16), dims, preferred_element_type=F32)

        if nk == 1:
            o_ref[...] = part().astype(o_ref.dtype)
            return
        acc_ref, = acc
        ki = pl.program_id(2)

        @pl.when(ki == 0)
        def _():
            acc_ref[...] = part()

        @pl.when(jnp.logical_and(ki > 0, ki < nk - 1))
        def _():
            acc_ref[...] += part()

        @pl.when(ki == nk - 1)
        def _():
            o_ref[...] = (acc_ref[...] + part()).astype(o_ref.dtype)

    return pl.pallas_call(
        body,
        name=name,
        grid=(m // bm, n // bn, nk),
        in_specs=[a_spec, b_spec],
        out_specs=o_spec,
        out_shape=jax.ShapeDtypeStruct((out_split, m, n // out_split), out_dtype),
        scratch_shapes=[pltpu.VMEM((bm, bn), F32)] if nk > 1 else [],
        compiler_params=_cp("parallel", "parallel", "arbitrary"),
    )(a, b)


def _rms(x, g):
    r = lax.rsqrt(jnp.mean(x * x, axis=-1, keepdims=True) + EPS)
    return x * r * g


def _rms_bwd(x, g, dy):
    r = lax.rsqrt(jnp.mean(x * x, axis=-1, keepdims=True) + EPS)
    xh = x * r
    gy = dy * g
    dx = r * (gy - xh * jnp.mean(gy * xh, axis=-1, keepdims=True))
    return dx, jnp.sum(dy * xh, axis=0, keepdims=True)


def _row_block(s):
    return _pick(s, 256, ROWS)


def _rms_fwd(h, g, name):
    s, d = h.shape
    br = _row_block(s)

    def body(h_ref, g_ref, o_ref):
        o_ref[...] = _rms(h_ref[...], g_ref[...]).astype(BF16)

    row = pl.BlockSpec((br, d), lambda i: (i, 0))
    vec = pl.BlockSpec((1, d), lambda i: (0, 0))
    return pl.pallas_call(
        body, name=name, grid=(s // br,), in_specs=[row, vec], out_specs=row,
        out_shape=jax.ShapeDtypeStruct((s, d), BF16), compiler_params=_cp("parallel"),
    )(h, g)


def _resid_rms(h_in, f, g_post, g_next, name):
    s, d = h_in.shape
    br = _row_block(s)
    n_next = len(g_next)

    def body(h_ref, f_ref, gp_ref, *refs):
        gn_refs, ho_ref, hn_refs = refs[:n_next], refs[n_next], refs[n_next + 1:]
        h = h_ref[...] + _rms(f_ref[...], gp_ref[...])
        ho_ref[...] = h
        for gn_ref, hn_ref in zip(gn_refs, hn_refs):
            hn_ref[...] = _rms(h, gn_ref[...]).astype(BF16)

    row = pl.BlockSpec((br, d), lambda i: (i, 0))
    vec = pl.BlockSpec((1, d), lambda i: (0, 0))
    return pl.pallas_call(
        body, name=name, grid=(s // br,),
        in_specs=[row, row, vec] + [vec] * n_next,
        out_specs=[row] * (1 + n_next),
        out_shape=[jax.ShapeDtypeStruct((s, d), F32)] + [jax.ShapeDtypeStruct((s, d), BF16)] * n_next,
        compiler_params=_cp("parallel"),
    )(h_in, f, g_post, *g_next)


def _loss_head(h_in, f, g_post, target, name):
    s, d = h_in.shape
    br = _row_block(s)

    def body(h_ref, f_ref, gp_ref, t_ref, dh_ref, loss_ref):
        @pl.when(pl.program_id(0) == 0)
        def _():
            loss_ref[...] = jnp.zeros_like(loss_ref)

        diff = h_ref[...] + _rms(f_ref[...], gp_ref[...]) - t_ref[...]
        dh_ref[...] = diff * (1.0 / d)
        loss_ref[...] += 0.5 * jnp.sum(jnp.mean(diff * diff, axis=-1, keepdims=True))

    row = pl.BlockSpec((br, d), lambda i: (i, 0))
    vec = pl.BlockSpec((1, d), lambda i: (0, 0))
    return pl.pallas_call(
        body, name=name, grid=(s // br,),
        in_specs=[row, row, vec, row],
        out_specs=[row, pl.BlockSpec((SUBLANE, LANE), lambda i: (0, 0))],
        out_shape=[jax.ShapeDtypeStruct((s, d), F32), jax.ShapeDtypeStruct((SUBLANE, LANE), F32)],
        compiler_params=_cp("arbitrary"),
    )(h_in, f, g_post, target)


def _rms_bwd_out(dy, f, g, name):
    s, d = f.shape
    br = _row_block(s)

    def body(dy_ref, f_ref, g_ref, df_ref, dg_ref):
        @pl.when(pl.program_id(0) == 0)
        def _():
            dg_ref[...] = jnp.zeros_like(dg_ref)

        dx, dg = _rms_bwd(f_ref[...], g_ref[...], dy_ref[...])
        df_ref[...] = dx.astype(BF16)
        dg_ref[...] += dg

    row = pl.BlockSpec((br, d), lambda i: (i, 0))
    vec = pl.BlockSpec((1, d), lambda i: (0, 0))
    return pl.pallas_call(
        body, name=name, grid=(s // br,), in_specs=[row, row, vec], out_specs=[row, vec],
        out_shape=[jax.ShapeDtypeStruct((s, d), BF16), jax.ShapeDtypeStruct((1, d), F32)],
        compiler_params=_cp("arbitrary"),
    )(dy, f, g)


def _rms_bwd_in(dh_out, h_in, branches, name):
    s, d = h_in.shape
    br = _row_block(s)
    counts = [len(ds) for ds, _ in branches]
    n_d = sum(counts)
    n_b = len(branches)

    def body(dho_ref, h_ref, *refs):
        d_refs, g_refs = refs[:n_d], refs[n_d:n_d + n_b]
        dh_ref, dg_refs = refs[n_d + n_b], refs[n_d + n_b + 1:]

        @pl.when(pl.program_id(0) == 0)
        def _():
            for r in dg_refs:
                r[...] = jnp.zeros_like(r)

        h = h_ref[...]
        acc = dho_ref[...]
        at = 0
        for bi, cnt in enumerate(counts):
            dn = d_refs[at][...]
            for r in d_refs[at + 1:at + cnt]:
                dn = dn + r[...]
            at += cnt
            dx, dg = _rms_bwd(h, g_refs[bi][...], dn)
            acc = acc + dx
            dg_refs[bi][...] += dg
        dh_ref[...] = acc

    row = pl.BlockSpec((br, d), lambda i: (i, 0))
    vec = pl.BlockSpec((1, d), lambda i: (0, 0))
    flat_d = [x for ds, _ in branches for x in ds]
    outs = pl.pallas_call(
        body, name=name, grid=(s // br,),
        in_specs=[row, row] + [row] * n_d + [vec] * n_b,
        out_specs=[row] + [vec] * n_b,
        out_shape=[jax.ShapeDtypeStruct((s, d), F32)] + [jax.ShapeDtypeStruct((1, d), F32)] * n_b,
        compiler_params=_cp("arbitrary"),
    )(dh_out, h_in, *flat_d, *[g for _, g in branches])
    return outs[0], list(outs[1:])


def _split3(x):
    x0 = x.astype(BF16)
    r1 = x - x0.astype(F32)
    x1 = r1.astype(BF16)
    x2 = (r1 - x1.astype(F32)).astype(BF16)
    return x0, x1, x2


def _tri(n, kind):
    r = lax.broadcasted_iota(jnp.int32, (n, n), 0)
    c = lax.broadcasted_iota(jnp.int32, (n, n), 1)
    m = {"lt": r < c, "le": r <= c, "gt": r > c}[kind]
    return jnp.where(m, 1.0, 0.0).astype(BF16)


_GELU_C = math.sqrt(2.0 / math.pi)
_GELU_A = 0.044715


def _gelu(x):
    return 0.5 * x * (1.0 + jnp.tanh(_GELU_C * (x + _GELU_A * (x * x * x))))


def _gelu_grad(x):
    t = jnp.tanh(_GELU_C * (x + _GELU_A * (x * x * x)))
    return 0.5 * (1.0 + t) + 0.5 * x * (1.0 - t * t) * (_GELU_C * (1.0 + 3.0 * _GELU_A * (x * x)))


def _causal_w(w):
    r = lax.broadcasted_iota(jnp.int32, (TILE, TILE), 0)
    c = lax.broadcasted_iota(jnp.int32, (TILE, TILE), 1)
    return jnp.where(c <= r, w, 0.0)


def _uv_tiles(uv_ref, g, d_a, dq):
    cu, cv = g * TILE, d_a + g * TILE
    u = uv_ref[cu // dq, :, pl.ds(cu % dq, TILE)]
    v = uv_ref[cv // dq, :, pl.ds(cv % dq, TILE)]
    return u, v


def _gmlp_fwd(uv, v_g, w_s, bias, name):
    _, s, dq = uv.shape
    d_a = 2 * dq
    n_g = d_a // TILE

    def body(uv_ref, vg_ref, ws_ref, b_ref, o_ref):
        for g in range(n_g):
            up, vp = _uv_tiles(uv_ref, g, d_a, dq)
            cols = pl.ds(g * TILE, TILE)
            vn = _rms(_gelu(vp), vg_ref[:, cols])
            mixed = jnp.dot(_causal_w(ws_ref[g]).astype(BF16), vn.astype(BF16), preferred_element_type=F32) + b_ref[:, cols]
            o_ref[:, cols] = (_gelu(up) * mixed).astype(BF16)

    return pl.pallas_call(
        body, name=name, grid=(s // TILE,),
        in_specs=[
            pl.BlockSpec((4, TILE, dq), lambda i: (0, i, 0)),
            pl.BlockSpec((1, d_a), lambda i: (0, 0)),
            pl.BlockSpec((n_g, TILE, TILE), lambda i: (0, 0, 0)),
            pl.BlockSpec((TILE, d_a), lambda i: (0, 0)),
        ],
        out_specs=pl.BlockSpec((TILE, d_a), lambda i: (i, 0)),
        out_shape=jax.ShapeDtypeStruct((s, d_a), BF16),
        compiler_params=_cp("parallel"),
    )(uv, v_g, w_s, bias)


def _gmlp_bwd(uv, dgm, v_g, w_s, bias, name):
    _, s, dq = uv.shape
    d_a = 2 * dq
    n_g = d_a // TILE
    n_c = s // TILE

    def body(uv_ref, d_ref, vg_ref, ws_ref, b_ref, duv_ref, dws_ref, dbs_ref, dvg_ref, dbias_acc):
        i = pl.program_id(0)

        @pl.when(i == 0)
        def _():
            dws_ref[...] = jnp.zeros_like(dws_ref)
            dvg_ref[...] = jnp.zeros_like(dvg_ref)
            dbias_acc[...] = jnp.zeros_like(dbias_acc)

        for g in range(n_g):
            up, vp = _uv_tiles(uv_ref, g, d_a, dq)
            cols = pl.ds(g * TILE, TILE)
            vg = vg_ref[:, cols]
            u = _gelu(up)
            v = _gelu(vp)
            r = lax.rsqrt(jnp.mean(v * v, axis=-1, keepdims=True) + EPS)
            vh = v * r
            vn = (vh * vg).astype(BF16)
            wc = _causal_w(ws_ref[g]).astype(BF16)
            mixed = jnp.dot(wc, vn, preferred_element_type=F32) + b_ref[:, cols]
            d_out = d_ref[:, cols]
            du = d_out * mixed
            dmixed = d_out * u
            dmb = dmixed.astype(BF16)
            dvn = lax.dot_general(wc, dmb, _DIMS["tn"], preferred_element_type=F32)
            dws_ref[g] += lax.dot_general(dmb, vn, _DIMS["nt"], preferred_element_type=F32)
            dbias_acc[:, cols] += dmixed
            dvg_ref[:, cols] += jnp.sum(dvn * vh, axis=0, keepdims=True)
            gv = dvn * vg
            dv = r * (gv - vh * jnp.mean(gv * vh, axis=-1, keepdims=True))
            cu, cv = g * TILE, d_a + g * TILE
            duv_ref[cu // dq, :, pl.ds(cu % dq, TILE)] = (du * _gelu_grad(up)).astype(BF16)
            duv_ref[cv // dq, :, pl.ds(cv % dq, TILE)] = (dv * _gelu_grad(vp)).astype(BF16)

        @pl.when(i == n_c - 1)
        def _():
            ones = jnp.ones((SUBLANE, TILE), BF16)
            for g in range(n_g):
                dws_ref[g] = _causal_w(dws_ref[g])
                cols = pl.ds(g * TILE, TILE)
                out = None
                for t in _split3(dbias_acc[:, cols]):
                    p = lax.dot_general(ones, t, _DIMS["nt"], preferred_element_type=F32)
                    out = p if out is None else out + p
                dbs_ref[pl.ds(g * SUBLANE, SUBLANE), :] = out

    return pl.pallas_call(
        body, name=name, grid=(n_c,),
        in_specs=[
            pl.BlockSpec((4, TILE, dq), lambda i: (0, i, 0)),
            pl.BlockSpec((TILE, d_a), lambda i: (i, 0)),
            pl.BlockSpec((1, d_a), lambda i: (0, 0)),
            pl.BlockSpec((n_g, TILE, TILE), lambda i: (0, 0, 0)),
            pl.BlockSpec((TILE, d_a), lambda i: (0, 0)),
        ],
        out_specs=[
            pl.BlockSpec((4, TILE, dq), lambda i: (0, i, 0)),
            pl.BlockSpec((n_g, TILE, TILE), lambda i: (0, 0, 0)),
            pl.BlockSpec((n_g * SUBLANE, TILE), lambda i: (0, 0)),
            pl.BlockSpec((1, d_a), lambda i: (0, 0)),
        ],
        out_shape=[
            jax.ShapeDtypeStruct((4, s, dq), BF16),
            jax.ShapeDtypeStruct((n_g, TILE, TILE), F32),
            jax.ShapeDtypeStruct((n_g * SUBLANE, TILE), F32),
            jax.ShapeDtypeStruct((1, d_a), F32),
        ],
        scratch_shapes=[pltpu.VMEM((TILE, d_a), F32)],
        compiler_params=_cp("arbitrary"),
    )(uv, dgm, v_g, w_s, bias)


def _sigmoid(x):
    return 1.0 / (1.0 + jnp.exp(-x))


def _conv3(ext, w, b):
    return b + ((w[0:1] * pltpu.roll(ext, 2, 0) + w[1:2] * pltpu.roll(ext, 1, 0)) + w[2:3] * ext)


def _act_blocks(s, ns):
    return _pick(s, 512, ROWS), _pick(ns, 256)


def _ffn_act_fwd(a, cw, cb, name):
    _, s, ns = a.shape
    bs, cb_w = _act_blocks(s, ns)
    hb = bs // SUBLANE

    def body(a_ref, prev_ref, cw_ref, cb_ref, o_ref):
        first = pl.program_id(0) == 0

        def conv(comp):
            prev = jnp.where(first, 0.0, prev_ref[comp])
            ext = jnp.concatenate([prev, a_ref[comp]], axis=0)
            return _conv3(ext, cw_ref[comp], cb_ref[comp])[SUBLANE:]

        for p in range(2):
            cg = conv(p)
            o_ref[p] = (cg * _sigmoid(cg) * conv(2 + p)).astype(BF16)

    return pl.pallas_call(
        body, name=name, grid=(s // bs, ns // cb_w),
        in_specs=[
            pl.BlockSpec((4, bs, cb_w), lambda i, j: (0, i, j)),
            pl.BlockSpec((4, SUBLANE, cb_w), lambda i, j: (0, jnp.maximum(i * hb - 1, 0), j)),
            pl.BlockSpec((4, 3, cb_w), lambda i, j: (0, 0, j)),
            pl.BlockSpec((4, 1, cb_w), lambda i, j: (0, 0, j)),
        ],
        out_specs=pl.BlockSpec((2, bs, cb_w), lambda i, j: (0, i, j)),
        out_shape=jax.ShapeDtypeStruct((2, s, ns), BF16),
        compiler_params=_cp("parallel", "parallel"),
    )(a, a, cw, cb)


def _ffn_act_bwd(a, dhm, cw, cb, name):
    _, s, ns = a.shape
    bs, cb_w = _act_blocks(s, ns)
    hb = bs // SUBLANE
    n_i = s // bs
    n_ext = bs + 2 * SUBLANE
    cur = slice(SUBLANE, SUBLANE + bs)

    def body(a_ref, prev_ref, next_ref, d_ref, dnext_ref, cw_ref, cb_ref, da_ref, dcw_ref, dcb_ref):
        i = pl.program_id(1)
        first, last = i == 0, i == n_i - 1

        @pl.when(first)
        def _():
            dcw_ref[...] = jnp.zeros_like(dcw_ref)
            dcb_ref[...] = jnp.zeros_like(dcb_ref)

        def ext_of(comp):
            return jnp.concatenate([jnp.where(first, 0.0, prev_ref[comp]), a_ref[comp], next_ref[comp]], axis=0)

        def back(comp, a_ext, dc):
            w = cw_ref[comp]
            da = (w[2:3] * dc + w[1:2] * pltpu.roll(dc, n_ext - 1, 0)) + w[0:1] * pltpu.roll(dc, n_ext - 2, 0)
            da_ref[comp] = da[cur].astype(BF16)
            dcc = dc[cur]
            dcw_ref[comp, 0:1, :] += jnp.sum(dcc * pltpu.roll(a_ext, 2, 0)[cur], axis=0, keepdims=True)
            dcw_ref[comp, 1:2, :] += jnp.sum(dcc * pltpu.roll(a_ext, 1, 0)[cur], axis=0, keepdims=True)
            dcw_ref[comp, 2:3, :] += jnp.sum(dcc * a_ext[cur], axis=0, keepdims=True)
            dcb_ref[comp] += jnp.sum(dcc, axis=0, keepdims=True)

        for p in range(2):
            ag, av = ext_of(p), ext_of(2 + p)
            cg = _conv3(ag, cw_ref[p], cb_ref[p])
            cv = _conv3(av, cw_ref[2 + p], cb_ref[2 + p])
            d = jnp.concatenate(
                [jnp.zeros((SUBLANE, cb_w), F32), d_ref[p], jnp.where(last, 0.0, dnext_ref[p])], axis=0)
            sg = _sigmoid(cg)
            back(2 + p, av, d * (cg * sg))
            back(p, ag, d * cv * (sg * (1.0 + cg * (1.0 - sg))))

    return pl.pallas_call(
        body, name=name, grid=(ns // cb_w, n_i),
        in_specs=[
            pl.BlockSpec((4, bs, cb_w), lambda j, i: (0, i, j)),
            pl.BlockSpec((4, SUBLANE, cb_w), lambda j, i: (0, jnp.maximum(i * hb - 1, 0), j)),
            pl.BlockSpec((4, SUBLANE, cb_w), lambda j, i: (0, jnp.minimum((i + 1) * hb, n_i * hb - 1), j)),
            pl.BlockSpec((2, bs, cb_w), lambda j, i: (0, i, j)),
            pl.BlockSpec((2, SUBLANE, cb_w), lambda j, i: (0, jnp.minimum((i + 1) * hb, n_i * hb - 1), j)),
            pl.BlockSpec((4, 3, cb_w), lambda j, i: (0, 0, j)),
            pl.BlockSpec((4, 1, cb_w), lambda j, i: (0, 0, j)),
        ],
        out_specs=[
            pl.BlockSpec((4, bs, cb_w), lambda j, i: (0, i, j)),
            pl.BlockSpec((4, 3, cb_w), lambda j, i: (0, 0, j)),
            pl.BlockSpec((4, 1, cb_w), lambda j, i: (0, 0, j)),
        ],
        out_shape=[
            jax.ShapeDtypeStruct((4, s, ns), BF16),
            jax.ShapeDtypeStruct((4, 3, ns), F32),
            jax.ShapeDtypeStruct((4, 1, ns), F32),
        ],
        compiler_params=_cp("parallel", "arbitrary"),
    )(a, a, a, dhm, dhm, cw, cb)


ATT_BQ = 512
ATT_BK = 256


def _att_blocks(s):
    bq = _pick(s, ATT_BQ)
    bk = min(ATT_BK, bq)
    assert bq % bk == 0
    return bq, bk


def _dot_sel2(x, sel):
    hi = x.astype(BF16)
    lo = (x - hi.astype(F32)).astype(BF16)
    n = x.shape[0]
    both = jnp.dot(jnp.concatenate([hi, lo], axis=0), sel, preferred_element_type=F32)
    return both[:n] + both[n:]


def _causal_mask(bq, bk, row0, col0):
    rows = row0 + lax.broadcasted_iota(jnp.int32, (bq, bk), 0)
    cols = col0 + lax.broadcasted_iota(jnp.int32, (bq, bk), 1)
    return cols < rows


def _sb_tile(qb, kb, scale, mask):
    z = lax.dot_general(qb, kb, _DIMS["nt"], preferred_element_type=F32) * scale
    e = jnp.exp(-jnp.abs(z))
    lb = jnp.minimum(z, 0.0) - jnp.log(1.0 + e)
    l1m = lb - z
    if mask is not None:
        l1m = jnp.where(mask, l1m, 0.0)
    return z, e, lb, l1m


def _attn_fwd(q, k, v, name):
    s, hd = q.shape
    bq, bk = _att_blocks(s)
    r = bq // bk
    n_h, n_q = hd // TILE, s // bq
    scale = 1.0 / math.sqrt(TILE)

    def body(q_ref, k_ref, v_ref, o_ref, l_ref, acc_ref, suf_ref):
        i = pl.program_id(1)
        qb = q_ref[...]
        later = _tri(bk, "gt")
        acc_ref[...] = jnp.zeros_like(acc_ref)
        suf_ref[...] = jnp.zeros_like(suf_ref)

        def tile(j, masked):
            rows = pl.ds(pl.multiple_of(j * bk, bk), bk)
            mask = _causal_mask(bq, bk, i * bq, j * bk) if masked else None
            _, _, lb, l1m = _sb_tile(qb, k_ref[rows, :], scale, mask)
            a = jnp.exp(lb + _dot_sel2(l1m, later) + suf_ref[...])
            if masked:
                a = jnp.where(mask, a, 0.0)
            acc_ref[...] += jnp.dot(a.astype(BF16), v_ref[rows, :], preferred_element_type=F32)
            suf_ref[...] += jnp.sum(l1m, axis=1, keepdims=True)

        for dgl in range(r - 1, -1, -1):
            tile(r * i + dgl, True)

        def step(t, carry):
            tile(r * i - 1 - t, False)
            return carry

        lax.fori_loop(0, r * i, step, 0)
        o_ref[...] = acc_ref[...].astype(BF16)
        l_ref[...] = jnp.broadcast_to(suf_ref[...], (bq, TILE))

    blk = pl.BlockSpec((bq, TILE), lambda h, i: (i, h))
    head = pl.BlockSpec((s, TILE), lambda h, i: (0, h))
    return pl.pallas_call(
        body, name=name, grid=(n_h, n_q), in_specs=[blk, head, head], out_specs=[blk, blk],
        out_shape=[jax.ShapeDtypeStruct((s, hd), BF16), jax.ShapeDtypeStruct((s, hd), F32)],
        scratch_shapes=[pltpu.VMEM((bq, TILE), F32), pltpu.VMEM((bq, 1), F32)],
        compiler_params=_cp("parallel", "parallel"),
    )(q, k, v)


def _attn_bwd(q, k, v, do, lsum, name):
    s, hd = q.shape
    bq, bk = _att_blocks(s)
    r = bq // bk
    n_h, n_q = hd // TILE, s // bq
    scale = 1.0 / math.sqrt(TILE)

    def body(q_ref, k_ref, v_ref, do_ref, l_ref, dq_ref, dk_ref, dv_ref, dq_acc, pre_ref, cp_ref):
        i = pl.program_id(1)

        @pl.when(i == 0)
        def _():
            dk_ref[...] = jnp.zeros_like(dk_ref)
            dv_ref[...] = jnp.zeros_like(dv_ref)

        qb = q_ref[...]
        dob = do_ref[...]
        upto = _tri(bk, "le")
        before = _tri(bk, "lt")
        dq_acc[...] = jnp.zeros_like(dq_acc)
        pre_ref[...] = jnp.zeros_like(pre_ref)
        cp_ref[...] = jnp.zeros_like(cp_ref)

        def tile(j, masked):
            rows = pl.ds(pl.multiple_of(j * bk, bk), bk)
            kb, vb = k_ref[rows, :], v_ref[rows, :]
            mask = _causal_mask(bq, bk, i * bq, j * bk) if masked else None
            z, e, lb, l1m = _sb_tile(qb, kb, scale, mask)
            suffix = (l_ref[:, 0:1] - pre_ref[...]) - _dot_sel2(l1m, upto)
            a = jnp.exp(lb + suffix)
            if masked:
                a = jnp.where(mask, a, 0.0)
            p = a * lax.dot_general(dob, vb, _DIMS["nt"], preferred_element_type=F32)
            cprev = cp_ref[...] + _dot_sel2(p, before)
            inv = pl.reciprocal(1.0 + e, approx=True)
            pos = z >= 0.0
            dz = p * (jnp.where(pos, e, 1.0) * inv) - cprev * (jnp.where(pos, 1.0, e) * inv)
            if masked:
                dz = jnp.where(mask, dz, 0.0)
            dz = (dz * scale).astype(BF16)
            dq_acc[...] += jnp.dot(dz, kb, preferred_element_type=F32)
            dk_ref[rows, :] += lax.dot_general(dz, qb, _DIMS["tn"], preferred_element_type=F32)
            dv_ref[rows, :] += lax.dot_general(a.astype(BF16), dob, _DIMS["tn"], preferred_element_type=F32)
            pre_ref[...] += jnp.sum(l1m, axis=1, keepdims=True)
            cp_ref[...] += jnp.sum(p, axis=1, keepdims=True)

        def step(j, carry):
            tile(j, False)
            return carry

        lax.fori_loop(0, r * i, step, 0)
        for dgl in range(r):
            tile(r * i + dgl, True)
        dq_ref[...] = dq_acc[...].astype(BF16)

    blk = pl.BlockSpec((bq, TILE), lambda h, i: (i, h))
    head = pl.BlockSpec((s, TILE), lambda h, i: (0, h))
    return pl.pallas_call(
        body, name=name, grid=(n_h, n_q), in_specs=[blk, head, head, blk, blk], out_specs=[blk, head, head],
        out_shape=[jax.ShapeDtypeStruct((s, hd), BF16), jax.ShapeDtypeStruct((s, hd), F32),
                   jax.ShapeDtypeStruct((s, hd), F32)],
        scratch_shapes=[pltpu.VMEM((bq, TILE), F32), pltpu.VMEM((bq, 1), F32), pltpu.VMEM((bq, 1), F32)],
        compiler_params=_cp("parallel", "arbitrary"),
    )(q, k, v, do, lsum)


def _cast_bf16(w, layer, chip_idx, name):
    _, r, c = w.shape
    br, bc = _pick(r, 256, ROWS), _pick(c, 1024)

    def body(chip_ref, w_ref, o_ref):
        o_ref[...] = w_ref[...].astype(BF16)

    return pl.pallas_call(
        body, name=name,
        grid_spec=pltpu.PrefetchScalarGridSpec(
            num_scalar_prefetch=1, grid=(r // br, c // bc),
            in_specs=[pl.BlockSpec((None, br, bc), lambda i, j, chip_ref: (layer, i, j))],
            out_specs=pl.BlockSpec((None, br, bc), lambda i, j, chip_ref: (chip_ref[0], i, j)),
        ),
        out_shape=jax.ShapeDtypeStruct((N_CHIPS, r, c), BF16), compiler_params=_cp("parallel", "parallel"),
    )(chip_idx, w)


def _pair_add(dw, recv, c_idx, name):
    _, r, c = dw.shape
    hr = r // 2
    br, bc = _pick(hr, 256, ROWS), _pick(c, 1024)
    nb = hr // br

    def body(c_ref, a_ref, b_ref, o_ref):
        o_ref[...] = (a_ref[...].astype(F32) + b_ref[...].astype(F32)).astype(BF16)

    return pl.pallas_call(
        body, name=name,
        grid_spec=pltpu.PrefetchScalarGridSpec(
            num_scalar_prefetch=1, grid=(N_CHIPS, nb, c // bc),
            in_specs=[
                pl.BlockSpec((None, br, bc), lambda s, i, j, c_ref: (s, c_ref[0] * nb + i, j)),
                pl.BlockSpec((None, br, bc), lambda s, i, j, c_ref: (s, i, j)),
            ],
            out_specs=pl.BlockSpec((None, br, bc), lambda s, i, j, c_ref: (s, i, j)),
        ),
        out_shape=jax.ShapeDtypeStruct((N_CHIPS, hr, c), BF16),
        compiler_params=_cp("parallel", "parallel", "parallel"),
    )(c_idx, dw, recv)


def _chip_sum(parts, dest, shape, layer, c_idx, name):
    _, hr, c = parts.shape
    br, bc = _pick(hr, 256, ROWS), _pick(c, 1024)
    nb = hr // br

    def body(c_ref, p_ref, *refs):
        o_ref = refs[-1]
        acc = p_ref[0].astype(F32)
        for s in range(1, N_CHIPS):
            acc = acc + p_ref[s].astype(F32)
        o_ref[...] = acc

    in_specs = [pl.BlockSpec((N_CHIPS, br, bc), lambda i, j, c_ref: (0, i, j))]
    operands = [c_idx, parts]
    aliases = {}
    if dest is not None:
        in_specs.append(ANY)
        operands.append(dest)
        aliases = {2: 0}
    return pl.pallas_call(
        body, name=name,
        grid_spec=pltpu.PrefetchScalarGridSpec(
            num_scalar_prefetch=1, grid=(nb, c // bc), in_specs=in_specs,
            out_specs=pl.BlockSpec((None, br, bc), lambda i, j, c_ref: (layer, c_ref[0] * nb + i, j)),
        ),
        out_shape=jax.ShapeDtypeStruct(shape, F32), input_output_aliases=aliases,
        compiler_params=_cp("parallel", "parallel"),
    )(*operands)


def _adamw(w, g, m, v, name):
    n_l, r, c = w.shape
    br, bc = _pick(r, 256, ROWS), _pick(c, 1024)

    def body(w_ref, g_ref, m_ref, v_ref, d_ref, mo_ref, vo_ref):
        g = g_ref[...]
        m = ADAM_B1 * m_ref[...] + (1.0 - ADAM_B1) * g
        v = ADAM_B2 * v_ref[...] + (1.0 - ADAM_B2) * (g * g)
        m_hat = m / (1.0 - ADAM_B1 ** ADAM_STEP)
        v_hat = v / (1.0 - ADAM_B2 ** ADAM_STEP)
        d_ref[...] = -ADAM_LR * (m_hat / (jnp.sqrt(v_hat) + ADAM_EPS) + ADAM_WD * w_ref[...])
        mo_ref[...] = m
        vo_ref[...] = v

    blk = pl.BlockSpec((None, br, bc), lambda l, i, j: (l, i, j))
    return pl.pallas_call(
        body, name=name, grid=(n_l, r // br, c // bc), in_specs=[blk] * 4, out_specs=[blk] * 3,
        out_shape=[jax.ShapeDtypeStruct(w.shape, F32)] * 3, compiler_params=_cp("parallel", "parallel", "parallel"),
    )(w, g, m, v)


def _place():
    x, y, c = lax.axis_index("x"), lax.axis_index("y"), lax.axis_index("c")
    chips = [(1 - x, y), (x, 1 - y), (1 - x, 1 - y)]
    return x, y, c, chips


def _gather_shards(slots, name):
    n = len(slots)
    halves = [a.shape[1] // 2 for a in slots]
    for a, hr in zip(slots, halves):
        assert a.shape[1] == 2 * hr and hr % (2 * SUBLANE) == 0, a.shape

    def body(*refs):
        bufs = refs[n:2 * n]
        send_sems, recv_sems = refs[2 * n:]
        x, y, c, chips = _place()
        me = 2 * x + y

        def remote(i, k, slot, core, to):
            rows = bufs[i].at[slot, pl.ds(pl.multiple_of(core * halves[i], 2 * SUBLANE), halves[i])]
            return pltpu.make_async_remote_copy(
                src_ref=rows, dst_ref=rows, send_sem=send_sems.at[i * 6 + k], recv_sem=recv_sems.at[i * 6 + k],
                device_id=to, device_id_type=MESH)

        sends = [remote(i, k, me, c, (px, py, c)) for i in range(n) for k, (px, py) in enumerate(chips)]
        for cp in sends:
            cp.start()
        passed = []
        for i in range(n):
            for k, (px, py) in enumerate(chips):
                remote(i, k, 2 * px + py, c, (x, y, c)).wait_recv()
                fwd = remote(i, 3 + k, 2 * px + py, c, (x, y, 1 - c))
                fwd.start()
                passed.append(fwd)
        for i in range(n):
            for k, (px, py) in enumerate(chips):
                remote(i, 3 + k, 2 * px + py, 1 - c, (x, y, c)).wait_recv()
        for cp in sends + passed:
            cp.wait_send()

    return pl.pallas_call(
        body, name=name,
        in_specs=[ANY] * n, out_specs=[ANY] * n,
        out_shape=[jax.ShapeDtypeStruct(a.shape, a.dtype) for a in slots],
        input_output_aliases={i: i for i in range(n)},
        scratch_shapes=[pltpu.SemaphoreType.DMA((6 * n,)), pltpu.SemaphoreType.DMA((6 * n,))],
    )(*slots)


def _swap_halves(grads, name):
    n = len(grads)
    halves = [a.shape[1] // 2 for a in grads]

    def body(*refs):
        ins, outs = refs[:n], refs[n:2 * n]
        send_sems, recv_sems = refs[2 * n:]
        x, y, c, _ = _place()
        copies = []
        for i in range(n):
            rows = pl.ds(pl.multiple_of((1 - c) * halves[i], 2 * SUBLANE), halves[i])
            copies.append(pltpu.make_async_remote_copy(
                src_ref=ins[i].at[:, rows, :], dst_ref=outs[i], send_sem=send_sems.at[i], recv_sem=recv_sems.at[i],
                device_id=(x, y, 1 - c), device_id_type=MESH))
        for cp in copies:
            cp.start()
        for cp in copies:
            cp.wait()

    return pl.pallas_call(
        body, name=name, in_specs=[ANY] * n, out_specs=[ANY] * n,
        out_shape=[jax.ShapeDtypeStruct((N_CHIPS, hr, a.shape[2]), a.dtype) for a, hr in zip(grads, halves)],
        scratch_shapes=[pltpu.SemaphoreType.DMA((n,)), pltpu.SemaphoreType.DMA((n,))],
    )(*grads)


def _scatter_to_owners(parts, name):
    n = len(parts)

    def body(*refs):
        ins, outs = refs[:n], refs[n:2 * n]
        send_sems, recv_sems, local_sems = refs[2 * n:]
        x, y, c, chips = _place()
        me = 2 * x + y
        own = [pltpu.make_async_copy(ins[i].at[me], outs[i].at[me], local_sems.at[i]) for i in range(n)]
        for cp in own:
            cp.start()
        sends = []
        for i in range(n):
            for k, (px, py) in enumerate(chips):
                sends.append(pltpu.make_async_remote_copy(
                    src_ref=ins[i].at[2 * px + py], dst_ref=outs[i].at[me],
                    send_sem=send_sems.at[3 * i + k], recv_sem=recv_sems.at[3 * i + k],
                    device_id=(px, py, c), device_id_type=MESH))
        for cp in sends:
            cp.start()
        for i in range(n):
            for k, (px, py) in enumerate(chips):
                slot = outs[i].at[2 * px + py]
                pltpu.make_async_remote_copy(
                    src_ref=slot, dst_ref=slot, send_sem=send_sems.at[3 * i + k], recv_sem=recv_sems.at[3 * i + k],
                    device_id=(x, y, c), device_id_type=MESH).wait_recv()
        for cp in sends:
            cp.wait_send()
        for cp in own:
            cp.wait()

    return pl.pallas_call(
        body, name=name, in_specs=[ANY] * n, out_specs=[ANY] * n,
        out_shape=[jax.ShapeDtypeStruct(a.shape, a.dtype) for a in parts],
        scratch_shapes=[pltpu.SemaphoreType.DMA((3 * n,)), pltpu.SemaphoreType.DMA((3 * n,)),
                        pltpu.SemaphoreType.DMA((n,))],
    )(*parts)


def _join_halves(grads, name):
    n = len(grads)

    def body(*refs):
        bufs = refs[n:2 * n]
        send_sems, recv_sems = refs[2 * n:]
        x, y, c, _ = _place()

        def half(i, core):
            hr = grads[i].shape[1] // 2
            return bufs[i].at[:, pl.ds(pl.multiple_of(core * hr, SUBLANE), hr), :]

        def copy(i, core, to):
            return pltpu.make_async_remote_copy(
                src_ref=half(i, core), dst_ref=half(i, core), send_sem=send_sems.at[i], recv_sem=recv_sems.at[i],
                device_id=to, device_id_type=MESH)

        sends = [copy(i, c, (x, y, 1 - c)) for i in range(n)]
        for cp in sends:
            cp.start()
        for i in range(n):
            copy(i, 1 - c, (x, y, c)).wait_recv()
        for cp in sends:
            cp.wait_send()

    return pl.pallas_call(
        body, name=name, in_specs=[ANY] * n, out_specs=[ANY] * n,
        out_shape=[jax.ShapeDtypeStruct(g.shape, F32) for g in grads],
        input_output_aliases={i: i for i in range(n)},
        scratch_shapes=[pltpu.SemaphoreType.DMA((n,)), pltpu.SemaphoreType.DMA((n,))],
    )(*grads)


def _all_reduce_small(packed, name):
    r, c = packed.shape
    chunk = _pick(r, 256, ROWS)

    def body(x_ref, out_ref, gath, send_sems, recv_sems, local_sem):
        x, y, cc, chips = _place()
        me, sibling = (x, y, cc), (x, y, 1 - cc)

        def slot(px, py, pc):
            return gath.at[4 * px + 2 * py + pc]

        def copy(k, block, to, src=None):
            return pltpu.make_async_remote_copy(
                src_ref=slot(*block) if src is None else src, dst_ref=slot(*block),
                send_sem=send_sems.at[k], recv_sem=recv_sems.at[k], device_id=to, device_id_type=MESH)

        mine = pltpu.make_async_copy(x_ref, slot(*me), local_sem)
        mine.start()
        first = [copy(0, me, sibling, src=x_ref)]
        first += [copy(1 + j, me, (*chip, cc), src=x_ref) for j, chip in enumerate(chips)]
        for cp in first:
            cp.start()
        passed = [copy(4 + j, (*chip, cc), sibling) for j, chip in enumerate(chips)]
        for j, chip in enumerate(chips):
            copy(1 + j, (*chip, cc), me).wait_recv()
            passed[j].start()
        copy(0, sibling, me).wait_recv()
        for j, chip in enumerate(chips):
            copy(4 + j, (*chip, 1 - cc), me).wait_recv()
        for cp in first + passed:
            cp.wait_send()
        mine.wait()

        def add(i, carry):
            rows = pl.ds(pl.multiple_of(i * chunk, SUBLANE), chunk)
            acc = gath[0, rows, :]
            for dev in range(1, N_DEV):
                acc = acc + gath[dev, rows, :]
            out_ref[rows, :] = acc
            return carry

        lax.fori_loop(0, r // chunk, add, 0)

    return pl.pallas_call(
        body, name=name, in_specs=[VMEM_SPEC], out_specs=VMEM_SPEC,
        out_shape=jax.ShapeDtypeStruct((r, c), F32),
        scratch_shapes=[pltpu.VMEM((N_DEV, r, c), F32), pltpu.SemaphoreType.DMA((7,)),
                        pltpu.SemaphoreType.DMA((7,)), pltpu.SemaphoreType.DMA],
        compiler_params=pltpu.CompilerParams(vmem_limit_bytes=VMEM_LIMIT),
    )(packed)


_PACK_ROWS = 256


def _pack(arrays):
    flat = jnp.concatenate([a.reshape(-1).astype(F32) for a in arrays])
    unit = _PACK_ROWS * LANE
    total = -(-flat.shape[0] // unit) * unit
    return jnp.pad(flat, (0, total - flat.shape[0])).reshape(-1, LANE)


def _unpack(packed, shapes, lead=()):
    flat = packed.reshape(lead + (-1,))
    out, at = [], 0
    for s in shapes:
        size = math.prod(s)
        out.append(flat[..., at:at + size].reshape(lead + tuple(s)))
        at += size
    return out


def kernel(x, pre_mix_g, post_mix_g, pre_ffn_g, post_ffn_g, a_w_in, a_v_norm_g, a_w_spatial, a_b_spatial, a_w_out, kv_norm_g, w_k, w_v, b_w_q, b_w_o, ffn_w_up, ffn_conv_w, ffn_conv_b, ffn_w_down, loss_target, m_pre_mix_g, m_post_mix_g, m_pre_ffn_g, m_post_ffn_g, m_a_w_in, m_a_v_norm_g, m_a_w_spatial, m_a_b_spatial, m_a_w_out, m_kv_norm_g, m_w_k, m_w_v, m_b_w_q, m_b_w_o, m_ffn_w_up, m_ffn_conv_w, m_ffn_conv_b, m_ffn_w_down, v_pre_mix_g, v_post_mix_g, v_pre_ffn_g, v_post_ffn_g, v_a_w_in, v_a_v_norm_g, v_a_w_spatial, v_a_b_spatial, v_a_w_out, v_kv_norm_g, v_w_k, v_w_v, v_b_w_q, v_b_w_o, v_ffn_w_up, v_ffn_conv_w, v_ffn_conv_b, v_ffn_w_down):
    xi, yi, ci = lax.axis_index("x"), lax.axis_index("y"), lax.axis_index("c")
    chip = 2 * xi + yi
    c_idx = jnp.reshape(ci, (1,)).astype(jnp.int32)
    _, s, d = x.shape
    n_layers = pre_mix_g.shape[0]
    assert n_layers == 2 and a_w_in.shape[0] == 1 and b_w_q.shape[0] == 1
    d_a = a_w_out.shape[1] * N_CHIPS
    n_g = a_w_spatial.shape[1]
    ns = ffn_w_up.shape[2]
    assert a_w_spatial.shape[2] == TILE and d_a == n_g * TILE and s % TILE == 0
    h0 = x[0]
    target = loss_target[0]

    big = {
        "win": (a_w_in, m_a_w_in, v_a_w_in),
        "wout": (a_w_out, m_a_w_out, v_a_w_out),
        "wk": (w_k[None], m_w_k[None], v_w_k[None]),
        "wv": (w_v[None], m_w_v[None], v_w_v[None]),
        "wq": (b_w_q, m_b_w_q, v_b_w_q),
        "wo": (b_w_o, m_b_w_o, v_b_w_o),
        "wup": (ffn_w_up, m_ffn_w_up, v_ffn_w_up),
        "wdn": (ffn_w_down, m_ffn_w_down, v_ffn_w_down),
    }
    units = [(nm, layer) for nm in big for layer in range(big[nm][0].shape[0])]
    chip_idx = jnp.reshape(chip, (1,)).astype(jnp.int32)
    shards = [_cast_bf16(big[nm][0], layer, chip_idx, f"cast_{nm}{layer}") for nm, layer in units]
    small_sharded = _pack([a_v_norm_g, ffn_conv_w])
    small_sharded = lax.dynamic_update_index_in_dim(
        jnp.zeros((N_CHIPS,) + small_sharded.shape, F32), small_sharded, chip, 0)
    gathered = _gather_shards(shards + [small_sharded], "gather_weights")
    full = dict(zip(units, gathered[:-1]))
    vg_parts, cw_parts = _unpack(gathered[-1], [a_v_norm_g.shape, ffn_conv_w.shape], lead=(N_CHIPS,))
    v_g = jnp.transpose(vg_parts, (1, 0, 2)).reshape(1, d_a)

    def rows(nm, layer=0):
        w = full[(nm, layer)]
        return w.reshape(w.shape[0] * w.shape[1], w.shape[2])

    gains = lambda g, layer: g[layer:layer + 1]
    bias = jnp.repeat(a_b_spatial[0].T, TILE, axis=1)
    w_s = a_w_spatial[0]
    kv_g = kv_norm_g[None]
    conv_w = [cw_parts[:, layer] for layer in range(n_layers)]
    conv_b = [ffn_conv_b[layer].reshape(N_CHIPS, 1, ns) for layer in range(n_layers)]

    def ffn_fwd(hn, layer):
        a = _mm(hn, full[("wup", layer)], "nn", f"ffn_up{layer}", out_split=N_CHIPS)
        hm = _ffn_act_fwd(a, conv_w[layer], conv_b[layer], f"ffn_act{layer}")
        return a, hm, _mm(hm, rows("wdn", layer), "nn", f"ffn_down{layer}")[0]

    hn0 = _rms_fwd(h0, gains(pre_mix_g, 0), "norm_in")
    uv = _mm(hn0, full[("win", 0)], "nn", "gmlp_in", out_split=N_CHIPS)
    gm = _gmlp_fwd(uv, v_g, w_s, bias, "gmlp_gate")
    mix0 = _mm(gm, rows("wout"), "nn", "gmlp_out")[0]
    h1, hn1 = _resid_rms(h0, mix0, gains(post_mix_g, 0), [gains(pre_ffn_g, 0)], "resid_mix0")
    a0, hm0, f0 = ffn_fwd(hn1, 0)
    h2, hn2, kvn = _resid_rms(h1, f0, gains(post_ffn_g, 0), [gains(pre_mix_g, 1), kv_g], "resid_ffn0")
    q = _mm(hn2, rows("wq"), "nn", "proj_q", out_dtype=BF16)[0]
    k = _mm(kvn, rows("wk"), "nn", "proj_k", out_dtype=BF16)[0]
    v = _mm(kvn, rows("wv"), "nn", "proj_v", out_dtype=BF16)[0]
    att, lsum = _attn_fwd(q, k, v, "attn_fwd")
    mix1 = _mm(att, rows("wo"), "nn", "proj_o")[0]
    h3, hn3 = _resid_rms(h2, mix1, gains(post_mix_g, 1), [gains(pre_ffn_g, 1)], "resid_mix1")
    a1, hm1, f1 = ffn_fwd(hn3, 1)
    dh4, loss_tile = _loss_head(h3, f1, gains(post_ffn_g, 1), target, "loss_head")
    loss = lax.psum(loss_tile[0, 0], ("x", "y", "c"))

    dw = {}
    dg = {}

    def ffn_bwd(dh_out, h_in, hn, a, hm, f, layer):
        df, dg[("post_ffn", layer)] = _rms_bwd_out(dh_out, f, gains(post_ffn_g, layer), f"d_norm_ffn_out{layer}")
        dwd = _mm(hm, df, "tn", f"d_w_down{layer}", out_dtype=BF16)[0]
        dw[("wdn", layer)] = dwd.reshape(N_CHIPS, dwd.shape[0] // N_CHIPS, d)
        dhm = _mm(df, rows("wdn", layer), "nt", f"d_ffn_mid{layer}", out_split=2)
        da, dg[("conv_w", layer)], dg[("conv_b", layer)] = _ffn_act_bwd(
            a, dhm, conv_w[layer], conv_b[layer], f"d_ffn_act{layer}")
        dw[("wup", layer)] = _mm(hn, da, "tn", f"d_w_up{layer}", out_dtype=BF16, out_split=N_CHIPS)
        dhn = _mm(da, full[("wup", layer)], "nt", f"d_ffn_in{layer}")[0]
        return dhn

    dhn3 = ffn_bwd(dh4, h3, hn3, a1, hm1, f1, 1)
    dh3, (dg[("pre_ffn", 1)],) = _rms_bwd_in(dh4, h3, [([dhn3], gains(pre_ffn_g, 1))], "d_norm_ffn_in1")
    dmix1, dg[("post_mix", 1)] = _rms_bwd_out(dh3, mix1, gains(post_mix_g, 1), "d_norm_mix_out1")
    dwo = _mm(att, dmix1, "tn", "d_w_o", out_dtype=BF16)[0]
    dw[("wo", 0)] = dwo.reshape(N_CHIPS, dwo.shape[0] // N_CHIPS, d)
    datt = _mm(dmix1, rows("wo"), "nt", "d_attn_out", out_dtype=BF16)[0]
    dq, dk, dv = _attn_bwd(q, k, v, datt, lsum, "attn_bwd")
    for nm, act, dact in (("wq", hn2, dq), ("wk", kvn, dk), ("wv", kvn, dv)):
        g = _mm(act, dact, "tn", f"d_{nm}", out_dtype=BF16)[0]
        dw[(nm, 0)] = g.reshape(N_CHIPS, g.shape[0] // N_CHIPS, g.shape[1])
    dhn2 = _mm(dq, rows("wq"), "nt", "d_q_in")[0]
    dkvn_k = _mm(dk, rows("wk"), "nt", "d_k_in")[0]
    dkvn_v = _mm(dv, rows("wv"), "nt", "d_v_in")[0]
    dh2, (dg[("pre_mix", 1)], dg["kv"]) = _rms_bwd_in(
        dh3, h2, [([dhn2], gains(pre_mix_g, 1)), ([dkvn_k, dkvn_v], kv_g)], "d_norm_mix_in1")
    dhn1 = ffn_bwd(dh2, h1, hn1, a0, hm0, f0, 0)
    dh1, (dg[("pre_ffn", 0)],) = _rms_bwd_in(dh2, h1, [([dhn1], gains(pre_ffn_g, 0))], "d_norm_ffn_in0")
    dmix0, dg[("post_mix", 0)] = _rms_bwd_out(dh1, mix0, gains(post_mix_g, 0), "d_norm_mix_out0")
    dwout = _mm(gm, dmix0, "tn", "d_w_out", out_dtype=BF16)[0]
    dw[("wout", 0)] = dwout.reshape(N_CHIPS, dwout.shape[0] // N_CHIPS, d)
    dgm = _mm(dmix0, rows("wout"), "nt", "d_gmlp_gate")[0]
    duv, d_ws, d_bs, d_vg = _gmlp_bwd(uv, dgm, v_g, w_s, bias, "d_gmlp")
    dw[("win", 0)] = _mm(hn0, duv, "tn", "d_w_in", out_dtype=BF16, out_split=N_CHIPS)
    dhn0 = _mm(duv, full[("win", 0)], "nt", "d_gmlp_in")[0]
    dx, (dg[("pre_mix", 0)],) = _rms_bwd_in(dh1, h0, [([dhn0], gains(pre_mix_g, 0))], "d_norm_in")

    dws = [dw[u] for u in units]
    from_sibling = _swap_halves(dws, "grads_to_sibling")
    pair = [_pair_add(g, r, c_idx, f"pair_add_{nm}{layer}") for g, r, (nm, layer) in zip(dws, from_sibling, units)]
    at_owner = _scatter_to_owners(pair, "grads_to_owner")
    half_done = {nm: None for nm in big}
    for p, (nm, layer) in zip(at_owner, units):
        half_done[nm] = _chip_sum(p, half_done[nm], big[nm][0].shape, layer, c_idx, f"chip_sum_{nm}{layer}")
    grads_big = dict(zip(big, _join_halves([half_done[nm] for nm in big], "grads_join")))

    stack = lambda key: jnp.concatenate([dg[(key, layer)] for layer in range(n_layers)], axis=0)
    small_parts = [
        stack("pre_mix"), stack("post_mix"), stack("pre_ffn"), stack("post_ffn"),
        d_vg, d_ws, d_bs[::SUBLANE], dg["kv"],
        jnp.stack([dg[("conv_w", layer)] for layer in range(n_layers)]),
        jnp.stack([dg[("conv_b", layer)] for layer in range(n_layers)]),
    ]
    summed = _all_reduce_small(_pack(small_parts), "small_grads_sum")
    (g_pre_mix, g_post_mix, g_pre_ffn, g_post_ffn, g_vg, g_ws, g_bs, g_kv, g_cw, g_cb) = _unpack(
        summed, [p.shape for p in small_parts])
    g_vg = lax.dynamic_index_in_dim(g_vg.reshape(N_CHIPS, 1, d_a // N_CHIPS), chip, 0, keepdims=False)
    g_cw = lax.dynamic_index_in_dim(g_cw, chip, 1, keepdims=False)
    g_cb = g_cb.reshape(n_layers, N_CHIPS * ns)
    small = [
        (pre_mix_g, g_pre_mix, m_pre_mix_g, v_pre_mix_g),
        (post_mix_g, g_post_mix, m_post_mix_g, v_post_mix_g),
        (pre_ffn_g, g_pre_ffn, m_pre_ffn_g, v_pre_ffn_g),
        (post_ffn_g, g_post_ffn, m_post_ffn_g, v_post_ffn_g),
        (a_v_norm_g, g_vg, m_a_v_norm_g, v_a_v_norm_g),
        (a_w_spatial, g_ws[None], m_a_w_spatial, v_a_w_spatial),
        (a_b_spatial, g_bs[None], m_a_b_spatial, v_a_b_spatial),
        (kv_norm_g, g_kv.reshape(d), m_kv_norm_g, v_kv_norm_g),
        (ffn_conv_w, g_cw, m_ffn_conv_w, v_ffn_conv_w),
        (ffn_conv_b, g_cb, m_ffn_conv_b, v_ffn_conv_b),
    ]
    small = [(w, g.reshape(w.shape), m, v) for w, g, m, v in small]
    packed = [_pack([t[i] for t in small])[None] for i in range(4)]
    small_new = [_unpack(p[0], [t[0].shape for t in small]) for p in _adamw(*packed, "adamw_small")]

    new_big = {nm: _adamw(big[nm][0], grads_big[nm], big[nm][1], big[nm][2], f"adamw_{nm}") for nm in big}

    def big_out(nm, which):
        ref_shape = {"wk": w_k.shape, "wv": w_v.shape}.get(nm, big[nm][0].shape)
        arr = grads_big[nm] if which == 0 else new_big[nm][which - 1]
        return arr.reshape(ref_shape)

    order = ["pre_mix", "post_mix", "pre_ffn", "post_ffn", "win", "vg", "ws", "bs", "wout", "kv", "wk", "wv", "wq",
             "wo", "wup", "cw", "cb", "wdn"]
    small_at = {"pre_mix": 0, "post_mix": 1, "pre_ffn": 2, "post_ffn": 3, "vg": 4, "ws": 5, "bs": 6, "kv": 7,
                "cw": 8, "cb": 9}
    outs = [loss, dx[None]]
    for which in range(4):
        for nm in order:
            if nm in small_at:
                outs.append(small[small_at[nm]][1] if which == 0 else small_new[which - 1][small_at[nm]])
            else:
                outs.append(big_out(nm, which))
    return tuple(outs)
```

```python
import functools
import math

import jax
import jax.numpy as jnp
from jax import lax
from jax.experimental import pallas as pl
from jax.experimental.pallas import tpu as pltpu

F32 = jnp.float32
BF16 = jnp.bfloat16
EPS = 1e-6
ADAM_LR = 0.001
ADAM_B1 = 0.9
ADAM_B2 = 0.999
ADAM_EPS = 1e-08
ADAM_WD = 0.01
ADAM_STEP = 10

LANE = 128
SUBLANE = 8
ROWS = 16
TILE = 128
N_CHIPS = 4
N_DEV = 8
VMEM_LIMIT = 56 * 1024 * 1024
MM_VMEM = 40 * 1024 * 1024
MESH = pl.DeviceIdType.MESH
ANY = pl.BlockSpec(memory_space=pl.ANY)
VMEM_SPEC = pl.BlockSpec(memory_space=pltpu.VMEM)


def _cp(*sem):
    return pltpu.CompilerParams(dimension_semantics=sem, vmem_limit_bytes=VMEM_LIMIT)


def _pick(dim, pref, align=LANE):
    if dim <= pref:
        return dim
    best = None
    for d in range(align, pref + 1, align):
        if dim % d == 0:
            best = d
    assert best is not None, (dim, pref)
    return best


_DIMS = {
    "nn": (((1,), (0,)), ((), ())),
    "nt": (((1,), (1,)), ((), ())),
    "tn": (((0,), (0,)), ((), ())),
}


def _as3(a):
    return a if a.ndim == 3 else a[None]


def _spec3(br, bc, cols_j, rc):
    per = cols_j // bc

    def imap(m, n, k):
        r, c = rc(m, n, k)
        return (c // per, r, c % per)

    return pl.BlockSpec((None, br, bc), imap)


def _mm(a, b, mode, name, out_dtype=F32, out_split=1, rides=()):
    a, b = _as3(a), _as3(b)
    ja, ra, caj = a.shape
    jb, rb, cbj = b.shape
    if mode == "nn":
        m, k, n = ra, ja * caj, jb * cbj
        assert rb == k
        m_ext, k_ext, n_ext = [ra], [caj, rb], [cbj]
    elif mode == "nt":
        m, k, n = ra, ja * caj, rb
        assert jb * cbj == k
        m_ext, k_ext, n_ext = [ra], [caj, cbj], [rb]
    else:
        m, k, n = ja * caj, ra, jb * cbj
        assert rb == k
        m_ext, k_ext, n_ext = [caj], [ra], [cbj]
    assert n % out_split == 0
    n_ext.append(n // out_split)
    bm = _pick(math.gcd(*m_ext), 1536)
    bn = _pick(math.gcd(*n_ext), 1536)
    k_unit = math.gcd(*k_ext)
    o_bytes = jnp.dtype(out_dtype).itemsize

    def vmem_need(bk):
        tiles = bm * bk * a.dtype.itemsize + bk * bn * b.dtype.itemsize + bm * bn * o_bytes
        return 2 * tiles + (bm * bn * 4 if bk < k else 0)

    bk = max(d for d in range(LANE, k_unit + 1, LANE) if k_unit % d == 0 and (d == LANE or vmem_need(d) <= MM_VMEM))
    nk = k // bk
    if mode == "nn":
        a_spec = _spec3(bm, bk, caj, lambda mi, ni, ki: (mi, ki))
        b_spec = _spec3(bk, bn, cbj, lambda mi, ni, ki: (ki, ni))
    elif mode == "nt":
        a_spec = _spec3(bm, bk, caj, lambda mi, ni, ki: (mi, ki))
        b_spec = _spec3(bn, bk, cbj, lambda mi, ni, ki: (ni, ki))
    else:
        a_spec = _spec3(bk, bm, caj, lambda mi, ni, ki: (ki, mi))
        b_spec = _spec3(bk, bn, cbj, lambda mi, ni, ki: (ki, ni))
    o_spec = _spec3(bm, bn, n // out_split, lambda mi, ni, ki: (mi, ni))
    dims = _DIMS[mode]

    def body(a_ref, b_ref, o_ref, *acc):
        def part():
            return lax.dot_general(a_ref[...].astype(BF16), b_ref[...].astype(BF16), dims, preferred_element_type=F32)

        if nk == 1:
            o_ref[...] = part().astype(o_ref.dtype)
            return
        acc_ref, = acc
        ki = pl.program_id(2)

        @pl.when(ki == 0)
        def _():
            acc_ref[...] = part()

        @pl.when(jnp.logical_and(ki > 0, ki < nk - 1))
        def _():
            acc_ref[...] += part()

        @pl.when(ki == nk - 1)
        def _():
            o_ref[...] = (acc_ref[...] + part()).astype(o_ref.dtype)

    out, rode = _hosted_call(
        body, [a, b], name=name, grid=(m // bm, n // bn, nk), in_specs=[a_spec, b_spec], out_specs=o_spec,
        out_shape=jax.ShapeDtypeStruct((out_split, m, n // out_split), out_dtype),
        scratch_shapes=[pltpu.VMEM((bm, bn), F32)] if nk > 1 else [],
        semantics=("parallel", "parallel", "arbitrary"), rides=rides)
    return (out, rode) if rides else out


def _rms(x, g):
    r = lax.rsqrt(jnp.mean(x * x, axis=-1, keepdims=True) + EPS)
    return x * r * g


def _rms_bwd(x, g, dy):
    r = lax.rsqrt(jnp.mean(x * x, axis=-1, keepdims=True) + EPS)
    xh = x * r
    gy = dy * g
    dx = r * (gy - xh * jnp.mean(gy * xh, axis=-1, keepdims=True))
    return dx, jnp.sum(dy * xh, axis=0, keepdims=True)


def _row_block(s):
    return _pick(s, 256, ROWS)


def _rms_fwd(h, g, name):
    s, d = h.shape
    br = _row_block(s)

    def body(h_ref, g_ref, o_ref):
        o_ref[...] = _rms(h_ref[...], g_ref[...]).astype(BF16)

    row = pl.BlockSpec((br, d), lambda i: (i, 0))
    vec = pl.BlockSpec((1, d), lambda i: (0, 0))
    return pl.pallas_call(
        body, name=name, grid=(s // br,), in_specs=[row, vec], out_specs=row,
        out_shape=jax.ShapeDtypeStruct((s, d), BF16), compiler_params=_cp("parallel"),
    )(h, g)


def _resid_rms(h_in, f, g_post, g_next, name):
    s, d = h_in.shape
    br = _row_block(s)
    n_next = len(g_next)

    def body(h_ref, f_ref, gp_ref, *refs):
        gn_refs, ho_ref, hn_refs = refs[:n_next], refs[n_next], refs[n_next + 1:]
        h = h_ref[...] + _rms(f_ref[...], gp_ref[...])
        ho_ref[...] = h
        for gn_ref, hn_ref in zip(gn_refs, hn_refs):
            hn_ref[...] = _rms(h, gn_ref[...]).astype(BF16)

    row = pl.BlockSpec((br, d), lambda i: (i, 0))
    vec = pl.BlockSpec((1, d), lambda i: (0, 0))
    return pl.pallas_call(
        body, name=name, grid=(s // br,),
        in_specs=[row, row, vec] + [vec] * n_next,
        out_specs=[row] * (1 + n_next),
        out_shape=[jax.ShapeDtypeStruct((s, d), F32)] + [jax.ShapeDtypeStruct((s, d), BF16)] * n_next,
        compiler_params=_cp("parallel"),
    )(h_in, f, g_post, *g_next)


def _loss_head(h_in, f, g_post, target, name):
    s, d = h_in.shape
    br = _row_block(s)

    def body(h_ref, f_ref, gp_ref, t_ref, dh_ref, loss_ref):
        @pl.when(pl.program_id(0) == 0)
        def _():
            loss_ref[...] = jnp.zeros_like(loss_ref)

        diff = h_ref[...] + _rms(f_ref[...], gp_ref[...]) - t_ref[...]
        dh_ref[...] = diff * (1.0 / d)
        loss_ref[...] += 0.5 * jnp.sum(jnp.mean(diff * diff, axis=-1, keepdims=True))

    row = pl.BlockSpec((br, d), lambda i: (i, 0))
    vec = pl.BlockSpec((1, d), lambda i: (0, 0))
    return pl.pallas_call(
        body, name=name, grid=(s // br,),
        in_specs=[row, row, vec, row],
        out_specs=[row, pl.BlockSpec((SUBLANE, LANE), lambda i: (0, 0))],
        out_shape=[jax.ShapeDtypeStruct((s, d), F32), jax.ShapeDtypeStruct((SUBLANE, LANE), F32)],
        compiler_params=_cp("arbitrary"),
    )(h_in, f, g_post, target)


def _rms_bwd_out(dy, f, g, name):
    s, d = f.shape
    br = _row_block(s)

    def body(dy_ref, f_ref, g_ref, df_ref, dg_ref):
        @pl.when(pl.program_id(0) == 0)
        def _():
            dg_ref[...] = jnp.zeros_like(dg_ref)

        dx, dg = _rms_bwd(f_ref[...], g_ref[...], dy_ref[...])
        df_ref[...] = dx.astype(BF16)
        dg_ref[...] += dg

    row = pl.BlockSpec((br, d), lambda i: (i, 0))
    vec = pl.BlockSpec((1, d), lambda i: (0, 0))
    return pl.pallas_call(
        body, name=name, grid=(s // br,), in_specs=[row, row, vec], out_specs=[row, vec],
        out_shape=[jax.ShapeDtypeStruct((s, d), BF16), jax.ShapeDtypeStruct((1, d), F32)],
        compiler_params=_cp("arbitrary"),
    )(dy, f, g)


def _rms_bwd_in(dh_out, h_in, branches, name):
    s, d = h_in.shape
    br = _row_block(s)
    counts = [len(ds) for ds, _ in branches]
    n_d = sum(counts)
    n_b = len(branches)

    def body(dho_ref, h_ref, *refs):
        d_refs, g_refs = refs[:n_d], refs[n_d:n_d + n_b]
        dh_ref, dg_refs = refs[n_d + n_b], refs[n_d + n_b + 1:]

        @pl.when(pl.program_id(0) == 0)
        def _():
            for r in dg_refs:
                r[...] = jnp.zeros_like(r)

        h = h_ref[...]
        acc = dho_ref[...]
        at = 0
        for bi, cnt in enumerate(counts):
            dn = d_refs[at][...]
            for r in d_refs[at + 1:at + cnt]:
                dn = dn + r[...]
            at += cnt
            dx, dg = _rms_bwd(h, g_refs[bi][...], dn)
            acc = acc + dx
            dg_refs[bi][...] += dg
        dh_ref[...] = acc

    row = pl.BlockSpec((br, d), lambda i: (i, 0))
    vec = pl.BlockSpec((1, d), lambda i: (0, 0))
    flat_d = [x for ds, _ in branches for x in ds]
    outs = pl.pallas_call(
        body, name=name, grid=(s // br,),
        in_specs=[row, row] + [row] * n_d + [vec] * n_b,
        out_specs=[row] + [vec] * n_b,
        out_shape=[jax.ShapeDtypeStruct((s, d), F32)] + [jax.ShapeDtypeStruct((1, d), F32)] * n_b,
        compiler_params=_cp("arbitrary"),
    )(dh_out, h_in, *flat_d, *[g for _, g in branches])
    return outs[0], list(outs[1:])


def _split3(x):
    x0 = x.astype(BF16)
    r1 = x - x0.astype(F32)
    x1 = r1.astype(BF16)
    x2 = (r1 - x1.astype(F32)).astype(BF16)
    return x0, x1, x2


def _tri(n, kind):
    r = lax.broadcasted_iota(jnp.int32, (n, n), 0)
    c = lax.broadcasted_iota(jnp.int32, (n, n), 1)
    m = {"lt": r < c, "le": r <= c, "gt": r > c}[kind]
    return jnp.where(m, 1.0, 0.0).astype(BF16)


_GELU_C = math.sqrt(2.0 / math.pi)
_GELU_A = 0.044715


def _gelu(x):
    return 0.5 * x * (1.0 + jnp.tanh(_GELU_C * (x + _GELU_A * (x * x * x))))


def _gelu_grad(x):
    t = jnp.tanh(_GELU_C * (x + _GELU_A * (x * x * x)))
    return 0.5 * (1.0 + t) + 0.5 * x * (1.0 - t * t) * (_GELU_C * (1.0 + 3.0 * _GELU_A * (x * x)))


def _causal_w(w):
    r = lax.broadcasted_iota(jnp.int32, (TILE, TILE), 0)
    c = lax.broadcasted_iota(jnp.int32, (TILE, TILE), 1)
    return jnp.where(c <= r, w, 0.0)


def _uv_tiles(uv_ref, g, d_a, dq):
    cu, cv = g * TILE, d_a + g * TILE
    u = uv_ref[cu // dq, :, pl.ds(cu % dq, TILE)]
    v = uv_ref[cv // dq, :, pl.ds(cv % dq, TILE)]
    return u, v


def _gmlp_fwd(uv, v_g, w_s, bias, name):
    _, s, dq = uv.shape
    d_a = 2 * dq
    n_g = d_a // TILE

    def body(uv_ref, vg_ref, ws_ref, b_ref, o_ref):
        for g in range(n_g):
            up, vp = _uv_tiles(uv_ref, g, d_a, dq)
            cols = pl.ds(g * TILE, TILE)
            vn = _rms(_gelu(vp), vg_ref[:, cols])
            mixed = jnp.dot(_causal_w(ws_ref[g]).astype(BF16), vn.astype(BF16), preferred_element_type=F32) + b_ref[:, cols]
            o_ref[:, cols] = (_gelu(up) * mixed).astype(BF16)

    return pl.pallas_call(
        body, name=name, grid=(s // TILE,),
        in_specs=[
            pl.BlockSpec((4, TILE, dq), lambda i: (0, i, 0)),
            pl.BlockSpec((1, d_a), lambda i: (0, 0)),
            pl.BlockSpec((n_g, TILE, TILE), lambda i: (0, 0, 0)),
            pl.BlockSpec((TILE, d_a), lambda i: (0, 0)),
        ],
        out_specs=pl.BlockSpec((TILE, d_a), lambda i: (i, 0)),
        out_shape=jax.ShapeDtypeStruct((s, d_a), BF16),
        compiler_params=_cp("parallel"),
    )(uv, v_g, w_s, bias)


def _gmlp_bwd(uv, dgm, v_g, w_s, bias, name):
    _, s, dq = uv.shape
    d_a = 2 * dq
    n_g = d_a // TILE
    n_c = s // TILE

    def body(uv_ref, d_ref, vg_ref, ws_ref, b_ref, duv_ref, dws_ref, dbs_ref, dvg_ref, dbias_acc):
        i = pl.program_id(0)

        @pl.when(i == 0)
        def _():
            dws_ref[...] = jnp.zeros_like(dws_ref)
            dvg_ref[...] = jnp.zeros_like(dvg_ref)
            dbias_acc[...] = jnp.zeros_like(dbias_acc)

        for g in range(n_g):
            up, vp = _uv_tiles(uv_ref, g, d_a, dq)
            cols = pl.ds(g * TILE, TILE)
            vg = vg_ref[:, cols]
            u = _gelu(up)
            v = _gelu(vp)
            r = lax.rsqrt(jnp.mean(v * v, axis=-1, keepdims=True) + EPS)
            vh = v * r
            vn = (vh * vg).astype(BF16)
            wc = _causal_w(ws_ref[g]).astype(BF16)
            mixed = jnp.dot(wc, vn, preferred_element_type=F32) + b_ref[:, cols]
            d_out = d_ref[:, cols]
            du = d_out * mixed
            dmixed = d_out * u
            dmb = dmixed.astype(BF16)
            dvn = lax.dot_general(wc, dmb, _DIMS["tn"], preferred_element_type=F32)
            dws_ref[g] += lax.dot_general(dmb, vn, _DIMS["nt"], preferred_element_type=F32)
            dbias_acc[:, cols] += dmixed
            dvg_ref[:, cols] += jnp.sum(dvn * vh, axis=0, keepdims=True)
            gv = dvn * vg
            dv = r * (gv - vh * jnp.mean(gv * vh, axis=-1, keepdims=True))
            cu, cv = g * TILE, d_a + g * TILE
            duv_ref[cu // dq, :, pl.ds(cu % dq, TILE)] = (du * _gelu_grad(up)).astype(BF16)
            duv_ref[cv // dq, :, pl.ds(cv % dq, TILE)] = (dv * _gelu_grad(vp)).astype(BF16)

        @pl.when(i == n_c - 1)
        def _():
            ones = jnp.ones((SUBLANE, TILE), BF16)
            for g in range(n_g):
                dws_ref[g] = _causal_w(dws_ref[g])
                cols = pl.ds(g * TILE, TILE)
                out = None
                for t in _split3(dbias_acc[:, cols]):
                    p = lax.dot_general(ones, t, _DIMS["nt"], preferred_element_type=F32)
                    out = p if out is None else out + p
                dbs_ref[pl.ds(g * SUBLANE, SUBLANE), :] = out

    return pl.pallas_call(
        body, name=name, grid=(n_c,),
        in_specs=[
            pl.BlockSpec((4, TILE, dq), lambda i: (0, i, 0)),
            pl.BlockSpec((TILE, d_a), lambda i: (i, 0)),
            pl.BlockSpec((1, d_a), lambda i: (0, 0)),
            pl.BlockSpec((n_g, TILE, TILE), lambda i: (0, 0, 0)),
            pl.BlockSpec((TILE, d_a), lambda i: (0, 0)),
        ],
        out_specs=[
            pl.BlockSpec((4, TILE, dq), lambda i: (0, i, 0)),
            pl.BlockSpec((n_g, TILE, TILE), lambda i: (0, 0, 0)),
            pl.BlockSpec((n_g * SUBLANE, TILE), lambda i: (0, 0)),
            pl.BlockSpec((1, d_a), lambda i: (0, 0)),
        ],
        out_shape=[
            jax.ShapeDtypeStruct((4, s, dq), BF16),
            jax.ShapeDtypeStruct((n_g, TILE, TILE), F32),
            jax.ShapeDtypeStruct((n_g * SUBLANE, TILE), F32),
            jax.ShapeDtypeStruct((1, d_a), F32),
        ],
        scratch_shapes=[pltpu.VMEM((TILE, d_a), F32)],
        compiler_params=_cp("arbitrary"),
    )(uv, dgm, v_g, w_s, bias)


def _sigmoid(x):
    return 1.0 / (1.0 + jnp.exp(-x))


def _conv3(ext, w, b):
    return b + ((w[0:1] * pltpu.roll(ext, 2, 0) + w[1:2] * pltpu.roll(ext, 1, 0)) + w[2:3] * ext)


def _act_blocks(s, ns):
    return _pick(s, 512, ROWS), _pick(ns, 256)


def _ffn_act_fwd(a, cw, cb, name):
    _, s, ns = a.shape
    bs, cb_w = _act_blocks(s, ns)
    hb = bs // SUBLANE

    def body(a_ref, prev_ref, cw_ref, cb_ref, o_ref):
        first = pl.program_id(0) == 0

        def conv(comp):
            prev = jnp.where(first, 0.0, prev_ref[comp])
            ext = jnp.concatenate([prev, a_ref[comp]], axis=0)
            return _conv3(ext, cw_ref[comp], cb_ref[comp])[SUBLANE:]

        for p in range(2):
            cg = conv(p)
            o_ref[p] = (cg * _sigmoid(cg) * conv(2 + p)).astype(BF16)

    return pl.pallas_call(
        body, name=name, grid=(s // bs, ns // cb_w),
        in_specs=[
            pl.BlockSpec((4, bs, cb_w), lambda i, j: (0, i, j)),
            pl.BlockSpec((4, SUBLANE, cb_w), lambda i, j: (0, jnp.maximum(i * hb - 1, 0), j)),
            pl.BlockSpec((4, 3, cb_w), lambda i, j: (0, 0, j)),
            pl.BlockSpec((4, 1, cb_w), lambda i, j: (0, 0, j)),
        ],
        out_specs=pl.BlockSpec((2, bs, cb_w), lambda i, j: (0, i, j)),
        out_shape=jax.ShapeDtypeStruct((2, s, ns), BF16),
        compiler_params=_cp("parallel", "parallel"),
    )(a, a, cw, cb)


def _ffn_act_bwd(a, dhm, cw, cb, name, rides=()):
    _, s, ns = a.shape
    bs, cb_w = _act_blocks(s, ns)
    hb = bs // SUBLANE
    n_i = s // bs
    n_ext = bs + 2 * SUBLANE
    cur = slice(SUBLANE, SUBLANE + bs)

    def body(a_ref, prev_ref, next_ref, d_ref, dnext_ref, cw_ref, cb_ref, da_ref, dcw_ref, dcb_ref):
        i = pl.program_id(1)
        first, last = i == 0, i == n_i - 1

        @pl.when(first)
        def _():
            dcw_ref[...] = jnp.zeros_like(dcw_ref)
            dcb_ref[...] = jnp.zeros_like(dcb_ref)

        def ext_of(comp):
            return jnp.concatenate([jnp.where(first, 0.0, prev_ref[comp]), a_ref[comp], next_ref[comp]], axis=0)

        def back(comp, a_ext, dc):
            w = cw_ref[comp]
            da = (w[2:3] * dc + w[1:2] * pltpu.roll(dc, n_ext - 1, 0)) + w[0:1] * pltpu.roll(dc, n_ext - 2, 0)
            da_ref[comp] = da[cur].astype(BF16)
            dcc = dc[cur]
            dcw_ref[comp, 0:1, :] += jnp.sum(dcc * pltpu.roll(a_ext, 2, 0)[cur], axis=0, keepdims=True)
            dcw_ref[comp, 1:2, :] += jnp.sum(dcc * pltpu.roll(a_ext, 1, 0)[cur], axis=0, keepdims=True)
            dcw_ref[comp, 2:3, :] += jnp.sum(dcc * a_ext[cur], axis=0, keepdims=True)
            dcb_ref[comp] += jnp.sum(dcc, axis=0, keepdims=True)

        for p in range(2):
            ag, av = ext_of(p), ext_of(2 + p)
            cg = _conv3(ag, cw_ref[p], cb_ref[p])
            cv = _conv3(av, cw_ref[2 + p], cb_ref[2 + p])
            d = jnp.concatenate(
                [jnp.zeros((SUBLANE, cb_w), F32), d_ref[p], jnp.where(last, 0.0, dnext_ref[p])], axis=0)
            sg = _sigmoid(cg)
            back(2 + p, av, d * (cg * sg))
            back(p, ag, d * cv * (sg * (1.0 + cg * (1.0 - sg))))

    return _hosted_call(
        body, [a, a, a, dhm, dhm, cw, cb], name=name, grid=(ns // cb_w, n_i),
        in_specs=[
            pl.BlockSpec((4, bs, cb_w), lambda j, i: (0, i, j)),
            pl.BlockSpec((4, SUBLANE, cb_w), lambda j, i: (0, jnp.maximum(i * hb - 1, 0), j)),
            pl.BlockSpec((4, SUBLANE, cb_w), lambda j, i: (0, jnp.minimum((i + 1) * hb, n_i * hb - 1), j)),
            pl.BlockSpec((2, bs, cb_w), lambda j, i: (0, i, j)),
            pl.BlockSpec((2, SUBLANE, cb_w), lambda j, i: (0, jnp.minimum((i + 1) * hb, n_i * hb - 1), j)),
            pl.BlockSpec((4, 3, cb_w), lambda j, i: (0, 0, j)),
            pl.BlockSpec((4, 1, cb_w), lambda j, i: (0, 0, j)),
        ],
        out_specs=[
            pl.BlockSpec((4, bs, cb_w), lambda j, i: (0, i, j)),
            pl.BlockSpec((4, 3, cb_w), lambda j, i: (0, 0, j)),
            pl.BlockSpec((4, 1, cb_w), lambda j, i: (0, 0, j)),
        ],
        out_shape=[
            jax.ShapeDtypeStruct((4, s, ns), BF16),
            jax.ShapeDtypeStruct((4, 3, ns), F32),
            jax.ShapeDtypeStruct((4, 1, ns), F32),
        ],
        semantics=("parallel", "arbitrary"), rides=rides)


ATT_BQ = 512
ATT_BK = 256


def _att_blocks(s):
    bq = _pick(s, ATT_BQ)
    bk = min(ATT_BK, bq)
    assert bq % bk == 0
    return bq, bk


def _dot_sel2(x, sel):
    hi = x.astype(BF16)
    lo = (x - hi.astype(F32)).astype(BF16)
    n = x.shape[0]
    both = jnp.dot(jnp.concatenate([hi, lo], axis=0), sel, preferred_element_type=F32)
    return both[:n] + both[n:]


def _causal_mask(bq, bk, row0, col0):
    rows = row0 + lax.broadcasted_iota(jnp.int32, (bq, bk), 0)
    cols = col0 + lax.broadcasted_iota(jnp.int32, (bq, bk), 1)
    return cols < rows


def _sb_tile(qb, kb, scale, mask):
    z = lax.dot_general(qb, kb, _DIMS["nt"], preferred_element_type=F32) * scale
    e = jnp.exp(-jnp.abs(z))
    lb = jnp.minimum(z, 0.0) - jnp.log(1.0 + e)
    l1m = lb - z
    if mask is not None:
        l1m = jnp.where(mask, l1m, 0.0)
    return z, e, lb, l1m


def _attn_fwd(q, k, v, name, rides=()):
    s, hd = q.shape
    bq, bk = _att_blocks(s)
    r = bq // bk
    n_h, n_q = hd // TILE, s // bq
    scale = 1.0 / math.sqrt(TILE)

    def body(q_ref, k_ref, v_ref, o_ref, l_ref, acc_ref, suf_ref):
        i = pl.program_id(1)
        qb = q_ref[...]
        later = _tri(bk, "gt")
        acc_ref[...] = jnp.zeros_like(acc_ref)
        suf_ref[...] = jnp.zeros_like(suf_ref)

        def tile(j, masked):
            rows = pl.ds(pl.multiple_of(j * bk, bk), bk)
            mask = _causal_mask(bq, bk, i * bq, j * bk) if masked else None
            _, _, lb, l1m = _sb_tile(qb, k_ref[rows, :], scale, mask)
            a = jnp.exp(lb + _dot_sel2(l1m, later) + suf_ref[...])
            if masked:
                a = jnp.where(mask, a, 0.0)
            acc_ref[...] += jnp.dot(a.astype(BF16), v_ref[rows, :], preferred_element_type=F32)
            suf_ref[...] += jnp.sum(l1m, axis=1, keepdims=True)

        for dgl in range(r - 1, -1, -1):
            tile(r * i + dgl, True)

        def step(t, carry):
            tile(r * i - 1 - t, False)
            return carry

        lax.fori_loop(0, r * i, step, 0)
        o_ref[...] = acc_ref[...].astype(BF16)
        l_ref[...] = jnp.broadcast_to(suf_ref[...], (bq, TILE))

    blk = pl.BlockSpec((bq, TILE), lambda h, i: (i, h))
    head = pl.BlockSpec((s, TILE), lambda h, i: (0, h))
    return _hosted_call(
        body, [q, k, v], name=name, grid=(n_h, n_q), in_specs=[blk, head, head], out_specs=[blk, blk],
        out_shape=[jax.ShapeDtypeStruct((s, hd), BF16), jax.ShapeDtypeStruct((s, hd), F32)],
        scratch_shapes=[pltpu.VMEM((bq, TILE), F32), pltpu.VMEM((bq, 1), F32)],
        semantics=("parallel", "parallel"), rides=rides)


def _attn_bwd(q, k, v, do, lsum, name, rides=()):
    s, hd = q.shape
    bq, bk = _att_blocks(s)
    r = bq // bk
    n_h, n_q = hd // TILE, s // bq
    scale = 1.0 / math.sqrt(TILE)

    def body(q_ref, k_ref, v_ref, do_ref, l_ref, dq_ref, dk_ref, dv_ref, dq_acc, pre_ref, cp_ref):
        i = pl.program_id(1)

        @pl.when(i == 0)
        def _():
            dk_ref[...] = jnp.zeros_like(dk_ref)
            dv_ref[...] = jnp.zeros_like(dv_ref)

        qb = q_ref[...]
        dob = do_ref[...]
        upto = _tri(bk, "le")
        before = _tri(bk, "lt")
        dq_acc[...] = jnp.zeros_like(dq_acc)
        pre_ref[...] = jnp.zeros_like(pre_ref)
        cp_ref[...] = jnp.zeros_like(cp_ref)

        def tile(j, masked):
            rows = pl.ds(pl.multiple_of(j * bk, bk), bk)
            kb, vb = k_ref[rows, :], v_ref[rows, :]
            mask = _causal_mask(bq, bk, i * bq, j * bk) if masked else None
            z, e, lb, l1m = _sb_tile(qb, kb, scale, mask)
            suffix = (l_ref[:, 0:1] - pre_ref[...]) - _dot_sel2(l1m, upto)
            a = jnp.exp(lb + suffix)
            if masked:
                a = jnp.where(mask, a, 0.0)
            p = a * lax.dot_general(dob, vb, _DIMS["nt"], preferred_element_type=F32)
            cprev = cp_ref[...] + _dot_sel2(p, before)
            inv = pl.reciprocal(1.0 + e, approx=True)
            pos = z >= 0.0
            dz = p * (jnp.where(pos, e, 1.0) * inv) - cprev * (jnp.where(pos, 1.0, e) * inv)
            if masked:
                dz = jnp.where(mask, dz, 0.0)
            dz = (dz * scale).astype(BF16)
            dq_acc[...] += jnp.dot(dz, kb, preferred_element_type=F32)
            dk_ref[rows, :] += lax.dot_general(dz, qb, _DIMS["tn"], preferred_element_type=F32)
            dv_ref[rows, :] += lax.dot_general(a.astype(BF16), dob, _DIMS["tn"], preferred_element_type=F32)
            pre_ref[...] += jnp.sum(l1m, axis=1, keepdims=True)
            cp_ref[...] += jnp.sum(p, axis=1, keepdims=True)

        def step(j, carry):
            tile(j, False)
            return carry

        lax.fori_loop(0, r * i, step, 0)
        for dgl in range(r):
            tile(r * i + dgl, True)
        dq_ref[...] = dq_acc[...].astype(BF16)

    blk = pl.BlockSpec((bq, TILE), lambda h, i: (i, h))
    head = pl.BlockSpec((s, TILE), lambda h, i: (0, h))
    return _hosted_call(
        body, [q, k, v, do, lsum], name=name, grid=(n_h, n_q), in_specs=[blk, head, head, blk, blk],
        out_specs=[blk, head, head],
        out_shape=[jax.ShapeDtypeStruct((s, hd), BF16), jax.ShapeDtypeStruct((s, hd), F32),
                   jax.ShapeDtypeStruct((s, hd), F32)],
        scratch_shapes=[pltpu.VMEM((bq, TILE), F32), pltpu.VMEM((bq, 1), F32), pltpu.VMEM((bq, 1), F32)],
        semantics=("parallel", "arbitrary"), rides=rides)


def _cast_bf16(w, layer, chip_idx, name):
    _, r, c = w.shape
    br, bc = _pick(r, 256, ROWS), _pick(c, 1024)

    def body(chip_ref, w_ref, o_ref):
        o_ref[...] = w_ref[...].astype(BF16)

    return pl.pallas_call(
        body, name=name,
        grid_spec=pltpu.PrefetchScalarGridSpec(
            num_scalar_prefetch=1, grid=(r // br, c // bc),
            in_specs=[pl.BlockSpec((None, br, bc), lambda i, j, chip_ref: (layer, i, j))],
            out_specs=pl.BlockSpec((None, br, bc), lambda i, j, chip_ref: (chip_ref[0], i, j)),
        ),
        out_shape=jax.ShapeDtypeStruct((N_CHIPS, r, c), BF16), compiler_params=_cp("parallel", "parallel"),
    )(chip_idx, w)


def _pair_add(dw, recv, c_idx, name):
    _, r, c = dw.shape
    hr = r // 2
    br, bc = _pick(hr, 256, ROWS), _pick(c, 1024)
    nb = hr // br

    def body(c_ref, a_ref, b_ref, o_ref):
        o_ref[...] = (a_ref[...].astype(F32) + b_ref[...].astype(F32)).astype(BF16)

    return pl.pallas_call(
        body, name=name,
        grid_spec=pltpu.PrefetchScalarGridSpec(
            num_scalar_prefetch=1, grid=(N_CHIPS, nb, c // bc),
            in_specs=[
                pl.BlockSpec((None, br, bc), lambda s, i, j, c_ref: (s, c_ref[0] * nb + i, j)),
                pl.BlockSpec((None, br, bc), lambda s, i, j, c_ref: (s, i, j)),
            ],
            out_specs=pl.BlockSpec((None, br, bc), lambda s, i, j, c_ref: (s, i, j)),
        ),
        out_shape=jax.ShapeDtypeStruct((N_CHIPS, hr, c), BF16),
        compiler_params=_cp("parallel", "parallel", "parallel"),
    )(c_idx, dw, recv)


def _chip_sum(parts, dest, shape, layer, c_idx, name):
    _, hr, c = parts.shape
    br, bc = _pick(hr, 256, ROWS), _pick(c, 1024)
    nb = hr // br

    def body(c_ref, p_ref, *refs):
        o_ref = refs[-1]
        acc = p_ref[0].astype(F32)
        for s in range(1, N_CHIPS):
            acc = acc + p_ref[s].astype(F32)
        o_ref[...] = acc

    in_specs = [pl.BlockSpec((N_CHIPS, br, bc), lambda i, j, c_ref: (0, i, j))]
    operands = [c_idx, parts]
    aliases = {}
    if dest is not None:
        in_specs.append(ANY)
        operands.append(dest)
        aliases = {2: 0}
    return pl.pallas_call(
        body, name=name,
        grid_spec=pltpu.PrefetchScalarGridSpec(
            num_scalar_prefetch=1, grid=(nb, c // bc), in_specs=in_specs,
            out_specs=pl.BlockSpec((None, br, bc), lambda i, j, c_ref: (layer, c_ref[0] * nb + i, j)),
        ),
        out_shape=jax.ShapeDtypeStruct(shape, F32), input_output_aliases=aliases,
        compiler_params=_cp("parallel", "parallel"),
    )(*operands)


def _adamw(w, g, m, v, name):
    n_l, r, c = w.shape
    br, bc = _pick(r, 256, ROWS), _pick(c, 1024)

    def body(w_ref, g_ref, m_ref, v_ref, d_ref, mo_ref, vo_ref):
        g = g_ref[...]
        m = ADAM_B1 * m_ref[...] + (1.0 - ADAM_B1) * g
        v = ADAM_B2 * v_ref[...] + (1.0 - ADAM_B2) * (g * g)
        m_hat = m / (1.0 - ADAM_B1 ** ADAM_STEP)
        v_hat = v / (1.0 - ADAM_B2 ** ADAM_STEP)
        d_ref[...] = -ADAM_LR * (m_hat / (jnp.sqrt(v_hat) + ADAM_EPS) + ADAM_WD * w_ref[...])
        mo_ref[...] = m
        vo_ref[...] = v

    blk = pl.BlockSpec((None, br, bc), lambda l, i, j: (l, i, j))
    return pl.pallas_call(
        body, name=name, grid=(n_l, r // br, c // bc), in_specs=[blk] * 4, out_specs=[blk] * 3,
        out_shape=[jax.ShapeDtypeStruct(w.shape, F32)] * 3, compiler_params=_cp("parallel", "parallel", "parallel"),
    )(w, g, m, v)


def _place():
    x, y, c = lax.axis_index("x"), lax.axis_index("y"), lax.axis_index("c")
    chips = [(1 - x, y), (x, 1 - y), (1 - x, 1 - y)]
    return x, y, c, chips


class _Ride:
    def __init__(self, reads, bufs, new, n_sems, start, finish):
        self.reads, self.bufs, self.new, self.n_sems, self.start, self.finish = reads, bufs, new, n_sems, start, finish


def _hosted_call(body, operands, *, name, grid, in_specs, out_specs, out_shape, scratch_shapes=(), semantics=(), rides=()):
    single = not isinstance(out_shape, (list, tuple))
    out_specs = [out_specs] if single else list(out_specs)
    out_shape = [out_shape] if single else list(out_shape)
    in_specs, scratch_shapes = list(in_specs), list(scratch_shapes)
    n_in, n_out, n_scr = len(in_specs), len(out_shape), len(scratch_shapes)
    extra_in, extra_out, aliases, where = [], [], {}, []
    for ride in rides:
        r0 = len(extra_in)
        extra_in += list(ride.reads)
        b0 = len(extra_in)
        extra_in += list(ride.bufs)
        ob0 = len(extra_out)
        extra_out += [jax.ShapeDtypeStruct(b.shape, b.dtype) for b in ride.bufs]
        for t in range(len(ride.bufs)):
            aliases[n_in + b0 + t] = n_out + ob0 + t
        on0 = len(extra_out)
        extra_out += list(ride.new)
        where.append((r0, len(ride.reads), ob0, len(ride.bufs), on0, len(ride.new)))
    n_ein, n_eout = len(extra_in), len(extra_out)
    sem_shapes = [pltpu.SemaphoreType.DMA((max(1, k),)) for ride in rides for k in ride.n_sems]

    def full_body(*refs):
        ins, outs, scr = refs[:n_in + n_ein], refs[n_in + n_ein:n_in + n_ein + n_out + n_eout], refs[n_in + n_ein + n_out + n_eout:]

        def run(which):
            for idx, (ride, (r0, nr, ob0, nb, on0, nn)) in enumerate(zip(rides, where)):
                fn = ride.start if which == 0 else ride.finish
                fn(ins[n_in + r0:n_in + r0 + nr], outs[n_out + ob0:n_out + ob0 + nb], outs[n_out + on0:n_out + on0 + nn],
                   *scr[n_scr + 3 * idx:n_scr + 3 * idx + 3])

        host = lambda: body(*ins[:n_in], *outs[:n_out], *scr[:n_scr])
        if not rides:
            host()
        elif not grid:
            run(0)
            host()
            run(1)
        else:
            ids = [pl.program_id(ax) for ax in range(len(grid))]
            first = functools.reduce(jnp.logical_and, [i == 0 for i in ids])
            last = functools.reduce(jnp.logical_and, [i == g - 1 for i, g in zip(ids, grid)])
            pl.when(first)(lambda: run(0))
            host()
            pl.when(last)(lambda: run(1))

    if rides:
        params = pltpu.CompilerParams(dimension_semantics=("arbitrary",) * len(grid), vmem_limit_bytes=VMEM_LIMIT)
    else:
        params = _cp(*semantics)
    outs = pl.pallas_call(
        full_body, name=name, grid=grid,
        in_specs=in_specs + [ANY] * n_ein, out_specs=out_specs + [ANY] * n_eout,
        out_shape=out_shape + extra_out, input_output_aliases=aliases,
        scratch_shapes=scratch_shapes + sem_shapes, compiler_params=params,
    )(*operands, *extra_in)
    main = outs[0] if single else list(outs[:n_out])
    rode = [(list(outs[n_out + ob0:n_out + ob0 + nb]), list(outs[n_out + on0:n_out + on0 + nn]))
            for (_, _, ob0, nb, on0, nn) in where]
    return main, rode


def _run_rides(rides, name):
    return _hosted_call(lambda: None, [], name=name, grid=(), in_specs=[], out_specs=[], out_shape=[], rides=rides)[1]


def _ride_gather(slots):
    n = len(slots)
    halves = [a.shape[1] // 2 for a in slots]
    for a, hr in zip(slots, halves):
        assert a.shape[1] == 2 * hr and hr % (2 * SUBLANE) == 0, a.shape

    def remote(bufs, send_sems, recv_sems, i, k, slot, core, to):
        rows = bufs[i].at[slot, pl.ds(pl.multiple_of(core * halves[i], 2 * SUBLANE), halves[i])]
        return pltpu.make_async_remote_copy(
            src_ref=rows, dst_ref=rows, send_sem=send_sems.at[i * 6 + k], recv_sem=recv_sems.at[i * 6 + k],
            device_id=to, device_id_type=MESH)

    def start(reads, bufs, new, send_sems, recv_sems, local_sems):
        x, y, c, chips = _place()
        for i in range(n):
            for k, (px, py) in enumerate(chips):
                remote(bufs, send_sems, recv_sems, i, k, 2 * x + y, c, (px, py, c)).start()

    def finish(reads, bufs, new, send_sems, recv_sems, local_sems):
        x, y, c, chips = _place()
        cp = functools.partial(remote, bufs, send_sems, recv_sems)
        for i in range(n):
            for k, (px, py) in enumerate(chips):
                cp(i, k, 2 * px + py, c, (x, y, c)).wait_recv()
                cp(i, 3 + k, 2 * px + py, c, (x, y, 1 - c)).start()
        for i in range(n):
            for k, (px, py) in enumerate(chips):
                cp(i, 3 + k, 2 * px + py, 1 - c, (x, y, c)).wait_recv()
        for i in range(n):
            for k, (px, py) in enumerate(chips):
                cp(i, k, 2 * x + y, c, (px, py, c)).wait_send()
                cp(i, 3 + k, 2 * px + py, c, (x, y, 1 - c)).wait_send()

    return _Ride([], slots, [], (6 * n, 6 * n, 0), start, finish)


def _ride_swap(grads):
    n = len(grads)
    halves = [a.shape[1] // 2 for a in grads]

    def copies(reads, new, send_sems, recv_sems):
        x, y, c, _ = _place()
        out = []
        for i in range(n):
            rows = pl.ds(pl.multiple_of((1 - c) * halves[i], 2 * SUBLANE), halves[i])
            out.append(pltpu.make_async_remote_copy(
                src_ref=reads[i].at[:, rows, :], dst_ref=new[i], send_sem=send_sems.at[i], recv_sem=recv_sems.at[i],
                device_id=(x, y, 1 - c), device_id_type=MESH))
        return out

    def start(reads, bufs, new, send_sems, recv_sems, local_sems):
        for cp in copies(reads, new, send_sems, recv_sems):
            cp.start()

    def finish(reads, bufs, new, send_sems, recv_sems, local_sems):
        for cp in copies(reads, new, send_sems, recv_sems):
            cp.wait()

    shapes = [jax.ShapeDtypeStruct((N_CHIPS, hr, a.shape[2]), a.dtype) for a, hr in zip(grads, halves)]
    return _Ride(grads, [], shapes, (n, n, 0), start, finish)


def _ride_scatter(parts):
    n = len(parts)

    def own(reads, new, local_sems, i):
        me = 2 * lax.axis_index("x") + lax.axis_index("y")
        return pltpu.make_async_copy(reads[i].at[me], new[i].at[me], local_sems.at[i])

    def send(reads, new, send_sems, recv_sems, i, k):
        x, y, c, chips = _place()
        px, py = chips[k]
        return pltpu.make_async_remote_copy(
            src_ref=reads[i].at[2 * px + py], dst_ref=new[i].at[2 * x + y],
            send_sem=send_sems.at[3 * i + k], recv_sem=recv_sems.at[3 * i + k],
            device_id=(px, py, c), device_id_type=MESH)

    def start(reads, bufs, new, send_sems, recv_sems, local_sems):
        for i in range(n):
            own(reads, new, local_sems, i).start()
            for k in range(3):
                send(reads, new, send_sems, recv_sems, i, k).start()

    def finish(reads, bufs, new, send_sems, recv_sems, local_sems):
        x, y, c, chips = _place()
        for i in range(n):
            for k, (px, py) in enumerate(chips):
                slot = new[i].at[2 * px + py]
                pltpu.make_async_remote_copy(
                    src_ref=slot, dst_ref=slot, send_sem=send_sems.at[3 * i + k], recv_sem=recv_sems.at[3 * i + k],
                    device_id=(x, y, c), device_id_type=MESH).wait_recv()
        for i in range(n):
            for k in range(3):
                send(reads, new, send_sems, recv_sems, i, k).wait_send()
            own(reads, new, local_sems, i).wait()

    shapes = [jax.ShapeDtypeStruct(a.shape, a.dtype) for a in parts]
    return _Ride(parts, [], shapes, (3 * n, 3 * n, n), start, finish)


def _ride_join(grads):
    n = len(grads)

    def copy(bufs, send_sems, recv_sems, i, core, to):
        hr = grads[i].shape[1] // 2
        rows = bufs[i].at[:, pl.ds(pl.multiple_of(core * hr, SUBLANE), hr), :]
        return pltpu.make_async_remote_copy(
            src_ref=rows, dst_ref=rows, send_sem=send_sems.at[i], recv_sem=recv_sems.at[i],
            device_id=to, device_id_type=MESH)

    def start(reads, bufs, new, send_sems, recv_sems, local_sems):
        x, y, c, _ = _place()
        for i in range(n):
            copy(bufs, send_sems, recv_sems, i, c, (x, y, 1 - c)).start()

    def finish(reads, bufs, new, send_sems, recv_sems, local_sems):
        x, y, c, _ = _place()
        for i in range(n):
            copy(bufs, send_sems, recv_sems, i, 1 - c, (x, y, c)).wait_recv()
        for i in range(n):
            copy(bufs, send_sems, recv_sems, i, c, (x, y, 1 - c)).wait_send()

    return _Ride([], grads, [], (n, n, 0), start, finish)


def _all_reduce_small(packed, name):
    r, c = packed.shape
    chunk = _pick(r, 256, ROWS)

    def body(x_ref, out_ref, gath, send_sems, recv_sems, local_sem):
        x, y, cc, chips = _place()
        me, sibling = (x, y, cc), (x, y, 1 - cc)

        def slot(px, py, pc):
            return gath.at[4 * px + 2 * py + pc]

        def copy(k, block, to, src=None):
            return pltpu.make_async_remote_copy(
                src_ref=slot(*block) if src is None else src, dst_ref=slot(*block),
                send_sem=send_sems.at[k], recv_sem=recv_sems.at[k], device_id=to, device_id_type=MESH)

        mine = pltpu.make_async_copy(x_ref, slot(*me), local_sem)
        mine.start()
        first = [copy(0, me, sibling, src=x_ref)]
        first += [copy(1 + j, me, (*chip, cc), src=x_ref) for j, chip in enumerate(chips)]
        for cp in first:
            cp.start()
        passed = [copy(4 + j, (*chip, cc), sibling) for j, chip in enumerate(chips)]
        for j, chip in enumerate(chips):
            copy(1 + j, (*chip, cc), me).wait_recv()
            passed[j].start()
        copy(0, sibling, me).wait_recv()
        for j, chip in enumerate(chips):
            copy(4 + j, (*chip, 1 - cc), me).wait_recv()
        for cp in first + passed:
            cp.wait_send()
        mine.wait()

        def add(i, carry):
            rows = pl.ds(pl.multiple_of(i * chunk, SUBLANE), chunk)
            acc = gath[0, rows, :]
            for dev in range(1, N_DEV):
                acc = acc + gath[dev, rows, :]
            out_ref[rows, :] = acc
            return carry

        lax.fori_loop(0, r // chunk, add, 0)

    return pl.pallas_call(
        body, name=name, in_specs=[VMEM_SPEC], out_specs=VMEM_SPEC,
        out_shape=jax.ShapeDtypeStruct((r, c), F32),
        scratch_shapes=[pltpu.VMEM((N_DEV, r, c), F32), pltpu.SemaphoreType.DMA((7,)),
                        pltpu.SemaphoreType.DMA((7,)), pltpu.SemaphoreType.DMA],
        compiler_params=pltpu.CompilerParams(vmem_limit_bytes=VMEM_LIMIT),
    )(packed)


_PACK_ROWS = 256


def _pack(arrays):
    flat = jnp.concatenate([a.reshape(-1).astype(F32) for a in arrays])
    unit = _PACK_ROWS * LANE
    total = -(-flat.shape[0] // unit) * unit
    return jnp.pad(flat, (0, total - flat.shape[0])).reshape(-1, LANE)


def _unpack(packed, shapes, lead=()):
    flat = packed.reshape(lead + (-1,))
    out, at = [], 0
    for s in shapes:
        size = math.prod(s)
        out.append(flat[..., at:at + size].reshape(lead + tuple(s)))
        at += size
    return out


def kernel(x, pre_mix_g, post_mix_g, pre_ffn_g, post_ffn_g, a_w_in, a_v_norm_g, a_w_spatial, a_b_spatial, a_w_out, kv_norm_g, w_k, w_v, b_w_q, b_w_o, ffn_w_up, ffn_conv_w, ffn_conv_b, ffn_w_down, loss_target, m_pre_mix_g, m_post_mix_g, m_pre_ffn_g, m_post_ffn_g, m_a_w_in, m_a_v_norm_g, m_a_w_spatial, m_a_b_spatial, m_a_w_out, m_kv_norm_g, m_w_k, m_w_v, m_b_w_q, m_b_w_o, m_ffn_w_up, m_ffn_conv_w, m_ffn_conv_b, m_ffn_w_down, v_pre_mix_g, v_post_mix_g, v_pre_ffn_g, v_post_ffn_g, v_a_w_in, v_a_v_norm_g, v_a_w_spatial, v_a_b_spatial, v_a_w_out, v_kv_norm_g, v_w_k, v_w_v, v_b_w_q, v_b_w_o, v_ffn_w_up, v_ffn_conv_w, v_ffn_conv_b, v_ffn_w_down):
    xi, yi, ci = lax.axis_index("x"), lax.axis_index("y"), lax.axis_index("c")
    chip = 2 * xi + yi
    c_idx = jnp.reshape(ci, (1,)).astype(jnp.int32)
    _, s, d = x.shape
    n_layers = pre_mix_g.shape[0]
    assert n_layers == 2 and a_w_in.shape[0] == 1 and b_w_q.shape[0] == 1
    d_a = a_w_out.shape[1] * N_CHIPS
    n_g = a_w_spatial.shape[1]
    ns = ffn_w_up.shape[2]
    assert a_w_spatial.shape[2] == TILE and d_a == n_g * TILE and s % TILE == 0
    h0 = x[0]
    target = loss_target[0]

    big = {
        "win": (a_w_in, m_a_w_in, v_a_w_in),
        "wout": (a_w_out, m_a_w_out, v_a_w_out),
        "wk": (w_k[None], m_w_k[None], v_w_k[None]),
        "wv": (w_v[None], m_w_v[None], v_w_v[None]),
        "wq": (b_w_q, m_b_w_q, v_b_w_q),
        "wo": (b_w_o, m_b_w_o, v_b_w_o),
        "wup": (ffn_w_up, m_ffn_w_up, v_ffn_w_up),
        "wdn": (ffn_w_down, m_ffn_w_down, v_ffn_w_down),
    }
    units = [(nm, layer) for nm in big for layer in range(big[nm][0].shape[0])]
    chip_idx = jnp.reshape(chip, (1,)).astype(jnp.int32)
    shards = [_cast_bf16(big[nm][0], layer, chip_idx, f"cast_{nm}{layer}") for nm, layer in units]
    small_sharded = _pack([a_v_norm_g, ffn_conv_w])
    small_sharded = lax.dynamic_update_index_in_dim(
        jnp.zeros((N_CHIPS,) + small_sharded.shape, F32), small_sharded, chip, 0)
    own = dict(zip(units, shards))
    full = {}

    def gather_ride(keys):
        return _ride_gather([own[key] for key in keys])

    def gathered(keys, rode):
        full.update(zip(keys, rode[0]))

    first_keys = [("win", 0), ("wout", 0), ("wup", 0)]
    (first_bufs, _), = _run_rides([_ride_gather([own[key] for key in first_keys] + [small_sharded])], "gather_first")
    full.update(zip(first_keys, first_bufs[:-1]))
    vg_parts, cw_parts = _unpack(first_bufs[-1], [a_v_norm_g.shape, ffn_conv_w.shape], lead=(N_CHIPS,))
    v_g = jnp.transpose(vg_parts, (1, 0, 2)).reshape(1, d_a)

    def rows(nm, layer=0):
        w = full[(nm, layer)]
        return w.reshape(w.shape[0] * w.shape[1], w.shape[2])

    gains = lambda g, layer: g[layer:layer + 1]
    bias = jnp.repeat(a_b_spatial[0].T, TILE, axis=1)
    w_s = a_w_spatial[0]
    kv_g = kv_norm_g[None]
    conv_w = [cw_parts[:, layer] for layer in range(n_layers)]
    conv_b = [ffn_conv_b[layer].reshape(N_CHIPS, 1, ns) for layer in range(n_layers)]

    def ffn_fwd(hn, layer, up_keys=(), down_keys=()):
        a = _mm(hn, full[("wup", layer)], "nn", f"ffn_up{layer}", out_split=N_CHIPS,
                rides=[gather_ride(up_keys)] if up_keys else ())
        if up_keys:
            a, (rode,) = a
            gathered(up_keys, rode)
        hm = _ffn_act_fwd(a, conv_w[layer], conv_b[layer], f"ffn_act{layer}")
        f = _mm(hm, rows("wdn", layer), "nn", f"ffn_down{layer}", rides=[gather_ride(down_keys)] if down_keys else ())
        if down_keys:
            f, (rode,) = f
            gathered(down_keys, rode)
        return a, hm, f[0]

    hn0 = _rms_fwd(h0, gains(pre_mix_g, 0), "norm_in")
    uv = _mm(hn0, full[("win", 0)], "nn", "gmlp_in", out_split=N_CHIPS)
    gm = _gmlp_fwd(uv, v_g, w_s, bias, "gmlp_gate")
    mix0 = _mm(gm, rows("wout"), "nn", "gmlp_out")[0]
    h1, hn1 = _resid_rms(h0, mix0, gains(post_mix_g, 0), [gains(pre_ffn_g, 0)], "resid_mix0")
    a0, hm0, f0 = ffn_fwd(hn1, 0, up_keys=[("wdn", 0), ("wq", 0), ("wk", 0)], down_keys=[("wv", 0), ("wo", 0)])
    h2, hn2, kvn = _resid_rms(h1, f0, gains(post_ffn_g, 0), [gains(pre_mix_g, 1), kv_g], "resid_ffn0")
    q = _mm(hn2, rows("wq"), "nn", "proj_q", out_dtype=BF16)[0]
    k = _mm(kvn, rows("wk"), "nn", "proj_k", out_dtype=BF16)[0]
    v = _mm(kvn, rows("wv"), "nn", "proj_v", out_dtype=BF16)[0]
    last_keys = [("wup", 1), ("wdn", 1)]
    (att, lsum), (rode,) = _attn_fwd(q, k, v, "attn_fwd", rides=[gather_ride(last_keys)])
    gathered(last_keys, rode)
    mix1 = _mm(att, rows("wo"), "nn", "proj_o")[0]
    h3, hn3 = _resid_rms(h2, mix1, gains(post_mix_g, 1), [gains(pre_ffn_g, 1)], "resid_mix1")
    a1, hm1, f1 = ffn_fwd(hn3, 1)
    dh4, loss_tile = _loss_head(h3, f1, gains(post_ffn_g, 1), target, "loss_head")
    loss = lax.psum(loss_tile[0, 0], ("x", "y", "c"))

    dw = {}
    dg = {}

    pair = {}
    half_done = {nm: None for nm in big}

    def swap_ride(keys):
        return _ride_swap([dw[key] for key in keys])

    def swapped(keys, rode):
        for (nm, layer), got in zip(keys, rode[1]):
            pair[(nm, layer)] = _pair_add(dw[(nm, layer)], got, c_idx, f"pair_add_{nm}{layer}")

    def scatter_ride(keys):
        return _ride_scatter([pair[key] for key in keys])

    def scattered(keys, rode):
        for (nm, layer), got in zip(keys, rode[1]):
            half_done[nm] = _chip_sum(got, half_done[nm], big[nm][0].shape, layer, c_idx, f"chip_sum_{nm}{layer}")

    def ffn_bwd(dh_out, h_in, hn, a, hm, f, layer, act_rides=()):
        df, dg[("post_ffn", layer)] = _rms_bwd_out(dh_out, f, gains(post_ffn_g, layer), f"d_norm_ffn_out{layer}")
        dwd = _mm(hm, df, "tn", f"d_w_down{layer}", out_dtype=BF16)[0]
        dw[("wdn", layer)] = dwd.reshape(N_CHIPS, dwd.shape[0] // N_CHIPS, d)
        dhm = _mm(df, rows("wdn", layer), "nt", f"d_ffn_mid{layer}", out_split=2)
        (da, dg[("conv_w", layer)], dg[("conv_b", layer)]), act_rode = _ffn_act_bwd(
            a, dhm, conv_w[layer], conv_b[layer], f"d_ffn_act{layer}", rides=act_rides)
        dw[("wup", layer)] = _mm(hn, da, "tn", f"d_w_up{layer}", out_dtype=BF16, out_split=N_CHIPS)
        keys = [("wdn", layer), ("wup", layer)]
        dhn, (rode,) = _mm(da, full[("wup", layer)], "nt", f"d_ffn_in{layer}", rides=[swap_ride(keys)])
        swapped(keys, rode)
        return dhn[0], act_rode

    dhn3, _ = ffn_bwd(dh4, h3, hn3, a1, hm1, f1, 1)
    dh3, (dg[("pre_ffn", 1)],) = _rms_bwd_in(dh4, h3, [([dhn3], gains(pre_ffn_g, 1))], "d_norm_ffn_in1")
    dmix1, dg[("post_mix", 1)] = _rms_bwd_out(dh3, mix1, gains(post_mix_g, 1), "d_norm_mix_out1")
    dwo = _mm(att, dmix1, "tn", "d_w_o", out_dtype=BF16)[0]
    dw[("wo", 0)] = dwo.reshape(N_CHIPS, dwo.shape[0] // N_CHIPS, d)
    datt = _mm(dmix1, rows("wo"), "nt", "d_attn_out", out_dtype=BF16)[0]
    ffn1_keys = [("wdn", 1), ("wup", 1)]
    (dq, dk, dv), (rode,) = _attn_bwd(q, k, v, datt, lsum, "attn_bwd", rides=[scatter_ride(ffn1_keys)])
    scattered(ffn1_keys, rode)
    for nm, act, dact in (("wq", hn2, dq), ("wk", kvn, dk), ("wv", kvn, dv)):
        g = _mm(act, dact, "tn", f"d_{nm}", out_dtype=BF16)[0]
        dw[(nm, 0)] = g.reshape(N_CHIPS, g.shape[0] // N_CHIPS, g.shape[1])
    dhn2 = _mm(dq, rows("wq"), "nt", "d_q_in")[0]
    dkvn_k = _mm(dk, rows("wk"), "nt", "d_k_in")[0]
    attn_keys = [("wo", 0), ("wq", 0), ("wk", 0), ("wv", 0)]
    dkvn_v, (rode,) = _mm(dv, rows("wv"), "nt", "d_v_in", rides=[swap_ride(attn_keys)])
    swapped(attn_keys, rode)
    dh2, (dg[("pre_mix", 1)], dg["kv"]) = _rms_bwd_in(
        dh3, h2, [([dhn2], gains(pre_mix_g, 1)), ([dkvn_k, dkvn_v[0]], kv_g)], "d_norm_mix_in1")
    dhn1, (rode,) = ffn_bwd(dh2, h1, hn1, a0, hm0, f0, 0, act_rides=[scatter_ride(attn_keys)])
    scattered(attn_keys, rode)
    dh1, (dg[("pre_ffn", 0)],) = _rms_bwd_in(dh2, h1, [([dhn1], gains(pre_ffn_g, 0))], "d_norm_ffn_in0")
    dmix0, dg[("post_mix", 0)] = _rms_bwd_out(dh1, mix0, gains(post_mix_g, 0), "d_norm_mix_out0")
    dwout = _mm(gm, dmix0, "tn", "d_w_out", out_dtype=BF16)[0]
    dw[("wout", 0)] = dwout.reshape(N_CHIPS, dwout.shape[0] // N_CHIPS, d)
    dgm = _mm(dmix0, rows("wout"), "nt", "d_gmlp_gate")[0]
    duv, d_ws, d_bs, d_vg = _gmlp_bwd(uv, dgm, v_g, w_s, bias, "d_gmlp")
    dw[("win", 0)] = _mm(hn0, duv, "tn", "d_w_in", out_dtype=BF16, out_split=N_CHIPS)
    gmlp_keys = [("wout", 0), ("win", 0)]
    dhn0, (rode,) = _mm(duv, full[("win", 0)], "nt", "d_gmlp_in", rides=[swap_ride(gmlp_keys)])
    swapped(gmlp_keys, rode)
    dx, (dg[("pre_mix", 0)],) = _rms_bwd_in(dh1, h0, [([dhn0[0]], gains(pre_mix_g, 0))], "d_norm_in")
    tail_keys = [("wdn", 0), ("wup", 0)] + gmlp_keys
    scattered(tail_keys, _run_rides([scatter_ride(tail_keys)], "grads_to_owner")[0])
    (joined, _), = _run_rides([_ride_join([half_done[nm] for nm in big])], "grads_join")
    grads_big = dict(zip(big, joined))

    stack = lambda key: jnp.concatenate([dg[(key, layer)] for layer in range(n_layers)], axis=0)
    small_parts = [
        stack("pre_mix"), stack("post_mix"), stack("pre_ffn"), stack("post_ffn"),
        d_vg, d_ws, d_bs[::SUBLANE], dg["kv"],
        jnp.stack([dg[("conv_w", layer)] for layer in range(n_layers)]),
        jnp.stack([dg[("conv_b", layer)] for layer in range(n_layers)]),
    ]
    summed = _all_reduce_small(_pack(small_parts), "small_grads_sum")
    (g_pre_mix, g_post_mix, g_pre_ffn, g_post_ffn, g_vg, g_ws, g_bs, g_kv, g_cw, g_cb) = _unpack(
        summed, [p.shape for p in small_parts])
    g_vg = lax.dynamic_index_in_dim(g_vg.reshape(N_CHIPS, 1, d_a // N_CHIPS), chip, 0, keepdims=False)
    g_cw = lax.dynamic_index_in_dim(g_cw, chip, 1, keepdims=False)
    g_cb = g_cb.reshape(n_layers, N_CHIPS * ns)
    small = [
        (pre_mix_g, g_pre_mix, m_pre_mix_g, v_pre_mix_g),
        (post_mix_g, g_post_mix, m_post_mix_g, v_post_mix_g),
        (pre_ffn_g, g_pre_ffn, m_pre_ffn_g, v_pre_ffn_g),
        (post_ffn_g, g_post_ffn, m_post_ffn_g, v_post_ffn_g),
        (a_v_norm_g, g_vg, m_a_v_norm_g, v_a_v_norm_g),
        (a_w_spatial, g_ws[None], m_a_w_spatial, v_a_w_spatial),
        (a_b_spatial, g_bs[None], m_a_b_spatial, v_a_b_spatial),
        (kv_norm_g, g_kv.reshape(d), m_kv_norm_g, v_kv_norm_g),
        (ffn_conv_w, g_cw, m_ffn_conv_w, v_ffn_conv_w),
        (ffn_conv_b, g_cb, m_ffn_conv_b, v_ffn_conv_b),
    ]
    small = [(w, g.reshape(w.shape), m, v) for w, g, m, v in small]
    packed = [_pack([t[i] for t in small])[None] for i in range(4)]
    small_new = [_unpack(p[0], [t[0].shape for t in small]) for p in _adamw(*packed, "adamw_small")]

    new_big = {nm: _adamw(big[nm][0], grads_big[nm], big[nm][1], big[nm][2], f"adamw_{nm}") for nm in big}

    def big_out(nm, which):
        ref_shape = {"wk": w_k.shape, "wv": w_v.shape}.get(nm, big[nm][0].shape)
        arr = grads_big[nm] if which == 0 else new_big[nm][which - 1]
        return arr.reshape(ref_shape)

    order = ["pre_mix", "post_mix", "pre_ffn", "post_ffn", "win", "vg", "ws", "bs", "wout", "kv", "wk", "wv", "wq",
             "wo", "wup", "cw", "cb", "wdn"]
    small_at = {"pre_mix": 0, "post_mix": 1, "pre_ffn": 2, "post_ffn": 3, "vg": 4, "ws": 5, "bs": 6, "kv": 7,
                "cw": 8, "cb": 9}
    outs = [loss, dx[None]]
    for which in range(4):
        for nm in order:
            if nm in small_at:
                outs.append(small[small_at[nm]][1] if which == 0 else small_new[which - 1][small_at[nm]])
            else:
                outs.append(big_out(nm, which))
    return tuple(outs)
```

```python
import functools
import math

import jax
import jax.numpy as jnp
from jax import lax
from jax.experimental import pallas as pl
from jax.experimental.pallas import tpu as pltpu

F32 = jnp.float32
BF16 = jnp.bfloat16
EPS = 1e-6
ADAM_LR = 0.001
ADAM_B1 = 0.9
ADAM_B2 = 0.999
ADAM_EPS = 1e-08
ADAM_WD = 0.01
ADAM_STEP = 10

LANE = 128
SUBLANE = 8
ROWS = 16
TILE = 128
N_CHIPS = 4
N_DEV = 8
VMEM_LIMIT = 56 * 1024 * 1024
MM_VMEM = 40 * 1024 * 1024
MESH = pl.DeviceIdType.MESH
ANY = pl.BlockSpec(memory_space=pl.ANY)
VMEM_SPEC = pl.BlockSpec(memory_space=pltpu.VMEM)


def _cp(*sem):
    return pltpu.CompilerParams(dimension_semantics=sem, vmem_limit_bytes=VMEM_LIMIT)


def _pick(dim, pref, align=LANE):
    if dim <= pref:
        return dim
    best = None
    for d in range(align, pref + 1, align):
        if dim % d == 0:
            best = d
    assert best is not None, (dim, pref)
    return best


_DIMS = {
    "nn": (((1,), (0,)), ((), ())),
    "nt": (((1,), (1,)), ((), ())),
    "tn": (((0,), (0,)), ((), ())),
}


def _as3(a):
    return a if a.ndim == 3 else a[None]


def _spec3(br, bc, cols_j, rc):
    per = cols_j // bc

    def imap(m, n, k):
        r, c = rc(m, n, k)
        return (c // per, r, c % per)

    return pl.BlockSpec((None, br, bc), imap)


def _mm(a, b, mode, name, out_dtype=F32, out_split=1, rides=()):
    a, b = _as3(a), _as3(b)
    ja, ra, caj = a.shape
    jb, rb, cbj = b.shape
    if mode == "nn":
        m, k, n = ra, ja * caj, jb * cbj
        assert rb == k
        m_ext, k_ext, n_ext = [ra], [caj, rb], [cbj]
    elif mode == "nt":
        m, k, n = ra, ja * caj, rb
        assert jb * cbj == k
        m_ext, k_ext, n_ext = [ra], [caj, cbj], [rb]
    else:
        m, k, n = ja * caj, ra, jb * cbj
        assert rb == k
        m_ext, k_ext, n_ext = [caj], [ra], [cbj]
    assert n % out_split == 0
    n_ext.append(n // out_split)
    bm = _pick(math.gcd(*m_ext), 1536)
    bn = _pick(math.gcd(*n_ext), 1536)
    k_unit = math.gcd(*k_ext)
    o_bytes = jnp.dtype(out_dtype).itemsize

    def vmem_need(bk):
        tiles = bm * bk * a.dtype.itemsize + bk * bn * b.dtype.itemsize + bm * bn * o_bytes
        return 2 * tiles + (bm * bn * 4 if bk < k else 0)

    bk = max(d for d in range(LANE, k_unit + 1, LANE) if k_unit % d == 0 and (d == LANE or vmem_need(d) <= MM_VMEM))
    nk = k // bk
    if mode == "nn":
        a_spec = _spec3(bm, bk, caj, lambda mi, ni, ki: (mi, ki))
        b_spec = _spec3(bk, bn, cbj, lambda mi, ni, ki: (ki, ni))
    elif mode == "nt":
        a_spec = _spec3(bm, bk, caj, lambda mi, ni, ki: (mi, ki))
        b_spec = _spec3(bn, bk, cbj, lambda mi, ni, ki: (ni, ki))
    else:
        a_spec = _spec3(bk, bm, caj, lambda mi, ni, ki: (ki, mi))
        b_spec = _spec3(bk, bn, cbj, lambda mi, ni, ki: (ki, ni))
    o_spec = _spec3(bm, bn, n // out_split, lambda mi, ni, ki: (mi, ni))
    dims = _DIMS[mode]

    def body(a_ref, b_ref, o_ref, *acc):
        def part():
            return lax.dot_general(a_ref[...].astype(BF16), b_ref[...].astype(BF16), dims, preferred_element_type=F32)

        if nk == 1:
            o_ref[...] = part().astype(o_ref.dtype)
            return
        acc_ref, = acc
        ki = pl.program_id(2)

        @pl.when(ki == 0)
        def _():
            acc_ref[...] = part()

        @pl.when(jnp.logical_and(ki > 0, ki < nk - 1))
        def _():
            acc_ref[...] += part()

        @pl.when(ki == nk - 1)
        def _():
            o_ref[...] = (acc_ref[...] + part()).astype(o_ref.dtype)

    out, rode = _hosted_call(
        body, [a, b], name=name, grid=(m // bm, n // bn, nk), in_specs=[a_spec, b_spec], out_specs=o_spec,
        out_shape=jax.ShapeDtypeStruct((out_split, m, n // out_split), out_dtype),
        scratch_shapes=[pltpu.VMEM((bm, bn), F32)] if nk > 1 else [],
        semantics=("parallel", "parallel", "arbitrary"), rides=rides)
    return (out, rode) if rides else out


def _rms(x, g):
    r = lax.rsqrt(jnp.mean(x * x, axis=-1, keepdims=True) + EPS)
    return x * r * g


def _rms_bwd(x, g, dy):
    r = lax.rsqrt(jnp.mean(x * x, axis=-1, keepdims=True) + EPS)
    xh = x * r
    gy = dy * g
    dx = r * (gy - xh * jnp.mean(gy * xh, axis=-1, keepdims=True))
    return dx, jnp.sum(dy * xh, axis=0, keepdims=True)


def _row_block(s):
    return _pick(s, 256, ROWS)


def _rms_fwd(h, g, name):
    s, d = h.shape
    br = _row_block(s)

    def body(h_ref, g_ref, o_ref):
        o_ref[...] = _rms(h_ref[...], g_ref[...]).astype(BF16)

    row = pl.BlockSpec((br, d), lambda i: (i, 0))
    vec = pl.BlockSpec((1, d), lambda i: (0, 0))
    return pl.pallas_call(
        body, name=name, grid=(s // br,), in_specs=[row, vec], out_specs=row,
        out_shape=jax.ShapeDtypeStruct((s, d), BF16), compiler_params=_cp("parallel"),
    )(h, g)


def _resid_rms(h_in, f, g_post, g_next, name, rides=()):
    s, d = h_in.shape
    br = _row_block(s)
    n_next = len(g_next)

    def body(h_ref, f_ref, gp_ref, *refs):
        gn_refs, ho_ref, hn_refs = refs[:n_next], refs[n_next], refs[n_next + 1:]
        h = h_ref[...] + _rms(f_ref[...], gp_ref[...])
        ho_ref[...] = h
        for gn_ref, hn_ref in zip(gn_refs, hn_refs):
            hn_ref[...] = _rms(h, gn_ref[...]).astype(BF16)

    row = pl.BlockSpec((br, d), lambda i: (i, 0))
    vec = pl.BlockSpec((1, d), lambda i: (0, 0))
    outs, rode = _hosted_call(
        body, [h_in, f, g_post, *g_next], name=name, grid=(s // br,),
        in_specs=[row, row, vec] + [vec] * n_next,
        out_specs=[row] * (1 + n_next),
        out_shape=[jax.ShapeDtypeStruct((s, d), F32)] + [jax.ShapeDtypeStruct((s, d), BF16)] * n_next,
        semantics=("parallel",), rides=rides)
    return (outs, rode) if rides else outs


def _loss_head(h_in, f, g_post, target, name):
    s, d = h_in.shape
    br = _row_block(s)

    def body(h_ref, f_ref, gp_ref, t_ref, dh_ref, loss_ref):
        @pl.when(pl.program_id(0) == 0)
        def _():
            loss_ref[...] = jnp.zeros_like(loss_ref)

        diff = h_ref[...] + _rms(f_ref[...], gp_ref[...]) - t_ref[...]
        dh_ref[...] = diff * (1.0 / d)
        loss_ref[...] += 0.5 * jnp.sum(jnp.mean(diff * diff, axis=-1, keepdims=True))

    row = pl.BlockSpec((br, d), lambda i: (i, 0))
    vec = pl.BlockSpec((1, d), lambda i: (0, 0))
    return pl.pallas_call(
        body, name=name, grid=(s // br,),
        in_specs=[row, row, vec, row],
        out_specs=[row, pl.BlockSpec((SUBLANE, LANE), lambda i: (0, 0))],
        out_shape=[jax.ShapeDtypeStruct((s, d), F32), jax.ShapeDtypeStruct((SUBLANE, LANE), F32)],
        compiler_params=_cp("arbitrary"),
    )(h_in, f, g_post, target)


def _rms_bwd_out(dy, f, g, name):
    s, d = f.shape
    br = _row_block(s)

    def body(dy_ref, f_ref, g_ref, df_ref, dg_ref):
        @pl.when(pl.program_id(0) == 0)
        def _():
            dg_ref[...] = jnp.zeros_like(dg_ref)

        dx, dg = _rms_bwd(f_ref[...], g_ref[...], dy_ref[...])
        df_ref[...] = dx.astype(BF16)
        dg_ref[...] += dg

    row = pl.BlockSpec((br, d), lambda i: (i, 0))
    vec = pl.BlockSpec((1, d), lambda i: (0, 0))
    return pl.pallas_call(
        body, name=name, grid=(s // br,), in_specs=[row, row, vec], out_specs=[row, vec],
        out_shape=[jax.ShapeDtypeStruct((s, d), BF16), jax.ShapeDtypeStruct((1, d), F32)],
        compiler_params=_cp("arbitrary"),
    )(dy, f, g)


def _rms_bwd_in(dh_out, h_in, branches, name, rides=()):
    s, d = h_in.shape
    br = _row_block(s)
    counts = [len(ds) for ds, _ in branches]
    n_d = sum(counts)
    n_b = len(branches)

    def body(dho_ref, h_ref, *refs):
        d_refs, g_refs = refs[:n_d], refs[n_d:n_d + n_b]
        dh_ref, dg_refs = refs[n_d + n_b], refs[n_d + n_b + 1:]

        @pl.when(pl.program_id(0) == 0)
        def _():
            for r in dg_refs:
                r[...] = jnp.zeros_like(r)

        h = h_ref[...]
        acc = dho_ref[...]
        at = 0
        for bi, cnt in enumerate(counts):
            dn = d_refs[at][...]
            for r in d_refs[at + 1:at + cnt]:
                dn = dn + r[...]
            at += cnt
            dx, dg = _rms_bwd(h, g_refs[bi][...], dn)
            acc = acc + dx
            dg_refs[bi][...] += dg
        dh_ref[...] = acc

    row = pl.BlockSpec((br, d), lambda i: (i, 0))
    vec = pl.BlockSpec((1, d), lambda i: (0, 0))
    flat_d = [x for ds, _ in branches for x in ds]
    outs, rode = _hosted_call(
        body, [dh_out, h_in, *flat_d, *[g for _, g in branches]], name=name, grid=(s // br,),
        in_specs=[row, row] + [row] * n_d + [vec] * n_b,
        out_specs=[row] + [vec] * n_b,
        out_shape=[jax.ShapeDtypeStruct((s, d), F32)] + [jax.ShapeDtypeStruct((1, d), F32)] * n_b,
        semantics=("arbitrary",), rides=rides)
    return (outs[0], list(outs[1:]), rode) if rides else (outs[0], list(outs[1:]))


def _split3(x):
    x0 = x.astype(BF16)
    r1 = x - x0.astype(F32)
    x1 = r1.astype(BF16)
    x2 = (r1 - x1.astype(F32)).astype(BF16)
    return x0, x1, x2


def _tri(n, kind):
    r = lax.broadcasted_iota(jnp.int32, (n, n), 0)
    c = lax.broadcasted_iota(jnp.int32, (n, n), 1)
    m = {"lt": r < c, "le": r <= c, "gt": r > c}[kind]
    return jnp.where(m, 1.0, 0.0).astype(BF16)


_GELU_C = math.sqrt(2.0 / math.pi)
_GELU_A = 0.044715


def _gelu(x):
    return 0.5 * x * (1.0 + jnp.tanh(_GELU_C * (x + _GELU_A * (x * x * x))))


def _gelu_grad(x):
    t = jnp.tanh(_GELU_C * (x + _GELU_A * (x * x * x)))
    return 0.5 * (1.0 + t) + 0.5 * x * (1.0 - t * t) * (_GELU_C * (1.0 + 3.0 * _GELU_A * (x * x)))


def _causal_w(w):
    r = lax.broadcasted_iota(jnp.int32, (TILE, TILE), 0)
    c = lax.broadcasted_iota(jnp.int32, (TILE, TILE), 1)
    return jnp.where(c <= r, w, 0.0)


def _uv_tiles(uv_ref, g, d_a, dq):
    cu, cv = g * TILE, d_a + g * TILE
    u = uv_ref[cu // dq, :, pl.ds(cu % dq, TILE)]
    v = uv_ref[cv // dq, :, pl.ds(cv % dq, TILE)]
    return u, v


def _gmlp_fwd(uv, v_g, w_s, bias, name, rides=()):
    _, s, dq = uv.shape
    d_a = 2 * dq
    n_g = d_a // TILE

    def body(uv_ref, vg_ref, ws_ref, b_ref, o_ref):
        for g in range(n_g):
            up, vp = _uv_tiles(uv_ref, g, d_a, dq)
            cols = pl.ds(g * TILE, TILE)
            vn = _rms(_gelu(vp), vg_ref[:, cols])
            mixed = jnp.dot(_causal_w(ws_ref[g]).astype(BF16), vn.astype(BF16), preferred_element_type=F32) + b_ref[:, cols]
            o_ref[:, cols] = (_gelu(up) * mixed).astype(BF16)

    return _hosted_call(
        body, [uv, v_g, w_s, bias], name=name, grid=(s // TILE,),
        in_specs=[
            pl.BlockSpec((4, TILE, dq), lambda i: (0, i, 0)),
            pl.BlockSpec((1, d_a), lambda i: (0, 0)),
            pl.BlockSpec((n_g, TILE, TILE), lambda i: (0, 0, 0)),
            pl.BlockSpec((TILE, d_a), lambda i: (0, 0)),
        ],
        out_specs=pl.BlockSpec((TILE, d_a), lambda i: (i, 0)),
        out_shape=jax.ShapeDtypeStruct((s, d_a), BF16),
        semantics=("parallel",), rides=rides)


def _gmlp_bwd(uv, dgm, v_g, w_s, bias, name, rides=()):
    _, s, dq = uv.shape
    d_a = 2 * dq
    n_g = d_a // TILE
    n_c = s // TILE

    def body(uv_ref, d_ref, vg_ref, ws_ref, b_ref, duv_ref, dws_ref, dbs_ref, dvg_ref, dbias_acc):
        i = pl.program_id(0)

        @pl.when(i == 0)
        def _():
            dws_ref[...] = jnp.zeros_like(dws_ref)
            dvg_ref[...] = jnp.zeros_like(dvg_ref)
            dbias_acc[...] = jnp.zeros_like(dbias_acc)

        for g in range(n_g):
            up, vp = _uv_tiles(uv_ref, g, d_a, dq)
            cols = pl.ds(g * TILE, TILE)
            vg = vg_ref[:, cols]
            u = _gelu(up)
            v = _gelu(vp)
            r = lax.rsqrt(jnp.mean(v * v, axis=-1, keepdims=True) + EPS)
            vh = v * r
            vn = (vh * vg).astype(BF16)
            wc = _causal_w(ws_ref[g]).astype(BF16)
            mixed = jnp.dot(wc, vn, preferred_element_type=F32) + b_ref[:, cols]
            d_out = d_ref[:, cols]
            du = d_out * mixed
            dmixed = d_out * u
            dmb = dmixed.astype(BF16)
            dvn = lax.dot_general(wc, dmb, _DIMS["tn"], preferred_element_type=F32)
            dws_ref[g] += lax.dot_general(dmb, vn, _DIMS["nt"], preferred_element_type=F32)
            dbias_acc[:, cols] += dmixed
            dvg_ref[:, cols] += jnp.sum(dvn * vh, axis=0, keepdims=True)
            gv = dvn * vg
            dv = r * (gv - vh * jnp.mean(gv * vh, axis=-1, keepdims=True))
            cu, cv = g * TILE, d_a + g * TILE
            duv_ref[cu // dq, :, pl.ds(cu % dq, TILE)] = (du * _gelu_grad(up)).astype(BF16)
            duv_ref[cv // dq, :, pl.ds(cv % dq, TILE)] = (dv * _gelu_grad(vp)).astype(BF16)

        @pl.when(i == n_c - 1)
        def _():
            ones = jnp.ones((SUBLANE, TILE), BF16)
            for g in range(n_g):
                dws_ref[g] = _causal_w(dws_ref[g])
                cols = pl.ds(g * TILE, TILE)
                out = None
                for t in _split3(dbias_acc[:, cols]):
                    p = lax.dot_general(ones, t, _DIMS["nt"], preferred_element_type=F32)
                    out = p if out is None else out + p
                dbs_ref[pl.ds(g * SUBLANE, SUBLANE), :] = out

    return _hosted_call(
        body, [uv, dgm, v_g, w_s, bias], name=name, grid=(n_c,), semantics=("arbitrary",), rides=rides,
        in_specs=[
            pl.BlockSpec((4, TILE, dq), lambda i: (0, i, 0)),
            pl.BlockSpec((TILE, d_a), lambda i: (i, 0)),
            pl.BlockSpec((1, d_a), lambda i: (0, 0)),
            pl.BlockSpec((n_g, TILE, TILE), lambda i: (0, 0, 0)),
            pl.BlockSpec((TILE, d_a), lambda i: (0, 0)),
        ],
        out_specs=[
            pl.BlockSpec((4, TILE, dq), lambda i: (0, i, 0)),
            pl.BlockSpec((n_g, TILE, TILE), lambda i: (0, 0, 0)),
            pl.BlockSpec((n_g * SUBLANE, TILE), lambda i: (0, 0)),
            pl.BlockSpec((1, d_a), lambda i: (0, 0)),
        ],
        out_shape=[
            jax.ShapeDtypeStruct((4, s, dq), BF16),
            jax.ShapeDtypeStruct((n_g, TILE, TILE), F32),
            jax.ShapeDtypeStruct((n_g * SUBLANE, TILE), F32),
            jax.ShapeDtypeStruct((1, d_a), F32),
        ],
        scratch_shapes=[pltpu.VMEM((TILE, d_a), F32)])


def _sigmoid(x):
    return 1.0 / (1.0 + jnp.exp(-x))


def _conv3(ext, w, b):
    return b + ((w[0:1] * pltpu.roll(ext, 2, 0) + w[1:2] * pltpu.roll(ext, 1, 0)) + w[2:3] * ext)


def _act_blocks(s, ns):
    return _pick(s, 512, ROWS), _pick(ns, 256)


def _ffn_act_fwd(a, cw, cb, name):
    _, s, ns = a.shape
    bs, cb_w = _act_blocks(s, ns)
    hb = bs // SUBLANE

    def body(a_ref, prev_ref, cw_ref, cb_ref, o_ref):
        first = pl.program_id(0) == 0

        def conv(comp):
            prev = jnp.where(first, 0.0, prev_ref[comp])
            ext = jnp.concatenate([prev, a_ref[comp]], axis=0)
            return _conv3(ext, cw_ref[comp], cb_ref[comp])[SUBLANE:]

        for p in range(2):
            cg = conv(p)
            o_ref[p] = (cg * _sigmoid(cg) * conv(2 + p)).astype(BF16)

    return pl.pallas_call(
        body, name=name, grid=(s // bs, ns // cb_w),
        in_specs=[
            pl.BlockSpec((4, bs, cb_w), lambda i, j: (0, i, j)),
            pl.BlockSpec((4, SUBLANE, cb_w), lambda i, j: (0, jnp.maximum(i * hb - 1, 0), j)),
            pl.BlockSpec((4, 3, cb_w), lambda i, j: (0, 0, j)),
            pl.BlockSpec((4, 1, cb_w), lambda i, j: (0, 0, j)),
        ],
        out_specs=pl.BlockSpec((2, bs, cb_w), lambda i, j: (0, i, j)),
        out_shape=jax.ShapeDtypeStruct((2, s, ns), BF16),
        compiler_params=_cp("parallel", "parallel"),
    )(a, a, cw, cb)


def _ffn_act_bwd(a, dhm, cw, cb, name, rides=()):
    _, s, ns = a.shape
    bs, cb_w = _act_blocks(s, ns)
    hb = bs // SUBLANE
    n_i = s // bs
    n_ext = bs + 2 * SUBLANE
    cur = slice(SUBLANE, SUBLANE + bs)

    def body(a_ref, prev_ref, next_ref, d_ref, dnext_ref, cw_ref, cb_ref, da_ref, dcw_ref, dcb_ref):
        i = pl.program_id(1)
        first, last = i == 0, i == n_i - 1

        @pl.when(first)
        def _():
            dcw_ref[...] = jnp.zeros_like(dcw_ref)
            dcb_ref[...] = jnp.zeros_like(dcb_ref)

        def ext_of(comp):
            return jnp.concatenate([jnp.where(first, 0.0, prev_ref[comp]), a_ref[comp], next_ref[comp]], axis=0)

        def back(comp, a_ext, dc):
            w = cw_ref[comp]
            da = (w[2:3] * dc + w[1:2] * pltpu.roll(dc, n_ext - 1, 0)) + w[0:1] * pltpu.roll(dc, n_ext - 2, 0)
            da_ref[comp] = da[cur].astype(BF16)
            dcc = dc[cur]
            dcw_ref[comp, 0:1, :] += jnp.sum(dcc * pltpu.roll(a_ext, 2, 0)[cur], axis=0, keepdims=True)
            dcw_ref[comp, 1:2, :] += jnp.sum(dcc * pltpu.roll(a_ext, 1, 0)[cur], axis=0, keepdims=True)
            dcw_ref[comp, 2:3, :] += jnp.sum(dcc * a_ext[cur], axis=0, keepdims=True)
            dcb_ref[comp] += jnp.sum(dcc, axis=0, keepdims=True)

        for p in range(2):
            ag, av = ext_of(p), ext_of(2 + p)
            cg = _conv3(ag, cw_ref[p], cb_ref[p])
            cv = _conv3(av, cw_ref[2 + p], cb_ref[2 + p])
            d = jnp.concatenate(
                [jnp.zeros((SUBLANE, cb_w), F32), d_ref[p], jnp.where(last, 0.0, dnext_ref[p])], axis=0)
            sg = _sigmoid(cg)
            back(2 + p, av, d * (cg * sg))
            back(p, ag, d * cv * (sg * (1.0 + cg * (1.0 - sg))))

    return _hosted_call(
        body, [a, a, a, dhm, dhm, cw, cb], name=name, grid=(ns // cb_w, n_i),
        in_specs=[
            pl.BlockSpec((4, bs, cb_w), lambda j, i: (0, i, j)),
            pl.BlockSpec((4, SUBLANE, cb_w), lambda j, i: (0, jnp.maximum(i * hb - 1, 0), j)),
            pl.BlockSpec((4, SUBLANE, cb_w), lambda j, i: (0, jnp.minimum((i + 1) * hb, n_i * hb - 1), j)),
            pl.BlockSpec((2, bs, cb_w), lambda j, i: (0, i, j)),
            pl.BlockSpec((2, SUBLANE, cb_w), lambda j, i: (0, jnp.minimum((i + 1) * hb, n_i * hb - 1), j)),
            pl.BlockSpec((4, 3, cb_w), lambda j, i: (0, 0, j)),
            pl.BlockSpec((4, 1, cb_w), lambda j, i: (0, 0, j)),
        ],
        out_specs=[
            pl.BlockSpec((4, bs, cb_w), lambda j, i: (0, i, j)),
            pl.BlockSpec((4, 3, cb_w), lambda j, i: (0, 0, j)),
            pl.BlockSpec((4, 1, cb_w), lambda j, i: (0, 0, j)),
        ],
        out_shape=[
            jax.ShapeDtypeStruct((4, s, ns), BF16),
            jax.ShapeDtypeStruct((4, 3, ns), F32),
            jax.ShapeDtypeStruct((4, 1, ns), F32),
        ],
        semantics=("parallel", "arbitrary"), rides=rides)


ATT_BQ = 512
ATT_BK = 256


def _att_blocks(s):
    bq = _pick(s, ATT_BQ)
    bk = min(ATT_BK, bq)
    assert bq % bk == 0
    return bq, bk


def _dot_sel2(x, sel):
    hi = x.astype(BF16)
    lo = (x - hi.astype(F32)).astype(BF16)
    n = x.shape[0]
    both = jnp.dot(jnp.concatenate([hi, lo], axis=0), sel, preferred_element_type=F32)
    return both[:n] + both[n:]


def _causal_mask(bq, bk, row0, col0):
    rows = row0 + lax.broadcasted_iota(jnp.int32, (bq, bk), 0)
    cols = col0 + lax.broadcasted_iota(jnp.int32, (bq, bk), 1)
    return cols < rows


def _sb_tile(qb, kb, scale, mask):
    z = lax.dot_general(qb, kb, _DIMS["nt"], preferred_element_type=F32) * scale
    e = jnp.exp(-jnp.abs(z))
    lb = jnp.minimum(z, 0.0) - jnp.log(1.0 + e)
    l1m = lb - z
    if mask is not None:
        l1m = jnp.where(mask, l1m, 0.0)
    return z, e, lb, l1m


def _attn_fwd(q, k, v, name, rides=()):
    s, hd = q.shape
    bq, bk = _att_blocks(s)
    r = bq // bk
    n_h, n_q = hd // TILE, s // bq
    scale = 1.0 / math.sqrt(TILE)

    def body(q_ref, k_ref, v_ref, o_ref, l_ref, acc_ref, suf_ref):
        i = pl.program_id(1)
        qb = q_ref[...]
        later = _tri(bk, "gt")
        acc_ref[...] = jnp.zeros_like(acc_ref)
        suf_ref[...] = jnp.zeros_like(suf_ref)

        def tile(j, masked):
            rows = pl.ds(pl.multiple_of(j * bk, bk), bk)
            mask = _causal_mask(bq, bk, i * bq, j * bk) if masked else None
            _, _, lb, l1m = _sb_tile(qb, k_ref[rows, :], scale, mask)
            a = jnp.exp(lb + _dot_sel2(l1m, later) + suf_ref[...])
            if masked:
                a = jnp.where(mask, a, 0.0)
            acc_ref[...] += jnp.dot(a.astype(BF16), v_ref[rows, :], preferred_element_type=F32)
            suf_ref[...] += jnp.sum(l1m, axis=1, keepdims=True)

        for dgl in range(r - 1, -1, -1):
            tile(r * i + dgl, True)

        def step(t, carry):
            tile(r * i - 1 - t, False)
            return carry

        lax.fori_loop(0, r * i, step, 0)
        o_ref[...] = acc_ref[...].astype(BF16)
        l_ref[...] = jnp.broadcast_to(suf_ref[...], (bq, TILE))

    blk = pl.BlockSpec((bq, TILE), lambda h, i: (i, h))
    head = pl.BlockSpec((s, TILE), lambda h, i: (0, h))
    return _hosted_call(
        body, [q, k, v], name=name, grid=(n_h, n_q), in_specs=[blk, head, head], out_specs=[blk, blk],
        out_shape=[jax.ShapeDtypeStruct((s, hd), BF16), jax.ShapeDtypeStruct((s, hd), F32)],
        scratch_shapes=[pltpu.VMEM((bq, TILE), F32), pltpu.VMEM((bq, 1), F32)],
        semantics=("parallel", "parallel"), rides=rides)


def _attn_bwd(q, k, v, do, lsum, name, rides=()):
    s, hd = q.shape
    bq, bk = _att_blocks(s)
    r = bq // bk
    n_h, n_q = hd // TILE, s // bq
    scale = 1.0 / math.sqrt(TILE)

    def body(q_ref, k_ref, v_ref, do_ref, l_ref, dq_ref, dk_ref, dv_ref, dq_acc, pre_ref, cp_ref):
        i = pl.program_id(1)

        @pl.when(i == 0)
        def _():
            dk_ref[...] = jnp.zeros_like(dk_ref)
            dv_ref[...] = jnp.zeros_like(dv_ref)

        qb = q_ref[...]
        dob = do_ref[...]
        upto = _tri(bk, "le")
        before = _tri(bk, "lt")
        dq_acc[...] = jnp.zeros_like(dq_acc)
        pre_ref[...] = jnp.zeros_like(pre_ref)
        cp_ref[...] = jnp.zeros_like(cp_ref)

        def tile(j, masked):
            rows = pl.ds(pl.multiple_of(j * bk, bk), bk)
            kb, vb = k_ref[rows, :], v_ref[rows, :]
            mask = _causal_mask(bq, bk, i * bq, j * bk) if masked else None
            z, e, lb, l1m = _sb_tile(qb, kb, scale, mask)
            suffix = (l_ref[:, 0:1] - pre_ref[...]) - _dot_sel2(l1m, upto)
            a = jnp.exp(lb + suffix)
            if masked:
                a = jnp.where(mask, a, 0.0)
            p = a * lax.dot_general(dob, vb, _DIMS["nt"], preferred_element_type=F32)
            cprev = cp_ref[...] + _dot_sel2(p, before)
            inv = pl.reciprocal(1.0 + e, approx=True)
            pos = z >= 0.0
            dz = p * (jnp.where(pos, e, 1.0) * inv) - cprev * (jnp.where(pos, 1.0, e) * inv)
            if masked:
                dz = jnp.where(mask, dz, 0.0)
            dz = (dz * scale).astype(BF16)
            dq_acc[...] += jnp.dot(dz, kb, preferred_element_type=F32)
            dk_ref[rows, :] += lax.dot_general(dz, qb, _DIMS["tn"], preferred_element_type=F32)
            dv_ref[rows, :] += lax.dot_general(a.astype(BF16), dob, _DIMS["tn"], preferred_element_type=F32)
            pre_ref[...] += jnp.sum(l1m, axis=1, keepdims=True)
            cp_ref[...] += jnp.sum(p, axis=1, keepdims=True)

        def step(j, carry):
            tile(j, False)
            return carry

        lax.fori_loop(0, r * i, step, 0)
        for dgl in range(r):
            tile(r * i + dgl, True)
        dq_ref[...] = dq_acc[...].astype(BF16)

    blk = pl.BlockSpec((bq, TILE), lambda h, i: (i, h))
    head = pl.BlockSpec((s, TILE), lambda h, i: (0, h))
    return _hosted_call(
        body, [q, k, v, do, lsum], name=name, grid=(n_h, n_q), in_specs=[blk, head, head, blk, blk],
        out_specs=[blk, head, head],
        out_shape=[jax.ShapeDtypeStruct((s, hd), BF16), jax.ShapeDtypeStruct((s, hd), F32),
                   jax.ShapeDtypeStruct((s, hd), F32)],
        scratch_shapes=[pltpu.VMEM((bq, TILE), F32), pltpu.VMEM((bq, 1), F32), pltpu.VMEM((bq, 1), F32)],
        semantics=("parallel", "arbitrary"), rides=rides)


EW_BLOCK = 512 * 1024


def _ew_blocks(r, c, elems=EW_BLOCK):
    return _pick(r, max(ROWS, elems // c // ROWS * ROWS), ROWS), c


def _cast_bf16(w, layer, chip_idx, name):
    _, r, c = w.shape
    br, bc = _ew_blocks(r, c)

    def body(chip_ref, w_ref, o_ref):
        o_ref[...] = w_ref[...].astype(BF16)

    return pl.pallas_call(
        body, name=name,
        grid_spec=pltpu.PrefetchScalarGridSpec(
            num_scalar_prefetch=1, grid=(r // br, c // bc),
            in_specs=[pl.BlockSpec((None, br, bc), lambda i, j, chip_ref: (layer, i, j))],
            out_specs=pl.BlockSpec((None, br, bc), lambda i, j, chip_ref: (chip_ref[0], i, j)),
        ),
        out_shape=jax.ShapeDtypeStruct((N_CHIPS, r, c), BF16), compiler_params=_cp("parallel", "parallel"),
    )(chip_idx, w)


def _pair_add(dw, recv, c_idx, name):
    _, r, c = dw.shape
    hr = r // 2
    br, bc = _ew_blocks(hr, c)
    nb = hr // br

    def body(c_ref, a_ref, b_ref, o_ref):
        o_ref[...] = (a_ref[...].astype(F32) + b_ref[...].astype(F32)).astype(BF16)

    return pl.pallas_call(
        body, name=name,
        grid_spec=pltpu.PrefetchScalarGridSpec(
            num_scalar_prefetch=1, grid=(N_CHIPS, nb, c // bc),
            in_specs=[
                pl.BlockSpec((None, br, bc), lambda s, i, j, c_ref: (s, c_ref[0] * nb + i, j)),
                pl.BlockSpec((None, br, bc), lambda s, i, j, c_ref: (s, i, j)),
            ],
            out_specs=pl.BlockSpec((None, br, bc), lambda s, i, j, c_ref: (s, i, j)),
        ),
        out_shape=jax.ShapeDtypeStruct((N_CHIPS, hr, c), BF16),
        compiler_params=_cp("parallel", "parallel", "parallel"),
    )(c_idx, dw, recv)


def _chip_sum(parts, dest, shape, layer, c_idx, name):
    _, hr, c = parts.shape
    br, bc = _ew_blocks(hr, c, EW_BLOCK // 2)
    nb = hr // br

    def body(c_ref, p_ref, *refs):
        o_ref = refs[-1]
        acc = p_ref[0].astype(F32)
        for s in range(1, N_CHIPS):
            acc = acc + p_ref[s].astype(F32)
        o_ref[...] = acc

    in_specs = [pl.BlockSpec((N_CHIPS, br, bc), lambda i, j, c_ref: (0, i, j))]
    operands = [c_idx, parts]
    aliases = {}
    if dest is not None:
        in_specs.append(ANY)
        operands.append(dest)
        aliases = {2: 0}
    return pl.pallas_call(
        body, name=name,
        grid_spec=pltpu.PrefetchScalarGridSpec(
            num_scalar_prefetch=1, grid=(nb, c // bc), in_specs=in_specs,
            out_specs=pl.BlockSpec((None, br, bc), lambda i, j, c_ref: (layer, c_ref[0] * nb + i, j)),
        ),
        out_shape=jax.ShapeDtypeStruct(shape, F32), input_output_aliases=aliases,
        compiler_params=_cp("parallel", "parallel"),
    )(*operands)


def _adamw(w, g, m, v, name):
    n_l, r, c = w.shape
    br, bc = _ew_blocks(r, c, EW_BLOCK // 2)

    def body(w_ref, g_ref, m_ref, v_ref, d_ref, mo_ref, vo_ref):
        g = g_ref[...]
        m = ADAM_B1 * m_ref[...] + (1.0 - ADAM_B1) * g
        v = ADAM_B2 * v_ref[...] + (1.0 - ADAM_B2) * (g * g)
        m_hat = m / (1.0 - ADAM_B1 ** ADAM_STEP)
        v_hat = v / (1.0 - ADAM_B2 ** ADAM_STEP)
        d_ref[...] = -ADAM_LR * (m_hat / (jnp.sqrt(v_hat) + ADAM_EPS) + ADAM_WD * w_ref[...])
        mo_ref[...] = m
        vo_ref[...] = v

    blk = pl.BlockSpec((None, br, bc), lambda l, i, j: (l, i, j))
    return pl.pallas_call(
        body, name=name, grid=(n_l, r // br, c // bc), in_specs=[blk] * 4, out_specs=[blk] * 3,
        out_shape=[jax.ShapeDtypeStruct(w.shape, F32)] * 3, compiler_params=_cp("parallel", "parallel", "parallel"),
    )(w, g, m, v)


def _place():
    x, y, c = lax.axis_index("x"), lax.axis_index("y"), lax.axis_index("c")
    chips = [(1 - x, y), (x, 1 - y), (1 - x, 1 - y)]
    return x, y, c, chips


class _Ride:
    def __init__(self, reads, bufs, new, n_sems, start, finish):
        self.reads, self.bufs, self.new, self.n_sems, self.start, self.finish = reads, bufs, new, n_sems, start, finish


def _hosted_call(body, operands, *, name, grid, in_specs, out_specs, out_shape, scratch_shapes=(), semantics=(), rides=()):
    single = not isinstance(out_shape, (list, tuple))
    out_specs = [out_specs] if single else list(out_specs)
    out_shape = [out_shape] if single else list(out_shape)
    in_specs, scratch_shapes = list(in_specs), list(scratch_shapes)
    n_in, n_out, n_scr = len(in_specs), len(out_shape), len(scratch_shapes)
    extra_in, extra_out, aliases, where = [], [], {}, []
    for ride in rides:
        r0 = len(extra_in)
        extra_in += list(ride.reads)
        b0 = len(extra_in)
        extra_in += list(ride.bufs)
        ob0 = len(extra_out)
        extra_out += [jax.ShapeDtypeStruct(b.shape, b.dtype) for b in ride.bufs]
        for t in range(len(ride.bufs)):
            aliases[n_in + b0 + t] = n_out + ob0 + t
        on0 = len(extra_out)
        extra_out += list(ride.new)
        where.append((r0, len(ride.reads), ob0, len(ride.bufs), on0, len(ride.new)))
    n_ein, n_eout = len(extra_in), len(extra_out)
    sem_shapes = [pltpu.SemaphoreType.DMA((max(1, k),)) for ride in rides for k in ride.n_sems]

    def full_body(*refs):
        ins, outs, scr = refs[:n_in + n_ein], refs[n_in + n_ein:n_in + n_ein + n_out + n_eout], refs[n_in + n_ein + n_out + n_eout:]

        def run(which):
            for idx, (ride, (r0, nr, ob0, nb, on0, nn)) in enumerate(zip(rides, where)):
                fn = ride.start if which == 0 else ride.finish
                fn(ins[n_in + r0:n_in + r0 + nr], outs[n_out + ob0:n_out + ob0 + nb], outs[n_out + on0:n_out + on0 + nn],
                   *scr[n_scr + 3 * idx:n_scr + 3 * idx + 3])

        host = lambda: body(*ins[:n_in], *outs[:n_out], *scr[:n_scr])
        if not rides:
            host()
        elif not grid:
            run(0)
            host()
            run(1)
        else:
            ids = [pl.program_id(ax) for ax in range(len(grid))]
            first = functools.reduce(jnp.logical_and, [i == 0 for i in ids])
            last = functools.reduce(jnp.logical_and, [i == g - 1 for i, g in zip(ids, grid)])
            pl.when(first)(lambda: run(0))
            host()
            pl.when(last)(lambda: run(1))

    if rides:
        params = pltpu.CompilerParams(dimension_semantics=("arbitrary",) * len(grid), vmem_limit_bytes=VMEM_LIMIT)
    else:
        params = _cp(*semantics)
    outs = pl.pallas_call(
        full_body, name=name, grid=grid,
        in_specs=in_specs + [ANY] * n_ein, out_specs=out_specs + [ANY] * n_eout,
        out_shape=out_shape + extra_out, input_output_aliases=aliases,
        scratch_shapes=scratch_shapes + sem_shapes, compiler_params=params,
    )(*operands, *extra_in)
    main = outs[0] if single else list(outs[:n_out])
    rode = [(list(outs[n_out + ob0:n_out + ob0 + nb]), list(outs[n_out + on0:n_out + on0 + nn]))
            for (_, _, ob0, nb, on0, nn) in where]
    return main, rode


def _run_rides(rides, name):
    return _hosted_call(lambda: None, [], name=name, grid=(), in_specs=[], out_specs=[], out_shape=[], rides=rides)[1]


def _ride_gather(slots, part=0, n_parts=1):
    n = len(slots)
    halves = [a.shape[1] // 2 for a in slots]
    sizes = [hr // n_parts for hr in halves]
    for a, hr, size in zip(slots, halves, sizes):
        assert a.shape[1] == 2 * hr and hr == size * n_parts and size % ROWS == 0, a.shape

    def remote(bufs, send_sems, recv_sems, i, k, slot, core, to):
        rows = bufs[i].at[slot, pl.ds(pl.multiple_of(core * halves[i] + part * sizes[i], ROWS), sizes[i])]
        return pltpu.make_async_remote_copy(
            src_ref=rows, dst_ref=rows, send_sem=send_sems.at[i * 6 + k], recv_sem=recv_sems.at[i * 6 + k],
            device_id=to, device_id_type=MESH)

    def start(reads, bufs, new, send_sems, recv_sems, local_sems):
        x, y, c, chips = _place()
        for i in range(n):
            for k, (px, py) in enumerate(chips):
                remote(bufs, send_sems, recv_sems, i, k, 2 * x + y, c, (px, py, c)).start()

    def finish(reads, bufs, new, send_sems, recv_sems, local_sems):
        x, y, c, chips = _place()
        cp = functools.partial(remote, bufs, send_sems, recv_sems)
        for i in range(n):
            for k, (px, py) in enumerate(chips):
                cp(i, k, 2 * px + py, c, (x, y, c)).wait_recv()
                cp(i, 3 + k, 2 * px + py, c, (x, y, 1 - c)).start()
        for i in range(n):
            for k, (px, py) in enumerate(chips):
                cp(i, 3 + k, 2 * px + py, 1 - c, (x, y, c)).wait_recv()
        for i in range(n):
            for k, (px, py) in enumerate(chips):
                cp(i, k, 2 * x + y, c, (px, py, c)).wait_send()
                cp(i, 3 + k, 2 * px + py, c, (x, y, 1 - c)).wait_send()

    return _Ride([], slots, [], (6 * n, 6 * n, 0), start, finish)


def _ride_swap(grads):
    n = len(grads)
    halves = [a.shape[1] // 2 for a in grads]

    def copies(reads, new, send_sems, recv_sems):
        x, y, c, _ = _place()
        out = []
        for i in range(n):
            rows = pl.ds(pl.multiple_of((1 - c) * halves[i], 2 * SUBLANE), halves[i])
            out.append(pltpu.make_async_remote_copy(
                src_ref=reads[i].at[:, rows, :], dst_ref=new[i], send_sem=send_sems.at[i], recv_sem=recv_sems.at[i],
                device_id=(x, y, 1 - c), device_id_type=MESH))
        return out

    def start(reads, bufs, new, send_sems, recv_sems, local_sems):
        for cp in copies(reads, new, send_sems, recv_sems):
            cp.start()

    def finish(reads, bufs, new, send_sems, recv_sems, local_sems):
        for cp in copies(reads, new, send_sems, recv_sems):
            cp.wait()

    shapes = [jax.ShapeDtypeStruct((N_CHIPS, hr, a.shape[2]), a.dtype) for a, hr in zip(grads, halves)]
    return _Ride(grads, [], shapes, (n, n, 0), start, finish)


def _ride_scatter(parts, part=0, n_parts=1, into=None):
    n = len(parts)
    sizes = [a.shape[1] // n_parts for a in parts]
    for a, size in zip(parts, sizes):
        assert a.shape[1] == size * n_parts and size % ROWS == 0, a.shape

    def piece(ref, i, slot):
        return ref.at[slot, pl.ds(part * sizes[i], sizes[i])]

    def own(reads, land, local_sems, i):
        me = 2 * lax.axis_index("x") + lax.axis_index("y")
        return pltpu.make_async_copy(piece(reads[i], i, me), piece(land[i], i, me), local_sems.at[i])

    def send(reads, land, send_sems, recv_sems, i, k):
        x, y, c, chips = _place()
        px, py = chips[k]
        return pltpu.make_async_remote_copy(
            src_ref=piece(reads[i], i, 2 * px + py), dst_ref=piece(land[i], i, 2 * x + y),
            send_sem=send_sems.at[3 * i + k], recv_sem=recv_sems.at[3 * i + k],
            device_id=(px, py, c), device_id_type=MESH)

    def start(reads, bufs, new, send_sems, recv_sems, local_sems):
        land = new if into is None else bufs
        for i in range(n):
            own(reads, land, local_sems, i).start()
            for k in range(3):
                send(reads, land, send_sems, recv_sems, i, k).start()

    def finish(reads, bufs, new, send_sems, recv_sems, local_sems):
        land = new if into is None else bufs
        x, y, c, chips = _place()
        for i in range(n):
            for k, (px, py) in enumerate(chips):
                slot = piece(land[i], i, 2 * px + py)
                pltpu.make_async_remote_copy(
                    src_ref=slot, dst_ref=slot, send_sem=send_sems.at[3 * i + k], recv_sem=recv_sems.at[3 * i + k],
                    device_id=(x, y, c), device_id_type=MESH).wait_recv()
        for i in range(n):
            for k in range(3):
                send(reads, land, send_sems, recv_sems, i, k).wait_send()
            own(reads, land, local_sems, i).wait()

    shapes = [jax.ShapeDtypeStruct(a.shape, a.dtype) for a in parts]
    if into is None:
        return _Ride(parts, [], shapes, (3 * n, 3 * n, n), start, finish)
    return _Ride(parts, list(into), [], (3 * n, 3 * n, n), start, finish)


def _ride_join(grads):
    n = len(grads)

    def copy(bufs, send_sems, recv_sems, i, core, to):
        hr = grads[i].shape[1] // 2
        rows = bufs[i].at[:, pl.ds(pl.multiple_of(core * hr, SUBLANE), hr), :]
        return pltpu.make_async_remote_copy(
            src_ref=rows, dst_ref=rows, send_sem=send_sems.at[i], recv_sem=recv_sems.at[i],
            device_id=to, device_id_type=MESH)

    def start(reads, bufs, new, send_sems, recv_sems, local_sems):
        x, y, c, _ = _place()
        for i in range(n):
            copy(bufs, send_sems, recv_sems, i, c, (x, y, 1 - c)).start()

    def finish(reads, bufs, new, send_sems, recv_sems, local_sems):
        x, y, c, _ = _place()
        for i in range(n):
            copy(bufs, send_sems, recv_sems, i, 1 - c, (x, y, c)).wait_recv()
        for i in range(n):
            copy(bufs, send_sems, recv_sems, i, c, (x, y, 1 - c)).wait_send()

    return _Ride([], grads, [], (n, n, 0), start, finish)


def _all_reduce_small(packed, name):
    r, c = packed.shape
    chunk = _pick(r, 256, ROWS)

    def body(x_ref, out_ref, gath, send_sems, recv_sems, local_sem):
        x, y, cc, chips = _place()
        me, sibling = (x, y, cc), (x, y, 1 - cc)

        def slot(px, py, pc):
            return gath.at[4 * px + 2 * py + pc]

        def copy(k, block, to, src=None):
            return pltpu.make_async_remote_copy(
                src_ref=slot(*block) if src is None else src, dst_ref=slot(*block),
                send_sem=send_sems.at[k], recv_sem=recv_sems.at[k], device_id=to, device_id_type=MESH)

        mine = pltpu.make_async_copy(x_ref, slot(*me), local_sem)
        mine.start()
        first = [copy(0, me, sibling, src=x_ref)]
        first += [copy(1 + j, me, (*chip, cc), src=x_ref) for j, chip in enumerate(chips)]
        for cp in first:
            cp.start()
        passed = [copy(4 + j, (*chip, cc), sibling) for j, chip in enumerate(chips)]
        for j, chip in enumerate(chips):
            copy(1 + j, (*chip, cc), me).wait_recv()
            passed[j].start()
        copy(0, sibling, me).wait_recv()
        for j, chip in enumerate(chips):
            copy(4 + j, (*chip, 1 - cc), me).wait_recv()
        for cp in first + passed:
            cp.wait_send()
        mine.wait()

        def add(i, carry):
            rows = pl.ds(pl.multiple_of(i * chunk, SUBLANE), chunk)
            acc = gath[0, rows, :]
            for dev in range(1, N_DEV):
                acc = acc + gath[dev, rows, :]
            out_ref[rows, :] = acc
            return carry

        lax.fori_loop(0, r // chunk, add, 0)

    return pl.pallas_call(
        body, name=name, in_specs=[VMEM_SPEC], out_specs=VMEM_SPEC,
        out_shape=jax.ShapeDtypeStruct((r, c), F32),
        scratch_shapes=[pltpu.VMEM((N_DEV, r, c), F32), pltpu.SemaphoreType.DMA((7,)),
                        pltpu.SemaphoreType.DMA((7,)), pltpu.SemaphoreType.DMA],
        compiler_params=pltpu.CompilerParams(vmem_limit_bytes=VMEM_LIMIT),
    )(packed)


_PACK_ROWS = 256


def _pack(arrays):
    flat = jnp.concatenate([a.reshape(-1).astype(F32) for a in arrays])
    unit = _PACK_ROWS * LANE
    total = -(-flat.shape[0] // unit) * unit
    return jnp.pad(flat, (0, total - flat.shape[0])).reshape(-1, LANE)


def _unpack(packed, shapes, lead=()):
    flat = packed.reshape(lead + (-1,))
    out, at = [], 0
    for s in shapes:
        size = math.prod(s)
        out.append(flat[..., at:at + size].reshape(lead + tuple(s)))
        at += size
    return out


def kernel(x, pre_mix_g, post_mix_g, pre_ffn_g, post_ffn_g, a_w_in, a_v_norm_g, a_w_spatial, a_b_spatial, a_w_out, kv_norm_g, w_k, w_v, b_w_q, b_w_o, ffn_w_up, ffn_conv_w, ffn_conv_b, ffn_w_down, loss_target, m_pre_mix_g, m_post_mix_g, m_pre_ffn_g, m_post_ffn_g, m_a_w_in, m_a_v_norm_g, m_a_w_spatial, m_a_b_spatial, m_a_w_out, m_kv_norm_g, m_w_k, m_w_v, m_b_w_q, m_b_w_o, m_ffn_w_up, m_ffn_conv_w, m_ffn_conv_b, m_ffn_w_down, v_pre_mix_g, v_post_mix_g, v_pre_ffn_g, v_post_ffn_g, v_a_w_in, v_a_v_norm_g, v_a_w_spatial, v_a_b_spatial, v_a_w_out, v_kv_norm_g, v_w_k, v_w_v, v_b_w_q, v_b_w_o, v_ffn_w_up, v_ffn_conv_w, v_ffn_conv_b, v_ffn_w_down):
    xi, yi, ci = lax.axis_index("x"), lax.axis_index("y"), lax.axis_index("c")
    chip = 2 * xi + yi
    c_idx = jnp.reshape(ci, (1,)).astype(jnp.int32)
    _, s, d = x.shape
    n_layers = pre_mix_g.shape[0]
    assert n_layers == 2 and a_w_in.shape[0] == 1 and b_w_q.shape[0] == 1
    d_a = a_w_out.shape[1] * N_CHIPS
    n_g = a_w_spatial.shape[1]
    ns = ffn_w_up.shape[2]
    assert a_w_spatial.shape[2] == TILE and d_a == n_g * TILE and s % TILE == 0
    h0 = x[0]
    target = loss_target[0]

    big = {
        "win": (a_w_in, m_a_w_in, v_a_w_in),
        "wout": (a_w_out, m_a_w_out, v_a_w_out),
        "wk": (w_k[None], m_w_k[None], v_w_k[None]),
        "wv": (w_v[None], m_w_v[None], v_w_v[None]),
        "wq": (b_w_q, m_b_w_q, v_b_w_q),
        "wo": (b_w_o, m_b_w_o, v_b_w_o),
        "wup": (ffn_w_up, m_ffn_w_up, v_ffn_w_up),
        "wdn": (ffn_w_down, m_ffn_w_down, v_ffn_w_down),
    }
    units = [(nm, layer) for nm in big for layer in range(big[nm][0].shape[0])]
    chip_idx = jnp.reshape(chip, (1,)).astype(jnp.int32)
    shards = [_cast_bf16(big[nm][0], layer, chip_idx, f"cast_{nm}{layer}") for nm, layer in units]
    small_sharded = _pack([a_v_norm_g, ffn_conv_w])
    small_sharded = lax.dynamic_update_index_in_dim(
        jnp.zeros((N_CHIPS,) + small_sharded.shape, F32), small_sharded, chip, 0)
    own = dict(zip(units, shards))
    full = {}

    def gather_ride(keys):
        return _ride_gather([own[key] for key in keys])

    def gathered(keys, rode):
        full.update(zip(keys, rode[0]))

    first_keys = [("win", 0), ("wout", 0)]
    (first_bufs, _), = _run_rides([_ride_gather([own[key] for key in first_keys] + [small_sharded])], "gather_first")
    full.update(zip(first_keys, first_bufs[:-1]))
    vg_parts, cw_parts = _unpack(first_bufs[-1], [a_v_norm_g.shape, ffn_conv_w.shape], lead=(N_CHIPS,))
    v_g = jnp.transpose(vg_parts, (1, 0, 2)).reshape(1, d_a)

    def rows(nm, layer=0):
        w = full[(nm, layer)]
        return w.reshape(w.shape[0] * w.shape[1], w.shape[2])

    gains = lambda g, layer: g[layer:layer + 1]
    bias = jnp.repeat(a_b_spatial[0].T, TILE, axis=1)
    w_s = a_w_spatial[0]
    kv_g = kv_norm_g[None]
    conv_w = [cw_parts[:, layer] for layer in range(n_layers)]
    conv_b = [ffn_conv_b[layer].reshape(N_CHIPS, 1, ns) for layer in range(n_layers)]

    def ffn_fwd(hn, layer, up_keys=(), down_keys=()):
        a = _mm(hn, full[("wup", layer)], "nn", f"ffn_up{layer}", out_split=N_CHIPS,
                rides=[gather_ride(up_keys)] if up_keys else ())
        if up_keys:
            a, (rode,) = a
            gathered(up_keys, rode)
        hm = _ffn_act_fwd(a, conv_w[layer], conv_b[layer], f"ffn_act{layer}")
        f = _mm(hm, rows("wdn", layer), "nn", f"ffn_down{layer}", rides=[gather_ride(down_keys)] if down_keys else ())
        if down_keys:
            f, (rode,) = f
            gathered(down_keys, rode)
        return a, hm, f[0]

    up0 = own[("wup", 0)]
    piece = lambda p: [_ride_gather([up0], part=p, n_parts=4)]
    hn0 = _rms_fwd(h0, gains(pre_mix_g, 0), "norm_in")
    uv, (((up0,), _),) = _mm(hn0, full[("win", 0)], "nn", "gmlp_in", out_split=N_CHIPS, rides=piece(0))
    gm, (((up0,), _),) = _gmlp_fwd(uv, v_g, w_s, bias, "gmlp_gate", rides=piece(1))
    mix0, (((up0,), _),) = _mm(gm, rows("wout"), "nn", "gmlp_out", rides=piece(2))
    mix0 = mix0[0]
    (h1, hn1), (((up0,), _),) = _resid_rms(
        h0, mix0, gains(post_mix_g, 0), [gains(pre_ffn_g, 0)], "resid_mix0", rides=piece(3))
    full[("wup", 0)] = up0
    a0, hm0, f0 = ffn_fwd(hn1, 0, up_keys=[("wdn", 0), ("wq", 0), ("wk", 0)], down_keys=[("wv", 0), ("wo", 0)])
    h2, hn2, kvn = _resid_rms(h1, f0, gains(post_ffn_g, 0), [gains(pre_mix_g, 1), kv_g], "resid_ffn0")
    q = _mm(hn2, rows("wq"), "nn", "proj_q", out_dtype=BF16)[0]
    k = _mm(kvn, rows("wk"), "nn", "proj_k", out_dtype=BF16)[0]
    v = _mm(kvn, rows("wv"), "nn", "proj_v", out_dtype=BF16)[0]
    last_keys = [("wup", 1), ("wdn", 1)]
    (att, lsum), (rode,) = _attn_fwd(q, k, v, "attn_fwd", rides=[gather_ride(last_keys)])
    gathered(last_keys, rode)
    mix1 = _mm(att, rows("wo"), "nn", "proj_o")[0]
    h3, hn3 = _resid_rms(h2, mix1, gains(post_mix_g, 1), [gains(pre_ffn_g, 1)], "resid_mix1")
    a1, hm1, f1 = ffn_fwd(hn3, 1)
    dh4, loss_tile = _loss_head(h3, f1, gains(post_ffn_g, 1), target, "loss_head")
    loss = lax.psum(loss_tile[0, 0], ("x", "y", "c"))

    dw = {}
    dg = {}

    pair = {}
    half_done = {nm: None for nm in big}

    def swap_ride(keys):
        return _ride_swap([dw[key] for key in keys])

    def swapped(keys, rode):
        for (nm, layer), got in zip(keys, rode[1]):
            pair[(nm, layer)] = _pair_add(dw[(nm, layer)], got, c_idx, f"pair_add_{nm}{layer}")

    def scatter_ride(keys):
        return _ride_scatter([pair[key] for key in keys])

    def scattered(keys, rode):
        for (nm, layer), got in zip(keys, rode[1]):
            half_done[nm] = _chip_sum(got, half_done[nm], big[nm][0].shape, layer, c_idx, f"chip_sum_{nm}{layer}")

    def ffn_bwd(dh_out, h_in, hn, a, hm, f, layer, act_rides=()):
        df, dg[("post_ffn", layer)] = _rms_bwd_out(dh_out, f, gains(post_ffn_g, layer), f"d_norm_ffn_out{layer}")
        dwd = _mm(hm, df, "tn", f"d_w_down{layer}", out_dtype=BF16)[0]
        down, up = [("wdn", layer)], [("wup", layer)]
        dw[down[0]] = dwd.reshape(N_CHIPS, dwd.shape[0] // N_CHIPS, d)
        dhm, (rode,) = _mm(df, rows("wdn", layer), "nt", f"d_ffn_mid{layer}", out_split=2, rides=[swap_ride(down)])
        swapped(down, rode)
        (da, dg[("conv_w", layer)], dg[("conv_b", layer)]), act_rode = _ffn_act_bwd(
            a, dhm, conv_w[layer], conv_b[layer], f"d_ffn_act{layer}", rides=act_rides)
        dw[up[0]], (rode,) = _mm(hn, da, "tn", f"d_w_up{layer}", out_dtype=BF16, out_split=N_CHIPS,
                                 rides=[scatter_ride(down)])
        scattered(down, rode)
        dhn, (rode,) = _mm(da, full[("wup", layer)], "nt", f"d_ffn_in{layer}", rides=[swap_ride(up)])
        swapped(up, rode)
        return dhn[0], act_rode

    dhn3, _ = ffn_bwd(dh4, h3, hn3, a1, hm1, f1, 1)
    dh3, (dg[("pre_ffn", 1)],) = _rms_bwd_in(dh4, h3, [([dhn3], gains(pre_ffn_g, 1))], "d_norm_ffn_in1")
    dmix1, dg[("post_mix", 1)] = _rms_bwd_out(dh3, mix1, gains(post_mix_g, 1), "d_norm_mix_out1")
    dwo = _mm(att, dmix1, "tn", "d_w_o", out_dtype=BF16)[0]
    dw[("wo", 0)] = dwo.reshape(N_CHIPS, dwo.shape[0] // N_CHIPS, d)
    datt = _mm(dmix1, rows("wo"), "nt", "d_attn_out", out_dtype=BF16)[0]
    ffn1_keys = [("wup", 1)]
    (dq, dk, dv), (rode,) = _attn_bwd(q, k, v, datt, lsum, "attn_bwd", rides=[scatter_ride(ffn1_keys)])
    scattered(ffn1_keys, rode)
    for nm, act, dact in (("wq", hn2, dq), ("wk", kvn, dk), ("wv", kvn, dv)):
        g = _mm(act, dact, "tn", f"d_{nm}", out_dtype=BF16)[0]
        dw[(nm, 0)] = g.reshape(N_CHIPS, g.shape[0] // N_CHIPS, g.shape[1])
    dhn2 = _mm(dq, rows("wq"), "nt", "d_q_in")[0]
    dkvn_k = _mm(dk, rows("wk"), "nt", "d_k_in")[0]
    attn_keys = [("wo", 0), ("wq", 0), ("wk", 0), ("wv", 0)]
    dkvn_v, (rode,) = _mm(dv, rows("wv"), "nt", "d_v_in", rides=[swap_ride(attn_keys)])
    swapped(attn_keys, rode)
    dh2, (dg[("pre_mix", 1)], dg["kv"]) = _rms_bwd_in(
        dh3, h2, [([dhn2], gains(pre_mix_g, 1)), ([dkvn_k, dkvn_v[0]], kv_g)], "d_norm_mix_in1")
    dhn1, (rode,) = ffn_bwd(dh2, h1, hn1, a0, hm0, f0, 0, act_rides=[scatter_ride(attn_keys)])
    scattered(attn_keys, rode)
    dh1, (dg[("pre_ffn", 0)],) = _rms_bwd_in(dh2, h1, [([dhn1], gains(pre_ffn_g, 0))], "d_norm_ffn_in0")
    dmix0, dg[("post_mix", 0)] = _rms_bwd_out(dh1, mix0, gains(post_mix_g, 0), "d_norm_mix_out0")
    dwout = _mm(gm, dmix0, "tn", "d_w_out", out_dtype=BF16)[0]
    dw[("wout", 0)] = dwout.reshape(N_CHIPS, dwout.shape[0] // N_CHIPS, d)
    dgm = _mm(dmix0, rows("wout"), "nt", "d_gmlp_gate")[0]
    up0_pair = [pair[("wup", 0)]]
    (duv, d_ws, d_bs, d_vg), ((_, up0_landed),) = _gmlp_bwd(
        uv, dgm, v_g, w_s, bias, "d_gmlp", rides=[_ride_scatter(up0_pair, 0, 2)])
    dw[("win", 0)] = _mm(hn0, duv, "tn", "d_w_in", out_dtype=BF16, out_split=N_CHIPS)
    gmlp_keys = [("wout", 0), ("win", 0)]
    dhn0, (rode, (up0_landed, _)) = _mm(
        duv, full[("win", 0)], "nt", "d_gmlp_in",
        rides=[swap_ride(gmlp_keys), _ride_scatter(up0_pair, 1, 2, into=up0_landed)])
    swapped(gmlp_keys, rode)
    scattered([("wup", 0)], (None, up0_landed))
    dx, (dg[("pre_mix", 0)],), (rode,) = _rms_bwd_in(
        dh1, h0, [([dhn0[0]], gains(pre_mix_g, 0))], "d_norm_in", rides=[scatter_ride(gmlp_keys)])
    scattered(gmlp_keys, rode)
    (joined, _), = _run_rides([_ride_join([half_done[nm] for nm in big])], "grads_join")
    grads_big = dict(zip(big, joined))

    stack = lambda key: jnp.concatenate([dg[(key, layer)] for layer in range(n_layers)], axis=0)
    small_parts = [
        stack("pre_mix"), stack("post_mix"), stack("pre_ffn"), stack("post_ffn"),
        d_vg, d_ws, d_bs[::SUBLANE], dg["kv"],
        jnp.stack([dg[("conv_w", layer)] for layer in range(n_layers)]),
        jnp.stack([dg[("conv_b", layer)] for layer in range(n_layers)]),
    ]
    summed = _all_reduce_small(_pack(small_parts), "small_grads_sum")
    (g_pre_mix, g_post_mix, g_pre_ffn, g_post_ffn, g_vg, g_ws, g_bs, g_kv, g_cw, g_cb) = _unpack(
        summed, [p.shape for p in small_parts])
    g_vg = lax.dynamic_index_in_dim(g_vg.reshape(N_CHIPS, 1, d_a // N_CHIPS), chip, 0, keepdims=False)
    g_cw = lax.dynamic_index_in_dim(g_cw, chip, 1, keepdims=False)
    g_cb = g_cb.reshape(n_layers, N_CHIPS * ns)
    small = [
        (pre_mix_g, g_pre_mix, m_pre_mix_g, v_pre_mix_g),
        (post_mix_g, g_post_mix, m_post_mix_g, v_post_mix_g),
        (pre_ffn_g, g_pre_ffn, m_pre_ffn_g, v_pre_ffn_g),
        (post_ffn_g, g_post_ffn, m_post_ffn_g, v_post_ffn_g),
        (a_v_norm_g, g_vg, m_a_v_norm_g, v_a_v_norm_g),
        (a_w_spatial, g_ws[None], m_a_w_spatial, v_a_w_spatial),
        (a_b_spatial, g_bs[None], m_a_b_spatial, v_a_b_spatial),
        (kv_norm_g, g_kv.reshape(d), m_kv_norm_g, v_kv_norm_g),
        (ffn_conv_w, g_cw, m_ffn_conv_w, v_ffn_conv_w),
        (ffn_conv_b, g_cb, m_ffn_conv_b, v_ffn_conv_b),
    ]
    small = [(w, g.reshape(w.shape), m, v) for w, g, m, v in small]
    packed = [_pack([t[i] for t in small])[None] for i in range(4)]
    small_new = [_unpack(p[0], [t[0].shape for t in small]) for p in _adamw(*packed, "adamw_small")]

    new_big = {nm: _adamw(big[nm][0], grads_big[nm], big[nm][1], big[nm][2], f"adamw_{nm}") for nm in big}

    def big_out(nm, which):
        ref_shape = {"wk": w_k.shape, "wv": w_v.shape}.get(nm, big[nm][0].shape)
        arr = grads_big[nm] if which == 0 else new_big[nm][which - 1]
        return arr.reshape(ref_shape)

    order = ["pre_mix", "post_mix", "pre_ffn", "post_ffn", "win", "vg", "ws", "bs", "wout", "kv", "wk", "wv", "wq",
             "wo", "wup", "cw", "cb", "wdn"]
    small_at = {"pre_mix": 0, "post_mix": 1, "pre_ffn": 2, "post_ffn": 3, "vg": 4, "ws": 5, "bs": 6, "kv": 7,
                "cw": 8, "cb": 9}
    outs = [loss, dx[None]]
    for which in range(4):
        for nm in order:
            if nm in small_at:
                outs.append(small[small_at[nm]][1] if which == 0 else small_new[which - 1][small_at[nm]])
            else:
                outs.append(big_out(nm, which))
    return tuple(outs)
```

```python
import functools
import math

import jax
import jax.numpy as jnp
from jax import lax
from jax.experimental import pallas as pl
from jax.experimental.pallas import tpu as pltpu

F32 = jnp.float32
BF16 = jnp.bfloat16
EPS = 1e-6
ADAM_LR = 0.001
ADAM_B1 = 0.9
ADAM_B2 = 0.999
ADAM_EPS = 1e-08
ADAM_WD = 0.01
ADAM_STEP = 10

LANE = 128
SUBLANE = 8
ROWS = 16
TILE = 128
N_CHIPS = 4
N_DEV = 8
VMEM_LIMIT = 56 * 1024 * 1024
MM_VMEM = 40 * 1024 * 1024
MESH = pl.DeviceIdType.MESH
ANY = pl.BlockSpec(memory_space=pl.ANY)
VMEM_SPEC = pl.BlockSpec(memory_space=pltpu.VMEM)


def _cp(*sem):
    return pltpu.CompilerParams(dimension_semantics=sem, vmem_limit_bytes=VMEM_LIMIT)


def _pick(dim, pref, align=LANE):
    if dim <= pref:
        return dim
    best = None
    for d in range(align, pref + 1, align):
        if dim % d == 0:
            best = d
    assert best is not None, (dim, pref)
    return best


_DIMS = {
    "nn": (((1,), (0,)), ((), ())),
    "nt": (((1,), (1,)), ((), ())),
    "tn": (((0,), (0,)), ((), ())),
}


def _as3(a):
    return a if a.ndim == 3 else a[None]


def _spec3(br, bc, cols_j, rc):
    per = cols_j // bc

    def imap(m, n, k):
        r, c = rc(m, n, k)
        return (c // per, r, c % per)

    return pl.BlockSpec((None, br, bc), imap)


def _mm(a, b, mode, name, out_dtype=F32, out_split=1, rides=()):
    a, b = _as3(a), _as3(b)
    ja, ra, caj = a.shape
    jb, rb, cbj = b.shape
    if mode == "nn":
        m, k, n = ra, ja * caj, jb * cbj
        assert rb == k
        m_ext, k_ext, n_ext = [ra], [caj, rb], [cbj]
    elif mode == "nt":
        m, k, n = ra, ja * caj, rb
        assert jb * cbj == k
        m_ext, k_ext, n_ext = [ra], [caj, cbj], [rb]
    else:
        m, k, n = ja * caj, ra, jb * cbj
        assert rb == k
        m_ext, k_ext, n_ext = [caj], [ra], [cbj]
    assert n % out_split == 0
    n_ext.append(n // out_split)
    bm = _pick(math.gcd(*m_ext), 1536)
    bn = _pick(math.gcd(*n_ext), 1536)
    k_unit = math.gcd(*k_ext)
    o_bytes = jnp.dtype(out_dtype).itemsize

    def vmem_need(bk):
        tiles = bm * bk * a.dtype.itemsize + bk * bn * b.dtype.itemsize + bm * bn * o_bytes
        return 2 * tiles + (bm * bn * 4 if bk < k else 0)

    bk = max(d for d in range(LANE, k_unit + 1, LANE) if k_unit % d == 0 and (d == LANE or vmem_need(d) <= MM_VMEM))
    nk = k // bk
    if mode == "nn":
        a_spec = _spec3(bm, bk, caj, lambda mi, ni, ki: (mi, ki))
        b_spec = _spec3(bk, bn, cbj, lambda mi, ni, ki: (ki, ni))
    elif mode == "nt":
        a_spec = _spec3(bm, bk, caj, lambda mi, ni, ki: (mi, ki))
        b_spec = _spec3(bn, bk, cbj, lambda mi, ni, ki: (ni, ki))
    else:
        a_spec = _spec3(bk, bm, caj, lambda mi, ni, ki: (ki, mi))
        b_spec = _spec3(bk, bn, cbj, lambda mi, ni, ki: (ki, ni))
    o_spec = _spec3(bm, bn, n // out_split, lambda mi, ni, ki: (mi, ni))
    dims = _DIMS[mode]

    def body(a_ref, b_ref, o_ref, *acc):
        def part():
            return lax.dot_general(a_ref[...].astype(BF16), b_ref[...].astype(BF16), dims, preferred_element_type=F32)

        if nk == 1:
            o_ref[...] = part().astype(o_ref.dtype)
            return
        acc_ref, = acc
        ki = pl.program_id(2)

        @pl.when(ki == 0)
        def _():
            acc_ref[...] = part()

        @pl.when(jnp.logical_and(ki > 0, ki < nk - 1))
        def _():
            acc_ref[...] += part()

        @pl.when(ki == nk - 1)
        def _():
            o_ref[...] = (acc_ref[...] + part()).astype(o_ref.dtype)

    out, rode = _hosted_call(
        body, [a, b], name=name, grid=(m // bm, n // bn, nk), in_specs=[a_spec, b_spec], out_specs=o_spec,
        out_shape=jax.ShapeDtypeStruct((out_split, m, n // out_split), out_dtype),
        scratch_shapes=[pltpu.VMEM((bm, bn), F32)] if nk > 1 else [],
        semantics=("parallel", "parallel", "arbitrary"), rides=rides)
    return (out, rode) if rides else out


def _rms(x, g):
    r = lax.rsqrt(jnp.mean(x * x, axis=-1, keepdims=True) + EPS)
    return x * r * g


def _rms_bwd(x, g, dy):
    r = lax.rsqrt(jnp.mean(x * x, axis=-1, keepdims=True) + EPS)
    xh = x * r
    gy = dy * g
    dx = r * (gy - xh * jnp.mean(gy * xh, axis=-1, keepdims=True))
    return dx, jnp.sum(dy * xh, axis=0, keepdims=True)


def _row_block(s):
    return _pick(s, 256, ROWS)


def _rms_fwd(h, g, name):
    s, d = h.shape
    br = _row_block(s)

    def body(h_ref, g_ref, o_ref):
        o_ref[...] = _rms(h_ref[...], g_ref[...]).astype(BF16)

    row = pl.BlockSpec((br, d), lambda i: (i, 0))
    vec = pl.BlockSpec((1, d), lambda i: (0, 0))
    return pl.pallas_call(
        body, name=name, grid=(s // br,), in_specs=[row, vec], out_specs=row,
        out_shape=jax.ShapeDtypeStruct((s, d), BF16), compiler_params=_cp("parallel"),
    )(h, g)


def _resid_rms(h_in, f, g_post, g_next, name, rides=()):
    s, d = h_in.shape
    br = _row_block(s)
    n_next = len(g_next)

    def body(h_ref, f_ref, gp_ref, *refs):
        gn_refs, ho_ref, hn_refs = refs[:n_next], refs[n_next], refs[n_next + 1:]
        h = h_ref[...] + _rms(f_ref[...], gp_ref[...])
        ho_ref[...] = h
        for gn_ref, hn_ref in zip(gn_refs, hn_refs):
            hn_ref[...] = _rms(h, gn_ref[...]).astype(BF16)

    row = pl.BlockSpec((br, d), lambda i: (i, 0))
    vec = pl.BlockSpec((1, d), lambda i: (0, 0))
    outs, rode = _hosted_call(
        body, [h_in, f, g_post, *g_next], name=name, grid=(s // br,),
        in_specs=[row, row, vec] + [vec] * n_next,
        out_specs=[row] * (1 + n_next),
        out_shape=[jax.ShapeDtypeStruct((s, d), F32)] + [jax.ShapeDtypeStruct((s, d), BF16)] * n_next,
        semantics=("parallel",), rides=rides)
    return (outs, rode) if rides else outs


def _loss_head(h_in, f, g_post, target, name):
    s, d = h_in.shape
    br = _row_block(s)

    def body(h_ref, f_ref, gp_ref, t_ref, dh_ref, loss_ref):
        @pl.when(pl.program_id(0) == 0)
        def _():
            loss_ref[...] = jnp.zeros_like(loss_ref)

        diff = h_ref[...] + _rms(f_ref[...], gp_ref[...]) - t_ref[...]
        dh_ref[...] = diff * (1.0 / d)
        loss_ref[...] += 0.5 * jnp.sum(jnp.mean(diff * diff, axis=-1, keepdims=True))

    row = pl.BlockSpec((br, d), lambda i: (i, 0))
    vec = pl.BlockSpec((1, d), lambda i: (0, 0))
    return pl.pallas_call(
        body, name=name, grid=(s // br,),
        in_specs=[row, row, vec, row],
        out_specs=[row, pl.BlockSpec((SUBLANE, LANE), lambda i: (0, 0))],
        out_shape=[jax.ShapeDtypeStruct((s, d), F32), jax.ShapeDtypeStruct((SUBLANE, LANE), F32)],
        compiler_params=_cp("arbitrary"),
    )(h_in, f, g_post, target)


def _rms_bwd_out(dy, f, g, name):
    s, d = f.shape
    br = _row_block(s)

    def body(dy_ref, f_ref, g_ref, df_ref, dg_ref):
        @pl.when(pl.program_id(0) == 0)
        def _():
            dg_ref[...] = jnp.zeros_like(dg_ref)

        dx, dg = _rms_bwd(f_ref[...], g_ref[...], dy_ref[...])
        df_ref[...] = dx.astype(BF16)
        dg_ref[...] += dg

    row = pl.BlockSpec((br, d), lambda i: (i, 0))
    vec = pl.BlockSpec((1, d), lambda i: (0, 0))
    return pl.pallas_call(
        body, name=name, grid=(s // br,), in_specs=[row, row, vec], out_specs=[row, vec],
        out_shape=[jax.ShapeDtypeStruct((s, d), BF16), jax.ShapeDtypeStruct((1, d), F32)],
        compiler_params=_cp("arbitrary"),
    )(dy, f, g)


def _rms_bwd_in(dh_out, h_in, branches, name, rides=()):
    s, d = h_in.shape
    br = _row_block(s)
    counts = [len(ds) for ds, _ in branches]
    n_d = sum(counts)
    n_b = len(branches)

    def body(dho_ref, h_ref, *refs):
        d_refs, g_refs = refs[:n_d], refs[n_d:n_d + n_b]
        dh_ref, dg_refs = refs[n_d + n_b], refs[n_d + n_b + 1:]

        @pl.when(pl.program_id(0) == 0)
        def _():
            for r in dg_refs:
                r[...] = jnp.zeros_like(r)

        h = h_ref[...]
        acc = dho_ref[...]
        at = 0
        for bi, cnt in enumerate(counts):
            dn = d_refs[at][...]
            for r in d_refs[at + 1:at + cnt]:
                dn = dn + r[...]
            at += cnt
            dx, dg = _rms_bwd(h, g_refs[bi][...], dn)
            acc = acc + dx
            dg_refs[bi][...] += dg
        dh_ref[...] = acc

    row = pl.BlockSpec((br, d), lambda i: (i, 0))
    vec = pl.BlockSpec((1, d), lambda i: (0, 0))
    flat_d = [x for ds, _ in branches for x in ds]
    outs, rode = _hosted_call(
        body, [dh_out, h_in, *flat_d, *[g for _, g in branches]], name=name, grid=(s // br,),
        in_specs=[row, row] + [row] * n_d + [vec] * n_b,
        out_specs=[row] + [vec] * n_b,
        out_shape=[jax.ShapeDtypeStruct((s, d), F32)] + [jax.ShapeDtypeStruct((1, d), F32)] * n_b,
        semantics=("arbitrary",), rides=rides)
    return (outs[0], list(outs[1:]), rode) if rides else (outs[0], list(outs[1:]))


def _split3(x):
    x0 = x.astype(BF16)
    r1 = x - x0.astype(F32)
    x1 = r1.astype(BF16)
    x2 = (r1 - x1.astype(F32)).astype(BF16)
    return x0, x1, x2


def _tri(n, kind):
    r = lax.broadcasted_iota(jnp.int32, (n, n), 0)
    c = lax.broadcasted_iota(jnp.int32, (n, n), 1)
    m = {"lt": r < c, "le": r <= c, "gt": r > c}[kind]
    return jnp.where(m, 1.0, 0.0).astype(BF16)


_GELU_C = math.sqrt(2.0 / math.pi)
_GELU_A = 0.044715


def _gelu(x):
    return 0.5 * x * (1.0 + jnp.tanh(_GELU_C * (x + _GELU_A * (x * x * x))))


def _gelu_grad(x):
    t = jnp.tanh(_GELU_C * (x + _GELU_A * (x * x * x)))
    return 0.5 * (1.0 + t) + 0.5 * x * (1.0 - t * t) * (_GELU_C * (1.0 + 3.0 * _GELU_A * (x * x)))


def _causal_w(w):
    r = lax.broadcasted_iota(jnp.int32, (TILE, TILE), 0)
    c = lax.broadcasted_iota(jnp.int32, (TILE, TILE), 1)
    return jnp.where(c <= r, w, 0.0)


def _uv_tiles(uv_ref, g, d_a, dq):
    cu, cv = g * TILE, d_a + g * TILE
    u = uv_ref[cu // dq, :, pl.ds(cu % dq, TILE)]
    v = uv_ref[cv // dq, :, pl.ds(cv % dq, TILE)]
    return u, v


def _gmlp_fwd(uv, v_g, w_s, bias, name, rides=()):
    _, s, dq = uv.shape
    d_a = 2 * dq
    n_g = d_a // TILE

    def body(uv_ref, vg_ref, ws_ref, b_ref, o_ref):
        for g in range(n_g):
            up, vp = _uv_tiles(uv_ref, g, d_a, dq)
            cols = pl.ds(g * TILE, TILE)
            vn = _rms(_gelu(vp), vg_ref[:, cols])
            mixed = jnp.dot(_causal_w(ws_ref[g]).astype(BF16), vn.astype(BF16), preferred_element_type=F32) + b_ref[:, cols]
            o_ref[:, cols] = (_gelu(up) * mixed).astype(BF16)

    return _hosted_call(
        body, [uv, v_g, w_s, bias], name=name, grid=(s // TILE,),
        in_specs=[
            pl.BlockSpec((4, TILE, dq), lambda i: (0, i, 0)),
            pl.BlockSpec((1, d_a), lambda i: (0, 0)),
            pl.BlockSpec((n_g, TILE, TILE), lambda i: (0, 0, 0)),
            pl.BlockSpec((TILE, d_a), lambda i: (0, 0)),
        ],
        out_specs=pl.BlockSpec((TILE, d_a), lambda i: (i, 0)),
        out_shape=jax.ShapeDtypeStruct((s, d_a), BF16),
        semantics=("parallel",), rides=rides)


def _gmlp_bwd(uv, dgm, v_g, w_s, bias, name, rides=()):
    _, s, dq = uv.shape
    d_a = 2 * dq
    n_g = d_a // TILE
    n_c = s // TILE

    def body(uv_ref, d_ref, vg_ref, ws_ref, b_ref, duv_ref, dws_ref, dbs_ref, dvg_ref, dbias_acc):
        i = pl.program_id(0)

        @pl.when(i == 0)
        def _():
            dws_ref[...] = jnp.zeros_like(dws_ref)
            dvg_ref[...] = jnp.zeros_like(dvg_ref)
            dbias_acc[...] = jnp.zeros_like(dbias_acc)

        for g in range(n_g):
            up, vp = _uv_tiles(uv_ref, g, d_a, dq)
            cols = pl.ds(g * TILE, TILE)
            vg = vg_ref[:, cols]
            u = _gelu(up)
            v = _gelu(vp)
            r = lax.rsqrt(jnp.mean(v * v, axis=-1, keepdims=True) + EPS)
            vh = v * r
            vn = (vh * vg).astype(BF16)
            wc = _causal_w(ws_ref[g]).astype(BF16)
            mixed = jnp.dot(wc, vn, preferred_element_type=F32) + b_ref[:, cols]
            d_out = d_ref[:, cols]
            du = d_out * mixed
            dmixed = d_out * u
            dmb = dmixed.astype(BF16)
            dvn = lax.dot_general(wc, dmb, _DIMS["tn"], preferred_element_type=F32)
            dws_ref[g] += lax.dot_general(dmb, vn, _DIMS["nt"], preferred_element_type=F32)
            dbias_acc[:, cols] += dmixed
            dvg_ref[:, cols] += jnp.sum(dvn * vh, axis=0, keepdims=True)
            gv = dvn * vg
            dv = r * (gv - vh * jnp.mean(gv * vh, axis=-1, keepdims=True))
            cu, cv = g * TILE, d_a + g * TILE
            duv_ref[cu // dq, :, pl.ds(cu % dq, TILE)] = (du * _gelu_grad(up)).astype(BF16)
            duv_ref[cv // dq, :, pl.ds(cv % dq, TILE)] = (dv * _gelu_grad(vp)).astype(BF16)

        @pl.when(i == n_c - 1)
        def _():
            ones = jnp.ones((SUBLANE, TILE), BF16)
            for g in range(n_g):
                dws_ref[g] = _causal_w(dws_ref[g])
                cols = pl.ds(g * TILE, TILE)
                out = None
                for t in _split3(dbias_acc[:, cols]):
                    p = lax.dot_general(ones, t, _DIMS["nt"], preferred_element_type=F32)
                    out = p if out is None else out + p
                dbs_ref[pl.ds(g * SUBLANE, SUBLANE), :] = out

    return _hosted_call(
        body, [uv, dgm, v_g, w_s, bias], name=name, grid=(n_c,), semantics=("arbitrary",), rides=rides,
        in_specs=[
            pl.BlockSpec((4, TILE, dq), lambda i: (0, i, 0)),
            pl.BlockSpec((TILE, d_a), lambda i: (i, 0)),
            pl.BlockSpec((1, d_a), lambda i: (0, 0)),
            pl.BlockSpec((n_g, TILE, TILE), lambda i: (0, 0, 0)),
            pl.BlockSpec((TILE, d_a), lambda i: (0, 0)),
        ],
        out_specs=[
            pl.BlockSpec((4, TILE, dq), lambda i: (0, i, 0)),
            pl.BlockSpec((n_g, TILE, TILE), lambda i: (0, 0, 0)),
            pl.BlockSpec((n_g * SUBLANE, TILE), lambda i: (0, 0)),
            pl.BlockSpec((1, d_a), lambda i: (0, 0)),
        ],
        out_shape=[
            jax.ShapeDtypeStruct((4, s, dq), BF16),
            jax.ShapeDtypeStruct((n_g, TILE, TILE), F32),
            jax.ShapeDtypeStruct((n_g * SUBLANE, TILE), F32),
            jax.ShapeDtypeStruct((1, d_a), F32),
        ],
        scratch_shapes=[pltpu.VMEM((TILE, d_a), F32)])


def _sigmoid(x):
    return 1.0 / (1.0 + jnp.exp(-x))


def _conv3(ext, w, b):
    return b + ((w[0:1] * pltpu.roll(ext, 2, 0) + w[1:2] * pltpu.roll(ext, 1, 0)) + w[2:3] * ext)


def _act_blocks(s, ns):
    return _pick(s, 512, ROWS), _pick(ns, 256)


def _ffn_act_fwd(a, cw, cb, name):
    _, s, ns = a.shape
    bs, cb_w = _act_blocks(s, ns)
    hb = bs // SUBLANE

    def body(a_ref, prev_ref, cw_ref, cb_ref, o_ref):
        first = pl.program_id(0) == 0

        def conv(comp):
            prev = jnp.where(first, 0.0, prev_ref[comp])
            ext = jnp.concatenate([prev, a_ref[comp]], axis=0)
            return _conv3(ext, cw_ref[comp], cb_ref[comp])[SUBLANE:]

        for p in range(2):
            cg = conv(p)
            o_ref[p] = (cg * _sigmoid(cg) * conv(2 + p)).astype(BF16)

    return pl.pallas_call(
        body, name=name, grid=(s // bs, ns // cb_w),
        in_specs=[
            pl.BlockSpec((4, bs, cb_w), lambda i, j: (0, i, j)),
            pl.BlockSpec((4, SUBLANE, cb_w), lambda i, j: (0, jnp.maximum(i * hb - 1, 0), j)),
            pl.BlockSpec((4, 3, cb_w), lambda i, j: (0, 0, j)),
            pl.BlockSpec((4, 1, cb_w), lambda i, j: (0, 0, j)),
        ],
        out_specs=pl.BlockSpec((2, bs, cb_w), lambda i, j: (0, i, j)),
        out_shape=jax.ShapeDtypeStruct((2, s, ns), BF16),
        compiler_params=_cp("parallel", "parallel"),
    )(a, a, cw, cb)


def _ffn_act_bwd(a, dhm, cw, cb, name, rides=()):
    _, s, ns = a.shape
    bs, cb_w = _act_blocks(s, ns)
    hb = bs // SUBLANE
    n_i = s // bs
    n_ext = bs + 2 * SUBLANE
    cur = slice(SUBLANE, SUBLANE + bs)

    def body(a_ref, prev_ref, next_ref, d_ref, dnext_ref, cw_ref, cb_ref, da_ref, dcw_ref, dcb_ref):
        i = pl.program_id(1)
        first, last = i == 0, i == n_i - 1

        @pl.when(first)
        def _():
            dcw_ref[...] = jnp.zeros_like(dcw_ref)
            dcb_ref[...] = jnp.zeros_like(dcb_ref)

        def ext_of(comp):
            return jnp.concatenate([jnp.where(first, 0.0, prev_ref[comp]), a_ref[comp], next_ref[comp]], axis=0)

        def back(comp, a_ext, dc):
            w = cw_ref[comp]
            da = (w[2:3] * dc + w[1:2] * pltpu.roll(dc, n_ext - 1, 0)) + w[0:1] * pltpu.roll(dc, n_ext - 2, 0)
            da_ref[comp] = da[cur].astype(BF16)
            dcc = dc[cur]
            dcw_ref[comp, 0:1, :] += jnp.sum(dcc * pltpu.roll(a_ext, 2, 0)[cur], axis=0, keepdims=True)
            dcw_ref[comp, 1:2, :] += jnp.sum(dcc * pltpu.roll(a_ext, 1, 0)[cur], axis=0, keepdims=True)
            dcw_ref[comp, 2:3, :] += jnp.sum(dcc * a_ext[cur], axis=0, keepdims=True)
            dcb_ref[comp] += jnp.sum(dcc, axis=0, keepdims=True)

        for p in range(2):
            ag, av = ext_of(p), ext_of(2 + p)
            cg = _conv3(ag, cw_ref[p], cb_ref[p])
            cv = _conv3(av, cw_ref[2 + p], cb_ref[2 + p])
            d = jnp.concatenate(
                [jnp.zeros((SUBLANE, cb_w), F32), d_ref[p], jnp.where(last, 0.0, dnext_ref[p])], axis=0)
            sg = _sigmoid(cg)
            back(2 + p, av, d * (cg * sg))
            back(p, ag, d * cv * (sg * (1.0 + cg * (1.0 - sg))))

    return _hosted_call(
        body, [a, a, a, dhm, dhm, cw, cb], name=name, grid=(ns // cb_w, n_i),
        in_specs=[
            pl.BlockSpec((4, bs, cb_w), lambda j, i: (0, i, j)),
            pl.BlockSpec((4, SUBLANE, cb_w), lambda j, i: (0, jnp.maximum(i * hb - 1, 0), j)),
            pl.BlockSpec((4, SUBLANE, cb_w), lambda j, i: (0, jnp.minimum((i + 1) * hb, n_i * hb - 1), j)),
            pl.BlockSpec((2, bs, cb_w), lambda j, i: (0, i, j)),
            pl.BlockSpec((2, SUBLANE, cb_w), lambda j, i: (0, jnp.minimum((i + 1) * hb, n_i * hb - 1), j)),
            pl.BlockSpec((4, 3, cb_w), lambda j, i: (0, 0, j)),
            pl.BlockSpec((4, 1, cb_w), lambda j, i: (0, 0, j)),
        ],
        out_specs=[
            pl.BlockSpec((4, bs, cb_w), lambda j, i: (0, i, j)),
            pl.BlockSpec((4, 3, cb_w), lambda j, i: (0, 0, j)),
            pl.BlockSpec((4, 1, cb_w), lambda j, i: (0, 0, j)),
        ],
        out_shape=[
            jax.ShapeDtypeStruct((4, s, ns), BF16),
            jax.ShapeDtypeStruct((4, 3, ns), F32),
            jax.ShapeDtypeStruct((4, 1, ns), F32),
        ],
        semantics=("parallel", "arbitrary"), rides=rides)


ATT_BQ = 1024
ATT_BK = 256


def _att_blocks(s):
    bq = _pick(s, ATT_BQ)
    bk = min(ATT_BK, bq)
    assert bq % bk == 0
    return bq, bk


def _dot_sel2(x, sel):
    hi = x.astype(BF16)
    lo = (x - hi.astype(F32)).astype(BF16)
    n = x.shape[0]
    both = jnp.dot(jnp.concatenate([hi, lo], axis=0), sel, preferred_element_type=F32)
    return both[:n] + both[n:]


def _causal_mask(bq, bk, row0, col0):
    rows = row0 + lax.broadcasted_iota(jnp.int32, (bq, bk), 0)
    cols = col0 + lax.broadcasted_iota(jnp.int32, (bq, bk), 1)
    return cols < rows


def _sb_tile(qb, kb, scale, mask):
    z = lax.dot_general(qb, kb, _DIMS["nt"], preferred_element_type=F32) * scale
    e = jnp.exp(-jnp.abs(z))
    lb = jnp.minimum(z, 0.0) - jnp.log(1.0 + e)
    l1m = lb - z
    if mask is not None:
        l1m = jnp.where(mask, l1m, 0.0)
    return z, e, lb, l1m


def _attn_fwd(q, k, v, name, rides=()):
    s, hd = q.shape
    bq, bk = _att_blocks(s)
    r = bq // bk
    n_h, n_q = hd // TILE, s // bq
    scale = 1.0 / math.sqrt(TILE)

    def body(q_ref, k_ref, v_ref, o_ref, l_ref, acc_ref, suf_ref):
        i = pl.program_id(1)
        qb = q_ref[...]
        later = _tri(bk, "gt")
        acc_ref[...] = jnp.zeros_like(acc_ref)
        suf_ref[...] = jnp.zeros_like(suf_ref)

        def tile(j, row0):
            rows = pl.ds(pl.multiple_of(j * bk, bk), bk)
            masked = row0 is not None
            r0 = row0 if masked else 0
            rs = pl.ds(r0, bq - r0)
            mask = _causal_mask(bq - r0, bk, i * bq + r0, j * bk) if masked else None
            _, _, lb, l1m = _sb_tile(qb[r0:], k_ref[rows, :], scale, mask)
            a = jnp.exp(lb + _dot_sel2(l1m, later) + suf_ref[rs, :])
            if masked:
                a = jnp.where(mask, a, 0.0)
            acc_ref[rs, :] += jnp.dot(a.astype(BF16), v_ref[rows, :], preferred_element_type=F32)
            suf_ref[rs, :] += jnp.sum(l1m, axis=1, keepdims=True)

        for dgl in range(r - 1, -1, -1):
            tile(r * i + dgl, dgl * bk)

        def step(t, carry):
            tile(r * i - 1 - t, None)
            return carry

        lax.fori_loop(0, r * i, step, 0)
        o_ref[...] = acc_ref[...].astype(BF16)
        l_ref[...] = jnp.broadcast_to(suf_ref[...], (bq, TILE))

    blk = pl.BlockSpec((bq, TILE), lambda h, i: (i, h))
    head = pl.BlockSpec((s, TILE), lambda h, i: (0, h))
    return _hosted_call(
        body, [q, k, v], name=name, grid=(n_h, n_q), in_specs=[blk, head, head], out_specs=[blk, blk],
        out_shape=[jax.ShapeDtypeStruct((s, hd), BF16), jax.ShapeDtypeStruct((s, hd), F32)],
        scratch_shapes=[pltpu.VMEM((bq, TILE), F32), pltpu.VMEM((bq, 1), F32)],
        semantics=("parallel", "parallel"), rides=rides)


def _attn_bwd(q, k, v, do, lsum, name, rides=()):
    s, hd = q.shape
    bq, bk = _att_blocks(s)
    r = bq // bk
    n_h, n_q = hd // TILE, s // bq
    scale = 1.0 / math.sqrt(TILE)

    def body(q_ref, k_ref, v_ref, do_ref, l_ref, dq_ref, dk_ref, dv_ref, dq_acc, pre_ref, cp_ref):
        i = pl.program_id(1)

        @pl.when(i == 0)
        def _():
            dk_ref[...] = jnp.zeros_like(dk_ref)
            dv_ref[...] = jnp.zeros_like(dv_ref)

        qb = q_ref[...]
        dob = do_ref[...]
        upto = _tri(bk, "le")
        before = _tri(bk, "lt")
        dq_acc[...] = jnp.zeros_like(dq_acc)
        pre_ref[...] = jnp.zeros_like(pre_ref)
        cp_ref[...] = jnp.zeros_like(cp_ref)

        def tile(j, row0):
            rows = pl.ds(pl.multiple_of(j * bk, bk), bk)
            kb, vb = k_ref[rows, :], v_ref[rows, :]
            masked = row0 is not None
            r0 = row0 if masked else 0
            rs = pl.ds(r0, bq - r0)
            qs, dos = qb[r0:], dob[r0:]
            mask = _causal_mask(bq - r0, bk, i * bq + r0, j * bk) if masked else None
            z, e, lb, l1m = _sb_tile(qs, kb, scale, mask)
            suffix = (l_ref[rs, 0:1] - pre_ref[rs, :]) - _dot_sel2(l1m, upto)
            a = jnp.exp(lb + suffix)
            if masked:
                a = jnp.where(mask, a, 0.0)
            p = a * lax.dot_general(dos, vb, _DIMS["nt"], preferred_element_type=F32)
            both = p + (cp_ref[rs, :] + _dot_sel2(p, before))
            sg = jnp.where(z >= 0.0, 1.0, e) * pl.reciprocal(1.0 + e, approx=True)
            dz = p - both * sg
            if masked:
                dz = jnp.where(mask, dz, 0.0)
            dz = (dz * scale).astype(BF16)
            dq_acc[rs, :] += jnp.dot(dz, kb, preferred_element_type=F32)
            dk_ref[rows, :] += lax.dot_general(dz, qs, _DIMS["tn"], preferred_element_type=F32)
            dv_ref[rows, :] += lax.dot_general(a.astype(BF16), dos, _DIMS["tn"], preferred_element_type=F32)
            pre_ref[rs, :] += jnp.sum(l1m, axis=1, keepdims=True)
            cp_ref[rs, :] += jnp.sum(p, axis=1, keepdims=True)

        def step(j, carry):
            tile(j, None)
            return carry

        lax.fori_loop(0, r * i, step, 0)
        for dgl in range(r):
            tile(r * i + dgl, dgl * bk)
        dq_ref[...] = dq_acc[...].astype(BF16)

    blk = pl.BlockSpec((bq, TILE), lambda h, i: (i, h))
    head = pl.BlockSpec((s, TILE), lambda h, i: (0, h))
    return _hosted_call(
        body, [q, k, v, do, lsum], name=name, grid=(n_h, n_q), in_specs=[blk, head, head, blk, blk],
        out_specs=[blk, head, head],
        out_shape=[jax.ShapeDtypeStruct((s, hd), BF16), jax.ShapeDtypeStruct((s, hd), F32),
                   jax.ShapeDtypeStruct((s, hd), F32)],
        scratch_shapes=[pltpu.VMEM((bq, TILE), F32), pltpu.VMEM((bq, 1), F32), pltpu.VMEM((bq, 1), F32)],
        semantics=("parallel", "arbitrary"), rides=rides)


EW_BLOCK = 512 * 1024


def _ew_blocks(r, c, elems=EW_BLOCK):
    return _pick(r, max(ROWS, elems // c // ROWS * ROWS), ROWS), c


def _cast_bf16(w, layer, chip_idx, name):
    _, r, c = w.shape
    br, bc = _ew_blocks(r, c)

    def body(chip_ref, w_ref, o_ref):
        o_ref[...] = w_ref[...].astype(BF16)

    return pl.pallas_call(
        body, name=name,
        grid_spec=pltpu.PrefetchScalarGridSpec(
            num_scalar_prefetch=1, grid=(r // br, c // bc),
            in_specs=[pl.BlockSpec((None, br, bc), lambda i, j, chip_ref: (layer, i, j))],
            out_specs=pl.BlockSpec((None, br, bc), lambda i, j, chip_ref: (chip_ref[0], i, j)),
        ),
        out_shape=jax.ShapeDtypeStruct((N_CHIPS, r, c), BF16), compiler_params=_cp("parallel", "parallel"),
    )(chip_idx, w)


def _pair_add(dw, recv, c_idx, name):
    _, r, c = dw.shape
    hr = r // 2
    br, bc = _ew_blocks(hr, c)
    nb = hr // br

    def body(c_ref, a_ref, b_ref, o_ref):
        o_ref[...] = (a_ref[...].astype(F32) + b_ref[...].astype(F32)).astype(BF16)

    return pl.pallas_call(
        body, name=name,
        grid_spec=pltpu.PrefetchScalarGridSpec(
            num_scalar_prefetch=1, grid=(N_CHIPS, nb, c // bc),
            in_specs=[
                pl.BlockSpec((None, br, bc), lambda s, i, j, c_ref: (s, c_ref[0] * nb + i, j)),
                pl.BlockSpec((None, br, bc), lambda s, i, j, c_ref: (s, i, j)),
            ],
            out_specs=pl.BlockSpec((None, br, bc), lambda s, i, j, c_ref: (s, i, j)),
        ),
        out_shape=jax.ShapeDtypeStruct((N_CHIPS, hr, c), BF16),
        compiler_params=_cp("parallel", "parallel", "parallel"),
    )(c_idx, dw, recv)


def _chip_sum(parts, dest, shape, layer, c_idx, name):
    _, hr, c = parts.shape
    br, bc = _ew_blocks(hr, c, EW_BLOCK // 2)
    nb = hr // br

    def body(c_ref, p_ref, *refs):
        o_ref = refs[-1]
        acc = p_ref[0].astype(F32)
        for s in range(1, N_CHIPS):
            acc = acc + p_ref[s].astype(F32)
        o_ref[...] = acc

    in_specs = [pl.BlockSpec((N_CHIPS, br, bc), lambda i, j, c_ref: (0, i, j))]
    operands = [c_idx, parts]
    aliases = {}
    if dest is not None:
        in_specs.append(ANY)
        operands.append(dest)
        aliases = {2: 0}
    return pl.pallas_call(
        body, name=name,
        grid_spec=pltpu.PrefetchScalarGridSpec(
            num_scalar_prefetch=1, grid=(nb, c // bc), in_specs=in_specs,
            out_specs=pl.BlockSpec((None, br, bc), lambda i, j, c_ref: (layer, c_ref[0] * nb + i, j)),
        ),
        out_shape=jax.ShapeDtypeStruct(shape, F32), input_output_aliases=aliases,
        compiler_params=_cp("parallel", "parallel"),
    )(*operands)


def _adamw(w, g, m, v, name):
    n_l, r, c = w.shape
    br, bc = _ew_blocks(r, c, EW_BLOCK // 2)

    def body(w_ref, g_ref, m_ref, v_ref, d_ref, mo_ref, vo_ref):
        g = g_ref[...]
        m = ADAM_B1 * m_ref[...] + (1.0 - ADAM_B1) * g
        v = ADAM_B2 * v_ref[...] + (1.0 - ADAM_B2) * (g * g)
        m_hat = m / (1.0 - ADAM_B1 ** ADAM_STEP)
        v_hat = v / (1.0 - ADAM_B2 ** ADAM_STEP)
        d_ref[...] = -ADAM_LR * (m_hat / (jnp.sqrt(v_hat) + ADAM_EPS) + ADAM_WD * w_ref[...])
        mo_ref[...] = m
        vo_ref[...] = v

    blk = pl.BlockSpec((None, br, bc), lambda l, i, j: (l, i, j))
    return pl.pallas_call(
        body, name=name, grid=(n_l, r // br, c // bc), in_specs=[blk] * 4, out_specs=[blk] * 3,
        out_shape=[jax.ShapeDtypeStruct(w.shape, F32)] * 3, compiler_params=_cp("parallel", "parallel", "parallel"),
    )(w, g, m, v)


def _place():
    x, y, c = lax.axis_index("x"), lax.axis_index("y"), lax.axis_index("c")
    chips = [(1 - x, y), (x, 1 - y), (1 - x, 1 - y)]
    return x, y, c, chips


class _Ride:
    def __init__(self, reads, bufs, new, n_sems, start, finish):
        self.reads, self.bufs, self.new, self.n_sems, self.start, self.finish = reads, bufs, new, n_sems, start, finish


def _hosted_call(body, operands, *, name, grid, in_specs, out_specs, out_shape, scratch_shapes=(), semantics=(), rides=()):
    single = not isinstance(out_shape, (list, tuple))
    out_specs = [out_specs] if single else list(out_specs)
    out_shape = [out_shape] if single else list(out_shape)
    in_specs, scratch_shapes = list(in_specs), list(scratch_shapes)
    n_in, n_out, n_scr = len(in_specs), len(out_shape), len(scratch_shapes)
    extra_in, extra_out, aliases, where = [], [], {}, []
    for ride in rides:
        r0 = len(extra_in)
        extra_in += list(ride.reads)
        b0 = len(extra_in)
        extra_in += list(ride.bufs)
        ob0 = len(extra_out)
        extra_out += [jax.ShapeDtypeStruct(b.shape, b.dtype) for b in ride.bufs]
        for t in range(len(ride.bufs)):
            aliases[n_in + b0 + t] = n_out + ob0 + t
        on0 = len(extra_out)
        extra_out += list(ride.new)
        where.append((r0, len(ride.reads), ob0, len(ride.bufs), on0, len(ride.new)))
    n_ein, n_eout = len(extra_in), len(extra_out)
    sem_shapes = [pltpu.SemaphoreType.DMA((max(1, k),)) for ride in rides for k in ride.n_sems]

    def full_body(*refs):
        ins, outs, scr = refs[:n_in + n_ein], refs[n_in + n_ein:n_in + n_ein + n_out + n_eout], refs[n_in + n_ein + n_out + n_eout:]

        def run(which):
            for idx, (ride, (r0, nr, ob0, nb, on0, nn)) in enumerate(zip(rides, where)):
                fn = ride.start if which == 0 else ride.finish
                fn(ins[n_in + r0:n_in + r0 + nr], outs[n_out + ob0:n_out + ob0 + nb], outs[n_out + on0:n_out + on0 + nn],
                   *scr[n_scr + 3 * idx:n_scr + 3 * idx + 3])

        host = lambda: body(*ins[:n_in], *outs[:n_out], *scr[:n_scr])
        if not rides:
            host()
        elif not grid:
            run(0)
            host()
            run(1)
        else:
            ids = [pl.program_id(ax) for ax in range(len(grid))]
            first = functools.reduce(jnp.logical_and, [i == 0 for i in ids])
            last = functools.reduce(jnp.logical_and, [i == g - 1 for i, g in zip(ids, grid)])
            pl.when(first)(lambda: run(0))
            host()
            pl.when(last)(lambda: run(1))

    if rides:
        params = pltpu.CompilerParams(dimension_semantics=("arbitrary",) * len(grid), vmem_limit_bytes=VMEM_LIMIT)
    else:
        params = _cp(*semantics)
    outs = pl.pallas_call(
        full_body, name=name, grid=grid,
        in_specs=in_specs + [ANY] * n_ein, out_specs=out_specs + [ANY] * n_eout,
        out_shape=out_shape + extra_out, input_output_aliases=aliases,
        scratch_shapes=scratch_shapes + sem_shapes, compiler_params=params,
    )(*operands, *extra_in)
    main = outs[0] if single else list(outs[:n_out])
    rode = [(list(outs[n_out + ob0:n_out + ob0 + nb]), list(outs[n_out + on0:n_out + on0 + nn]))
            for (_, _, ob0, nb, on0, nn) in where]
    return main, rode


def _run_rides(rides, name):
    return _hosted_call(lambda: None, [], name=name, grid=(), in_specs=[], out_specs=[], out_shape=[], rides=rides)[1]


def _ride_gather(slots, part=0, n_parts=1):
    n = len(slots)
    halves = [a.shape[1] // 2 for a in slots]
    sizes = [hr // n_parts for hr in halves]
    for a, hr, size in zip(slots, halves, sizes):
        assert a.shape[1] == 2 * hr and hr == size * n_parts and size % ROWS == 0, a.shape

    def remote(bufs, send_sems, recv_sems, i, k, slot, core, to):
        rows = bufs[i].at[slot, pl.ds(pl.multiple_of(core * halves[i] + part * sizes[i], ROWS), sizes[i])]
        return pltpu.make_async_remote_copy(
            src_ref=rows, dst_ref=rows, send_sem=send_sems.at[i * 6 + k], recv_sem=recv_sems.at[i * 6 + k],
            device_id=to, device_id_type=MESH)

    def start(reads, bufs, new, send_sems, recv_sems, local_sems):
        x, y, c, chips = _place()
        for i in range(n):
            for k, (px, py) in enumerate(chips):
                remote(bufs, send_sems, recv_sems, i, k, 2 * x + y, c, (px, py, c)).start()

    def finish(reads, bufs, new, send_sems, recv_sems, local_sems):
        x, y, c, chips = _place()
        cp = functools.partial(remote, bufs, send_sems, recv_sems)
        for i in range(n):
            for k, (px, py) in enumerate(chips):
                cp(i, k, 2 * px + py, c, (x, y, c)).wait_recv()
                cp(i, 3 + k, 2 * px + py, c, (x, y, 1 - c)).start()
        for i in range(n):
            for k, (px, py) in enumerate(chips):
                cp(i, 3 + k, 2 * px + py, 1 - c, (x, y, c)).wait_recv()
        for i in range(n):
            for k, (px, py) in enumerate(chips):
                cp(i, k, 2 * x + y, c, (px, py, c)).wait_send()
                cp(i, 3 + k, 2 * px + py, c, (x, y, 1 - c)).wait_send()

    return _Ride([], slots, [], (6 * n, 6 * n, 0), start, finish)


def _ride_swap(grads):
    n = len(grads)
    halves = [a.shape[1] // 2 for a in grads]

    def copies(reads, new, send_sems, recv_sems):
        x, y, c, _ = _place()
        out = []
        for i in range(n):
            rows = pl.ds(pl.multiple_of((1 - c) * halves[i], 2 * SUBLANE), halves[i])
            out.append(pltpu.make_async_remote_copy(
                src_ref=reads[i].at[:, rows, :], dst_ref=new[i], send_sem=send_sems.at[i], recv_sem=recv_sems.at[i],
                device_id=(x, y, 1 - c), device_id_type=MESH))
        return out

    def start(reads, bufs, new, send_sems, recv_sems, local_sems):
        for cp in copies(reads, new, send_sems, recv_sems):
            cp.start()

    def finish(reads, bufs, new, send_sems, recv_sems, local_sems):
        for cp in copies(reads, new, send_sems, recv_sems):
            cp.wait()

    shapes = [jax.ShapeDtypeStruct((N_CHIPS, hr, a.shape[2]), a.dtype) for a, hr in zip(grads, halves)]
    return _Ride(grads, [], shapes, (n, n, 0), start, finish)


def _ride_scatter(parts, part=0, n_parts=1, into=None):
    n = len(parts)
    sizes = [a.shape[1] // n_parts for a in parts]
    for a, size in zip(parts, sizes):
        assert a.shape[1] == size * n_parts and size % ROWS == 0, a.shape

    def piece(ref, i, slot):
        return ref.at[slot, pl.ds(part * sizes[i], sizes[i])]

    def own(reads, land, local_sems, i):
        me = 2 * lax.axis_index("x") + lax.axis_index("y")
        return pltpu.make_async_copy(piece(reads[i], i, me), piece(land[i], i, me), local_sems.at[i])

    def send(reads, land, send_sems, recv_sems, i, k):
        x, y, c, chips = _place()
        px, py = chips[k]
        return pltpu.make_async_remote_copy(
            src_ref=piece(reads[i], i, 2 * px + py), dst_ref=piece(land[i], i, 2 * x + y),
            send_sem=send_sems.at[3 * i + k], recv_sem=recv_sems.at[3 * i + k],
            device_id=(px, py, c), device_id_type=MESH)

    def start(reads, bufs, new, send_sems, recv_sems, local_sems):
        land = new if into is None else bufs
        for i in range(n):
            own(reads, land, local_sems, i).start()
            for k in range(3):
                send(reads, land, send_sems, recv_sems, i, k).start()

    def finish(reads, bufs, new, send_sems, recv_sems, local_sems):
        land = new if into is None else bufs
        x, y, c, chips = _place()
        for i in range(n):
            for k, (px, py) in enumerate(chips):
                slot = piece(land[i], i, 2 * px + py)
                pltpu.make_async_remote_copy(
                    src_ref=slot, dst_ref=slot, send_sem=send_sems.at[3 * i + k], recv_sem=recv_sems.at[3 * i + k],
                    device_id=(x, y, c), device_id_type=MESH).wait_recv()
        for i in range(n):
            for k in range(3):
                send(reads, land, send_sems, recv_sems, i, k).wait_send()
            own(reads, land, local_sems, i).wait()

    shapes = [jax.ShapeDtypeStruct(a.shape, a.dtype) for a in parts]
    if into is None:
        return _Ride(parts, [], shapes, (3 * n, 3 * n, n), start, finish)
    return _Ride(parts, list(into), [], (3 * n, 3 * n, n), start, finish)


def _ride_join(grads):
    n = len(grads)

    def copy(bufs, send_sems, recv_sems, i, core, to):
        hr = grads[i].shape[1] // 2
        rows = bufs[i].at[:, pl.ds(pl.multiple_of(core * hr, SUBLANE), hr), :]
        return pltpu.make_async_remote_copy(
            src_ref=rows, dst_ref=rows, send_sem=send_sems.at[i], recv_sem=recv_sems.at[i],
            device_id=to, device_id_type=MESH)

    def start(reads, bufs, new, send_sems, recv_sems, local_sems):
        x, y, c, _ = _place()
        for i in range(n):
            copy(bufs, send_sems, recv_sems, i, c, (x, y, 1 - c)).start()

    def finish(reads, bufs, new, send_sems, recv_sems, local_sems):
        x, y, c, _ = _place()
        for i in range(n):
            copy(bufs, send_sems, recv_sems, i, 1 - c, (x, y, c)).wait_recv()
        for i in range(n):
            copy(bufs, send_sems, recv_sems, i, c, (x, y, 1 - c)).wait_send()

    return _Ride([], grads, [], (n, n, 0), start, finish)


def _all_reduce_small(packed, name):
    r, c = packed.shape
    chunk = _pick(r, 256, ROWS)

    def body(x_ref, out_ref, gath, send_sems, recv_sems, local_sem):
        x, y, cc, chips = _place()
        me, sibling = (x, y, cc), (x, y, 1 - cc)

        def slot(px, py, pc):
            return gath.at[4 * px + 2 * py + pc]

        def copy(k, block, to, src=None):
            return pltpu.make_async_remote_copy(
                src_ref=slot(*block) if src is None else src, dst_ref=slot(*block),
                send_sem=send_sems.at[k], recv_sem=recv_sems.at[k], device_id=to, device_id_type=MESH)

        mine = pltpu.make_async_copy(x_ref, slot(*me), local_sem)
        mine.start()
        first = [copy(0, me, sibling, src=x_ref)]
        first += [copy(1 + j, me, (*chip, cc), src=x_ref) for j, chip in enumerate(chips)]
        for cp in first:
            cp.start()
        passed = [copy(4 + j, (*chip, cc), sibling) for j, chip in enumerate(chips)]
        for j, chip in enumerate(chips):
            copy(1 + j, (*chip, cc), me).wait_recv()
            passed[j].start()
        copy(0, sibling, me).wait_recv()
        for j, chip in enumerate(chips):
            copy(4 + j, (*chip, 1 - cc), me).wait_recv()
        for cp in first + passed:
            cp.wait_send()
        mine.wait()

        def add(i, carry):
            rows = pl.ds(pl.multiple_of(i * chunk, SUBLANE), chunk)
            acc = gath[0, rows, :]
            for dev in range(1, N_DEV):
                acc = acc + gath[dev, rows, :]
            out_ref[rows, :] = acc
            return carry

        lax.fori_loop(0, r // chunk, add, 0)

    return pl.pallas_call(
        body, name=name, in_specs=[VMEM_SPEC], out_specs=VMEM_SPEC,
        out_shape=jax.ShapeDtypeStruct((r, c), F32),
        scratch_shapes=[pltpu.VMEM((N_DEV, r, c), F32), pltpu.SemaphoreType.DMA((7,)),
                        pltpu.SemaphoreType.DMA((7,)), pltpu.SemaphoreType.DMA],
        compiler_params=pltpu.CompilerParams(vmem_limit_bytes=VMEM_LIMIT),
    )(packed)


_PACK_ROWS = 256


def _pack(arrays):
    flat = jnp.concatenate([a.reshape(-1).astype(F32) for a in arrays])
    unit = _PACK_ROWS * LANE
    total = -(-flat.shape[0] // unit) * unit
    return jnp.pad(flat, (0, total - flat.shape[0])).reshape(-1, LANE)


def _unpack(packed, shapes, lead=()):
    flat = packed.reshape(lead + (-1,))
    out, at = [], 0
    for s in shapes:
        size = math.prod(s)
        out.append(flat[..., at:at + size].reshape(lead + tuple(s)))
        at += size
    return out


def kernel(x, pre_mix_g, post_mix_g, pre_ffn_g, post_ffn_g, a_w_in, a_v_norm_g, a_w_spatial, a_b_spatial, a_w_out, kv_norm_g, w_k, w_v, b_w_q, b_w_o, ffn_w_up, ffn_conv_w, ffn_conv_b, ffn_w_down, loss_target, m_pre_mix_g, m_post_mix_g, m_pre_ffn_g, m_post_ffn_g, m_a_w_in, m_a_v_norm_g, m_a_w_spatial, m_a_b_spatial, m_a_w_out, m_kv_norm_g, m_w_k, m_w_v, m_b_w_q, m_b_w_o, m_ffn_w_up, m_ffn_conv_w, m_ffn_conv_b, m_ffn_w_down, v_pre_mix_g, v_post_mix_g, v_pre_ffn_g, v_post_ffn_g, v_a_w_in, v_a_v_norm_g, v_a_w_spatial, v_a_b_spatial, v_a_w_out, v_kv_norm_g, v_w_k, v_w_v, v_b_w_q, v_b_w_o, v_ffn_w_up, v_ffn_conv_w, v_ffn_conv_b, v_ffn_w_down):
    xi, yi, ci = lax.axis_index("x"), lax.axis_index("y"), lax.axis_index("c")
    chip = 2 * xi + yi
    c_idx = jnp.reshape(ci, (1,)).astype(jnp.int32)
    _, s, d = x.shape
    n_layers = pre_mix_g.shape[0]
    assert n_layers == 2 and a_w_in.shape[0] == 1 and b_w_q.shape[0] == 1
    d_a = a_w_out.shape[1] * N_CHIPS
    n_g = a_w_spatial.shape[1]
    ns = ffn_w_up.shape[2]
    assert a_w_spatial.shape[2] == TILE and d_a == n_g * TILE and s % TILE == 0
    h0 = x[0]
    target = loss_target[0]

    big = {
        "win": (a_w_in, m_a_w_in, v_a_w_in),
        "wout": (a_w_out, m_a_w_out, v_a_w_out),
        "wk": (w_k[None], m_w_k[None], v_w_k[None]),
        "wv": (w_v[None], m_w_v[None], v_w_v[None]),
        "wq": (b_w_q, m_b_w_q, v_b_w_q),
        "wo": (b_w_o, m_b_w_o, v_b_w_o),
        "wup": (ffn_w_up, m_ffn_w_up, v_ffn_w_up),
        "wdn": (ffn_w_down, m_ffn_w_down, v_ffn_w_down),
    }
    units = [(nm, layer) for nm in big for layer in range(big[nm][0].shape[0])]
    chip_idx = jnp.reshape(chip, (1,)).astype(jnp.int32)
    shards = [_cast_bf16(big[nm][0], layer, chip_idx, f"cast_{nm}{layer}") for nm, layer in units]
    small_sharded = _pack([a_v_norm_g, ffn_conv_w])
    small_sharded = lax.dynamic_update_index_in_dim(
        jnp.zeros((N_CHIPS,) + small_sharded.shape, F32), small_sharded, chip, 0)
    own = dict(zip(units, shards))
    full = {}

    def gather_ride(keys):
        return _ride_gather([own[key] for key in keys])

    def gathered(keys, rode):
        full.update(zip(keys, rode[0]))

    first_keys = [("win", 0), ("wout", 0)]
    (first_bufs, _), = _run_rides([_ride_gather([own[key] for key in first_keys] + [small_sharded])], "gather_first")
    full.update(zip(first_keys, first_bufs[:-1]))
    vg_parts, cw_parts = _unpack(first_bufs[-1], [a_v_norm_g.shape, ffn_conv_w.shape], lead=(N_CHIPS,))
    v_g = jnp.transpose(vg_parts, (1, 0, 2)).reshape(1, d_a)

    def rows(nm, layer=0):
        w = full[(nm, layer)]
        return w.reshape(w.shape[0] * w.shape[1], w.shape[2])

    gains = lambda g, layer: g[layer:layer + 1]
    bias = jnp.repeat(a_b_spatial[0].T, TILE, axis=1)
    w_s = a_w_spatial[0]
    kv_g = kv_norm_g[None]
    conv_w = [cw_parts[:, layer] for layer in range(n_layers)]
    conv_b = [ffn_conv_b[layer].reshape(N_CHIPS, 1, ns) for layer in range(n_layers)]

    def ffn_fwd(hn, layer, up_keys=(), down_keys=()):
        a = _mm(hn, full[("wup", layer)], "nn", f"ffn_up{layer}", out_split=N_CHIPS,
                rides=[gather_ride(up_keys)] if up_keys else ())
        if up_keys:
            a, (rode,) = a
            gathered(up_keys, rode)
        hm = _ffn_act_fwd(a, conv_w[layer], conv_b[layer], f"ffn_act{layer}")
        f = _mm(hm, rows("wdn", layer), "nn", f"ffn_down{layer}", rides=[gather_ride(down_keys)] if down_keys else ())
        if down_keys:
            f, (rode,) = f
            gathered(down_keys, rode)
        return a, hm, f[0]

    up0 = own[("wup", 0)]
    piece = lambda p: [_ride_gather([up0], part=p, n_parts=4)]
    hn0 = _rms_fwd(h0, gains(pre_mix_g, 0), "norm_in")
    uv, (((up0,), _),) = _mm(hn0, full[("win", 0)], "nn", "gmlp_in", out_split=N_CHIPS, rides=piece(0))
    gm, (((up0,), _),) = _gmlp_fwd(uv, v_g, w_s, bias, "gmlp_gate", rides=piece(1))
    mix0, (((up0,), _),) = _mm(gm, rows("wout"), "nn", "gmlp_out", rides=piece(2))
    mix0 = mix0[0]
    (h1, hn1), (((up0,), _),) = _resid_rms(
        h0, mix0, gains(post_mix_g, 0), [gains(pre_ffn_g, 0)], "resid_mix0", rides=piece(3))
    full[("wup", 0)] = up0
    a0, hm0, f0 = ffn_fwd(hn1, 0, up_keys=[("wdn", 0), ("wq", 0), ("wk", 0)], down_keys=[("wv", 0), ("wo", 0)])
    h2, hn2, kvn = _resid_rms(h1, f0, gains(post_ffn_g, 0), [gains(pre_mix_g, 1), kv_g], "resid_ffn0")
    q = _mm(hn2, rows("wq"), "nn", "proj_q", out_dtype=BF16)[0]
    k = _mm(kvn, rows("wk"), "nn", "proj_k", out_dtype=BF16)[0]
    v = _mm(kvn, rows("wv"), "nn", "proj_v", out_dtype=BF16)[0]
    last_keys = [("wup", 1), ("wdn", 1)]
    (att, lsum), (rode,) = _attn_fwd(q, k, v, "attn_fwd", rides=[gather_ride(last_keys)])
    gathered(last_keys, rode)
    mix1 = _mm(att, rows("wo"), "nn", "proj_o")[0]
    h3, hn3 = _resid_rms(h2, mix1, gains(post_mix_g, 1), [gains(pre_ffn_g, 1)], "resid_mix1")
    a1, hm1, f1 = ffn_fwd(hn3, 1)
    dh4, loss_tile = _loss_head(h3, f1, gains(post_ffn_g, 1), target, "loss_head")
    loss = lax.psum(loss_tile[0, 0], ("x", "y", "c"))

    dw = {}
    dg = {}

    pair = {}
    half_done = {nm: None for nm in big}

    def swap_ride(keys):
        return _ride_swap([dw[key] for key in keys])

    def swapped(keys, rode):
        for (nm, layer), got in zip(keys, rode[1]):
            pair[(nm, layer)] = _pair_add(dw[(nm, layer)], got, c_idx, f"pair_add_{nm}{layer}")

    def scatter_ride(keys):
        return _ride_scatter([pair[key] for key in keys])

    def scattered(keys, rode):
        for (nm, layer), got in zip(keys, rode[1]):
            half_done[nm] = _chip_sum(got, half_done[nm], big[nm][0].shape, layer, c_idx, f"chip_sum_{nm}{layer}")

    def ffn_bwd(dh_out, h_in, hn, a, hm, f, layer, act_rides=()):
        df, dg[("post_ffn", layer)] = _rms_bwd_out(dh_out, f, gains(post_ffn_g, layer), f"d_norm_ffn_out{layer}")
        dwd = _mm(hm, df, "tn", f"d_w_down{layer}", out_dtype=BF16)[0]
        down, up = [("wdn", layer)], [("wup", layer)]
        dw[down[0]] = dwd.reshape(N_CHIPS, dwd.shape[0] // N_CHIPS, d)
        dhm, (rode,) = _mm(df, rows("wdn", layer), "nt", f"d_ffn_mid{layer}", out_split=2, rides=[swap_ride(down)])
        swapped(down, rode)
        (da, dg[("conv_w", layer)], dg[("conv_b", layer)]), act_rode = _ffn_act_bwd(
            a, dhm, conv_w[layer], conv_b[layer], f"d_ffn_act{layer}", rides=act_rides)
        dw[up[0]], (rode,) = _mm(hn, da, "tn", f"d_w_up{layer}", out_dtype=BF16, out_split=N_CHIPS,
                                 rides=[scatter_ride(down)])
        scattered(down, rode)
        dhn, (rode,) = _mm(da, full[("wup", layer)], "nt", f"d_ffn_in{layer}", rides=[swap_ride(up)])
        swapped(up, rode)
        return dhn[0], act_rode

    dhn3, _ = ffn_bwd(dh4, h3, hn3, a1, hm1, f1, 1)
    dh3, (dg[("pre_ffn", 1)],) = _rms_bwd_in(dh4, h3, [([dhn3], gains(pre_ffn_g, 1))], "d_norm_ffn_in1")
    dmix1, dg[("post_mix", 1)] = _rms_bwd_out(dh3, mix1, gains(post_mix_g, 1), "d_norm_mix_out1")
    dwo = _mm(att, dmix1, "tn", "d_w_o", out_dtype=BF16)[0]
    dw[("wo", 0)] = dwo.reshape(N_CHIPS, dwo.shape[0] // N_CHIPS, d)
    datt = _mm(dmix1, rows("wo"), "nt", "d_attn_out", out_dtype=BF16)[0]
    ffn1_keys = [("wup", 1)]
    (dq, dk, dv), (rode,) = _attn_bwd(q, k, v, datt, lsum, "attn_bwd", rides=[scatter_ride(ffn1_keys)])
    scattered(ffn1_keys, rode)
    for nm, act, dact in (("wq", hn2, dq), ("wk", kvn, dk), ("wv", kvn, dv)):
        g = _mm(act, dact, "tn", f"d_{nm}", out_dtype=BF16)[0]
        dw[(nm, 0)] = g.reshape(N_CHIPS, g.shape[0] // N_CHIPS, g.shape[1])
    dhn2 = _mm(dq, rows("wq"), "nt", "d_q_in")[0]
    dkvn_k = _mm(dk, rows("wk"), "nt", "d_k_in")[0]
    attn_keys = [("wo", 0), ("wq", 0), ("wk", 0), ("wv", 0)]
    dkvn_v, (rode,) = _mm(dv, rows("wv"), "nt", "d_v_in", rides=[swap_ride(attn_keys)])
    swapped(attn_keys, rode)
    dh2, (dg[("pre_mix", 1)], dg["kv"]) = _rms_bwd_in(
        dh3, h2, [([dhn2], gains(pre_mix_g, 1)), ([dkvn_k, dkvn_v[0]], kv_g)], "d_norm_mix_in1")
    dhn1, (rode,) = ffn_bwd(dh2, h1, hn1, a0, hm0, f0, 0, act_rides=[scatter_ride(attn_keys)])
    scattered(attn_keys, rode)
    dh1, (dg[("pre_ffn", 0)],) = _rms_bwd_in(dh2, h1, [([dhn1], gains(pre_ffn_g, 0))], "d_norm_ffn_in0")
    dmix0, dg[("post_mix", 0)] = _rms_bwd_out(dh1, mix0, gains(post_mix_g, 0), "d_norm_mix_out0")
    dwout = _mm(gm, dmix0, "tn", "d_w_out", out_dtype=BF16)[0]
    dw[("wout", 0)] = dwout.reshape(N_CHIPS, dwout.shape[0] // N_CHIPS, d)
    dgm = _mm(dmix0, rows("wout"), "nt", "d_gmlp_gate")[0]
    up0_pair = [pair[("wup", 0)]]
    (duv, d_ws, d_bs, d_vg), ((_, up0_landed),) = _gmlp_bwd(
        uv, dgm, v_g, w_s, bias, "d_gmlp", rides=[_ride_scatter(up0_pair, 0, 2)])
    dw[("win", 0)] = _mm(hn0, duv, "tn", "d_w_in", out_dtype=BF16, out_split=N_CHIPS)
    gmlp_keys = [("wout", 0), ("win", 0)]
    dhn0, (rode, (up0_landed, _)) = _mm(
        duv, full[("win", 0)], "nt", "d_gmlp_in",
        rides=[swap_ride(gmlp_keys), _ride_scatter(up0_pair, 1, 2, into=up0_landed)])
    swapped(gmlp_keys, rode)
    scattered([("wup", 0)], (None, up0_landed))
    dx, (dg[("pre_mix", 0)],), (rode,) = _rms_bwd_in(
        dh1, h0, [([dhn0[0]], gains(pre_mix_g, 0))], "d_norm_in", rides=[scatter_ride(gmlp_keys)])
    scattered(gmlp_keys, rode)
    (joined, _), = _run_rides([_ride_join([half_done[nm] for nm in big])], "grads_join")
    grads_big = dict(zip(big, joined))

    stack = lambda key: jnp.concatenate([dg[(key, layer)] for layer in range(n_layers)], axis=0)
    small_parts = [
        stack("pre_mix"), stack("post_mix"), stack("pre_ffn"), stack("post_ffn"),
        d_vg, d_ws, d_bs[::SUBLANE], dg["kv"],
        jnp.stack([dg[("conv_w", layer)] for layer in range(n_layers)]),
        jnp.stack([dg[("conv_b", layer)] for layer in range(n_layers)]),
    ]
    summed = _all_reduce_small(_pack(small_parts), "small_grads_sum")
    (g_pre_mix, g_post_mix, g_pre_ffn, g_post_ffn, g_vg, g_ws, g_bs, g_kv, g_cw, g_cb) = _unpack(
        summed, [p.shape for p in small_parts])
    g_vg = lax.dynamic_index_in_dim(g_vg.reshape(N_CHIPS, 1, d_a // N_CHIPS), chip, 0, keepdims=False)
    g_cw = lax.dynamic_index_in_dim(g_cw, chip, 1, keepdims=False)
    g_cb = g_cb.reshape(n_layers, N_CHIPS * ns)
    small = [
        (pre_mix_g, g_pre_mix, m_pre_mix_g, v_pre_mix_g),
        (post_mix_g, g_post_mix, m_post_mix_g, v_post_mix_g),
        (pre_ffn_g, g_pre_ffn, m_pre_ffn_g, v_pre_ffn_g),
        (post_ffn_g, g_post_ffn, m_post_ffn_g, v_post_ffn_g),
        (a_v_norm_g, g_vg, m_a_v_norm_g, v_a_v_norm_g),
        (a_w_spatial, g_ws[None], m_a_w_spatial, v_a_w_spatial),
        (a_b_spatial, g_bs[None], m_a_b_spatial, v_a_b_spatial),
        (kv_norm_g, g_kv.reshape(d), m_kv_norm_g, v_kv_norm_g),
        (ffn_conv_w, g_cw, m_ffn_conv_w, v_ffn_conv_w),
        (ffn_conv_b, g_cb, m_ffn_conv_b, v_ffn_conv_b),
    ]
    small = [(w, g.reshape(w.shape), m, v) for w, g, m, v in small]
    packed = [_pack([t[i] for t in small])[None] for i in range(4)]
    small_new = [_unpack(p[0], [t[0].shape for t in small]) for p in _adamw(*packed, "adamw_small")]

    new_big = {nm: _adamw(big[nm][0], grads_big[nm], big[nm][1], big[nm][2], f"adamw_{nm}") for nm in big}

    def big_out(nm, which):
        ref_shape = {"wk": w_k.shape, "wv": w_v.shape}.get(nm, big[nm][0].shape)
        arr = grads_big[nm] if which == 0 else new_big[nm][which - 1]
        return arr.reshape(ref_shape)

    order = ["pre_mix", "post_mix", "pre_ffn", "post_ffn", "win", "vg", "ws", "bs", "wout", "kv", "wk", "wv", "wq",
             "wo", "wup", "cw", "cb", "wdn"]
    small_at = {"pre_mix": 0, "post_mix": 1, "pre_ffn": 2, "post_ffn": 3, "vg": 4, "ws": 5, "bs": 6, "kv": 7,
                "cw": 8, "cb": 9}
    outs = [loss, dx[None]]
    for which in range(4):
        for nm in order:
            if nm in small_at:
                outs.append(small[small_at[nm]][1] if which == 0 else small_new[which - 1][small_at[nm]])
            else:
                outs.append(big_out(nm, which))
    return tuple(outs)
```

```python
import functools
import math

import jax
import jax.numpy as jnp
from jax import lax
from jax.experimental import pallas as pl
from jax.experimental.pallas import tpu as pltpu

F32 = jnp.float32
BF16 = jnp.bfloat16
EPS = 1e-6
ADAM_LR = 0.001
ADAM_B1 = 0.9
ADAM_B2 = 0.999
ADAM_EPS = 1e-08
ADAM_WD = 0.01
ADAM_STEP = 10

LANE = 128
SUBLANE = 8
ROWS = 16
TILE = 128
N_CHIPS = 4
N_DEV = 8
VMEM_LIMIT = 56 * 1024 * 1024
MM_VMEM = 40 * 1024 * 1024
MESH = pl.DeviceIdType.MESH
ANY = pl.BlockSpec(memory_space=pl.ANY)
VMEM_SPEC = pl.BlockSpec(memory_space=pltpu.VMEM)


def _cp(*sem):
    return pltpu.CompilerParams(dimension_semantics=sem, vmem_limit_bytes=VMEM_LIMIT)


def _pick(dim, pref, align=LANE):
    if dim <= pref:
        return dim
    best = None
    for d in range(align, pref + 1, align):
        if dim % d == 0:
            best = d
    assert best is not None, (dim, pref)
    return best


_DIMS = {
    "nn": (((1,), (0,)), ((), ())),
    "nt": (((1,), (1,)), ((), ())),
    "tn": (((0,), (0,)), ((), ())),
}


def _as3(a):
    return a if a.ndim == 3 else a[None]


def _spec3(br, bc, cols_j, rc):
    per = cols_j // bc

    def imap(m, n, k):
        r, c = rc(m, n, k)
        return (c // per, r, c % per)

    return pl.BlockSpec((None, br, bc), imap)


def _mm(a, b, mode, name, out_dtype=F32, out_split=1, rides=()):
    a, b = _as3(a), _as3(b)
    ja, ra, caj = a.shape
    jb, rb, cbj = b.shape
    if mode == "nn":
        m, k, n = ra, ja * caj, jb * cbj
        assert rb == k
        m_ext, k_ext, n_ext = [ra], [caj, rb], [cbj]
    elif mode == "nt":
        m, k, n = ra, ja * caj, rb
        assert jb * cbj == k
        m_ext, k_ext, n_ext = [ra], [caj, cbj], [rb]
    else:
        m, k, n = ja * caj, ra, jb * cbj
        assert rb == k
        m_ext, k_ext, n_ext = [caj], [ra], [cbj]
    assert n % out_split == 0
    n_ext.append(n // out_split)
    bm = _pick(math.gcd(*m_ext), 1536)
    bn = _pick(math.gcd(*n_ext), 1536)
    k_unit = math.gcd(*k_ext)
    o_bytes = jnp.dtype(out_dtype).itemsize

    def vmem_need(bk):
        tiles = bm * bk * a.dtype.itemsize + bk * bn * b.dtype.itemsize + bm * bn * o_bytes
        return 2 * tiles + (bm * bn * 4 if bk < k else 0)

    bk = max(d for d in range(LANE, k_unit + 1, LANE) if k_unit % d == 0 and (d == LANE or vmem_need(d) <= MM_VMEM))
    nk = k // bk
    if mode == "nn":
        a_spec = _spec3(bm, bk, caj, lambda mi, ni, ki: (mi, ki))
        b_spec = _spec3(bk, bn, cbj, lambda mi, ni, ki: (ki, ni))
    elif mode == "nt":
        a_spec = _spec3(bm, bk, caj, lambda mi, ni, ki: (mi, ki))
        b_spec = _spec3(bn, bk, cbj, lambda mi, ni, ki: (ni, ki))
    else:
        a_spec = _spec3(bk, bm, caj, lambda mi, ni, ki: (ki, mi))
        b_spec = _spec3(bk, bn, cbj, lambda mi, ni, ki: (ki, ni))
    o_spec = _spec3(bm, bn, n // out_split, lambda mi, ni, ki: (mi, ni))
    dims = _DIMS[mode]

    def body(a_ref, b_ref, o_ref, *acc):
        def part():
            return lax.dot_general(a_ref[...].astype(BF16), b_ref[...].astype(BF16), dims, preferred_element_type=F32)

        if nk == 1:
            o_ref[...] = part().astype(o_ref.dtype)
            return
        acc_ref, = acc
        ki = pl.program_id(2)

        @pl.when(ki == 0)
        def _():
            acc_ref[...] = part()

        @pl.when(jnp.logical_and(ki > 0, ki < nk - 1))
        def _():
            acc_ref[...] += part()

        @pl.when(ki == nk - 1)
        def _():
            o_ref[...] = (acc_ref[...] + part()).astype(o_ref.dtype)

    out, rode = _hosted_call(
        body, [a, b], name=name, grid=(m // bm, n // bn, nk), in_specs=[a_spec, b_spec], out_specs=o_spec,
        out_shape=jax.ShapeDtypeStruct((out_split, m, n // out_split), out_dtype),
        scratch_shapes=[pltpu.VMEM((bm, bn), F32)] if nk > 1 else [],
        semantics=("parallel", "parallel", "arbitrary"), rides=rides)
    return (out, rode) if rides else out


def _rms(x, g):
    r = lax.rsqrt(jnp.mean(x * x, axis=-1, keepdims=True) + EPS)
    return x * r * g


def _rms_bwd(x, g, dy):
    r = lax.rsqrt(jnp.mean(x * x, axis=-1, keepdims=True) + EPS)
    xh = x * r
    gy = dy * g
    dx = r * (gy - xh * jnp.mean(gy * xh, axis=-1, keepdims=True))
    return dx, jnp.sum(dy * xh, axis=0, keepdims=True)


def _row_block(s):
    return _pick(s, 256, ROWS)


def _rms_fwd(h, g, name):
    s, d = h.shape
    br = _row_block(s)

    def body(h_ref, g_ref, o_ref):
        o_ref[...] = _rms(h_ref[...], g_ref[...]).astype(BF16)

    row = pl.BlockSpec((br, d), lambda i: (i, 0))
    vec = pl.BlockSpec((1, d), lambda i: (0, 0))
    return pl.pallas_call(
        body, name=name, grid=(s // br,), in_specs=[row, vec], out_specs=row,
        out_shape=jax.ShapeDtypeStruct((s, d), BF16), compiler_params=_cp("parallel"),
    )(h, g)


def _resid_rms(h_in, f, g_post, g_next, name, rides=()):
    s, d = h_in.shape
    br = _row_block(s)
    n_next = len(g_next)

    def body(h_ref, f_ref, gp_ref, *refs):
        gn_refs, ho_ref, hn_refs = refs[:n_next], refs[n_next], refs[n_next + 1:]
        h = h_ref[...] + _rms(f_ref[...], gp_ref[...])
        ho_ref[...] = h
        for gn_ref, hn_ref in zip(gn_refs, hn_refs):
            hn_ref[...] = _rms(h, gn_ref[...]).astype(BF16)

    row = pl.BlockSpec((br, d), lambda i: (i, 0))
    vec = pl.BlockSpec((1, d), lambda i: (0, 0))
    outs, rode = _hosted_call(
        body, [h_in, f, g_post, *g_next], name=name, grid=(s // br,),
        in_specs=[row, row, vec] + [vec] * n_next,
        out_specs=[row] * (1 + n_next),
        out_shape=[jax.ShapeDtypeStruct((s, d), F32)] + [jax.ShapeDtypeStruct((s, d), BF16)] * n_next,
        semantics=("parallel",), rides=rides)
    return (outs, rode) if rides else outs


def _loss_head(h_in, f, g_post, target, name):
    s, d = h_in.shape
    br = _row_block(s)

    def body(h_ref, f_ref, gp_ref, t_ref, dh_ref, loss_ref):
        @pl.when(pl.program_id(0) == 0)
        def _():
            loss_ref[...] = jnp.zeros_like(loss_ref)

        diff = h_ref[...] + _rms(f_ref[...], gp_ref[...]) - t_ref[...]
        dh_ref[...] = diff * (1.0 / d)
        loss_ref[...] += 0.5 * jnp.sum(jnp.mean(diff * diff, axis=-1, keepdims=True))

    row = pl.BlockSpec((br, d), lambda i: (i, 0))
    vec = pl.BlockSpec((1, d), lambda i: (0, 0))
    return pl.pallas_call(
        body, name=name, grid=(s // br,),
        in_specs=[row, row, vec, row],
        out_specs=[row, pl.BlockSpec((SUBLANE, LANE), lambda i: (0, 0))],
        out_shape=[jax.ShapeDtypeStruct((s, d), F32), jax.ShapeDtypeStruct((SUBLANE, LANE), F32)],
        compiler_params=_cp("arbitrary"),
    )(h_in, f, g_post, target)


def _rms_bwd_out(dy, f, g, name):
    s, d = f.shape
    br = _row_block(s)

    def body(dy_ref, f_ref, g_ref, df_ref, dg_ref):
        @pl.when(pl.program_id(0) == 0)
        def _():
            dg_ref[...] = jnp.zeros_like(dg_ref)

        dx, dg = _rms_bwd(f_ref[...], g_ref[...], dy_ref[...])
        df_ref[...] = dx.astype(BF16)
        dg_ref[...] += dg

    row = pl.BlockSpec((br, d), lambda i: (i, 0))
    vec = pl.BlockSpec((1, d), lambda i: (0, 0))
    return pl.pallas_call(
        body, name=name, grid=(s // br,), in_specs=[row, row, vec], out_specs=[row, vec],
        out_shape=[jax.ShapeDtypeStruct((s, d), BF16), jax.ShapeDtypeStruct((1, d), F32)],
        compiler_params=_cp("arbitrary"),
    )(dy, f, g)


def _rms_bwd_in(dh_out, h_in, branches, name, rides=()):
    s, d = h_in.shape
    br = _row_block(s)
    counts = [len(ds) for ds, _ in branches]
    n_d = sum(counts)
    n_b = len(branches)

    def body(dho_ref, h_ref, *refs):
        d_refs, g_refs = refs[:n_d], refs[n_d:n_d + n_b]
        dh_ref, dg_refs = refs[n_d + n_b], refs[n_d + n_b + 1:]

        @pl.when(pl.program_id(0) == 0)
        def _():
            for r in dg_refs:
                r[...] = jnp.zeros_like(r)

        h = h_ref[...]
        acc = dho_ref[...]
        at = 0
        for bi, cnt in enumerate(counts):
            dn = d_refs[at][...]
            for r in d_refs[at + 1:at + cnt]:
                dn = dn + r[...]
            at += cnt
            dx, dg = _rms_bwd(h, g_refs[bi][...], dn)
            acc = acc + dx
            dg_refs[bi][...] += dg
        dh_ref[...] = acc

    row = pl.BlockSpec((br, d), lambda i: (i, 0))
    vec = pl.BlockSpec((1, d), lambda i: (0, 0))
    flat_d = [x for ds, _ in branches for x in ds]
    outs, rode = _hosted_call(
        body, [dh_out, h_in, *flat_d, *[g for _, g in branches]], name=name, grid=(s // br,),
        in_specs=[row, row] + [row] * n_d + [vec] * n_b,
        out_specs=[row] + [vec] * n_b,
        out_shape=[jax.ShapeDtypeStruct((s, d), F32)] + [jax.ShapeDtypeStruct((1, d), F32)] * n_b,
        semantics=("arbitrary",), rides=rides)
    return (outs[0], list(outs[1:]), rode) if rides else (outs[0], list(outs[1:]))


def _split3(x):
    x0 = x.astype(BF16)
    r1 = x - x0.astype(F32)
    x1 = r1.astype(BF16)
    x2 = (r1 - x1.astype(F32)).astype(BF16)
    return x0, x1, x2


def _tri(n, kind):
    r = lax.broadcasted_iota(jnp.int32, (n, n), 0)
    c = lax.broadcasted_iota(jnp.int32, (n, n), 1)
    m = {"lt": r < c, "le": r <= c, "gt": r > c}[kind]
    return jnp.where(m, 1.0, 0.0).astype(BF16)


_GELU_C = math.sqrt(2.0 / math.pi)
_GELU_A = 0.044715


def _gelu(x):
    return 0.5 * x * (1.0 + jnp.tanh(_GELU_C * (x + _GELU_A * (x * x * x))))


def _gelu_grad(x):
    t = jnp.tanh(_GELU_C * (x + _GELU_A * (x * x * x)))
    return 0.5 * (1.0 + t) + 0.5 * x * (1.0 - t * t) * (_GELU_C * (1.0 + 3.0 * _GELU_A * (x * x)))


def _causal_w(w):
    r = lax.broadcasted_iota(jnp.int32, (TILE, TILE), 0)
    c = lax.broadcasted_iota(jnp.int32, (TILE, TILE), 1)
    return jnp.where(c <= r, w, 0.0)


def _uv_tiles(uv_ref, g, d_a, dq):
    cu, cv = g * TILE, d_a + g * TILE
    u = uv_ref[cu // dq, :, pl.ds(cu % dq, TILE)]
    v = uv_ref[cv // dq, :, pl.ds(cv % dq, TILE)]
    return u, v


def _gmlp_fwd(uv, v_g, w_s, bias, name, rides=()):
    _, s, dq = uv.shape
    d_a = 2 * dq
    n_g = d_a // TILE

    def body(uv_ref, vg_ref, ws_ref, b_ref, o_ref):
        for g in range(n_g):
            up, vp = _uv_tiles(uv_ref, g, d_a, dq)
            cols = pl.ds(g * TILE, TILE)
            vn = _rms(_gelu(vp), vg_ref[:, cols])
            mixed = jnp.dot(_causal_w(ws_ref[g]).astype(BF16), vn.astype(BF16), preferred_element_type=F32) + b_ref[:, cols]
            o_ref[:, cols] = (_gelu(up) * mixed).astype(BF16)

    return _hosted_call(
        body, [uv, v_g, w_s, bias], name=name, grid=(s // TILE,),
        in_specs=[
            pl.BlockSpec((4, TILE, dq), lambda i: (0, i, 0)),
            pl.BlockSpec((1, d_a), lambda i: (0, 0)),
            pl.BlockSpec((n_g, TILE, TILE), lambda i: (0, 0, 0)),
            pl.BlockSpec((TILE, d_a), lambda i: (0, 0)),
        ],
        out_specs=pl.BlockSpec((TILE, d_a), lambda i: (i, 0)),
        out_shape=jax.ShapeDtypeStruct((s, d_a), BF16),
        semantics=("parallel",), rides=rides)


def _gmlp_bwd(uv, dgm, v_g, w_s, bias, name, rides=()):
    _, s, dq = uv.shape
    d_a = 2 * dq
    n_g = d_a // TILE
    n_c = s // TILE

    def body(uv_ref, d_ref, vg_ref, ws_ref, b_ref, duv_ref, dws_ref, dbs_ref, dvg_ref, dbias_acc):
        i = pl.program_id(0)

        @pl.when(i == 0)
        def _():
            dws_ref[...] = jnp.zeros_like(dws_ref)
            dvg_ref[...] = jnp.zeros_like(dvg_ref)
            dbias_acc[...] = jnp.zeros_like(dbias_acc)

        for g in range(n_g):
            up, vp = _uv_tiles(uv_ref, g, d_a, dq)
            cols = pl.ds(g * TILE, TILE)
            vg = vg_ref[:, cols]
            u = _gelu(up)
            v = _gelu(vp)
            r = lax.rsqrt(jnp.mean(v * v, axis=-1, keepdims=True) + EPS)
            vh = v * r
            vn = (vh * vg).astype(BF16)
            wc = _causal_w(ws_ref[g]).astype(BF16)
            mixed = jnp.dot(wc, vn, preferred_element_type=F32) + b_ref[:, cols]
            d_out = d_ref[:, cols]
            du = d_out * mixed
            dmixed = d_out * u
            dmb = dmixed.astype(BF16)
            dvn = lax.dot_general(wc, dmb, _DIMS["tn"], preferred_element_type=F32)
            dws_ref[g] += lax.dot_general(dmb, vn, _DIMS["nt"], preferred_element_type=F32)
            dbias_acc[:, cols] += dmixed
            dvg_ref[:, cols] += jnp.sum(dvn * vh, axis=0, keepdims=True)
            gv = dvn * vg
            dv = r * (gv - vh * jnp.mean(gv * vh, axis=-1, keepdims=True))
            cu, cv = g * TILE, d_a + g * TILE
            duv_ref[cu // dq, :, pl.ds(cu % dq, TILE)] = (du * _gelu_grad(up)).astype(BF16)
            duv_ref[cv // dq, :, pl.ds(cv % dq, TILE)] = (dv * _gelu_grad(vp)).astype(BF16)

        @pl.when(i == n_c - 1)
        def _():
            ones = jnp.ones((SUBLANE, TILE), BF16)
            for g in range(n_g):
                dws_ref[g] = _causal_w(dws_ref[g])
                cols = pl.ds(g * TILE, TILE)
                out = None
                for t in _split3(dbias_acc[:, cols]):
                    p = lax.dot_general(ones, t, _DIMS["nt"], preferred_element_type=F32)
                    out = p if out is None else out + p
                dbs_ref[pl.ds(g * SUBLANE, SUBLANE), :] = out

    return _hosted_call(
        body, [uv, dgm, v_g, w_s, bias], name=name, grid=(n_c,), semantics=("arbitrary",), rides=rides,
        in_specs=[
            pl.BlockSpec((4, TILE, dq), lambda i: (0, i, 0)),
            pl.BlockSpec((TILE, d_a), lambda i: (i, 0)),
            pl.BlockSpec((1, d_a), lambda i: (0, 0)),
            pl.BlockSpec((n_g, TILE, TILE), lambda i: (0, 0, 0)),
            pl.BlockSpec((TILE, d_a), lambda i: (0, 0)),
        ],
        out_specs=[
            pl.BlockSpec((4, TILE, dq), lambda i: (0, i, 0)),
            pl.BlockSpec((n_g, TILE, TILE), lambda i: (0, 0, 0)),
            pl.BlockSpec((n_g * SUBLANE, TILE), lambda i: (0, 0)),
            pl.BlockSpec((1, d_a), lambda i: (0, 0)),
        ],
        out_shape=[
            jax.ShapeDtypeStruct((4, s, dq), BF16),
            jax.ShapeDtypeStruct((n_g, TILE, TILE), F32),
            jax.ShapeDtypeStruct((n_g * SUBLANE, TILE), F32),
            jax.ShapeDtypeStruct((1, d_a), F32),
        ],
        scratch_shapes=[pltpu.VMEM((TILE, d_a), F32)])


def _sigmoid(x):
    return 1.0 / (1.0 + jnp.exp(-x))


def _conv3(ext, w, b):
    return b + ((w[0:1] * pltpu.roll(ext, 2, 0) + w[1:2] * pltpu.roll(ext, 1, 0)) + w[2:3] * ext)


def _act_blocks(s, ns):
    return _pick(s, 512, ROWS), _pick(ns, 256)


def _ffn_act_fwd(a, cw, cb, name):
    _, s, ns = a.shape
    bs, cb_w = _act_blocks(s, ns)
    hb = bs // SUBLANE

    def body(a_ref, prev_ref, cw_ref, cb_ref, o_ref):
        first = pl.program_id(0) == 0

        def conv(comp):
            prev = jnp.where(first, 0.0, prev_ref[comp])
            ext = jnp.concatenate([prev, a_ref[comp]], axis=0)
            return _conv3(ext, cw_ref[comp], cb_ref[comp])[SUBLANE:]

        for p in range(2):
            cg = conv(p)
            o_ref[p] = (cg * _sigmoid(cg) * conv(2 + p)).astype(BF16)

    return pl.pallas_call(
        body, name=name, grid=(s // bs, ns // cb_w),
        in_specs=[
            pl.BlockSpec((4, bs, cb_w), lambda i, j: (0, i, j)),
            pl.BlockSpec((4, SUBLANE, cb_w), lambda i, j: (0, jnp.maximum(i * hb - 1, 0), j)),
            pl.BlockSpec((4, 3, cb_w), lambda i, j: (0, 0, j)),
            pl.BlockSpec((4, 1, cb_w), lambda i, j: (0, 0, j)),
        ],
        out_specs=pl.BlockSpec((2, bs, cb_w), lambda i, j: (0, i, j)),
        out_shape=jax.ShapeDtypeStruct((2, s, ns), BF16),
        compiler_params=_cp("parallel", "parallel"),
    )(a, a, cw, cb)


def _ffn_act_bwd(a, dhm, cw, cb, name, rides=()):
    _, s, ns = a.shape
    bs, cb_w = _act_blocks(s, ns)
    hb = bs // SUBLANE
    n_i = s // bs
    n_ext = bs + 2 * SUBLANE
    cur = slice(SUBLANE, SUBLANE + bs)

    def body(a_ref, prev_ref, next_ref, d_ref, dnext_ref, cw_ref, cb_ref, da_ref, dcw_ref, dcb_ref):
        i = pl.program_id(1)
        first, last = i == 0, i == n_i - 1

        @pl.when(first)
        def _():
            dcw_ref[...] = jnp.zeros_like(dcw_ref)
            dcb_ref[...] = jnp.zeros_like(dcb_ref)

        def ext_of(comp):
            return jnp.concatenate([jnp.where(first, 0.0, prev_ref[comp]), a_ref[comp], next_ref[comp]], axis=0)

        def back(comp, a_ext, dc):
            w = cw_ref[comp]
            da = (w[2:3] * dc + w[1:2] * pltpu.roll(dc, n_ext - 1, 0)) + w[0:1] * pltpu.roll(dc, n_ext - 2, 0)
            da_ref[comp] = da[cur].astype(BF16)
            dcc = dc[cur]
            dcw_ref[comp, 0:1, :] += jnp.sum(dcc * pltpu.roll(a_ext, 2, 0)[cur], axis=0, keepdims=True)
            dcw_ref[comp, 1:2, :] += jnp.sum(dcc * pltpu.roll(a_ext, 1, 0)[cur], axis=0, keepdims=True)
            dcw_ref[comp, 2:3, :] += jnp.sum(dcc * a_ext[cur], axis=0, keepdims=True)
            dcb_ref[comp] += jnp.sum(dcc, axis=0, keepdims=True)

        for p in range(2):
            ag, av = ext_of(p), ext_of(2 + p)
            cg = _conv3(ag, cw_ref[p], cb_ref[p])
            cv = _conv3(av, cw_ref[2 + p], cb_ref[2 + p])
            d = jnp.concatenate(
                [jnp.zeros((SUBLANE, cb_w), F32), d_ref[p], jnp.where(last, 0.0, dnext_ref[p])], axis=0)
            sg = _sigmoid(cg)
            back(2 + p, av, d * (cg * sg))
            back(p, ag, d * cv * (sg * (1.0 + cg * (1.0 - sg))))

    return _hosted_call(
        body, [a, a, a, dhm, dhm, cw, cb], name=name, grid=(ns // cb_w, n_i),
        in_specs=[
            pl.BlockSpec((4, bs, cb_w), lambda j, i: (0, i, j)),
            pl.BlockSpec((4, SUBLANE, cb_w), lambda j, i: (0, jnp.maximum(i * hb - 1, 0), j)),
            pl.BlockSpec((4, SUBLANE, cb_w), lambda j, i: (0, jnp.minimum((i + 1) * hb, n_i * hb - 1), j)),
            pl.BlockSpec((2, bs, cb_w), lambda j, i: (0, i, j)),
            pl.BlockSpec((2, SUBLANE, cb_w), lambda j, i: (0, jnp.minimum((i + 1) * hb, n_i * hb - 1), j)),
            pl.BlockSpec((4, 3, cb_w), lambda j, i: (0, 0, j)),
            pl.BlockSpec((4, 1, cb_w), lambda j, i: (0, 0, j)),
        ],
        out_specs=[
            pl.BlockSpec((4, bs, cb_w), lambda j, i: (0, i, j)),
            pl.BlockSpec((4, 3, cb_w), lambda j, i: (0, 0, j)),
            pl.BlockSpec((4, 1, cb_w), lambda j, i: (0, 0, j)),
        ],
        out_shape=[
            jax.ShapeDtypeStruct((4, s, ns), BF16),
            jax.ShapeDtypeStruct((4, 3, ns), F32),
            jax.ShapeDtypeStruct((4, 1, ns), F32),
        ],
        semantics=("parallel", "arbitrary"), rides=rides)


ATT_BQ_FWD = 2048
ATT_BQ_BWD = 1024
ATT_BK = 256


def _att_blocks(s, bq_pref):
    bq = _pick(s, bq_pref)
    bk = min(ATT_BK, bq)
    assert bq % bk == 0
    return bq, bk


def _dot_sel2(x, sel):
    hi = x.astype(BF16)
    lo = (x - hi.astype(F32)).astype(BF16)
    n = x.shape[0]
    both = jnp.dot(jnp.concatenate([hi, lo], axis=0), sel, preferred_element_type=F32)
    return both[:n] + both[n:]


def _causal_mask(bq, bk, row0, col0):
    rows = row0 + lax.broadcasted_iota(jnp.int32, (bq, bk), 0)
    cols = col0 + lax.broadcasted_iota(jnp.int32, (bq, bk), 1)
    return cols < rows


def _sb_tile(qb, kb, scale, mask):
    z = lax.dot_general(qb, kb, _DIMS["nt"], preferred_element_type=F32) * scale
    e = jnp.exp(-jnp.abs(z))
    lb = jnp.minimum(z, 0.0) - jnp.log(1.0 + e)
    l1m = lb - z
    if mask is not None:
        l1m = jnp.where(mask, l1m, 0.0)
    return z, e, lb, l1m


def _attn_fwd(q, k, v, name, rides=()):
    s, hd = q.shape
    bq, bk = _att_blocks(s, ATT_BQ_FWD)
    r = bq // bk
    n_h, n_q = hd // TILE, s // bq
    scale = 1.0 / math.sqrt(TILE)

    def body(q_ref, k_ref, v_ref, o_ref, l_ref, acc_ref, suf_ref):
        i = pl.program_id(1)
        qb = q_ref[...]
        later = _tri(bk, "gt")
        acc_ref[...] = jnp.zeros_like(acc_ref)
        suf_ref[...] = jnp.zeros_like(suf_ref)

        def tile(j, row0):
            rows = pl.ds(pl.multiple_of(j * bk, bk), bk)
            masked = row0 is not None
            r0 = row0 if masked else 0
            rs = pl.ds(r0, bq - r0)
            mask = _causal_mask(bq - r0, bk, i * bq + r0, j * bk) if masked else None
            _, _, lb, l1m = _sb_tile(qb[r0:], k_ref[rows, :], scale, mask)
            a = jnp.exp(lb + _dot_sel2(l1m, later) + suf_ref[rs, :])
            if masked:
                a = jnp.where(mask, a, 0.0)
            acc_ref[rs, :] += jnp.dot(a.astype(BF16), v_ref[rows, :], preferred_element_type=F32)
            suf_ref[rs, :] += jnp.sum(l1m, axis=1, keepdims=True)

        for dgl in range(r - 1, -1, -1):
            tile(r * i + dgl, dgl * bk)

        def step(t, carry):
            tile(r * i - 1 - t, None)
            return carry

        lax.fori_loop(0, r * i, step, 0)
        o_ref[...] = acc_ref[...].astype(BF16)
        l_ref[...] = jnp.broadcast_to(suf_ref[...], (bq, TILE))

    blk = pl.BlockSpec((bq, TILE), lambda h, i: (i, h))
    head = pl.BlockSpec((s, TILE), lambda h, i: (0, h))
    return _hosted_call(
        body, [q, k, v], name=name, grid=(n_h, n_q), in_specs=[blk, head, head], out_specs=[blk, blk],
        out_shape=[jax.ShapeDtypeStruct((s, hd), BF16), jax.ShapeDtypeStruct((s, hd), F32)],
        scratch_shapes=[pltpu.VMEM((bq, TILE), F32), pltpu.VMEM((bq, 1), F32)],
        semantics=("parallel", "parallel"), rides=rides)


def _attn_bwd(q, k, v, do, lsum, name, rides=()):
    s, hd = q.shape
    bq, bk = _att_blocks(s, ATT_BQ_BWD)
    r = bq // bk
    n_h, n_q = hd // TILE, s // bq
    scale = 1.0 / math.sqrt(TILE)

    def body(q_ref, k_ref, v_ref, do_ref, l_ref, dq_ref, dk_ref, dv_ref, dq_acc, pre_ref, cp_ref):
        i = pl.program_id(1)

        @pl.when(i == 0)
        def _():
            dk_ref[...] = jnp.zeros_like(dk_ref)
            dv_ref[...] = jnp.zeros_like(dv_ref)

        qb = q_ref[...]
        dob = do_ref[...]
        upto = _tri(bk, "le")
        before = _tri(bk, "lt")
        dq_acc[...] = jnp.zeros_like(dq_acc)
        pre_ref[...] = jnp.zeros_like(pre_ref)
        cp_ref[...] = jnp.zeros_like(cp_ref)

        def tile(j, row0):
            rows = pl.ds(pl.multiple_of(j * bk, bk), bk)
            kb, vb = k_ref[rows, :], v_ref[rows, :]
            masked = row0 is not None
            r0 = row0 if masked else 0
            rs = pl.ds(r0, bq - r0)
            qs, dos = qb[r0:], dob[r0:]
            mask = _causal_mask(bq - r0, bk, i * bq + r0, j * bk) if masked else None
            z, e, lb, l1m = _sb_tile(qs, kb, scale, mask)
            suffix = (l_ref[rs, 0:1] - pre_ref[rs, :]) - _dot_sel2(l1m, upto)
            a = jnp.exp(lb + suffix)
            if masked:
                a = jnp.where(mask, a, 0.0)
            p = a * lax.dot_general(dos, vb, _DIMS["nt"], preferred_element_type=F32)
            both = p + (cp_ref[rs, :] + _dot_sel2(p, before))
            sg = jnp.where(z >= 0.0, 1.0, e) * pl.reciprocal(1.0 + e, approx=True)
            dz = p - both * sg
            if masked:
                dz = jnp.where(mask, dz, 0.0)
            dz = (dz * scale).astype(BF16)
            dq_acc[rs, :] += jnp.dot(dz, kb, preferred_element_type=F32)
            dk_ref[rows, :] += lax.dot_general(dz, qs, _DIMS["tn"], preferred_element_type=F32)
            dv_ref[rows, :] += lax.dot_general(a.astype(BF16), dos, _DIMS["tn"], preferred_element_type=F32)
            pre_ref[rs, :] += jnp.sum(l1m, axis=1, keepdims=True)
            cp_ref[rs, :] += jnp.sum(p, axis=1, keepdims=True)

        def step(j, carry):
            tile(j, None)
            return carry

        lax.fori_loop(0, r * i, step, 0)
        for dgl in range(r):
            tile(r * i + dgl, dgl * bk)
        dq_ref[...] = dq_acc[...].astype(BF16)

    blk = pl.BlockSpec((bq, TILE), lambda h, i: (i, h))
    head = pl.BlockSpec((s, TILE), lambda h, i: (0, h))
    return _hosted_call(
        body, [q, k, v, do, lsum], name=name, grid=(n_h, n_q), in_specs=[blk, head, head, blk, blk],
        out_specs=[blk, head, head],
        out_shape=[jax.ShapeDtypeStruct((s, hd), BF16), jax.ShapeDtypeStruct((s, hd), F32),
                   jax.ShapeDtypeStruct((s, hd), F32)],
        scratch_shapes=[pltpu.VMEM((bq, TILE), F32), pltpu.VMEM((bq, 1), F32), pltpu.VMEM((bq, 1), F32)],
        semantics=("parallel", "arbitrary"), rides=rides)


EW_BLOCK = 512 * 1024


def _ew_blocks(r, c, elems=EW_BLOCK):
    return _pick(r, max(ROWS, elems // c // ROWS * ROWS), ROWS), c


def _cast_bf16(w, layer, chip_idx, name):
    _, r, c = w.shape
    br, bc = _ew_blocks(r, c)

    def body(chip_ref, w_ref, o_ref):
        o_ref[...] = w_ref[...].astype(BF16)

    return pl.pallas_call(
        body, name=name,
        grid_spec=pltpu.PrefetchScalarGridSpec(
            num_scalar_prefetch=1, grid=(r // br, c // bc),
            in_specs=[pl.BlockSpec((None, br, bc), lambda i, j, chip_ref: (layer, i, j))],
            out_specs=pl.BlockSpec((None, br, bc), lambda i, j, chip_ref: (chip_ref[0], i, j)),
        ),
        out_shape=jax.ShapeDtypeStruct((N_CHIPS, r, c), BF16), compiler_params=_cp("parallel", "parallel"),
    )(chip_idx, w)


def _pair_add(dw, recv, c_idx, name):
    _, r, c = dw.shape
    hr = r // 2
    br, bc = _ew_blocks(hr, c)
    nb = hr // br

    def body(c_ref, a_ref, b_ref, o_ref):
        o_ref[...] = (a_ref[...].astype(F32) + b_ref[...].astype(F32)).astype(BF16)

    return pl.pallas_call(
        body, name=name,
        grid_spec=pltpu.PrefetchScalarGridSpec(
            num_scalar_prefetch=1, grid=(N_CHIPS, nb, c // bc),
            in_specs=[
                pl.BlockSpec((None, br, bc), lambda s, i, j, c_ref: (s, c_ref[0] * nb + i, j)),
                pl.BlockSpec((None, br, bc), lambda s, i, j, c_ref: (s, i, j)),
            ],
            out_specs=pl.BlockSpec((None, br, bc), lambda s, i, j, c_ref: (s, i, j)),
        ),
        out_shape=jax.ShapeDtypeStruct((N_CHIPS, hr, c), BF16),
        compiler_params=_cp("parallel", "parallel", "parallel"),
    )(c_idx, dw, recv)


def _chip_sum(parts, dest, shape, layer, c_idx, name):
    _, hr, c = parts.shape
    br, bc = _ew_blocks(hr, c, EW_BLOCK // 2)
    nb = hr // br

    def body(c_ref, p_ref, *refs):
        o_ref = refs[-1]
        acc = p_ref[0].astype(F32)
        for s in range(1, N_CHIPS):
            acc = acc + p_ref[s].astype(F32)
        o_ref[...] = acc

    in_specs = [pl.BlockSpec((N_CHIPS, br, bc), lambda i, j, c_ref: (0, i, j))]
    operands = [c_idx, parts]
    aliases = {}
    if dest is not None:
        in_specs.append(ANY)
        operands.append(dest)
        aliases = {2: 0}
    return pl.pallas_call(
        body, name=name,
        grid_spec=pltpu.PrefetchScalarGridSpec(
            num_scalar_prefetch=1, grid=(nb, c // bc), in_specs=in_specs,
            out_specs=pl.BlockSpec((None, br, bc), lambda i, j, c_ref: (layer, c_ref[0] * nb + i, j)),
        ),
        out_shape=jax.ShapeDtypeStruct(shape, F32), input_output_aliases=aliases,
        compiler_params=_cp("parallel", "parallel"),
    )(*operands)


def _adamw(w, g, m, v, name):
    n_l, r, c = w.shape
    br, bc = _ew_blocks(r, c, EW_BLOCK // 2)

    def body(w_ref, g_ref, m_ref, v_ref, d_ref, mo_ref, vo_ref):
        g = g_ref[...]
        m = ADAM_B1 * m_ref[...] + (1.0 - ADAM_B1) * g
        v = ADAM_B2 * v_ref[...] + (1.0 - ADAM_B2) * (g * g)
        m_hat = m / (1.0 - ADAM_B1 ** ADAM_STEP)
        v_hat = v / (1.0 - ADAM_B2 ** ADAM_STEP)
        d_ref[...] = -ADAM_LR * (m_hat / (jnp.sqrt(v_hat) + ADAM_EPS) + ADAM_WD * w_ref[...])
        mo_ref[...] = m
        vo_ref[...] = v

    blk = pl.BlockSpec((None, br, bc), lambda l, i, j: (l, i, j))
    return pl.pallas_call(
        body, name=name, grid=(n_l, r // br, c // bc), in_specs=[blk] * 4, out_specs=[blk] * 3,
        out_shape=[jax.ShapeDtypeStruct(w.shape, F32)] * 3, compiler_params=_cp("parallel", "parallel", "parallel"),
    )(w, g, m, v)


def _place():
    x, y, c = lax.axis_index("x"), lax.axis_index("y"), lax.axis_index("c")
    chips = [(1 - x, y), (x, 1 - y), (1 - x, 1 - y)]
    return x, y, c, chips


class _Ride:
    def __init__(self, reads, bufs, new, n_sems, start, finish):
        self.reads, self.bufs, self.new, self.n_sems, self.start, self.finish = reads, bufs, new, n_sems, start, finish


def _hosted_call(body, operands, *, name, grid, in_specs, out_specs, out_shape, scratch_shapes=(), semantics=(), rides=()):
    single = not isinstance(out_shape, (list, tuple))
    out_specs = [out_specs] if single else list(out_specs)
    out_shape = [out_shape] if single else list(out_shape)
    in_specs, scratch_shapes = list(in_specs), list(scratch_shapes)
    n_in, n_out, n_scr = len(in_specs), len(out_shape), len(scratch_shapes)
    extra_in, extra_out, aliases, where = [], [], {}, []
    for ride in rides:
        r0 = len(extra_in)
        extra_in += list(ride.reads)
        b0 = len(extra_in)
        extra_in += list(ride.bufs)
        ob0 = len(extra_out)
        extra_out += [jax.ShapeDtypeStruct(b.shape, b.dtype) for b in ride.bufs]
        for t in range(len(ride.bufs)):
            aliases[n_in + b0 + t] = n_out + ob0 + t
        on0 = len(extra_out)
        extra_out += list(ride.new)
        where.append((r0, len(ride.reads), ob0, len(ride.bufs), on0, len(ride.new)))
    n_ein, n_eout = len(extra_in), len(extra_out)
    sem_shapes = [pltpu.SemaphoreType.DMA((max(1, k),)) for ride in rides for k in ride.n_sems]

    def full_body(*refs):
        ins, outs, scr = refs[:n_in + n_ein], refs[n_in + n_ein:n_in + n_ein + n_out + n_eout], refs[n_in + n_ein + n_out + n_eout:]

        def run(which):
            for idx, (ride, (r0, nr, ob0, nb, on0, nn)) in enumerate(zip(rides, where)):
                fn = ride.start if which == 0 else ride.finish
                fn(ins[n_in + r0:n_in + r0 + nr], outs[n_out + ob0:n_out + ob0 + nb], outs[n_out + on0:n_out + on0 + nn],
                   *scr[n_scr + 3 * idx:n_scr + 3 * idx + 3])

        host = lambda: body(*ins[:n_in], *outs[:n_out], *scr[:n_scr])
        if not rides:
            host()
        elif not grid:
            run(0)
            host()
            run(1)
        else:
            ids = [pl.program_id(ax) for ax in range(len(grid))]
            first = functools.reduce(jnp.logical_and, [i == 0 for i in ids])
            last = functools.reduce(jnp.logical_and, [i == g - 1 for i, g in zip(ids, grid)])
            pl.when(first)(lambda: run(0))
            host()
            pl.when(last)(lambda: run(1))

    if rides:
        params = pltpu.CompilerParams(dimension_semantics=("arbitrary",) * len(grid), vmem_limit_bytes=VMEM_LIMIT)
    else:
        params = _cp(*semantics)
    outs = pl.pallas_call(
        full_body, name=name, grid=grid,
        in_specs=in_specs + [ANY] * n_ein, out_specs=out_specs + [ANY] * n_eout,
        out_shape=out_shape + extra_out, input_output_aliases=aliases,
        scratch_shapes=scratch_shapes + sem_shapes, compiler_params=params,
    )(*operands, *extra_in)
    main = outs[0] if single else list(outs[:n_out])
    rode = [(list(outs[n_out + ob0:n_out + ob0 + nb]), list(outs[n_out + on0:n_out + on0 + nn]))
            for (_, _, ob0, nb, on0, nn) in where]
    return main, rode


def _run_rides(rides, name):
    return _hosted_call(lambda: None, [], name=name, grid=(), in_specs=[], out_specs=[], out_shape=[], rides=rides)[1]


def _ride_gather(slots, part=0, n_parts=1):
    n = len(slots)
    halves = [a.shape[1] // 2 for a in slots]
    sizes = [hr // n_parts for hr in halves]
    for a, hr, size in zip(slots, halves, sizes):
        assert a.shape[1] == 2 * hr and hr == size * n_parts and size % ROWS == 0, a.shape

    def remote(bufs, send_sems, recv_sems, i, k, slot, core, to):
        rows = bufs[i].at[slot, pl.ds(pl.multiple_of(core * halves[i] + part * sizes[i], ROWS), sizes[i])]
        return pltpu.make_async_remote_copy(
            src_ref=rows, dst_ref=rows, send_sem=send_sems.at[i * 6 + k], recv_sem=recv_sems.at[i * 6 + k],
            device_id=to, device_id_type=MESH)

    def start(reads, bufs, new, send_sems, recv_sems, local_sems):
        x, y, c, chips = _place()
        for i in range(n):
            for k, (px, py) in enumerate(chips):
                remote(bufs, send_sems, recv_sems, i, k, 2 * x + y, c, (px, py, c)).start()

    def finish(reads, bufs, new, send_sems, recv_sems, local_sems):
        x, y, c, chips = _place()
        cp = functools.partial(remote, bufs, send_sems, recv_sems)
        for i in range(n):
            for k, (px, py) in enumerate(chips):
                cp(i, k, 2 * px + py, c, (x, y, c)).wait_recv()
                cp(i, 3 + k, 2 * px + py, c, (x, y, 1 - c)).start()
        for i in range(n):
            for k, (px, py) in enumerate(chips):
                cp(i, 3 + k, 2 * px + py, 1 - c, (x, y, c)).wait_recv()
        for i in range(n):
            for k, (px, py) in enumerate(chips):
                cp(i, k, 2 * x + y, c, (px, py, c)).wait_send()
                cp(i, 3 + k, 2 * px + py, c, (x, y, 1 - c)).wait_send()

    return _Ride([], slots, [], (6 * n, 6 * n, 0), start, finish)


def _ride_swap(grads):
    n = len(grads)
    halves = [a.shape[1] // 2 for a in grads]

    def copies(reads, new, send_sems, recv_sems):
        x, y, c, _ = _place()
        out = []
        for i in range(n):
            rows = pl.ds(pl.multiple_of((1 - c) * halves[i], 2 * SUBLANE), halves[i])
            out.append(pltpu.make_async_remote_copy(
                src_ref=reads[i].at[:, rows, :], dst_ref=new[i], send_sem=send_sems.at[i], recv_sem=recv_sems.at[i],
                device_id=(x, y, 1 - c), device_id_type=MESH))
        return out

    def start(reads, bufs, new, send_sems, recv_sems, local_sems):
        for cp in copies(reads, new, send_sems, recv_sems):
            cp.start()

    def finish(reads, bufs, new, send_sems, recv_sems, local_sems):
        for cp in copies(reads, new, send_sems, recv_sems):
            cp.wait()

    shapes = [jax.ShapeDtypeStruct((N_CHIPS, hr, a.shape[2]), a.dtype) for a, hr in zip(grads, halves)]
    return _Ride(grads, [], shapes, (n, n, 0), start, finish)


def _ride_scatter(parts, part=0, n_parts=1, into=None):
    n = len(parts)
    sizes = [a.shape[1] // n_parts for a in parts]
    for a, size in zip(parts, sizes):
        assert a.shape[1] == size * n_parts and size % ROWS == 0, a.shape

    def piece(ref, i, slot):
        return ref.at[slot, pl.ds(part * sizes[i], sizes[i])]

    def own(reads, land, local_sems, i):
        me = 2 * lax.axis_index("x") + lax.axis_index("y")
        return pltpu.make_async_copy(piece(reads[i], i, me), piece(land[i], i, me), local_sems.at[i])

    def send(reads, land, send_sems, recv_sems, i, k):
        x, y, c, chips = _place()
        px, py = chips[k]
        return pltpu.make_async_remote_copy(
            src_ref=piece(reads[i], i, 2 * px + py), dst_ref=piece(land[i], i, 2 * x + y),
            send_sem=send_sems.at[3 * i + k], recv_sem=recv_sems.at[3 * i + k],
            device_id=(px, py, c), device_id_type=MESH)

    def start(reads, bufs, new, send_sems, recv_sems, local_sems):
        land = new if into is None else bufs
        for i in range(n):
            own(reads, land, local_sems, i).start()
            for k in range(3):
                send(reads, land, send_sems, recv_sems, i, k).start()

    def finish(reads, bufs, new, send_sems, recv_sems, local_sems):
        land = new if into is None else bufs
        x, y, c, chips = _place()
        for i in range(n):
            for k, (px, py) in enumerate(chips):
                slot = piece(land[i], i, 2 * px + py)
                pltpu.make_async_remote_copy(
                    src_ref=slot, dst_ref=slot, send_sem=send_sems.at[3 * i + k], recv_sem=recv_sems.at[3 * i + k],
                    device_id=(x, y, c), device_id_type=MESH).wait_recv()
        for i in range(n):
            for k in range(3):
                send(reads, land, send_sems, recv_sems, i, k).wait_send()
            own(reads, land, local_sems, i).wait()

    shapes = [jax.ShapeDtypeStruct(a.shape, a.dtype) for a in parts]
    if into is None:
        return _Ride(parts, [], shapes, (3 * n, 3 * n, n), start, finish)
    return _Ride(parts, list(into), [], (3 * n, 3 * n, n), start, finish)


def _ride_join(grads):
    n = len(grads)

    def copy(bufs, send_sems, recv_sems, i, core, to):
        hr = grads[i].shape[1] // 2
        rows = bufs[i].at[:, pl.ds(pl.multiple_of(core * hr, SUBLANE), hr), :]
        return pltpu.make_async_remote_copy(
            src_ref=rows, dst_ref=rows, send_sem=send_sems.at[i], recv_sem=recv_sems.at[i],
            device_id=to, device_id_type=MESH)

    def start(reads, bufs, new, send_sems, recv_sems, local_sems):
        x, y, c, _ = _place()
        for i in range(n):
            copy(bufs, send_sems, recv_sems, i, c, (x, y, 1 - c)).start()

    def finish(reads, bufs, new, send_sems, recv_sems, local_sems):
        x, y, c, _ = _place()
        for i in range(n):
            copy(bufs, send_sems, recv_sems, i, 1 - c, (x, y, c)).wait_recv()
        for i in range(n):
            copy(bufs, send_sems, recv_sems, i, c, (x, y, 1 - c)).wait_send()

    return _Ride([], grads, [], (n, n, 0), start, finish)


def _all_reduce_small(packed, name, rides=()):
    r, c = packed.shape
    chunk = _pick(r, 256, ROWS)

    def body(x_ref, out_ref, gath, send_sems, recv_sems, local_sem):
        x, y, cc, chips = _place()
        me, sibling = (x, y, cc), (x, y, 1 - cc)

        def slot(px, py, pc):
            return gath.at[4 * px + 2 * py + pc]

        def copy(k, block, to, src=None):
            return pltpu.make_async_remote_copy(
                src_ref=slot(*block) if src is None else src, dst_ref=slot(*block),
                send_sem=send_sems.at[k], recv_sem=recv_sems.at[k], device_id=to, device_id_type=MESH)

        mine = pltpu.make_async_copy(x_ref, slot(*me), local_sem)
        mine.start()
        first = [copy(0, me, sibling, src=x_ref)]
        first += [copy(1 + j, me, (*chip, cc), src=x_ref) for j, chip in enumerate(chips)]
        for cp in first:
            cp.start()
        passed = [copy(4 + j, (*chip, cc), sibling) for j, chip in enumerate(chips)]
        for j, chip in enumerate(chips):
            copy(1 + j, (*chip, cc), me).wait_recv()
            passed[j].start()
        copy(0, sibling, me).wait_recv()
        for j, chip in enumerate(chips):
            copy(4 + j, (*chip, 1 - cc), me).wait_recv()
        for cp in first + passed:
            cp.wait_send()
        mine.wait()

        def add(i, carry):
            rows = pl.ds(pl.multiple_of(i * chunk, SUBLANE), chunk)
            acc = gath[0, rows, :]
            for dev in range(1, N_DEV):
                acc = acc + gath[dev, rows, :]
            out_ref[rows, :] = acc
            return carry

        lax.fori_loop(0, r // chunk, add, 0)

    return _hosted_call(
        body, [packed], name=name, grid=(), in_specs=[VMEM_SPEC], out_specs=VMEM_SPEC,
        out_shape=jax.ShapeDtypeStruct((r, c), F32),
        scratch_shapes=[pltpu.VMEM((N_DEV, r, c), F32), pltpu.SemaphoreType.DMA((7,)),
                        pltpu.SemaphoreType.DMA((7,)), pltpu.SemaphoreType.DMA],
        rides=rides)


_PACK_ROWS = 256


def _pack(arrays):
    flat = jnp.concatenate([a.reshape(-1).astype(F32) for a in arrays])
    unit = _PACK_ROWS * LANE
    total = -(-flat.shape[0] // unit) * unit
    return jnp.pad(flat, (0, total - flat.shape[0])).reshape(-1, LANE)


def _unpack(packed, shapes, lead=()):
    flat = packed.reshape(lead + (-1,))
    out, at = [], 0
    for s in shapes:
        size = math.prod(s)
        out.append(flat[..., at:at + size].reshape(lead + tuple(s)))
        at += size
    return out


def kernel(x, pre_mix_g, post_mix_g, pre_ffn_g, post_ffn_g, a_w_in, a_v_norm_g, a_w_spatial, a_b_spatial, a_w_out, kv_norm_g, w_k, w_v, b_w_q, b_w_o, ffn_w_up, ffn_conv_w, ffn_conv_b, ffn_w_down, loss_target, m_pre_mix_g, m_post_mix_g, m_pre_ffn_g, m_post_ffn_g, m_a_w_in, m_a_v_norm_g, m_a_w_spatial, m_a_b_spatial, m_a_w_out, m_kv_norm_g, m_w_k, m_w_v, m_b_w_q, m_b_w_o, m_ffn_w_up, m_ffn_conv_w, m_ffn_conv_b, m_ffn_w_down, v_pre_mix_g, v_post_mix_g, v_pre_ffn_g, v_post_ffn_g, v_a_w_in, v_a_v_norm_g, v_a_w_spatial, v_a_b_spatial, v_a_w_out, v_kv_norm_g, v_w_k, v_w_v, v_b_w_q, v_b_w_o, v_ffn_w_up, v_ffn_conv_w, v_ffn_conv_b, v_ffn_w_down):
    xi, yi, ci = lax.axis_index("x"), lax.axis_index("y"), lax.axis_index("c")
    chip = 2 * xi + yi
    c_idx = jnp.reshape(ci, (1,)).astype(jnp.int32)
    _, s, d = x.shape
    n_layers = pre_mix_g.shape[0]
    assert n_layers == 2 and a_w_in.shape[0] == 1 and b_w_q.shape[0] == 1
    d_a = a_w_out.shape[1] * N_CHIPS
    n_g = a_w_spatial.shape[1]
    ns = ffn_w_up.shape[2]
    assert a_w_spatial.shape[2] == TILE and d_a == n_g * TILE and s % TILE == 0
    h0 = x[0]
    target = loss_target[0]

    big = {
        "win": (a_w_in, m_a_w_in, v_a_w_in),
        "wout": (a_w_out, m_a_w_out, v_a_w_out),
        "wk": (w_k[None], m_w_k[None], v_w_k[None]),
        "wv": (w_v[None], m_w_v[None], v_w_v[None]),
        "wq": (b_w_q, m_b_w_q, v_b_w_q),
        "wo": (b_w_o, m_b_w_o, v_b_w_o),
        "wup": (ffn_w_up, m_ffn_w_up, v_ffn_w_up),
        "wdn": (ffn_w_down, m_ffn_w_down, v_ffn_w_down),
    }
    units = [(nm, layer) for nm in big for layer in range(big[nm][0].shape[0])]
    chip_idx = jnp.reshape(chip, (1,)).astype(jnp.int32)
    shards = [_cast_bf16(big[nm][0], layer, chip_idx, f"cast_{nm}{layer}") for nm, layer in units]
    small_sharded = _pack([a_v_norm_g, ffn_conv_w])
    small_sharded = lax.dynamic_update_index_in_dim(
        jnp.zeros((N_CHIPS,) + small_sharded.shape, F32), small_sharded, chip, 0)
    own = dict(zip(units, shards))
    full = {}

    def gather_ride(keys):
        return _ride_gather([own[key] for key in keys])

    def gathered(keys, rode):
        full.update(zip(keys, rode[0]))

    first_keys = [("win", 0), ("wout", 0)]
    (first_bufs, _), = _run_rides([_ride_gather([own[key] for key in first_keys] + [small_sharded])], "gather_first")
    full.update(zip(first_keys, first_bufs[:-1]))
    vg_parts, cw_parts = _unpack(first_bufs[-1], [a_v_norm_g.shape, ffn_conv_w.shape], lead=(N_CHIPS,))
    v_g = jnp.transpose(vg_parts, (1, 0, 2)).reshape(1, d_a)

    def rows(nm, layer=0):
        w = full[(nm, layer)]
        return w.reshape(w.shape[0] * w.shape[1], w.shape[2])

    gains = lambda g, layer: g[layer:layer + 1]
    bias = jnp.repeat(a_b_spatial[0].T, TILE, axis=1)
    w_s = a_w_spatial[0]
    kv_g = kv_norm_g[None]
    conv_w = [cw_parts[:, layer] for layer in range(n_layers)]
    conv_b = [ffn_conv_b[layer].reshape(N_CHIPS, 1, ns) for layer in range(n_layers)]

    def ffn_fwd(hn, layer, up_keys=(), down_keys=()):
        a = _mm(hn, full[("wup", layer)], "nn", f"ffn_up{layer}", out_split=N_CHIPS,
                rides=[gather_ride(up_keys)] if up_keys else ())
        if up_keys:
            a, (rode,) = a
            gathered(up_keys, rode)
        hm = _ffn_act_fwd(a, conv_w[layer], conv_b[layer], f"ffn_act{layer}")
        f = _mm(hm, rows("wdn", layer), "nn", f"ffn_down{layer}", rides=[gather_ride(down_keys)] if down_keys else ())
        if down_keys:
            f, (rode,) = f
            gathered(down_keys, rode)
        return a, hm, f[0]

    up0 = own[("wup", 0)]
    piece = lambda p: [_ride_gather([up0], part=p, n_parts=4)]
    hn0 = _rms_fwd(h0, gains(pre_mix_g, 0), "norm_in")
    uv, (((up0,), _),) = _mm(hn0, full[("win", 0)], "nn", "gmlp_in", out_split=N_CHIPS, rides=piece(0))
    gm, (((up0,), _),) = _gmlp_fwd(uv, v_g, w_s, bias, "gmlp_gate", rides=piece(1))
    mix0, (((up0,), _),) = _mm(gm, rows("wout"), "nn", "gmlp_out", rides=piece(2))
    mix0 = mix0[0]
    (h1, hn1), (((up0,), _),) = _resid_rms(
        h0, mix0, gains(post_mix_g, 0), [gains(pre_ffn_g, 0)], "resid_mix0", rides=piece(3))
    full[("wup", 0)] = up0
    a0, hm0, f0 = ffn_fwd(hn1, 0, up_keys=[("wdn", 0), ("wq", 0), ("wk", 0)], down_keys=[("wv", 0), ("wo", 0)])
    h2, hn2, kvn = _resid_rms(h1, f0, gains(post_ffn_g, 0), [gains(pre_mix_g, 1), kv_g], "resid_ffn0")
    q = _mm(hn2, rows("wq"), "nn", "proj_q", out_dtype=BF16)[0]
    k = _mm(kvn, rows("wk"), "nn", "proj_k", out_dtype=BF16)[0]
    v = _mm(kvn, rows("wv"), "nn", "proj_v", out_dtype=BF16)[0]
    last_keys = [("wup", 1), ("wdn", 1)]
    (att, lsum), (rode,) = _attn_fwd(q, k, v, "attn_fwd", rides=[gather_ride(last_keys)])
    gathered(last_keys, rode)
    mix1 = _mm(att, rows("wo"), "nn", "proj_o")[0]
    h3, hn3 = _resid_rms(h2, mix1, gains(post_mix_g, 1), [gains(pre_ffn_g, 1)], "resid_mix1")
    a1, hm1, f1 = ffn_fwd(hn3, 1)
    dh4, loss_tile = _loss_head(h3, f1, gains(post_ffn_g, 1), target, "loss_head")
    loss = lax.psum(loss_tile[0, 0], ("x", "y", "c"))

    dw = {}
    dg = {}

    pair = {}
    half_done = {nm: None for nm in big}

    def swap_ride(keys):
        return _ride_swap([dw[key] for key in keys])

    def swapped(keys, rode):
        for (nm, layer), got in zip(keys, rode[1]):
            pair[(nm, layer)] = _pair_add(dw[(nm, layer)], got, c_idx, f"pair_add_{nm}{layer}")

    def scatter_ride(keys):
        return _ride_scatter([pair[key] for key in keys])

    def scattered(keys, rode):
        for (nm, layer), got in zip(keys, rode[1]):
            half_done[nm] = _chip_sum(got, half_done[nm], big[nm][0].shape, layer, c_idx, f"chip_sum_{nm}{layer}")

    def ffn_bwd(dh_out, h_in, hn, a, hm, f, layer, act_rides=()):
        df, dg[("post_ffn", layer)] = _rms_bwd_out(dh_out, f, gains(post_ffn_g, layer), f"d_norm_ffn_out{layer}")
        dwd = _mm(hm, df, "tn", f"d_w_down{layer}", out_dtype=BF16)[0]
        down, up = [("wdn", layer)], [("wup", layer)]
        dw[down[0]] = dwd.reshape(N_CHIPS, dwd.shape[0] // N_CHIPS, d)
        dhm, (rode,) = _mm(df, rows("wdn", layer), "nt", f"d_ffn_mid{layer}", out_split=2, rides=[swap_ride(down)])
        swapped(down, rode)
        (da, dg[("conv_w", layer)], dg[("conv_b", layer)]), act_rode = _ffn_act_bwd(
            a, dhm, conv_w[layer], conv_b[layer], f"d_ffn_act{layer}", rides=act_rides)
        dw[up[0]], (rode,) = _mm(hn, da, "tn", f"d_w_up{layer}", out_dtype=BF16, out_split=N_CHIPS,
                                 rides=[scatter_ride(down)])
        scattered(down, rode)
        dhn, (rode,) = _mm(da, full[("wup", layer)], "nt", f"d_ffn_in{layer}", rides=[swap_ride(up)])
        swapped(up, rode)
        return dhn[0], act_rode

    dhn3, _ = ffn_bwd(dh4, h3, hn3, a1, hm1, f1, 1)
    dh3, (dg[("pre_ffn", 1)],) = _rms_bwd_in(dh4, h3, [([dhn3], gains(pre_ffn_g, 1))], "d_norm_ffn_in1")
    dmix1, dg[("post_mix", 1)] = _rms_bwd_out(dh3, mix1, gains(post_mix_g, 1), "d_norm_mix_out1")
    dwo = _mm(att, dmix1, "tn", "d_w_o", out_dtype=BF16)[0]
    dw[("wo", 0)] = dwo.reshape(N_CHIPS, dwo.shape[0] // N_CHIPS, d)
    datt = _mm(dmix1, rows("wo"), "nt", "d_attn_out", out_dtype=BF16)[0]
    ffn1_keys = [("wup", 1)]
    (dq, dk, dv), (rode,) = _attn_bwd(q, k, v, datt, lsum, "attn_bwd", rides=[scatter_ride(ffn1_keys)])
    scattered(ffn1_keys, rode)
    for nm, act, dact in (("wq", hn2, dq), ("wk", kvn, dk), ("wv", kvn, dv)):
        g = _mm(act, dact, "tn", f"d_{nm}", out_dtype=BF16)[0]
        dw[(nm, 0)] = g.reshape(N_CHIPS, g.shape[0] // N_CHIPS, g.shape[1])
    dhn2 = _mm(dq, rows("wq"), "nt", "d_q_in")[0]
    dkvn_k = _mm(dk, rows("wk"), "nt", "d_k_in")[0]
    attn_keys = [("wo", 0), ("wq", 0), ("wk", 0), ("wv", 0)]
    dkvn_v, (rode,) = _mm(dv, rows("wv"), "nt", "d_v_in", rides=[swap_ride(attn_keys)])
    swapped(attn_keys, rode)
    dh2, (dg[("pre_mix", 1)], dg["kv"]) = _rms_bwd_in(
        dh3, h2, [([dhn2], gains(pre_mix_g, 1)), ([dkvn_k, dkvn_v[0]], kv_g)], "d_norm_mix_in1")
    dhn1, (rode,) = ffn_bwd(dh2, h1, hn1, a0, hm0, f0, 0, act_rides=[scatter_ride(attn_keys)])
    scattered(attn_keys, rode)
    dh1, (dg[("pre_ffn", 0)],) = _rms_bwd_in(dh2, h1, [([dhn1], gains(pre_ffn_g, 0))], "d_norm_ffn_in0")
    dmix0, dg[("post_mix", 0)] = _rms_bwd_out(dh1, mix0, gains(post_mix_g, 0), "d_norm_mix_out0")
    early = ["wq", "wk", "wv", "wo", "wdn"]
    dwout, (((joined_early, _)),) = _mm(
        gm, dmix0, "tn", "d_w_out", out_dtype=BF16, rides=[_ride_join([half_done[nm] for nm in early])])
    grads_big = dict(zip(early, joined_early))
    w_out_key, w_in_key = [("wout", 0)], [("win", 0)]
    dw[w_out_key[0]] = dwout[0].reshape(N_CHIPS, dwout.shape[1] // N_CHIPS, d)
    dgm, (rode,) = _mm(dmix0, rows("wout"), "nt", "d_gmlp_gate", rides=[swap_ride(w_out_key)])
    swapped(w_out_key, rode)
    up0_pair = [pair[("wup", 0)]]
    (duv, d_ws, d_bs, d_vg), ((_, up0_landed),) = _gmlp_bwd(
        uv, dgm[0], v_g, w_s, bias, "d_gmlp", rides=[_ride_scatter(up0_pair, 0, 2)])
    dw[w_in_key[0]], (rode,) = _mm(
        hn0, duv, "tn", "d_w_in", out_dtype=BF16, out_split=N_CHIPS, rides=[scatter_ride(w_out_key)])
    scattered(w_out_key, rode)
    dhn0, (rode, (up0_landed, _)) = _mm(
        duv, full[("win", 0)], "nt", "d_gmlp_in",
        rides=[swap_ride(w_in_key), _ride_scatter(up0_pair, 1, 2, into=up0_landed)])
    swapped(w_in_key, rode)
    scattered([("wup", 0)], (None, up0_landed))
    dx, (dg[("pre_mix", 0)],), (rode,) = _rms_bwd_in(
        dh1, h0, [([dhn0[0]], gains(pre_mix_g, 0))], "d_norm_in", rides=[scatter_ride(w_in_key)])
    scattered(w_in_key, rode)

    stack = lambda key: jnp.concatenate([dg[(key, layer)] for layer in range(n_layers)], axis=0)
    small_parts = [
        stack("pre_mix"), stack("post_mix"), stack("pre_ffn"), stack("post_ffn"),
        d_vg, d_ws, d_bs[::SUBLANE], dg["kv"],
        jnp.stack([dg[("conv_w", layer)] for layer in range(n_layers)]),
        jnp.stack([dg[("conv_b", layer)] for layer in range(n_layers)]),
    ]
    late = [nm for nm in big if nm not in early]
    summed, ((joined_late, _),) = _all_reduce_small(
        _pack(small_parts), "small_grads_sum", rides=[_ride_join([half_done[nm] for nm in late])])
    grads_big.update(zip(late, joined_late))
    (g_pre_mix, g_post_mix, g_pre_ffn, g_post_ffn, g_vg, g_ws, g_bs, g_kv, g_cw, g_cb) = _unpack(
        summed, [p.shape for p in small_parts])
    g_vg = lax.dynamic_index_in_dim(g_vg.reshape(N_CHIPS, 1, d_a // N_CHIPS), chip, 0, keepdims=False)
    g_cw = lax.dynamic_index_in_dim(g_cw, chip, 1, keepdims=False)
    g_cb = g_cb.reshape(n_layers, N_CHIPS * ns)
    small = [
        (pre_mix_g, g_pre_mix, m_pre_mix_g, v_pre_mix_g),
        (post_mix_g, g_post_mix, m_post_mix_g, v_post_mix_g),
        (pre_ffn_g, g_pre_ffn, m_pre_ffn_g, v_pre_ffn_g),
        (post_ffn_g, g_post_ffn, m_post_ffn_g, v_post_ffn_g),
        (a_v_norm_g, g_vg, m_a_v_norm_g, v_a_v_norm_g),
        (a_w_spatial, g_ws[None], m_a_w_spatial, v_a_w_spatial),
        (a_b_spatial, g_bs[None], m_a_b_spatial, v_a_b_spatial),
        (kv_norm_g, g_kv.reshape(d), m_kv_norm_g, v_kv_norm_g),
        (ffn_conv_w, g_cw, m_ffn_conv_w, v_ffn_conv_w),
        (ffn_conv_b, g_cb, m_ffn_conv_b, v_ffn_conv_b),
    ]
    small = [(w, g.reshape(w.shape), m, v) for w, g, m, v in small]
    packed = [_pack([t[i] for t in small])[None] for i in range(4)]
    small_new = [_unpack(p[0], [t[0].shape for t in small]) for p in _adamw(*packed, "adamw_small")]

    new_big = {nm: _adamw(big[nm][0], grads_big[nm], big[nm][1], big[nm][2], f"adamw_{nm}") for nm in big}

    def big_out(nm, which):
        ref_shape = {"wk": w_k.shape, "wv": w_v.shape}.get(nm, big[nm][0].shape)
        arr = grads_big[nm] if which == 0 else new_big[nm][which - 1]
        return arr.reshape(ref_shape)

    order = ["pre_mix", "post_mix", "pre_ffn", "post_ffn", "win", "vg", "ws", "bs", "wout", "kv", "wk", "wv", "wq",
             "wo", "wup", "cw", "cb", "wdn"]
    small_at = {"pre_mix": 0, "post_mix": 1, "pre_ffn": 2, "post_ffn": 3, "vg": 4, "ws": 5, "bs": 6, "kv": 7,
                "cw": 8, "cb": 9}
    outs = [loss, dx[None]]
    for which in range(4):
        for nm in order:
            if nm in small_at:
                outs.append(small[small_at[nm]][1] if which == 0 else small_new[which - 1][small_at[nm]])
            else:
                outs.append(big_out(nm, which))
    return tuple(outs)
```

```python
import functools
import math

import jax
import jax.numpy as jnp
from jax import lax
from jax.experimental import pallas as pl
from jax.experimental.pallas import tpu as pltpu

F32 = jnp.float32
BF16 = jnp.bfloat16
EPS = 1e-6
ADAM_LR = 0.001
ADAM_B1 = 0.9
ADAM_B2 = 0.999
ADAM_EPS = 1e-08
ADAM_WD = 0.01
ADAM_STEP = 10

LANE = 128
SUBLANE = 8
ROWS = 16
TILE = 128
N_CHIPS = 4
N_DEV = 8
VMEM_LIMIT = 56 * 1024 * 1024
MM_VMEM = 40 * 1024 * 1024
MESH = pl.DeviceIdType.MESH
ANY = pl.BlockSpec(memory_space=pl.ANY)
VMEM_SPEC = pl.BlockSpec(memory_space=pltpu.VMEM)


def _cp(*sem):
    return pltpu.CompilerParams(dimension_semantics=sem, vmem_limit_bytes=VMEM_LIMIT)


def _pick(dim, pref, align=LANE):
    if dim <= pref:
        return dim
    best = None
    for d in range(align, pref + 1, align):
        if dim % d == 0:
            best = d
    assert best is not None, (dim, pref)
    return best


_DIMS = {
    "nn": (((1,), (0,)), ((), ())),
    "nt": (((1,), (1,)), ((), ())),
    "tn": (((0,), (0,)), ((), ())),
}


def _as3(a):
    return a if a.ndim == 3 else a[None]


def _spec3(br, bc, cols_j, rc):
    per = cols_j // bc

    def imap(m, n, k):
        r, c = rc(m, n, k)
        return (c // per, r, c % per)

    return pl.BlockSpec((None, br, bc), imap)


def _mm(a, b, mode, name, out_dtype=F32, out_split=1, rides=()):
    a, b = _as3(a), _as3(b)
    ja, ra, caj = a.shape
    jb, rb, cbj = b.shape
    if mode == "nn":
        m, k, n = ra, ja * caj, jb * cbj
        assert rb == k
        m_ext, k_ext, n_ext = [ra], [caj, rb], [cbj]
    elif mode == "nt":
        m, k, n = ra, ja * caj, rb
        assert jb * cbj == k
        m_ext, k_ext, n_ext = [ra], [caj, cbj], [rb]
    else:
        m, k, n = ja * caj, ra, jb * cbj
        assert rb == k
        m_ext, k_ext, n_ext = [caj], [ra], [cbj]
    assert n % out_split == 0
    n_ext.append(n // out_split)
    bm = _pick(math.gcd(*m_ext), 1536)
    bn = _pick(math.gcd(*n_ext), 1536)
    k_unit = math.gcd(*k_ext)
    o_bytes = jnp.dtype(out_dtype).itemsize

    def vmem_need(bk):
        tiles = bm * bk * a.dtype.itemsize + bk * bn * b.dtype.itemsize + bm * bn * o_bytes
        return 2 * tiles + (bm * bn * 4 if bk < k else 0)

    bk = max(d for d in range(LANE, k_unit + 1, LANE) if k_unit % d == 0 and (d == LANE or vmem_need(d) <= MM_VMEM))
    nk = k // bk
    if mode == "nn":
        a_spec = _spec3(bm, bk, caj, lambda mi, ni, ki: (mi, ki))
        b_spec = _spec3(bk, bn, cbj, lambda mi, ni, ki: (ki, ni))
    elif mode == "nt":
        a_spec = _spec3(bm, bk, caj, lambda mi, ni, ki: (mi, ki))
        b_spec = _spec3(bn, bk, cbj, lambda mi, ni, ki: (ni, ki))
    else:
        a_spec = _spec3(bk, bm, caj, lambda mi, ni, ki: (ki, mi))
        b_spec = _spec3(bk, bn, cbj, lambda mi, ni, ki: (ki, ni))
    o_spec = _spec3(bm, bn, n // out_split, lambda mi, ni, ki: (mi, ni))
    dims = _DIMS[mode]

    def body(a_ref, b_ref, o_ref, *acc):
        def part():
            return lax.dot_general(a_ref[...].astype(BF16), b_ref[...].astype(BF16), dims, preferred_element_type=F32)

        if nk == 1:
            o_ref[...] = part().astype(o_ref.dtype)
            return
        acc_ref, = acc
        ki = pl.program_id(2)

        @pl.when(ki == 0)
        def _():
            acc_ref[...] = part()

        @pl.when(jnp.logical_and(ki > 0, ki < nk - 1))
        def _():
            acc_ref[...] += part()

        @pl.when(ki == nk - 1)
        def _():
            o_ref[...] = (acc_ref[...] + part()).astype(o_ref.dtype)

    out, rode = _hosted_call(
        body, [a, b], name=name, grid=(m // bm, n // bn, nk), in_specs=[a_spec, b_spec], out_specs=o_spec,
        out_shape=jax.ShapeDtypeStruct((out_split, m, n // out_split), out_dtype),
        scratch_shapes=[pltpu.VMEM((bm, bn), F32)] if nk > 1 else [],
        semantics=("parallel", "parallel", "arbitrary"), rides=rides)
    return (out, rode) if rides else out


def _rms(x, g):
    r = lax.rsqrt(jnp.mean(x * x, axis=-1, keepdims=True) + EPS)
    return x * r * g


def _rms_bwd(x, g, dy):
    r = lax.rsqrt(jnp.mean(x * x, axis=-1, keepdims=True) + EPS)
    xh = x * r
    gy = dy * g
    dx = r * (gy - xh * jnp.mean(gy * xh, axis=-1, keepdims=True))
    return dx, jnp.sum(dy * xh, axis=0, keepdims=True)


def _row_block(s):
    return _pick(s, 256, ROWS)


def _rms_fwd(h, g, name):
    s, d = h.shape
    br = _row_block(s)

    def body(h_ref, g_ref, o_ref):
        o_ref[...] = _rms(h_ref[...], g_ref[...]).astype(BF16)

    row = pl.BlockSpec((br, d), lambda i: (i, 0))
    vec = pl.BlockSpec((1, d), lambda i: (0, 0))
    return pl.pallas_call(
        body, name=name, grid=(s // br,), in_specs=[row, vec], out_specs=row,
        out_shape=jax.ShapeDtypeStruct((s, d), BF16), compiler_params=_cp("parallel"),
    )(h, g)


def _resid_rms(h_in, f, g_post, g_next, name, rides=()):
    s, d = h_in.shape
    br = _row_block(s)
    n_next = len(g_next)

    def body(h_ref, f_ref, gp_ref, *refs):
        gn_refs, ho_ref, hn_refs = refs[:n_next], refs[n_next], refs[n_next + 1:]
        h = h_ref[...] + _rms(f_ref[...], gp_ref[...])
        ho_ref[...] = h
        for gn_ref, hn_ref in zip(gn_refs, hn_refs):
            hn_ref[...] = _rms(h, gn_ref[...]).astype(BF16)

    row = pl.BlockSpec((br, d), lambda i: (i, 0))
    vec = pl.BlockSpec((1, d), lambda i: (0, 0))
    outs, rode = _hosted_call(
        body, [h_in, f, g_post, *g_next], name=name, grid=(s // br,),
        in_specs=[row, row, vec] + [vec] * n_next,
        out_specs=[row] * (1 + n_next),
        out_shape=[jax.ShapeDtypeStruct((s, d), F32)] + [jax.ShapeDtypeStruct((s, d), BF16)] * n_next,
        semantics=("parallel",), rides=rides)
    return (outs, rode) if rides else outs


def _loss_head(h_in, f, g_post, target, name):
    s, d = h_in.shape
    br = _row_block(s)

    def body(h_ref, f_ref, gp_ref, t_ref, dh_ref, loss_ref):
        @pl.when(pl.program_id(0) == 0)
        def _():
            loss_ref[...] = jnp.zeros_like(loss_ref)

        diff = h_ref[...] + _rms(f_ref[...], gp_ref[...]) - t_ref[...]
        dh_ref[...] = diff * (1.0 / d)
        loss_ref[...] += 0.5 * jnp.sum(jnp.mean(diff * diff, axis=-1, keepdims=True))

    row = pl.BlockSpec((br, d), lambda i: (i, 0))
    vec = pl.BlockSpec((1, d), lambda i: (0, 0))
    return pl.pallas_call(
        body, name=name, grid=(s // br,),
        in_specs=[row, row, vec, row],
        out_specs=[row, pl.BlockSpec((SUBLANE, LANE), lambda i: (0, 0))],
        out_shape=[jax.ShapeDtypeStruct((s, d), F32), jax.ShapeDtypeStruct((SUBLANE, LANE), F32)],
        compiler_params=_cp("arbitrary"),
    )(h_in, f, g_post, target)


def _rms_bwd_out(dy, f, g, name):
    s, d = f.shape
    br = _row_block(s)

    def body(dy_ref, f_ref, g_ref, df_ref, dg_ref):
        @pl.when(pl.program_id(0) == 0)
        def _():
            dg_ref[...] = jnp.zeros_like(dg_ref)

        dx, dg = _rms_bwd(f_ref[...], g_ref[...], dy_ref[...])
        df_ref[...] = dx.astype(BF16)
        dg_ref[...] += dg

    row = pl.BlockSpec((br, d), lambda i: (i, 0))
    vec = pl.BlockSpec((1, d), lambda i: (0, 0))
    return pl.pallas_call(
        body, name=name, grid=(s // br,), in_specs=[row, row, vec], out_specs=[row, vec],
        out_shape=[jax.ShapeDtypeStruct((s, d), BF16), jax.ShapeDtypeStruct((1, d), F32)],
        compiler_params=_cp("arbitrary"),
    )(dy, f, g)


def _rms_bwd_in(dh_out, h_in, branches, name, rides=()):
    s, d = h_in.shape
    br = _row_block(s)
    counts = [len(ds) for ds, _ in branches]
    n_d = sum(counts)
    n_b = len(branches)

    def body(dho_ref, h_ref, *refs):
        d_refs, g_refs = refs[:n_d], refs[n_d:n_d + n_b]
        dh_ref, dg_refs = refs[n_d + n_b], refs[n_d + n_b + 1:]

        @pl.when(pl.program_id(0) == 0)
        def _():
            for r in dg_refs:
                r[...] = jnp.zeros_like(r)

        h = h_ref[...]
        acc = dho_ref[...]
        at = 0
        for bi, cnt in enumerate(counts):
            dn = d_refs[at][...]
            for r in d_refs[at + 1:at + cnt]:
                dn = dn + r[...]
            at += cnt
            dx, dg = _rms_bwd(h, g_refs[bi][...], dn)
            acc = acc + dx
            dg_refs[bi][...] += dg
        dh_ref[...] = acc

    row = pl.BlockSpec((br, d), lambda i: (i, 0))
    vec = pl.BlockSpec((1, d), lambda i: (0, 0))
    flat_d = [x for ds, _ in branches for x in ds]
    outs, rode = _hosted_call(
        body, [dh_out, h_in, *flat_d, *[g for _, g in branches]], name=name, grid=(s // br,),
        in_specs=[row, row] + [row] * n_d + [vec] * n_b,
        out_specs=[row] + [vec] * n_b,
        out_shape=[jax.ShapeDtypeStruct((s, d), F32)] + [jax.ShapeDtypeStruct((1, d), F32)] * n_b,
        semantics=("arbitrary",), rides=rides)
    return (outs[0], list(outs[1:]), rode) if rides else (outs[0], list(outs[1:]))


def _split3(x):
    x0 = x.astype(BF16)
    r1 = x - x0.astype(F32)
    x1 = r1.astype(BF16)
    x2 = (r1 - x1.astype(F32)).astype(BF16)
    return x0, x1, x2


def _tri(n, kind):
    r = lax.broadcasted_iota(jnp.int32, (n, n), 0)
    c = lax.broadcasted_iota(jnp.int32, (n, n), 1)
    m = {"lt": r < c, "le": r <= c, "gt": r > c}[kind]
    return jnp.where(m, 1.0, 0.0).astype(BF16)


_GELU_C = math.sqrt(2.0 / math.pi)
_GELU_A = 0.044715


def _gelu(x):
    return 0.5 * x * (1.0 + jnp.tanh(_GELU_C * (x + _GELU_A * (x * x * x))))


def _gelu_grad(x):
    t = jnp.tanh(_GELU_C * (x + _GELU_A * (x * x * x)))
    return 0.5 * (1.0 + t) + 0.5 * x * (1.0 - t * t) * (_GELU_C * (1.0 + 3.0 * _GELU_A * (x * x)))


def _causal_w(w):
    r = lax.broadcasted_iota(jnp.int32, (TILE, TILE), 0)
    c = lax.broadcasted_iota(jnp.int32, (TILE, TILE), 1)
    return jnp.where(c <= r, w, 0.0)


def _uv_tiles(uv_ref, g, d_a, dq):
    cu, cv = g * TILE, d_a + g * TILE
    u = uv_ref[cu // dq, :, pl.ds(cu % dq, TILE)]
    v = uv_ref[cv // dq, :, pl.ds(cv % dq, TILE)]
    return u, v


def _gmlp_fwd(uv, v_g, w_s, bias, name, rides=()):
    _, s, dq = uv.shape
    d_a = 2 * dq
    n_g = d_a // TILE

    def body(uv_ref, vg_ref, ws_ref, b_ref, o_ref):
        for g in range(n_g):
            up, vp = _uv_tiles(uv_ref, g, d_a, dq)
            cols = pl.ds(g * TILE, TILE)
            vn = _rms(_gelu(vp), vg_ref[:, cols])
            mixed = jnp.dot(_causal_w(ws_ref[g]).astype(BF16), vn.astype(BF16), preferred_element_type=F32) + b_ref[:, cols]
            o_ref[:, cols] = (_gelu(up) * mixed).astype(BF16)

    return _hosted_call(
        body, [uv, v_g, w_s, bias], name=name, grid=(s // TILE,),
        in_specs=[
            pl.BlockSpec((4, TILE, dq), lambda i: (0, i, 0)),
            pl.BlockSpec((1, d_a), lambda i: (0, 0)),
            pl.BlockSpec((n_g, TILE, TILE), lambda i: (0, 0, 0)),
            pl.BlockSpec((TILE, d_a), lambda i: (0, 0)),
        ],
        out_specs=pl.BlockSpec((TILE, d_a), lambda i: (i, 0)),
        out_shape=jax.ShapeDtypeStruct((s, d_a), BF16),
        semantics=("parallel",), rides=rides)


def _gmlp_bwd(uv, dgm, v_g, w_s, bias, name, rides=()):
    _, s, dq = uv.shape
    d_a = 2 * dq
    n_g = d_a // TILE
    n_c = s // TILE

    def body(uv_ref, d_ref, vg_ref, ws_ref, b_ref, duv_ref, dws_ref, dbs_ref, dvg_ref, dbias_acc):
        i = pl.program_id(0)

        @pl.when(i == 0)
        def _():
            dws_ref[...] = jnp.zeros_like(dws_ref)
            dvg_ref[...] = jnp.zeros_like(dvg_ref)
            dbias_acc[...] = jnp.zeros_like(dbias_acc)

        for g in range(n_g):
            up, vp = _uv_tiles(uv_ref, g, d_a, dq)
            cols = pl.ds(g * TILE, TILE)
            vg = vg_ref[:, cols]
            u = _gelu(up)
            v = _gelu(vp)
            r = lax.rsqrt(jnp.mean(v * v, axis=-1, keepdims=True) + EPS)
            vh = v * r
            vn = (vh * vg).astype(BF16)
            wc = _causal_w(ws_ref[g]).astype(BF16)
            mixed = jnp.dot(wc, vn, preferred_element_type=F32) + b_ref[:, cols]
            d_out = d_ref[:, cols]
            du = d_out * mixed
            dmixed = d_out * u
            dmb = dmixed.astype(BF16)
            dvn = lax.dot_general(wc, dmb, _DIMS["tn"], preferred_element_type=F32)
            dws_ref[g] += lax.dot_general(dmb, vn, _DIMS["nt"], preferred_element_type=F32)
            dbias_acc[:, cols] += dmixed
            dvg_ref[:, cols] += jnp.sum(dvn * vh, axis=0, keepdims=True)
            gv = dvn * vg
            dv = r * (gv - vh * jnp.mean(gv * vh, axis=-1, keepdims=True))
            cu, cv = g * TILE, d_a + g * TILE
            duv_ref[cu // dq, :, pl.ds(cu % dq, TILE)] = (du * _gelu_grad(up)).astype(BF16)
            duv_ref[cv // dq, :, pl.ds(cv % dq, TILE)] = (dv * _gelu_grad(vp)).astype(BF16)

        @pl.when(i == n_c - 1)
        def _():
            ones = jnp.ones((SUBLANE, TILE), BF16)
            for g in range(n_g):
                dws_ref[g] = _causal_w(dws_ref[g])
                cols = pl.ds(g * TILE, TILE)
                out = None
                for t in _split3(dbias_acc[:, cols]):
                    p = lax.dot_general(ones, t, _DIMS["nt"], preferred_element_type=F32)
                    out = p if out is None else out + p
                dbs_ref[pl.ds(g * SUBLANE, SUBLANE), :] = out

    return _hosted_call(
        body, [uv, dgm, v_g, w_s, bias], name=name, grid=(n_c,), semantics=("arbitrary",), rides=rides,
        in_specs=[
            pl.BlockSpec((4, TILE, dq), lambda i: (0, i, 0)),
            pl.BlockSpec((TILE, d_a), lambda i: (i, 0)),
            pl.BlockSpec((1, d_a), lambda i: (0, 0)),
            pl.BlockSpec((n_g, TILE, TILE), lambda i: (0, 0, 0)),
            pl.BlockSpec((TILE, d_a), lambda i: (0, 0)),
        ],
        out_specs=[
            pl.BlockSpec((4, TILE, dq), lambda i: (0, i, 0)),
            pl.BlockSpec((n_g, TILE, TILE), lambda i: (0, 0, 0)),
            pl.BlockSpec((n_g * SUBLANE, TILE), lambda i: (0, 0)),
            pl.BlockSpec((1, d_a), lambda i: (0, 0)),
        ],
        out_shape=[
            jax.ShapeDtypeStruct((4, s, dq), BF16),
            jax.ShapeDtypeStruct((n_g, TILE, TILE), F32),
            jax.ShapeDtypeStruct((n_g * SUBLANE, TILE), F32),
            jax.ShapeDtypeStruct((1, d_a), F32),
        ],
        scratch_shapes=[pltpu.VMEM((TILE, d_a), F32)])


def _sigmoid(x):
    return 1.0 / (1.0 + jnp.exp(-x))


def _conv3(ext, w, b):
    return b + ((w[0:1] * pltpu.roll(ext, 2, 0) + w[1:2] * pltpu.roll(ext, 1, 0)) + w[2:3] * ext)


def _act_blocks(s, ns):
    return _pick(s, 512, ROWS), _pick(ns, 256)


def _ffn_act_fwd(a, cw, cb, name):
    _, s, ns = a.shape
    bs, cb_w = _act_blocks(s, ns)
    hb = bs // SUBLANE

    def body(a_ref, prev_ref, cw_ref, cb_ref, o_ref):
        first = pl.program_id(0) == 0

        def conv(comp):
            prev = jnp.where(first, 0.0, prev_ref[comp])
            ext = jnp.concatenate([prev, a_ref[comp]], axis=0)
            return _conv3(ext, cw_ref[comp], cb_ref[comp])[SUBLANE:]

        for p in range(2):
            cg = conv(p)
            o_ref[p] = (cg * _sigmoid(cg) * conv(2 + p)).astype(BF16)

    return pl.pallas_call(
        body, name=name, grid=(s // bs, ns // cb_w),
        in_specs=[
            pl.BlockSpec((4, bs, cb_w), lambda i, j: (0, i, j)),
            pl.BlockSpec((4, SUBLANE, cb_w), lambda i, j: (0, jnp.maximum(i * hb - 1, 0), j)),
            pl.BlockSpec((4, 3, cb_w), lambda i, j: (0, 0, j)),
            pl.BlockSpec((4, 1, cb_w), lambda i, j: (0, 0, j)),
        ],
        out_specs=pl.BlockSpec((2, bs, cb_w), lambda i, j: (0, i, j)),
        out_shape=jax.ShapeDtypeStruct((2, s, ns), BF16),
        compiler_params=_cp("parallel", "parallel"),
    )(a, a, cw, cb)


def _ffn_act_bwd(a, dhm, cw, cb, name, rides=()):
    _, s, ns = a.shape
    bs, cb_w = _act_blocks(s, ns)
    hb = bs // SUBLANE
    n_i = s // bs
    n_ext = bs + 2 * SUBLANE
    cur = slice(SUBLANE, SUBLANE + bs)

    def body(a_ref, prev_ref, next_ref, d_ref, dnext_ref, cw_ref, cb_ref, da_ref, dcw_ref, dcb_ref):
        i = pl.program_id(1)
        first, last = i == 0, i == n_i - 1

        @pl.when(first)
        def _():
            dcw_ref[...] = jnp.zeros_like(dcw_ref)
            dcb_ref[...] = jnp.zeros_like(dcb_ref)

        def ext_of(comp):
            return jnp.concatenate([jnp.where(first, 0.0, prev_ref[comp]), a_ref[comp], next_ref[comp]], axis=0)

        def back(comp, a_ext, dc):
            w = cw_ref[comp]
            da = (w[2:3] * dc + w[1:2] * pltpu.roll(dc, n_ext - 1, 0)) + w[0:1] * pltpu.roll(dc, n_ext - 2, 0)
            da_ref[comp] = da[cur].astype(BF16)
            dcc = dc[cur]
            dcw_ref[comp, 0:1, :] += jnp.sum(dcc * pltpu.roll(a_ext, 2, 0)[cur], axis=0, keepdims=True)
            dcw_ref[comp, 1:2, :] += jnp.sum(dcc * pltpu.roll(a_ext, 1, 0)[cur], axis=0, keepdims=True)
            dcw_ref[comp, 2:3, :] += jnp.sum(dcc * a_ext[cur], axis=0, keepdims=True)
            dcb_ref[comp] += jnp.sum(dcc, axis=0, keepdims=True)

        for p in range(2):
            ag, av = ext_of(p), ext_of(2 + p)
            cg = _conv3(ag, cw_ref[p], cb_ref[p])
            cv = _conv3(av, cw_ref[2 + p], cb_ref[2 + p])
            d = jnp.concatenate(
                [jnp.zeros((SUBLANE, cb_w), F32), d_ref[p], jnp.where(last, 0.0, dnext_ref[p])], axis=0)
            sg = _sigmoid(cg)
            back(2 + p, av, d * (cg * sg))
            back(p, ag, d * cv * (sg * (1.0 + cg * (1.0 - sg))))

    return _hosted_call(
        body, [a, a, a, dhm, dhm, cw, cb], name=name, grid=(ns // cb_w, n_i),
        in_specs=[
            pl.BlockSpec((4, bs, cb_w), lambda j, i: (0, i, j)),
            pl.BlockSpec((4, SUBLANE, cb_w), lambda j, i: (0, jnp.maximum(i * hb - 1, 0), j)),
            pl.BlockSpec((4, SUBLANE, cb_w), lambda j, i: (0, jnp.minimum((i + 1) * hb, n_i * hb - 1), j)),
            pl.BlockSpec((2, bs, cb_w), lambda j, i: (0, i, j)),
            pl.BlockSpec((2, SUBLANE, cb_w), lambda j, i: (0, jnp.minimum((i + 1) * hb, n_i * hb - 1), j)),
            pl.BlockSpec((4, 3, cb_w), lambda j, i: (0, 0, j)),
            pl.BlockSpec((4, 1, cb_w), lambda j, i: (0, 0, j)),
        ],
        out_specs=[
            pl.BlockSpec((4, bs, cb_w), lambda j, i: (0, i, j)),
            pl.BlockSpec((4, 3, cb_w), lambda j, i: (0, 0, j)),
            pl.BlockSpec((4, 1, cb_w), lambda j, i: (0, 0, j)),
        ],
        out_shape=[
            jax.ShapeDtypeStruct((4, s, ns), BF16),
            jax.ShapeDtypeStruct((4, 3, ns), F32),
            jax.ShapeDtypeStruct((4, 1, ns), F32),
        ],
        semantics=("parallel", "arbitrary"), rides=rides)


ATT_BQ_FWD = 2048
ATT_BQ_BWD = 1024
ATT_BK = 256


def _att_blocks(s, bq_pref):
    bq = _pick(s, bq_pref)
    bk = min(ATT_BK, bq)
    assert bq % bk == 0
    return bq, bk


def _dot_sel2(x, sel):
    hi = x.astype(BF16)
    lo = (x - hi.astype(F32)).astype(BF16)
    n = x.shape[0]
    both = jnp.dot(jnp.concatenate([hi, lo], axis=0), sel, preferred_element_type=F32)
    return both[:n] + both[n:]


def _causal_mask(bq, bk, row0, col0):
    rows = row0 + lax.broadcasted_iota(jnp.int32, (bq, bk), 0)
    cols = col0 + lax.broadcasted_iota(jnp.int32, (bq, bk), 1)
    return cols < rows


def _sb_tile(qb, kb, scale, mask):
    z = lax.dot_general(qb, kb, _DIMS["nt"], preferred_element_type=F32) * scale
    e = jnp.exp(-jnp.abs(z))
    lb = jnp.minimum(z, 0.0) - jnp.log(1.0 + e)
    l1m = lb - z
    if mask is not None:
        l1m = jnp.where(mask, l1m, 0.0)
    return z, e, lb, l1m


def _attn_fwd(q, k, v, name, rides=()):
    s, hd = q.shape
    bq, bk = _att_blocks(s, ATT_BQ_FWD)
    r = bq // bk
    n_h, n_q = hd // TILE, s // bq
    scale = 1.0 / math.sqrt(TILE)

    def body(q_ref, k_ref, v_ref, o_ref, l_ref, acc_ref, suf_ref):
        i = pl.program_id(1)
        qb = q_ref[...]
        later = _tri(bk, "gt")
        acc_ref[...] = jnp.zeros_like(acc_ref)
        suf_ref[...] = jnp.zeros_like(suf_ref)

        def tile(j, row0):
            rows = pl.ds(pl.multiple_of(j * bk, bk), bk)
            masked = row0 is not None
            r0 = row0 if masked else 0
            rs = pl.ds(r0, bq - r0)
            mask = _causal_mask(bq - r0, bk, i * bq + r0, j * bk) if masked else None
            _, _, lb, l1m = _sb_tile(qb[r0:], k_ref[rows, :], scale, mask)
            a = jnp.exp(lb + _dot_sel2(l1m, later) + suf_ref[rs, :])
            if masked:
                a = jnp.where(mask, a, 0.0)
            acc_ref[rs, :] += jnp.dot(a.astype(BF16), v_ref[rows, :], preferred_element_type=F32)
            suf_ref[rs, :] += jnp.sum(l1m, axis=1, keepdims=True)

        for dgl in range(r - 1, -1, -1):
            tile(r * i + dgl, dgl * bk)

        def step(t, carry):
            tile(r * i - 1 - t, None)
            return carry

        lax.fori_loop(0, r * i, step, 0)
        o_ref[...] = acc_ref[...].astype(BF16)
        l_ref[...] = jnp.broadcast_to(suf_ref[...], (bq, TILE))

    blk = pl.BlockSpec((bq, TILE), lambda h, i: (i, h))
    head = pl.BlockSpec((s, TILE), lambda h, i: (0, h))
    return _hosted_call(
        body, [q, k, v], name=name, grid=(n_h, n_q), in_specs=[blk, head, head], out_specs=[blk, blk],
        out_shape=[jax.ShapeDtypeStruct((s, hd), BF16), jax.ShapeDtypeStruct((s, hd), F32)],
        scratch_shapes=[pltpu.VMEM((bq, TILE), F32), pltpu.VMEM((bq, 1), F32)],
        semantics=("parallel", "parallel"), rides=rides)


def _attn_bwd(q, k, v, do, lsum, name, rides=()):
    s, hd = q.shape
    bq, bk = _att_blocks(s, ATT_BQ_BWD)
    r = bq // bk
    n_h, n_q = hd // TILE, s // bq
    scale = 1.0 / math.sqrt(TILE)

    def body(q_ref, k_ref, v_ref, do_ref, l_ref, dq_ref, dk_ref, dv_ref, dq_acc, pre_ref, cp_ref):
        i = pl.program_id(1)

        @pl.when(i == 0)
        def _():
            dk_ref[...] = jnp.zeros_like(dk_ref)
            dv_ref[...] = jnp.zeros_like(dv_ref)

        qb = q_ref[...]
        dob = do_ref[...]
        upto = _tri(bk, "le")
        before = _tri(bk, "lt")
        dq_acc[...] = jnp.zeros_like(dq_acc)
        pre_ref[...] = jnp.zeros_like(pre_ref)
        cp_ref[...] = jnp.zeros_like(cp_ref)

        def tile(j, row0):
            rows = pl.ds(pl.multiple_of(j * bk, bk), bk)
            kb, vb = k_ref[rows, :], v_ref[rows, :]
            masked = row0 is not None
            r0 = row0 if masked else 0
            rs = pl.ds(r0, bq - r0)
            qs, dos = qb[r0:], dob[r0:]
            mask = _causal_mask(bq - r0, bk, i * bq + r0, j * bk) if masked else None
            z, e, lb, l1m = _sb_tile(qs, kb, scale, mask)
            suffix = (l_ref[rs, 0:1] - pre_ref[rs, :]) - _dot_sel2(l1m, upto)
            a = jnp.exp(lb + suffix)
            if masked:
                a = jnp.where(mask, a, 0.0)
            p = a * lax.dot_general(dos, vb, _DIMS["nt"], preferred_element_type=F32)
            both = p + (cp_ref[rs, :] + jnp.dot(p.astype(BF16), before, preferred_element_type=F32))
            sg = jnp.where(z >= 0.0, 1.0, e) * pl.reciprocal(1.0 + e, approx=True)
            dz = p - both * sg
            if masked:
                dz = jnp.where(mask, dz, 0.0)
            dz = (dz * scale).astype(BF16)
            dq_acc[rs, :] += jnp.dot(dz, kb, preferred_element_type=F32)
            dk_ref[rows, :] += lax.dot_general(dz, qs, _DIMS["tn"], preferred_element_type=F32)
            dv_ref[rows, :] += lax.dot_general(a.astype(BF16), dos, _DIMS["tn"], preferred_element_type=F32)
            pre_ref[rs, :] += jnp.sum(l1m, axis=1, keepdims=True)
            cp_ref[rs, :] += jnp.sum(p, axis=1, keepdims=True)

        def step(j, carry):
            tile(j, None)
            return carry

        lax.fori_loop(0, r * i, step, 0)
        for dgl in range(r):
            tile(r * i + dgl, dgl * bk)
        dq_ref[...] = dq_acc[...].astype(BF16)

    blk = pl.BlockSpec((bq, TILE), lambda h, i: (i, h))
    head = pl.BlockSpec((s, TILE), lambda h, i: (0, h))
    return _hosted_call(
        body, [q, k, v, do, lsum], name=name, grid=(n_h, n_q), in_specs=[blk, head, head, blk, blk],
        out_specs=[blk, head, head],
        out_shape=[jax.ShapeDtypeStruct((s, hd), BF16), jax.ShapeDtypeStruct((s, hd), F32),
                   jax.ShapeDtypeStruct((s, hd), F32)],
        scratch_shapes=[pltpu.VMEM((bq, TILE), F32), pltpu.VMEM((bq, 1), F32), pltpu.VMEM((bq, 1), F32)],
        semantics=("parallel", "arbitrary"), rides=rides)


EW_BLOCK = 512 * 1024


def _ew_blocks(r, c, elems=EW_BLOCK):
    return _pick(r, max(ROWS, elems // c // ROWS * ROWS), ROWS), c


def _cast_bf16(w, layer, chip_idx, name):
    _, r, c = w.shape
    br, bc = _ew_blocks(r, c)

    def body(chip_ref, w_ref, o_ref):
        o_ref[...] = w_ref[...].astype(BF16)

    return pl.pallas_call(
        body, name=name,
        grid_spec=pltpu.PrefetchScalarGridSpec(
            num_scalar_prefetch=1, grid=(r // br, c // bc),
            in_specs=[pl.BlockSpec((None, br, bc), lambda i, j, chip_ref: (layer, i, j))],
            out_specs=pl.BlockSpec((None, br, bc), lambda i, j, chip_ref: (chip_ref[0], i, j)),
        ),
        out_shape=jax.ShapeDtypeStruct((N_CHIPS, r, c), BF16), compiler_params=_cp("parallel", "parallel"),
    )(chip_idx, w)


def _pair_add(dw, recv, c_idx, name):
    _, r, c = dw.shape
    hr = r // 2
    br, bc = _ew_blocks(hr, c)
    nb = hr // br

    def body(c_ref, a_ref, b_ref, o_ref):
        o_ref[...] = (a_ref[...].astype(F32) + b_ref[...].astype(F32)).astype(BF16)

    return pl.pallas_call(
        body, name=name,
        grid_spec=pltpu.PrefetchScalarGridSpec(
            num_scalar_prefetch=1, grid=(N_CHIPS, nb, c // bc),
            in_specs=[
                pl.BlockSpec((None, br, bc), lambda s, i, j, c_ref: (s, c_ref[0] * nb + i, j)),
                pl.BlockSpec((None, br, bc), lambda s, i, j, c_ref: (s, i, j)),
            ],
            out_specs=pl.BlockSpec((None, br, bc), lambda s, i, j, c_ref: (s, i, j)),
        ),
        out_shape=jax.ShapeDtypeStruct((N_CHIPS, hr, c), BF16),
        compiler_params=_cp("parallel", "parallel", "parallel"),
    )(c_idx, dw, recv)


def _chip_sum(parts, dest, shape, layer, c_idx, name):
    _, hr, c = parts.shape
    br, bc = _ew_blocks(hr, c, EW_BLOCK // 2)
    nb = hr // br

    def body(c_ref, p_ref, *refs):
        o_ref = refs[-1]
        acc = p_ref[0].astype(F32)
        for s in range(1, N_CHIPS):
            acc = acc + p_ref[s].astype(F32)
        o_ref[...] = acc

    in_specs = [pl.BlockSpec((N_CHIPS, br, bc), lambda i, j, c_ref: (0, i, j))]
    operands = [c_idx, parts]
    aliases = {}
    if dest is not None:
        in_specs.append(ANY)
        operands.append(dest)
        aliases = {2: 0}
    return pl.pallas_call(
        body, name=name,
        grid_spec=pltpu.PrefetchScalarGridSpec(
            num_scalar_prefetch=1, grid=(nb, c // bc), in_specs=in_specs,
            out_specs=pl.BlockSpec((None, br, bc), lambda i, j, c_ref: (layer, c_ref[0] * nb + i, j)),
        ),
        out_shape=jax.ShapeDtypeStruct(shape, F32), input_output_aliases=aliases,
        compiler_params=_cp("parallel", "parallel"),
    )(*operands)


def _adamw(w, g, m, v, name, pass_g=False):
    n_l, r, c = w.shape
    br, bc = _ew_blocks(r, c, EW_BLOCK // 2)

    def body(w_ref, g_ref, m_ref, v_ref, *out_refs):
        d_ref, mo_ref, vo_ref = out_refs[-3:]
        g = g_ref[...]
        if pass_g:
            out_refs[0][...] = g
        m = ADAM_B1 * m_ref[...] + (1.0 - ADAM_B1) * g
        v = ADAM_B2 * v_ref[...] + (1.0 - ADAM_B2) * (g * g)
        m_hat = m / (1.0 - ADAM_B1 ** ADAM_STEP)
        v_hat = v / (1.0 - ADAM_B2 ** ADAM_STEP)
        d_ref[...] = -ADAM_LR * (m_hat / (jnp.sqrt(v_hat) + ADAM_EPS) + ADAM_WD * w_ref[...])
        mo_ref[...] = m
        vo_ref[...] = v

    blk = pl.BlockSpec((None, br, bc), lambda l, i, j: (l, i, j))
    n_out = 4 if pass_g else 3
    return pl.pallas_call(
        body, name=name, grid=(n_l, r // br, c // bc), in_specs=[blk] * 4, out_specs=[blk] * n_out,
        out_shape=[jax.ShapeDtypeStruct(w.shape, F32)] * n_out,
        compiler_params=_cp("parallel", "parallel", "parallel"),
    )(w, g, m, v)


def _place():
    x, y, c = lax.axis_index("x"), lax.axis_index("y"), lax.axis_index("c")
    chips = [(1 - x, y), (x, 1 - y), (1 - x, 1 - y)]
    return x, y, c, chips


class _Ride:
    def __init__(self, reads, bufs, new, n_sems, start, finish):
        self.reads, self.bufs, self.new, self.n_sems, self.start, self.finish = reads, bufs, new, n_sems, start, finish


def _hosted_call(body, operands, *, name, grid, in_specs, out_specs, out_shape, scratch_shapes=(), semantics=(), rides=()):
    single = not isinstance(out_shape, (list, tuple))
    out_specs = [out_specs] if single else list(out_specs)
    out_shape = [out_shape] if single else list(out_shape)
    in_specs, scratch_shapes = list(in_specs), list(scratch_shapes)
    n_in, n_out, n_scr = len(in_specs), len(out_shape), len(scratch_shapes)
    extra_in, extra_out, aliases, where = [], [], {}, []
    for ride in rides:
        r0 = len(extra_in)
        extra_in += list(ride.reads)
        b0 = len(extra_in)
        extra_in += list(ride.bufs)
        ob0 = len(extra_out)
        extra_out += [jax.ShapeDtypeStruct(b.shape, b.dtype) for b in ride.bufs]
        for t in range(len(ride.bufs)):
            aliases[n_in + b0 + t] = n_out + ob0 + t
        on0 = len(extra_out)
        extra_out += list(ride.new)
        where.append((r0, len(ride.reads), ob0, len(ride.bufs), on0, len(ride.new)))
    n_ein, n_eout = len(extra_in), len(extra_out)
    sem_shapes = [pltpu.SemaphoreType.DMA((max(1, k),)) for ride in rides for k in ride.n_sems]

    def full_body(*refs):
        ins, outs, scr = refs[:n_in + n_ein], refs[n_in + n_ein:n_in + n_ein + n_out + n_eout], refs[n_in + n_ein + n_out + n_eout:]

        def run(which):
            for idx, (ride, (r0, nr, ob0, nb, on0, nn)) in enumerate(zip(rides, where)):
                fn = ride.start if which == 0 else ride.finish
                fn(ins[n_in + r0:n_in + r0 + nr], outs[n_out + ob0:n_out + ob0 + nb], outs[n_out + on0:n_out + on0 + nn],
                   *scr[n_scr + 3 * idx:n_scr + 3 * idx + 3])

        host = lambda: body(*ins[:n_in], *outs[:n_out], *scr[:n_scr])
        if not rides:
            host()
        elif not grid:
            run(0)
            host()
            run(1)
        else:
            ids = [pl.program_id(ax) for ax in range(len(grid))]
            first = functools.reduce(jnp.logical_and, [i == 0 for i in ids])
            last = functools.reduce(jnp.logical_and, [i == g - 1 for i, g in zip(ids, grid)])
            pl.when(first)(lambda: run(0))
            host()
            pl.when(last)(lambda: run(1))

    if rides:
        params = pltpu.CompilerParams(dimension_semantics=("arbitrary",) * len(grid), vmem_limit_bytes=VMEM_LIMIT)
    else:
        params = _cp(*semantics)
    outs = pl.pallas_call(
        full_body, name=name, grid=grid,
        in_specs=in_specs + [ANY] * n_ein, out_specs=out_specs + [ANY] * n_eout,
        out_shape=out_shape + extra_out, input_output_aliases=aliases,
        scratch_shapes=scratch_shapes + sem_shapes, compiler_params=params,
    )(*operands, *extra_in)
    main = outs[0] if single else list(outs[:n_out])
    rode = [(list(outs[n_out + ob0:n_out + ob0 + nb]), list(outs[n_out + on0:n_out + on0 + nn]))
            for (_, _, ob0, nb, on0, nn) in where]
    return main, rode


def _run_rides(rides, name):
    return _hosted_call(lambda: None, [], name=name, grid=(), in_specs=[], out_specs=[], out_shape=[], rides=rides)[1]


def _ride_gather(slots, part=0, n_parts=1):
    n = len(slots)
    halves = [a.shape[1] // 2 for a in slots]
    sizes = [hr // n_parts for hr in halves]
    for a, hr, size in zip(slots, halves, sizes):
        assert a.shape[1] == 2 * hr and hr == size * n_parts and size % ROWS == 0, a.shape

    def remote(bufs, send_sems, recv_sems, i, k, slot, core, to):
        rows = bufs[i].at[slot, pl.ds(pl.multiple_of(core * halves[i] + part * sizes[i], ROWS), sizes[i])]
        return pltpu.make_async_remote_copy(
            src_ref=rows, dst_ref=rows, send_sem=send_sems.at[i * 6 + k], recv_sem=recv_sems.at[i * 6 + k],
            device_id=to, device_id_type=MESH)

    def start(reads, bufs, new, send_sems, recv_sems, local_sems):
        x, y, c, chips = _place()
        for i in range(n):
            for k, (px, py) in enumerate(chips):
                remote(bufs, send_sems, recv_sems, i, k, 2 * x + y, c, (px, py, c)).start()

    def finish(reads, bufs, new, send_sems, recv_sems, local_sems):
        x, y, c, chips = _place()
        cp = functools.partial(remote, bufs, send_sems, recv_sems)
        for i in range(n):
            for k, (px, py) in enumerate(chips):
                cp(i, k, 2 * px + py, c, (x, y, c)).wait_recv()
                cp(i, 3 + k, 2 * px + py, c, (x, y, 1 - c)).start()
        for i in range(n):
            for k, (px, py) in enumerate(chips):
                cp(i, 3 + k, 2 * px + py, 1 - c, (x, y, c)).wait_recv()
        for i in range(n):
            for k, (px, py) in enumerate(chips):
                cp(i, k, 2 * x + y, c, (px, py, c)).wait_send()
                cp(i, 3 + k, 2 * px + py, c, (x, y, 1 - c)).wait_send()

    return _Ride([], slots, [], (6 * n, 6 * n, 0), start, finish)


def _ride_swap(grads):
    n = len(grads)
    halves = [a.shape[1] // 2 for a in grads]

    def copies(reads, new, send_sems, recv_sems):
        x, y, c, _ = _place()
        out = []
        for i in range(n):
            rows = pl.ds(pl.multiple_of((1 - c) * halves[i], 2 * SUBLANE), halves[i])
            out.append(pltpu.make_async_remote_copy(
                src_ref=reads[i].at[:, rows, :], dst_ref=new[i], send_sem=send_sems.at[i], recv_sem=recv_sems.at[i],
                device_id=(x, y, 1 - c), device_id_type=MESH))
        return out

    def start(reads, bufs, new, send_sems, recv_sems, local_sems):
        for cp in copies(reads, new, send_sems, recv_sems):
            cp.start()

    def finish(reads, bufs, new, send_sems, recv_sems, local_sems):
        for cp in copies(reads, new, send_sems, recv_sems):
            cp.wait()

    shapes = [jax.ShapeDtypeStruct((N_CHIPS, hr, a.shape[2]), a.dtype) for a, hr in zip(grads, halves)]
    return _Ride(grads, [], shapes, (n, n, 0), start, finish)


def _ride_scatter(parts, part=0, n_parts=1, into=None):
    n = len(parts)
    sizes = [a.shape[1] // n_parts for a in parts]
    for a, size in zip(parts, sizes):
        assert a.shape[1] == size * n_parts and size % ROWS == 0, a.shape

    def piece(ref, i, slot):
        return ref.at[slot, pl.ds(part * sizes[i], sizes[i])]

    def own(reads, land, local_sems, i):
        me = 2 * lax.axis_index("x") + lax.axis_index("y")
        return pltpu.make_async_copy(piece(reads[i], i, me), piece(land[i], i, me), local_sems.at[i])

    def send(reads, land, send_sems, recv_sems, i, k):
        x, y, c, chips = _place()
        px, py = chips[k]
        return pltpu.make_async_remote_copy(
            src_ref=piece(reads[i], i, 2 * px + py), dst_ref=piece(land[i], i, 2 * x + y),
            send_sem=send_sems.at[3 * i + k], recv_sem=recv_sems.at[3 * i + k],
            device_id=(px, py, c), device_id_type=MESH)

    def start(reads, bufs, new, send_sems, recv_sems, local_sems):
        land = new if into is None else bufs
        for i in range(n):
            own(reads, land, local_sems, i).start()
            for k in range(3):
                send(reads, land, send_sems, recv_sems, i, k).start()

    def finish(reads, bufs, new, send_sems, recv_sems, local_sems):
        land = new if into is None else bufs
        x, y, c, chips = _place()
        for i in range(n):
            for k, (px, py) in enumerate(chips):
                slot = piece(land[i], i, 2 * px + py)
                pltpu.make_async_remote_copy(
                    src_ref=slot, dst_ref=slot, send_sem=send_sems.at[3 * i + k], recv_sem=recv_sems.at[3 * i + k],
                    device_id=(x, y, c), device_id_type=MESH).wait_recv()
        for i in range(n):
            for k in range(3):
                send(reads, land, send_sems, recv_sems, i, k).wait_send()
            own(reads, land, local_sems, i).wait()

    shapes = [jax.ShapeDtypeStruct(a.shape, a.dtype) for a in parts]
    if into is None:
        return _Ride(parts, [], shapes, (3 * n, 3 * n, n), start, finish)
    return _Ride(parts, list(into), [], (3 * n, 3 * n, n), start, finish)


def _ride_join(grads):
    n = len(grads)

    def copy(bufs, send_sems, recv_sems, i, core, to):
        hr = grads[i].shape[1] // 2
        rows = bufs[i].at[:, pl.ds(pl.multiple_of(core * hr, SUBLANE), hr), :]
        return pltpu.make_async_remote_copy(
            src_ref=rows, dst_ref=rows, send_sem=send_sems.at[i], recv_sem=recv_sems.at[i],
            device_id=to, device_id_type=MESH)

    def start(reads, bufs, new, send_sems, recv_sems, local_sems):
        x, y, c, _ = _place()
        for i in range(n):
            copy(bufs, send_sems, recv_sems, i, c, (x, y, 1 - c)).start()

    def finish(reads, bufs, new, send_sems, recv_sems, local_sems):
        x, y, c, _ = _place()
        for i in range(n):
            copy(bufs, send_sems, recv_sems, i, 1 - c, (x, y, c)).wait_recv()
        for i in range(n):
            copy(bufs, send_sems, recv_sems, i, c, (x, y, 1 - c)).wait_send()

    return _Ride([], grads, [], (n, n, 0), start, finish)


def _all_reduce_small(packed, name, rides=()):
    r, c = packed.shape
    chunk = _pick(r, 256, ROWS)

    def body(x_ref, out_ref, gath, send_sems, recv_sems, local_sem):
        x, y, cc, chips = _place()
        me, sibling = (x, y, cc), (x, y, 1 - cc)

        def slot(px, py, pc):
            return gath.at[4 * px + 2 * py + pc]

        def copy(k, block, to, src=None):
            return pltpu.make_async_remote_copy(
                src_ref=slot(*block) if src is None else src, dst_ref=slot(*block),
                send_sem=send_sems.at[k], recv_sem=recv_sems.at[k], device_id=to, device_id_type=MESH)

        mine = pltpu.make_async_copy(x_ref, slot(*me), local_sem)
        mine.start()
        first = [copy(0, me, sibling, src=x_ref)]
        first += [copy(1 + j, me, (*chip, cc), src=x_ref) for j, chip in enumerate(chips)]
        for cp in first:
            cp.start()
        passed = [copy(4 + j, (*chip, cc), sibling) for j, chip in enumerate(chips)]
        for j, chip in enumerate(chips):
            copy(1 + j, (*chip, cc), me).wait_recv()
            passed[j].start()
        copy(0, sibling, me).wait_recv()
        for j, chip in enumerate(chips):
            copy(4 + j, (*chip, 1 - cc), me).wait_recv()
        for cp in first + passed:
            cp.wait_send()
        mine.wait()

        def add(i, carry):
            rows = pl.ds(pl.multiple_of(i * chunk, SUBLANE), chunk)
            acc = gath[0, rows, :]
            for dev in range(1, N_DEV):
                acc = acc + gath[dev, rows, :]
            out_ref[rows, :] = acc
            return carry

        lax.fori_loop(0, r // chunk, add, 0)

    return _hosted_call(
        body, [packed], name=name, grid=(), in_specs=[VMEM_SPEC], out_specs=VMEM_SPEC,
        out_shape=jax.ShapeDtypeStruct((r, c), F32),
        scratch_shapes=[pltpu.VMEM((N_DEV, r, c), F32), pltpu.SemaphoreType.DMA((7,)),
                        pltpu.SemaphoreType.DMA((7,)), pltpu.SemaphoreType.DMA],
        rides=rides)


_PACK_ROWS = 256


def _pack(arrays):
    flat = jnp.concatenate([a.reshape(-1).astype(F32) for a in arrays])
    unit = _PACK_ROWS * LANE
    total = -(-flat.shape[0] // unit) * unit
    return jnp.pad(flat, (0, total - flat.shape[0])).reshape(-1, LANE)


def _unpack(packed, shapes, lead=()):
    flat = packed.reshape(lead + (-1,))
    out, at = [], 0
    for s in shapes:
        size = math.prod(s)
        out.append(flat[..., at:at + size].reshape(lead + tuple(s)))
        at += size
    return out


def kernel(x, pre_mix_g, post_mix_g, pre_ffn_g, post_ffn_g, a_w_in, a_v_norm_g, a_w_spatial, a_b_spatial, a_w_out, kv_norm_g, w_k, w_v, b_w_q, b_w_o, ffn_w_up, ffn_conv_w, ffn_conv_b, ffn_w_down, loss_target, m_pre_mix_g, m_post_mix_g, m_pre_ffn_g, m_post_ffn_g, m_a_w_in, m_a_v_norm_g, m_a_w_spatial, m_a_b_spatial, m_a_w_out, m_kv_norm_g, m_w_k, m_w_v, m_b_w_q, m_b_w_o, m_ffn_w_up, m_ffn_conv_w, m_ffn_conv_b, m_ffn_w_down, v_pre_mix_g, v_post_mix_g, v_pre_ffn_g, v_post_ffn_g, v_a_w_in, v_a_v_norm_g, v_a_w_spatial, v_a_b_spatial, v_a_w_out, v_kv_norm_g, v_w_k, v_w_v, v_b_w_q, v_b_w_o, v_ffn_w_up, v_ffn_conv_w, v_ffn_conv_b, v_ffn_w_down):
    xi, yi, ci = lax.axis_index("x"), lax.axis_index("y"), lax.axis_index("c")
    chip = 2 * xi + yi
    c_idx = jnp.reshape(ci, (1,)).astype(jnp.int32)
    _, s, d = x.shape
    n_layers = pre_mix_g.shape[0]
    assert n_layers == 2 and a_w_in.shape[0] == 1 and b_w_q.shape[0] == 1
    d_a = a_w_out.shape[1] * N_CHIPS
    n_g = a_w_spatial.shape[1]
    ns = ffn_w_up.shape[2]
    assert a_w_spatial.shape[2] == TILE and d_a == n_g * TILE and s % TILE == 0
    h0 = x[0]
    target = loss_target[0]

    big = {
        "win": (a_w_in, m_a_w_in, v_a_w_in),
        "wout": (a_w_out, m_a_w_out, v_a_w_out),
        "wk": (w_k[None], m_w_k[None], v_w_k[None]),
        "wv": (w_v[None], m_w_v[None], v_w_v[None]),
        "wq": (b_w_q, m_b_w_q, v_b_w_q),
        "wo": (b_w_o, m_b_w_o, v_b_w_o),
        "wup": (ffn_w_up, m_ffn_w_up, v_ffn_w_up),
        "wdn": (ffn_w_down, m_ffn_w_down, v_ffn_w_down),
    }
    units = [(nm, layer) for nm in big for layer in range(big[nm][0].shape[0])]
    chip_idx = jnp.reshape(chip, (1,)).astype(jnp.int32)
    shards = [_cast_bf16(big[nm][0], layer, chip_idx, f"cast_{nm}{layer}") for nm, layer in units]
    small_sharded = _pack([a_v_norm_g, ffn_conv_w])
    small_sharded = lax.dynamic_update_index_in_dim(
        jnp.zeros((N_CHIPS,) + small_sharded.shape, F32), small_sharded, chip, 0)
    own = dict(zip(units, shards))
    full = {}

    def gather_ride(keys):
        return _ride_gather([own[key] for key in keys])

    def gathered(keys, rode):
        full.update(zip(keys, rode[0]))

    first_keys = [("win", 0), ("wout", 0)]
    (first_bufs, _), = _run_rides([_ride_gather([own[key] for key in first_keys] + [small_sharded])], "gather_first")
    full.update(zip(first_keys, first_bufs[:-1]))
    vg_parts, cw_parts = _unpack(first_bufs[-1], [a_v_norm_g.shape, ffn_conv_w.shape], lead=(N_CHIPS,))
    v_g = jnp.transpose(vg_parts, (1, 0, 2)).reshape(1, d_a)

    def rows(nm, layer=0):
        w = full[(nm, layer)]
        return w.reshape(w.shape[0] * w.shape[1], w.shape[2])

    gains = lambda g, layer: g[layer:layer + 1]
    bias = jnp.repeat(a_b_spatial[0].T, TILE, axis=1)
    w_s = a_w_spatial[0]
    kv_g = kv_norm_g[None]
    conv_w = [cw_parts[:, layer] for layer in range(n_layers)]
    conv_b = [ffn_conv_b[layer].reshape(N_CHIPS, 1, ns) for layer in range(n_layers)]

    def ffn_fwd(hn, layer, up_keys=(), down_keys=()):
        a = _mm(hn, full[("wup", layer)], "nn", f"ffn_up{layer}", out_split=N_CHIPS,
                rides=[gather_ride(up_keys)] if up_keys else ())
        if up_keys:
            a, (rode,) = a
            gathered(up_keys, rode)
        hm = _ffn_act_fwd(a, conv_w[layer], conv_b[layer], f"ffn_act{layer}")
        f = _mm(hm, rows("wdn", layer), "nn", f"ffn_down{layer}", rides=[gather_ride(down_keys)] if down_keys else ())
        if down_keys:
            f, (rode,) = f
            gathered(down_keys, rode)
        return a, hm, f[0]

    up0 = own[("wup", 0)]
    piece = lambda p: [_ride_gather([up0], part=p, n_parts=4)]
    hn0 = _rms_fwd(h0, gains(pre_mix_g, 0), "norm_in")
    uv, (((up0,), _),) = _mm(hn0, full[("win", 0)], "nn", "gmlp_in", out_split=N_CHIPS, rides=piece(0))
    gm, (((up0,), _),) = _gmlp_fwd(uv, v_g, w_s, bias, "gmlp_gate", rides=piece(1))
    mix0, (((up0,), _),) = _mm(gm, rows("wout"), "nn", "gmlp_out", rides=piece(2))
    mix0 = mix0[0]
    (h1, hn1), (((up0,), _),) = _resid_rms(
        h0, mix0, gains(post_mix_g, 0), [gains(pre_ffn_g, 0)], "resid_mix0", rides=piece(3))
    full[("wup", 0)] = up0
    a0, hm0, f0 = ffn_fwd(hn1, 0, up_keys=[("wdn", 0), ("wq", 0), ("wk", 0)], down_keys=[("wv", 0), ("wo", 0)])
    h2, hn2, kvn = _resid_rms(h1, f0, gains(post_ffn_g, 0), [gains(pre_mix_g, 1), kv_g], "resid_ffn0")
    q = _mm(hn2, rows("wq"), "nn", "proj_q", out_dtype=BF16)[0]
    k = _mm(kvn, rows("wk"), "nn", "proj_k", out_dtype=BF16)[0]
    v = _mm(kvn, rows("wv"), "nn", "proj_v", out_dtype=BF16)[0]
    last_keys = [("wup", 1), ("wdn", 1)]
    (att, lsum), (rode,) = _attn_fwd(q, k, v, "attn_fwd", rides=[gather_ride(last_keys)])
    gathered(last_keys, rode)
    mix1 = _mm(att, rows("wo"), "nn", "proj_o")[0]
    h3, hn3 = _resid_rms(h2, mix1, gains(post_mix_g, 1), [gains(pre_ffn_g, 1)], "resid_mix1")
    a1, hm1, f1 = ffn_fwd(hn3, 1)
    dh4, loss_tile = _loss_head(h3, f1, gains(post_ffn_g, 1), target, "loss_head")
    loss = lax.psum(loss_tile[0, 0], ("x", "y", "c"))

    dw = {}
    dg = {}

    pair = {}
    half_done = {nm: None for nm in big}

    def swap_ride(keys):
        return _ride_swap([dw[key] for key in keys])

    def swapped(keys, rode):
        for (nm, layer), got in zip(keys, rode[1]):
            pair[(nm, layer)] = _pair_add(dw[(nm, layer)], got, c_idx, f"pair_add_{nm}{layer}")

    def scatter_ride(keys):
        return _ride_scatter([pair[key] for key in keys])

    def scattered(keys, rode):
        for (nm, layer), got in zip(keys, rode[1]):
            half_done[nm] = _chip_sum(got, half_done[nm], big[nm][0].shape, layer, c_idx, f"chip_sum_{nm}{layer}")

    def ffn_bwd(dh_out, h_in, hn, a, hm, f, layer, act_rides=()):
        df, dg[("post_ffn", layer)] = _rms_bwd_out(dh_out, f, gains(post_ffn_g, layer), f"d_norm_ffn_out{layer}")
        dwd = _mm(hm, df, "tn", f"d_w_down{layer}", out_dtype=BF16)[0]
        down, up = [("wdn", layer)], [("wup", layer)]
        dw[down[0]] = dwd.reshape(N_CHIPS, dwd.shape[0] // N_CHIPS, d)
        dhm, (rode,) = _mm(df, rows("wdn", layer), "nt", f"d_ffn_mid{layer}", out_split=2, rides=[swap_ride(down)])
        swapped(down, rode)
        (da, dg[("conv_w", layer)], dg[("conv_b", layer)]), act_rode = _ffn_act_bwd(
            a, dhm, conv_w[layer], conv_b[layer], f"d_ffn_act{layer}", rides=act_rides)
        dw[up[0]], (rode,) = _mm(hn, da, "tn", f"d_w_up{layer}", out_dtype=BF16, out_split=N_CHIPS,
                                 rides=[scatter_ride(down)])
        scattered(down, rode)
        dhn, (rode,) = _mm(da, full[("wup", layer)], "nt", f"d_ffn_in{layer}", rides=[swap_ride(up)])
        swapped(up, rode)
        return dhn[0], act_rode

    dhn3, _ = ffn_bwd(dh4, h3, hn3, a1, hm1, f1, 1)
    dh3, (dg[("pre_ffn", 1)],) = _rms_bwd_in(dh4, h3, [([dhn3], gains(pre_ffn_g, 1))], "d_norm_ffn_in1")
    dmix1, dg[("post_mix", 1)] = _rms_bwd_out(dh3, mix1, gains(post_mix_g, 1), "d_norm_mix_out1")
    dwo = _mm(att, dmix1, "tn", "d_w_o", out_dtype=BF16)[0]
    dw[("wo", 0)] = dwo.reshape(N_CHIPS, dwo.shape[0] // N_CHIPS, d)
    datt = _mm(dmix1, rows("wo"), "nt", "d_attn_out", out_dtype=BF16)[0]
    ffn1_keys = [("wup", 1)]
    (dq, dk, dv), (rode,) = _attn_bwd(q, k, v, datt, lsum, "attn_bwd", rides=[scatter_ride(ffn1_keys)])
    scattered(ffn1_keys, rode)
    for nm, act, dact in (("wq", hn2, dq), ("wk", kvn, dk), ("wv", kvn, dv)):
        g = _mm(act, dact, "tn", f"d_{nm}", out_dtype=BF16)[0]
        dw[(nm, 0)] = g.reshape(N_CHIPS, g.shape[0] // N_CHIPS, g.shape[1])
    dhn2 = _mm(dq, rows("wq"), "nt", "d_q_in")[0]
    dkvn_k = _mm(dk, rows("wk"), "nt", "d_k_in")[0]
    attn_keys = [("wo", 0), ("wq", 0), ("wk", 0), ("wv", 0)]
    dkvn_v, (rode,) = _mm(dv, rows("wv"), "nt", "d_v_in", rides=[swap_ride(attn_keys)])
    swapped(attn_keys, rode)
    dh2, (dg[("pre_mix", 1)], dg["kv"]) = _rms_bwd_in(
        dh3, h2, [([dhn2], gains(pre_mix_g, 1)), ([dkvn_k, dkvn_v[0]], kv_g)], "d_norm_mix_in1")
    dhn1, (rode,) = ffn_bwd(dh2, h1, hn1, a0, hm0, f0, 0, act_rides=[scatter_ride(attn_keys)])
    scattered(attn_keys, rode)
    dh1, (dg[("pre_ffn", 0)],) = _rms_bwd_in(dh2, h1, [([dhn1], gains(pre_ffn_g, 0))], "d_norm_ffn_in0")
    dmix0, dg[("post_mix", 0)] = _rms_bwd_out(dh1, mix0, gains(post_mix_g, 0), "d_norm_mix_out0")
    early = ["wq", "wk", "wv", "wo", "wdn"]
    dwout, (((joined_early, _)),) = _mm(
        gm, dmix0, "tn", "d_w_out", out_dtype=BF16, rides=[_ride_join([half_done[nm] for nm in early])])
    grads_big = dict(zip(early, joined_early))
    w_out_key, w_in_key = [("wout", 0)], [("win", 0)]
    dw[w_out_key[0]] = dwout[0].reshape(N_CHIPS, dwout.shape[1] // N_CHIPS, d)
    dgm, (rode,) = _mm(dmix0, rows("wout"), "nt", "d_gmlp_gate", rides=[swap_ride(w_out_key)])
    swapped(w_out_key, rode)
    up0_pair = [pair[("wup", 0)]]
    (duv, d_ws, d_bs, d_vg), ((_, up0_landed),) = _gmlp_bwd(
        uv, dgm[0], v_g, w_s, bias, "d_gmlp", rides=[_ride_scatter(up0_pair, 0, 2)])
    dw[w_in_key[0]], (rode,) = _mm(
        hn0, duv, "tn", "d_w_in", out_dtype=BF16, out_split=N_CHIPS, rides=[scatter_ride(w_out_key)])
    scattered(w_out_key, rode)
    dhn0, (rode, (up0_landed, _)) = _mm(
        duv, full[("win", 0)], "nt", "d_gmlp_in",
        rides=[swap_ride(w_in_key), _ride_scatter(up0_pair, 1, 2, into=up0_landed)])
    swapped(w_in_key, rode)
    scattered([("wup", 0)], (None, up0_landed))
    dx, (dg[("pre_mix", 0)],), (rode,) = _rms_bwd_in(
        dh1, h0, [([dhn0[0]], gains(pre_mix_g, 0))], "d_norm_in", rides=[scatter_ride(w_in_key)])
    scattered(w_in_key, rode)

    stack = lambda key: jnp.concatenate([dg[(key, layer)] for layer in range(n_layers)], axis=0)
    small_parts = [
        stack("pre_mix"), stack("post_mix"), stack("pre_ffn"), stack("post_ffn"),
        d_vg, d_ws, d_bs[::SUBLANE], dg["kv"],
        jnp.stack([dg[("conv_w", layer)] for layer in range(n_layers)]),
        jnp.stack([dg[("conv_b", layer)] for layer in range(n_layers)]),
    ]
    late = [nm for nm in big if nm not in early]
    summed, ((joined_late, _),) = _all_reduce_small(
        _pack(small_parts), "small_grads_sum", rides=[_ride_join([half_done[nm] for nm in late])])
    grads_big.update(zip(late, joined_late))
    (g_pre_mix, g_post_mix, g_pre_ffn, g_post_ffn, g_vg, g_ws, g_bs, g_kv, g_cw, g_cb) = _unpack(
        summed, [p.shape for p in small_parts])
    g_vg = lax.dynamic_index_in_dim(g_vg.reshape(N_CHIPS, 1, d_a // N_CHIPS), chip, 0, keepdims=False)
    g_cw = lax.dynamic_index_in_dim(g_cw, chip, 1, keepdims=False)
    g_cb = g_cb.reshape(n_layers, N_CHIPS * ns)
    small = [
        (pre_mix_g, g_pre_mix, m_pre_mix_g, v_pre_mix_g),
        (post_mix_g, g_post_mix, m_post_mix_g, v_post_mix_g),
        (pre_ffn_g, g_pre_ffn, m_pre_ffn_g, v_pre_ffn_g),
        (post_ffn_g, g_post_ffn, m_post_ffn_g, v_post_ffn_g),
        (a_v_norm_g, g_vg, m_a_v_norm_g, v_a_v_norm_g),
        (a_w_spatial, g_ws[None], m_a_w_spatial, v_a_w_spatial),
        (a_b_spatial, g_bs[None], m_a_b_spatial, v_a_b_spatial),
        (kv_norm_g, g_kv.reshape(d), m_kv_norm_g, v_kv_norm_g),
        (ffn_conv_w, g_cw, m_ffn_conv_w, v_ffn_conv_w),
        (ffn_conv_b, g_cb, m_ffn_conv_b, v_ffn_conv_b),
    ]
    small = [(w, g.reshape(w.shape), m, v) for w, g, m, v in small]
    packed = [_pack([t[i] for t in small])[None] for i in range(4)]
    small_new = [_unpack(p[0], [t[0].shape for t in small]) for p in _adamw(*packed, "adamw_small")]

    new_big = {nm: _adamw(big[nm][0], grads_big[nm], big[nm][1], big[nm][2], f"adamw_{nm}", pass_g=True)
               for nm in big}

    def big_out(nm, which):
        ref_shape = {"wk": w_k.shape, "wv": w_v.shape}.get(nm, big[nm][0].shape)
        return new_big[nm][which].reshape(ref_shape)

    order = ["pre_mix", "post_mix", "pre_ffn", "post_ffn", "win", "vg", "ws", "bs", "wout", "kv", "wk", "wv", "wq",
             "wo", "wup", "cw", "cb", "wdn"]
    small_at = {"pre_mix": 0, "post_mix": 1, "pre_ffn": 2, "post_ffn": 3, "vg": 4, "ws": 5, "bs": 6, "kv": 7,
                "cw": 8, "cb": 9}
    outs = [loss, dx[None]]
    for which in range(4):
        for nm in order:
            if nm in small_at:
                outs.append(small[small_at[nm]][1] if which == 0 else small_new[which - 1][small_at[nm]])
            else:
                outs.append(big_out(nm, which))
    return tuple(outs)
```

```python
import functools
import math

import jax
import jax.numpy as jnp
from jax import lax
from jax.experimental import pallas as pl
from jax.experimental.pallas import tpu as pltpu

F32 = jnp.float32
BF16 = jnp.bfloat16
EPS = 1e-6
ADAM_LR = 0.001
ADAM_B1 = 0.9
ADAM_B2 = 0.999
ADAM_EPS = 1e-08
ADAM_WD = 0.01
ADAM_STEP = 10

LANE = 128
SUBLANE = 8
ROWS = 16
TILE = 128
N_CHIPS = 4
N_DEV = 8
VMEM_LIMIT = 56 * 1024 * 1024
MM_VMEM = 40 * 1024 * 1024
MESH = pl.DeviceIdType.MESH
ANY = pl.BlockSpec(memory_space=pl.ANY)
VMEM_SPEC = pl.BlockSpec(memory_space=pltpu.VMEM)


def _cp(*sem):
    return pltpu.CompilerParams(dimension_semantics=sem, vmem_limit_bytes=VMEM_LIMIT)


def _pick(dim, pref, align=LANE):
    if dim <= pref:
        return dim
    best = None
    for d in range(align, pref + 1, align):
        if dim % d == 0:
            best = d
    assert best is not None, (dim, pref)
    return best


_DIMS = {
    "nn": (((1,), (0,)), ((), ())),
    "nt": (((1,), (1,)), ((), ())),
    "tn": (((0,), (0,)), ((), ())),
}


def _as3(a):
    return a if a.ndim == 3 else a[None]


def _spec3(br, bc, cols_j, rc):
    per = cols_j // bc

    def imap(m, n, k):
        r, c = rc(m, n, k)
        return (c // per, r, c % per)

    return pl.BlockSpec((None, br, bc), imap)


def _mm(a, b, mode, name, out_dtype=F32, out_split=1, rides=()):
    a, b = _as3(a), _as3(b)
    ja, ra, caj = a.shape
    jb, rb, cbj = b.shape
    if mode == "nn":
        m, k, n = ra, ja * caj, jb * cbj
        assert rb == k
        m_ext, k_ext, n_ext = [ra], [caj, rb], [cbj]
    elif mode == "nt":
        m, k, n = ra, ja * caj, rb
        assert jb * cbj == k
        m_ext, k_ext, n_ext = [ra], [caj, cbj], [rb]
    else:
        m, k, n = ja * caj, ra, jb * cbj
        assert rb == k
        m_ext, k_ext, n_ext = [caj], [ra], [cbj]
    assert n % out_split == 0
    n_ext.append(n // out_split)
    bm = _pick(math.gcd(*m_ext), 1536)
    bn = _pick(math.gcd(*n_ext), 1536)
    k_unit = math.gcd(*k_ext)
    o_bytes = jnp.dtype(out_dtype).itemsize

    def vmem_need(bk):
        tiles = bm * bk * a.dtype.itemsize + bk * bn * b.dtype.itemsize + bm * bn * o_bytes
        return 2 * tiles + (bm * bn * 4 if bk < k else 0)

    bk = max(d for d in range(LANE, k_unit + 1, LANE) if k_unit % d == 0 and (d == LANE or vmem_need(d) <= MM_VMEM))
    nk = k // bk
    if mode == "nn":
        a_spec = _spec3(bm, bk, caj, lambda mi, ni, ki: (mi, ki))
        b_spec = _spec3(bk, bn, cbj, lambda mi, ni, ki: (ki, ni))
    elif mode == "nt":
        a_spec = _spec3(bm, bk, caj, lambda mi, ni, ki: (mi, ki))
        b_spec = _spec3(bn, bk, cbj, lambda mi, ni, ki: (ni, ki))
    else:
        a_spec = _spec3(bk, bm, caj, lambda mi, ni, ki: (ki, mi))
        b_spec = _spec3(bk, bn, cbj, lambda mi, ni, ki: (ki, ni))
    o_spec = _spec3(bm, bn, n // out_split, lambda mi, ni, ki: (mi, ni))
    dims = _DIMS[mode]

    def body(a_ref, b_ref, o_ref, *acc):
        def part():
            return lax.dot_general(a_ref[...].astype(BF16), b_ref[...].astype(BF16), dims, preferred_element_type=F32)

        if nk == 1:
            o_ref[...] = part().astype(o_ref.dtype)
            return
        acc_ref, = acc
        ki = pl.program_id(2)

        @pl.when(ki == 0)
        def _():
            acc_ref[...] = part()

        @pl.when(jnp.logical_and(ki > 0, ki < nk - 1))
        def _():
            acc_ref[...] += part()

        @pl.when(ki == nk - 1)
        def _():
            o_ref[...] = (acc_ref[...] + part()).astype(o_ref.dtype)

    out, rode = _hosted_call(
        body, [a, b], name=name, grid=(m // bm, n // bn, nk), in_specs=[a_spec, b_spec], out_specs=o_spec,
        out_shape=jax.ShapeDtypeStruct((out_split, m, n // out_split), out_dtype),
        scratch_shapes=[pltpu.VMEM((bm, bn), F32)] if nk > 1 else [],
        semantics=("parallel", "parallel", "arbitrary"), rides=rides)
    return (out, rode) if rides else out


def _rms(x, g):
    r = lax.rsqrt(jnp.mean(x * x, axis=-1, keepdims=True) + EPS)
    return x * r * g


def _rms_bwd(x, g, dy):
    r = lax.rsqrt(jnp.mean(x * x, axis=-1, keepdims=True) + EPS)
    xh = x * r
    gy = dy * g
    dx = r * (gy - xh * jnp.mean(gy * xh, axis=-1, keepdims=True))
    return dx, jnp.sum(dy * xh, axis=0, keepdims=True)


def _row_block(s):
    return _pick(s, 256, ROWS)


def _rms_fwd(h, g, name):
    s, d = h.shape
    br = _row_block(s)

    def body(h_ref, g_ref, o_ref):
        o_ref[...] = _rms(h_ref[...], g_ref[...]).astype(BF16)

    row = pl.BlockSpec((br, d), lambda i: (i, 0))
    vec = pl.BlockSpec((1, d), lambda i: (0, 0))
    return pl.pallas_call(
        body, name=name, grid=(s // br,), in_specs=[row, vec], out_specs=row,
        out_shape=jax.ShapeDtypeStruct((s, d), BF16), compiler_params=_cp("parallel"),
    )(h, g)


def _resid_rms(h_in, f, g_post, g_next, name, rides=()):
    s, d = h_in.shape
    br = _row_block(s)
    n_next = len(g_next)

    def body(h_ref, f_ref, gp_ref, *refs):
        gn_refs, ho_ref, hn_refs = refs[:n_next], refs[n_next], refs[n_next + 1:]
        h = h_ref[...] + _rms(f_ref[...], gp_ref[...])
        ho_ref[...] = h
        for gn_ref, hn_ref in zip(gn_refs, hn_refs):
            hn_ref[...] = _rms(h, gn_ref[...]).astype(BF16)

    row = pl.BlockSpec((br, d), lambda i: (i, 0))
    vec = pl.BlockSpec((1, d), lambda i: (0, 0))
    outs, rode = _hosted_call(
        body, [h_in, f, g_post, *g_next], name=name, grid=(s // br,),
        in_specs=[row, row, vec] + [vec] * n_next,
        out_specs=[row] * (1 + n_next),
        out_shape=[jax.ShapeDtypeStruct((s, d), F32)] + [jax.ShapeDtypeStruct((s, d), BF16)] * n_next,
        semantics=("parallel",), rides=rides)
    return (outs, rode) if rides else outs


def _loss_head(h_in, f, g_post, target, name):
    s, d = h_in.shape
    br = _row_block(s)

    def body(h_ref, f_ref, gp_ref, t_ref, dh_ref, loss_ref):
        @pl.when(pl.program_id(0) == 0)
        def _():
            loss_ref[...] = jnp.zeros_like(loss_ref)

        diff = h_ref[...] + _rms(f_ref[...], gp_ref[...]) - t_ref[...]
        dh_ref[...] = diff * (1.0 / d)
        loss_ref[...] += 0.5 * jnp.sum(jnp.mean(diff * diff, axis=-1, keepdims=True))

    row = pl.BlockSpec((br, d), lambda i: (i, 0))
    vec = pl.BlockSpec((1, d), lambda i: (0, 0))
    return pl.pallas_call(
        body, name=name, grid=(s // br,),
        in_specs=[row, row, vec, row],
        out_specs=[row, pl.BlockSpec((SUBLANE, LANE), lambda i: (0, 0))],
        out_shape=[jax.ShapeDtypeStruct((s, d), F32), jax.ShapeDtypeStruct((SUBLANE, LANE), F32)],
        compiler_params=_cp("arbitrary"),
    )(h_in, f, g_post, target)


def _rms_bwd_out(dy, f, g, name):
    s, d = f.shape
    br = _row_block(s)

    def body(dy_ref, f_ref, g_ref, df_ref, dg_ref):
        @pl.when(pl.program_id(0) == 0)
        def _():
            dg_ref[...] = jnp.zeros_like(dg_ref)

        dx, dg = _rms_bwd(f_ref[...], g_ref[...], dy_ref[...])
        df_ref[...] = dx.astype(BF16)
        dg_ref[...] += dg

    row = pl.BlockSpec((br, d), lambda i: (i, 0))
    vec = pl.BlockSpec((1, d), lambda i: (0, 0))
    return pl.pallas_call(
        body, name=name, grid=(s // br,), in_specs=[row, row, vec], out_specs=[row, vec],
        out_shape=[jax.ShapeDtypeStruct((s, d), BF16), jax.ShapeDtypeStruct((1, d), F32)],
        compiler_params=_cp("arbitrary"),
    )(dy, f, g)


def _rms_bwd_in(dh_out, h_in, branches, name, rides=()):
    s, d = h_in.shape
    br = _row_block(s)
    counts = [len(ds) for ds, _ in branches]
    n_d = sum(counts)
    n_b = len(branches)

    def body(dho_ref, h_ref, *refs):
        d_refs, g_refs = refs[:n_d], refs[n_d:n_d + n_b]
        dh_ref, dg_refs = refs[n_d + n_b], refs[n_d + n_b + 1:]

        @pl.when(pl.program_id(0) == 0)
        def _():
            for r in dg_refs:
                r[...] = jnp.zeros_like(r)

        h = h_ref[...]
        acc = dho_ref[...]
        at = 0
        for bi, cnt in enumerate(counts):
            dn = d_refs[at][...]
            for r in d_refs[at + 1:at + cnt]:
                dn = dn + r[...]
            at += cnt
            dx, dg = _rms_bwd(h, g_refs[bi][...], dn)
            acc = acc + dx
            dg_refs[bi][...] += dg
        dh_ref[...] = acc

    row = pl.BlockSpec((br, d), lambda i: (i, 0))
    vec = pl.BlockSpec((1, d), lambda i: (0, 0))
    flat_d = [x for ds, _ in branches for x in ds]
    outs, rode = _hosted_call(
        body, [dh_out, h_in, *flat_d, *[g for _, g in branches]], name=name, grid=(s // br,),
        in_specs=[row, row] + [row] * n_d + [vec] * n_b,
        out_specs=[row] + [vec] * n_b,
        out_shape=[jax.ShapeDtypeStruct((s, d), F32)] + [jax.ShapeDtypeStruct((1, d), F32)] * n_b,
        semantics=("arbitrary",), rides=rides)
    return (outs[0], list(outs[1:]), rode) if rides else (outs[0], list(outs[1:]))


def _split3(x):
    x0 = x.astype(BF16)
    r1 = x - x0.astype(F32)
    x1 = r1.astype(BF16)
    x2 = (r1 - x1.astype(F32)).astype(BF16)
    return x0, x1, x2


def _tri(n, kind):
    r = lax.broadcasted_iota(jnp.int32, (n, n), 0)
    c = lax.broadcasted_iota(jnp.int32, (n, n), 1)
    m = {"lt": r < c, "le": r <= c, "gt": r > c}[kind]
    return jnp.where(m, 1.0, 0.0).astype(BF16)


_GELU_C = math.sqrt(2.0 / math.pi)
_GELU_A = 0.044715


def _gelu(x):
    return 0.5 * x * (1.0 + jnp.tanh(_GELU_C * (x + _GELU_A * (x * x * x))))


def _gelu_grad(x):
    t = jnp.tanh(_GELU_C * (x + _GELU_A * (x * x * x)))
    return 0.5 * (1.0 + t) + 0.5 * x * (1.0 - t * t) * (_GELU_C * (1.0 + 3.0 * _GELU_A * (x * x)))


def _causal_w(w):
    r = lax.broadcasted_iota(jnp.int32, (TILE, TILE), 0)
    c = lax.broadcasted_iota(jnp.int32, (TILE, TILE), 1)
    return jnp.where(c <= r, w, 0.0)


def _uv_tiles(uv_ref, g, d_a, dq):
    cu, cv = g * TILE, d_a + g * TILE
    u = uv_ref[cu // dq, :, pl.ds(cu % dq, TILE)]
    v = uv_ref[cv // dq, :, pl.ds(cv % dq, TILE)]
    return u, v


def _gmlp_fwd(uv, v_g, w_s, bias, name, rides=()):
    _, s, dq = uv.shape
    d_a = 2 * dq
    n_g = d_a // TILE

    def body(uv_ref, vg_ref, ws_ref, b_ref, o_ref):
        for g in range(n_g):
            up, vp = _uv_tiles(uv_ref, g, d_a, dq)
            cols = pl.ds(g * TILE, TILE)
            vn = _rms(_gelu(vp), vg_ref[:, cols])
            mixed = jnp.dot(_causal_w(ws_ref[g]).astype(BF16), vn.astype(BF16), preferred_element_type=F32) + b_ref[:, cols]
            o_ref[:, cols] = (_gelu(up) * mixed).astype(BF16)

    return _hosted_call(
        body, [uv, v_g, w_s, bias], name=name, grid=(s // TILE,),
        in_specs=[
            pl.BlockSpec((4, TILE, dq), lambda i: (0, i, 0)),
            pl.BlockSpec((1, d_a), lambda i: (0, 0)),
            pl.BlockSpec((n_g, TILE, TILE), lambda i: (0, 0, 0)),
            pl.BlockSpec((TILE, d_a), lambda i: (0, 0)),
        ],
        out_specs=pl.BlockSpec((TILE, d_a), lambda i: (i, 0)),
        out_shape=jax.ShapeDtypeStruct((s, d_a), BF16),
        semantics=("parallel",), rides=rides)


def _gmlp_bwd(uv, dgm, v_g, w_s, bias, name, rides=()):
    _, s, dq = uv.shape
    d_a = 2 * dq
    n_g = d_a // TILE
    n_c = s // TILE

    def body(uv_ref, d_ref, vg_ref, ws_ref, b_ref, duv_ref, dws_ref, dbs_ref, dvg_ref, dbias_acc):
        i = pl.program_id(0)

        @pl.when(i == 0)
        def _():
            dws_ref[...] = jnp.zeros_like(dws_ref)
            dvg_ref[...] = jnp.zeros_like(dvg_ref)
            dbias_acc[...] = jnp.zeros_like(dbias_acc)

        for g in range(n_g):
            up, vp = _uv_tiles(uv_ref, g, d_a, dq)
            cols = pl.ds(g * TILE, TILE)
            vg = vg_ref[:, cols]
            u = _gelu(up)
            v = _gelu(vp)
            r = lax.rsqrt(jnp.mean(v * v, axis=-1, keepdims=True) + EPS)
            vh = v * r
            vn = (vh * vg).astype(BF16)
            wc = _causal_w(ws_ref[g]).astype(BF16)
            mixed = jnp.dot(wc, vn, preferred_element_type=F32) + b_ref[:, cols]
            d_out = d_ref[:, cols]
            du = d_out * mixed
            dmixed = d_out * u
            dmb = dmixed.astype(BF16)
            dvn = lax.dot_general(wc, dmb, _DIMS["tn"], preferred_element_type=F32)
            dws_ref[g] += lax.dot_general(dmb, vn, _DIMS["nt"], preferred_element_type=F32)
            dbias_acc[:, cols] += dmixed
            dvg_ref[:, cols] += jnp.sum(dvn * vh, axis=0, keepdims=True)
            gv = dvn * vg
            dv = r * (gv - vh * jnp.mean(gv * vh, axis=-1, keepdims=True))
            cu, cv = g * TILE, d_a + g * TILE
            duv_ref[cu // dq, :, pl.ds(cu % dq, TILE)] = (du * _gelu_grad(up)).astype(BF16)
            duv_ref[cv // dq, :, pl.ds(cv % dq, TILE)] = (dv * _gelu_grad(vp)).astype(BF16)

        @pl.when(i == n_c - 1)
        def _():
            ones = jnp.ones((SUBLANE, TILE), BF16)
            for g in range(n_g):
                dws_ref[g] = _causal_w(dws_ref[g])
                cols = pl.ds(g * TILE, TILE)
                out = None
                for t in _split3(dbias_acc[:, cols]):
                    p = lax.dot_general(ones, t, _DIMS["nt"], preferred_element_type=F32)
                    out = p if out is None else out + p
                dbs_ref[pl.ds(g * SUBLANE, SUBLANE), :] = out

    return _hosted_call(
        body, [uv, dgm, v_g, w_s, bias], name=name, grid=(n_c,), semantics=("arbitrary",), rides=rides,
        in_specs=[
            pl.BlockSpec((4, TILE, dq), lambda i: (0, i, 0)),
            pl.BlockSpec((TILE, d_a), lambda i: (i, 0)),
            pl.BlockSpec((1, d_a), lambda i: (0, 0)),
            pl.BlockSpec((n_g, TILE, TILE), lambda i: (0, 0, 0)),
            pl.BlockSpec((TILE, d_a), lambda i: (0, 0)),
        ],
        out_specs=[
            pl.BlockSpec((4, TILE, dq), lambda i: (0, i, 0)),
            pl.BlockSpec((n_g, TILE, TILE), lambda i: (0, 0, 0)),
            pl.BlockSpec((n_g * SUBLANE, TILE), lambda i: (0, 0)),
            pl.BlockSpec((1, d_a), lambda i: (0, 0)),
        ],
        out_shape=[
            jax.ShapeDtypeStruct((4, s, dq), BF16),
            jax.ShapeDtypeStruct((n_g, TILE, TILE), F32),
            jax.ShapeDtypeStruct((n_g * SUBLANE, TILE), F32),
            jax.ShapeDtypeStruct((1, d_a), F32),
        ],
        scratch_shapes=[pltpu.VMEM((TILE, d_a), F32)])


def _sigmoid(x):
    return 1.0 / (1.0 + jnp.exp(-x))


def _conv3(ext, w, b):
    return b + ((w[0:1] * pltpu.roll(ext, 2, 0) + w[1:2] * pltpu.roll(ext, 1, 0)) + w[2:3] * ext)


def _act_blocks(s, ns):
    return _pick(s, 512, ROWS), _pick(ns, 256)


def _ffn_act_fwd(a, cw, cb, name):
    _, s, ns = a.shape
    bs, cb_w = _act_blocks(s, ns)
    hb = bs // SUBLANE

    def body(a_ref, prev_ref, cw_ref, cb_ref, o_ref):
        first = pl.program_id(0) == 0

        def conv(comp):
            prev = jnp.where(first, 0.0, prev_ref[comp])
            ext = jnp.concatenate([prev, a_ref[comp]], axis=0)
            return _conv3(ext, cw_ref[comp], cb_ref[comp])[SUBLANE:]

        for p in range(2):
            cg = conv(p)
            o_ref[p] = (cg * _sigmoid(cg) * conv(2 + p)).astype(BF16)

    return pl.pallas_call(
        body, name=name, grid=(s // bs, ns // cb_w),
        in_specs=[
            pl.BlockSpec((4, bs, cb_w), lambda i, j: (0, i, j)),
            pl.BlockSpec((4, SUBLANE, cb_w), lambda i, j: (0, jnp.maximum(i * hb - 1, 0), j)),
            pl.BlockSpec((4, 3, cb_w), lambda i, j: (0, 0, j)),
            pl.BlockSpec((4, 1, cb_w), lambda i, j: (0, 0, j)),
        ],
        out_specs=pl.BlockSpec((2, bs, cb_w), lambda i, j: (0, i, j)),
        out_shape=jax.ShapeDtypeStruct((2, s, ns), BF16),
        compiler_params=_cp("parallel", "parallel"),
    )(a, a, cw, cb)


def _ffn_act_bwd(a, dhm, cw, cb, name, rides=()):
    _, s, ns = a.shape
    bs, cb_w = _act_blocks(s, ns)
    hb = bs // SUBLANE
    n_i = s // bs
    n_ext = bs + 2 * SUBLANE
    cur = slice(SUBLANE, SUBLANE + bs)

    def body(a_ref, prev_ref, next_ref, d_ref, dnext_ref, cw_ref, cb_ref, da_ref, dcw_ref, dcb_ref):
        i = pl.program_id(1)
        first, last = i == 0, i == n_i - 1

        @pl.when(first)
        def _():
            dcw_ref[...] = jnp.zeros_like(dcw_ref)
            dcb_ref[...] = jnp.zeros_like(dcb_ref)

        def ext_of(comp):
            return jnp.concatenate([jnp.where(first, 0.0, prev_ref[comp]), a_ref[comp], next_ref[comp]], axis=0)

        def back(comp, a_ext, dc):
            w = cw_ref[comp]
            da = (w[2:3] * dc + w[1:2] * pltpu.roll(dc, n_ext - 1, 0)) + w[0:1] * pltpu.roll(dc, n_ext - 2, 0)
            da_ref[comp] = da[cur].astype(BF16)
            dcc = dc[cur]
            dcw_ref[comp, 0:1, :] += jnp.sum(dcc * pltpu.roll(a_ext, 2, 0)[cur], axis=0, keepdims=True)
            dcw_ref[comp, 1:2, :] += jnp.sum(dcc * pltpu.roll(a_ext, 1, 0)[cur], axis=0, keepdims=True)
            dcw_ref[comp, 2:3, :] += jnp.sum(dcc * a_ext[cur], axis=0, keepdims=True)
            dcb_ref[comp] += jnp.sum(dcc, axis=0, keepdims=True)

        for p in range(2):
            ag, av = ext_of(p), ext_of(2 + p)
            cg = _conv3(ag, cw_ref[p], cb_ref[p])
            cv = _conv3(av, cw_ref[2 + p], cb_ref[2 + p])
            d = jnp.concatenate(
                [jnp.zeros((SUBLANE, cb_w), F32), d_ref[p], jnp.where(last, 0.0, dnext_ref[p])], axis=0)
            sg = _sigmoid(cg)
            back(2 + p, av, d * (cg * sg))
            back(p, ag, d * cv * (sg * (1.0 + cg * (1.0 - sg))))

    return _hosted_call(
        body, [a, a, a, dhm, dhm, cw, cb], name=name, grid=(ns // cb_w, n_i),
        in_specs=[
            pl.BlockSpec((4, bs, cb_w), lambda j, i: (0, i, j)),
            pl.BlockSpec((4, SUBLANE, cb_w), lambda j, i: (0, jnp.maximum(i * hb - 1, 0), j)),
            pl.BlockSpec((4, SUBLANE, cb_w), lambda j, i: (0, jnp.minimum((i + 1) * hb, n_i * hb - 1), j)),
            pl.BlockSpec((2, bs, cb_w), lambda j, i: (0, i, j)),
            pl.BlockSpec((2, SUBLANE, cb_w), lambda j, i: (0, jnp.minimum((i + 1) * hb, n_i * hb - 1), j)),
            pl.BlockSpec((4, 3, cb_w), lambda j, i: (0, 0, j)),
            pl.BlockSpec((4, 1, cb_w), lambda j, i: (0, 0, j)),
        ],
        out_specs=[
            pl.BlockSpec((4, bs, cb_w), lambda j, i: (0, i, j)),
            pl.BlockSpec((4, 3, cb_w), lambda j, i: (0, 0, j)),
            pl.BlockSpec((4, 1, cb_w), lambda j, i: (0, 0, j)),
        ],
        out_shape=[
            jax.ShapeDtypeStruct((4, s, ns), BF16),
            jax.ShapeDtypeStruct((4, 3, ns), F32),
            jax.ShapeDtypeStruct((4, 1, ns), F32),
        ],
        semantics=("parallel", "arbitrary"), rides=rides)


ATT_BQ_FWD = 2048
ATT_BQ_BWD = 1024
ATT_BK = 256
ATT_UNROLL = 2


def _att_blocks(s, bq_pref):
    bq = _pick(s, bq_pref)
    bk = min(ATT_BK, bq)
    assert bq % bk == 0
    return bq, bk


def _dot_sel2(x, sel):
    hi = x.astype(BF16)
    lo = (x - hi.astype(F32)).astype(BF16)
    n = x.shape[0]
    both = jnp.dot(jnp.concatenate([hi, lo], axis=0), sel, preferred_element_type=F32)
    return both[:n] + both[n:]


def _causal_mask(bq, bk, row0, col0):
    rows = row0 + lax.broadcasted_iota(jnp.int32, (bq, bk), 0)
    cols = col0 + lax.broadcasted_iota(jnp.int32, (bq, bk), 1)
    return cols < rows


def _sb_tile(qb, kb, scale, mask):
    z = lax.dot_general(qb, kb, _DIMS["nt"], preferred_element_type=F32) * scale
    e = jnp.exp(-jnp.abs(z))
    lb = jnp.minimum(z, 0.0) - jnp.log(1.0 + e)
    l1m = lb - z
    if mask is not None:
        l1m = jnp.where(mask, l1m, 0.0)
    return z, e, lb, l1m


def _attn_fwd(q, k, v, name, rides=()):
    s, hd = q.shape
    bq, bk = _att_blocks(s, ATT_BQ_FWD)
    r = bq // bk
    n_h, n_q = hd // TILE, s // bq
    scale = 1.0 / math.sqrt(TILE)

    def body(q_ref, k_ref, v_ref, o_ref, l_ref, acc_ref, suf_ref):
        i = pl.program_id(1)
        qb = q_ref[...]
        later = _tri(bk, "gt")
        acc_ref[...] = jnp.zeros_like(acc_ref)
        suf_ref[...] = jnp.zeros_like(suf_ref)

        def tile(j, row0):
            rows = pl.ds(pl.multiple_of(j * bk, bk), bk)
            masked = row0 is not None
            r0 = row0 if masked else 0
            rs = pl.ds(r0, bq - r0)
            mask = _causal_mask(bq - r0, bk, i * bq + r0, j * bk) if masked else None
            _, _, lb, l1m = _sb_tile(qb[r0:], k_ref[rows, :], scale, mask)
            a = jnp.exp(lb + _dot_sel2(l1m, later) + suf_ref[rs, :])
            if masked:
                a = jnp.where(mask, a, 0.0)
            acc_ref[rs, :] += jnp.dot(a.astype(BF16), v_ref[rows, :], preferred_element_type=F32)
            suf_ref[rs, :] += jnp.sum(l1m, axis=1, keepdims=True)

        for dgl in range(r - 1, -1, -1):
            tile(r * i + dgl, dgl * bk)

        def step(t, carry):
            for u in range(ATT_UNROLL):
                tile(r * i - 1 - (ATT_UNROLL * t + u), None)
            return carry

        lax.fori_loop(0, (r * i) // ATT_UNROLL, step, 0)
        o_ref[...] = acc_ref[...].astype(BF16)
        l_ref[...] = jnp.broadcast_to(suf_ref[...], (bq, TILE))

    blk = pl.BlockSpec((bq, TILE), lambda h, i: (i, h))
    head = pl.BlockSpec((s, TILE), lambda h, i: (0, h))
    return _hosted_call(
        body, [q, k, v], name=name, grid=(n_h, n_q), in_specs=[blk, head, head], out_specs=[blk, blk],
        out_shape=[jax.ShapeDtypeStruct((s, hd), BF16), jax.ShapeDtypeStruct((s, hd), F32)],
        scratch_shapes=[pltpu.VMEM((bq, TILE), F32), pltpu.VMEM((bq, 1), F32)],
        semantics=("parallel", "parallel"), rides=rides)


def _attn_bwd(q, k, v, do, lsum, name, rides=()):
    s, hd = q.shape
    bq, bk = _att_blocks(s, ATT_BQ_BWD)
    r = bq // bk
    n_h, n_q = hd // TILE, s // bq
    scale = 1.0 / math.sqrt(TILE)

    def body(q_ref, k_ref, v_ref, do_ref, l_ref, dq_ref, dk_ref, dv_ref, dq_acc, pre_ref, cp_ref):
        i = pl.program_id(1)

        @pl.when(i == 0)
        def _():
            dk_ref[...] = jnp.zeros_like(dk_ref)
            dv_ref[...] = jnp.zeros_like(dv_ref)

        qb = q_ref[...]
        dob = do_ref[...]
        upto = _tri(bk, "le")
        before = _tri(bk, "lt")
        dq_acc[...] = jnp.zeros_like(dq_acc)
        pre_ref[...] = jnp.zeros_like(pre_ref)
        cp_ref[...] = jnp.zeros_like(cp_ref)

        def tile(j, row0):
            rows = pl.ds(pl.multiple_of(j * bk, bk), bk)
            kb, vb = k_ref[rows, :], v_ref[rows, :]
            masked = row0 is not None
            r0 = row0 if masked else 0
            rs = pl.ds(r0, bq - r0)
            qs, dos = qb[r0:], dob[r0:]
            mask = _causal_mask(bq - r0, bk, i * bq + r0, j * bk) if masked else None
            z, e, lb, l1m = _sb_tile(qs, kb, scale, mask)
            suffix = (l_ref[rs, 0:1] - pre_ref[rs, :]) - _dot_sel2(l1m, upto)
            a = jnp.exp(lb + suffix)
            if masked:
                a = jnp.where(mask, a, 0.0)
            p = a * lax.dot_general(dos, vb, _DIMS["nt"], preferred_element_type=F32)
            both = p + (cp_ref[rs, :] + jnp.dot(p.astype(BF16), before, preferred_element_type=F32))
            sg = jnp.where(z >= 0.0, 1.0, e) * pl.reciprocal(1.0 + e, approx=True)
            dz = p - both * sg
            if masked:
                dz = jnp.where(mask, dz, 0.0)
            dz = (dz * scale).astype(BF16)
            dq_acc[rs, :] += jnp.dot(dz, kb, preferred_element_type=F32)
            dk_ref[rows, :] += lax.dot_general(dz, qs, _DIMS["tn"], preferred_element_type=F32)
            dv_ref[rows, :] += lax.dot_general(a.astype(BF16), dos, _DIMS["tn"], preferred_element_type=F32)
            pre_ref[rs, :] += jnp.sum(l1m, axis=1, keepdims=True)
            cp_ref[rs, :] += jnp.sum(p, axis=1, keepdims=True)

        def step(j, carry):
            for u in range(ATT_UNROLL):
                tile(ATT_UNROLL * j + u, None)
            return carry

        lax.fori_loop(0, (r * i) // ATT_UNROLL, step, 0)
        for dgl in range(r):
            tile(r * i + dgl, dgl * bk)
        dq_ref[...] = dq_acc[...].astype(BF16)

    blk = pl.BlockSpec((bq, TILE), lambda h, i: (i, h))
    head = pl.BlockSpec((s, TILE), lambda h, i: (0, h))
    return _hosted_call(
        body, [q, k, v, do, lsum], name=name, grid=(n_h, n_q), in_specs=[blk, head, head, blk, blk],
        out_specs=[blk, head, head],
        out_shape=[jax.ShapeDtypeStruct((s, hd), BF16), jax.ShapeDtypeStruct((s, hd), F32),
                   jax.ShapeDtypeStruct((s, hd), F32)],
        scratch_shapes=[pltpu.VMEM((bq, TILE), F32), pltpu.VMEM((bq, 1), F32), pltpu.VMEM((bq, 1), F32)],
        semantics=("parallel", "arbitrary"), rides=rides)


EW_BLOCK = 512 * 1024


def _ew_blocks(r, c, elems=EW_BLOCK):
    return _pick(r, max(ROWS, elems // c // ROWS * ROWS), ROWS), c


def _cast_bf16(w, layer, chip_idx, name):
    _, r, c = w.shape
    br, bc = _ew_blocks(r, c)

    def body(chip_ref, w_ref, o_ref):
        o_ref[...] = w_ref[...].astype(BF16)

    return pl.pallas_call(
        body, name=name,
        grid_spec=pltpu.PrefetchScalarGridSpec(
            num_scalar_prefetch=1, grid=(r // br, c // bc),
            in_specs=[pl.BlockSpec((None, br, bc), lambda i, j, chip_ref: (layer, i, j))],
            out_specs=pl.BlockSpec((None, br, bc), lambda i, j, chip_ref: (chip_ref[0], i, j)),
        ),
        out_shape=jax.ShapeDtypeStruct((N_CHIPS, r, c), BF16), compiler_params=_cp("parallel", "parallel"),
    )(chip_idx, w)


def _pair_add(dw, recv, c_idx, name):
    _, r, c = dw.shape
    hr = r // 2
    br, bc = _ew_blocks(hr, c)
    nb = hr // br

    def body(c_ref, a_ref, b_ref, o_ref):
        o_ref[...] = (a_ref[...].astype(F32) + b_ref[...].astype(F32)).astype(BF16)

    return pl.pallas_call(
        body, name=name,
        grid_spec=pltpu.PrefetchScalarGridSpec(
            num_scalar_prefetch=1, grid=(N_CHIPS, nb, c // bc),
            in_specs=[
                pl.BlockSpec((None, br, bc), lambda s, i, j, c_ref: (s, c_ref[0] * nb + i, j)),
                pl.BlockSpec((None, br, bc), lambda s, i, j, c_ref: (s, i, j)),
            ],
            out_specs=pl.BlockSpec((None, br, bc), lambda s, i, j, c_ref: (s, i, j)),
        ),
        out_shape=jax.ShapeDtypeStruct((N_CHIPS, hr, c), BF16),
        compiler_params=_cp("parallel", "parallel", "parallel"),
    )(c_idx, dw, recv)


def _chip_sum(parts, dest, shape, layer, c_idx, name):
    _, hr, c = parts.shape
    br, bc = _ew_blocks(hr, c, EW_BLOCK // 2)
    nb = hr // br

    def body(c_ref, p_ref, *refs):
        o_ref = refs[-1]
        acc = p_ref[0].astype(F32)
        for s in range(1, N_CHIPS):
            acc = acc + p_ref[s].astype(F32)
        o_ref[...] = acc

    in_specs = [pl.BlockSpec((N_CHIPS, br, bc), lambda i, j, c_ref: (0, i, j))]
    operands = [c_idx, parts]
    aliases = {}
    if dest is not None:
        in_specs.append(ANY)
        operands.append(dest)
        aliases = {2: 0}
    return pl.pallas_call(
        body, name=name,
        grid_spec=pltpu.PrefetchScalarGridSpec(
            num_scalar_prefetch=1, grid=(nb, c // bc), in_specs=in_specs,
            out_specs=pl.BlockSpec((None, br, bc), lambda i, j, c_ref: (layer, c_ref[0] * nb + i, j)),
        ),
        out_shape=jax.ShapeDtypeStruct(shape, F32), input_output_aliases=aliases,
        compiler_params=_cp("parallel", "parallel"),
    )(*operands)


def _adamw(w, g, m, v, name, pass_g=False):
    n_l, r, c = w.shape
    br, bc = _ew_blocks(r, c, EW_BLOCK // 2)

    def body(w_ref, g_ref, m_ref, v_ref, *out_refs):
        d_ref, mo_ref, vo_ref = out_refs[-3:]
        g = g_ref[...]
        if pass_g:
            out_refs[0][...] = g
        m = ADAM_B1 * m_ref[...] + (1.0 - ADAM_B1) * g
        v = ADAM_B2 * v_ref[...] + (1.0 - ADAM_B2) * (g * g)
        m_hat = m / (1.0 - ADAM_B1 ** ADAM_STEP)
        v_hat = v / (1.0 - ADAM_B2 ** ADAM_STEP)
        d_ref[...] = -ADAM_LR * (m_hat / (jnp.sqrt(v_hat) + ADAM_EPS) + ADAM_WD * w_ref[...])
        mo_ref[...] = m
        vo_ref[...] = v

    blk = pl.BlockSpec((None, br, bc), lambda l, i, j: (l, i, j))
    n_out = 4 if pass_g else 3
    return pl.pallas_call(
        body, name=name, grid=(n_l, r // br, c // bc), in_specs=[blk] * 4, out_specs=[blk] * n_out,
        out_shape=[jax.ShapeDtypeStruct(w.shape, F32)] * n_out,
        compiler_params=_cp("parallel", "parallel", "parallel"),
    )(w, g, m, v)


def _place():
    x, y, c = lax.axis_index("x"), lax.axis_index("y"), lax.axis_index("c")
    chips = [(1 - x, y), (x, 1 - y), (1 - x, 1 - y)]
    return x, y, c, chips


class _Ride:
    def __init__(self, reads, bufs, new, n_sems, start, finish):
        self.reads, self.bufs, self.new, self.n_sems, self.start, self.finish = reads, bufs, new, n_sems, start, finish


def _hosted_call(body, operands, *, name, grid, in_specs, out_specs, out_shape, scratch_shapes=(), semantics=(), rides=()):
    single = not isinstance(out_shape, (list, tuple))
    out_specs = [out_specs] if single else list(out_specs)
    out_shape = [out_shape] if single else list(out_shape)
    in_specs, scratch_shapes = list(in_specs), list(scratch_shapes)
    n_in, n_out, n_scr = len(in_specs), len(out_shape), len(scratch_shapes)
    extra_in, extra_out, aliases, where = [], [], {}, []
    for ride in rides:
        r0 = len(extra_in)
        extra_in += list(ride.reads)
        b0 = len(extra_in)
        extra_in += list(ride.bufs)
        ob0 = len(extra_out)
        extra_out += [jax.ShapeDtypeStruct(b.shape, b.dtype) for b in ride.bufs]
        for t in range(len(ride.bufs)):
            aliases[n_in + b0 + t] = n_out + ob0 + t
        on0 = len(extra_out)
        extra_out += list(ride.new)
        where.append((r0, len(ride.reads), ob0, len(ride.bufs), on0, len(ride.new)))
    n_ein, n_eout = len(extra_in), len(extra_out)
    sem_shapes = [pltpu.SemaphoreType.DMA((max(1, k),)) for ride in rides for k in ride.n_sems]

    def full_body(*refs):
        ins, outs, scr = refs[:n_in + n_ein], refs[n_in + n_ein:n_in + n_ein + n_out + n_eout], refs[n_in + n_ein + n_out + n_eout:]

        def run(which):
            for idx, (ride, (r0, nr, ob0, nb, on0, nn)) in enumerate(zip(rides, where)):
                fn = ride.start if which == 0 else ride.finish
                fn(ins[n_in + r0:n_in + r0 + nr], outs[n_out + ob0:n_out + ob0 + nb], outs[n_out + on0:n_out + on0 + nn],
                   *scr[n_scr + 3 * idx:n_scr + 3 * idx + 3])

        host = lambda: body(*ins[:n_in], *outs[:n_out], *scr[:n_scr])
        if not rides:
            host()
        elif not grid:
            run(0)
            host()
            run(1)
        else:
            ids = [pl.program_id(ax) for ax in range(len(grid))]
            first = functools.reduce(jnp.logical_and, [i == 0 for i in ids])
            last = functools.reduce(jnp.logical_and, [i == g - 1 for i, g in zip(ids, grid)])
            pl.when(first)(lambda: run(0))
            host()
            pl.when(last)(lambda: run(1))

    if rides:
        params = pltpu.CompilerParams(dimension_semantics=("arbitrary",) * len(grid), vmem_limit_bytes=VMEM_LIMIT)
    else:
        params = _cp(*semantics)
    outs = pl.pallas_call(
        full_body, name=name, grid=grid,
        in_specs=in_specs + [ANY] * n_ein, out_specs=out_specs + [ANY] * n_eout,
        out_shape=out_shape + extra_out, input_output_aliases=aliases,
        scratch_shapes=scratch_shapes + sem_shapes, compiler_params=params,
    )(*operands, *extra_in)
    main = outs[0] if single else list(outs[:n_out])
    rode = [(list(outs[n_out + ob0:n_out + ob0 + nb]), list(outs[n_out + on0:n_out + on0 + nn]))
            for (_, _, ob0, nb, on0, nn) in where]
    return main, rode


def _run_rides(rides, name):
    return _hosted_call(lambda: None, [], name=name, grid=(), in_specs=[], out_specs=[], out_shape=[], rides=rides)[1]


def _ride_gather(slots, part=0, n_parts=1):
    n = len(slots)
    halves = [a.shape[1] // 2 for a in slots]
    sizes = [hr // n_parts for hr in halves]
    for a, hr, size in zip(slots, halves, sizes):
        assert a.shape[1] == 2 * hr and hr == size * n_parts and size % ROWS == 0, a.shape

    def remote(bufs, send_sems, recv_sems, i, k, slot, core, to):
        rows = bufs[i].at[slot, pl.ds(pl.multiple_of(core * halves[i] + part * sizes[i], ROWS), sizes[i])]
        return pltpu.make_async_remote_copy(
            src_ref=rows, dst_ref=rows, send_sem=send_sems.at[i * 6 + k], recv_sem=recv_sems.at[i * 6 + k],
            device_id=to, device_id_type=MESH)

    def start(reads, bufs, new, send_sems, recv_sems, local_sems):
        x, y, c, chips = _place()
        for i in range(n):
            for k, (px, py) in enumerate(chips):
                remote(bufs, send_sems, recv_sems, i, k, 2 * x + y, c, (px, py, c)).start()

    def finish(reads, bufs, new, send_sems, recv_sems, local_sems):
        x, y, c, chips = _place()
        cp = functools.partial(remote, bufs, send_sems, recv_sems)
        for i in range(n):
            for k, (px, py) in enumerate(chips):
                cp(i, k, 2 * px + py, c, (x, y, c)).wait_recv()
                cp(i, 3 + k, 2 * px + py, c, (x, y, 1 - c)).start()
        for i in range(n):
            for k, (px, py) in enumerate(chips):
                cp(i, 3 + k, 2 * px + py, 1 - c, (x, y, c)).wait_recv()
        for i in range(n):
            for k, (px, py) in enumerate(chips):
                cp(i, k, 2 * x + y, c, (px, py, c)).wait_send()
                cp(i, 3 + k, 2 * px + py, c, (x, y, 1 - c)).wait_send()

    return _Ride([], slots, [], (6 * n, 6 * n, 0), start, finish)


def _ride_swap(grads):
    n = len(grads)
    halves = [a.shape[1] // 2 for a in grads]

    def copies(reads, new, send_sems, recv_sems):
        x, y, c, _ = _place()
        out = []
        for i in range(n):
            rows = pl.ds(pl.multiple_of((1 - c) * halves[i], 2 * SUBLANE), halves[i])
            out.append(pltpu.make_async_remote_copy(
                src_ref=reads[i].at[:, rows, :], dst_ref=new[i], send_sem=send_sems.at[i], recv_sem=recv_sems.at[i],
                device_id=(x, y, 1 - c), device_id_type=MESH))
        return out

    def start(reads, bufs, new, send_sems, recv_sems, local_sems):
        for cp in copies(reads, new, send_sems, recv_sems):
            cp.start()

    def finish(reads, bufs, new, send_sems, recv_sems, local_sems):
        for cp in copies(reads, new, send_sems, recv_sems):
            cp.wait()

    shapes = [jax.ShapeDtypeStruct((N_CHIPS, hr, a.shape[2]), a.dtype) for a, hr in zip(grads, halves)]
    return _Ride(grads, [], shapes, (n, n, 0), start, finish)


def _ride_scatter(parts, part=0, n_parts=1, into=None):
    n = len(parts)
    sizes = [a.shape[1] // n_parts for a in parts]
    for a, size in zip(parts, sizes):
        assert a.shape[1] == size * n_parts and size % ROWS == 0, a.shape

    def piece(ref, i, slot):
        return ref.at[slot, pl.ds(part * sizes[i], sizes[i])]

    def own(reads, land, local_sems, i):
        me = 2 * lax.axis_index("x") + lax.axis_index("y")
        return pltpu.make_async_copy(piece(reads[i], i, me), piece(land[i], i, me), local_sems.at[i])

    def send(reads, land, send_sems, recv_sems, i, k):
        x, y, c, chips = _place()
        px, py = chips[k]
        return pltpu.make_async_remote_copy(
            src_ref=piece(reads[i], i, 2 * px + py), dst_ref=piece(land[i], i, 2 * x + y),
            send_sem=send_sems.at[3 * i + k], recv_sem=recv_sems.at[3 * i + k],
            device_id=(px, py, c), device_id_type=MESH)

    def start(reads, bufs, new, send_sems, recv_sems, local_sems):
        land = new if into is None else bufs
        for i in range(n):
            own(reads, land, local_sems, i).start()
            for k in range(3):
                send(reads, land, send_sems, recv_sems, i, k).start()

    def finish(reads, bufs, new, send_sems, recv_sems, local_sems):
        land = new if into is None else bufs
        x, y, c, chips = _place()
        for i in range(n):
            for k, (px, py) in enumerate(chips):
                slot = piece(land[i], i, 2 * px + py)
                pltpu.make_async_remote_copy(
                    src_ref=slot, dst_ref=slot, send_sem=send_sems.at[3 * i + k], recv_sem=recv_sems.at[3 * i + k],
                    device_id=(x, y, c), device_id_type=MESH).wait_recv()
        for i in range(n):
            for k in range(3):
                send(reads, land, send_sems, recv_sems, i, k).wait_send()
            own(reads, land, local_sems, i).wait()

    shapes = [jax.ShapeDtypeStruct(a.shape, a.dtype) for a in parts]
    if into is None:
        return _Ride(parts, [], shapes, (3 * n, 3 * n, n), start, finish)
    return _Ride(parts, list(into), [], (3 * n, 3 * n, n), start, finish)


def _ride_join(grads):
    n = len(grads)

    def copy(bufs, send_sems, recv_sems, i, core, to):
        hr = grads[i].shape[1] // 2
        rows = bufs[i].at[:, pl.ds(pl.multiple_of(core * hr, SUBLANE), hr), :]
        return pltpu.make_async_remote_copy(
            src_ref=rows, dst_ref=rows, send_sem=send_sems.at[i], recv_sem=recv_sems.at[i],
            device_id=to, device_id_type=MESH)

    def start(reads, bufs, new, send_sems, recv_sems, local_sems):
        x, y, c, _ = _place()
        for i in range(n):
            copy(bufs, send_sems, recv_sems, i, c, (x, y, 1 - c)).start()

    def finish(reads, bufs, new, send_sems, recv_sems, local_sems):
        x, y, c, _ = _place()
        for i in range(n):
            copy(bufs, send_sems, recv_sems, i, 1 - c, (x, y, c)).wait_recv()
        for i in range(n):
            copy(bufs, send_sems, recv_sems, i, c, (x, y, 1 - c)).wait_send()

    return _Ride([], grads, [], (n, n, 0), start, finish)


def _all_reduce_small(packed, name, rides=()):
    r, c = packed.shape
    chunk = _pick(r, 256, ROWS)

    def body(x_ref, out_ref, gath, send_sems, recv_sems, local_sem):
        x, y, cc, chips = _place()
        me, sibling = (x, y, cc), (x, y, 1 - cc)

        def slot(px, py, pc):
            return gath.at[4 * px + 2 * py + pc]

        def copy(k, block, to, src=None):
            return pltpu.make_async_remote_copy(
                src_ref=slot(*block) if src is None else src, dst_ref=slot(*block),
                send_sem=send_sems.at[k], recv_sem=recv_sems.at[k], device_id=to, device_id_type=MESH)

        mine = pltpu.make_async_copy(x_ref, slot(*me), local_sem)
        mine.start()
        first = [copy(0, me, sibling, src=x_ref)]
        first += [copy(1 + j, me, (*chip, cc), src=x_ref) for j, chip in enumerate(chips)]
        for cp in first:
            cp.start()
        passed = [copy(4 + j, (*chip, cc), sibling) for j, chip in enumerate(chips)]
        for j, chip in enumerate(chips):
            copy(1 + j, (*chip, cc), me).wait_recv()
            passed[j].start()
        copy(0, sibling, me).wait_recv()
        for j, chip in enumerate(chips):
            copy(4 + j, (*chip, 1 - cc), me).wait_recv()
        for cp in first + passed:
            cp.wait_send()
        mine.wait()

        def add(i, carry):
            rows = pl.ds(pl.multiple_of(i * chunk, SUBLANE), chunk)
            acc = gath[0, rows, :]
            for dev in range(1, N_DEV):
                acc = acc + gath[dev, rows, :]
            out_ref[rows, :] = acc
            return carry

        lax.fori_loop(0, r // chunk, add, 0)

    return _hosted_call(
        body, [packed], name=name, grid=(), in_specs=[VMEM_SPEC], out_specs=VMEM_SPEC,
        out_shape=jax.ShapeDtypeStruct((r, c), F32),
        scratch_shapes=[pltpu.VMEM((N_DEV, r, c), F32), pltpu.SemaphoreType.DMA((7,)),
                        pltpu.SemaphoreType.DMA((7,)), pltpu.SemaphoreType.DMA],
        rides=rides)


_PACK_ROWS = 256


def _pack(arrays):
    flat = jnp.concatenate([a.reshape(-1).astype(F32) for a in arrays])
    unit = _PACK_ROWS * LANE
    total = -(-flat.shape[0] // unit) * unit
    return jnp.pad(flat, (0, total - flat.shape[0])).reshape(-1, LANE)


def _unpack(packed, shapes, lead=()):
    flat = packed.reshape(lead + (-1,))
    out, at = [], 0
    for s in shapes:
        size = math.prod(s)
        out.append(flat[..., at:at + size].reshape(lead + tuple(s)))
        at += size
    return out


def kernel(x, pre_mix_g, post_mix_g, pre_ffn_g, post_ffn_g, a_w_in, a_v_norm_g, a_w_spatial, a_b_spatial, a_w_out, kv_norm_g, w_k, w_v, b_w_q, b_w_o, ffn_w_up, ffn_conv_w, ffn_conv_b, ffn_w_down, loss_target, m_pre_mix_g, m_post_mix_g, m_pre_ffn_g, m_post_ffn_g, m_a_w_in, m_a_v_norm_g, m_a_w_spatial, m_a_b_spatial, m_a_w_out, m_kv_norm_g, m_w_k, m_w_v, m_b_w_q, m_b_w_o, m_ffn_w_up, m_ffn_conv_w, m_ffn_conv_b, m_ffn_w_down, v_pre_mix_g, v_post_mix_g, v_pre_ffn_g, v_post_ffn_g, v_a_w_in, v_a_v_norm_g, v_a_w_spatial, v_a_b_spatial, v_a_w_out, v_kv_norm_g, v_w_k, v_w_v, v_b_w_q, v_b_w_o, v_ffn_w_up, v_ffn_conv_w, v_ffn_conv_b, v_ffn_w_down):
    xi, yi, ci = lax.axis_index("x"), lax.axis_index("y"), lax.axis_index("c")
    chip = 2 * xi + yi
    c_idx = jnp.reshape(ci, (1,)).astype(jnp.int32)
    _, s, d = x.shape
    n_layers = pre_mix_g.shape[0]
    assert n_layers == 2 and a_w_in.shape[0] == 1 and b_w_q.shape[0] == 1
    d_a = a_w_out.shape[1] * N_CHIPS
    n_g = a_w_spatial.shape[1]
    ns = ffn_w_up.shape[2]
    assert a_w_spatial.shape[2] == TILE and d_a == n_g * TILE and s % TILE == 0
    h0 = x[0]
    target = loss_target[0]

    big = {
        "win": (a_w_in, m_a_w_in, v_a_w_in),
        "wout": (a_w_out, m_a_w_out, v_a_w_out),
        "wk": (w_k[None], m_w_k[None], v_w_k[None]),
        "wv": (w_v[None], m_w_v[None], v_w_v[None]),
        "wq": (b_w_q, m_b_w_q, v_b_w_q),
        "wo": (b_w_o, m_b_w_o, v_b_w_o),
        "wup": (ffn_w_up, m_ffn_w_up, v_ffn_w_up),
        "wdn": (ffn_w_down, m_ffn_w_down, v_ffn_w_down),
    }
    units = [(nm, layer) for nm in big for layer in range(big[nm][0].shape[0])]
    chip_idx = jnp.reshape(chip, (1,)).astype(jnp.int32)
    shards = [_cast_bf16(big[nm][0], layer, chip_idx, f"cast_{nm}{layer}") for nm, layer in units]
    small_sharded = _pack([a_v_norm_g, ffn_conv_w])
    small_sharded = lax.dynamic_update_index_in_dim(
        jnp.zeros((N_CHIPS,) + small_sharded.shape, F32), small_sharded, chip, 0)
    own = dict(zip(units, shards))
    full = {}

    def gather_ride(keys):
        return _ride_gather([own[key] for key in keys])

    def gathered(keys, rode):
        full.update(zip(keys, rode[0]))

    first_keys = [("win", 0), ("wout", 0)]
    (first_bufs, _), = _run_rides([_ride_gather([own[key] for key in first_keys] + [small_sharded])], "gather_first")
    full.update(zip(first_keys, first_bufs[:-1]))
    vg_parts, cw_parts = _unpack(first_bufs[-1], [a_v_norm_g.shape, ffn_conv_w.shape], lead=(N_CHIPS,))
    v_g = jnp.transpose(vg_parts, (1, 0, 2)).reshape(1, d_a)

    def rows(nm, layer=0):
        w = full[(nm, layer)]
        return w.reshape(w.shape[0] * w.shape[1], w.shape[2])

    gains = lambda g, layer: g[layer:layer + 1]
    bias = jnp.repeat(a_b_spatial[0].T, TILE, axis=1)
    w_s = a_w_spatial[0]
    kv_g = kv_norm_g[None]
    conv_w = [cw_parts[:, layer] for layer in range(n_layers)]
    conv_b = [ffn_conv_b[layer].reshape(N_CHIPS, 1, ns) for layer in range(n_layers)]

    def ffn_fwd(hn, layer, up_keys=(), down_keys=()):
        a = _mm(hn, full[("wup", layer)], "nn", f"ffn_up{layer}", out_split=N_CHIPS,
                rides=[gather_ride(up_keys)] if up_keys else ())
        if up_keys:
            a, (rode,) = a
            gathered(up_keys, rode)
        hm = _ffn_act_fwd(a, conv_w[layer], conv_b[layer], f"ffn_act{layer}")
        f = _mm(hm, rows("wdn", layer), "nn", f"ffn_down{layer}", rides=[gather_ride(down_keys)] if down_keys else ())
        if down_keys:
            f, (rode,) = f
            gathered(down_keys, rode)
        return a, hm, f[0]

    up0 = own[("wup", 0)]
    piece = lambda p: [_ride_gather([up0], part=p, n_parts=4)]
    hn0 = _rms_fwd(h0, gains(pre_mix_g, 0), "norm_in")
    uv, (((up0,), _),) = _mm(hn0, full[("win", 0)], "nn", "gmlp_in", out_split=N_CHIPS, rides=piece(0))
    gm, (((up0,), _),) = _gmlp_fwd(uv, v_g, w_s, bias, "gmlp_gate", rides=piece(1))
    mix0, (((up0,), _),) = _mm(gm, rows("wout"), "nn", "gmlp_out", rides=piece(2))
    mix0 = mix0[0]
    (h1, hn1), (((up0,), _),) = _resid_rms(
        h0, mix0, gains(post_mix_g, 0), [gains(pre_ffn_g, 0)], "resid_mix0", rides=piece(3))
    full[("wup", 0)] = up0
    a0, hm0, f0 = ffn_fwd(hn1, 0, up_keys=[("wdn", 0), ("wq", 0), ("wk", 0)], down_keys=[("wv", 0), ("wo", 0)])
    h2, hn2, kvn = _resid_rms(h1, f0, gains(post_ffn_g, 0), [gains(pre_mix_g, 1), kv_g], "resid_ffn0")
    q = _mm(hn2, rows("wq"), "nn", "proj_q", out_dtype=BF16)[0]
    k = _mm(kvn, rows("wk"), "nn", "proj_k", out_dtype=BF16)[0]
    v = _mm(kvn, rows("wv"), "nn", "proj_v", out_dtype=BF16)[0]
    last_keys = [("wup", 1), ("wdn", 1)]
    (att, lsum), (rode,) = _attn_fwd(q, k, v, "attn_fwd", rides=[gather_ride(last_keys)])
    gathered(last_keys, rode)
    mix1 = _mm(att, rows("wo"), "nn", "proj_o")[0]
    h3, hn3 = _resid_rms(h2, mix1, gains(post_mix_g, 1), [gains(pre_ffn_g, 1)], "resid_mix1")
    a1, hm1, f1 = ffn_fwd(hn3, 1)
    dh4, loss_tile = _loss_head(h3, f1, gains(post_ffn_g, 1), target, "loss_head")
    loss = lax.psum(loss_tile[0, 0], ("x", "y", "c"))

    dw = {}
    dg = {}

    pair = {}
    half_done = {nm: None for nm in big}

    def swap_ride(keys):
        return _ride_swap([dw[key] for key in keys])

    def swapped(keys, rode):
        for (nm, layer), got in zip(keys, rode[1]):
            pair[(nm, layer)] = _pair_add(dw[(nm, layer)], got, c_idx, f"pair_add_{nm}{layer}")

    def scatter_ride(keys):
        return _ride_scatter([pair[key] for key in keys])

    def scattered(keys, rode):
        for (nm, layer), got in zip(keys, rode[1]):
            half_done[nm] = _chip_sum(got, half_done[nm], big[nm][0].shape, layer, c_idx, f"chip_sum_{nm}{layer}")

    def ffn_bwd(dh_out, h_in, hn, a, hm, f, layer, act_rides=()):
        df, dg[("post_ffn", layer)] = _rms_bwd_out(dh_out, f, gains(post_ffn_g, layer), f"d_norm_ffn_out{layer}")
        dwd = _mm(hm, df, "tn", f"d_w_down{layer}", out_dtype=BF16)[0]
        down, up = [("wdn", layer)], [("wup", layer)]
        dw[down[0]] = dwd.reshape(N_CHIPS, dwd.shape[0] // N_CHIPS, d)
        dhm, (rode,) = _mm(df, rows("wdn", layer), "nt", f"d_ffn_mid{layer}", out_split=2, rides=[swap_ride(down)])
        swapped(down, rode)
        (da, dg[("conv_w", layer)], dg[("conv_b", layer)]), act_rode = _ffn_act_bwd(
            a, dhm, conv_w[layer], conv_b[layer], f"d_ffn_act{layer}", rides=act_rides)
        dw[up[0]], (rode,) = _mm(hn, da, "tn", f"d_w_up{layer}", out_dtype=BF16, out_split=N_CHIPS,
                                 rides=[scatter_ride(down)])
        scattered(down, rode)
        dhn, (rode,) = _mm(da, full[("wup", layer)], "nt", f"d_ffn_in{layer}", rides=[swap_ride(up)])
        swapped(up, rode)
        return dhn[0], act_rode

    dhn3, _ = ffn_bwd(dh4, h3, hn3, a1, hm1, f1, 1)
    dh3, (dg[("pre_ffn", 1)],) = _rms_bwd_in(dh4, h3, [([dhn3], gains(pre_ffn_g, 1))], "d_norm_ffn_in1")
    dmix1, dg[("post_mix", 1)] = _rms_bwd_out(dh3, mix1, gains(post_mix_g, 1), "d_norm_mix_out1")
    dwo = _mm(att, dmix1, "tn", "d_w_o", out_dtype=BF16)[0]
    dw[("wo", 0)] = dwo.reshape(N_CHIPS, dwo.shape[0] // N_CHIPS, d)
    datt = _mm(dmix1, rows("wo"), "nt", "d_attn_out", out_dtype=BF16)[0]
    ffn1_keys = [("wup", 1)]
    (dq, dk, dv), (rode,) = _attn_bwd(q, k, v, datt, lsum, "attn_bwd", rides=[scatter_ride(ffn1_keys)])
    scattered(ffn1_keys, rode)
    for nm, act, dact in (("wq", hn2, dq), ("wk", kvn, dk), ("wv", kvn, dv)):
        g = _mm(act, dact, "tn", f"d_{nm}", out_dtype=BF16)[0]
        dw[(nm, 0)] = g.reshape(N_CHIPS, g.shape[0] // N_CHIPS, g.shape[1])
    dhn2 = _mm(dq, rows("wq"), "nt", "d_q_in")[0]
    dkvn_k = _mm(dk, rows("wk"), "nt", "d_k_in")[0]
    attn_keys = [("wo", 0), ("wq", 0), ("wk", 0), ("wv", 0)]
    dkvn_v, (rode,) = _mm(dv, rows("wv"), "nt", "d_v_in", rides=[swap_ride(attn_keys)])
    swapped(attn_keys, rode)
    dh2, (dg[("pre_mix", 1)], dg["kv"]) = _rms_bwd_in(
        dh3, h2, [([dhn2], gains(pre_mix_g, 1)), ([dkvn_k, dkvn_v[0]], kv_g)], "d_norm_mix_in1")
    dhn1, (rode,) = ffn_bwd(dh2, h1, hn1, a0, hm0, f0, 0, act_rides=[scatter_ride(attn_keys)])
    scattered(attn_keys, rode)
    dh1, (dg[("pre_ffn", 0)],) = _rms_bwd_in(dh2, h1, [([dhn1], gains(pre_ffn_g, 0))], "d_norm_ffn_in0")
    dmix0, dg[("post_mix", 0)] = _rms_bwd_out(dh1, mix0, gains(post_mix_g, 0), "d_norm_mix_out0")
    early = ["wq", "wk", "wv", "wo", "wdn"]
    dwout, (((joined_early, _)),) = _mm(
        gm, dmix0, "tn", "d_w_out", out_dtype=BF16, rides=[_ride_join([half_done[nm] for nm in early])])
    grads_big = dict(zip(early, joined_early))
    w_out_key, w_in_key = [("wout", 0)], [("win", 0)]
    dw[w_out_key[0]] = dwout[0].reshape(N_CHIPS, dwout.shape[1] // N_CHIPS, d)
    dgm, (rode,) = _mm(dmix0, rows("wout"), "nt", "d_gmlp_gate", rides=[swap_ride(w_out_key)])
    swapped(w_out_key, rode)
    up0_pair = [pair[("wup", 0)]]
    (duv, d_ws, d_bs, d_vg), ((_, up0_landed),) = _gmlp_bwd(
        uv, dgm[0], v_g, w_s, bias, "d_gmlp", rides=[_ride_scatter(up0_pair, 0, 2)])
    dw[w_in_key[0]], (rode,) = _mm(
        hn0, duv, "tn", "d_w_in", out_dtype=BF16, out_split=N_CHIPS, rides=[scatter_ride(w_out_key)])
    scattered(w_out_key, rode)
    dhn0, (rode, (up0_landed, _)) = _mm(
        duv, full[("win", 0)], "nt", "d_gmlp_in",
        rides=[swap_ride(w_in_key), _ride_scatter(up0_pair, 1, 2, into=up0_landed)])
    swapped(w_in_key, rode)
    scattered([("wup", 0)], (None, up0_landed))
    dx, (dg[("pre_mix", 0)],), (rode,) = _rms_bwd_in(
        dh1, h0, [([dhn0[0]], gains(pre_mix_g, 0))], "d_norm_in", rides=[scatter_ride(w_in_key)])
    scattered(w_in_key, rode)

    stack = lambda key: jnp.concatenate([dg[(key, layer)] for layer in range(n_layers)], axis=0)
    small_parts = [
        stack("pre_mix"), stack("post_mix"), stack("pre_ffn"), stack("post_ffn"),
        d_vg, d_ws, d_bs[::SUBLANE], dg["kv"],
        jnp.stack([dg[("conv_w", layer)] for layer in range(n_layers)]),
        jnp.stack([dg[("conv_b", layer)] for layer in range(n_layers)]),
    ]
    late = [nm for nm in big if nm not in early]
    summed, ((joined_late, _),) = _all_reduce_small(
        _pack(small_parts), "small_grads_sum", rides=[_ride_join([half_done[nm] for nm in late])])
    grads_big.update(zip(late, joined_late))
    (g_pre_mix, g_post_mix, g_pre_ffn, g_post_ffn, g_vg, g_ws, g_bs, g_kv, g_cw, g_cb) = _unpack(
        summed, [p.shape for p in small_parts])
    g_vg = lax.dynamic_index_in_dim(g_vg.reshape(N_CHIPS, 1, d_a // N_CHIPS), chip, 0, keepdims=False)
    g_cw = lax.dynamic_index_in_dim(g_cw, chip, 1, keepdims=False)
    g_cb = g_cb.reshape(n_layers, N_CHIPS * ns)
    small = [
        (pre_mix_g, g_pre_mix, m_pre_mix_g, v_pre_mix_g),
        (post_mix_g, g_post_mix, m_post_mix_g, v_post_mix_g),
        (pre_ffn_g, g_pre_ffn, m_pre_ffn_g, v_pre_ffn_g),
        (post_ffn_g, g_post_ffn, m_post_ffn_g, v_post_ffn_g),
        (a_v_norm_g, g_vg, m_a_v_norm_g, v_a_v_norm_g),
        (a_w_spatial, g_ws[None], m_a_w_spatial, v_a_w_spatial),
        (a_b_spatial, g_bs[None], m_a_b_spatial, v_a_b_spatial),
        (kv_norm_g, g_kv.reshape(d), m_kv_norm_g, v_kv_norm_g),
        (ffn_conv_w, g_cw, m_ffn_conv_w, v_ffn_conv_w),
        (ffn_conv_b, g_cb, m_ffn_conv_b, v_ffn_conv_b),
    ]
    small = [(w, g.reshape(w.shape), m, v) for w, g, m, v in small]
    packed = [_pack([t[i] for t in small])[None] for i in range(4)]
    small_new = [_unpack(p[0], [t[0].shape for t in small]) for p in _adamw(*packed, "adamw_small")]

    new_big = {nm: _adamw(big[nm][0], grads_big[nm], big[nm][1], big[nm][2], f"adamw_{nm}", pass_g=True)
               for nm in big}

    def big_out(nm, which):
        ref_shape = {"wk": w_k.shape, "wv": w_v.shape}.get(nm, big[nm][0].shape)
        return new_big[nm][which].reshape(ref_shape)

    order = ["pre_mix", "post_mix", "pre_ffn", "post_ffn", "win", "vg", "ws", "bs", "wout", "kv", "wk", "wv", "wq",
             "wo", "wup", "cw", "cb", "wdn"]
    small_at = {"pre_mix": 0, "post_mix": 1, "pre_ffn": 2, "post_ffn": 3, "vg": 4, "ws": 5, "bs": 6, "kv": 7,
                "cw": 8, "cb": 9}
    outs = [loss, dx[None]]
    for which in range(4):
        for nm in order:
            if nm in small_at:
                outs.append(small[small_at[nm]][1] if which == 0 else small_new[which - 1][small_at[nm]])
            else:
                outs.append(big_out(nm, which))
    return tuple(outs)
```

```python
import functools
import math

import jax
import jax.numpy as jnp
from jax import lax
from jax.experimental import pallas as pl
from jax.experimental.pallas import tpu as pltpu

F32 = jnp.float32
BF16 = jnp.bfloat16
EPS = 1e-6
ADAM_LR = 0.001
ADAM_B1 = 0.9
ADAM_B2 = 0.999
ADAM_EPS = 1e-08
ADAM_WD = 0.01
ADAM_STEP = 10

LANE = 128
SUBLANE = 8
ROWS = 16
TILE = 128
N_CHIPS = 4
N_DEV = 8
VMEM_LIMIT = 56 * 1024 * 1024
MM_VMEM = 40 * 1024 * 1024
MESH = pl.DeviceIdType.MESH
ANY = pl.BlockSpec(memory_space=pl.ANY)
VMEM_SPEC = pl.BlockSpec(memory_space=pltpu.VMEM)


def _cp(*sem):
    return pltpu.CompilerParams(dimension_semantics=sem, vmem_limit_bytes=VMEM_LIMIT)


def _pick(dim, pref, align=LANE):
    if dim <= pref:
        return dim
    best = None
    for d in range(align, pref + 1, align):
        if dim % d == 0:
            best = d
    assert best is not None, (dim, pref)
    return best


_DIMS = {
    "nn": (((1,), (0,)), ((), ())),
    "nt": (((1,), (1,)), ((), ())),
    "tn": (((0,), (0,)), ((), ())),
}


def _as3(a):
    return a if a.ndim == 3 else a[None]


def _spec3(br, bc, cols_j, rc):
    per = cols_j // bc

    def imap(m, n, k):
        r, c = rc(m, n, k)
        return (c // per, r, c % per)

    return pl.BlockSpec((None, br, bc), imap)


def _mm(a, b, mode, name, out_dtype=F32, out_split=1, rides=()):
    a, b = _as3(a), _as3(b)
    ja, ra, caj = a.shape
    jb, rb, cbj = b.shape
    if mode == "nn":
        m, k, n = ra, ja * caj, jb * cbj
        assert rb == k
        m_ext, k_ext, n_ext = [ra], [caj, rb], [cbj]
    elif mode == "nt":
        m, k, n = ra, ja * caj, rb
        assert jb * cbj == k
        m_ext, k_ext, n_ext = [ra], [caj, cbj], [rb]
    else:
        m, k, n = ja * caj, ra, jb * cbj
        assert rb == k
        m_ext, k_ext, n_ext = [caj], [ra], [cbj]
    assert n % out_split == 0
    n_ext.append(n // out_split)
    bm = _pick(math.gcd(*m_ext), 1536)
    bn = _pick(math.gcd(*n_ext), 1536)
    k_unit = math.gcd(*k_ext)
    o_bytes = jnp.dtype(out_dtype).itemsize

    def vmem_need(bk):
        tiles = bm * bk * a.dtype.itemsize + bk * bn * b.dtype.itemsize + bm * bn * o_bytes
        return 2 * tiles + (bm * bn * 4 if bk < k else 0)

    bk = max(d for d in range(LANE, k_unit + 1, LANE) if k_unit % d == 0 and (d == LANE or vmem_need(d) <= MM_VMEM))
    nk = k // bk
    if mode == "nn":
        a_spec = _spec3(bm, bk, caj, lambda mi, ni, ki: (mi, ki))
        b_spec = _spec3(bk, bn, cbj, lambda mi, ni, ki: (ki, ni))
    elif mode == "nt":
        a_spec = _spec3(bm, bk, caj, lambda mi, ni, ki: (mi, ki))
        b_spec = _spec3(bn, bk, cbj, lambda mi, ni, ki: (ni, ki))
    else:
        a_spec = _spec3(bk, bm, caj, lambda mi, ni, ki: (ki, mi))
        b_spec = _spec3(bk, bn, cbj, lambda mi, ni, ki: (ki, ni))
    o_spec = _spec3(bm, bn, n // out_split, lambda mi, ni, ki: (mi, ni))
    dims = _DIMS[mode]

    def body(a_ref, b_ref, o_ref, *acc):
        def part():
            return lax.dot_general(a_ref[...].astype(BF16), b_ref[...].astype(BF16), dims, preferred_element_type=F32)

        if nk == 1:
            o_ref[...] = part().astype(o_ref.dtype)
            return
        acc_ref, = acc
        ki = pl.program_id(2)

        @pl.when(ki == 0)
        def _():
            acc_ref[...] = part()

        @pl.when(jnp.logical_and(ki > 0, ki < nk - 1))
        def _():
            acc_ref[...] += part()

        @pl.when(ki == nk - 1)
        def _():
            o_ref[...] = (acc_ref[...] + part()).astype(o_ref.dtype)

    out, rode = _hosted_call(
        body, [a, b], name=name, grid=(m // bm, n // bn, nk), in_specs=[a_spec, b_spec], out_specs=o_spec,
        out_shape=jax.ShapeDtypeStruct((out_split, m, n // out_split), out_dtype),
        scratch_shapes=[pltpu.VMEM((bm, bn), F32)] if nk > 1 else [],
        semantics=("parallel", "parallel", "arbitrary"), rides=rides)
    return (out, rode) if rides else out


def _rms(x, g):
    r = lax.rsqrt(jnp.mean(x * x, axis=-1, keepdims=True) + EPS)
    return x * r * g


def _rms_bwd(x, g, dy):
    r = lax.rsqrt(jnp.mean(x * x, axis=-1, keepdims=True) + EPS)
    xh = x * r
    gy = dy * g
    dx = r * (gy - xh * jnp.mean(gy * xh, axis=-1, keepdims=True))
    return dx, jnp.sum(dy * xh, axis=0, keepdims=True)


def _row_block(s):
    return _pick(s, 256, ROWS)


def _rms_fwd(h, g, name):
    s, d = h.shape
    br = _row_block(s)

    def body(h_ref, g_ref, o_ref):
        o_ref[...] = _rms(h_ref[...], g_ref[...]).astype(BF16)

    row = pl.BlockSpec((br, d), lambda i: (i, 0))
    vec = pl.BlockSpec((1, d), lambda i: (0, 0))
    return pl.pallas_call(
        body, name=name, grid=(s // br,), in_specs=[row, vec], out_specs=row,
        out_shape=jax.ShapeDtypeStruct((s, d), BF16), compiler_params=_cp("parallel"),
    )(h, g)


def _resid_rms(h_in, f, g_post, g_next, name, rides=()):
    s, d = h_in.shape
    br = _row_block(s)
    n_next = len(g_next)

    def body(h_ref, f_ref, gp_ref, *refs):
        gn_refs, ho_ref, hn_refs = refs[:n_next], refs[n_next], refs[n_next + 1:]
        h = h_ref[...] + _rms(f_ref[...], gp_ref[...])
        ho_ref[...] = h
        for gn_ref, hn_ref in zip(gn_refs, hn_refs):
            hn_ref[...] = _rms(h, gn_ref[...]).astype(BF16)

    row = pl.BlockSpec((br, d), lambda i: (i, 0))
    vec = pl.BlockSpec((1, d), lambda i: (0, 0))
    outs, rode = _hosted_call(
        body, [h_in, f, g_post, *g_next], name=name, grid=(s // br,),
        in_specs=[row, row, vec] + [vec] * n_next,
        out_specs=[row] * (1 + n_next),
        out_shape=[jax.ShapeDtypeStruct((s, d), F32)] + [jax.ShapeDtypeStruct((s, d), BF16)] * n_next,
        semantics=("parallel",), rides=rides)
    return (outs, rode) if rides else outs


def _loss_head(h_in, f, g_post, target, name):
    s, d = h_in.shape
    br = _row_block(s)

    def body(h_ref, f_ref, gp_ref, t_ref, dh_ref, loss_ref):
        @pl.when(pl.program_id(0) == 0)
        def _():
            loss_ref[...] = jnp.zeros_like(loss_ref)

        diff = h_ref[...] + _rms(f_ref[...], gp_ref[...]) - t_ref[...]
        dh_ref[...] = diff * (1.0 / d)
        loss_ref[...] += 0.5 * jnp.sum(jnp.mean(diff * diff, axis=-1, keepdims=True))

    row = pl.BlockSpec((br, d), lambda i: (i, 0))
    vec = pl.BlockSpec((1, d), lambda i: (0, 0))
    return pl.pallas_call(
        body, name=name, grid=(s // br,),
        in_specs=[row, row, vec, row],
        out_specs=[row, pl.BlockSpec((SUBLANE, LANE), lambda i: (0, 0))],
        out_shape=[jax.ShapeDtypeStruct((s, d), F32), jax.ShapeDtypeStruct((SUBLANE, LANE), F32)],
        compiler_params=_cp("arbitrary"),
    )(h_in, f, g_post, target)


def _rms_bwd_out(dy, f, g, name):
    s, d = f.shape
    br = _row_block(s)

    def body(dy_ref, f_ref, g_ref, df_ref, dg_ref):
        @pl.when(pl.program_id(0) == 0)
        def _():
            dg_ref[...] = jnp.zeros_like(dg_ref)

        dx, dg = _rms_bwd(f_ref[...], g_ref[...], dy_ref[...])
        df_ref[...] = dx.astype(BF16)
        dg_ref[...] += dg

    row = pl.BlockSpec((br, d), lambda i: (i, 0))
    vec = pl.BlockSpec((1, d), lambda i: (0, 0))
    return pl.pallas_call(
        body, name=name, grid=(s // br,), in_specs=[row, row, vec], out_specs=[row, vec],
        out_shape=[jax.ShapeDtypeStruct((s, d), BF16), jax.ShapeDtypeStruct((1, d), F32)],
        compiler_params=_cp("arbitrary"),
    )(dy, f, g)


def _rms_bwd_in(dh_out, h_in, branches, name, rides=()):
    s, d = h_in.shape
    br = _row_block(s)
    counts = [len(ds) for ds, _ in branches]
    n_d = sum(counts)
    n_b = len(branches)

    def body(dho_ref, h_ref, *refs):
        d_refs, g_refs = refs[:n_d], refs[n_d:n_d + n_b]
        dh_ref, dg_refs = refs[n_d + n_b], refs[n_d + n_b + 1:]

        @pl.when(pl.program_id(0) == 0)
        def _():
            for r in dg_refs:
                r[...] = jnp.zeros_like(r)

        h = h_ref[...]
        acc = dho_ref[...]
        at = 0
        for bi, cnt in enumerate(counts):
            dn = d_refs[at][...]
            for r in d_refs[at + 1:at + cnt]:
                dn = dn + r[...]
            at += cnt
            dx, dg = _rms_bwd(h, g_refs[bi][...], dn)
            acc = acc + dx
            dg_refs[bi][...] += dg
        dh_ref[...] = acc

    row = pl.BlockSpec((br, d), lambda i: (i, 0))
    vec = pl.BlockSpec((1, d), lambda i: (0, 0))
    flat_d = [x for ds, _ in branches for x in ds]
    outs, rode = _hosted_call(
        body, [dh_out, h_in, *flat_d, *[g for _, g in branches]], name=name, grid=(s // br,),
        in_specs=[row, row] + [row] * n_d + [vec] * n_b,
        out_specs=[row] + [vec] * n_b,
        out_shape=[jax.ShapeDtypeStruct((s, d), F32)] + [jax.ShapeDtypeStruct((1, d), F32)] * n_b,
        semantics=("arbitrary",), rides=rides)
    return (outs[0], list(outs[1:]), rode) if rides else (outs[0], list(outs[1:]))


def _split3(x):
    x0 = x.astype(BF16)
    r1 = x - x0.astype(F32)
    x1 = r1.astype(BF16)
    x2 = (r1 - x1.astype(F32)).astype(BF16)
    return x0, x1, x2


def _tri(n, kind):
    r = lax.broadcasted_iota(jnp.int32, (n, n), 0)
    c = lax.broadcasted_iota(jnp.int32, (n, n), 1)
    m = {"lt": r < c, "le": r <= c, "gt": r > c}[kind]
    return jnp.where(m, 1.0, 0.0).astype(BF16)


_GELU_C = math.sqrt(2.0 / math.pi)
_GELU_A = 0.044715


def _gelu(x):
    return 0.5 * x * (1.0 + jnp.tanh(_GELU_C * (x + _GELU_A * (x * x * x))))


def _gelu_grad(x):
    t = jnp.tanh(_GELU_C * (x + _GELU_A * (x * x * x)))
    return 0.5 * (1.0 + t) + 0.5 * x * (1.0 - t * t) * (_GELU_C * (1.0 + 3.0 * _GELU_A * (x * x)))


def _causal_w(w):
    r = lax.broadcasted_iota(jnp.int32, (TILE, TILE), 0)
    c = lax.broadcasted_iota(jnp.int32, (TILE, TILE), 1)
    return jnp.where(c <= r, w, 0.0)


def _uv_tiles(uv_ref, g, d_a, dq):
    cu, cv = g * TILE, d_a + g * TILE
    u = uv_ref[cu // dq, :, pl.ds(cu % dq, TILE)]
    v = uv_ref[cv // dq, :, pl.ds(cv % dq, TILE)]
    return u, v


def _gmlp_fwd(uv, v_g, w_s, bias, name, rides=()):
    _, s, dq = uv.shape
    d_a = 2 * dq
    n_g = d_a // TILE

    def body(uv_ref, vg_ref, ws_ref, b_ref, o_ref):
        for g in range(n_g):
            up, vp = _uv_tiles(uv_ref, g, d_a, dq)
            cols = pl.ds(g * TILE, TILE)
            vn = _rms(_gelu(vp), vg_ref[:, cols])
            mixed = jnp.dot(_causal_w(ws_ref[g]).astype(BF16), vn.astype(BF16), preferred_element_type=F32) + b_ref[:, cols]
            o_ref[:, cols] = (_gelu(up) * mixed).astype(BF16)

    return _hosted_call(
        body, [uv, v_g, w_s, bias], name=name, grid=(s // TILE,),
        in_specs=[
            pl.BlockSpec((4, TILE, dq), lambda i: (0, i, 0)),
            pl.BlockSpec((1, d_a), lambda i: (0, 0)),
            pl.BlockSpec((n_g, TILE, TILE), lambda i: (0, 0, 0)),
            pl.BlockSpec((TILE, d_a), lambda i: (0, 0)),
        ],
        out_specs=pl.BlockSpec((TILE, d_a), lambda i: (i, 0)),
        out_shape=jax.ShapeDtypeStruct((s, d_a), BF16),
        semantics=("parallel",), rides=rides)


def _gmlp_bwd(uv, dgm, v_g, w_s, bias, name, rides=()):
    _, s, dq = uv.shape
    d_a = 2 * dq
    n_g = d_a // TILE
    n_c = s // TILE

    def body(uv_ref, d_ref, vg_ref, ws_ref, b_ref, duv_ref, dws_ref, dbs_ref, dvg_ref, dbias_acc):
        i = pl.program_id(0)

        @pl.when(i == 0)
        def _():
            dws_ref[...] = jnp.zeros_like(dws_ref)
            dvg_ref[...] = jnp.zeros_like(dvg_ref)
            dbias_acc[...] = jnp.zeros_like(dbias_acc)

        for g in range(n_g):
            up, vp = _uv_tiles(uv_ref, g, d_a, dq)
            cols = pl.ds(g * TILE, TILE)
            vg = vg_ref[:, cols]
            u = _gelu(up)
            v = _gelu(vp)
            r = lax.rsqrt(jnp.mean(v * v, axis=-1, keepdims=True) + EPS)
            vh = v * r
            vn = (vh * vg).astype(BF16)
            wc = _causal_w(ws_ref[g]).astype(BF16)
            mixed = jnp.dot(wc, vn, preferred_element_type=F32) + b_ref[:, cols]
            d_out = d_ref[:, cols]
            du = d_out * mixed
            dmixed = d_out * u
            dmb = dmixed.astype(BF16)
            dvn = lax.dot_general(wc, dmb, _DIMS["tn"], preferred_element_type=F32)
            dws_ref[g] += lax.dot_general(dmb, vn, _DIMS["nt"], preferred_element_type=F32)
            dbias_acc[:, cols] += dmixed
            dvg_ref[:, cols] += jnp.sum(dvn * vh, axis=0, keepdims=True)
            gv = dvn * vg
            dv = r * (gv - vh * jnp.mean(gv * vh, axis=-1, keepdims=True))
            cu, cv = g * TILE, d_a + g * TILE
            duv_ref[cu // dq, :, pl.ds(cu % dq, TILE)] = (du * _gelu_grad(up)).astype(BF16)
            duv_ref[cv // dq, :, pl.ds(cv % dq, TILE)] = (dv * _gelu_grad(vp)).astype(BF16)

        @pl.when(i == n_c - 1)
        def _():
            ones = jnp.ones((SUBLANE, TILE), BF16)
            for g in range(n_g):
                dws_ref[g] = _causal_w(dws_ref[g])
                cols = pl.ds(g * TILE, TILE)
                out = None
                for t in _split3(dbias_acc[:, cols]):
                    p = lax.dot_general(ones, t, _DIMS["nt"], preferred_element_type=F32)
                    out = p if out is None else out + p
                dbs_ref[pl.ds(g * SUBLANE, SUBLANE), :] = out

    return _hosted_call(
        body, [uv, dgm, v_g, w_s, bias], name=name, grid=(n_c,), semantics=("arbitrary",), rides=rides,
        in_specs=[
            pl.BlockSpec((4, TILE, dq), lambda i: (0, i, 0)),
            pl.BlockSpec((TILE, d_a), lambda i: (i, 0)),
            pl.BlockSpec((1, d_a), lambda i: (0, 0)),
            pl.BlockSpec((n_g, TILE, TILE), lambda i: (0, 0, 0)),
            pl.BlockSpec((TILE, d_a), lambda i: (0, 0)),
        ],
        out_specs=[
            pl.BlockSpec((4, TILE, dq), lambda i: (0, i, 0)),
            pl.BlockSpec((n_g, TILE, TILE), lambda i: (0, 0, 0)),
            pl.BlockSpec((n_g * SUBLANE, TILE), lambda i: (0, 0)),
            pl.BlockSpec((1, d_a), lambda i: (0, 0)),
        ],
        out_shape=[
            jax.ShapeDtypeStruct((4, s, dq), BF16),
            jax.ShapeDtypeStruct((n_g, TILE, TILE), F32),
            jax.ShapeDtypeStruct((n_g * SUBLANE, TILE), F32),
            jax.ShapeDtypeStruct((1, d_a), F32),
        ],
        scratch_shapes=[pltpu.VMEM((TILE, d_a), F32)])


def _sigmoid(x):
    return 1.0 / (1.0 + jnp.exp(-x))


def _conv3(ext, w, b):
    return b + ((w[0:1] * pltpu.roll(ext, 2, 0) + w[1:2] * pltpu.roll(ext, 1, 0)) + w[2:3] * ext)


def _act_blocks(s, ns):
    return _pick(s, 512, ROWS), _pick(ns, 256)


def _ffn_act_fwd(a, cw, cb, name):
    _, s, ns = a.shape
    bs, cb_w = _act_blocks(s, ns)
    hb = bs // SUBLANE

    def body(a_ref, prev_ref, cw_ref, cb_ref, o_ref):
        first = pl.program_id(0) == 0

        def conv(comp):
            prev = jnp.where(first, 0.0, prev_ref[comp])
            ext = jnp.concatenate([prev, a_ref[comp]], axis=0)
            return _conv3(ext, cw_ref[comp], cb_ref[comp])[SUBLANE:]

        for p in range(2):
            cg = conv(p)
            o_ref[p] = (cg * _sigmoid(cg) * conv(2 + p)).astype(BF16)

    return pl.pallas_call(
        body, name=name, grid=(s // bs, ns // cb_w),
        in_specs=[
            pl.BlockSpec((4, bs, cb_w), lambda i, j: (0, i, j)),
            pl.BlockSpec((4, SUBLANE, cb_w), lambda i, j: (0, jnp.maximum(i * hb - 1, 0), j)),
            pl.BlockSpec((4, 3, cb_w), lambda i, j: (0, 0, j)),
            pl.BlockSpec((4, 1, cb_w), lambda i, j: (0, 0, j)),
        ],
        out_specs=pl.BlockSpec((2, bs, cb_w), lambda i, j: (0, i, j)),
        out_shape=jax.ShapeDtypeStruct((2, s, ns), BF16),
        compiler_params=_cp("parallel", "parallel"),
    )(a, a, cw, cb)


def _ffn_act_bwd(a, dhm, cw, cb, name, rides=()):
    _, s, ns = a.shape
    bs, cb_w = _act_blocks(s, ns)
    hb = bs // SUBLANE
    n_i = s // bs
    n_ext = bs + 2 * SUBLANE
    cur = slice(SUBLANE, SUBLANE + bs)

    def body(a_ref, prev_ref, next_ref, d_ref, dnext_ref, cw_ref, cb_ref, da_ref, dcw_ref, dcb_ref):
        i = pl.program_id(1)
        first, last = i == 0, i == n_i - 1

        @pl.when(first)
        def _():
            dcw_ref[...] = jnp.zeros_like(dcw_ref)
            dcb_ref[...] = jnp.zeros_like(dcb_ref)

        def ext_of(comp):
            return jnp.concatenate([jnp.where(first, 0.0, prev_ref[comp]), a_ref[comp], next_ref[comp]], axis=0)

        def back(comp, a_ext, dc):
            w = cw_ref[comp]
            da = (w[2:3] * dc + w[1:2] * pltpu.roll(dc, n_ext - 1, 0)) + w[0:1] * pltpu.roll(dc, n_ext - 2, 0)
            da_ref[comp] = da[cur].astype(BF16)
            dcc = dc[cur]
            dcw_ref[comp, 0:1, :] += jnp.sum(dcc * pltpu.roll(a_ext, 2, 0)[cur], axis=0, keepdims=True)
            dcw_ref[comp, 1:2, :] += jnp.sum(dcc * pltpu.roll(a_ext, 1, 0)[cur], axis=0, keepdims=True)
            dcw_ref[comp, 2:3, :] += jnp.sum(dcc * a_ext[cur], axis=0, keepdims=True)
            dcb_ref[comp] += jnp.sum(dcc, axis=0, keepdims=True)

        for p in range(2):
            ag, av = ext_of(p), ext_of(2 + p)
            cg = _conv3(ag, cw_ref[p], cb_ref[p])
            cv = _conv3(av, cw_ref[2 + p], cb_ref[2 + p])
            d = jnp.concatenate(
                [jnp.zeros((SUBLANE, cb_w), F32), d_ref[p], jnp.where(last, 0.0, dnext_ref[p])], axis=0)
            sg = _sigmoid(cg)
            back(2 + p, av, d * (cg * sg))
            back(p, ag, d * cv * (sg * (1.0 + cg * (1.0 - sg))))

    return _hosted_call(
        body, [a, a, a, dhm, dhm, cw, cb], name=name, grid=(ns // cb_w, n_i),
        in_specs=[
            pl.BlockSpec((4, bs, cb_w), lambda j, i: (0, i, j)),
            pl.BlockSpec((4, SUBLANE, cb_w), lambda j, i: (0, jnp.maximum(i * hb - 1, 0), j)),
            pl.BlockSpec((4, SUBLANE, cb_w), lambda j, i: (0, jnp.minimum((i + 1) * hb, n_i * hb - 1), j)),
            pl.BlockSpec((2, bs, cb_w), lambda j, i: (0, i, j)),
            pl.BlockSpec((2, SUBLANE, cb_w), lambda j, i: (0, jnp.minimum((i + 1) * hb, n_i * hb - 1), j)),
            pl.BlockSpec((4, 3, cb_w), lambda j, i: (0, 0, j)),
            pl.BlockSpec((4, 1, cb_w), lambda j, i: (0, 0, j)),
        ],
        out_specs=[
            pl.BlockSpec((4, bs, cb_w), lambda j, i: (0, i, j)),
            pl.BlockSpec((4, 3, cb_w), lambda j, i: (0, 0, j)),
            pl.BlockSpec((4, 1, cb_w), lambda j, i: (0, 0, j)),
        ],
        out_shape=[
            jax.ShapeDtypeStruct((4, s, ns), BF16),
            jax.ShapeDtypeStruct((4, 3, ns), F32),
            jax.ShapeDtypeStruct((4, 1, ns), F32),
        ],
        semantics=("parallel", "arbitrary"), rides=rides)


ATT_BQ_FWD = 2048
ATT_BQ_BWD = 1024
ATT_BK = 256
ATT_UNROLL = 2
ATT_UNROLL_BWD = 4


def _att_blocks(s, bq_pref):
    bq = _pick(s, bq_pref)
    bk = min(ATT_BK, bq)
    assert bq % bk == 0
    return bq, bk


def _dot_sel2(x, sel):
    hi = x.astype(BF16)
    lo = (x - hi.astype(F32)).astype(BF16)
    n = x.shape[0]
    both = jnp.dot(jnp.concatenate([hi, lo], axis=0), sel, preferred_element_type=F32)
    return both[:n] + both[n:]


def _causal_mask(bq, bk, row0, col0):
    rows = row0 + lax.broadcasted_iota(jnp.int32, (bq, bk), 0)
    cols = col0 + lax.broadcasted_iota(jnp.int32, (bq, bk), 1)
    return cols < rows


def _sb_tile(qb, kb, scale, mask):
    z = lax.dot_general(qb, kb, _DIMS["nt"], preferred_element_type=F32) * scale
    e = jnp.exp(-jnp.abs(z))
    lb = jnp.minimum(z, 0.0) - jnp.log(1.0 + e)
    l1m = lb - z
    if mask is not None:
        l1m = jnp.where(mask, l1m, 0.0)
    return z, e, lb, l1m


def _attn_fwd(q, k, v, name, rides=()):
    s, hd = q.shape
    bq, bk = _att_blocks(s, ATT_BQ_FWD)
    r = bq // bk
    unroll = math.gcd(r, ATT_UNROLL)
    n_h, n_q = hd // TILE, s // bq
    scale = 1.0 / math.sqrt(TILE)

    def body(q_ref, k_ref, v_ref, o_ref, l_ref, acc_ref, suf_ref):
        i = pl.program_id(1)
        qb = q_ref[...]
        later = _tri(bk, "gt")
        acc_ref[...] = jnp.zeros_like(acc_ref)
        suf_ref[...] = jnp.zeros_like(suf_ref)

        def tile(j, row0):
            rows = pl.ds(pl.multiple_of(j * bk, bk), bk)
            masked = row0 is not None
            r0 = row0 if masked else 0
            rs = pl.ds(r0, bq - r0)
            mask = _causal_mask(bq - r0, bk, i * bq + r0, j * bk) if masked else None
            _, _, lb, l1m = _sb_tile(qb[r0:], k_ref[rows, :], scale, mask)
            a = jnp.exp(lb + _dot_sel2(l1m, later) + suf_ref[rs, :])
            if masked:
                a = jnp.where(mask, a, 0.0)
            acc_ref[rs, :] += jnp.dot(a.astype(BF16), v_ref[rows, :], preferred_element_type=F32)
            suf_ref[rs, :] += jnp.sum(l1m, axis=1, keepdims=True)

        for dgl in range(r - 1, -1, -1):
            tile(r * i + dgl, dgl * bk)

        def step(t, carry):
            for u in range(unroll):
                tile(r * i - 1 - (unroll * t + u), None)
            return carry

        lax.fori_loop(0, (r * i) // unroll, step, 0)
        o_ref[...] = acc_ref[...].astype(BF16)
        l_ref[...] = jnp.broadcast_to(suf_ref[...], (bq, TILE))

    blk = pl.BlockSpec((bq, TILE), lambda h, i: (i, h))
    head = pl.BlockSpec((s, TILE), lambda h, i: (0, h))
    return _hosted_call(
        body, [q, k, v], name=name, grid=(n_h, n_q), in_specs=[blk, head, head], out_specs=[blk, blk],
        out_shape=[jax.ShapeDtypeStruct((s, hd), BF16), jax.ShapeDtypeStruct((s, hd), F32)],
        scratch_shapes=[pltpu.VMEM((bq, TILE), F32), pltpu.VMEM((bq, 1), F32)],
        semantics=("parallel", "parallel"), rides=rides)


def _attn_bwd(q, k, v, do, lsum, name, rides=()):
    s, hd = q.shape
    bq, bk = _att_blocks(s, ATT_BQ_BWD)
    r = bq // bk
    unroll = math.gcd(r, ATT_UNROLL_BWD)
    n_h, n_q = hd // TILE, s // bq
    scale = 1.0 / math.sqrt(TILE)

    def body(q_ref, k_ref, v_ref, do_ref, l_ref, dq_ref, dk_ref, dv_ref, dq_acc, pre_ref, cp_ref):
        i = pl.program_id(1)

        @pl.when(i == 0)
        def _():
            dk_ref[...] = jnp.zeros_like(dk_ref)
            dv_ref[...] = jnp.zeros_like(dv_ref)

        qb = q_ref[...]
        dob = do_ref[...]
        upto = _tri(bk, "le")
        before = _tri(bk, "lt")
        dq_acc[...] = jnp.zeros_like(dq_acc)
        pre_ref[...] = jnp.zeros_like(pre_ref)
        cp_ref[...] = jnp.zeros_like(cp_ref)

        def tile(j, row0):
            rows = pl.ds(pl.multiple_of(j * bk, bk), bk)
            kb, vb = k_ref[rows, :], v_ref[rows, :]
            masked = row0 is not None
            r0 = row0 if masked else 0
            rs = pl.ds(r0, bq - r0)
            qs, dos = qb[r0:], dob[r0:]
            mask = _causal_mask(bq - r0, bk, i * bq + r0, j * bk) if masked else None
            z, e, lb, l1m = _sb_tile(qs, kb, scale, mask)
            suffix = (l_ref[rs, 0:1] - pre_ref[rs, :]) - _dot_sel2(l1m, upto)
            a = jnp.exp(lb + suffix)
            if masked:
                a = jnp.where(mask, a, 0.0)
            p = a * lax.dot_general(dos, vb, _DIMS["nt"], preferred_element_type=F32)
            both = p + (cp_ref[rs, :] + jnp.dot(p.astype(BF16), before, preferred_element_type=F32))
            sg = jnp.where(z >= 0.0, 1.0, e) * pl.reciprocal(1.0 + e, approx=True)
            dz = p - both * sg
            if masked:
                dz = jnp.where(mask, dz, 0.0)
            dz = (dz * scale).astype(BF16)
            dq_acc[rs, :] += jnp.dot(dz, kb, preferred_element_type=F32)
            dk_ref[rows, :] += lax.dot_general(dz, qs, _DIMS["tn"], preferred_element_type=F32)
            dv_ref[rows, :] += lax.dot_general(a.astype(BF16), dos, _DIMS["tn"], preferred_element_type=F32)
            pre_ref[rs, :] += jnp.sum(l1m, axis=1, keepdims=True)
            cp_ref[rs, :] += jnp.sum(p, axis=1, keepdims=True)

        def step(j, carry):
            for u in range(unroll):
                tile(unroll * j + u, None)
            return carry

        lax.fori_loop(0, (r * i) // unroll, step, 0)
        for dgl in range(r):
            tile(r * i + dgl, dgl * bk)
        dq_ref[...] = dq_acc[...].astype(BF16)

    blk = pl.BlockSpec((bq, TILE), lambda h, i: (i, h))
    head = pl.BlockSpec((s, TILE), lambda h, i: (0, h))
    return _hosted_call(
        body, [q, k, v, do, lsum], name=name, grid=(n_h, n_q), in_specs=[blk, head, head, blk, blk],
        out_specs=[blk, head, head],
        out_shape=[jax.ShapeDtypeStruct((s, hd), BF16), jax.ShapeDtypeStruct((s, hd), F32),
                   jax.ShapeDtypeStruct((s, hd), F32)],
        scratch_shapes=[pltpu.VMEM((bq, TILE), F32), pltpu.VMEM((bq, 1), F32), pltpu.VMEM((bq, 1), F32)],
        semantics=("parallel", "arbitrary"), rides=rides)


EW_BLOCK = 512 * 1024


def _ew_blocks(r, c, elems=EW_BLOCK):
    return _pick(r, max(ROWS, elems // c // ROWS * ROWS), ROWS), c


def _cast_bf16(w, layer, chip_idx, name):
    _, r, c = w.shape
    br, bc = _ew_blocks(r, c)

    def body(chip_ref, w_ref, o_ref):
        o_ref[...] = w_ref[...].astype(BF16)

    return pl.pallas_call(
        body, name=name,
        grid_spec=pltpu.PrefetchScalarGridSpec(
            num_scalar_prefetch=1, grid=(r // br, c // bc),
            in_specs=[pl.BlockSpec((None, br, bc), lambda i, j, chip_ref: (layer, i, j))],
            out_specs=pl.BlockSpec((None, br, bc), lambda i, j, chip_ref: (chip_ref[0], i, j)),
        ),
        out_shape=jax.ShapeDtypeStruct((N_CHIPS, r, c), BF16), compiler_params=_cp("parallel", "parallel"),
    )(chip_idx, w)


def _pair_add(dw, recv, c_idx, name):
    _, r, c = dw.shape
    hr = r // 2
    br, bc = _ew_blocks(hr, c)
    nb = hr // br

    def body(c_ref, a_ref, b_ref, o_ref):
        o_ref[...] = (a_ref[...].astype(F32) + b_ref[...].astype(F32)).astype(BF16)

    return pl.pallas_call(
        body, name=name,
        grid_spec=pltpu.PrefetchScalarGridSpec(
            num_scalar_prefetch=1, grid=(N_CHIPS, nb, c // bc),
            in_specs=[
                pl.BlockSpec((None, br, bc), lambda s, i, j, c_ref: (s, c_ref[0] * nb + i, j)),
                pl.BlockSpec((None, br, bc), lambda s, i, j, c_ref: (s, i, j)),
            ],
            out_specs=pl.BlockSpec((None, br, bc), lambda s, i, j, c_ref: (s, i, j)),
        ),
        out_shape=jax.ShapeDtypeStruct((N_CHIPS, hr, c), BF16),
        compiler_params=_cp("parallel", "parallel", "parallel"),
    )(c_idx, dw, recv)


def _chip_sum(parts, dest, shape, layer, c_idx, name):
    _, hr, c = parts.shape
    br, bc = _ew_blocks(hr, c, EW_BLOCK // 2)
    nb = hr // br

    def body(c_ref, p_ref, *refs):
        o_ref = refs[-1]
        acc = p_ref[0].astype(F32)
        for s in range(1, N_CHIPS):
            acc = acc + p_ref[s].astype(F32)
        o_ref[...] = acc

    in_specs = [pl.BlockSpec((N_CHIPS, br, bc), lambda i, j, c_ref: (0, i, j))]
    operands = [c_idx, parts]
    aliases = {}
    if dest is not None:
        in_specs.append(ANY)
        operands.append(dest)
        aliases = {2: 0}
    return pl.pallas_call(
        body, name=name,
        grid_spec=pltpu.PrefetchScalarGridSpec(
            num_scalar_prefetch=1, grid=(nb, c // bc), in_specs=in_specs,
            out_specs=pl.BlockSpec((None, br, bc), lambda i, j, c_ref: (layer, c_ref[0] * nb + i, j)),
        ),
        out_shape=jax.ShapeDtypeStruct(shape, F32), input_output_aliases=aliases,
        compiler_params=_cp("parallel", "parallel"),
    )(*operands)


def _adamw(w, g, m, v, name, pass_g=False):
    n_l, r, c = w.shape
    br, bc = _ew_blocks(r, c, EW_BLOCK // 2)

    def body(w_ref, g_ref, m_ref, v_ref, *out_refs):
        d_ref, mo_ref, vo_ref = out_refs[-3:]
        g = g_ref[...]
        if pass_g:
            out_refs[0][...] = g
        m = ADAM_B1 * m_ref[...] + (1.0 - ADAM_B1) * g
        v = ADAM_B2 * v_ref[...] + (1.0 - ADAM_B2) * (g * g)
        m_hat = m / (1.0 - ADAM_B1 ** ADAM_STEP)
        v_hat = v / (1.0 - ADAM_B2 ** ADAM_STEP)
        d_ref[...] = -ADAM_LR * (m_hat / (jnp.sqrt(v_hat) + ADAM_EPS) + ADAM_WD * w_ref[...])
        mo_ref[...] = m
        vo_ref[...] = v

    blk = pl.BlockSpec((None, br, bc), lambda l, i, j: (l, i, j))
    n_out = 4 if pass_g else 3
    return pl.pallas_call(
        body, name=name, grid=(n_l, r // br, c // bc), in_specs=[blk] * 4, out_specs=[blk] * n_out,
        out_shape=[jax.ShapeDtypeStruct(w.shape, F32)] * n_out,
        compiler_params=_cp("parallel", "parallel", "parallel"),
    )(w, g, m, v)


def _place():
    x, y, c = lax.axis_index("x"), lax.axis_index("y"), lax.axis_index("c")
    chips = [(1 - x, y), (x, 1 - y), (1 - x, 1 - y)]
    return x, y, c, chips


class _Ride:
    def __init__(self, reads, bufs, new, n_sems, start, finish):
        self.reads, self.bufs, self.new, self.n_sems, self.start, self.finish = reads, bufs, new, n_sems, start, finish


def _hosted_call(body, operands, *, name, grid, in_specs, out_specs, out_shape, scratch_shapes=(), semantics=(), rides=()):
    single = not isinstance(out_shape, (list, tuple))
    out_specs = [out_specs] if single else list(out_specs)
    out_shape = [out_shape] if single else list(out_shape)
    in_specs, scratch_shapes = list(in_specs), list(scratch_shapes)
    n_in, n_out, n_scr = len(in_specs), len(out_shape), len(scratch_shapes)
    extra_in, extra_out, aliases, where = [], [], {}, []
    for ride in rides:
        r0 = len(extra_in)
        extra_in += list(ride.reads)
        b0 = len(extra_in)
        extra_in += list(ride.bufs)
        ob0 = len(extra_out)
        extra_out += [jax.ShapeDtypeStruct(b.shape, b.dtype) for b in ride.bufs]
        for t in range(len(ride.bufs)):
            aliases[n_in + b0 + t] = n_out + ob0 + t
        on0 = len(extra_out)
        extra_out += list(ride.new)
        where.append((r0, len(ride.reads), ob0, len(ride.bufs), on0, len(ride.new)))
    n_ein, n_eout = len(extra_in), len(extra_out)
    sem_shapes = [pltpu.SemaphoreType.DMA((max(1, k),)) for ride in rides for k in ride.n_sems]

    def full_body(*refs):
        ins, outs, scr = refs[:n_in + n_ein], refs[n_in + n_ein:n_in + n_ein + n_out + n_eout], refs[n_in + n_ein + n_out + n_eout:]

        def run(which):
            for idx, (ride, (r0, nr, ob0, nb, on0, nn)) in enumerate(zip(rides, where)):
                fn = ride.start if which == 0 else ride.finish
                fn(ins[n_in + r0:n_in + r0 + nr], outs[n_out + ob0:n_out + ob0 + nb], outs[n_out + on0:n_out + on0 + nn],
                   *scr[n_scr + 3 * idx:n_scr + 3 * idx + 3])

        host = lambda: body(*ins[:n_in], *outs[:n_out], *scr[:n_scr])
        if not rides:
            host()
        elif not grid:
            run(0)
            host()
            run(1)
        else:
            ids = [pl.program_id(ax) for ax in range(len(grid))]
            first = functools.reduce(jnp.logical_and, [i == 0 for i in ids])
            last = functools.reduce(jnp.logical_and, [i == g - 1 for i, g in zip(ids, grid)])
            pl.when(first)(lambda: run(0))
            host()
            pl.when(last)(lambda: run(1))

    if rides:
        params = pltpu.CompilerParams(dimension_semantics=("arbitrary",) * len(grid), vmem_limit_bytes=VMEM_LIMIT)
    else:
        params = _cp(*semantics)
    outs = pl.pallas_call(
        full_body, name=name, grid=grid,
        in_specs=in_specs + [ANY] * n_ein, out_specs=out_specs + [ANY] * n_eout,
        out_shape=out_shape + extra_out, input_output_aliases=aliases,
        scratch_shapes=scratch_shapes + sem_shapes, compiler_params=params,
    )(*operands, *extra_in)
    main = outs[0] if single else list(outs[:n_out])
    rode = [(list(outs[n_out + ob0:n_out + ob0 + nb]), list(outs[n_out + on0:n_out + on0 + nn]))
            for (_, _, ob0, nb, on0, nn) in where]
    return main, rode


def _run_rides(rides, name):
    return _hosted_call(lambda: None, [], name=name, grid=(), in_specs=[], out_specs=[], out_shape=[], rides=rides)[1]


def _ride_gather(slots, part=0, n_parts=1, span=1):
    n = len(slots)
    halves = [a.shape[1] // 2 for a in slots]
    sizes = [hr // n_parts for hr in halves]
    assert part + span <= n_parts
    for a, hr, size in zip(slots, halves, sizes):
        assert a.shape[1] == 2 * hr and hr == size * n_parts and size % ROWS == 0, a.shape

    def remote(bufs, send_sems, recv_sems, i, k, slot, core, to):
        rows = bufs[i].at[slot, pl.ds(pl.multiple_of(core * halves[i] + part * sizes[i], ROWS), span * sizes[i])]
        return pltpu.make_async_remote_copy(
            src_ref=rows, dst_ref=rows, send_sem=send_sems.at[i * 6 + k], recv_sem=recv_sems.at[i * 6 + k],
            device_id=to, device_id_type=MESH)

    def start(reads, bufs, new, send_sems, recv_sems, local_sems):
        x, y, c, chips = _place()
        for i in range(n):
            for k, (px, py) in enumerate(chips):
                remote(bufs, send_sems, recv_sems, i, k, 2 * x + y, c, (px, py, c)).start()

    def finish(reads, bufs, new, send_sems, recv_sems, local_sems):
        x, y, c, chips = _place()
        cp = functools.partial(remote, bufs, send_sems, recv_sems)
        for i in range(n):
            for k, (px, py) in enumerate(chips):
                cp(i, k, 2 * px + py, c, (x, y, c)).wait_recv()
                cp(i, 3 + k, 2 * px + py, c, (x, y, 1 - c)).start()
        for i in range(n):
            for k, (px, py) in enumerate(chips):
                cp(i, 3 + k, 2 * px + py, 1 - c, (x, y, c)).wait_recv()
        for i in range(n):
            for k, (px, py) in enumerate(chips):
                cp(i, k, 2 * x + y, c, (px, py, c)).wait_send()
                cp(i, 3 + k, 2 * px + py, c, (x, y, 1 - c)).wait_send()

    return _Ride([], slots, [], (6 * n, 6 * n, 0), start, finish)


def _ride_swap(grads):
    n = len(grads)
    halves = [a.shape[1] // 2 for a in grads]

    def copies(reads, new, send_sems, recv_sems):
        x, y, c, _ = _place()
        out = []
        for i in range(n):
            rows = pl.ds(pl.multiple_of((1 - c) * halves[i], 2 * SUBLANE), halves[i])
            out.append(pltpu.make_async_remote_copy(
                src_ref=reads[i].at[:, rows, :], dst_ref=new[i], send_sem=send_sems.at[i], recv_sem=recv_sems.at[i],
                device_id=(x, y, 1 - c), device_id_type=MESH))
        return out

    def start(reads, bufs, new, send_sems, recv_sems, local_sems):
        for cp in copies(reads, new, send_sems, recv_sems):
            cp.start()

    def finish(reads, bufs, new, send_sems, recv_sems, local_sems):
        for cp in copies(reads, new, send_sems, recv_sems):
            cp.wait()

    shapes = [jax.ShapeDtypeStruct((N_CHIPS, hr, a.shape[2]), a.dtype) for a, hr in zip(grads, halves)]
    return _Ride(grads, [], shapes, (n, n, 0), start, finish)


def _ride_scatter(parts, part=0, n_parts=1, into=None):
    n = len(parts)
    sizes = [a.shape[1] // n_parts for a in parts]
    for a, size in zip(parts, sizes):
        assert a.shape[1] == size * n_parts and size % ROWS == 0, a.shape

    def piece(ref, i, slot):
        return ref.at[slot, pl.ds(part * sizes[i], sizes[i])]

    def own(reads, land, local_sems, i):
        me = 2 * lax.axis_index("x") + lax.axis_index("y")
        return pltpu.make_async_copy(piece(reads[i], i, me), piece(land[i], i, me), local_sems.at[i])

    def send(reads, land, send_sems, recv_sems, i, k):
        x, y, c, chips = _place()
        px, py = chips[k]
        return pltpu.make_async_remote_copy(
            src_ref=piece(reads[i], i, 2 * px + py), dst_ref=piece(land[i], i, 2 * x + y),
            send_sem=send_sems.at[3 * i + k], recv_sem=recv_sems.at[3 * i + k],
            device_id=(px, py, c), device_id_type=MESH)

    def start(reads, bufs, new, send_sems, recv_sems, local_sems):
        land = new if into is None else bufs
        for i in range(n):
            own(reads, land, local_sems, i).start()
            for k in range(3):
                send(reads, land, send_sems, recv_sems, i, k).start()

    def finish(reads, bufs, new, send_sems, recv_sems, local_sems):
        land = new if into is None else bufs
        x, y, c, chips = _place()
        for i in range(n):
            for k, (px, py) in enumerate(chips):
                slot = piece(land[i], i, 2 * px + py)
                pltpu.make_async_remote_copy(
                    src_ref=slot, dst_ref=slot, send_sem=send_sems.at[3 * i + k], recv_sem=recv_sems.at[3 * i + k],
                    device_id=(x, y, c), device_id_type=MESH).wait_recv()
        for i in range(n):
            for k in range(3):
                send(reads, land, send_sems, recv_sems, i, k).wait_send()
            own(reads, land, local_sems, i).wait()

    shapes = [jax.ShapeDtypeStruct(a.shape, a.dtype) for a in parts]
    if into is None:
        return _Ride(parts, [], shapes, (3 * n, 3 * n, n), start, finish)
    return _Ride(parts, list(into), [], (3 * n, 3 * n, n), start, finish)


def _ride_join(grads):
    n = len(grads)

    def copy(bufs, send_sems, recv_sems, i, core, to):
        hr = grads[i].shape[1] // 2
        rows = bufs[i].at[:, pl.ds(pl.multiple_of(core * hr, SUBLANE), hr), :]
        return pltpu.make_async_remote_copy(
            src_ref=rows, dst_ref=rows, send_sem=send_sems.at[i], recv_sem=recv_sems.at[i],
            device_id=to, device_id_type=MESH)

    def start(reads, bufs, new, send_sems, recv_sems, local_sems):
        x, y, c, _ = _place()
        for i in range(n):
            copy(bufs, send_sems, recv_sems, i, c, (x, y, 1 - c)).start()

    def finish(reads, bufs, new, send_sems, recv_sems, local_sems):
        x, y, c, _ = _place()
        for i in range(n):
            copy(bufs, send_sems, recv_sems, i, 1 - c, (x, y, c)).wait_recv()
        for i in range(n):
            copy(bufs, send_sems, recv_sems, i, c, (x, y, 1 - c)).wait_send()

    return _Ride([], grads, [], (n, n, 0), start, finish)


def _all_reduce_small(packed, name, rides=()):
    r, c = packed.shape
    chunk = _pick(r, 256, ROWS)

    def body(x_ref, out_ref, gath, send_sems, recv_sems, local_sem):
        x, y, cc, chips = _place()
        me, sibling = (x, y, cc), (x, y, 1 - cc)

        def slot(px, py, pc):
            return gath.at[4 * px + 2 * py + pc]

        def copy(k, block, to, src=None):
            return pltpu.make_async_remote_copy(
                src_ref=slot(*block) if src is None else src, dst_ref=slot(*block),
                send_sem=send_sems.at[k], recv_sem=recv_sems.at[k], device_id=to, device_id_type=MESH)

        mine = pltpu.make_async_copy(x_ref, slot(*me), local_sem)
        mine.start()
        first = [copy(0, me, sibling, src=x_ref)]
        first += [copy(1 + j, me, (*chip, cc), src=x_ref) for j, chip in enumerate(chips)]
        for cp in first:
            cp.start()
        passed = [copy(4 + j, (*chip, cc), sibling) for j, chip in enumerate(chips)]
        for j, chip in enumerate(chips):
            copy(1 + j, (*chip, cc), me).wait_recv()
            passed[j].start()
        copy(0, sibling, me).wait_recv()
        for j, chip in enumerate(chips):
            copy(4 + j, (*chip, 1 - cc), me).wait_recv()
        for cp in first + passed:
            cp.wait_send()
        mine.wait()

        def add(i, carry):
            rows = pl.ds(pl.multiple_of(i * chunk, SUBLANE), chunk)
            acc = gath[0, rows, :]
            for dev in range(1, N_DEV):
                acc = acc + gath[dev, rows, :]
            out_ref[rows, :] = acc
            return carry

        lax.fori_loop(0, r // chunk, add, 0)

    return _hosted_call(
        body, [packed], name=name, grid=(), in_specs=[VMEM_SPEC], out_specs=VMEM_SPEC,
        out_shape=jax.ShapeDtypeStruct((r, c), F32),
        scratch_shapes=[pltpu.VMEM((N_DEV, r, c), F32), pltpu.SemaphoreType.DMA((7,)),
                        pltpu.SemaphoreType.DMA((7,)), pltpu.SemaphoreType.DMA],
        rides=rides)


_PACK_ROWS = 256


def _pack(arrays):
    flat = jnp.concatenate([a.reshape(-1).astype(F32) for a in arrays])
    unit = _PACK_ROWS * LANE
    total = -(-flat.shape[0] // unit) * unit
    return jnp.pad(flat, (0, total - flat.shape[0])).reshape(-1, LANE)


def _unpack(packed, shapes, lead=()):
    flat = packed.reshape(lead + (-1,))
    out, at = [], 0
    for s in shapes:
        size = math.prod(s)
        out.append(flat[..., at:at + size].reshape(lead + tuple(s)))
        at += size
    return out


def kernel(x, pre_mix_g, post_mix_g, pre_ffn_g, post_ffn_g, a_w_in, a_v_norm_g, a_w_spatial, a_b_spatial, a_w_out, kv_norm_g, w_k, w_v, b_w_q, b_w_o, ffn_w_up, ffn_conv_w, ffn_conv_b, ffn_w_down, loss_target, m_pre_mix_g, m_post_mix_g, m_pre_ffn_g, m_post_ffn_g, m_a_w_in, m_a_v_norm_g, m_a_w_spatial, m_a_b_spatial, m_a_w_out, m_kv_norm_g, m_w_k, m_w_v, m_b_w_q, m_b_w_o, m_ffn_w_up, m_ffn_conv_w, m_ffn_conv_b, m_ffn_w_down, v_pre_mix_g, v_post_mix_g, v_pre_ffn_g, v_post_ffn_g, v_a_w_in, v_a_v_norm_g, v_a_w_spatial, v_a_b_spatial, v_a_w_out, v_kv_norm_g, v_w_k, v_w_v, v_b_w_q, v_b_w_o, v_ffn_w_up, v_ffn_conv_w, v_ffn_conv_b, v_ffn_w_down):
    xi, yi, ci = lax.axis_index("x"), lax.axis_index("y"), lax.axis_index("c")
    chip = 2 * xi + yi
    c_idx = jnp.reshape(ci, (1,)).astype(jnp.int32)
    _, s, d = x.shape
    n_layers = pre_mix_g.shape[0]
    assert n_layers == 2 and a_w_in.shape[0] == 1 and b_w_q.shape[0] == 1
    d_a = a_w_out.shape[1] * N_CHIPS
    n_g = a_w_spatial.shape[1]
    ns = ffn_w_up.shape[2]
    assert a_w_spatial.shape[2] == TILE and d_a == n_g * TILE and s % TILE == 0
    h0 = x[0]
    target = loss_target[0]

    big = {
        "win": (a_w_in, m_a_w_in, v_a_w_in),
        "wout": (a_w_out, m_a_w_out, v_a_w_out),
        "wk": (w_k[None], m_w_k[None], v_w_k[None]),
        "wv": (w_v[None], m_w_v[None], v_w_v[None]),
        "wq": (b_w_q, m_b_w_q, v_b_w_q),
        "wo": (b_w_o, m_b_w_o, v_b_w_o),
        "wup": (ffn_w_up, m_ffn_w_up, v_ffn_w_up),
        "wdn": (ffn_w_down, m_ffn_w_down, v_ffn_w_down),
    }
    units = [(nm, layer) for nm in big for layer in range(big[nm][0].shape[0])]
    chip_idx = jnp.reshape(chip, (1,)).astype(jnp.int32)
    shards = [_cast_bf16(big[nm][0], layer, chip_idx, f"cast_{nm}{layer}") for nm, layer in units]
    small_sharded = _pack([a_v_norm_g, ffn_conv_w])
    small_sharded = lax.dynamic_update_index_in_dim(
        jnp.zeros((N_CHIPS,) + small_sharded.shape, F32), small_sharded, chip, 0)
    own = dict(zip(units, shards))
    full = {}

    def gather_ride(keys):
        return _ride_gather([own[key] for key in keys])

    def gathered(keys, rode):
        full.update(zip(keys, rode[0]))

    first_keys = [("win", 0)]
    (first_bufs, _), = _run_rides([_ride_gather([own[key] for key in first_keys] + [small_sharded])], "gather_first")
    full.update(zip(first_keys, first_bufs[:-1]))
    vg_parts, cw_parts = _unpack(first_bufs[-1], [a_v_norm_g.shape, ffn_conv_w.shape], lead=(N_CHIPS,))
    v_g = jnp.transpose(vg_parts, (1, 0, 2)).reshape(1, d_a)

    def rows(nm, layer=0):
        w = full[(nm, layer)]
        return w.reshape(w.shape[0] * w.shape[1], w.shape[2])

    gains = lambda g, layer: g[layer:layer + 1]
    bias = jnp.repeat(a_b_spatial[0].T, TILE, axis=1)
    w_s = a_w_spatial[0]
    kv_g = kv_norm_g[None]
    conv_w = [cw_parts[:, layer] for layer in range(n_layers)]
    conv_b = [ffn_conv_b[layer].reshape(N_CHIPS, 1, ns) for layer in range(n_layers)]

    def ffn_fwd(hn, layer, up_keys=(), down_keys=()):
        a = _mm(hn, full[("wup", layer)], "nn", f"ffn_up{layer}", out_split=N_CHIPS,
                rides=[gather_ride(up_keys)] if up_keys else ())
        if up_keys:
            a, (rode,) = a
            gathered(up_keys, rode)
        hm = _ffn_act_fwd(a, conv_w[layer], conv_b[layer], f"ffn_act{layer}")
        f = _mm(hm, rows("wdn", layer), "nn", f"ffn_down{layer}", rides=[gather_ride(down_keys)] if down_keys else ())
        if down_keys:
            f, (rode,) = f
            gathered(down_keys, rode)
        return a, hm, f[0]

    up0 = own[("wup", 0)]
    pieces = lambda p, span: _ride_gather([up0], part=p, n_parts=8, span=span)
    hn0 = _rms_fwd(h0, gains(pre_mix_g, 0), "norm_in")
    uv, ((out_bufs, _), ((up0,), _)) = _mm(
        hn0, full[("win", 0)], "nn", "gmlp_in", out_split=N_CHIPS, rides=[gather_ride([("wout", 0)]), pieces(0, 1)])
    full[("wout", 0)] = out_bufs[0]
    gm, (((up0,), _),) = _gmlp_fwd(uv, v_g, w_s, bias, "gmlp_gate", rides=[pieces(1, 2)])
    mix0, (((up0,), _),) = _mm(gm, rows("wout"), "nn", "gmlp_out", rides=[pieces(3, 2)])
    mix0 = mix0[0]
    (h1, hn1), (((up0,), _),) = _resid_rms(
        h0, mix0, gains(post_mix_g, 0), [gains(pre_ffn_g, 0)], "resid_mix0", rides=[pieces(5, 3)])
    full[("wup", 0)] = up0
    a0, hm0, f0 = ffn_fwd(hn1, 0, up_keys=[("wdn", 0), ("wq", 0), ("wk", 0)], down_keys=[("wv", 0), ("wo", 0)])
    h2, hn2, kvn = _resid_rms(h1, f0, gains(post_ffn_g, 0), [gains(pre_mix_g, 1), kv_g], "resid_ffn0")
    q = _mm(hn2, rows("wq"), "nn", "proj_q", out_dtype=BF16)[0]
    k = _mm(kvn, rows("wk"), "nn", "proj_k", out_dtype=BF16)[0]
    v = _mm(kvn, rows("wv"), "nn", "proj_v", out_dtype=BF16)[0]
    last_keys = [("wup", 1), ("wdn", 1)]
    (att, lsum), (rode,) = _attn_fwd(q, k, v, "attn_fwd", rides=[gather_ride(last_keys)])
    gathered(last_keys, rode)
    mix1 = _mm(att, rows("wo"), "nn", "proj_o")[0]
    h3, hn3 = _resid_rms(h2, mix1, gains(post_mix_g, 1), [gains(pre_ffn_g, 1)], "resid_mix1")
    a1, hm1, f1 = ffn_fwd(hn3, 1)
    dh4, loss_tile = _loss_head(h3, f1, gains(post_ffn_g, 1), target, "loss_head")
    loss = lax.psum(loss_tile[0, 0], ("x", "y", "c"))

    dw = {}
    dg = {}

    pair = {}
    half_done = {nm: None for nm in big}

    def swap_ride(keys):
        return _ride_swap([dw[key] for key in keys])

    def swapped(keys, rode):
        for (nm, layer), got in zip(keys, rode[1]):
            pair[(nm, layer)] = _pair_add(dw[(nm, layer)], got, c_idx, f"pair_add_{nm}{layer}")

    def scatter_ride(keys):
        return _ride_scatter([pair[key] for key in keys])

    def scattered(keys, rode):
        for (nm, layer), got in zip(keys, rode[1]):
            half_done[nm] = _chip_sum(got, half_done[nm], big[nm][0].shape, layer, c_idx, f"chip_sum_{nm}{layer}")

    def ffn_bwd(dh_out, h_in, hn, a, hm, f, layer, act_rides=()):
        df, dg[("post_ffn", layer)] = _rms_bwd_out(dh_out, f, gains(post_ffn_g, layer), f"d_norm_ffn_out{layer}")
        dwd = _mm(hm, df, "tn", f"d_w_down{layer}", out_dtype=BF16)[0]
        down, up = [("wdn", layer)], [("wup", layer)]
        dw[down[0]] = dwd.reshape(N_CHIPS, dwd.shape[0] // N_CHIPS, d)
        dhm, (rode,) = _mm(df, rows("wdn", layer), "nt", f"d_ffn_mid{layer}", out_split=2, rides=[swap_ride(down)])
        swapped(down, rode)
        (da, dg[("conv_w", layer)], dg[("conv_b", layer)]), act_rode = _ffn_act_bwd(
            a, dhm, conv_w[layer], conv_b[layer], f"d_ffn_act{layer}", rides=act_rides)
        dw[up[0]], (rode,) = _mm(hn, da, "tn", f"d_w_up{layer}", out_dtype=BF16, out_split=N_CHIPS,
                                 rides=[scatter_ride(down)])
        scattered(down, rode)
        dhn, (rode,) = _mm(da, full[("wup", layer)], "nt", f"d_ffn_in{layer}", rides=[swap_ride(up)])
        swapped(up, rode)
        return dhn[0], act_rode

    dhn3, _ = ffn_bwd(dh4, h3, hn3, a1, hm1, f1, 1)
    dh3, (dg[("pre_ffn", 1)],) = _rms_bwd_in(dh4, h3, [([dhn3], gains(pre_ffn_g, 1))], "d_norm_ffn_in1")
    dmix1, dg[("post_mix", 1)] = _rms_bwd_out(dh3, mix1, gains(post_mix_g, 1), "d_norm_mix_out1")
    dwo = _mm(att, dmix1, "tn", "d_w_o", out_dtype=BF16)[0]
    dw[("wo", 0)] = dwo.reshape(N_CHIPS, dwo.shape[0] // N_CHIPS, d)
    datt = _mm(dmix1, rows("wo"), "nt", "d_attn_out", out_dtype=BF16)[0]
    ffn1_keys = [("wup", 1)]
    (dq, dk, dv), (rode,) = _attn_bwd(q, k, v, datt, lsum, "attn_bwd", rides=[scatter_ride(ffn1_keys)])
    scattered(ffn1_keys, rode)
    for nm, act, dact in (("wq", hn2, dq), ("wk", kvn, dk), ("wv", kvn, dv)):
        g = _mm(act, dact, "tn", f"d_{nm}", out_dtype=BF16)[0]
        dw[(nm, 0)] = g.reshape(N_CHIPS, g.shape[0] // N_CHIPS, g.shape[1])
    dhn2 = _mm(dq, rows("wq"), "nt", "d_q_in")[0]
    dkvn_k = _mm(dk, rows("wk"), "nt", "d_k_in")[0]
    attn_keys = [("wo", 0), ("wq", 0), ("wk", 0), ("wv", 0)]
    dkvn_v, (rode,) = _mm(dv, rows("wv"), "nt", "d_v_in", rides=[swap_ride(attn_keys)])
    swapped(attn_keys, rode)
    dh2, (dg[("pre_mix", 1)], dg["kv"]) = _rms_bwd_in(
        dh3, h2, [([dhn2], gains(pre_mix_g, 1)), ([dkvn_k, dkvn_v[0]], kv_g)], "d_norm_mix_in1")
    dhn1, (rode,) = ffn_bwd(dh2, h1, hn1, a0, hm0, f0, 0, act_rides=[scatter_ride(attn_keys)])
    scattered(attn_keys, rode)
    dh1, (dg[("pre_ffn", 0)],) = _rms_bwd_in(dh2, h1, [([dhn1], gains(pre_ffn_g, 0))], "d_norm_ffn_in0")
    dmix0, dg[("post_mix", 0)] = _rms_bwd_out(dh1, mix0, gains(post_mix_g, 0), "d_norm_mix_out0")
    early = ["wq", "wk", "wv", "wo", "wdn"]
    dwout, (((joined_early, _)),) = _mm(
        gm, dmix0, "tn", "d_w_out", out_dtype=BF16, rides=[_ride_join([half_done[nm] for nm in early])])
    grads_big = dict(zip(early, joined_early))
    w_out_key, w_in_key = [("wout", 0)], [("win", 0)]
    dw[w_out_key[0]] = dwout[0].reshape(N_CHIPS, dwout.shape[1] // N_CHIPS, d)
    dgm, (rode,) = _mm(dmix0, rows("wout"), "nt", "d_gmlp_gate", rides=[swap_ride(w_out_key)])
    swapped(w_out_key, rode)
    up0_pair = [pair[("wup", 0)]]
    (duv, d_ws, d_bs, d_vg), ((_, up0_landed),) = _gmlp_bwd(
        uv, dgm[0], v_g, w_s, bias, "d_gmlp", rides=[_ride_scatter(up0_pair, 0, 2)])
    dw[w_in_key[0]], (rode,) = _mm(
        hn0, duv, "tn", "d_w_in", out_dtype=BF16, out_split=N_CHIPS, rides=[scatter_ride(w_out_key)])
    scattered(w_out_key, rode)
    dhn0, (rode, (up0_landed, _)) = _mm(
        duv, full[("win", 0)], "nt", "d_gmlp_in",
        rides=[swap_ride(w_in_key), _ride_scatter(up0_pair, 1, 2, into=up0_landed)])
    swapped(w_in_key, rode)
    scattered([("wup", 0)], (None, up0_landed))
    dx, (dg[("pre_mix", 0)],), (rode,) = _rms_bwd_in(
        dh1, h0, [([dhn0[0]], gains(pre_mix_g, 0))], "d_norm_in", rides=[scatter_ride(w_in_key)])
    scattered(w_in_key, rode)

    stack = lambda key: jnp.concatenate([dg[(key, layer)] for layer in range(n_layers)], axis=0)
    small_parts = [
        stack("pre_mix"), stack("post_mix"), stack("pre_ffn"), stack("post_ffn"),
        d_vg, d_ws, d_bs[::SUBLANE], dg["kv"],
        jnp.stack([dg[("conv_w", layer)] for layer in range(n_layers)]),
        jnp.stack([dg[("conv_b", layer)] for layer in range(n_layers)]),
    ]
    late = [nm for nm in big if nm not in early]
    summed, ((joined_late, _),) = _all_reduce_small(
        _pack(small_parts), "small_grads_sum", rides=[_ride_join([half_done[nm] for nm in late])])
    grads_big.update(zip(late, joined_late))
    (g_pre_mix, g_post_mix, g_pre_ffn, g_post_ffn, g_vg, g_ws, g_bs, g_kv, g_cw, g_cb) = _unpack(
        summed, [p.shape for p in small_parts])
    g_vg = lax.dynamic_index_in_dim(g_vg.reshape(N_CHIPS, 1, d_a // N_CHIPS), chip, 0, keepdims=False)
    g_cw = lax.dynamic_index_in_dim(g_cw, chip, 1, keepdims=False)
    g_cb = g_cb.reshape(n_layers, N_CHIPS * ns)
    small = [
        (pre_mix_g, g_pre_mix, m_pre_mix_g, v_pre_mix_g),
        (post_mix_g, g_post_mix, m_post_mix_g, v_post_mix_g),
        (pre_ffn_g, g_pre_ffn, m_pre_ffn_g, v_pre_ffn_g),
        (post_ffn_g, g_post_ffn, m_post_ffn_g, v_post_ffn_g),
        (a_v_norm_g, g_vg, m_a_v_norm_g, v_a_v_norm_g),
        (a_w_spatial, g_ws[None], m_a_w_spatial, v_a_w_spatial),
        (a_b_spatial, g_bs[None], m_a_b_spatial, v_a_b_spatial),
        (kv_norm_g, g_kv.reshape(d), m_kv_norm_g, v_kv_norm_g),
        (ffn_conv_w, g_cw, m_ffn_conv_w, v_ffn_conv_w),
        (ffn_conv_b, g_cb, m_ffn_conv_b, v_ffn_conv_b),
    ]
    small = [(w, g.reshape(w.shape), m, v) for w, g, m, v in small]
    packed = [_pack([t[i] for t in small])[None] for i in range(4)]
    small_new = [_unpack(p[0], [t[0].shape for t in small]) for p in _adamw(*packed, "adamw_small")]

    new_big = {nm: _adamw(big[nm][0], grads_big[nm], big[nm][1], big[nm][2], f"adamw_{nm}", pass_g=True)
               for nm in big}

    def big_out(nm, which):
        ref_shape = {"wk": w_k.shape, "wv": w_v.shape}.get(nm, big[nm][0].shape)
        return new_big[nm][which].reshape(ref_shape)

    order = ["pre_mix", "post_mix", "pre_ffn", "post_ffn", "win", "vg", "ws", "bs", "wout", "kv", "wk", "wv", "wq",
             "wo", "wup", "cw", "cb", "wdn"]
    small_at = {"pre_mix": 0, "post_mix": 1, "pre_ffn": 2, "post_ffn": 3, "vg": 4, "ws": 5, "bs": 6, "kv": 7,
                "cw": 8, "cb": 9}
    outs = [loss, dx[None]]
    for which in range(4):
        for nm in order:
            if nm in small_at:
                outs.append(small[small_at[nm]][1] if which == 0 else small_new[which - 1][small_at[nm]])
            else:
                outs.append(big_out(nm, which))
    return tuple(outs)
```

```python
import functools
import math

import jax
import jax.numpy as jnp
from jax import lax
from jax.experimental import pallas as pl
from jax.experimental.pallas import tpu as pltpu

F32 = jnp.float32
BF16 = jnp.bfloat16
EPS = 1e-6
ADAM_LR = 0.001
ADAM_B1 = 0.9
ADAM_B2 = 0.999
ADAM_EPS = 1e-08
ADAM_WD = 0.01
ADAM_STEP = 10

LANE = 128
SUBLANE = 8
ROWS = 16
TILE = 128
N_CHIPS = 4
N_DEV = 8
VMEM_LIMIT = 56 * 1024 * 1024
MM_VMEM = 40 * 1024 * 1024
MESH = pl.DeviceIdType.MESH
ANY = pl.BlockSpec(memory_space=pl.ANY)
VMEM_SPEC = pl.BlockSpec(memory_space=pltpu.VMEM)


def _cp(*sem):
    return pltpu.CompilerParams(dimension_semantics=sem, vmem_limit_bytes=VMEM_LIMIT)


def _pick(dim, pref, align=LANE):
    if dim <= pref:
        return dim
    best = None
    for d in range(align, pref + 1, align):
        if dim % d == 0:
            best = d
    assert best is not None, (dim, pref)
    return best


_DIMS = {
    "nn": (((1,), (0,)), ((), ())),
    "nt": (((1,), (1,)), ((), ())),
    "tn": (((0,), (0,)), ((), ())),
}


def _as3(a):
    return a if a.ndim == 3 else a[None]


def _spec3(br, bc, cols_j, rc):
    per = cols_j // bc

    def imap(m, n, k):
        r, c = rc(m, n, k)
        return (c // per, r, c % per)

    return pl.BlockSpec((None, br, bc), imap)


def _mm(a, b, mode, name, out_dtype=F32, out_split=1, rides=()):
    a, b = _as3(a), _as3(b)
    ja, ra, caj = a.shape
    jb, rb, cbj = b.shape
    if mode == "nn":
        m, k, n = ra, ja * caj, jb * cbj
        assert rb == k
        m_ext, k_ext, n_ext = [ra], [caj, rb], [cbj]
    elif mode == "nt":
        m, k, n = ra, ja * caj, rb
        assert jb * cbj == k
        m_ext, k_ext, n_ext = [ra], [caj, cbj], [rb]
    else:
        m, k, n = ja * caj, ra, jb * cbj
        assert rb == k
        m_ext, k_ext, n_ext = [caj], [ra], [cbj]
    assert n % out_split == 0
    n_ext.append(n // out_split)
    bm = _pick(math.gcd(*m_ext), 1536)
    bn = _pick(math.gcd(*n_ext), 1536)
    k_unit = math.gcd(*k_ext)
    o_bytes = jnp.dtype(out_dtype).itemsize

    def vmem_need(bk):
        tiles = bm * bk * a.dtype.itemsize + bk * bn * b.dtype.itemsize + bm * bn * o_bytes
        return 2 * tiles + (bm * bn * 4 if bk < k else 0)

    bk = max(d for d in range(LANE, k_unit + 1, LANE) if k_unit % d == 0 and (d == LANE or vmem_need(d) <= MM_VMEM))
    nk = k // bk
    if mode == "nn":
        a_spec = _spec3(bm, bk, caj, lambda mi, ni, ki: (mi, ki))
        b_spec = _spec3(bk, bn, cbj, lambda mi, ni, ki: (ki, ni))
    elif mode == "nt":
        a_spec = _spec3(bm, bk, caj, lambda mi, ni, ki: (mi, ki))
        b_spec = _spec3(bn, bk, cbj, lambda mi, ni, ki: (ni, ki))
    else:
        a_spec = _spec3(bk, bm, caj, lambda mi, ni, ki: (ki, mi))
        b_spec = _spec3(bk, bn, cbj, lambda mi, ni, ki: (ki, ni))
    o_spec = _spec3(bm, bn, n // out_split, lambda mi, ni, ki: (mi, ni))
    dims = _DIMS[mode]

    def body(a_ref, b_ref, o_ref, *acc):
        def part():
            return lax.dot_general(a_ref[...].astype(BF16), b_ref[...].astype(BF16), dims, preferred_element_type=F32)

        if nk == 1:
            o_ref[...] = part().astype(o_ref.dtype)
            return
        acc_ref, = acc
        ki = pl.program_id(2)

        @pl.when(ki == 0)
        def _():
            acc_ref[...] = part()

        @pl.when(jnp.logical_and(ki > 0, ki < nk - 1))
        def _():
            acc_ref[...] += part()

        @pl.when(ki == nk - 1)
        def _():
            o_ref[...] = (acc_ref[...] + part()).astype(o_ref.dtype)

    out, rode = _hosted_call(
        body, [a, b], name=name, grid=(m // bm, n // bn, nk), in_specs=[a_spec, b_spec], out_specs=o_spec,
        out_shape=jax.ShapeDtypeStruct((out_split, m, n // out_split), out_dtype),
        scratch_shapes=[pltpu.VMEM((bm, bn), F32)] if nk > 1 else [],
        semantics=("parallel", "parallel", "arbitrary"), rides=rides)
    return (out, rode) if rides else out


def _rms(x, g):
    r = lax.rsqrt(jnp.mean(x * x, axis=-1, keepdims=True) + EPS)
    return x * r * g


def _rms_bwd(x, g, dy):
    r = lax.rsqrt(jnp.mean(x * x, axis=-1, keepdims=True) + EPS)
    xh = x * r
    gy = dy * g
    dx = r * (gy - xh * jnp.mean(gy * xh, axis=-1, keepdims=True))
    return dx, jnp.sum(dy * xh, axis=0, keepdims=True)


def _row_block(s):
    return _pick(s, 256, ROWS)


def _rms_fwd(h, g, name):
    s, d = h.shape
    br = _row_block(s)

    def body(h_ref, g_ref, o_ref):
        o_ref[...] = _rms(h_ref[...], g_ref[...]).astype(BF16)

    row = pl.BlockSpec((br, d), lambda i: (i, 0))
    vec = pl.BlockSpec((1, d), lambda i: (0, 0))
    return pl.pallas_call(
        body, name=name, grid=(s // br,), in_specs=[row, vec], out_specs=row,
        out_shape=jax.ShapeDtypeStruct((s, d), BF16), compiler_params=_cp("parallel"),
    )(h, g)


def _resid_rms(h_in, f, g_post, g_next, name, rides=()):
    s, d = h_in.shape
    br = _row_block(s)
    n_next = len(g_next)

    def body(h_ref, f_ref, gp_ref, *refs):
        gn_refs, ho_ref, hn_refs = refs[:n_next], refs[n_next], refs[n_next + 1:]
        h = h_ref[...] + _rms(f_ref[...], gp_ref[...])
        ho_ref[...] = h
        for gn_ref, hn_ref in zip(gn_refs, hn_refs):
            hn_ref[...] = _rms(h, gn_ref[...]).astype(BF16)

    row = pl.BlockSpec((br, d), lambda i: (i, 0))
    vec = pl.BlockSpec((1, d), lambda i: (0, 0))
    outs, rode = _hosted_call(
        body, [h_in, f, g_post, *g_next], name=name, grid=(s // br,),
        in_specs=[row, row, vec] + [vec] * n_next,
        out_specs=[row] * (1 + n_next),
        out_shape=[jax.ShapeDtypeStruct((s, d), F32)] + [jax.ShapeDtypeStruct((s, d), BF16)] * n_next,
        semantics=("parallel",), rides=rides)
    return (outs, rode) if rides else outs


def _loss_head(h_in, f, g_post, target, name):
    s, d = h_in.shape
    br = _row_block(s)

    def body(h_ref, f_ref, gp_ref, t_ref, dh_ref, loss_ref):
        @pl.when(pl.program_id(0) == 0)
        def _():
            loss_ref[...] = jnp.zeros_like(loss_ref)

        diff = h_ref[...] + _rms(f_ref[...], gp_ref[...]) - t_ref[...]
        dh_ref[...] = diff * (1.0 / d)
        loss_ref[...] += 0.5 * jnp.sum(jnp.mean(diff * diff, axis=-1, keepdims=True))

    row = pl.BlockSpec((br, d), lambda i: (i, 0))
    vec = pl.BlockSpec((1, d), lambda i: (0, 0))
    return pl.pallas_call(
        body, name=name, grid=(s // br,),
        in_specs=[row, row, vec, row],
        out_specs=[row, pl.BlockSpec((SUBLANE, LANE), lambda i: (0, 0))],
        out_shape=[jax.ShapeDtypeStruct((s, d), F32), jax.ShapeDtypeStruct((SUBLANE, LANE), F32)],
        compiler_params=_cp("arbitrary"),
    )(h_in, f, g_post, target)


def _rms_bwd_out(dy, f, g, name, rides=()):
    s, d = f.shape
    br = _row_block(s)

    def body(dy_ref, f_ref, g_ref, df_ref, dg_ref):
        @pl.when(pl.program_id(0) == 0)
        def _():
            dg_ref[...] = jnp.zeros_like(dg_ref)

        dx, dg = _rms_bwd(f_ref[...], g_ref[...], dy_ref[...])
        df_ref[...] = dx.astype(BF16)
        dg_ref[...] += dg

    row = pl.BlockSpec((br, d), lambda i: (i, 0))
    vec = pl.BlockSpec((1, d), lambda i: (0, 0))
    (df, dg), rode = _hosted_call(
        body, [dy, f, g], name=name, grid=(s // br,), in_specs=[row, row, vec], out_specs=[row, vec],
        out_shape=[jax.ShapeDtypeStruct((s, d), BF16), jax.ShapeDtypeStruct((1, d), F32)],
        semantics=("arbitrary",), rides=rides)
    return (df, dg, rode) if rides else (df, dg)


def _rms_bwd_in(dh_out, h_in, branches, name, rides=()):
    s, d = h_in.shape
    br = _row_block(s)
    counts = [len(ds) for ds, _ in branches]
    n_d = sum(counts)
    n_b = len(branches)

    def body(dho_ref, h_ref, *refs):
        d_refs, g_refs = refs[:n_d], refs[n_d:n_d + n_b]
        dh_ref, dg_refs = refs[n_d + n_b], refs[n_d + n_b + 1:]

        @pl.when(pl.program_id(0) == 0)
        def _():
            for r in dg_refs:
                r[...] = jnp.zeros_like(r)

        h = h_ref[...]
        acc = dho_ref[...]
        at = 0
        for bi, cnt in enumerate(counts):
            dn = d_refs[at][...]
            for r in d_refs[at + 1:at + cnt]:
                dn = dn + r[...]
            at += cnt
            dx, dg = _rms_bwd(h, g_refs[bi][...], dn)
            acc = acc + dx
            dg_refs[bi][...] += dg
        dh_ref[...] = acc

    row = pl.BlockSpec((br, d), lambda i: (i, 0))
    vec = pl.BlockSpec((1, d), lambda i: (0, 0))
    flat_d = [x for ds, _ in branches for x in ds]
    outs, rode = _hosted_call(
        body, [dh_out, h_in, *flat_d, *[g for _, g in branches]], name=name, grid=(s // br,),
        in_specs=[row, row] + [row] * n_d + [vec] * n_b,
        out_specs=[row] + [vec] * n_b,
        out_shape=[jax.ShapeDtypeStruct((s, d), F32)] + [jax.ShapeDtypeStruct((1, d), F32)] * n_b,
        semantics=("arbitrary",), rides=rides)
    return (outs[0], list(outs[1:]), rode) if rides else (outs[0], list(outs[1:]))


def _split3(x):
    x0 = x.astype(BF16)
    r1 = x - x0.astype(F32)
    x1 = r1.astype(BF16)
    x2 = (r1 - x1.astype(F32)).astype(BF16)
    return x0, x1, x2


def _tri(n, kind):
    r = lax.broadcasted_iota(jnp.int32, (n, n), 0)
    c = lax.broadcasted_iota(jnp.int32, (n, n), 1)
    m = {"lt": r < c, "le": r <= c, "gt": r > c}[kind]
    return jnp.where(m, 1.0, 0.0).astype(BF16)


_GELU_C = math.sqrt(2.0 / math.pi)
_GELU_A = 0.044715


def _gelu(x):
    return 0.5 * x * (1.0 + jnp.tanh(_GELU_C * (x + _GELU_A * (x * x * x))))


def _gelu_grad(x):
    t = jnp.tanh(_GELU_C * (x + _GELU_A * (x * x * x)))
    return 0.5 * (1.0 + t) + 0.5 * x * (1.0 - t * t) * (_GELU_C * (1.0 + 3.0 * _GELU_A * (x * x)))


def _causal_w(w):
    r = lax.broadcasted_iota(jnp.int32, (TILE, TILE), 0)
    c = lax.broadcasted_iota(jnp.int32, (TILE, TILE), 1)
    return jnp.where(c <= r, w, 0.0)


def _uv_tiles(uv_ref, g, d_a, dq):
    cu, cv = g * TILE, d_a + g * TILE
    u = uv_ref[cu // dq, :, pl.ds(cu % dq, TILE)]
    v = uv_ref[cv // dq, :, pl.ds(cv % dq, TILE)]
    return u, v


def _gmlp_fwd(uv, v_g, w_s, bias, name, rides=()):
    _, s, dq = uv.shape
    d_a = 2 * dq
    n_g = d_a // TILE

    def body(uv_ref, vg_ref, ws_ref, b_ref, o_ref):
        for g in range(n_g):
            up, vp = _uv_tiles(uv_ref, g, d_a, dq)
            cols = pl.ds(g * TILE, TILE)
            vn = _rms(_gelu(vp), vg_ref[:, cols])
            mixed = jnp.dot(_causal_w(ws_ref[g]).astype(BF16), vn.astype(BF16), preferred_element_type=F32) + b_ref[:, cols]
            o_ref[:, cols] = (_gelu(up) * mixed).astype(BF16)

    return _hosted_call(
        body, [uv, v_g, w_s, bias], name=name, grid=(s // TILE,),
        in_specs=[
            pl.BlockSpec((4, TILE, dq), lambda i: (0, i, 0)),
            pl.BlockSpec((1, d_a), lambda i: (0, 0)),
            pl.BlockSpec((n_g, TILE, TILE), lambda i: (0, 0, 0)),
            pl.BlockSpec((TILE, d_a), lambda i: (0, 0)),
        ],
        out_specs=pl.BlockSpec((TILE, d_a), lambda i: (i, 0)),
        out_shape=jax.ShapeDtypeStruct((s, d_a), BF16),
        semantics=("parallel",), rides=rides)


def _gmlp_bwd(uv, dgm, v_g, w_s, bias, name, rides=()):
    _, s, dq = uv.shape
    d_a = 2 * dq
    n_g = d_a // TILE
    n_c = s // TILE

    def body(uv_ref, d_ref, vg_ref, ws_ref, b_ref, duv_ref, dws_ref, dbs_ref, dvg_ref, dbias_acc):
        i = pl.program_id(0)

        @pl.when(i == 0)
        def _():
            dws_ref[...] = jnp.zeros_like(dws_ref)
            dvg_ref[...] = jnp.zeros_like(dvg_ref)
            dbias_acc[...] = jnp.zeros_like(dbias_acc)

        for g in range(n_g):
            up, vp = _uv_tiles(uv_ref, g, d_a, dq)
            cols = pl.ds(g * TILE, TILE)
            vg = vg_ref[:, cols]
            u = _gelu(up)
            v = _gelu(vp)
            r = lax.rsqrt(jnp.mean(v * v, axis=-1, keepdims=True) + EPS)
            vh = v * r
            vn = (vh * vg).astype(BF16)
            wc = _causal_w(ws_ref[g]).astype(BF16)
            mixed = jnp.dot(wc, vn, preferred_element_type=F32) + b_ref[:, cols]
            d_out = d_ref[:, cols]
            du = d_out * mixed
            dmixed = d_out * u
            dmb = dmixed.astype(BF16)
            dvn = lax.dot_general(wc, dmb, _DIMS["tn"], preferred_element_type=F32)
            dws_ref[g] += lax.dot_general(dmb, vn, _DIMS["nt"], preferred_element_type=F32)
            dbias_acc[:, cols] += dmixed
            dvg_ref[:, cols] += jnp.sum(dvn * vh, axis=0, keepdims=True)
            gv = dvn * vg
            dv = r * (gv - vh * jnp.mean(gv * vh, axis=-1, keepdims=True))
            cu, cv = g * TILE, d_a + g * TILE
            duv_ref[cu // dq, :, pl.ds(cu % dq, TILE)] = (du * _gelu_grad(up)).astype(BF16)
            duv_ref[cv // dq, :, pl.ds(cv % dq, TILE)] = (dv * _gelu_grad(vp)).astype(BF16)

        @pl.when(i == n_c - 1)
        def _():
            ones = jnp.ones((SUBLANE, TILE), BF16)
            for g in range(n_g):
                dws_ref[g] = _causal_w(dws_ref[g])
                cols = pl.ds(g * TILE, TILE)
                out = None
                for t in _split3(dbias_acc[:, cols]):
                    p = lax.dot_general(ones, t, _DIMS["nt"], preferred_element_type=F32)
                    out = p if out is None else out + p
                dbs_ref[pl.ds(g * SUBLANE, SUBLANE), :] = out

    return _hosted_call(
        body, [uv, dgm, v_g, w_s, bias], name=name, grid=(n_c,), semantics=("arbitrary",), rides=rides,
        in_specs=[
            pl.BlockSpec((4, TILE, dq), lambda i: (0, i, 0)),
            pl.BlockSpec((TILE, d_a), lambda i: (i, 0)),
            pl.BlockSpec((1, d_a), lambda i: (0, 0)),
            pl.BlockSpec((n_g, TILE, TILE), lambda i: (0, 0, 0)),
            pl.BlockSpec((TILE, d_a), lambda i: (0, 0)),
        ],
        out_specs=[
            pl.BlockSpec((4, TILE, dq), lambda i: (0, i, 0)),
            pl.BlockSpec((n_g, TILE, TILE), lambda i: (0, 0, 0)),
            pl.BlockSpec((n_g * SUBLANE, TILE), lambda i: (0, 0)),
            pl.BlockSpec((1, d_a), lambda i: (0, 0)),
        ],
        out_shape=[
            jax.ShapeDtypeStruct((4, s, dq), BF16),
            jax.ShapeDtypeStruct((n_g, TILE, TILE), F32),
            jax.ShapeDtypeStruct((n_g * SUBLANE, TILE), F32),
            jax.ShapeDtypeStruct((1, d_a), F32),
        ],
        scratch_shapes=[pltpu.VMEM((TILE, d_a), F32)])


def _sigmoid(x):
    return 1.0 / (1.0 + jnp.exp(-x))


def _conv3(ext, w, b):
    return b + ((w[0:1] * pltpu.roll(ext, 2, 0) + w[1:2] * pltpu.roll(ext, 1, 0)) + w[2:3] * ext)


def _act_blocks(s, ns):
    return _pick(s, 512, ROWS), _pick(ns, 256)


def _ffn_act_fwd(a, cw, cb, name, rides=()):
    _, s, ns = a.shape
    bs, cb_w = _act_blocks(s, ns)
    hb = bs // SUBLANE

    def body(a_ref, prev_ref, cw_ref, cb_ref, o_ref):
        first = pl.program_id(0) == 0

        def conv(comp):
            prev = jnp.where(first, 0.0, prev_ref[comp])
            ext = jnp.concatenate([prev, a_ref[comp]], axis=0)
            return _conv3(ext, cw_ref[comp], cb_ref[comp])[SUBLANE:]

        for p in range(2):
            cg = conv(p)
            o_ref[p] = (cg * _sigmoid(cg) * conv(2 + p)).astype(BF16)

    hm, rode = _hosted_call(
        body, [a, a, cw, cb], name=name, grid=(s // bs, ns // cb_w),
        in_specs=[
            pl.BlockSpec((4, bs, cb_w), lambda i, j: (0, i, j)),
            pl.BlockSpec((4, SUBLANE, cb_w), lambda i, j: (0, jnp.maximum(i * hb - 1, 0), j)),
            pl.BlockSpec((4, 3, cb_w), lambda i, j: (0, 0, j)),
            pl.BlockSpec((4, 1, cb_w), lambda i, j: (0, 0, j)),
        ],
        out_specs=pl.BlockSpec((2, bs, cb_w), lambda i, j: (0, i, j)),
        out_shape=jax.ShapeDtypeStruct((2, s, ns), BF16),
        semantics=("parallel", "parallel"), rides=rides)
    return (hm, rode) if rides else hm


def _ffn_act_bwd(a, dhm, cw, cb, name, rides=()):
    _, s, ns = a.shape
    bs, cb_w = _act_blocks(s, ns)
    hb = bs // SUBLANE
    n_i = s // bs
    n_ext = bs + 2 * SUBLANE
    cur = slice(SUBLANE, SUBLANE + bs)

    def body(a_ref, prev_ref, next_ref, d_ref, dnext_ref, cw_ref, cb_ref, da_ref, dcw_ref, dcb_ref):
        i = pl.program_id(1)
        first, last = i == 0, i == n_i - 1

        @pl.when(first)
        def _():
            dcw_ref[...] = jnp.zeros_like(dcw_ref)
            dcb_ref[...] = jnp.zeros_like(dcb_ref)

        def ext_of(comp):
            return jnp.concatenate([jnp.where(first, 0.0, prev_ref[comp]), a_ref[comp], next_ref[comp]], axis=0)

        def back(comp, a_ext, dc):
            w = cw_ref[comp]
            da = (w[2:3] * dc + w[1:2] * pltpu.roll(dc, n_ext - 1, 0)) + w[0:1] * pltpu.roll(dc, n_ext - 2, 0)
            da_ref[comp] = da[cur].astype(BF16)
            dcc = dc[cur]
            dcw_ref[comp, 0:1, :] += jnp.sum(dcc * pltpu.roll(a_ext, 2, 0)[cur], axis=0, keepdims=True)
            dcw_ref[comp, 1:2, :] += jnp.sum(dcc * pltpu.roll(a_ext, 1, 0)[cur], axis=0, keepdims=True)
            dcw_ref[comp, 2:3, :] += jnp.sum(dcc * a_ext[cur], axis=0, keepdims=True)
            dcb_ref[comp] += jnp.sum(dcc, axis=0, keepdims=True)

        for p in range(2):
            ag, av = ext_of(p), ext_of(2 + p)
            cg = _conv3(ag, cw_ref[p], cb_ref[p])
            cv = _conv3(av, cw_ref[2 + p], cb_ref[2 + p])
            d = jnp.concatenate(
                [jnp.zeros((SUBLANE, cb_w), F32), d_ref[p], jnp.where(last, 0.0, dnext_ref[p])], axis=0)
            sg = _sigmoid(cg)
            back(2 + p, av, d * (cg * sg))
            back(p, ag, d * cv * (sg * (1.0 + cg * (1.0 - sg))))

    return _hosted_call(
        body, [a, a, a, dhm, dhm, cw, cb], name=name, grid=(ns // cb_w, n_i),
        in_specs=[
            pl.BlockSpec((4, bs, cb_w), lambda j, i: (0, i, j)),
            pl.BlockSpec((4, SUBLANE, cb_w), lambda j, i: (0, jnp.maximum(i * hb - 1, 0), j)),
            pl.BlockSpec((4, SUBLANE, cb_w), lambda j, i: (0, jnp.minimum((i + 1) * hb, n_i * hb - 1), j)),
            pl.BlockSpec((2, bs, cb_w), lambda j, i: (0, i, j)),
            pl.BlockSpec((2, SUBLANE, cb_w), lambda j, i: (0, jnp.minimum((i + 1) * hb, n_i * hb - 1), j)),
            pl.BlockSpec((4, 3, cb_w), lambda j, i: (0, 0, j)),
            pl.BlockSpec((4, 1, cb_w), lambda j, i: (0, 0, j)),
        ],
        out_specs=[
            pl.BlockSpec((4, bs, cb_w), lambda j, i: (0, i, j)),
            pl.BlockSpec((4, 3, cb_w), lambda j, i: (0, 0, j)),
            pl.BlockSpec((4, 1, cb_w), lambda j, i: (0, 0, j)),
        ],
        out_shape=[
            jax.ShapeDtypeStruct((4, s, ns), BF16),
            jax.ShapeDtypeStruct((4, 3, ns), F32),
            jax.ShapeDtypeStruct((4, 1, ns), F32),
        ],
        semantics=("parallel", "arbitrary"), rides=rides)


ATT_BQ_FWD = 2048
ATT_BQ_BWD = 1024
ATT_BK = 256
ATT_UNROLL = 2
ATT_UNROLL_BWD = 4


def _att_blocks(s, bq_pref):
    bq = _pick(s, bq_pref)
    bk = min(ATT_BK, bq)
    assert bq % bk == 0
    return bq, bk


def _dot_sel2(x, sel):
    hi = x.astype(BF16)
    lo = (x - hi.astype(F32)).astype(BF16)
    n = x.shape[0]
    both = jnp.dot(jnp.concatenate([hi, lo], axis=0), sel, preferred_element_type=F32)
    return both[:n] + both[n:]


def _causal_mask(bq, bk, row0, col0):
    rows = row0 + lax.broadcasted_iota(jnp.int32, (bq, bk), 0)
    cols = col0 + lax.broadcasted_iota(jnp.int32, (bq, bk), 1)
    return cols < rows


def _sb_tile(qb, kb, scale, mask):
    z = lax.dot_general(qb, kb, _DIMS["nt"], preferred_element_type=F32) * scale
    e = jnp.exp(-jnp.abs(z))
    lb = jnp.minimum(z, 0.0) - jnp.log(1.0 + e)
    l1m = lb - z
    if mask is not None:
        l1m = jnp.where(mask, l1m, 0.0)
    return z, e, lb, l1m


def _attn_fwd(q, k, v, name, rides=()):
    s, hd = q.shape
    bq, bk = _att_blocks(s, ATT_BQ_FWD)
    r = bq // bk
    unroll = math.gcd(r, ATT_UNROLL)
    n_h, n_q = hd // TILE, s // bq
    scale = 1.0 / math.sqrt(TILE)

    def body(q_ref, k_ref, v_ref, o_ref, l_ref, acc_ref, suf_ref):
        i = pl.program_id(1)
        qb = q_ref[...]
        later = _tri(bk, "gt")
        acc_ref[...] = jnp.zeros_like(acc_ref)
        suf_ref[...] = jnp.zeros_like(suf_ref)

        def tile(j, row0):
            rows = pl.ds(pl.multiple_of(j * bk, bk), bk)
            masked = row0 is not None
            r0 = row0 if masked else 0
            rs = pl.ds(r0, bq - r0)
            mask = _causal_mask(bq - r0, bk, i * bq + r0, j * bk) if masked else None
            _, _, lb, l1m = _sb_tile(qb[r0:], k_ref[rows, :], scale, mask)
            a = jnp.exp(lb + _dot_sel2(l1m, later) + suf_ref[rs, :])
            if masked:
                a = jnp.where(mask, a, 0.0)
            acc_ref[rs, :] += jnp.dot(a.astype(BF16), v_ref[rows, :], preferred_element_type=F32)
            suf_ref[rs, :] += jnp.sum(l1m, axis=1, keepdims=True)

        for dgl in range(r - 1, -1, -1):
            tile(r * i + dgl, dgl * bk)

        def step(t, carry):
            for u in range(unroll):
                tile(r * i - 1 - (unroll * t + u), None)
            return carry

        lax.fori_loop(0, (r * i) // unroll, step, 0)
        o_ref[...] = acc_ref[...].astype(BF16)
        l_ref[...] = jnp.broadcast_to(suf_ref[...], (bq, TILE))

    blk = pl.BlockSpec((bq, TILE), lambda h, i: (i, h))
    head = pl.BlockSpec((s, TILE), lambda h, i: (0, h))
    return _hosted_call(
        body, [q, k, v], name=name, grid=(n_h, n_q), in_specs=[blk, head, head], out_specs=[blk, blk],
        out_shape=[jax.ShapeDtypeStruct((s, hd), BF16), jax.ShapeDtypeStruct((s, hd), F32)],
        scratch_shapes=[pltpu.VMEM((bq, TILE), F32), pltpu.VMEM((bq, 1), F32)],
        semantics=("parallel", "parallel"), rides=rides)


def _attn_bwd(q, k, v, do, lsum, name, rides=()):
    s, hd = q.shape
    bq, bk = _att_blocks(s, ATT_BQ_BWD)
    r = bq // bk
    unroll = math.gcd(r, ATT_UNROLL_BWD)
    n_h, n_q = hd // TILE, s // bq
    scale = 1.0 / math.sqrt(TILE)

    def body(q_ref, k_ref, v_ref, do_ref, l_ref, dq_ref, dk_ref, dv_ref, dq_acc, pre_ref, cp_ref):
        i = pl.program_id(1)

        @pl.when(i == 0)
        def _():
            dk_ref[...] = jnp.zeros_like(dk_ref)
            dv_ref[...] = jnp.zeros_like(dv_ref)

        qb = q_ref[...]
        dob = do_ref[...]
        upto = _tri(bk, "le")
        before = _tri(bk, "lt")
        dq_acc[...] = jnp.zeros_like(dq_acc)
        pre_ref[...] = jnp.zeros_like(pre_ref)
        cp_ref[...] = jnp.zeros_like(cp_ref)

        def tile(j, row0):
            rows = pl.ds(pl.multiple_of(j * bk, bk), bk)
            kb, vb = k_ref[rows, :], v_ref[rows, :]
            masked = row0 is not None
            r0 = row0 if masked else 0
            rs = pl.ds(r0, bq - r0)
            qs, dos = qb[r0:], dob[r0:]
            mask = _causal_mask(bq - r0, bk, i * bq + r0, j * bk) if masked else None
            z, e, lb, l1m = _sb_tile(qs, kb, scale, mask)
            suffix = (l_ref[rs, 0:1] - pre_ref[rs, :]) - _dot_sel2(l1m, upto)
            a = jnp.exp(lb + suffix)
            if masked:
                a = jnp.where(mask, a, 0.0)
            p = a * lax.dot_general(dos, vb, _DIMS["nt"], preferred_element_type=F32)
            both = p + (cp_ref[rs, :] + jnp.dot(p.astype(BF16), before, preferred_element_type=F32))
            sg = jnp.where(z >= 0.0, 1.0, e) * pl.reciprocal(1.0 + e, approx=True)
            dz = p - both * sg
            if masked:
                dz = jnp.where(mask, dz, 0.0)
            dz = (dz * scale).astype(BF16)
            dq_acc[rs, :] += jnp.dot(dz, kb, preferred_element_type=F32)
            dk_ref[rows, :] += lax.dot_general(dz, qs, _DIMS["tn"], preferred_element_type=F32)
            dv_ref[rows, :] += lax.dot_general(a.astype(BF16), dos, _DIMS["tn"], preferred_element_type=F32)
            pre_ref[rs, :] += jnp.sum(l1m, axis=1, keepdims=True)
            cp_ref[rs, :] += jnp.sum(p, axis=1, keepdims=True)

        def step(j, carry):
            for u in range(unroll):
                tile(unroll * j + u, None)
            return carry

        lax.fori_loop(0, (r * i) // unroll, step, 0)
        for dgl in range(r):
            tile(r * i + dgl, dgl * bk)
        dq_ref[...] = dq_acc[...].astype(BF16)

    blk = pl.BlockSpec((bq, TILE), lambda h, i: (i, h))
    head = pl.BlockSpec((s, TILE), lambda h, i: (0, h))
    return _hosted_call(
        body, [q, k, v, do, lsum], name=name, grid=(n_h, n_q), in_specs=[blk, head, head, blk, blk],
        out_specs=[blk, head, head],
        out_shape=[jax.ShapeDtypeStruct((s, hd), BF16), jax.ShapeDtypeStruct((s, hd), F32),
                   jax.ShapeDtypeStruct((s, hd), F32)],
        scratch_shapes=[pltpu.VMEM((bq, TILE), F32), pltpu.VMEM((bq, 1), F32), pltpu.VMEM((bq, 1), F32)],
        semantics=("parallel", "arbitrary"), rides=rides)


EW_BLOCK = 512 * 1024


def _ew_blocks(r, c, elems=EW_BLOCK):
    return _pick(r, max(ROWS, elems // c // ROWS * ROWS), ROWS), c


def _cast_bf16(w, layer, chip_idx, name):
    _, r, c = w.shape
    br, bc = _ew_blocks(r, c)

    def body(chip_ref, w_ref, o_ref):
        o_ref[...] = w_ref[...].astype(BF16)

    return pl.pallas_call(
        body, name=name,
        grid_spec=pltpu.PrefetchScalarGridSpec(
            num_scalar_prefetch=1, grid=(r // br, c // bc),
            in_specs=[pl.BlockSpec((None, br, bc), lambda i, j, chip_ref: (layer, i, j))],
            out_specs=pl.BlockSpec((None, br, bc), lambda i, j, chip_ref: (chip_ref[0], i, j)),
        ),
        out_shape=jax.ShapeDtypeStruct((N_CHIPS, r, c), BF16), compiler_params=_cp("parallel", "parallel"),
    )(chip_idx, w)


def _pair_add(dw, recv, c_idx, name):
    _, r, c = dw.shape
    hr = r // 2
    br, bc = _ew_blocks(hr, c)
    nb = hr // br

    def body(c_ref, a_ref, b_ref, o_ref):
        o_ref[...] = (a_ref[...].astype(F32) + b_ref[...].astype(F32)).astype(BF16)

    return pl.pallas_call(
        body, name=name,
        grid_spec=pltpu.PrefetchScalarGridSpec(
            num_scalar_prefetch=1, grid=(N_CHIPS, nb, c // bc),
            in_specs=[
                pl.BlockSpec((None, br, bc), lambda s, i, j, c_ref: (s, c_ref[0] * nb + i, j)),
                pl.BlockSpec((None, br, bc), lambda s, i, j, c_ref: (s, i, j)),
            ],
            out_specs=pl.BlockSpec((None, br, bc), lambda s, i, j, c_ref: (s, i, j)),
        ),
        out_shape=jax.ShapeDtypeStruct((N_CHIPS, hr, c), BF16),
        compiler_params=_cp("parallel", "parallel", "parallel"),
    )(c_idx, dw, recv)


def _chip_sum(parts, dest, shape, layer, c_idx, name):
    _, hr, c = parts.shape
    br, bc = _ew_blocks(hr, c, EW_BLOCK // 2)
    nb = hr // br

    def body(c_ref, p_ref, *refs):
        o_ref = refs[-1]
        acc = p_ref[0].astype(F32)
        for s in range(1, N_CHIPS):
            acc = acc + p_ref[s].astype(F32)
        o_ref[...] = acc

    in_specs = [pl.BlockSpec((N_CHIPS, br, bc), lambda i, j, c_ref: (0, i, j))]
    operands = [c_idx, parts]
    aliases = {}
    if dest is not None:
        in_specs.append(ANY)
        operands.append(dest)
        aliases = {2: 0}
    return pl.pallas_call(
        body, name=name,
        grid_spec=pltpu.PrefetchScalarGridSpec(
            num_scalar_prefetch=1, grid=(nb, c // bc), in_specs=in_specs,
            out_specs=pl.BlockSpec((None, br, bc), lambda i, j, c_ref: (layer, c_ref[0] * nb + i, j)),
        ),
        out_shape=jax.ShapeDtypeStruct(shape, F32), input_output_aliases=aliases,
        compiler_params=_cp("parallel", "parallel"),
    )(*operands)


def _adamw(w, g, m, v, name, pass_g=False):
    n_l, r, c = w.shape
    br, bc = _ew_blocks(r, c, EW_BLOCK // 2)

    def body(w_ref, g_ref, m_ref, v_ref, *out_refs):
        d_ref, mo_ref, vo_ref = out_refs[-3:]
        g = g_ref[...]
        if pass_g:
            out_refs[0][...] = g
        m = ADAM_B1 * m_ref[...] + (1.0 - ADAM_B1) * g
        v = ADAM_B2 * v_ref[...] + (1.0 - ADAM_B2) * (g * g)
        m_hat = m / (1.0 - ADAM_B1 ** ADAM_STEP)
        v_hat = v / (1.0 - ADAM_B2 ** ADAM_STEP)
        d_ref[...] = -ADAM_LR * (m_hat / (jnp.sqrt(v_hat) + ADAM_EPS) + ADAM_WD * w_ref[...])
        mo_ref[...] = m
        vo_ref[...] = v

    blk = pl.BlockSpec((None, br, bc), lambda l, i, j: (l, i, j))
    n_out = 4 if pass_g else 3
    return pl.pallas_call(
        body, name=name, grid=(n_l, r // br, c // bc), in_specs=[blk] * 4, out_specs=[blk] * n_out,
        out_shape=[jax.ShapeDtypeStruct(w.shape, F32)] * n_out,
        compiler_params=_cp("parallel", "parallel", "parallel"),
    )(w, g, m, v)


def _place():
    x, y, c = lax.axis_index("x"), lax.axis_index("y"), lax.axis_index("c")
    chips = [(1 - x, y), (x, 1 - y), (1 - x, 1 - y)]
    return x, y, c, chips


class _Ride:
    def __init__(self, reads, bufs, new, n_sems, start, finish):
        self.reads, self.bufs, self.new, self.n_sems, self.start, self.finish = reads, bufs, new, n_sems, start, finish


def _hosted_call(body, operands, *, name, grid, in_specs, out_specs, out_shape, scratch_shapes=(), semantics=(), rides=()):
    single = not isinstance(out_shape, (list, tuple))
    out_specs = [out_specs] if single else list(out_specs)
    out_shape = [out_shape] if single else list(out_shape)
    in_specs, scratch_shapes = list(in_specs), list(scratch_shapes)
    n_in, n_out, n_scr = len(in_specs), len(out_shape), len(scratch_shapes)
    extra_in, extra_out, aliases, where = [], [], {}, []
    for ride in rides:
        r0 = len(extra_in)
        extra_in += list(ride.reads)
        b0 = len(extra_in)
        extra_in += list(ride.bufs)
        ob0 = len(extra_out)
        extra_out += [jax.ShapeDtypeStruct(b.shape, b.dtype) for b in ride.bufs]
        for t in range(len(ride.bufs)):
            aliases[n_in + b0 + t] = n_out + ob0 + t
        on0 = len(extra_out)
        extra_out += list(ride.new)
        where.append((r0, len(ride.reads), ob0, len(ride.bufs), on0, len(ride.new)))
    n_ein, n_eout = len(extra_in), len(extra_out)
    sem_shapes = [pltpu.SemaphoreType.DMA((max(1, k),)) for ride in rides for k in ride.n_sems]

    def full_body(*refs):
        ins, outs, scr = refs[:n_in + n_ein], refs[n_in + n_ein:n_in + n_ein + n_out + n_eout], refs[n_in + n_ein + n_out + n_eout:]

        def run(which):
            for idx, (ride, (r0, nr, ob0, nb, on0, nn)) in enumerate(zip(rides, where)):
                fn = ride.start if which == 0 else ride.finish
                fn(ins[n_in + r0:n_in + r0 + nr], outs[n_out + ob0:n_out + ob0 + nb], outs[n_out + on0:n_out + on0 + nn],
                   *scr[n_scr + 3 * idx:n_scr + 3 * idx + 3])

        host = lambda: body(*ins[:n_in], *outs[:n_out], *scr[:n_scr])
        if not rides:
            host()
        elif not grid:
            run(0)
            host()
            run(1)
        else:
            ids = [pl.program_id(ax) for ax in range(len(grid))]
            first = functools.reduce(jnp.logical_and, [i == 0 for i in ids])
            last = functools.reduce(jnp.logical_and, [i == g - 1 for i, g in zip(ids, grid)])
            pl.when(first)(lambda: run(0))
            host()
            pl.when(last)(lambda: run(1))

    if rides:
        params = pltpu.CompilerParams(dimension_semantics=("arbitrary",) * len(grid), vmem_limit_bytes=VMEM_LIMIT)
    else:
        params = _cp(*semantics)
    outs = pl.pallas_call(
        full_body, name=name, grid=grid,
        in_specs=in_specs + [ANY] * n_ein, out_specs=out_specs + [ANY] * n_eout,
        out_shape=out_shape + extra_out, input_output_aliases=aliases,
        scratch_shapes=scratch_shapes + sem_shapes, compiler_params=params,
    )(*operands, *extra_in)
    main = outs[0] if single else list(outs[:n_out])
    rode = [(list(outs[n_out + ob0:n_out + ob0 + nb]), list(outs[n_out + on0:n_out + on0 + nn]))
            for (_, _, ob0, nb, on0, nn) in where]
    return main, rode


def _run_rides(rides, name):
    return _hosted_call(lambda: None, [], name=name, grid=(), in_specs=[], out_specs=[], out_shape=[], rides=rides)[1]


def _ride_gather(slots, part=0, n_parts=1, span=1):
    n = len(slots)
    halves = [a.shape[1] // 2 for a in slots]
    sizes = [hr // n_parts for hr in halves]
    assert part + span <= n_parts
    for a, hr, size in zip(slots, halves, sizes):
        assert a.shape[1] == 2 * hr and hr == size * n_parts and size % ROWS == 0, a.shape

    def remote(bufs, send_sems, recv_sems, i, k, slot, core, to):
        rows = bufs[i].at[slot, pl.ds(pl.multiple_of(core * halves[i] + part * sizes[i], ROWS), span * sizes[i])]
        return pltpu.make_async_remote_copy(
            src_ref=rows, dst_ref=rows, send_sem=send_sems.at[i * 6 + k], recv_sem=recv_sems.at[i * 6 + k],
            device_id=to, device_id_type=MESH)

    def start(reads, bufs, new, send_sems, recv_sems, local_sems):
        x, y, c, chips = _place()
        for i in range(n):
            for k, (px, py) in enumerate(chips):
                remote(bufs, send_sems, recv_sems, i, k, 2 * x + y, c, (px, py, c)).start()

    def finish(reads, bufs, new, send_sems, recv_sems, local_sems):
        x, y, c, chips = _place()
        cp = functools.partial(remote, bufs, send_sems, recv_sems)
        for i in range(n):
            for k, (px, py) in enumerate(chips):
                cp(i, k, 2 * px + py, c, (x, y, c)).wait_recv()
                cp(i, 3 + k, 2 * px + py, c, (x, y, 1 - c)).start()
        for i in range(n):
            for k, (px, py) in enumerate(chips):
                cp(i, 3 + k, 2 * px + py, 1 - c, (x, y, c)).wait_recv()
        for i in range(n):
            for k, (px, py) in enumerate(chips):
                cp(i, k, 2 * x + y, c, (px, py, c)).wait_send()
                cp(i, 3 + k, 2 * px + py, c, (x, y, 1 - c)).wait_send()

    return _Ride([], slots, [], (6 * n, 6 * n, 0), start, finish)


def _ride_swap(grads):
    n = len(grads)
    halves = [a.shape[1] // 2 for a in grads]

    def copies(reads, new, send_sems, recv_sems):
        x, y, c, _ = _place()
        out = []
        for i in range(n):
            rows = pl.ds(pl.multiple_of((1 - c) * halves[i], 2 * SUBLANE), halves[i])
            out.append(pltpu.make_async_remote_copy(
                src_ref=reads[i].at[:, rows, :], dst_ref=new[i], send_sem=send_sems.at[i], recv_sem=recv_sems.at[i],
                device_id=(x, y, 1 - c), device_id_type=MESH))
        return out

    def start(reads, bufs, new, send_sems, recv_sems, local_sems):
        for cp in copies(reads, new, send_sems, recv_sems):
            cp.start()

    def finish(reads, bufs, new, send_sems, recv_sems, local_sems):
        for cp in copies(reads, new, send_sems, recv_sems):
            cp.wait()

    shapes = [jax.ShapeDtypeStruct((N_CHIPS, hr, a.shape[2]), a.dtype) for a, hr in zip(grads, halves)]
    return _Ride(grads, [], shapes, (n, n, 0), start, finish)


def _ride_scatter(parts, part=0, n_parts=1, into=None, span=1):
    n = len(parts)
    sizes = [a.shape[1] // n_parts for a in parts]
    assert part + span <= n_parts
    for a, size in zip(parts, sizes):
        assert a.shape[1] == size * n_parts and size % ROWS == 0, a.shape

    def piece(ref, i, slot):
        return ref.at[slot, pl.ds(part * sizes[i], span * sizes[i])]

    def own(reads, land, local_sems, i):
        me = 2 * lax.axis_index("x") + lax.axis_index("y")
        return pltpu.make_async_copy(piece(reads[i], i, me), piece(land[i], i, me), local_sems.at[i])

    def send(reads, land, send_sems, recv_sems, i, k):
        x, y, c, chips = _place()
        px, py = chips[k]
        return pltpu.make_async_remote_copy(
            src_ref=piece(reads[i], i, 2 * px + py), dst_ref=piece(land[i], i, 2 * x + y),
            send_sem=send_sems.at[3 * i + k], recv_sem=recv_sems.at[3 * i + k],
            device_id=(px, py, c), device_id_type=MESH)

    def start(reads, bufs, new, send_sems, recv_sems, local_sems):
        land = new if into is None else bufs
        for i in range(n):
            own(reads, land, local_sems, i).start()
            for k in range(3):
                send(reads, land, send_sems, recv_sems, i, k).start()

    def finish(reads, bufs, new, send_sems, recv_sems, local_sems):
        land = new if into is None else bufs
        x, y, c, chips = _place()
        for i in range(n):
            for k, (px, py) in enumerate(chips):
                slot = piece(land[i], i, 2 * px + py)
                pltpu.make_async_remote_copy(
                    src_ref=slot, dst_ref=slot, send_sem=send_sems.at[3 * i + k], recv_sem=recv_sems.at[3 * i + k],
                    device_id=(x, y, c), device_id_type=MESH).wait_recv()
        for i in range(n):
            for k in range(3):
                send(reads, land, send_sems, recv_sems, i, k).wait_send()
            own(reads, land, local_sems, i).wait()

    shapes = [jax.ShapeDtypeStruct(a.shape, a.dtype) for a in parts]
    if into is None:
        return _Ride(parts, [], shapes, (3 * n, 3 * n, n), start, finish)
    return _Ride(parts, list(into), [], (3 * n, 3 * n, n), start, finish)


def _ride_join(grads):
    n = len(grads)

    def copy(bufs, send_sems, recv_sems, i, core, to):
        hr = grads[i].shape[1] // 2
        rows = bufs[i].at[:, pl.ds(pl.multiple_of(core * hr, SUBLANE), hr), :]
        return pltpu.make_async_remote_copy(
            src_ref=rows, dst_ref=rows, send_sem=send_sems.at[i], recv_sem=recv_sems.at[i],
            device_id=to, device_id_type=MESH)

    def start(reads, bufs, new, send_sems, recv_sems, local_sems):
        x, y, c, _ = _place()
        for i in range(n):
            copy(bufs, send_sems, recv_sems, i, c, (x, y, 1 - c)).start()

    def finish(reads, bufs, new, send_sems, recv_sems, local_sems):
        x, y, c, _ = _place()
        for i in range(n):
            copy(bufs, send_sems, recv_sems, i, 1 - c, (x, y, c)).wait_recv()
        for i in range(n):
            copy(bufs, send_sems, recv_sems, i, c, (x, y, 1 - c)).wait_send()

    return _Ride([], grads, [], (n, n, 0), start, finish)


def _all_reduce_small(packed, name, rides=()):
    r, c = packed.shape
    chunk = _pick(r, 256, ROWS)

    def body(x_ref, out_ref, gath, send_sems, recv_sems, local_sem):
        x, y, cc, chips = _place()
        me, sibling = (x, y, cc), (x, y, 1 - cc)

        def slot(px, py, pc):
            return gath.at[4 * px + 2 * py + pc]

        def copy(k, block, to, src=None):
            return pltpu.make_async_remote_copy(
                src_ref=slot(*block) if src is None else src, dst_ref=slot(*block),
                send_sem=send_sems.at[k], recv_sem=recv_sems.at[k], device_id=to, device_id_type=MESH)

        mine = pltpu.make_async_copy(x_ref, slot(*me), local_sem)
        mine.start()
        first = [copy(0, me, sibling, src=x_ref)]
        first += [copy(1 + j, me, (*chip, cc), src=x_ref) for j, chip in enumerate(chips)]
        for cp in first:
            cp.start()
        passed = [copy(4 + j, (*chip, cc), sibling) for j, chip in enumerate(chips)]
        for j, chip in enumerate(chips):
            copy(1 + j, (*chip, cc), me).wait_recv()
            passed[j].start()
        copy(0, sibling, me).wait_recv()
        for j, chip in enumerate(chips):
            copy(4 + j, (*chip, 1 - cc), me).wait_recv()
        for cp in first + passed:
            cp.wait_send()
        mine.wait()

        def add(i, carry):
            rows = pl.ds(pl.multiple_of(i * chunk, SUBLANE), chunk)
            acc = gath[0, rows, :]
            for dev in range(1, N_DEV):
                acc = acc + gath[dev, rows, :]
            out_ref[rows, :] = acc
            return carry

        lax.fori_loop(0, r // chunk, add, 0)

    return _hosted_call(
        body, [packed], name=name, grid=(), in_specs=[VMEM_SPEC], out_specs=VMEM_SPEC,
        out_shape=jax.ShapeDtypeStruct((r, c), F32),
        scratch_shapes=[pltpu.VMEM((N_DEV, r, c), F32), pltpu.SemaphoreType.DMA((7,)),
                        pltpu.SemaphoreType.DMA((7,)), pltpu.SemaphoreType.DMA],
        rides=rides)


_PACK_ROWS = 256


def _pack(arrays):
    flat = jnp.concatenate([a.reshape(-1).astype(F32) for a in arrays])
    unit = _PACK_ROWS * LANE
    total = -(-flat.shape[0] // unit) * unit
    return jnp.pad(flat, (0, total - flat.shape[0])).reshape(-1, LANE)


def _unpack(packed, shapes, lead=()):
    flat = packed.reshape(lead + (-1,))
    out, at = [], 0
    for s in shapes:
        size = math.prod(s)
        out.append(flat[..., at:at + size].reshape(lead + tuple(s)))
        at += size
    return out


def kernel(x, pre_mix_g, post_mix_g, pre_ffn_g, post_ffn_g, a_w_in, a_v_norm_g, a_w_spatial, a_b_spatial, a_w_out, kv_norm_g, w_k, w_v, b_w_q, b_w_o, ffn_w_up, ffn_conv_w, ffn_conv_b, ffn_w_down, loss_target, m_pre_mix_g, m_post_mix_g, m_pre_ffn_g, m_post_ffn_g, m_a_w_in, m_a_v_norm_g, m_a_w_spatial, m_a_b_spatial, m_a_w_out, m_kv_norm_g, m_w_k, m_w_v, m_b_w_q, m_b_w_o, m_ffn_w_up, m_ffn_conv_w, m_ffn_conv_b, m_ffn_w_down, v_pre_mix_g, v_post_mix_g, v_pre_ffn_g, v_post_ffn_g, v_a_w_in, v_a_v_norm_g, v_a_w_spatial, v_a_b_spatial, v_a_w_out, v_kv_norm_g, v_w_k, v_w_v, v_b_w_q, v_b_w_o, v_ffn_w_up, v_ffn_conv_w, v_ffn_conv_b, v_ffn_w_down):
    xi, yi, ci = lax.axis_index("x"), lax.axis_index("y"), lax.axis_index("c")
    chip = 2 * xi + yi
    c_idx = jnp.reshape(ci, (1,)).astype(jnp.int32)
    _, s, d = x.shape
    n_layers = pre_mix_g.shape[0]
    assert n_layers == 2 and a_w_in.shape[0] == 1 and b_w_q.shape[0] == 1
    d_a = a_w_out.shape[1] * N_CHIPS
    n_g = a_w_spatial.shape[1]
    ns = ffn_w_up.shape[2]
    assert a_w_spatial.shape[2] == TILE and d_a == n_g * TILE and s % TILE == 0
    h0 = x[0]
    target = loss_target[0]

    big = {
        "win": (a_w_in, m_a_w_in, v_a_w_in),
        "wout": (a_w_out, m_a_w_out, v_a_w_out),
        "wk": (w_k[None], m_w_k[None], v_w_k[None]),
        "wv": (w_v[None], m_w_v[None], v_w_v[None]),
        "wq": (b_w_q, m_b_w_q, v_b_w_q),
        "wo": (b_w_o, m_b_w_o, v_b_w_o),
        "wup": (ffn_w_up, m_ffn_w_up, v_ffn_w_up),
        "wdn": (ffn_w_down, m_ffn_w_down, v_ffn_w_down),
    }
    units = [(nm, layer) for nm in big for layer in range(big[nm][0].shape[0])]
    chip_idx = jnp.reshape(chip, (1,)).astype(jnp.int32)
    shards = [_cast_bf16(big[nm][0], layer, chip_idx, f"cast_{nm}{layer}") for nm, layer in units]
    small_sharded = _pack([a_v_norm_g, ffn_conv_w])
    small_sharded = lax.dynamic_update_index_in_dim(
        jnp.zeros((N_CHIPS,) + small_sharded.shape, F32), small_sharded, chip, 0)
    own = dict(zip(units, shards))
    full = {}

    def gather_ride(keys):
        return _ride_gather([own[key] for key in keys])

    def gathered(keys, rode):
        full.update(zip(keys, rode[0]))

    first_keys = [("win", 0)]
    (first_bufs, _), = _run_rides([_ride_gather([own[key] for key in first_keys] + [small_sharded])], "gather_first")
    full.update(zip(first_keys, first_bufs[:-1]))
    vg_parts, cw_parts = _unpack(first_bufs[-1], [a_v_norm_g.shape, ffn_conv_w.shape], lead=(N_CHIPS,))
    v_g = jnp.transpose(vg_parts, (1, 0, 2)).reshape(1, d_a)

    def rows(nm, layer=0):
        w = full[(nm, layer)]
        return w.reshape(w.shape[0] * w.shape[1], w.shape[2])

    gains = lambda g, layer: g[layer:layer + 1]
    bias = jnp.repeat(a_b_spatial[0].T, TILE, axis=1)
    w_s = a_w_spatial[0]
    kv_g = kv_norm_g[None]
    conv_w = [cw_parts[:, layer] for layer in range(n_layers)]
    conv_b = [ffn_conv_b[layer].reshape(N_CHIPS, 1, ns) for layer in range(n_layers)]

    def ffn_fwd(hn, layer, up_keys=(), act_keys=(), down_keys=()):
        a = _mm(hn, full[("wup", layer)], "nn", f"ffn_up{layer}", out_split=N_CHIPS,
                rides=[gather_ride(up_keys)] if up_keys else ())
        if up_keys:
            a, (rode,) = a
            gathered(up_keys, rode)
        hm = _ffn_act_fwd(a, conv_w[layer], conv_b[layer], f"ffn_act{layer}",
                          rides=[gather_ride(act_keys)] if act_keys else ())
        if act_keys:
            hm, (rode,) = hm
            gathered(act_keys, rode)
        f = _mm(hm, rows("wdn", layer), "nn", f"ffn_down{layer}", rides=[gather_ride(down_keys)] if down_keys else ())
        if down_keys:
            f, (rode,) = f
            gathered(down_keys, rode)
        return a, hm, f[0]

    up0 = own[("wup", 0)]
    pieces = lambda p, span: _ride_gather([up0], part=p, n_parts=8, span=span)
    hn0 = _rms_fwd(h0, gains(pre_mix_g, 0), "norm_in")
    uv, ((out_bufs, _), ((up0,), _)) = _mm(
        hn0, full[("win", 0)], "nn", "gmlp_in", out_split=N_CHIPS, rides=[gather_ride([("wout", 0)]), pieces(0, 1)])
    full[("wout", 0)] = out_bufs[0]
    gm, (((up0,), _),) = _gmlp_fwd(uv, v_g, w_s, bias, "gmlp_gate", rides=[pieces(1, 2)])
    mix0, (((up0,), _),) = _mm(gm, rows("wout"), "nn", "gmlp_out", rides=[pieces(3, 2)])
    mix0 = mix0[0]
    (h1, hn1), (((up0,), _),) = _resid_rms(
        h0, mix0, gains(post_mix_g, 0), [gains(pre_ffn_g, 0)], "resid_mix0", rides=[pieces(5, 3)])
    full[("wup", 0)] = up0
    a0, hm0, f0 = ffn_fwd(hn1, 0, up_keys=[("wdn", 0)], act_keys=[("wq", 0), ("wk", 0)],
                          down_keys=[("wv", 0), ("wo", 0)])
    h2, hn2, kvn = _resid_rms(h1, f0, gains(post_ffn_g, 0), [gains(pre_mix_g, 1), kv_g], "resid_ffn0")
    q = _mm(hn2, rows("wq"), "nn", "proj_q", out_dtype=BF16)[0]
    k = _mm(kvn, rows("wk"), "nn", "proj_k", out_dtype=BF16)[0]
    v = _mm(kvn, rows("wv"), "nn", "proj_v", out_dtype=BF16)[0]
    last_keys = [("wup", 1), ("wdn", 1)]
    (att, lsum), (rode,) = _attn_fwd(q, k, v, "attn_fwd", rides=[gather_ride(last_keys)])
    gathered(last_keys, rode)
    mix1 = _mm(att, rows("wo"), "nn", "proj_o")[0]
    h3, hn3 = _resid_rms(h2, mix1, gains(post_mix_g, 1), [gains(pre_ffn_g, 1)], "resid_mix1")
    a1, hm1, f1 = ffn_fwd(hn3, 1)
    dh4, loss_tile = _loss_head(h3, f1, gains(post_ffn_g, 1), target, "loss_head")
    loss = lax.psum(loss_tile[0, 0], ("x", "y", "c"))

    dw = {}
    dg = {}

    pair = {}
    half_done = {nm: None for nm in big}

    def swap_ride(keys):
        return _ride_swap([dw[key] for key in keys])

    def swapped(keys, rode):
        for (nm, layer), got in zip(keys, rode[1]):
            pair[(nm, layer)] = _pair_add(dw[(nm, layer)], got, c_idx, f"pair_add_{nm}{layer}")

    def scatter_ride(keys):
        return _ride_scatter([pair[key] for key in keys])

    def scattered(keys, rode):
        for (nm, layer), got in zip(keys, rode[1]):
            half_done[nm] = _chip_sum(got, half_done[nm], big[nm][0].shape, layer, c_idx, f"chip_sum_{nm}{layer}")

    def ffn_bwd(dh_out, h_in, hn, a, hm, f, layer, act_rides=()):
        df, dg[("post_ffn", layer)] = _rms_bwd_out(dh_out, f, gains(post_ffn_g, layer), f"d_norm_ffn_out{layer}")
        dwd = _mm(hm, df, "tn", f"d_w_down{layer}", out_dtype=BF16)[0]
        down, up = [("wdn", layer)], [("wup", layer)]
        dw[down[0]] = dwd.reshape(N_CHIPS, dwd.shape[0] // N_CHIPS, d)
        dhm, (rode,) = _mm(df, rows("wdn", layer), "nt", f"d_ffn_mid{layer}", out_split=2, rides=[swap_ride(down)])
        swapped(down, rode)
        (da, dg[("conv_w", layer)], dg[("conv_b", layer)]), act_rode = _ffn_act_bwd(
            a, dhm, conv_w[layer], conv_b[layer], f"d_ffn_act{layer}", rides=act_rides)
        dw[up[0]], (rode,) = _mm(hn, da, "tn", f"d_w_up{layer}", out_dtype=BF16, out_split=N_CHIPS,
                                 rides=[scatter_ride(down)])
        scattered(down, rode)
        dhn, (rode,) = _mm(da, full[("wup", layer)], "nt", f"d_ffn_in{layer}", rides=[swap_ride(up)])
        swapped(up, rode)
        return dhn[0], act_rode

    dhn3, _ = ffn_bwd(dh4, h3, hn3, a1, hm1, f1, 1)
    dh3, (dg[("pre_ffn", 1)],) = _rms_bwd_in(dh4, h3, [([dhn3], gains(pre_ffn_g, 1))], "d_norm_ffn_in1")
    dmix1, dg[("post_mix", 1)] = _rms_bwd_out(dh3, mix1, gains(post_mix_g, 1), "d_norm_mix_out1")
    dwo = _mm(att, dmix1, "tn", "d_w_o", out_dtype=BF16)[0]
    dw[("wo", 0)] = dwo.reshape(N_CHIPS, dwo.shape[0] // N_CHIPS, d)
    datt = _mm(dmix1, rows("wo"), "nt", "d_attn_out", out_dtype=BF16)[0]
    ffn1_keys = [("wup", 1)]
    (dq, dk, dv), (rode,) = _attn_bwd(q, k, v, datt, lsum, "attn_bwd", rides=[scatter_ride(ffn1_keys)])
    scattered(ffn1_keys, rode)
    for nm, act, dact in (("wq", hn2, dq), ("wk", kvn, dk), ("wv", kvn, dv)):
        g = _mm(act, dact, "tn", f"d_{nm}", out_dtype=BF16)[0]
        dw[(nm, 0)] = g.reshape(N_CHIPS, g.shape[0] // N_CHIPS, g.shape[1])
    dhn2 = _mm(dq, rows("wq"), "nt", "d_q_in")[0]
    dkvn_k = _mm(dk, rows("wk"), "nt", "d_k_in")[0]
    attn_keys = [("wo", 0), ("wq", 0), ("wk", 0), ("wv", 0)]
    dkvn_v, (rode,) = _mm(dv, rows("wv"), "nt", "d_v_in", rides=[swap_ride(attn_keys)])
    swapped(attn_keys, rode)
    dh2, (dg[("pre_mix", 1)], dg["kv"]) = _rms_bwd_in(
        dh3, h2, [([dhn2], gains(pre_mix_g, 1)), ([dkvn_k, dkvn_v[0]], kv_g)], "d_norm_mix_in1")
    dhn1, (rode,) = ffn_bwd(dh2, h1, hn1, a0, hm0, f0, 0, act_rides=[scatter_ride(attn_keys)])
    scattered(attn_keys, rode)
    up0_pair = [pair[("wup", 0)]]
    up0_landed = [None]

    def up0_piece(part, span):
        return _ride_scatter(up0_pair, part, 8, into=up0_landed[0], span=span)

    def up0_rode(rode):
        up0_landed[0] = rode[1] if up0_landed[0] is None else rode[0]

    dh1, (dg[("pre_ffn", 0)],), (rode,) = _rms_bwd_in(
        dh2, h1, [([dhn1], gains(pre_ffn_g, 0))], "d_norm_ffn_in0", rides=[up0_piece(0, 1)])
    up0_rode(rode)
    dmix0, dg[("post_mix", 0)], (rode,) = _rms_bwd_out(
        dh1, mix0, gains(post_mix_g, 0), "d_norm_mix_out0", rides=[up0_piece(1, 1)])
    up0_rode(rode)
    early = ["wq", "wk", "wv", "wo", "wdn"]
    dwout, (((joined_early, _)),) = _mm(
        gm, dmix0, "tn", "d_w_out", out_dtype=BF16, rides=[_ride_join([half_done[nm] for nm in early])])
    grads_big = dict(zip(early, joined_early))
    w_out_key, w_in_key = [("wout", 0)], [("win", 0)]
    dw[w_out_key[0]] = dwout[0].reshape(N_CHIPS, dwout.shape[1] // N_CHIPS, d)
    dgm, (rode, up0) = _mm(dmix0, rows("wout"), "nt", "d_gmlp_gate", rides=[swap_ride(w_out_key), up0_piece(2, 1)])
    swapped(w_out_key, rode)
    up0_rode(up0)
    (duv, d_ws, d_bs, d_vg), (rode,) = _gmlp_bwd(uv, dgm[0], v_g, w_s, bias, "d_gmlp", rides=[up0_piece(3, 2)])
    up0_rode(rode)
    dw[w_in_key[0]], (rode, up0) = _mm(
        hn0, duv, "tn", "d_w_in", out_dtype=BF16, out_split=N_CHIPS, rides=[scatter_ride(w_out_key), up0_piece(5, 1)])
    scattered(w_out_key, rode)
    up0_rode(up0)
    dhn0, (rode, up0) = _mm(
        duv, full[("win", 0)], "nt", "d_gmlp_in", rides=[swap_ride(w_in_key), up0_piece(6, 2)])
    swapped(w_in_key, rode)
    up0_rode(up0)
    scattered([("wup", 0)], (None, up0_landed[0]))
    dx, (dg[("pre_mix", 0)],), (rode,) = _rms_bwd_in(
        dh1, h0, [([dhn0[0]], gains(pre_mix_g, 0))], "d_norm_in", rides=[scatter_ride(w_in_key)])
    scattered(w_in_key, rode)

    stack = lambda key: jnp.concatenate([dg[(key, layer)] for layer in range(n_layers)], axis=0)
    small_parts = [
        stack("pre_mix"), stack("post_mix"), stack("pre_ffn"), stack("post_ffn"),
        d_vg, d_ws, d_bs[::SUBLANE], dg["kv"],
        jnp.stack([dg[("conv_w", layer)] for layer in range(n_layers)]),
        jnp.stack([dg[("conv_b", layer)] for layer in range(n_layers)]),
    ]
    late = [nm for nm in big if nm not in early]
    summed, ((joined_late, _),) = _all_reduce_small(
        _pack(small_parts), "small_grads_sum", rides=[_ride_join([half_done[nm] for nm in late])])
    grads_big.update(zip(late, joined_late))
    (g_pre_mix, g_post_mix, g_pre_ffn, g_post_ffn, g_vg, g_ws, g_bs, g_kv, g_cw, g_cb) = _unpack(
        summed, [p.shape for p in small_parts])
    g_vg = lax.dynamic_index_in_dim(g_vg.reshape(N_CHIPS, 1, d_a // N_CHIPS), chip, 0, keepdims=False)
    g_cw = lax.dynamic_index_in_dim(g_cw, chip, 1, keepdims=False)
    g_cb = g_cb.reshape(n_layers, N_CHIPS * ns)
    small = [
        (pre_mix_g, g_pre_mix, m_pre_mix_g, v_pre_mix_g),
        (post_mix_g, g_post_mix, m_post_mix_g, v_post_mix_g),
        (pre_ffn_g, g_pre_ffn, m_pre_ffn_g, v_pre_ffn_g),
        (post_ffn_g, g_post_ffn, m_post_ffn_g, v_post_ffn_g),
        (a_v_norm_g, g_vg, m_a_v_norm_g, v_a_v_norm_g),
        (a_w_spatial, g_ws[None], m_a_w_spatial, v_a_w_spatial),
        (a_b_spatial, g_bs[None], m_a_b_spatial, v_a_b_spatial),
        (kv_norm_g, g_kv.reshape(d), m_kv_norm_g, v_kv_norm_g),
        (ffn_conv_w, g_cw, m_ffn_conv_w, v_ffn_conv_w),
        (ffn_conv_b, g_cb, m_ffn_conv_b, v_ffn_conv_b),
    ]
    small = [(w, g.reshape(w.shape), m, v) for w, g, m, v in small]
    packed = [_pack([t[i] for t in small])[None] for i in range(4)]
    small_new = [_unpack(p[0], [t[0].shape for t in small]) for p in _adamw(*packed, "adamw_small")]

    new_big = {nm: _adamw(big[nm][0], grads_big[nm], big[nm][1], big[nm][2], f"adamw_{nm}", pass_g=True)
               for nm in big}

    def big_out(nm, which):
        ref_shape = {"wk": w_k.shape, "wv": w_v.shape}.get(nm, big[nm][0].shape)
        return new_big[nm][which].reshape(ref_shape)

    order = ["pre_mix", "post_mix", "pre_ffn", "post_ffn", "win", "vg", "ws", "bs", "wout", "kv", "wk", "wv", "wq",
             "wo", "wup", "cw", "cb", "wdn"]
    small_at = {"pre_mix": 0, "post_mix": 1, "pre_ffn": 2, "post_ffn": 3, "vg": 4, "ws": 5, "bs": 6, "kv": 7,
                "cw": 8, "cb": 9}
    outs = [loss, dx[None]]
    for which in range(4):
        for nm in order:
            if nm in small_at:
                outs.append(small[small_at[nm]][1] if which == 0 else small_new[which - 1][small_at[nm]])
            else:
                outs.append(big_out(nm, which))
    return tuple(outs)
```

```python
import functools
import math

import jax
import jax.numpy as jnp
from jax import lax
from jax.experimental import pallas as pl
from jax.experimental.pallas import tpu as pltpu

F32 = jnp.float32
BF16 = jnp.bfloat16
EPS = 1e-6
ADAM_LR = 0.001
ADAM_B1 = 0.9
ADAM_B2 = 0.999
ADAM_EPS = 1e-08
ADAM_WD = 0.01
ADAM_STEP = 10

LANE = 128
SUBLANE = 8
ROWS = 16
TILE = 128
N_CHIPS = 4
N_DEV = 8
VMEM_LIMIT = 56 * 1024 * 1024
MM_VMEM = 40 * 1024 * 1024
MESH = pl.DeviceIdType.MESH
ANY = pl.BlockSpec(memory_space=pl.ANY)
VMEM_SPEC = pl.BlockSpec(memory_space=pltpu.VMEM)


def _cp(*sem):
    return pltpu.CompilerParams(dimension_semantics=sem, vmem_limit_bytes=VMEM_LIMIT)


def _pick(dim, pref, align=LANE):
    if dim <= pref:
        return dim
    best = None
    for d in range(align, pref + 1, align):
        if dim % d == 0:
            best = d
    assert best is not None, (dim, pref)
    return best


_DIMS = {
    "nn": (((1,), (0,)), ((), ())),
    "nt": (((1,), (1,)), ((), ())),
    "tn": (((0,), (0,)), ((), ())),
}


def _as3(a):
    return a if a.ndim == 3 else a[None]


def _spec3(br, bc, cols_j, rc):
    per = cols_j // bc

    def imap(m, n, k):
        r, c = rc(m, n, k)
        return (c // per, r, c % per)

    return pl.BlockSpec((None, br, bc), imap)


def _mm(a, b, mode, name, out_dtype=F32, out_split=1, rides=()):
    a, b = _as3(a), _as3(b)
    ja, ra, caj = a.shape
    jb, rb, cbj = b.shape
    if mode == "nn":
        m, k, n = ra, ja * caj, jb * cbj
        assert rb == k
        m_ext, k_ext, n_ext = [ra], [caj, rb], [cbj]
    elif mode == "nt":
        m, k, n = ra, ja * caj, rb
        assert jb * cbj == k
        m_ext, k_ext, n_ext = [ra], [caj, cbj], [rb]
    else:
        m, k, n = ja * caj, ra, jb * cbj
        assert rb == k
        m_ext, k_ext, n_ext = [caj], [ra], [cbj]
    assert n % out_split == 0
    n_ext.append(n // out_split)
    bm = _pick(math.gcd(*m_ext), 1536)
    bn = _pick(math.gcd(*n_ext), 1536)
    k_unit = math.gcd(*k_ext)
    o_bytes = jnp.dtype(out_dtype).itemsize

    def vmem_need(bm, bn, bk):
        tiles = bm * bk * a.dtype.itemsize + bk * bn * b.dtype.itemsize + bm * bn * o_bytes
        return 2 * tiles + bm * bn * 4 * (2 if bk < k else 1)

    def deepest(bm, bn):
        return max(d for d in range(LANE, k_unit + 1, LANE)
                   if k_unit % d == 0 and (d == LANE or vmem_need(bm, bn, d) <= MM_VMEM))

    bk = deepest(bm, bn)
    if bk < k_unit and k_unit == k:
        if bm % (2 * LANE) == 0 and deepest(bm // 2, bn) == k:
            bm, bk = bm // 2, k
        elif bn % (2 * LANE) == 0 and deepest(bm, bn // 2) == k:
            bn, bk = bn // 2, k
    nk = k // bk
    if mode == "nn":
        a_spec = _spec3(bm, bk, caj, lambda mi, ni, ki: (mi, ki))
        b_spec = _spec3(bk, bn, cbj, lambda mi, ni, ki: (ki, ni))
    elif mode == "nt":
        a_spec = _spec3(bm, bk, caj, lambda mi, ni, ki: (mi, ki))
        b_spec = _spec3(bn, bk, cbj, lambda mi, ni, ki: (ni, ki))
    else:
        a_spec = _spec3(bk, bm, caj, lambda mi, ni, ki: (ki, mi))
        b_spec = _spec3(bk, bn, cbj, lambda mi, ni, ki: (ki, ni))
    o_spec = _spec3(bm, bn, n // out_split, lambda mi, ni, ki: (mi, ni))
    dims = _DIMS[mode]

    def body(a_ref, b_ref, o_ref, *acc):
        def part():
            return lax.dot_general(a_ref[...].astype(BF16), b_ref[...].astype(BF16), dims, preferred_element_type=F32)

        if nk == 1:
            o_ref[...] = part().astype(o_ref.dtype)
            return
        acc_ref, = acc
        ki = pl.program_id(2)

        @pl.when(ki == 0)
        def _():
            acc_ref[...] = part()

        @pl.when(jnp.logical_and(ki > 0, ki < nk - 1))
        def _():
            acc_ref[...] += part()

        @pl.when(ki == nk - 1)
        def _():
            o_ref[...] = (acc_ref[...] + part()).astype(o_ref.dtype)

    out, rode = _hosted_call(
        body, [a, b], name=name, grid=(m // bm, n // bn, nk), in_specs=[a_spec, b_spec], out_specs=o_spec,
        out_shape=jax.ShapeDtypeStruct((out_split, m, n // out_split), out_dtype),
        scratch_shapes=[pltpu.VMEM((bm, bn), F32)] if nk > 1 else [],
        semantics=("parallel", "parallel", "arbitrary"), rides=rides)
    return (out, rode) if rides else out


def _rms(x, g):
    r = lax.rsqrt(jnp.mean(x * x, axis=-1, keepdims=True) + EPS)
    return x * r * g


def _rms_bwd(x, g, dy):
    r = lax.rsqrt(jnp.mean(x * x, axis=-1, keepdims=True) + EPS)
    xh = x * r
    gy = dy * g
    dx = r * (gy - xh * jnp.mean(gy * xh, axis=-1, keepdims=True))
    return dx, jnp.sum(dy * xh, axis=0, keepdims=True)


def _row_block(s):
    return _pick(s, 256, ROWS)


def _rms_fwd(h, g, name):
    s, d = h.shape
    br = _row_block(s)

    def body(h_ref, g_ref, o_ref):
        o_ref[...] = _rms(h_ref[...], g_ref[...]).astype(BF16)

    row = pl.BlockSpec((br, d), lambda i: (i, 0))
    vec = pl.BlockSpec((1, d), lambda i: (0, 0))
    return pl.pallas_call(
        body, name=name, grid=(s // br,), in_specs=[row, vec], out_specs=row,
        out_shape=jax.ShapeDtypeStruct((s, d), BF16), compiler_params=_cp("parallel"),
    )(h, g)


def _resid_rms(h_in, f, g_post, g_next, name, rides=()):
    s, d = h_in.shape
    br = _row_block(s)
    n_next = len(g_next)

    def body(h_ref, f_ref, gp_ref, *refs):
        gn_refs, ho_ref, hn_refs = refs[:n_next], refs[n_next], refs[n_next + 1:]
        h = h_ref[...] + _rms(f_ref[...], gp_ref[...])
        ho_ref[...] = h
        for gn_ref, hn_ref in zip(gn_refs, hn_refs):
            hn_ref[...] = _rms(h, gn_ref[...]).astype(BF16)

    row = pl.BlockSpec((br, d), lambda i: (i, 0))
    vec = pl.BlockSpec((1, d), lambda i: (0, 0))
    outs, rode = _hosted_call(
        body, [h_in, f, g_post, *g_next], name=name, grid=(s // br,),
        in_specs=[row, row, vec] + [vec] * n_next,
        out_specs=[row] * (1 + n_next),
        out_shape=[jax.ShapeDtypeStruct((s, d), F32)] + [jax.ShapeDtypeStruct((s, d), BF16)] * n_next,
        semantics=("parallel",), rides=rides)
    return (outs, rode) if rides else outs


def _loss_head(h_in, f, g_post, target, name):
    s, d = h_in.shape
    br = _row_block(s)

    def body(h_ref, f_ref, gp_ref, t_ref, dh_ref, loss_ref):
        @pl.when(pl.program_id(0) == 0)
        def _():
            loss_ref[...] = jnp.zeros_like(loss_ref)

        diff = h_ref[...] + _rms(f_ref[...], gp_ref[...]) - t_ref[...]
        dh_ref[...] = diff * (1.0 / d)
        loss_ref[...] += 0.5 * jnp.sum(jnp.mean(diff * diff, axis=-1, keepdims=True))

    row = pl.BlockSpec((br, d), lambda i: (i, 0))
    vec = pl.BlockSpec((1, d), lambda i: (0, 0))
    return pl.pallas_call(
        body, name=name, grid=(s // br,),
        in_specs=[row, row, vec, row],
        out_specs=[row, pl.BlockSpec((SUBLANE, LANE), lambda i: (0, 0))],
        out_shape=[jax.ShapeDtypeStruct((s, d), F32), jax.ShapeDtypeStruct((SUBLANE, LANE), F32)],
        compiler_params=_cp("arbitrary"),
    )(h_in, f, g_post, target)


def _rms_bwd_out(dy, f, g, name, rides=()):
    s, d = f.shape
    br = _row_block(s)

    def body(dy_ref, f_ref, g_ref, df_ref, dg_ref):
        @pl.when(pl.program_id(0) == 0)
        def _():
            dg_ref[...] = jnp.zeros_like(dg_ref)

        dx, dg = _rms_bwd(f_ref[...], g_ref[...], dy_ref[...])
        df_ref[...] = dx.astype(BF16)
        dg_ref[...] += dg

    row = pl.BlockSpec((br, d), lambda i: (i, 0))
    vec = pl.BlockSpec((1, d), lambda i: (0, 0))
    (df, dg), rode = _hosted_call(
        body, [dy, f, g], name=name, grid=(s // br,), in_specs=[row, row, vec], out_specs=[row, vec],
        out_shape=[jax.ShapeDtypeStruct((s, d), BF16), jax.ShapeDtypeStruct((1, d), F32)],
        semantics=("arbitrary",), rides=rides)
    return (df, dg, rode) if rides else (df, dg)


def _rms_bwd_in(dh_out, h_in, branches, name, rides=()):
    s, d = h_in.shape
    br = _row_block(s)
    counts = [len(ds) for ds, _ in branches]
    n_d = sum(counts)
    n_b = len(branches)

    def body(dho_ref, h_ref, *refs):
        d_refs, g_refs = refs[:n_d], refs[n_d:n_d + n_b]
        dh_ref, dg_refs = refs[n_d + n_b], refs[n_d + n_b + 1:]

        @pl.when(pl.program_id(0) == 0)
        def _():
            for r in dg_refs:
                r[...] = jnp.zeros_like(r)

        h = h_ref[...]
        acc = dho_ref[...]
        at = 0
        for bi, cnt in enumerate(counts):
            dn = d_refs[at][...]
            for r in d_refs[at + 1:at + cnt]:
                dn = dn + r[...]
            at += cnt
            dx, dg = _rms_bwd(h, g_refs[bi][...], dn)
            acc = acc + dx
            dg_refs[bi][...] += dg
        dh_ref[...] = acc

    row = pl.BlockSpec((br, d), lambda i: (i, 0))
    vec = pl.BlockSpec((1, d), lambda i: (0, 0))
    flat_d = [x for ds, _ in branches for x in ds]
    outs, rode = _hosted_call(
        body, [dh_out, h_in, *flat_d, *[g for _, g in branches]], name=name, grid=(s // br,),
        in_specs=[row, row] + [row] * n_d + [vec] * n_b,
        out_specs=[row] + [vec] * n_b,
        out_shape=[jax.ShapeDtypeStruct((s, d), F32)] + [jax.ShapeDtypeStruct((1, d), F32)] * n_b,
        semantics=("arbitrary",), rides=rides)
    return (outs[0], list(outs[1:]), rode) if rides else (outs[0], list(outs[1:]))


def _split3(x):
    x0 = x.astype(BF16)
    r1 = x - x0.astype(F32)
    x1 = r1.astype(BF16)
    x2 = (r1 - x1.astype(F32)).astype(BF16)
    return x0, x1, x2


def _tri(n, kind):
    r = lax.broadcasted_iota(jnp.int32, (n, n), 0)
    c = lax.broadcasted_iota(jnp.int32, (n, n), 1)
    m = {"lt": r < c, "le": r <= c, "gt": r > c}[kind]
    return jnp.where(m, 1.0, 0.0).astype(BF16)


_GELU_C = math.sqrt(2.0 / math.pi)
_GELU_A = 0.044715


def _gelu(x):
    return 0.5 * x * (1.0 + jnp.tanh(_GELU_C * (x + _GELU_A * (x * x * x))))


def _gelu_grad(x):
    t = jnp.tanh(_GELU_C * (x + _GELU_A * (x * x * x)))
    return 0.5 * (1.0 + t) + 0.5 * x * (1.0 - t * t) * (_GELU_C * (1.0 + 3.0 * _GELU_A * (x * x)))


def _causal_w(w):
    r = lax.broadcasted_iota(jnp.int32, (TILE, TILE), 0)
    c = lax.broadcasted_iota(jnp.int32, (TILE, TILE), 1)
    return jnp.where(c <= r, w, 0.0)


def _uv_tiles(uv_ref, g, d_a, dq):
    cu, cv = g * TILE, d_a + g * TILE
    u = uv_ref[cu // dq, :, pl.ds(cu % dq, TILE)]
    v = uv_ref[cv // dq, :, pl.ds(cv % dq, TILE)]
    return u, v


def _gmlp_fwd(uv, v_g, w_s, bias, name, rides=()):
    _, s, dq = uv.shape
    d_a = 2 * dq
    n_g = d_a // TILE

    def body(uv_ref, vg_ref, ws_ref, b_ref, o_ref):
        for g in range(n_g):
            up, vp = _uv_tiles(uv_ref, g, d_a, dq)
            cols = pl.ds(g * TILE, TILE)
            vn = _rms(_gelu(vp), vg_ref[:, cols])
            mixed = jnp.dot(_causal_w(ws_ref[g]).astype(BF16), vn.astype(BF16), preferred_element_type=F32) + b_ref[:, cols]
            o_ref[:, cols] = (_gelu(up) * mixed).astype(BF16)

    return _hosted_call(
        body, [uv, v_g, w_s, bias], name=name, grid=(s // TILE,),
        in_specs=[
            pl.BlockSpec((4, TILE, dq), lambda i: (0, i, 0)),
            pl.BlockSpec((1, d_a), lambda i: (0, 0)),
            pl.BlockSpec((n_g, TILE, TILE), lambda i: (0, 0, 0)),
            pl.BlockSpec((TILE, d_a), lambda i: (0, 0)),
        ],
        out_specs=pl.BlockSpec((TILE, d_a), lambda i: (i, 0)),
        out_shape=jax.ShapeDtypeStruct((s, d_a), BF16),
        semantics=("parallel",), rides=rides)


def _gmlp_bwd(uv, dgm, v_g, w_s, bias, name, rides=()):
    _, s, dq = uv.shape
    d_a = 2 * dq
    n_g = d_a // TILE
    n_c = s // TILE

    def body(uv_ref, d_ref, vg_ref, ws_ref, b_ref, duv_ref, dws_ref, dbs_ref, dvg_ref, dbias_acc):
        i = pl.program_id(0)

        @pl.when(i == 0)
        def _():
            dws_ref[...] = jnp.zeros_like(dws_ref)
            dvg_ref[...] = jnp.zeros_like(dvg_ref)
            dbias_acc[...] = jnp.zeros_like(dbias_acc)

        for g in range(n_g):
            up, vp = _uv_tiles(uv_ref, g, d_a, dq)
            cols = pl.ds(g * TILE, TILE)
            vg = vg_ref[:, cols]
            u = _gelu(up)
            v = _gelu(vp)
            r = lax.rsqrt(jnp.mean(v * v, axis=-1, keepdims=True) + EPS)
            vh = v * r
            vn = (vh * vg).astype(BF16)
            wc = _causal_w(ws_ref[g]).astype(BF16)
            mixed = jnp.dot(wc, vn, preferred_element_type=F32) + b_ref[:, cols]
            d_out = d_ref[:, cols]
            du = d_out * mixed
            dmixed = d_out * u
            dmb = dmixed.astype(BF16)
            dvn = lax.dot_general(wc, dmb, _DIMS["tn"], preferred_element_type=F32)
            dws_ref[g] += lax.dot_general(dmb, vn, _DIMS["nt"], preferred_element_type=F32)
            dbias_acc[:, cols] += dmixed
            dvg_ref[:, cols] += jnp.sum(dvn * vh, axis=0, keepdims=True)
            gv = dvn * vg
            dv = r * (gv - vh * jnp.mean(gv * vh, axis=-1, keepdims=True))
            cu, cv = g * TILE, d_a + g * TILE
            duv_ref[cu // dq, :, pl.ds(cu % dq, TILE)] = (du * _gelu_grad(up)).astype(BF16)
            duv_ref[cv // dq, :, pl.ds(cv % dq, TILE)] = (dv * _gelu_grad(vp)).astype(BF16)

        @pl.when(i == n_c - 1)
        def _():
            ones = jnp.ones((SUBLANE, TILE), BF16)
            for g in range(n_g):
                dws_ref[g] = _causal_w(dws_ref[g])
                cols = pl.ds(g * TILE, TILE)
                out = None
                for t in _split3(dbias_acc[:, cols]):
                    p = lax.dot_general(ones, t, _DIMS["nt"], preferred_element_type=F32)
                    out = p if out is None else out + p
                dbs_ref[pl.ds(g * SUBLANE, SUBLANE), :] = out

    return _hosted_call(
        body, [uv, dgm, v_g, w_s, bias], name=name, grid=(n_c,), semantics=("arbitrary",), rides=rides,
        in_specs=[
            pl.BlockSpec((4, TILE, dq), lambda i: (0, i, 0)),
            pl.BlockSpec((TILE, d_a), lambda i: (i, 0)),
            pl.BlockSpec((1, d_a), lambda i: (0, 0)),
            pl.BlockSpec((n_g, TILE, TILE), lambda i: (0, 0, 0)),
            pl.BlockSpec((TILE, d_a), lambda i: (0, 0)),
        ],
        out_specs=[
            pl.BlockSpec((4, TILE, dq), lambda i: (0, i, 0)),
            pl.BlockSpec((n_g, TILE, TILE), lambda i: (0, 0, 0)),
            pl.BlockSpec((n_g * SUBLANE, TILE), lambda i: (0, 0)),
            pl.BlockSpec((1, d_a), lambda i: (0, 0)),
        ],
        out_shape=[
            jax.ShapeDtypeStruct((4, s, dq), BF16),
            jax.ShapeDtypeStruct((n_g, TILE, TILE), F32),
            jax.ShapeDtypeStruct((n_g * SUBLANE, TILE), F32),
            jax.ShapeDtypeStruct((1, d_a), F32),
        ],
        scratch_shapes=[pltpu.VMEM((TILE, d_a), F32)])


def _sigmoid(x):
    return 1.0 / (1.0 + jnp.exp(-x))


def _conv3(ext, w, b):
    return b + ((w[0:1] * pltpu.roll(ext, 2, 0) + w[1:2] * pltpu.roll(ext, 1, 0)) + w[2:3] * ext)


def _act_blocks(s, ns):
    return _pick(s, 512, ROWS), _pick(ns, 256)


def _ffn_act_fwd(a, cw, cb, name, rides=()):
    _, s, ns = a.shape
    bs, cb_w = _act_blocks(s, ns)
    hb = bs // SUBLANE

    def body(a_ref, prev_ref, cw_ref, cb_ref, o_ref):
        first = pl.program_id(0) == 0

        def conv(comp):
            prev = jnp.where(first, 0.0, prev_ref[comp])
            ext = jnp.concatenate([prev, a_ref[comp]], axis=0)
            return _conv3(ext, cw_ref[comp], cb_ref[comp])[SUBLANE:]

        for p in range(2):
            cg = conv(p)
            o_ref[p] = (cg * _sigmoid(cg) * conv(2 + p)).astype(BF16)

    hm, rode = _hosted_call(
        body, [a, a, cw, cb], name=name, grid=(s // bs, ns // cb_w),
        in_specs=[
            pl.BlockSpec((4, bs, cb_w), lambda i, j: (0, i, j)),
            pl.BlockSpec((4, SUBLANE, cb_w), lambda i, j: (0, jnp.maximum(i * hb - 1, 0), j)),
            pl.BlockSpec((4, 3, cb_w), lambda i, j: (0, 0, j)),
            pl.BlockSpec((4, 1, cb_w), lambda i, j: (0, 0, j)),
        ],
        out_specs=pl.BlockSpec((2, bs, cb_w), lambda i, j: (0, i, j)),
        out_shape=jax.ShapeDtypeStruct((2, s, ns), BF16),
        semantics=("parallel", "parallel"), rides=rides)
    return (hm, rode) if rides else hm


def _ffn_act_bwd(a, dhm, cw, cb, name, rides=()):
    _, s, ns = a.shape
    bs, cb_w = _act_blocks(s, ns)
    hb = bs // SUBLANE
    n_i = s // bs
    n_ext = bs + 2 * SUBLANE
    cur = slice(SUBLANE, SUBLANE + bs)

    def body(a_ref, prev_ref, next_ref, d_ref, dnext_ref, cw_ref, cb_ref, da_ref, dcw_ref, dcb_ref):
        i = pl.program_id(1)
        first, last = i == 0, i == n_i - 1

        @pl.when(first)
        def _():
            dcw_ref[...] = jnp.zeros_like(dcw_ref)
            dcb_ref[...] = jnp.zeros_like(dcb_ref)

        def ext_of(comp):
            return jnp.concatenate([jnp.where(first, 0.0, prev_ref[comp]), a_ref[comp], next_ref[comp]], axis=0)

        def back(comp, a_ext, dc):
            w = cw_ref[comp]
            da = (w[2:3] * dc + w[1:2] * pltpu.roll(dc, n_ext - 1, 0)) + w[0:1] * pltpu.roll(dc, n_ext - 2, 0)
            da_ref[comp] = da[cur].astype(BF16)
            dcc = dc[cur]
            dcw_ref[comp, 0:1, :] += jnp.sum(dcc * pltpu.roll(a_ext, 2, 0)[cur], axis=0, keepdims=True)
            dcw_ref[comp, 1:2, :] += jnp.sum(dcc * pltpu.roll(a_ext, 1, 0)[cur], axis=0, keepdims=True)
            dcw_ref[comp, 2:3, :] += jnp.sum(dcc * a_ext[cur], axis=0, keepdims=True)
            dcb_ref[comp] += jnp.sum(dcc, axis=0, keepdims=True)

        for p in range(2):
            ag, av = ext_of(p), ext_of(2 + p)
            cg = _conv3(ag, cw_ref[p], cb_ref[p])
            cv = _conv3(av, cw_ref[2 + p], cb_ref[2 + p])
            d = jnp.concatenate(
                [jnp.zeros((SUBLANE, cb_w), F32), d_ref[p], jnp.where(last, 0.0, dnext_ref[p])], axis=0)
            sg = _sigmoid(cg)
            back(2 + p, av, d * (cg * sg))
            back(p, ag, d * cv * (sg * (1.0 + cg * (1.0 - sg))))

    return _hosted_call(
        body, [a, a, a, dhm, dhm, cw, cb], name=name, grid=(ns // cb_w, n_i),
        in_specs=[
            pl.BlockSpec((4, bs, cb_w), lambda j, i: (0, i, j)),
            pl.BlockSpec((4, SUBLANE, cb_w), lambda j, i: (0, jnp.maximum(i * hb - 1, 0), j)),
            pl.BlockSpec((4, SUBLANE, cb_w), lambda j, i: (0, jnp.minimum((i + 1) * hb, n_i * hb - 1), j)),
            pl.BlockSpec((2, bs, cb_w), lambda j, i: (0, i, j)),
            pl.BlockSpec((2, SUBLANE, cb_w), lambda j, i: (0, jnp.minimum((i + 1) * hb, n_i * hb - 1), j)),
            pl.BlockSpec((4, 3, cb_w), lambda j, i: (0, 0, j)),
            pl.BlockSpec((4, 1, cb_w), lambda j, i: (0, 0, j)),
        ],
        out_specs=[
            pl.BlockSpec((4, bs, cb_w), lambda j, i: (0, i, j)),
            pl.BlockSpec((4, 3, cb_w), lambda j, i: (0, 0, j)),
            pl.BlockSpec((4, 1, cb_w), lambda j, i: (0, 0, j)),
        ],
        out_shape=[
            jax.ShapeDtypeStruct((4, s, ns), BF16),
            jax.ShapeDtypeStruct((4, 3, ns), F32),
            jax.ShapeDtypeStruct((4, 1, ns), F32),
        ],
        semantics=("parallel", "arbitrary"), rides=rides)


ATT_BQ_FWD = 2048
ATT_BQ_BWD = 1024
ATT_BK = 256
ATT_UNROLL = 2
ATT_UNROLL_BWD = 4


def _att_blocks(s, bq_pref):
    bq = _pick(s, bq_pref)
    bk = min(ATT_BK, bq)
    assert bq % bk == 0
    return bq, bk


def _dot_sel2(x, sel):
    hi = x.astype(BF16)
    lo = (x - hi.astype(F32)).astype(BF16)
    n = x.shape[0]
    both = jnp.dot(jnp.concatenate([hi, lo], axis=0), sel, preferred_element_type=F32)
    return both[:n] + both[n:]


def _causal_mask(bq, bk, row0, col0):
    rows = row0 + lax.broadcasted_iota(jnp.int32, (bq, bk), 0)
    cols = col0 + lax.broadcasted_iota(jnp.int32, (bq, bk), 1)
    return cols < rows


def _sb_tile(qb, kb, scale, mask):
    z = lax.dot_general(qb, kb, _DIMS["nt"], preferred_element_type=F32) * scale
    e = jnp.exp(-jnp.abs(z))
    lb = jnp.minimum(z, 0.0) - jnp.log(1.0 + e)
    l1m = lb - z
    if mask is not None:
        l1m = jnp.where(mask, l1m, 0.0)
    return z, e, lb, l1m


def _attn_fwd(q, k, v, name, rides=()):
    s, hd = q.shape
    bq, bk = _att_blocks(s, ATT_BQ_FWD)
    r = bq // bk
    unroll = math.gcd(r, ATT_UNROLL)
    n_h, n_q = hd // TILE, s // bq
    scale = 1.0 / math.sqrt(TILE)

    def body(q_ref, k_ref, v_ref, o_ref, l_ref, acc_ref, suf_ref):
        i = pl.program_id(1)
        qb = q_ref[...]
        later = _tri(bk, "gt")
        acc_ref[...] = jnp.zeros_like(acc_ref)
        suf_ref[...] = jnp.zeros_like(suf_ref)

        def tile(j, row0):
            rows = pl.ds(pl.multiple_of(j * bk, bk), bk)
            masked = row0 is not None
            r0 = row0 if masked else 0
            rs = pl.ds(r0, bq - r0)
            mask = _causal_mask(bq - r0, bk, i * bq + r0, j * bk) if masked else None
            _, _, lb, l1m = _sb_tile(qb[r0:], k_ref[rows, :], scale, mask)
            a = jnp.exp(lb + _dot_sel2(l1m, later) + suf_ref[rs, :])
            if masked:
                a = jnp.where(mask, a, 0.0)
            acc_ref[rs, :] += jnp.dot(a.astype(BF16), v_ref[rows, :], preferred_element_type=F32)
            suf_ref[rs, :] += jnp.sum(l1m, axis=1, keepdims=True)

        for dgl in range(r - 1, -1, -1):
            tile(r * i + dgl, dgl * bk)

        def step(t, carry):
            for u in range(unroll):
                tile(r * i - 1 - (unroll * t + u), None)
            return carry

        lax.fori_loop(0, (r * i) // unroll, step, 0)
        o_ref[...] = acc_ref[...].astype(BF16)
        l_ref[...] = jnp.broadcast_to(suf_ref[...], (bq, TILE))

    blk = pl.BlockSpec((bq, TILE), lambda h, i: (i, h))
    head = pl.BlockSpec((s, TILE), lambda h, i: (0, h))
    return _hosted_call(
        body, [q, k, v], name=name, grid=(n_h, n_q), in_specs=[blk, head, head], out_specs=[blk, blk],
        out_shape=[jax.ShapeDtypeStruct((s, hd), BF16), jax.ShapeDtypeStruct((s, hd), F32)],
        scratch_shapes=[pltpu.VMEM((bq, TILE), F32), pltpu.VMEM((bq, 1), F32)],
        semantics=("parallel", "parallel"), rides=rides)


def _attn_bwd(q, k, v, do, lsum, name, rides=()):
    s, hd = q.shape
    bq, bk = _att_blocks(s, ATT_BQ_BWD)
    r = bq // bk
    unroll = math.gcd(r, ATT_UNROLL_BWD)
    n_h, n_q = hd // TILE, s // bq
    scale = 1.0 / math.sqrt(TILE)

    def body(q_ref, k_ref, v_ref, do_ref, l_ref, dq_ref, dk_ref, dv_ref, dq_acc, pre_ref, cp_ref):
        i = pl.program_id(1)

        @pl.when(i == 0)
        def _():
            dk_ref[...] = jnp.zeros_like(dk_ref)
            dv_ref[...] = jnp.zeros_like(dv_ref)

        qb = q_ref[...]
        dob = do_ref[...]
        upto = _tri(bk, "le")
        before = _tri(bk, "lt")
        dq_acc[...] = jnp.zeros_like(dq_acc)
        pre_ref[...] = jnp.zeros_like(pre_ref)
        cp_ref[...] = jnp.zeros_like(cp_ref)

        def tile(j, row0):
            rows = pl.ds(pl.multiple_of(j * bk, bk), bk)
            kb, vb = k_ref[rows, :], v_ref[rows, :]
            masked = row0 is not None
            r0 = row0 if masked else 0
            rs = pl.ds(r0, bq - r0)
            qs, dos = qb[r0:], dob[r0:]
            mask = _causal_mask(bq - r0, bk, i * bq + r0, j * bk) if masked else None
            z, e, lb, l1m = _sb_tile(qs, kb, scale, mask)
            suffix = (l_ref[rs, 0:1] - pre_ref[rs, :]) - _dot_sel2(l1m, upto)
            a = jnp.exp(lb + suffix)
            if masked:
                a = jnp.where(mask, a, 0.0)
            p = a * lax.dot_general(dos, vb, _DIMS["nt"], preferred_element_type=F32)
            both = p + (cp_ref[rs, :] + jnp.dot(p.astype(BF16), before, preferred_element_type=F32))
            sg = jnp.where(z >= 0.0, 1.0, e) * pl.reciprocal(1.0 + e, approx=True)
            dz = p - both * sg
            if masked:
                dz = jnp.where(mask, dz, 0.0)
            dz = (dz * scale).astype(BF16)
            dq_acc[rs, :] += jnp.dot(dz, kb, preferred_element_type=F32)
            dk_ref[rows, :] += lax.dot_general(dz, qs, _DIMS["tn"], preferred_element_type=F32)
            dv_ref[rows, :] += lax.dot_general(a.astype(BF16), dos, _DIMS["tn"], preferred_element_type=F32)
            pre_ref[rs, :] += jnp.sum(l1m, axis=1, keepdims=True)
            cp_ref[rs, :] += jnp.sum(p, axis=1, keepdims=True)

        def step(j, carry):
            for u in range(unroll):
                tile(unroll * j + u, None)
            return carry

        lax.fori_loop(0, (r * i) // unroll, step, 0)
        for dgl in range(r):
            tile(r * i + dgl, dgl * bk)
        dq_ref[...] = dq_acc[...].astype(BF16)

    blk = pl.BlockSpec((bq, TILE), lambda h, i: (i, h))
    head = pl.BlockSpec((s, TILE), lambda h, i: (0, h))
    return _hosted_call(
        body, [q, k, v, do, lsum], name=name, grid=(n_h, n_q), in_specs=[blk, head, head, blk, blk],
        out_specs=[blk, head, head],
        out_shape=[jax.ShapeDtypeStruct((s, hd), BF16), jax.ShapeDtypeStruct((s, hd), F32),
                   jax.ShapeDtypeStruct((s, hd), F32)],
        scratch_shapes=[pltpu.VMEM((bq, TILE), F32), pltpu.VMEM((bq, 1), F32), pltpu.VMEM((bq, 1), F32)],
        semantics=("parallel", "arbitrary"), rides=rides)


EW_BLOCK = 512 * 1024


def _ew_blocks(r, c, elems=EW_BLOCK):
    return _pick(r, max(ROWS, elems // c // ROWS * ROWS), ROWS), c


def _cast_bf16(w, layer, chip_idx, name):
    _, r, c = w.shape
    br, bc = _ew_blocks(r, c)

    def body(chip_ref, w_ref, o_ref):
        o_ref[...] = w_ref[...].astype(BF16)

    return pl.pallas_call(
        body, name=name,
        grid_spec=pltpu.PrefetchScalarGridSpec(
            num_scalar_prefetch=1, grid=(r // br, c // bc),
            in_specs=[pl.BlockSpec((None, br, bc), lambda i, j, chip_ref: (layer, i, j))],
            out_specs=pl.BlockSpec((None, br, bc), lambda i, j, chip_ref: (chip_ref[0], i, j)),
        ),
        out_shape=jax.ShapeDtypeStruct((N_CHIPS, r, c), BF16), compiler_params=_cp("parallel", "parallel"),
    )(chip_idx, w)


def _pair_add(dw, recv, c_idx, name):
    _, r, c = dw.shape
    hr = r // 2
    br, bc = _ew_blocks(hr, c)
    nb = hr // br

    def body(c_ref, a_ref, b_ref, o_ref):
        o_ref[...] = (a_ref[...].astype(F32) + b_ref[...].astype(F32)).astype(BF16)

    return pl.pallas_call(
        body, name=name,
        grid_spec=pltpu.PrefetchScalarGridSpec(
            num_scalar_prefetch=1, grid=(N_CHIPS, nb, c // bc),
            in_specs=[
                pl.BlockSpec((None, br, bc), lambda s, i, j, c_ref: (s, c_ref[0] * nb + i, j)),
                pl.BlockSpec((None, br, bc), lambda s, i, j, c_ref: (s, i, j)),
            ],
            out_specs=pl.BlockSpec((None, br, bc), lambda s, i, j, c_ref: (s, i, j)),
        ),
        out_shape=jax.ShapeDtypeStruct((N_CHIPS, hr, c), BF16),
        compiler_params=_cp("parallel", "parallel", "parallel"),
    )(c_idx, dw, recv)


def _chip_sum(parts, dest, shape, layer, c_idx, name):
    _, hr, c = parts.shape
    br, bc = _ew_blocks(hr, c, EW_BLOCK // 2)
    nb = hr // br

    def body(c_ref, p_ref, *refs):
        o_ref = refs[-1]
        acc = p_ref[0].astype(F32)
        for s in range(1, N_CHIPS):
            acc = acc + p_ref[s].astype(F32)
        o_ref[...] = acc

    in_specs = [pl.BlockSpec((N_CHIPS, br, bc), lambda i, j, c_ref: (0, i, j))]
    operands = [c_idx, parts]
    aliases = {}
    if dest is not None:
        in_specs.append(ANY)
        operands.append(dest)
        aliases = {2: 0}
    return pl.pallas_call(
        body, name=name,
        grid_spec=pltpu.PrefetchScalarGridSpec(
            num_scalar_prefetch=1, grid=(nb, c // bc), in_specs=in_specs,
            out_specs=pl.BlockSpec((None, br, bc), lambda i, j, c_ref: (layer, c_ref[0] * nb + i, j)),
        ),
        out_shape=jax.ShapeDtypeStruct(shape, F32), input_output_aliases=aliases,
        compiler_params=_cp("parallel", "parallel"),
    )(*operands)


def _adamw(w, g, m, v, name, pass_g=False):
    n_l, r, c = w.shape
    br, bc = _ew_blocks(r, c, EW_BLOCK // 2)

    def body(w_ref, g_ref, m_ref, v_ref, *out_refs):
        d_ref, mo_ref, vo_ref = out_refs[-3:]
        g = g_ref[...]
        if pass_g:
            out_refs[0][...] = g
        m = ADAM_B1 * m_ref[...] + (1.0 - ADAM_B1) * g
        v = ADAM_B2 * v_ref[...] + (1.0 - ADAM_B2) * (g * g)
        m_hat = m / (1.0 - ADAM_B1 ** ADAM_STEP)
        v_hat = v / (1.0 - ADAM_B2 ** ADAM_STEP)
        d_ref[...] = -ADAM_LR * (m_hat / (jnp.sqrt(v_hat) + ADAM_EPS) + ADAM_WD * w_ref[...])
        mo_ref[...] = m
        vo_ref[...] = v

    blk = pl.BlockSpec((None, br, bc), lambda l, i, j: (l, i, j))
    n_out = 4 if pass_g else 3
    return pl.pallas_call(
        body, name=name, grid=(n_l, r // br, c // bc), in_specs=[blk] * 4, out_specs=[blk] * n_out,
        out_shape=[jax.ShapeDtypeStruct(w.shape, F32)] * n_out,
        compiler_params=_cp("parallel", "parallel", "parallel"),
    )(w, g, m, v)


def _place():
    x, y, c = lax.axis_index("x"), lax.axis_index("y"), lax.axis_index("c")
    chips = [(1 - x, y), (x, 1 - y), (1 - x, 1 - y)]
    return x, y, c, chips


class _Ride:
    def __init__(self, reads, bufs, new, n_sems, start, finish):
        self.reads, self.bufs, self.new, self.n_sems, self.start, self.finish = reads, bufs, new, n_sems, start, finish


def _hosted_call(body, operands, *, name, grid, in_specs, out_specs, out_shape, scratch_shapes=(), semantics=(), rides=()):
    single = not isinstance(out_shape, (list, tuple))
    out_specs = [out_specs] if single else list(out_specs)
    out_shape = [out_shape] if single else list(out_shape)
    in_specs, scratch_shapes = list(in_specs), list(scratch_shapes)
    n_in, n_out, n_scr = len(in_specs), len(out_shape), len(scratch_shapes)
    extra_in, extra_out, aliases, where = [], [], {}, []
    for ride in rides:
        r0 = len(extra_in)
        extra_in += list(ride.reads)
        b0 = len(extra_in)
        extra_in += list(ride.bufs)
        ob0 = len(extra_out)
        extra_out += [jax.ShapeDtypeStruct(b.shape, b.dtype) for b in ride.bufs]
        for t in range(len(ride.bufs)):
            aliases[n_in + b0 + t] = n_out + ob0 + t
        on0 = len(extra_out)
        extra_out += list(ride.new)
        where.append((r0, len(ride.reads), ob0, len(ride.bufs), on0, len(ride.new)))
    n_ein, n_eout = len(extra_in), len(extra_out)
    sem_shapes = [pltpu.SemaphoreType.DMA((max(1, k),)) for ride in rides for k in ride.n_sems]

    def full_body(*refs):
        ins, outs, scr = refs[:n_in + n_ein], refs[n_in + n_ein:n_in + n_ein + n_out + n_eout], refs[n_in + n_ein + n_out + n_eout:]

        def run(which):
            for idx, (ride, (r0, nr, ob0, nb, on0, nn)) in enumerate(zip(rides, where)):
                fn = ride.start if which == 0 else ride.finish
                fn(ins[n_in + r0:n_in + r0 + nr], outs[n_out + ob0:n_out + ob0 + nb], outs[n_out + on0:n_out + on0 + nn],
                   *scr[n_scr + 3 * idx:n_scr + 3 * idx + 3])

        host = lambda: body(*ins[:n_in], *outs[:n_out], *scr[:n_scr])
        if not rides:
            host()
        elif not grid:
            run(0)
            host()
            run(1)
        else:
            ids = [pl.program_id(ax) for ax in range(len(grid))]
            first = functools.reduce(jnp.logical_and, [i == 0 for i in ids])
            last = functools.reduce(jnp.logical_and, [i == g - 1 for i, g in zip(ids, grid)])
            pl.when(first)(lambda: run(0))
            host()
            pl.when(last)(lambda: run(1))

    if rides:
        params = pltpu.CompilerParams(dimension_semantics=("arbitrary",) * len(grid), vmem_limit_bytes=VMEM_LIMIT)
    else:
        params = _cp(*semantics)
    outs = pl.pallas_call(
        full_body, name=name, grid=grid,
        in_specs=in_specs + [ANY] * n_ein, out_specs=out_specs + [ANY] * n_eout,
        out_shape=out_shape + extra_out, input_output_aliases=aliases,
        scratch_shapes=scratch_shapes + sem_shapes, compiler_params=params,
    )(*operands, *extra_in)
    main = outs[0] if single else list(outs[:n_out])
    rode = [(list(outs[n_out + ob0:n_out + ob0 + nb]), list(outs[n_out + on0:n_out + on0 + nn]))
            for (_, _, ob0, nb, on0, nn) in where]
    return main, rode


def _run_rides(rides, name):
    return _hosted_call(lambda: None, [], name=name, grid=(), in_specs=[], out_specs=[], out_shape=[], rides=rides)[1]


def _ride_gather(slots, part=0, n_parts=1, span=1):
    n = len(slots)
    halves = [a.shape[1] // 2 for a in slots]
    sizes = [hr // n_parts for hr in halves]
    assert part + span <= n_parts
    for a, hr, size in zip(slots, halves, sizes):
        assert a.shape[1] == 2 * hr and hr == size * n_parts and size % ROWS == 0, a.shape

    def remote(bufs, send_sems, recv_sems, i, k, slot, core, to):
        rows = bufs[i].at[slot, pl.ds(pl.multiple_of(core * halves[i] + part * sizes[i], ROWS), span * sizes[i])]
        return pltpu.make_async_remote_copy(
            src_ref=rows, dst_ref=rows, send_sem=send_sems.at[i * 6 + k], recv_sem=recv_sems.at[i * 6 + k],
            device_id=to, device_id_type=MESH)

    def start(reads, bufs, new, send_sems, recv_sems, local_sems):
        x, y, c, chips = _place()
        for i in range(n):
            for k, (px, py) in enumerate(chips):
                remote(bufs, send_sems, recv_sems, i, k, 2 * x + y, c, (px, py, c)).start()

    def finish(reads, bufs, new, send_sems, recv_sems, local_sems):
        x, y, c, chips = _place()
        cp = functools.partial(remote, bufs, send_sems, recv_sems)
        for i in range(n):
            for k, (px, py) in enumerate(chips):
                cp(i, k, 2 * px + py, c, (x, y, c)).wait_recv()
                cp(i, 3 + k, 2 * px + py, c, (x, y, 1 - c)).start()
        for i in range(n):
            for k, (px, py) in enumerate(chips):
                cp(i, 3 + k, 2 * px + py, 1 - c, (x, y, c)).wait_recv()
        for i in range(n):
            for k, (px, py) in enumerate(chips):
                cp(i, k, 2 * x + y, c, (px, py, c)).wait_send()
                cp(i, 3 + k, 2 * px + py, c, (x, y, 1 - c)).wait_send()

    return _Ride([], slots, [], (6 * n, 6 * n, 0), start, finish)


def _ride_swap(grads):
    n = len(grads)
    halves = [a.shape[1] // 2 for a in grads]

    def copies(reads, new, send_sems, recv_sems):
        x, y, c, _ = _place()
        out = []
        for i in range(n):
            rows = pl.ds(pl.multiple_of((1 - c) * halves[i], 2 * SUBLANE), halves[i])
            out.append(pltpu.make_async_remote_copy(
                src_ref=reads[i].at[:, rows, :], dst_ref=new[i], send_sem=send_sems.at[i], recv_sem=recv_sems.at[i],
                device_id=(x, y, 1 - c), device_id_type=MESH))
        return out

    def start(reads, bufs, new, send_sems, recv_sems, local_sems):
        for cp in copies(reads, new, send_sems, recv_sems):
            cp.start()

    def finish(reads, bufs, new, send_sems, recv_sems, local_sems):
        for cp in copies(reads, new, send_sems, recv_sems):
            cp.wait()

    shapes = [jax.ShapeDtypeStruct((N_CHIPS, hr, a.shape[2]), a.dtype) for a, hr in zip(grads, halves)]
    return _Ride(grads, [], shapes, (n, n, 0), start, finish)


def _ride_scatter(parts, part=0, n_parts=1, into=None, span=1):
    n = len(parts)
    sizes = [a.shape[1] // n_parts for a in parts]
    assert part + span <= n_parts
    for a, size in zip(parts, sizes):
        assert a.shape[1] == size * n_parts and size % ROWS == 0, a.shape

    def piece(ref, i, slot):
        return ref.at[slot, pl.ds(part * sizes[i], span * sizes[i])]

    def own(reads, land, local_sems, i):
        me = 2 * lax.axis_index("x") + lax.axis_index("y")
        return pltpu.make_async_copy(piece(reads[i], i, me), piece(land[i], i, me), local_sems.at[i])

    def send(reads, land, send_sems, recv_sems, i, k):
        x, y, c, chips = _place()
        px, py = chips[k]
        return pltpu.make_async_remote_copy(
            src_ref=piece(reads[i], i, 2 * px + py), dst_ref=piece(land[i], i, 2 * x + y),
            send_sem=send_sems.at[3 * i + k], recv_sem=recv_sems.at[3 * i + k],
            device_id=(px, py, c), device_id_type=MESH)

    def start(reads, bufs, new, send_sems, recv_sems, local_sems):
        land = new if into is None else bufs
        for i in range(n):
            own(reads, land, local_sems, i).start()
            for k in range(3):
                send(reads, land, send_sems, recv_sems, i, k).start()

    def finish(reads, bufs, new, send_sems, recv_sems, local_sems):
        land = new if into is None else bufs
        x, y, c, chips = _place()
        for i in range(n):
            for k, (px, py) in enumerate(chips):
                slot = piece(land[i], i, 2 * px + py)
                pltpu.make_async_remote_copy(
                    src_ref=slot, dst_ref=slot, send_sem=send_sems.at[3 * i + k], recv_sem=recv_sems.at[3 * i + k],
                    device_id=(x, y, c), device_id_type=MESH).wait_recv()
        for i in range(n):
            for k in range(3):
                send(reads, land, send_sems, recv_sems, i, k).wait_send()
            own(reads, land, local_sems, i).wait()

    shapes = [jax.ShapeDtypeStruct(a.shape, a.dtype) for a in parts]
    if into is None:
        return _Ride(parts, [], shapes, (3 * n, 3 * n, n), start, finish)
    return _Ride(parts, list(into), [], (3 * n, 3 * n, n), start, finish)


def _ride_join(grads):
    n = len(grads)

    def copy(bufs, send_sems, recv_sems, i, core, to):
        hr = grads[i].shape[1] // 2
        rows = bufs[i].at[:, pl.ds(pl.multiple_of(core * hr, SUBLANE), hr), :]
        return pltpu.make_async_remote_copy(
            src_ref=rows, dst_ref=rows, send_sem=send_sems.at[i], recv_sem=recv_sems.at[i],
            device_id=to, device_id_type=MESH)

    def start(reads, bufs, new, send_sems, recv_sems, local_sems):
        x, y, c, _ = _place()
        for i in range(n):
            copy(bufs, send_sems, recv_sems, i, c, (x, y, 1 - c)).start()

    def finish(reads, bufs, new, send_sems, recv_sems, local_sems):
        x, y, c, _ = _place()
        for i in range(n):
            copy(bufs, send_sems, recv_sems, i, 1 - c, (x, y, c)).wait_recv()
        for i in range(n):
            copy(bufs, send_sems, recv_sems, i, c, (x, y, 1 - c)).wait_send()

    return _Ride([], grads, [], (n, n, 0), start, finish)


def _all_reduce_small(packed, name, rides=()):
    r, c = packed.shape
    chunk = _pick(r, 256, ROWS)

    def body(x_ref, out_ref, gath, send_sems, recv_sems, local_sem):
        x, y, cc, chips = _place()
        me, sibling = (x, y, cc), (x, y, 1 - cc)

        def slot(px, py, pc):
            return gath.at[4 * px + 2 * py + pc]

        def copy(k, block, to, src=None):
            return pltpu.make_async_remote_copy(
                src_ref=slot(*block) if src is None else src, dst_ref=slot(*block),
                send_sem=send_sems.at[k], recv_sem=recv_sems.at[k], device_id=to, device_id_type=MESH)

        mine = pltpu.make_async_copy(x_ref, slot(*me), local_sem)
        mine.start()
        first = [copy(0, me, sibling, src=x_ref)]
        first += [copy(1 + j, me, (*chip, cc), src=x_ref) for j, chip in enumerate(chips)]
        for cp in first:
            cp.start()
        passed = [copy(4 + j, (*chip, cc), sibling) for j, chip in enumerate(chips)]
        for j, chip in enumerate(chips):
            copy(1 + j, (*chip, cc), me).wait_recv()
            passed[j].start()
        copy(0, sibling, me).wait_recv()
        for j, chip in enumerate(chips):
            copy(4 + j, (*chip, 1 - cc), me).wait_recv()
        for cp in first + passed:
            cp.wait_send()
        mine.wait()

        def add(i, carry):
            rows = pl.ds(pl.multiple_of(i * chunk, SUBLANE), chunk)
            acc = gath[0, rows, :]
            for dev in range(1, N_DEV):
                acc = acc + gath[dev, rows, :]
            out_ref[rows, :] = acc
            return carry

        lax.fori_loop(0, r // chunk, add, 0)

    return _hosted_call(
        body, [packed], name=name, grid=(), in_specs=[VMEM_SPEC], out_specs=VMEM_SPEC,
        out_shape=jax.ShapeDtypeStruct((r, c), F32),
        scratch_shapes=[pltpu.VMEM((N_DEV, r, c), F32), pltpu.SemaphoreType.DMA((7,)),
                        pltpu.SemaphoreType.DMA((7,)), pltpu.SemaphoreType.DMA],
        rides=rides)


_PACK_ROWS = 256


def _pack(arrays):
    flat = jnp.concatenate([a.reshape(-1).astype(F32) for a in arrays])
    unit = _PACK_ROWS * LANE
    total = -(-flat.shape[0] // unit) * unit
    return jnp.pad(flat, (0, total - flat.shape[0])).reshape(-1, LANE)


def _unpack(packed, shapes, lead=()):
    flat = packed.reshape(lead + (-1,))
    out, at = [], 0
    for s in shapes:
        size = math.prod(s)
        out.append(flat[..., at:at + size].reshape(lead + tuple(s)))
        at += size
    return out


def kernel(x, pre_mix_g, post_mix_g, pre_ffn_g, post_ffn_g, a_w_in, a_v_norm_g, a_w_spatial, a_b_spatial, a_w_out, kv_norm_g, w_k, w_v, b_w_q, b_w_o, ffn_w_up, ffn_conv_w, ffn_conv_b, ffn_w_down, loss_target, m_pre_mix_g, m_post_mix_g, m_pre_ffn_g, m_post_ffn_g, m_a_w_in, m_a_v_norm_g, m_a_w_spatial, m_a_b_spatial, m_a_w_out, m_kv_norm_g, m_w_k, m_w_v, m_b_w_q, m_b_w_o, m_ffn_w_up, m_ffn_conv_w, m_ffn_conv_b, m_ffn_w_down, v_pre_mix_g, v_post_mix_g, v_pre_ffn_g, v_post_ffn_g, v_a_w_in, v_a_v_norm_g, v_a_w_spatial, v_a_b_spatial, v_a_w_out, v_kv_norm_g, v_w_k, v_w_v, v_b_w_q, v_b_w_o, v_ffn_w_up, v_ffn_conv_w, v_ffn_conv_b, v_ffn_w_down):
    xi, yi, ci = lax.axis_index("x"), lax.axis_index("y"), lax.axis_index("c")
    chip = 2 * xi + yi
    c_idx = jnp.reshape(ci, (1,)).astype(jnp.int32)
    _, s, d = x.shape
    n_layers = pre_mix_g.shape[0]
    assert n_layers == 2 and a_w_in.shape[0] == 1 and b_w_q.shape[0] == 1
    d_a = a_w_out.shape[1] * N_CHIPS
    n_g = a_w_spatial.shape[1]
    ns = ffn_w_up.shape[2]
    assert a_w_spatial.shape[2] == TILE and d_a == n_g * TILE and s % TILE == 0
    h0 = x[0]
    target = loss_target[0]

    big = {
        "win": (a_w_in, m_a_w_in, v_a_w_in),
        "wout": (a_w_out, m_a_w_out, v_a_w_out),
        "wk": (w_k[None], m_w_k[None], v_w_k[None]),
        "wv": (w_v[None], m_w_v[None], v_w_v[None]),
        "wq": (b_w_q, m_b_w_q, v_b_w_q),
        "wo": (b_w_o, m_b_w_o, v_b_w_o),
        "wup": (ffn_w_up, m_ffn_w_up, v_ffn_w_up),
        "wdn": (ffn_w_down, m_ffn_w_down, v_ffn_w_down),
    }
    units = [(nm, layer) for nm in big for layer in range(big[nm][0].shape[0])]
    chip_idx = jnp.reshape(chip, (1,)).astype(jnp.int32)
    shards = [_cast_bf16(big[nm][0], layer, chip_idx, f"cast_{nm}{layer}") for nm, layer in units]
    small_sharded = _pack([a_v_norm_g, ffn_conv_w])
    small_sharded = lax.dynamic_update_index_in_dim(
        jnp.zeros((N_CHIPS,) + small_sharded.shape, F32), small_sharded, chip, 0)
    own = dict(zip(units, shards))
    full = {}

    def gather_ride(keys):
        return _ride_gather([own[key] for key in keys])

    def gathered(keys, rode):
        full.update(zip(keys, rode[0]))

    first_keys = [("win", 0)]
    (first_bufs, _), = _run_rides([_ride_gather([own[key] for key in first_keys] + [small_sharded])], "gather_first")
    full.update(zip(first_keys, first_bufs[:-1]))
    vg_parts, cw_parts = _unpack(first_bufs[-1], [a_v_norm_g.shape, ffn_conv_w.shape], lead=(N_CHIPS,))
    v_g = jnp.transpose(vg_parts, (1, 0, 2)).reshape(1, d_a)

    def rows(nm, layer=0):
        w = full[(nm, layer)]
        return w.reshape(w.shape[0] * w.shape[1], w.shape[2])

    gains = lambda g, layer: g[layer:layer + 1]
    bias = jnp.repeat(a_b_spatial[0].T, TILE, axis=1)
    w_s = a_w_spatial[0]
    kv_g = kv_norm_g[None]
    conv_w = [cw_parts[:, layer] for layer in range(n_layers)]
    conv_b = [ffn_conv_b[layer].reshape(N_CHIPS, 1, ns) for layer in range(n_layers)]

    def ffn_fwd(hn, layer, up_keys=(), act_keys=(), down_keys=()):
        a = _mm(hn, full[("wup", layer)], "nn", f"ffn_up{layer}", out_split=N_CHIPS,
                rides=[gather_ride(up_keys)] if up_keys else ())
        if up_keys:
            a, (rode,) = a
            gathered(up_keys, rode)
        hm = _ffn_act_fwd(a, conv_w[layer], conv_b[layer], f"ffn_act{layer}",
                          rides=[gather_ride(act_keys)] if act_keys else ())
        if act_keys:
            hm, (rode,) = hm
            gathered(act_keys, rode)
        f = _mm(hm, rows("wdn", layer), "nn", f"ffn_down{layer}", rides=[gather_ride(down_keys)] if down_keys else ())
        if down_keys:
            f, (rode,) = f
            gathered(down_keys, rode)
        return a, hm, f[0]

    up0 = own[("wup", 0)]
    pieces = lambda p, span: _ride_gather([up0], part=p, n_parts=8, span=span)
    hn0 = _rms_fwd(h0, gains(pre_mix_g, 0), "norm_in")
    uv, ((out_bufs, _), ((up0,), _)) = _mm(
        hn0, full[("win", 0)], "nn", "gmlp_in", out_split=N_CHIPS, rides=[gather_ride([("wout", 0)]), pieces(0, 1)])
    full[("wout", 0)] = out_bufs[0]
    gm, (((up0,), _),) = _gmlp_fwd(uv, v_g, w_s, bias, "gmlp_gate", rides=[pieces(1, 2)])
    mix0, (((up0,), _),) = _mm(gm, rows("wout"), "nn", "gmlp_out", rides=[pieces(3, 2)])
    mix0 = mix0[0]
    (h1, hn1), (((up0,), _),) = _resid_rms(
        h0, mix0, gains(post_mix_g, 0), [gains(pre_ffn_g, 0)], "resid_mix0", rides=[pieces(5, 3)])
    full[("wup", 0)] = up0
    a0, hm0, f0 = ffn_fwd(hn1, 0, up_keys=[("wdn", 0)], act_keys=[("wq", 0), ("wk", 0)],
                          down_keys=[("wv", 0), ("wo", 0)])
    h2, hn2, kvn = _resid_rms(h1, f0, gains(post_ffn_g, 0), [gains(pre_mix_g, 1), kv_g], "resid_ffn0")
    q = _mm(hn2, rows("wq"), "nn", "proj_q", out_dtype=BF16)[0]
    k = _mm(kvn, rows("wk"), "nn", "proj_k", out_dtype=BF16)[0]
    v = _mm(kvn, rows("wv"), "nn", "proj_v", out_dtype=BF16)[0]
    last_keys = [("wup", 1), ("wdn", 1)]
    (att, lsum), (rode,) = _attn_fwd(q, k, v, "attn_fwd", rides=[gather_ride(last_keys)])
    gathered(last_keys, rode)
    mix1 = _mm(att, rows("wo"), "nn", "proj_o")[0]
    h3, hn3 = _resid_rms(h2, mix1, gains(post_mix_g, 1), [gains(pre_ffn_g, 1)], "resid_mix1")
    a1, hm1, f1 = ffn_fwd(hn3, 1)
    dh4, loss_tile = _loss_head(h3, f1, gains(post_ffn_g, 1), target, "loss_head")
    loss = lax.psum(loss_tile[0, 0], ("x", "y", "c"))

    dw = {}
    dg = {}

    pair = {}
    half_done = {nm: None for nm in big}

    def swap_ride(keys):
        return _ride_swap([dw[key] for key in keys])

    def swapped(keys, rode):
        for (nm, layer), got in zip(keys, rode[1]):
            pair[(nm, layer)] = _pair_add(dw[(nm, layer)], got, c_idx, f"pair_add_{nm}{layer}")

    def scatter_ride(keys):
        return _ride_scatter([pair[key] for key in keys])

    def scattered(keys, rode):
        for (nm, layer), got in zip(keys, rode[1]):
            half_done[nm] = _chip_sum(got, half_done[nm], big[nm][0].shape, layer, c_idx, f"chip_sum_{nm}{layer}")

    def ffn_bwd(dh_out, h_in, hn, a, hm, f, layer, act_rides=()):
        df, dg[("post_ffn", layer)] = _rms_bwd_out(dh_out, f, gains(post_ffn_g, layer), f"d_norm_ffn_out{layer}")
        dwd = _mm(hm, df, "tn", f"d_w_down{layer}", out_dtype=BF16)[0]
        down, up = [("wdn", layer)], [("wup", layer)]
        dw[down[0]] = dwd.reshape(N_CHIPS, dwd.shape[0] // N_CHIPS, d)
        dhm, (rode,) = _mm(df, rows("wdn", layer), "nt", f"d_ffn_mid{layer}", out_split=2, rides=[swap_ride(down)])
        swapped(down, rode)
        (da, dg[("conv_w", layer)], dg[("conv_b", layer)]), act_rode = _ffn_act_bwd(
            a, dhm, conv_w[layer], conv_b[layer], f"d_ffn_act{layer}", rides=act_rides)
        dw[up[0]], (rode,) = _mm(hn, da, "tn", f"d_w_up{layer}", out_dtype=BF16, out_split=N_CHIPS,
                                 rides=[scatter_ride(down)])
        scattered(down, rode)
        dhn, (rode,) = _mm(da, full[("wup", layer)], "nt", f"d_ffn_in{layer}", rides=[swap_ride(up)])
        swapped(up, rode)
        return dhn[0], act_rode

    dhn3, _ = ffn_bwd(dh4, h3, hn3, a1, hm1, f1, 1)
    dh3, (dg[("pre_ffn", 1)],) = _rms_bwd_in(dh4, h3, [([dhn3], gains(pre_ffn_g, 1))], "d_norm_ffn_in1")
    dmix1, dg[("post_mix", 1)] = _rms_bwd_out(dh3, mix1, gains(post_mix_g, 1), "d_norm_mix_out1")
    dwo = _mm(att, dmix1, "tn", "d_w_o", out_dtype=BF16)[0]
    dw[("wo", 0)] = dwo.reshape(N_CHIPS, dwo.shape[0] // N_CHIPS, d)
    datt = _mm(dmix1, rows("wo"), "nt", "d_attn_out", out_dtype=BF16)[0]
    ffn1_keys = [("wup", 1)]
    (dq, dk, dv), (rode,) = _attn_bwd(q, k, v, datt, lsum, "attn_bwd", rides=[scatter_ride(ffn1_keys)])
    scattered(ffn1_keys, rode)
    for nm, act, dact in (("wq", hn2, dq), ("wk", kvn, dk), ("wv", kvn, dv)):
        g = _mm(act, dact, "tn", f"d_{nm}", out_dtype=BF16)[0]
        dw[(nm, 0)] = g.reshape(N_CHIPS, g.shape[0] // N_CHIPS, g.shape[1])
    dhn2 = _mm(dq, rows("wq"), "nt", "d_q_in")[0]
    dkvn_k = _mm(dk, rows("wk"), "nt", "d_k_in")[0]
    attn_keys = [("wo", 0), ("wq", 0), ("wk", 0), ("wv", 0)]
    dkvn_v, (rode,) = _mm(dv, rows("wv"), "nt", "d_v_in", rides=[swap_ride(attn_keys)])
    swapped(attn_keys, rode)
    dh2, (dg[("pre_mix", 1)], dg["kv"]) = _rms_bwd_in(
        dh3, h2, [([dhn2], gains(pre_mix_g, 1)), ([dkvn_k, dkvn_v[0]], kv_g)], "d_norm_mix_in1")
    dhn1, (rode,) = ffn_bwd(dh2, h1, hn1, a0, hm0, f0, 0, act_rides=[scatter_ride(attn_keys)])
    scattered(attn_keys, rode)
    up0_pair = [pair[("wup", 0)]]
    up0_landed = [None]

    def up0_piece(part, span):
        return _ride_scatter(up0_pair, part, 8, into=up0_landed[0], span=span)

    def up0_rode(rode):
        up0_landed[0] = rode[1] if up0_landed[0] is None else rode[0]

    dh1, (dg[("pre_ffn", 0)],), (rode,) = _rms_bwd_in(
        dh2, h1, [([dhn1], gains(pre_ffn_g, 0))], "d_norm_ffn_in0", rides=[up0_piece(0, 1)])
    up0_rode(rode)
    dmix0, dg[("post_mix", 0)], (rode,) = _rms_bwd_out(
        dh1, mix0, gains(post_mix_g, 0), "d_norm_mix_out0", rides=[up0_piece(1, 1)])
    up0_rode(rode)
    early = ["wq", "wk", "wv", "wo", "wdn"]
    dwout, (((joined_early, _)),) = _mm(
        gm, dmix0, "tn", "d_w_out", out_dtype=BF16, rides=[_ride_join([half_done[nm] for nm in early])])
    grads_big = dict(zip(early, joined_early))
    w_out_key, w_in_key = [("wout", 0)], [("win", 0)]
    dw[w_out_key[0]] = dwout[0].reshape(N_CHIPS, dwout.shape[1] // N_CHIPS, d)
    dgm, (rode, up0) = _mm(dmix0, rows("wout"), "nt", "d_gmlp_gate", rides=[swap_ride(w_out_key), up0_piece(2, 1)])
    swapped(w_out_key, rode)
    up0_rode(up0)
    (duv, d_ws, d_bs, d_vg), (rode,) = _gmlp_bwd(uv, dgm[0], v_g, w_s, bias, "d_gmlp", rides=[up0_piece(3, 2)])
    up0_rode(rode)
    dw[w_in_key[0]], (rode, up0) = _mm(
        hn0, duv, "tn", "d_w_in", out_dtype=BF16, out_split=N_CHIPS, rides=[scatter_ride(w_out_key), up0_piece(5, 1)])
    scattered(w_out_key, rode)
    up0_rode(up0)
    dhn0, (rode, up0) = _mm(
        duv, full[("win", 0)], "nt", "d_gmlp_in", rides=[swap_ride(w_in_key), up0_piece(6, 2)])
    swapped(w_in_key, rode)
    up0_rode(up0)
    scattered([("wup", 0)], (None, up0_landed[0]))
    dx, (dg[("pre_mix", 0)],), (rode,) = _rms_bwd_in(
        dh1, h0, [([dhn0[0]], gains(pre_mix_g, 0))], "d_norm_in", rides=[scatter_ride(w_in_key)])
    scattered(w_in_key, rode)

    stack = lambda key: jnp.concatenate([dg[(key, layer)] for layer in range(n_layers)], axis=0)
    small_parts = [
        stack("pre_mix"), stack("post_mix"), stack("pre_ffn"), stack("post_ffn"),
        d_vg, d_ws, d_bs[::SUBLANE], dg["kv"],
        jnp.stack([dg[("conv_w", layer)] for layer in range(n_layers)]),
        jnp.stack([dg[("conv_b", layer)] for layer in range(n_layers)]),
    ]
    late = [nm for nm in big if nm not in early]
    summed, ((joined_late, _),) = _all_reduce_small(
        _pack(small_parts), "small_grads_sum", rides=[_ride_join([half_done[nm] for nm in late])])
    grads_big.update(zip(late, joined_late))
    (g_pre_mix, g_post_mix, g_pre_ffn, g_post_ffn, g_vg, g_ws, g_bs, g_kv, g_cw, g_cb) = _unpack(
        summed, [p.shape for p in small_parts])
    g_vg = lax.dynamic_index_in_dim(g_vg.reshape(N_CHIPS, 1, d_a // N_CHIPS), chip, 0, keepdims=False)
    g_cw = lax.dynamic_index_in_dim(g_cw, chip, 1, keepdims=False)
    g_cb = g_cb.reshape(n_layers, N_CHIPS * ns)
    small = [
        (pre_mix_g, g_pre_mix, m_pre_mix_g, v_pre_mix_g),
        (post_mix_g, g_post_mix, m_post_mix_g, v_post_mix_g),
        (pre_ffn_g, g_pre_ffn, m_pre_ffn_g, v_pre_ffn_g),
        (post_ffn_g, g_post_ffn, m_post_ffn_g, v_post_ffn_g),
        (a_v_norm_g, g_vg, m_a_v_norm_g, v_a_v_norm_g),
        (a_w_spatial, g_ws[None], m_a_w_spatial, v_a_w_spatial),
        (a_b_spatial, g_bs[None], m_a_b_spatial, v_a_b_spatial),
        (kv_norm_g, g_kv.reshape(d), m_kv_norm_g, v_kv_norm_g),
        (ffn_conv_w, g_cw, m_ffn_conv_w, v_ffn_conv_w),
        (ffn_conv_b, g_cb, m_ffn_conv_b, v_ffn_conv_b),
    ]
    small = [(w, g.reshape(w.shape), m, v) for w, g, m, v in small]
    packed = [_pack([t[i] for t in small])[None] for i in range(4)]
    small_new = [_unpack(p[0], [t[0].shape for t in small]) for p in _adamw(*packed, "adamw_small")]

    new_big = {nm: _adamw(big[nm][0], grads_big[nm], big[nm][1], big[nm][2], f"adamw_{nm}", pass_g=True)
               for nm in big}

    def big_out(nm, which):
        ref_shape = {"wk": w_k.shape, "wv": w_v.shape}.get(nm, big[nm][0].shape)
        return new_big[nm][which].reshape(ref_shape)

    order = ["pre_mix", "post_mix", "pre_ffn", "post_ffn", "win", "vg", "ws", "bs", "wout", "kv", "wk", "wv", "wq",
             "wo", "wup", "cw", "cb", "wdn"]
    small_at = {"pre_mix": 0, "post_mix": 1, "pre_ffn": 2, "post_ffn": 3, "vg": 4, "ws": 5, "bs": 6, "kv": 7,
                "cw": 8, "cb": 9}
    outs = [loss, dx[None]]
    for which in range(4):
        for nm in order:
            if nm in small_at:
                outs.append(small[small_at[nm]][1] if which == 0 else small_new[which - 1][small_at[nm]])
            else:
                outs.append(big_out(nm, which))
    return tuple(outs)
```

```python
import functools
import math

import jax
import jax.numpy as jnp
from jax import lax
from jax.experimental import pallas as pl
from jax.experimental.pallas import tpu as pltpu

F32 = jnp.float32
BF16 = jnp.bfloat16
EPS = 1e-6
ADAM_LR = 0.001
ADAM_B1 = 0.9
ADAM_B2 = 0.999
ADAM_EPS = 1e-08
ADAM_WD = 0.01
ADAM_STEP = 10

LANE = 128
SUBLANE = 8
ROWS = 16
TILE = 128
N_CHIPS = 4
N_DEV = 8
VMEM_LIMIT = 56 * 1024 * 1024
MM_VMEM = 46 * 1024 * 1024
MXU_WIDTH = 256
MESH = pl.DeviceIdType.MESH
ANY = pl.BlockSpec(memory_space=pl.ANY)
VMEM_SPEC = pl.BlockSpec(memory_space=pltpu.VMEM)


def _cp(*sem):
    return pltpu.CompilerParams(dimension_semantics=sem, vmem_limit_bytes=VMEM_LIMIT)


def _pick(dim, pref, align=LANE):
    if dim <= pref:
        return dim
    best = None
    for d in range(align, pref + 1, align):
        if dim % d == 0:
            best = d
    assert best is not None, (dim, pref)
    return best


_DIMS = {
    "nn": (((1,), (0,)), ((), ())),
    "nt": (((1,), (1,)), ((), ())),
    "tn": (((0,), (0,)), ((), ())),
}


def _as3(a):
    return a if a.ndim == 3 else a[None]


def _spec3(br, bc, cols_j, rc):
    per = cols_j // bc

    def imap(m, n, k):
        r, c = rc(m, n, k)
        return (c // per, r, c % per)

    return pl.BlockSpec((None, br, bc), imap)


def _mm(a, b, mode, name, out_dtype=F32, out_split=1, rides=()):
    a, b = _as3(a), _as3(b)
    ja, ra, caj = a.shape
    jb, rb, cbj = b.shape
    if mode == "nn":
        m, k, n = ra, ja * caj, jb * cbj
        assert rb == k
        m_ext, k_ext, n_ext = [ra], [caj, rb], [cbj]
    elif mode == "nt":
        m, k, n = ra, ja * caj, rb
        assert jb * cbj == k
        m_ext, k_ext, n_ext = [ra], [caj, cbj], [rb]
    else:
        m, k, n = ja * caj, ra, jb * cbj
        assert rb == k
        m_ext, k_ext, n_ext = [caj], [ra], [cbj]
    assert n % out_split == 0
    n_ext.append(n // out_split)
    bm = _pick(math.gcd(*m_ext), 1536)
    n_unit = math.gcd(*n_ext)
    bn = _pick(n_unit, 1536)
    k_unit = math.gcd(*k_ext)
    o_bytes = jnp.dtype(out_dtype).itemsize

    def vmem_need(bm, bn, bk):
        tiles = bm * bk * a.dtype.itemsize + bk * bn * b.dtype.itemsize + bm * bn * o_bytes
        return 2 * tiles + bm * bn * 4 * (2 if bk < k else 1)

    def deepest(bm, bn):
        return max(d for d in range(LANE, k_unit + 1, LANE)
                   if k_unit % d == 0 and (d == LANE or vmem_need(bm, bn, d) <= MM_VMEM))

    bk = deepest(bm, bn)
    if bk < k_unit and k_unit == k:
        if bm % (2 * LANE) == 0 and deepest(bm // 2, bn) == k:
            bm, bk = bm // 2, k
        elif bn % (2 * LANE) == 0 and deepest(bm, bn // 2) == k:
            bn, bk = bn // 2, k
    n_outer = False
    if bn % MXU_WIDTH and n_unit % MXU_WIDTH == 0 and k_unit == k:
        for rows in (bm, bm // 2, bm // 4):
            if rows % LANE == 0 and vmem_need(rows, n_unit, k) <= MM_VMEM:
                bm, bn, bk, n_outer = rows, n_unit, k, True
                break
    nk = k // bk
    order = (lambda f: lambda ni, mi, ki: f(mi, ni, ki)) if n_outer else (lambda f: f)
    if mode == "nn":
        a_spec = _spec3(bm, bk, caj, order(lambda mi, ni, ki: (mi, ki)))
        b_spec = _spec3(bk, bn, cbj, order(lambda mi, ni, ki: (ki, ni)))
    elif mode == "nt":
        a_spec = _spec3(bm, bk, caj, order(lambda mi, ni, ki: (mi, ki)))
        b_spec = _spec3(bn, bk, cbj, order(lambda mi, ni, ki: (ni, ki)))
    else:
        a_spec = _spec3(bk, bm, caj, order(lambda mi, ni, ki: (ki, mi)))
        b_spec = _spec3(bk, bn, cbj, order(lambda mi, ni, ki: (ki, ni)))
    o_spec = _spec3(bm, bn, n // out_split, order(lambda mi, ni, ki: (mi, ni)))
    dims = _DIMS[mode]

    def body(a_ref, b_ref, o_ref, *acc):
        def part():
            return lax.dot_general(a_ref[...].astype(BF16), b_ref[...].astype(BF16), dims, preferred_element_type=F32)

        if nk == 1:
            o_ref[...] = part().astype(o_ref.dtype)
            return
        acc_ref, = acc
        ki = pl.program_id(2)

        @pl.when(ki == 0)
        def _():
            acc_ref[...] = part()

        @pl.when(jnp.logical_and(ki > 0, ki < nk - 1))
        def _():
            acc_ref[...] += part()

        @pl.when(ki == nk - 1)
        def _():
            o_ref[...] = (acc_ref[...] + part()).astype(o_ref.dtype)

    grid = (n // bn, m // bm, nk) if n_outer else (m // bm, n // bn, nk)
    out, rode = _hosted_call(
        body, [a, b], name=name, grid=grid, in_specs=[a_spec, b_spec], out_specs=o_spec,
        out_shape=jax.ShapeDtypeStruct((out_split, m, n // out_split), out_dtype),
        scratch_shapes=[pltpu.VMEM((bm, bn), F32)] if nk > 1 else [],
        semantics=("parallel", "parallel", "arbitrary"), rides=rides)
    return (out, rode) if rides else out


def _rms(x, g):
    r = lax.rsqrt(jnp.mean(x * x, axis=-1, keepdims=True) + EPS)
    return x * r * g


def _rms_bwd(x, g, dy):
    r = lax.rsqrt(jnp.mean(x * x, axis=-1, keepdims=True) + EPS)
    xh = x * r
    gy = dy * g
    dx = r * (gy - xh * jnp.mean(gy * xh, axis=-1, keepdims=True))
    return dx, jnp.sum(dy * xh, axis=0, keepdims=True)


def _row_block(s):
    return _pick(s, 256, ROWS)


def _rms_fwd(h, g, name):
    s, d = h.shape
    br = _row_block(s)

    def body(h_ref, g_ref, o_ref):
        o_ref[...] = _rms(h_ref[...], g_ref[...]).astype(BF16)

    row = pl.BlockSpec((br, d), lambda i: (i, 0))
    vec = pl.BlockSpec((1, d), lambda i: (0, 0))
    return pl.pallas_call(
        body, name=name, grid=(s // br,), in_specs=[row, vec], out_specs=row,
        out_shape=jax.ShapeDtypeStruct((s, d), BF16), compiler_params=_cp("parallel"),
    )(h, g)


def _resid_rms(h_in, f, g_post, g_next, name, rides=()):
    s, d = h_in.shape
    br = _row_block(s)
    n_next = len(g_next)

    def body(h_ref, f_ref, gp_ref, *refs):
        gn_refs, ho_ref, hn_refs = refs[:n_next], refs[n_next], refs[n_next + 1:]
        h = h_ref[...] + _rms(f_ref[...], gp_ref[...])
        ho_ref[...] = h
        for gn_ref, hn_ref in zip(gn_refs, hn_refs):
            hn_ref[...] = _rms(h, gn_ref[...]).astype(BF16)

    row = pl.BlockSpec((br, d), lambda i: (i, 0))
    vec = pl.BlockSpec((1, d), lambda i: (0, 0))
    outs, rode = _hosted_call(
        body, [h_in, f, g_post, *g_next], name=name, grid=(s // br,),
        in_specs=[row, row, vec] + [vec] * n_next,
        out_specs=[row] * (1 + n_next),
        out_shape=[jax.ShapeDtypeStruct((s, d), F32)] + [jax.ShapeDtypeStruct((s, d), BF16)] * n_next,
        semantics=("parallel",), rides=rides)
    return (outs, rode) if rides else outs


def _loss_head(h_in, f, g_post, target, name):
    s, d = h_in.shape
    br = _row_block(s)

    def body(h_ref, f_ref, gp_ref, t_ref, dh_ref, loss_ref):
        @pl.when(pl.program_id(0) == 0)
        def _():
            loss_ref[...] = jnp.zeros_like(loss_ref)

        diff = h_ref[...] + _rms(f_ref[...], gp_ref[...]) - t_ref[...]
        dh_ref[...] = diff * (1.0 / d)
        loss_ref[...] += 0.5 * jnp.sum(jnp.mean(diff * diff, axis=-1, keepdims=True))

    row = pl.BlockSpec((br, d), lambda i: (i, 0))
    vec = pl.BlockSpec((1, d), lambda i: (0, 0))
    return pl.pallas_call(
        body, name=name, grid=(s // br,),
        in_specs=[row, row, vec, row],
        out_specs=[row, pl.BlockSpec((SUBLANE, LANE), lambda i: (0, 0))],
        out_shape=[jax.ShapeDtypeStruct((s, d), F32), jax.ShapeDtypeStruct((SUBLANE, LANE), F32)],
        compiler_params=_cp("arbitrary"),
    )(h_in, f, g_post, target)


def _rms_bwd_out(dy, f, g, name, rides=()):
    s, d = f.shape
    br = _row_block(s)

    def body(dy_ref, f_ref, g_ref, df_ref, dg_ref):
        @pl.when(pl.program_id(0) == 0)
        def _():
            dg_ref[...] = jnp.zeros_like(dg_ref)

        dx, dg = _rms_bwd(f_ref[...], g_ref[...], dy_ref[...])
        df_ref[...] = dx.astype(BF16)
        dg_ref[...] += dg

    row = pl.BlockSpec((br, d), lambda i: (i, 0))
    vec = pl.BlockSpec((1, d), lambda i: (0, 0))
    (df, dg), rode = _hosted_call(
        body, [dy, f, g], name=name, grid=(s // br,), in_specs=[row, row, vec], out_specs=[row, vec],
        out_shape=[jax.ShapeDtypeStruct((s, d), BF16), jax.ShapeDtypeStruct((1, d), F32)],
        semantics=("arbitrary",), rides=rides)
    return (df, dg, rode) if rides else (df, dg)


def _rms_bwd_in(dh_out, h_in, branches, name, rides=()):
    s, d = h_in.shape
    br = _row_block(s)
    counts = [len(ds) for ds, _ in branches]
    n_d = sum(counts)
    n_b = len(branches)

    def body(dho_ref, h_ref, *refs):
        d_refs, g_refs = refs[:n_d], refs[n_d:n_d + n_b]
        dh_ref, dg_refs = refs[n_d + n_b], refs[n_d + n_b + 1:]

        @pl.when(pl.program_id(0) == 0)
        def _():
            for r in dg_refs:
                r[...] = jnp.zeros_like(r)

        h = h_ref[...]
        acc = dho_ref[...]
        at = 0
        for bi, cnt in enumerate(counts):
            dn = d_refs[at][...]
            for r in d_refs[at + 1:at + cnt]:
                dn = dn + r[...]
            at += cnt
            dx, dg = _rms_bwd(h, g_refs[bi][...], dn)
            acc = acc + dx
            dg_refs[bi][...] += dg
        dh_ref[...] = acc

    row = pl.BlockSpec((br, d), lambda i: (i, 0))
    vec = pl.BlockSpec((1, d), lambda i: (0, 0))
    flat_d = [x for ds, _ in branches for x in ds]
    outs, rode = _hosted_call(
        body, [dh_out, h_in, *flat_d, *[g for _, g in branches]], name=name, grid=(s // br,),
        in_specs=[row, row] + [row] * n_d + [vec] * n_b,
        out_specs=[row] + [vec] * n_b,
        out_shape=[jax.ShapeDtypeStruct((s, d), F32)] + [jax.ShapeDtypeStruct((1, d), F32)] * n_b,
        semantics=("arbitrary",), rides=rides)
    return (outs[0], list(outs[1:]), rode) if rides else (outs[0], list(outs[1:]))


def _split3(x):
    x0 = x.astype(BF16)
    r1 = x - x0.astype(F32)
    x1 = r1.astype(BF16)
    x2 = (r1 - x1.astype(F32)).astype(BF16)
    return x0, x1, x2


def _tri(n, kind):
    r = lax.broadcasted_iota(jnp.int32, (n, n), 0)
    c = lax.broadcasted_iota(jnp.int32, (n, n), 1)
    m = {"lt": r < c, "le": r <= c, "gt": r > c}[kind]
    return jnp.where(m, 1.0, 0.0).astype(BF16)


_GELU_C = math.sqrt(2.0 / math.pi)
_GELU_A = 0.044715


def _gelu(x):
    return 0.5 * x * (1.0 + jnp.tanh(_GELU_C * (x + _GELU_A * (x * x * x))))


def _gelu_grad(x):
    t = jnp.tanh(_GELU_C * (x + _GELU_A * (x * x * x)))
    return 0.5 * (1.0 + t) + 0.5 * x * (1.0 - t * t) * (_GELU_C * (1.0 + 3.0 * _GELU_A * (x * x)))


def _causal_w(w):
    r = lax.broadcasted_iota(jnp.int32, (TILE, TILE), 0)
    c = lax.broadcasted_iota(jnp.int32, (TILE, TILE), 1)
    return jnp.where(c <= r, w, 0.0)


def _uv_tiles(uv_ref, g, d_a, dq):
    cu, cv = g * TILE, d_a + g * TILE
    u = uv_ref[cu // dq, :, pl.ds(cu % dq, TILE)]
    v = uv_ref[cv // dq, :, pl.ds(cv % dq, TILE)]
    return u, v


def _gmlp_fwd(uv, v_g, w_s, bias, name, rides=()):
    _, s, dq = uv.shape
    d_a = 2 * dq
    n_g = d_a // TILE

    def body(uv_ref, vg_ref, ws_ref, b_ref, o_ref):
        for g in range(n_g):
            up, vp = _uv_tiles(uv_ref, g, d_a, dq)
            cols = pl.ds(g * TILE, TILE)
            vn = _rms(_gelu(vp), vg_ref[:, cols])
            mixed = jnp.dot(_causal_w(ws_ref[g]).astype(BF16), vn.astype(BF16), preferred_element_type=F32) + b_ref[:, cols]
            o_ref[:, cols] = (_gelu(up) * mixed).astype(BF16)

    return _hosted_call(
        body, [uv, v_g, w_s, bias], name=name, grid=(s // TILE,),
        in_specs=[
            pl.BlockSpec((4, TILE, dq), lambda i: (0, i, 0)),
            pl.BlockSpec((1, d_a), lambda i: (0, 0)),
            pl.BlockSpec((n_g, TILE, TILE), lambda i: (0, 0, 0)),
            pl.BlockSpec((TILE, d_a), lambda i: (0, 0)),
        ],
        out_specs=pl.BlockSpec((TILE, d_a), lambda i: (i, 0)),
        out_shape=jax.ShapeDtypeStruct((s, d_a), BF16),
        semantics=("parallel",), rides=rides)


def _gmlp_bwd(uv, dgm, v_g, w_s, bias, name, rides=()):
    _, s, dq = uv.shape
    d_a = 2 * dq
    n_g = d_a // TILE
    n_c = s // TILE

    def body(uv_ref, d_ref, vg_ref, ws_ref, b_ref, duv_ref, dws_ref, dbs_ref, dvg_ref, dbias_acc):
        i = pl.program_id(0)

        @pl.when(i == 0)
        def _():
            dws_ref[...] = jnp.zeros_like(dws_ref)
            dvg_ref[...] = jnp.zeros_like(dvg_ref)
            dbias_acc[...] = jnp.zeros_like(dbias_acc)

        for g in range(n_g):
            up, vp = _uv_tiles(uv_ref, g, d_a, dq)
            cols = pl.ds(g * TILE, TILE)
            vg = vg_ref[:, cols]
            u = _gelu(up)
            v = _gelu(vp)
            r = lax.rsqrt(jnp.mean(v * v, axis=-1, keepdims=True) + EPS)
            vh = v * r
            vn = (vh * vg).astype(BF16)
            wc = _causal_w(ws_ref[g]).astype(BF16)
            mixed = jnp.dot(wc, vn, preferred_element_type=F32) + b_ref[:, cols]
            d_out = d_ref[:, cols]
            du = d_out * mixed
            dmixed = d_out * u
            dmb = dmixed.astype(BF16)
            dvn = lax.dot_general(wc, dmb, _DIMS["tn"], preferred_element_type=F32)
            dws_ref[g] += lax.dot_general(dmb, vn, _DIMS["nt"], preferred_element_type=F32)
            dbias_acc[:, cols] += dmixed
            dvg_ref[:, cols] += jnp.sum(dvn * vh, axis=0, keepdims=True)
            gv = dvn * vg
            dv = r * (gv - vh * jnp.mean(gv * vh, axis=-1, keepdims=True))
            cu, cv = g * TILE, d_a + g * TILE
            duv_ref[cu // dq, :, pl.ds(cu % dq, TILE)] = (du * _gelu_grad(up)).astype(BF16)
            duv_ref[cv // dq, :, pl.ds(cv % dq, TILE)] = (dv * _gelu_grad(vp)).astype(BF16)

        @pl.when(i == n_c - 1)
        def _():
            ones = jnp.ones((SUBLANE, TILE), BF16)
            for g in range(n_g):
                dws_ref[g] = _causal_w(dws_ref[g])
                cols = pl.ds(g * TILE, TILE)
                out = None
                for t in _split3(dbias_acc[:, cols]):
                    p = lax.dot_general(ones, t, _DIMS["nt"], preferred_element_type=F32)
                    out = p if out is None else out + p
                dbs_ref[pl.ds(g * SUBLANE, SUBLANE), :] = out

    return _hosted_call(
        body, [uv, dgm, v_g, w_s, bias], name=name, grid=(n_c,), semantics=("arbitrary",), rides=rides,
        in_specs=[
            pl.BlockSpec((4, TILE, dq), lambda i: (0, i, 0)),
            pl.BlockSpec((TILE, d_a), lambda i: (i, 0)),
            pl.BlockSpec((1, d_a), lambda i: (0, 0)),
            pl.BlockSpec((n_g, TILE, TILE), lambda i: (0, 0, 0)),
            pl.BlockSpec((TILE, d_a), lambda i: (0, 0)),
        ],
        out_specs=[
            pl.BlockSpec((4, TILE, dq), lambda i: (0, i, 0)),
            pl.BlockSpec((n_g, TILE, TILE), lambda i: (0, 0, 0)),
            pl.BlockSpec((n_g * SUBLANE, TILE), lambda i: (0, 0)),
            pl.BlockSpec((1, d_a), lambda i: (0, 0)),
        ],
        out_shape=[
            jax.ShapeDtypeStruct((4, s, dq), BF16),
            jax.ShapeDtypeStruct((n_g, TILE, TILE), F32),
            jax.ShapeDtypeStruct((n_g * SUBLANE, TILE), F32),
            jax.ShapeDtypeStruct((1, d_a), F32),
        ],
        scratch_shapes=[pltpu.VMEM((TILE, d_a), F32)])


def _sigmoid(x):
    return 1.0 / (1.0 + jnp.exp(-x))


def _conv3(ext, w, b):
    return b + ((w[0:1] * pltpu.roll(ext, 2, 0) + w[1:2] * pltpu.roll(ext, 1, 0)) + w[2:3] * ext)


def _act_blocks(s, ns):
    return _pick(s, 512, ROWS), _pick(ns, 256)


def _ffn_act_fwd(a, cw, cb, name, rides=()):
    _, s, ns = a.shape
    bs, cb_w = _act_blocks(s, ns)
    hb = bs // SUBLANE

    def body(a_ref, prev_ref, cw_ref, cb_ref, o_ref):
        first = pl.program_id(0) == 0

        def conv(comp):
            prev = jnp.where(first, 0.0, prev_ref[comp])
            ext = jnp.concatenate([prev, a_ref[comp]], axis=0)
            return _conv3(ext, cw_ref[comp], cb_ref[comp])[SUBLANE:]

        for p in range(2):
            cg = conv(p)
            o_ref[p] = (cg * _sigmoid(cg) * conv(2 + p)).astype(BF16)

    hm, rode = _hosted_call(
        body, [a, a, cw, cb], name=name, grid=(s // bs, ns // cb_w),
        in_specs=[
            pl.BlockSpec((4, bs, cb_w), lambda i, j: (0, i, j)),
            pl.BlockSpec((4, SUBLANE, cb_w), lambda i, j: (0, jnp.maximum(i * hb - 1, 0), j)),
            pl.BlockSpec((4, 3, cb_w), lambda i, j: (0, 0, j)),
            pl.BlockSpec((4, 1, cb_w), lambda i, j: (0, 0, j)),
        ],
        out_specs=pl.BlockSpec((2, bs, cb_w), lambda i, j: (0, i, j)),
        out_shape=jax.ShapeDtypeStruct((2, s, ns), BF16),
        semantics=("parallel", "parallel"), rides=rides)
    return (hm, rode) if rides else hm


def _ffn_act_bwd(a, dhm, cw, cb, name, rides=()):
    _, s, ns = a.shape
    bs, cb_w = _act_blocks(s, ns)
    hb = bs // SUBLANE
    n_i = s // bs
    n_ext = bs + 2 * SUBLANE
    cur = slice(SUBLANE, SUBLANE + bs)

    def body(a_ref, prev_ref, next_ref, d_ref, dnext_ref, cw_ref, cb_ref, da_ref, dcw_ref, dcb_ref):
        i = pl.program_id(1)
        first, last = i == 0, i == n_i - 1

        @pl.when(first)
        def _():
            dcw_ref[...] = jnp.zeros_like(dcw_ref)
            dcb_ref[...] = jnp.zeros_like(dcb_ref)

        def ext_of(comp):
            return jnp.concatenate([jnp.where(first, 0.0, prev_ref[comp]), a_ref[comp], next_ref[comp]], axis=0)

        def back(comp, a_ext, dc):
            w = cw_ref[comp]
            da = (w[2:3] * dc + w[1:2] * pltpu.roll(dc, n_ext - 1, 0)) + w[0:1] * pltpu.roll(dc, n_ext - 2, 0)
            da_ref[comp] = da[cur].astype(BF16)
            dcc = dc[cur]
            dcw_ref[comp, 0:1, :] += jnp.sum(dcc * pltpu.roll(a_ext, 2, 0)[cur], axis=0, keepdims=True)
            dcw_ref[comp, 1:2, :] += jnp.sum(dcc * pltpu.roll(a_ext, 1, 0)[cur], axis=0, keepdims=True)
            dcw_ref[comp, 2:3, :] += jnp.sum(dcc * a_ext[cur], axis=0, keepdims=True)
            dcb_ref[comp] += jnp.sum(dcc, axis=0, keepdims=True)

        for p in range(2):
            ag, av = ext_of(p), ext_of(2 + p)
            cg = _conv3(ag, cw_ref[p], cb_ref[p])
            cv = _conv3(av, cw_ref[2 + p], cb_ref[2 + p])
            d = jnp.concatenate(
                [jnp.zeros((SUBLANE, cb_w), F32), d_ref[p], jnp.where(last, 0.0, dnext_ref[p])], axis=0)
            sg = _sigmoid(cg)
            back(2 + p, av, d * (cg * sg))
            back(p, ag, d * cv * (sg * (1.0 + cg * (1.0 - sg))))

    return _hosted_call(
        body, [a, a, a, dhm, dhm, cw, cb], name=name, grid=(ns // cb_w, n_i),
        in_specs=[
            pl.BlockSpec((4, bs, cb_w), lambda j, i: (0, i, j)),
            pl.BlockSpec((4, SUBLANE, cb_w), lambda j, i: (0, jnp.maximum(i * hb - 1, 0), j)),
            pl.BlockSpec((4, SUBLANE, cb_w), lambda j, i: (0, jnp.minimum((i + 1) * hb, n_i * hb - 1), j)),
            pl.BlockSpec((2, bs, cb_w), lambda j, i: (0, i, j)),
            pl.BlockSpec((2, SUBLANE, cb_w), lambda j, i: (0, jnp.minimum((i + 1) * hb, n_i * hb - 1), j)),
            pl.BlockSpec((4, 3, cb_w), lambda j, i: (0, 0, j)),
            pl.BlockSpec((4, 1, cb_w), lambda j, i: (0, 0, j)),
        ],
        out_specs=[
            pl.BlockSpec((4, bs, cb_w), lambda j, i: (0, i, j)),
            pl.BlockSpec((4, 3, cb_w), lambda j, i: (0, 0, j)),
            pl.BlockSpec((4, 1, cb_w), lambda j, i: (0, 0, j)),
        ],
        out_shape=[
            jax.ShapeDtypeStruct((4, s, ns), BF16),
            jax.ShapeDtypeStruct((4, 3, ns), F32),
            jax.ShapeDtypeStruct((4, 1, ns), F32),
        ],
        semantics=("parallel", "arbitrary"), rides=rides)


ATT_BQ_FWD = 2048
ATT_BQ_BWD = 1024
ATT_BK = 256
ATT_UNROLL = 2
ATT_UNROLL_BWD = 4


def _att_blocks(s, bq_pref):
    bq = _pick(s, bq_pref)
    bk = min(ATT_BK, bq)
    assert bq % bk == 0
    return bq, bk


def _dot_sel2(x, sel):
    hi = x.astype(BF16)
    lo = (x - hi.astype(F32)).astype(BF16)
    n = x.shape[0]
    both = jnp.dot(jnp.concatenate([hi, lo], axis=0), sel, preferred_element_type=F32)
    return both[:n] + both[n:]


def _causal_mask(bq, bk, row0, col0):
    rows = row0 + lax.broadcasted_iota(jnp.int32, (bq, bk), 0)
    cols = col0 + lax.broadcasted_iota(jnp.int32, (bq, bk), 1)
    return cols < rows


def _sb_tile(qb, kb, scale, mask):
    z = lax.dot_general(qb, kb, _DIMS["nt"], preferred_element_type=F32) * scale
    e = jnp.exp(-jnp.abs(z))
    lb = jnp.minimum(z, 0.0) - jnp.log(1.0 + e)
    l1m = lb - z
    if mask is not None:
        l1m = jnp.where(mask, l1m, 0.0)
    return z, e, lb, l1m


def _attn_fwd(q, k, v, name, rides=()):
    s, hd = q.shape
    bq, bk = _att_blocks(s, ATT_BQ_FWD)
    r = bq // bk
    unroll = math.gcd(r, ATT_UNROLL)
    n_h, n_q = hd // TILE, s // bq
    scale = 1.0 / math.sqrt(TILE)

    def body(q_ref, k_ref, v_ref, o_ref, l_ref, acc_ref, suf_ref):
        i = pl.program_id(1)
        qb = q_ref[...]
        later = _tri(bk, "gt")
        acc_ref[...] = jnp.zeros_like(acc_ref)
        suf_ref[...] = jnp.zeros_like(suf_ref)

        def tile(j, row0):
            rows = pl.ds(pl.multiple_of(j * bk, bk), bk)
            masked = row0 is not None
            r0 = row0 if masked else 0
            rs = pl.ds(r0, bq - r0)
            mask = _causal_mask(bq - r0, bk, i * bq + r0, j * bk) if masked else None
            _, _, lb, l1m = _sb_tile(qb[r0:], k_ref[rows, :], scale, mask)
            a = jnp.exp(lb + _dot_sel2(l1m, later) + suf_ref[rs, :])
            if masked:
                a = jnp.where(mask, a, 0.0)
            acc_ref[rs, :] += jnp.dot(a.astype(BF16), v_ref[rows, :], preferred_element_type=F32)
            suf_ref[rs, :] += jnp.sum(l1m, axis=1, keepdims=True)

        for dgl in range(r - 1, -1, -1):
            tile(r * i + dgl, dgl * bk)

        def step(t, carry):
            for u in range(unroll):
                tile(r * i - 1 - (unroll * t + u), None)
            return carry

        lax.fori_loop(0, (r * i) // unroll, step, 0)
        o_ref[...] = acc_ref[...].astype(BF16)
        l_ref[...] = jnp.broadcast_to(suf_ref[...], (bq, TILE))

    blk = pl.BlockSpec((bq, TILE), lambda h, i: (i, h))
    head = pl.BlockSpec((s, TILE), lambda h, i: (0, h))
    return _hosted_call(
        body, [q, k, v], name=name, grid=(n_h, n_q), in_specs=[blk, head, head], out_specs=[blk, blk],
        out_shape=[jax.ShapeDtypeStruct((s, hd), BF16), jax.ShapeDtypeStruct((s, hd), F32)],
        scratch_shapes=[pltpu.VMEM((bq, TILE), F32), pltpu.VMEM((bq, 1), F32)],
        semantics=("parallel", "parallel"), rides=rides)


def _attn_bwd(q, k, v, do, lsum, name, rides=()):
    s, hd = q.shape
    bq, bk = _att_blocks(s, ATT_BQ_BWD)
    r = bq // bk
    unroll = math.gcd(r, ATT_UNROLL_BWD)
    n_h, n_q = hd // TILE, s // bq
    scale = 1.0 / math.sqrt(TILE)

    def body(q_ref, k_ref, v_ref, do_ref, l_ref, dq_ref, dk_ref, dv_ref, dq_acc, pre_ref, cp_ref):
        i = pl.program_id(1)

        @pl.when(i == 0)
        def _():
            dk_ref[...] = jnp.zeros_like(dk_ref)
            dv_ref[...] = jnp.zeros_like(dv_ref)

        qb = q_ref[...]
        dob = do_ref[...]
        upto = _tri(bk, "le")
        before = _tri(bk, "lt")
        dq_acc[...] = jnp.zeros_like(dq_acc)
        pre_ref[...] = jnp.zeros_like(pre_ref)
        cp_ref[...] = jnp.zeros_like(cp_ref)

        def tile(j, row0):
            rows = pl.ds(pl.multiple_of(j * bk, bk), bk)
            kb, vb = k_ref[rows, :], v_ref[rows, :]
            masked = row0 is not None
            r0 = row0 if masked else 0
            rs = pl.ds(r0, bq - r0)
            qs, dos = qb[r0:], dob[r0:]
            mask = _causal_mask(bq - r0, bk, i * bq + r0, j * bk) if masked else None
            z, e, lb, l1m = _sb_tile(qs, kb, scale, mask)
            suffix = (l_ref[rs, 0:1] - pre_ref[rs, :]) - _dot_sel2(l1m, upto)
            a = jnp.exp(lb + suffix)
            if masked:
                a = jnp.where(mask, a, 0.0)
            p = a * lax.dot_general(dos, vb, _DIMS["nt"], preferred_element_type=F32)
            both = p + (cp_ref[rs, :] + jnp.dot(p.astype(BF16), before, preferred_element_type=F32))
            sg = jnp.where(z >= 0.0, 1.0, e) * pl.reciprocal(1.0 + e, approx=True)
            dz = p - both * sg
            if masked:
                dz = jnp.where(mask, dz, 0.0)
            dz = (dz * scale).astype(BF16)
            dq_acc[rs, :] += jnp.dot(dz, kb, preferred_element_type=F32)
            dk_ref[rows, :] += lax.dot_general(dz, qs, _DIMS["tn"], preferred_element_type=F32)
            dv_ref[rows, :] += lax.dot_general(a.astype(BF16), dos, _DIMS["tn"], preferred_element_type=F32)
            pre_ref[rs, :] += jnp.sum(l1m, axis=1, keepdims=True)
            cp_ref[rs, :] += jnp.sum(p, axis=1, keepdims=True)

        def step(j, carry):
            for u in range(unroll):
                tile(unroll * j + u, None)
            return carry

        lax.fori_loop(0, (r * i) // unroll, step, 0)
        for dgl in range(r):
            tile(r * i + dgl, dgl * bk)
        dq_ref[...] = dq_acc[...].astype(BF16)

    blk = pl.BlockSpec((bq, TILE), lambda h, i: (i, h))
    head = pl.BlockSpec((s, TILE), lambda h, i: (0, h))
    return _hosted_call(
        body, [q, k, v, do, lsum], name=name, grid=(n_h, n_q), in_specs=[blk, head, head, blk, blk],
        out_specs=[blk, head, head],
        out_shape=[jax.ShapeDtypeStruct((s, hd), BF16), jax.ShapeDtypeStruct((s, hd), F32),
                   jax.ShapeDtypeStruct((s, hd), F32)],
        scratch_shapes=[pltpu.VMEM((bq, TILE), F32), pltpu.VMEM((bq, 1), F32), pltpu.VMEM((bq, 1), F32)],
        semantics=("parallel", "arbitrary"), rides=rides)


EW_BLOCK = 512 * 1024


def _ew_blocks(r, c, elems=EW_BLOCK):
    return _pick(r, max(ROWS, elems // c // ROWS * ROWS), ROWS), c


def _cast_bf16(w, layer, chip_idx, name):
    _, r, c = w.shape
    br, bc = _ew_blocks(r, c)

    def body(chip_ref, w_ref, o_ref):
        o_ref[...] = w_ref[...].astype(BF16)

    return pl.pallas_call(
        body, name=name,
        grid_spec=pltpu.PrefetchScalarGridSpec(
            num_scalar_prefetch=1, grid=(r // br, c // bc),
            in_specs=[pl.BlockSpec((None, br, bc), lambda i, j, chip_ref: (layer, i, j))],
            out_specs=pl.BlockSpec((None, br, bc), lambda i, j, chip_ref: (chip_ref[0], i, j)),
        ),
        out_shape=jax.ShapeDtypeStruct((N_CHIPS, r, c), BF16), compiler_params=_cp("parallel", "parallel"),
    )(chip_idx, w)


def _pair_add(dw, recv, c_idx, name):
    _, r, c = dw.shape
    hr = r // 2
    br, bc = _ew_blocks(hr, c)
    nb = hr // br

    def body(c_ref, a_ref, b_ref, o_ref):
        o_ref[...] = (a_ref[...].astype(F32) + b_ref[...].astype(F32)).astype(BF16)

    return pl.pallas_call(
        body, name=name,
        grid_spec=pltpu.PrefetchScalarGridSpec(
            num_scalar_prefetch=1, grid=(N_CHIPS, nb, c // bc),
            in_specs=[
                pl.BlockSpec((None, br, bc), lambda s, i, j, c_ref: (s, c_ref[0] * nb + i, j)),
                pl.BlockSpec((None, br, bc), lambda s, i, j, c_ref: (s, i, j)),
            ],
            out_specs=pl.BlockSpec((None, br, bc), lambda s, i, j, c_ref: (s, i, j)),
        ),
        out_shape=jax.ShapeDtypeStruct((N_CHIPS, hr, c), BF16),
        compiler_params=_cp("parallel", "parallel", "parallel"),
    )(c_idx, dw, recv)


def _chip_sum(parts, dest, shape, layer, c_idx, name):
    _, hr, c = parts.shape
    br, bc = _ew_blocks(hr, c, EW_BLOCK // 2)
    nb = hr // br

    def body(c_ref, p_ref, *refs):
        o_ref = refs[-1]
        acc = p_ref[0].astype(F32)
        for s in range(1, N_CHIPS):
            acc = acc + p_ref[s].astype(F32)
        o_ref[...] = acc

    in_specs = [pl.BlockSpec((N_CHIPS, br, bc), lambda i, j, c_ref: (0, i, j))]
    operands = [c_idx, parts]
    aliases = {}
    if dest is not None:
        in_specs.append(ANY)
        operands.append(dest)
        aliases = {2: 0}
    return pl.pallas_call(
        body, name=name,
        grid_spec=pltpu.PrefetchScalarGridSpec(
            num_scalar_prefetch=1, grid=(nb, c // bc), in_specs=in_specs,
            out_specs=pl.BlockSpec((None, br, bc), lambda i, j, c_ref: (layer, c_ref[0] * nb + i, j)),
        ),
        out_shape=jax.ShapeDtypeStruct(shape, F32), input_output_aliases=aliases,
        compiler_params=_cp("parallel", "parallel"),
    )(*operands)


def _adamw(w, g, m, v, name, pass_g=False):
    n_l, r, c = w.shape
    br, bc = _ew_blocks(r, c, EW_BLOCK // 2)

    def body(w_ref, g_ref, m_ref, v_ref, *out_refs):
        d_ref, mo_ref, vo_ref = out_refs[-3:]
        g = g_ref[...]
        if pass_g:
            out_refs[0][...] = g
        m = ADAM_B1 * m_ref[...] + (1.0 - ADAM_B1) * g
        v = ADAM_B2 * v_ref[...] + (1.0 - ADAM_B2) * (g * g)
        m_hat = m / (1.0 - ADAM_B1 ** ADAM_STEP)
        v_hat = v / (1.0 - ADAM_B2 ** ADAM_STEP)
        d_ref[...] = -ADAM_LR * (m_hat / (jnp.sqrt(v_hat) + ADAM_EPS) + ADAM_WD * w_ref[...])
        mo_ref[...] = m
        vo_ref[...] = v

    blk = pl.BlockSpec((None, br, bc), lambda l, i, j: (l, i, j))
    n_out = 4 if pass_g else 3
    return pl.pallas_call(
        body, name=name, grid=(n_l, r // br, c // bc), in_specs=[blk] * 4, out_specs=[blk] * n_out,
        out_shape=[jax.ShapeDtypeStruct(w.shape, F32)] * n_out,
        compiler_params=_cp("parallel", "parallel", "parallel"),
    )(w, g, m, v)


def _place():
    x, y, c = lax.axis_index("x"), lax.axis_index("y"), lax.axis_index("c")
    chips = [(1 - x, y), (x, 1 - y), (1 - x, 1 - y)]
    return x, y, c, chips


class _Ride:
    def __init__(self, reads, bufs, new, n_sems, start, finish):
        self.reads, self.bufs, self.new, self.n_sems, self.start, self.finish = reads, bufs, new, n_sems, start, finish


def _hosted_call(body, operands, *, name, grid, in_specs, out_specs, out_shape, scratch_shapes=(), semantics=(), rides=()):
    single = not isinstance(out_shape, (list, tuple))
    out_specs = [out_specs] if single else list(out_specs)
    out_shape = [out_shape] if single else list(out_shape)
    in_specs, scratch_shapes = list(in_specs), list(scratch_shapes)
    n_in, n_out, n_scr = len(in_specs), len(out_shape), len(scratch_shapes)
    extra_in, extra_out, aliases, where = [], [], {}, []
    for ride in rides:
        r0 = len(extra_in)
        extra_in += list(ride.reads)
        b0 = len(extra_in)
        extra_in += list(ride.bufs)
        ob0 = len(extra_out)
        extra_out += [jax.ShapeDtypeStruct(b.shape, b.dtype) for b in ride.bufs]
        for t in range(len(ride.bufs)):
            aliases[n_in + b0 + t] = n_out + ob0 + t
        on0 = len(extra_out)
        extra_out += list(ride.new)
        where.append((r0, len(ride.reads), ob0, len(ride.bufs), on0, len(ride.new)))
    n_ein, n_eout = len(extra_in), len(extra_out)
    sem_shapes = [pltpu.SemaphoreType.DMA((max(1, k),)) for ride in rides for k in ride.n_sems]

    def full_body(*refs):
        ins, outs, scr = refs[:n_in + n_ein], refs[n_in + n_ein:n_in + n_ein + n_out + n_eout], refs[n_in + n_ein + n_out + n_eout:]

        def run(which):
            for idx, (ride, (r0, nr, ob0, nb, on0, nn)) in enumerate(zip(rides, where)):
                fn = ride.start if which == 0 else ride.finish
                fn(ins[n_in + r0:n_in + r0 + nr], outs[n_out + ob0:n_out + ob0 + nb], outs[n_out + on0:n_out + on0 + nn],
                   *scr[n_scr + 3 * idx:n_scr + 3 * idx + 3])

        host = lambda: body(*ins[:n_in], *outs[:n_out], *scr[:n_scr])
        if not rides:
            host()
        elif not grid:
            run(0)
            host()
            run(1)
        else:
            ids = [pl.program_id(ax) for ax in range(len(grid))]
            first = functools.reduce(jnp.logical_and, [i == 0 for i in ids])
            last = functools.reduce(jnp.logical_and, [i == g - 1 for i, g in zip(ids, grid)])
            pl.when(first)(lambda: run(0))
            host()
            pl.when(last)(lambda: run(1))

    if rides:
        params = pltpu.CompilerParams(dimension_semantics=("arbitrary",) * len(grid), vmem_limit_bytes=VMEM_LIMIT)
    else:
        params = _cp(*semantics)
    outs = pl.pallas_call(
        full_body, name=name, grid=grid,
        in_specs=in_specs + [ANY] * n_ein, out_specs=out_specs + [ANY] * n_eout,
        out_shape=out_shape + extra_out, input_output_aliases=aliases,
        scratch_shapes=scratch_shapes + sem_shapes, compiler_params=params,
    )(*operands, *extra_in)
    main = outs[0] if single else list(outs[:n_out])
    rode = [(list(outs[n_out + ob0:n_out + ob0 + nb]), list(outs[n_out + on0:n_out + on0 + nn]))
            for (_, _, ob0, nb, on0, nn) in where]
    return main, rode


def _run_rides(rides, name):
    return _hosted_call(lambda: None, [], name=name, grid=(), in_specs=[], out_specs=[], out_shape=[], rides=rides)[1]


def _ride_gather(slots, part=0, n_parts=1, span=1):
    n = len(slots)
    halves = [a.shape[1] // 2 for a in slots]
    sizes = [hr // n_parts for hr in halves]
    assert part + span <= n_parts
    for a, hr, size in zip(slots, halves, sizes):
        assert a.shape[1] == 2 * hr and hr == size * n_parts and size % ROWS == 0, a.shape

    def remote(bufs, send_sems, recv_sems, i, k, slot, core, to):
        rows = bufs[i].at[slot, pl.ds(pl.multiple_of(core * halves[i] + part * sizes[i], ROWS), span * sizes[i])]
        return pltpu.make_async_remote_copy(
            src_ref=rows, dst_ref=rows, send_sem=send_sems.at[i * 6 + k], recv_sem=recv_sems.at[i * 6 + k],
            device_id=to, device_id_type=MESH)

    def start(reads, bufs, new, send_sems, recv_sems, local_sems):
        x, y, c, chips = _place()
        for i in range(n):
            for k, (px, py) in enumerate(chips):
                remote(bufs, send_sems, recv_sems, i, k, 2 * x + y, c, (px, py, c)).start()

    def finish(reads, bufs, new, send_sems, recv_sems, local_sems):
        x, y, c, chips = _place()
        cp = functools.partial(remote, bufs, send_sems, recv_sems)
        for i in range(n):
            for k, (px, py) in enumerate(chips):
                cp(i, k, 2 * px + py, c, (x, y, c)).wait_recv()
                cp(i, 3 + k, 2 * px + py, c, (x, y, 1 - c)).start()
        for i in range(n):
            for k, (px, py) in enumerate(chips):
                cp(i, 3 + k, 2 * px + py, 1 - c, (x, y, c)).wait_recv()
        for i in range(n):
            for k, (px, py) in enumerate(chips):
                cp(i, k, 2 * x + y, c, (px, py, c)).wait_send()
                cp(i, 3 + k, 2 * px + py, c, (x, y, 1 - c)).wait_send()

    return _Ride([], slots, [], (6 * n, 6 * n, 0), start, finish)


def _ride_swap(grads):
    n = len(grads)
    halves = [a.shape[1] // 2 for a in grads]

    def copies(reads, new, send_sems, recv_sems):
        x, y, c, _ = _place()
        out = []
        for i in range(n):
            rows = pl.ds(pl.multiple_of((1 - c) * halves[i], 2 * SUBLANE), halves[i])
            out.append(pltpu.make_async_remote_copy(
                src_ref=reads[i].at[:, rows, :], dst_ref=new[i], send_sem=send_sems.at[i], recv_sem=recv_sems.at[i],
                device_id=(x, y, 1 - c), device_id_type=MESH))
        return out

    def start(reads, bufs, new, send_sems, recv_sems, local_sems):
        for cp in copies(reads, new, send_sems, recv_sems):
            cp.start()

    def finish(reads, bufs, new, send_sems, recv_sems, local_sems):
        for cp in copies(reads, new, send_sems, recv_sems):
            cp.wait()

    shapes = [jax.ShapeDtypeStruct((N_CHIPS, hr, a.shape[2]), a.dtype) for a, hr in zip(grads, halves)]
    return _Ride(grads, [], shapes, (n, n, 0), start, finish)


def _ride_scatter(parts, part=0, n_parts=1, into=None, span=1):
    n = len(parts)
    sizes = [a.shape[1] // n_parts for a in parts]
    assert part + span <= n_parts
    for a, size in zip(parts, sizes):
        assert a.shape[1] == size * n_parts and size % ROWS == 0, a.shape

    def piece(ref, i, slot):
        return ref.at[slot, pl.ds(part * sizes[i], span * sizes[i])]

    def own(reads, land, local_sems, i):
        me = 2 * lax.axis_index("x") + lax.axis_index("y")
        return pltpu.make_async_copy(piece(reads[i], i, me), piece(land[i], i, me), local_sems.at[i])

    def send(reads, land, send_sems, recv_sems, i, k):
        x, y, c, chips = _place()
        px, py = chips[k]
        return pltpu.make_async_remote_copy(
            src_ref=piece(reads[i], i, 2 * px + py), dst_ref=piece(land[i], i, 2 * x + y),
            send_sem=send_sems.at[3 * i + k], recv_sem=recv_sems.at[3 * i + k],
            device_id=(px, py, c), device_id_type=MESH)

    def start(reads, bufs, new, send_sems, recv_sems, local_sems):
        land = new if into is None else bufs
        for i in range(n):
            own(reads, land, local_sems, i).start()
            for k in range(3):
                send(reads, land, send_sems, recv_sems, i, k).start()

    def finish(reads, bufs, new, send_sems, recv_sems, local_sems):
        land = new if into is None else bufs
        x, y, c, chips = _place()
        for i in range(n):
            for k, (px, py) in enumerate(chips):
                slot = piece(land[i], i, 2 * px + py)
                pltpu.make_async_remote_copy(
                    src_ref=slot, dst_ref=slot, send_sem=send_sems.at[3 * i + k], recv_sem=recv_sems.at[3 * i + k],
                    device_id=(x, y, c), device_id_type=MESH).wait_recv()
        for i in range(n):
            for k in range(3):
                send(reads, land, send_sems, recv_sems, i, k).wait_send()
            own(reads, land, local_sems, i).wait()

    shapes = [jax.ShapeDtypeStruct(a.shape, a.dtype) for a in parts]
    if into is None:
        return _Ride(parts, [], shapes, (3 * n, 3 * n, n), start, finish)
    return _Ride(parts, list(into), [], (3 * n, 3 * n, n), start, finish)


def _ride_join(grads):
    n = len(grads)

    def copy(bufs, send_sems, recv_sems, i, core, to):
        hr = grads[i].shape[1] // 2
        rows = bufs[i].at[:, pl.ds(pl.multiple_of(core * hr, SUBLANE), hr), :]
        return pltpu.make_async_remote_copy(
            src_ref=rows, dst_ref=rows, send_sem=send_sems.at[i], recv_sem=recv_sems.at[i],
            device_id=to, device_id_type=MESH)

    def start(reads, bufs, new, send_sems, recv_sems, local_sems):
        x, y, c, _ = _place()
        for i in range(n):
            copy(bufs, send_sems, recv_sems, i, c, (x, y, 1 - c)).start()

    def finish(reads, bufs, new, send_sems, recv_sems, local_sems):
        x, y, c, _ = _place()
        for i in range(n):
            copy(bufs, send_sems, recv_sems, i, 1 - c, (x, y, c)).wait_recv()
        for i in range(n):
            copy(bufs, send_sems, recv_sems, i, c, (x, y, 1 - c)).wait_send()

    return _Ride([], grads, [], (n, n, 0), start, finish)


def _all_reduce_small(packed, name, rides=()):
    r, c = packed.shape
    chunk = _pick(r, 256, ROWS)

    def body(x_ref, out_ref, gath, send_sems, recv_sems, local_sem):
        x, y, cc, chips = _place()
        me, sibling = (x, y, cc), (x, y, 1 - cc)

        def slot(px, py, pc):
            return gath.at[4 * px + 2 * py + pc]

        def copy(k, block, to, src=None):
            return pltpu.make_async_remote_copy(
                src_ref=slot(*block) if src is None else src, dst_ref=slot(*block),
                send_sem=send_sems.at[k], recv_sem=recv_sems.at[k], device_id=to, device_id_type=MESH)

        mine = pltpu.make_async_copy(x_ref, slot(*me), local_sem)
        mine.start()
        first = [copy(0, me, sibling, src=x_ref)]
        first += [copy(1 + j, me, (*chip, cc), src=x_ref) for j, chip in enumerate(chips)]
        for cp in first:
            cp.start()
        passed = [copy(4 + j, (*chip, cc), sibling) for j, chip in enumerate(chips)]
        for j, chip in enumerate(chips):
            copy(1 + j, (*chip, cc), me).wait_recv()
            passed[j].start()
        copy(0, sibling, me).wait_recv()
        for j, chip in enumerate(chips):
            copy(4 + j, (*chip, 1 - cc), me).wait_recv()
        for cp in first + passed:
            cp.wait_send()
        mine.wait()

        def add(i, carry):
            rows = pl.ds(pl.multiple_of(i * chunk, SUBLANE), chunk)
            acc = gath[0, rows, :]
            for dev in range(1, N_DEV):
                acc = acc + gath[dev, rows, :]
            out_ref[rows, :] = acc
            return carry

        lax.fori_loop(0, r // chunk, add, 0)

    return _hosted_call(
        body, [packed], name=name, grid=(), in_specs=[VMEM_SPEC], out_specs=VMEM_SPEC,
        out_shape=jax.ShapeDtypeStruct((r, c), F32),
        scratch_shapes=[pltpu.VMEM((N_DEV, r, c), F32), pltpu.SemaphoreType.DMA((7,)),
                        pltpu.SemaphoreType.DMA((7,)), pltpu.SemaphoreType.DMA],
        rides=rides)


_PACK_ROWS = 256


def _pack(arrays):
    flat = jnp.concatenate([a.reshape(-1).astype(F32) for a in arrays])
    unit = _PACK_ROWS * LANE
    total = -(-flat.shape[0] // unit) * unit
    return jnp.pad(flat, (0, total - flat.shape[0])).reshape(-1, LANE)


def _unpack(packed, shapes, lead=()):
    flat = packed.reshape(lead + (-1,))
    out, at = [], 0
    for s in shapes:
        size = math.prod(s)
        out.append(flat[..., at:at + size].reshape(lead + tuple(s)))
        at += size
    return out


def kernel(x, pre_mix_g, post_mix_g, pre_ffn_g, post_ffn_g, a_w_in, a_v_norm_g, a_w_spatial, a_b_spatial, a_w_out, kv_norm_g, w_k, w_v, b_w_q, b_w_o, ffn_w_up, ffn_conv_w, ffn_conv_b, ffn_w_down, loss_target, m_pre_mix_g, m_post_mix_g, m_pre_ffn_g, m_post_ffn_g, m_a_w_in, m_a_v_norm_g, m_a_w_spatial, m_a_b_spatial, m_a_w_out, m_kv_norm_g, m_w_k, m_w_v, m_b_w_q, m_b_w_o, m_ffn_w_up, m_ffn_conv_w, m_ffn_conv_b, m_ffn_w_down, v_pre_mix_g, v_post_mix_g, v_pre_ffn_g, v_post_ffn_g, v_a_w_in, v_a_v_norm_g, v_a_w_spatial, v_a_b_spatial, v_a_w_out, v_kv_norm_g, v_w_k, v_w_v, v_b_w_q, v_b_w_o, v_ffn_w_up, v_ffn_conv_w, v_ffn_conv_b, v_ffn_w_down):
    xi, yi, ci = lax.axis_index("x"), lax.axis_index("y"), lax.axis_index("c")
    chip = 2 * xi + yi
    c_idx = jnp.reshape(ci, (1,)).astype(jnp.int32)
    _, s, d = x.shape
    n_layers = pre_mix_g.shape[0]
    assert n_layers == 2 and a_w_in.shape[0] == 1 and b_w_q.shape[0] == 1
    d_a = a_w_out.shape[1] * N_CHIPS
    n_g = a_w_spatial.shape[1]
    ns = ffn_w_up.shape[2]
    assert a_w_spatial.shape[2] == TILE and d_a == n_g * TILE and s % TILE == 0
    h0 = x[0]
    target = loss_target[0]

    big = {
        "win": (a_w_in, m_a_w_in, v_a_w_in),
        "wout": (a_w_out, m_a_w_out, v_a_w_out),
        "wk": (w_k[None], m_w_k[None], v_w_k[None]),
        "wv": (w_v[None], m_w_v[None], v_w_v[None]),
        "wq": (b_w_q, m_b_w_q, v_b_w_q),
        "wo": (b_w_o, m_b_w_o, v_b_w_o),
        "wup": (ffn_w_up, m_ffn_w_up, v_ffn_w_up),
        "wdn": (ffn_w_down, m_ffn_w_down, v_ffn_w_down),
    }
    units = [(nm, layer) for nm in big for layer in range(big[nm][0].shape[0])]
    chip_idx = jnp.reshape(chip, (1,)).astype(jnp.int32)
    shards = [_cast_bf16(big[nm][0], layer, chip_idx, f"cast_{nm}{layer}") for nm, layer in units]
    small_sharded = _pack([a_v_norm_g, ffn_conv_w])
    small_sharded = lax.dynamic_update_index_in_dim(
        jnp.zeros((N_CHIPS,) + small_sharded.shape, F32), small_sharded, chip, 0)
    own = dict(zip(units, shards))
    full = {}

    def gather_ride(keys):
        return _ride_gather([own[key] for key in keys])

    def gathered(keys, rode):
        full.update(zip(keys, rode[0]))

    first_keys = [("win", 0)]
    (first_bufs, _), = _run_rides([_ride_gather([own[key] for key in first_keys] + [small_sharded])], "gather_first")
    full.update(zip(first_keys, first_bufs[:-1]))
    vg_parts, cw_parts = _unpack(first_bufs[-1], [a_v_norm_g.shape, ffn_conv_w.shape], lead=(N_CHIPS,))
    v_g = jnp.transpose(vg_parts, (1, 0, 2)).reshape(1, d_a)

    def rows(nm, layer=0):
        w = full[(nm, layer)]
        return w.reshape(w.shape[0] * w.shape[1], w.shape[2])

    gains = lambda g, layer: g[layer:layer + 1]
    bias = jnp.repeat(a_b_spatial[0].T, TILE, axis=1)
    w_s = a_w_spatial[0]
    kv_g = kv_norm_g[None]
    conv_w = [cw_parts[:, layer] for layer in range(n_layers)]
    conv_b = [ffn_conv_b[layer].reshape(N_CHIPS, 1, ns) for layer in range(n_layers)]

    def ffn_fwd(hn, layer, up_keys=(), act_keys=(), down_keys=()):
        a = _mm(hn, full[("wup", layer)], "nn", f"ffn_up{layer}", out_split=N_CHIPS,
                rides=[gather_ride(up_keys)] if up_keys else ())
        if up_keys:
            a, (rode,) = a
            gathered(up_keys, rode)
        hm = _ffn_act_fwd(a, conv_w[layer], conv_b[layer], f"ffn_act{layer}",
                          rides=[gather_ride(act_keys)] if act_keys else ())
        if act_keys:
            hm, (rode,) = hm
            gathered(act_keys, rode)
        f = _mm(hm, rows("wdn", layer), "nn", f"ffn_down{layer}", rides=[gather_ride(down_keys)] if down_keys else ())
        if down_keys:
            f, (rode,) = f
            gathered(down_keys, rode)
        return a, hm, f[0]

    up0 = own[("wup", 0)]
    pieces = lambda p, span: _ride_gather([up0], part=p, n_parts=8, span=span)
    hn0 = _rms_fwd(h0, gains(pre_mix_g, 0), "norm_in")
    uv, ((out_bufs, _), ((up0,), _)) = _mm(
        hn0, full[("win", 0)], "nn", "gmlp_in", out_split=N_CHIPS, rides=[gather_ride([("wout", 0)]), pieces(0, 1)])
    full[("wout", 0)] = out_bufs[0]
    gm, (((up0,), _),) = _gmlp_fwd(uv, v_g, w_s, bias, "gmlp_gate", rides=[pieces(1, 2)])
    mix0, (((up0,), _),) = _mm(gm, rows("wout"), "nn", "gmlp_out", rides=[pieces(3, 2)])
    mix0 = mix0[0]
    (h1, hn1), (((up0,), _),) = _resid_rms(
        h0, mix0, gains(post_mix_g, 0), [gains(pre_ffn_g, 0)], "resid_mix0", rides=[pieces(5, 3)])
    full[("wup", 0)] = up0
    a0, hm0, f0 = ffn_fwd(hn1, 0, up_keys=[("wdn", 0)], act_keys=[("wq", 0), ("wk", 0)],
                          down_keys=[("wv", 0), ("wo", 0)])
    h2, hn2, kvn = _resid_rms(h1, f0, gains(post_ffn_g, 0), [gains(pre_mix_g, 1), kv_g], "resid_ffn0")
    q = _mm(hn2, rows("wq"), "nn", "proj_q", out_dtype=BF16)[0]
    k = _mm(kvn, rows("wk"), "nn", "proj_k", out_dtype=BF16)[0]
    v = _mm(kvn, rows("wv"), "nn", "proj_v", out_dtype=BF16)[0]
    last_keys = [("wup", 1), ("wdn", 1)]
    (att, lsum), (rode,) = _attn_fwd(q, k, v, "attn_fwd", rides=[gather_ride(last_keys)])
    gathered(last_keys, rode)
    mix1 = _mm(att, rows("wo"), "nn", "proj_o")[0]
    h3, hn3 = _resid_rms(h2, mix1, gains(post_mix_g, 1), [gains(pre_ffn_g, 1)], "resid_mix1")
    a1, hm1, f1 = ffn_fwd(hn3, 1)
    dh4, loss_tile = _loss_head(h3, f1, gains(post_ffn_g, 1), target, "loss_head")
    loss = lax.psum(loss_tile[0, 0], ("x", "y", "c"))

    dw = {}
    dg = {}

    pair = {}
    half_done = {nm: None for nm in big}

    def swap_ride(keys):
        return _ride_swap([dw[key] for key in keys])

    def swapped(keys, rode):
        for (nm, layer), got in zip(keys, rode[1]):
            pair[(nm, layer)] = _pair_add(dw[(nm, layer)], got, c_idx, f"pair_add_{nm}{layer}")

    def scatter_ride(keys):
        return _ride_scatter([pair[key] for key in keys])

    def scattered(keys, rode):
        for (nm, layer), got in zip(keys, rode[1]):
            half_done[nm] = _chip_sum(got, half_done[nm], big[nm][0].shape, layer, c_idx, f"chip_sum_{nm}{layer}")

    def ffn_bwd(dh_out, h_in, hn, a, hm, f, layer, act_rides=()):
        df, dg[("post_ffn", layer)] = _rms_bwd_out(dh_out, f, gains(post_ffn_g, layer), f"d_norm_ffn_out{layer}")
        dwd = _mm(hm, df, "tn", f"d_w_down{layer}", out_dtype=BF16)[0]
        down, up = [("wdn", layer)], [("wup", layer)]
        dw[down[0]] = dwd.reshape(N_CHIPS, dwd.shape[0] // N_CHIPS, d)
        dhm, (rode,) = _mm(df, rows("wdn", layer), "nt", f"d_ffn_mid{layer}", out_split=2, rides=[swap_ride(down)])
        swapped(down, rode)
        (da, dg[("conv_w", layer)], dg[("conv_b", layer)]), act_rode = _ffn_act_bwd(
            a, dhm, conv_w[layer], conv_b[layer], f"d_ffn_act{layer}", rides=act_rides)
        dw[up[0]], (rode,) = _mm(hn, da, "tn", f"d_w_up{layer}", out_dtype=BF16, out_split=N_CHIPS,
                                 rides=[scatter_ride(down)])
        scattered(down, rode)
        dhn, (rode,) = _mm(da, full[("wup", layer)], "nt", f"d_ffn_in{layer}", rides=[swap_ride(up)])
        swapped(up, rode)
        return dhn[0], act_rode

    dhn3, _ = ffn_bwd(dh4, h3, hn3, a1, hm1, f1, 1)
    dh3, (dg[("pre_ffn", 1)],) = _rms_bwd_in(dh4, h3, [([dhn3], gains(pre_ffn_g, 1))], "d_norm_ffn_in1")
    dmix1, dg[("post_mix", 1)] = _rms_bwd_out(dh3, mix1, gains(post_mix_g, 1), "d_norm_mix_out1")
    dwo = _mm(att, dmix1, "tn", "d_w_o", out_dtype=BF16)[0]
    dw[("wo", 0)] = dwo.reshape(N_CHIPS, dwo.shape[0] // N_CHIPS, d)
    datt = _mm(dmix1, rows("wo"), "nt", "d_attn_out", out_dtype=BF16)[0]
    ffn1_keys = [("wup", 1)]
    (dq, dk, dv), (rode,) = _attn_bwd(q, k, v, datt, lsum, "attn_bwd", rides=[scatter_ride(ffn1_keys)])
    scattered(ffn1_keys, rode)
    for nm, act, dact in (("wq", hn2, dq), ("wk", kvn, dk), ("wv", kvn, dv)):
        g = _mm(act, dact, "tn", f"d_{nm}", out_dtype=BF16)[0]
        dw[(nm, 0)] = g.reshape(N_CHIPS, g.shape[0] // N_CHIPS, g.shape[1])
    dhn2 = _mm(dq, rows("wq"), "nt", "d_q_in")[0]
    dkvn_k = _mm(dk, rows("wk"), "nt", "d_k_in")[0]
    attn_keys = [("wo", 0), ("wq", 0), ("wk", 0), ("wv", 0)]
    dkvn_v, (rode,) = _mm(dv, rows("wv"), "nt", "d_v_in", rides=[swap_ride(attn_keys)])
    swapped(attn_keys, rode)
    dh2, (dg[("pre_mix", 1)], dg["kv"]) = _rms_bwd_in(
        dh3, h2, [([dhn2], gains(pre_mix_g, 1)), ([dkvn_k, dkvn_v[0]], kv_g)], "d_norm_mix_in1")
    dhn1, (rode,) = ffn_bwd(dh2, h1, hn1, a0, hm0, f0, 0, act_rides=[scatter_ride(attn_keys)])
    scattered(attn_keys, rode)
    up0_pair = [pair[("wup", 0)]]
    up0_landed = [None]

    def up0_piece(part, span):
        return _ride_scatter(up0_pair, part, 8, into=up0_landed[0], span=span)

    def up0_rode(rode):
        up0_landed[0] = rode[1] if up0_landed[0] is None else rode[0]

    dh1, (dg[("pre_ffn", 0)],), (rode,) = _rms_bwd_in(
        dh2, h1, [([dhn1], gains(pre_ffn_g, 0))], "d_norm_ffn_in0", rides=[up0_piece(0, 1)])
    up0_rode(rode)
    dmix0, dg[("post_mix", 0)], (rode,) = _rms_bwd_out(
        dh1, mix0, gains(post_mix_g, 0), "d_norm_mix_out0", rides=[up0_piece(1, 1)])
    up0_rode(rode)
    early = ["wq", "wk", "wv", "wo", "wdn"]
    dwout, (((joined_early, _)),) = _mm(
        gm, dmix0, "tn", "d_w_out", out_dtype=BF16, rides=[_ride_join([half_done[nm] for nm in early])])
    grads_big = dict(zip(early, joined_early))
    w_out_key, w_in_key = [("wout", 0)], [("win", 0)]
    dw[w_out_key[0]] = dwout[0].reshape(N_CHIPS, dwout.shape[1] // N_CHIPS, d)
    dgm, (rode, up0) = _mm(dmix0, rows("wout"), "nt", "d_gmlp_gate", rides=[swap_ride(w_out_key), up0_piece(2, 1)])
    swapped(w_out_key, rode)
    up0_rode(up0)
    (duv, d_ws, d_bs, d_vg), (rode,) = _gmlp_bwd(uv, dgm[0], v_g, w_s, bias, "d_gmlp", rides=[up0_piece(3, 2)])
    up0_rode(rode)
    dw[w_in_key[0]], (rode, up0) = _mm(
        hn0, duv, "tn", "d_w_in", out_dtype=BF16, out_split=N_CHIPS, rides=[scatter_ride(w_out_key), up0_piece(5, 1)])
    scattered(w_out_key, rode)
    up0_rode(up0)
    dhn0, (rode, up0) = _mm(
        duv, full[("win", 0)], "nt", "d_gmlp_in", rides=[swap_ride(w_in_key), up0_piece(6, 2)])
    swapped(w_in_key, rode)
    up0_rode(up0)
    scattered([("wup", 0)], (None, up0_landed[0]))
    dx, (dg[("pre_mix", 0)],), (rode,) = _rms_bwd_in(
        dh1, h0, [([dhn0[0]], gains(pre_mix_g, 0))], "d_norm_in", rides=[scatter_ride(w_in_key)])
    scattered(w_in_key, rode)

    stack = lambda key: jnp.concatenate([dg[(key, layer)] for layer in range(n_layers)], axis=0)
    small_parts = [
        stack("pre_mix"), stack("post_mix"), stack("pre_ffn"), stack("post_ffn"),
        d_vg, d_ws, d_bs[::SUBLANE], dg["kv"],
        jnp.stack([dg[("conv_w", layer)] for layer in range(n_layers)]),
        jnp.stack([dg[("conv_b", layer)] for layer in range(n_layers)]),
    ]
    late = [nm for nm in big if nm not in early]
    summed, ((joined_late, _),) = _all_reduce_small(
        _pack(small_parts), "small_grads_sum", rides=[_ride_join([half_done[nm] for nm in late])])
    grads_big.update(zip(late, joined_late))
    (g_pre_mix, g_post_mix, g_pre_ffn, g_post_ffn, g_vg, g_ws, g_bs, g_kv, g_cw, g_cb) = _unpack(
        summed, [p.shape for p in small_parts])
    g_vg = lax.dynamic_index_in_dim(g_vg.reshape(N_CHIPS, 1, d_a // N_CHIPS), chip, 0, keepdims=False)
    g_cw = lax.dynamic_index_in_dim(g_cw, chip, 1, keepdims=False)
    g_cb = g_cb.reshape(n_layers, N_CHIPS * ns)
    small = [
        (pre_mix_g, g_pre_mix, m_pre_mix_g, v_pre_mix_g),
        (post_mix_g, g_post_mix, m_post_mix_g, v_post_mix_g),
        (pre_ffn_g, g_pre_ffn, m_pre_ffn_g, v_pre_ffn_g),
        (post_ffn_g, g_post_ffn, m_post_ffn_g, v_post_ffn_g),
        (a_v_norm_g, g_vg, m_a_v_norm_g, v_a_v_norm_g),
        (a_w_spatial, g_ws[None], m_a_w_spatial, v_a_w_spatial),
        (a_b_spatial, g_bs[None], m_a_b_spatial, v_a_b_spatial),
        (kv_norm_g, g_kv.reshape(d), m_kv_norm_g, v_kv_norm_g),
        (ffn_conv_w, g_cw, m_ffn_conv_w, v_ffn_conv_w),
        (ffn_conv_b, g_cb, m_ffn_conv_b, v_ffn_conv_b),
    ]
    small = [(w, g.reshape(w.shape), m, v) for w, g, m, v in small]
    packed = [_pack([t[i] for t in small])[None] for i in range(4)]
    small_new = [_unpack(p[0], [t[0].shape for t in small]) for p in _adamw(*packed, "adamw_small")]

    new_big = {nm: _adamw(big[nm][0], grads_big[nm], big[nm][1], big[nm][2], f"adamw_{nm}", pass_g=True)
               for nm in big}

    def big_out(nm, which):
        ref_shape = {"wk": w_k.shape, "wv": w_v.shape}.get(nm, big[nm][0].shape)
        return new_big[nm][which].reshape(ref_shape)

    order = ["pre_mix", "post_mix", "pre_ffn", "post_ffn", "win", "vg", "ws", "bs", "wout", "kv", "wk", "wv", "wq",
             "wo", "wup", "cw", "cb", "wdn"]
    small_at = {"pre_mix": 0, "post_mix": 1, "pre_ffn": 2, "post_ffn": 3, "vg": 4, "ws": 5, "bs": 6, "kv": 7,
                "cw": 8, "cb": 9}
    outs = [loss, dx[None]]
    for which in range(4):
        for nm in order:
            if nm in small_at:
                outs.append(small[small_at[nm]][1] if which == 0 else small_new[which - 1][small_at[nm]])
            else:
                outs.append(big_out(nm, which))
    return tuple(outs)
```

```python
import functools
import math

import jax
import jax.numpy as jnp
from jax import lax
from jax.experimental import pallas as pl
from jax.experimental.pallas import tpu as pltpu

F32 = jnp.float32
BF16 = jnp.bfloat16
EPS = 1e-6
ADAM_LR = 0.001
ADAM_B1 = 0.9
ADAM_B2 = 0.999
ADAM_EPS = 1e-08
ADAM_WD = 0.01
ADAM_STEP = 10

LANE = 128
SUBLANE = 8
ROWS = 16
TILE = 128
N_CHIPS = 4
N_DEV = 8
VMEM_LIMIT = 56 * 1024 * 1024
MM_VMEM = 46 * 1024 * 1024
MXU_WIDTH = 256
MESH = pl.DeviceIdType.MESH
ANY = pl.BlockSpec(memory_space=pl.ANY)
VMEM_SPEC = pl.BlockSpec(memory_space=pltpu.VMEM)


def _cp(*sem):
    return pltpu.CompilerParams(dimension_semantics=sem, vmem_limit_bytes=VMEM_LIMIT)


def _pick(dim, pref, align=LANE):
    if dim <= pref:
        return dim
    best = None
    for d in range(align, pref + 1, align):
        if dim % d == 0:
            best = d
    assert best is not None, (dim, pref)
    return best


_DIMS = {
    "nn": (((1,), (0,)), ((), ())),
    "nt": (((1,), (1,)), ((), ())),
    "tn": (((0,), (0,)), ((), ())),
}


def _as3(a):
    return a if a.ndim == 3 else a[None]


def _spec3(br, bc, cols_j, rc):
    per = cols_j // bc

    def imap(m, n, k):
        r, c = rc(m, n, k)
        return (c // per, r, c % per)

    return pl.BlockSpec((None, br, bc), imap)


def _mm(a, b, mode, name, out_dtype=F32, out_split=1, rides=()):
    a, b = _as3(a), _as3(b)
    ja, ra, caj = a.shape
    jb, rb, cbj = b.shape
    if mode == "nn":
        m, k, n = ra, ja * caj, jb * cbj
        assert rb == k
        m_ext, k_ext, n_ext = [ra], [caj, rb], [cbj]
    elif mode == "nt":
        m, k, n = ra, ja * caj, rb
        assert jb * cbj == k
        m_ext, k_ext, n_ext = [ra], [caj, cbj], [rb]
    else:
        m, k, n = ja * caj, ra, jb * cbj
        assert rb == k
        m_ext, k_ext, n_ext = [caj], [ra], [cbj]
    assert n % out_split == 0
    n_ext.append(n // out_split)
    bm = _pick(math.gcd(*m_ext), 1536)
    n_unit = math.gcd(*n_ext)
    bn = _pick(n_unit, 1536)
    k_unit = math.gcd(*k_ext)
    o_bytes = jnp.dtype(out_dtype).itemsize

    def vmem_need(bm, bn, bk):
        tiles = bm * bk * a.dtype.itemsize + bk * bn * b.dtype.itemsize + bm * bn * o_bytes
        return 2 * tiles + bm * bn * 4 * (2 if bk < k else 1)

    def deepest(bm, bn):
        return max(d for d in range(LANE, k_unit + 1, LANE)
                   if k_unit % d == 0 and (d == LANE or vmem_need(bm, bn, d) <= MM_VMEM))

    bk = deepest(bm, bn)
    if bk < k_unit and k_unit == k:
        if bm % (2 * LANE) == 0 and deepest(bm // 2, bn) == k:
            bm, bk = bm // 2, k
        elif bn % (2 * LANE) == 0 and deepest(bm, bn // 2) == k:
            bn, bk = bn // 2, k
    n_outer = False
    if bn % MXU_WIDTH and n_unit % MXU_WIDTH == 0 and k_unit == k:
        for rows in (bm, bm // 2, bm // 4):
            if rows % LANE == 0 and vmem_need(rows, n_unit, k) <= MM_VMEM:
                bm, bn, bk, n_outer = rows, n_unit, k, True
                break
    nk = k // bk
    order = (lambda f: lambda ni, mi, ki: f(mi, ni, ki)) if n_outer else (lambda f: f)
    if mode == "nn":
        a_spec = _spec3(bm, bk, caj, order(lambda mi, ni, ki: (mi, ki)))
        b_spec = _spec3(bk, bn, cbj, order(lambda mi, ni, ki: (ki, ni)))
    elif mode == "nt":
        a_spec = _spec3(bm, bk, caj, order(lambda mi, ni, ki: (mi, ki)))
        b_spec = _spec3(bn, bk, cbj, order(lambda mi, ni, ki: (ni, ki)))
    else:
        a_spec = _spec3(bk, bm, caj, order(lambda mi, ni, ki: (ki, mi)))
        b_spec = _spec3(bk, bn, cbj, order(lambda mi, ni, ki: (ki, ni)))
    o_spec = _spec3(bm, bn, n // out_split, order(lambda mi, ni, ki: (mi, ni)))
    dims = _DIMS[mode]

    def body(a_ref, b_ref, o_ref, *acc):
        def part():
            return lax.dot_general(a_ref[...].astype(BF16), b_ref[...].astype(BF16), dims, preferred_element_type=F32)

        if nk == 1:
            o_ref[...] = part().astype(o_ref.dtype)
            return
        acc_ref, = acc
        ki = pl.program_id(2)

        @pl.when(ki == 0)
        def _():
            acc_ref[...] = part()

        @pl.when(jnp.logical_and(ki > 0, ki < nk - 1))
        def _():
            acc_ref[...] += part()

        @pl.when(ki == nk - 1)
        def _():
            o_ref[...] = (acc_ref[...] + part()).astype(o_ref.dtype)

    grid = (n // bn, m // bm, nk) if n_outer else (m // bm, n // bn, nk)
    out, rode = _hosted_call(
        body, [a, b], name=name, grid=grid, in_specs=[a_spec, b_spec], out_specs=o_spec,
        out_shape=jax.ShapeDtypeStruct((out_split, m, n // out_split), out_dtype),
        scratch_shapes=[pltpu.VMEM((bm, bn), F32)] if nk > 1 else [],
        semantics=("parallel", "parallel", "arbitrary"), rides=rides)
    return (out, rode) if rides else out


def _rms(x, g):
    r = lax.rsqrt(jnp.mean(x * x, axis=-1, keepdims=True) + EPS)
    return x * r * g


def _rms_bwd(x, g, dy):
    r = lax.rsqrt(jnp.mean(x * x, axis=-1, keepdims=True) + EPS)
    xh = x * r
    gy = dy * g
    dx = r * (gy - xh * jnp.mean(gy * xh, axis=-1, keepdims=True))
    return dx, jnp.sum(dy * xh, axis=0, keepdims=True)


def _row_block(s):
    return _pick(s, 256, ROWS)


def _rms_fwd(h, g, name):
    s, d = h.shape
    br = _row_block(s)

    def body(h_ref, g_ref, o_ref):
        o_ref[...] = _rms(h_ref[...], g_ref[...]).astype(BF16)

    row = pl.BlockSpec((br, d), lambda i: (i, 0))
    vec = pl.BlockSpec((1, d), lambda i: (0, 0))
    return pl.pallas_call(
        body, name=name, grid=(s // br,), in_specs=[row, vec], out_specs=row,
        out_shape=jax.ShapeDtypeStruct((s, d), BF16), compiler_params=_cp("parallel"),
    )(h, g)


def _resid_rms(h_in, f, g_post, g_next, name, rides=()):
    s, d = h_in.shape
    br = _row_block(s)
    n_next = len(g_next)

    def body(h_ref, f_ref, gp_ref, *refs):
        gn_refs, ho_ref, hn_refs = refs[:n_next], refs[n_next], refs[n_next + 1:]
        h = h_ref[...] + _rms(f_ref[...], gp_ref[...])
        ho_ref[...] = h
        for gn_ref, hn_ref in zip(gn_refs, hn_refs):
            hn_ref[...] = _rms(h, gn_ref[...]).astype(BF16)

    row = pl.BlockSpec((br, d), lambda i: (i, 0))
    vec = pl.BlockSpec((1, d), lambda i: (0, 0))
    outs, rode = _hosted_call(
        body, [h_in, f, g_post, *g_next], name=name, grid=(s // br,),
        in_specs=[row, row, vec] + [vec] * n_next,
        out_specs=[row] * (1 + n_next),
        out_shape=[jax.ShapeDtypeStruct((s, d), F32)] + [jax.ShapeDtypeStruct((s, d), BF16)] * n_next,
        semantics=("parallel",), rides=rides)
    return (outs, rode) if rides else outs


def _loss_head(h_in, f, g_post, target, name):
    s, d = h_in.shape
    br = _row_block(s)

    def body(h_ref, f_ref, gp_ref, t_ref, dh_ref, loss_ref):
        @pl.when(pl.program_id(0) == 0)
        def _():
            loss_ref[...] = jnp.zeros_like(loss_ref)

        diff = h_ref[...] + _rms(f_ref[...], gp_ref[...]) - t_ref[...]
        dh_ref[...] = diff * (1.0 / d)
        loss_ref[...] += 0.5 * jnp.sum(jnp.mean(diff * diff, axis=-1, keepdims=True))

    row = pl.BlockSpec((br, d), lambda i: (i, 0))
    vec = pl.BlockSpec((1, d), lambda i: (0, 0))
    return pl.pallas_call(
        body, name=name, grid=(s // br,),
        in_specs=[row, row, vec, row],
        out_specs=[row, pl.BlockSpec((SUBLANE, LANE), lambda i: (0, 0))],
        out_shape=[jax.ShapeDtypeStruct((s, d), F32), jax.ShapeDtypeStruct((SUBLANE, LANE), F32)],
        compiler_params=_cp("arbitrary"),
    )(h_in, f, g_post, target)


def _rms_bwd_out(dy, f, g, name, rides=()):
    s, d = f.shape
    br = _row_block(s)

    def body(dy_ref, f_ref, g_ref, df_ref, dg_ref):
        @pl.when(pl.program_id(0) == 0)
        def _():
            dg_ref[...] = jnp.zeros_like(dg_ref)

        dx, dg = _rms_bwd(f_ref[...], g_ref[...], dy_ref[...])
        df_ref[...] = dx.astype(BF16)
        dg_ref[...] += dg

    row = pl.BlockSpec((br, d), lambda i: (i, 0))
    vec = pl.BlockSpec((1, d), lambda i: (0, 0))
    (df, dg), rode = _hosted_call(
        body, [dy, f, g], name=name, grid=(s // br,), in_specs=[row, row, vec], out_specs=[row, vec],
        out_shape=[jax.ShapeDtypeStruct((s, d), BF16), jax.ShapeDtypeStruct((1, d), F32)],
        semantics=("arbitrary",), rides=rides)
    return (df, dg, rode) if rides else (df, dg)


def _rms_bwd_in(dh_out, h_in, branches, name, rides=()):
    s, d = h_in.shape
    br = _row_block(s)
    counts = [len(ds) for ds, _ in branches]
    n_d = sum(counts)
    n_b = len(branches)

    def body(dho_ref, h_ref, *refs):
        d_refs, g_refs = refs[:n_d], refs[n_d:n_d + n_b]
        dh_ref, dg_refs = refs[n_d + n_b], refs[n_d + n_b + 1:]

        @pl.when(pl.program_id(0) == 0)
        def _():
            for r in dg_refs:
                r[...] = jnp.zeros_like(r)

        h = h_ref[...]
        acc = dho_ref[...]
        at = 0
        for bi, cnt in enumerate(counts):
            dn = d_refs[at][...]
            for r in d_refs[at + 1:at + cnt]:
                dn = dn + r[...]
            at += cnt
            dx, dg = _rms_bwd(h, g_refs[bi][...], dn)
            acc = acc + dx
            dg_refs[bi][...] += dg
        dh_ref[...] = acc

    row = pl.BlockSpec((br, d), lambda i: (i, 0))
    vec = pl.BlockSpec((1, d), lambda i: (0, 0))
    flat_d = [x for ds, _ in branches for x in ds]
    outs, rode = _hosted_call(
        body, [dh_out, h_in, *flat_d, *[g for _, g in branches]], name=name, grid=(s // br,),
        in_specs=[row, row] + [row] * n_d + [vec] * n_b,
        out_specs=[row] + [vec] * n_b,
        out_shape=[jax.ShapeDtypeStruct((s, d), F32)] + [jax.ShapeDtypeStruct((1, d), F32)] * n_b,
        semantics=("arbitrary",), rides=rides)
    return (outs[0], list(outs[1:]), rode) if rides else (outs[0], list(outs[1:]))


def _split3(x):
    x0 = x.astype(BF16)
    r1 = x - x0.astype(F32)
    x1 = r1.astype(BF16)
    x2 = (r1 - x1.astype(F32)).astype(BF16)
    return x0, x1, x2


def _tri(n, kind):
    r = lax.broadcasted_iota(jnp.int32, (n, n), 0)
    c = lax.broadcasted_iota(jnp.int32, (n, n), 1)
    m = {"lt": r < c, "le": r <= c, "gt": r > c}[kind]
    return jnp.where(m, 1.0, 0.0).astype(BF16)


_GELU_C = math.sqrt(2.0 / math.pi)
_GELU_A = 0.044715


def _gelu(x):
    return 0.5 * x * (1.0 + jnp.tanh(_GELU_C * (x + _GELU_A * (x * x * x))))


def _gelu_grad(x):
    t = jnp.tanh(_GELU_C * (x + _GELU_A * (x * x * x)))
    return 0.5 * (1.0 + t) + 0.5 * x * (1.0 - t * t) * (_GELU_C * (1.0 + 3.0 * _GELU_A * (x * x)))


def _causal_w(w):
    r = lax.broadcasted_iota(jnp.int32, (TILE, TILE), 0)
    c = lax.broadcasted_iota(jnp.int32, (TILE, TILE), 1)
    return jnp.where(c <= r, w, 0.0)


def _uv_tiles(uv_ref, g, d_a, dq):
    cu, cv = g * TILE, d_a + g * TILE
    u = uv_ref[cu // dq, :, pl.ds(cu % dq, TILE)]
    v = uv_ref[cv // dq, :, pl.ds(cv % dq, TILE)]
    return u, v


def _gmlp_fwd(uv, v_g, w_s, bias, name, rides=()):
    _, s, dq = uv.shape
    d_a = 2 * dq
    n_g = d_a // TILE

    def body(uv_ref, vg_ref, ws_ref, b_ref, o_ref):
        for g in range(n_g):
            up, vp = _uv_tiles(uv_ref, g, d_a, dq)
            cols = pl.ds(g * TILE, TILE)
            vn = _rms(_gelu(vp), vg_ref[:, cols])
            mixed = jnp.dot(_causal_w(ws_ref[g]).astype(BF16), vn.astype(BF16), preferred_element_type=F32) + b_ref[:, cols]
            o_ref[:, cols] = (_gelu(up) * mixed).astype(BF16)

    return _hosted_call(
        body, [uv, v_g, w_s, bias], name=name, grid=(s // TILE,),
        in_specs=[
            pl.BlockSpec((4, TILE, dq), lambda i: (0, i, 0)),
            pl.BlockSpec((1, d_a), lambda i: (0, 0)),
            pl.BlockSpec((n_g, TILE, TILE), lambda i: (0, 0, 0)),
            pl.BlockSpec((TILE, d_a), lambda i: (0, 0)),
        ],
        out_specs=pl.BlockSpec((TILE, d_a), lambda i: (i, 0)),
        out_shape=jax.ShapeDtypeStruct((s, d_a), BF16),
        semantics=("parallel",), rides=rides)


def _gmlp_bwd(uv, dgm, v_g, w_s, bias, name, rides=()):
    _, s, dq = uv.shape
    d_a = 2 * dq
    n_g = d_a // TILE
    n_c = s // TILE

    def body(uv_ref, d_ref, vg_ref, ws_ref, b_ref, duv_ref, dws_ref, dbs_ref, dvg_ref, dbias_acc):
        i = pl.program_id(0)

        @pl.when(i == 0)
        def _():
            dws_ref[...] = jnp.zeros_like(dws_ref)
            dvg_ref[...] = jnp.zeros_like(dvg_ref)
            dbias_acc[...] = jnp.zeros_like(dbias_acc)

        for g in range(n_g):
            up, vp = _uv_tiles(uv_ref, g, d_a, dq)
            cols = pl.ds(g * TILE, TILE)
            vg = vg_ref[:, cols]
            u = _gelu(up)
            v = _gelu(vp)
            r = lax.rsqrt(jnp.mean(v * v, axis=-1, keepdims=True) + EPS)
            vh = v * r
            vn = (vh * vg).astype(BF16)
            wc = _causal_w(ws_ref[g]).astype(BF16)
            mixed = jnp.dot(wc, vn, preferred_element_type=F32) + b_ref[:, cols]
            d_out = d_ref[:, cols]
            du = d_out * mixed
            dmixed = d_out * u
            dmb = dmixed.astype(BF16)
            dvn = lax.dot_general(wc, dmb, _DIMS["tn"], preferred_element_type=F32)
            dws_ref[g] += lax.dot_general(dmb, vn, _DIMS["nt"], preferred_element_type=F32)
            dbias_acc[:, cols] += dmixed
            dvg_ref[:, cols] += jnp.sum(dvn * vh, axis=0, keepdims=True)
            gv = dvn * vg
            dv = r * (gv - vh * jnp.mean(gv * vh, axis=-1, keepdims=True))
            cu, cv = g * TILE, d_a + g * TILE
            duv_ref[cu // dq, :, pl.ds(cu % dq, TILE)] = (du * _gelu_grad(up)).astype(BF16)
            duv_ref[cv // dq, :, pl.ds(cv % dq, TILE)] = (dv * _gelu_grad(vp)).astype(BF16)

        @pl.when(i == n_c - 1)
        def _():
            ones = jnp.ones((SUBLANE, TILE), BF16)
            for g in range(n_g):
                dws_ref[g] = _causal_w(dws_ref[g])
                cols = pl.ds(g * TILE, TILE)
                out = None
                for t in _split3(dbias_acc[:, cols]):
                    p = lax.dot_general(ones, t, _DIMS["nt"], preferred_element_type=F32)
                    out = p if out is None else out + p
                dbs_ref[pl.ds(g * SUBLANE, SUBLANE), :] = out

    return _hosted_call(
        body, [uv, dgm, v_g, w_s, bias], name=name, grid=(n_c,), semantics=("arbitrary",), rides=rides,
        in_specs=[
            pl.BlockSpec((4, TILE, dq), lambda i: (0, i, 0)),
            pl.BlockSpec((TILE, d_a), lambda i: (i, 0)),
            pl.BlockSpec((1, d_a), lambda i: (0, 0)),
            pl.BlockSpec((n_g, TILE, TILE), lambda i: (0, 0, 0)),
            pl.BlockSpec((TILE, d_a), lambda i: (0, 0)),
        ],
        out_specs=[
            pl.BlockSpec((4, TILE, dq), lambda i: (0, i, 0)),
            pl.BlockSpec((n_g, TILE, TILE), lambda i: (0, 0, 0)),
            pl.BlockSpec((n_g * SUBLANE, TILE), lambda i: (0, 0)),
            pl.BlockSpec((1, d_a), lambda i: (0, 0)),
        ],
        out_shape=[
            jax.ShapeDtypeStruct((4, s, dq), BF16),
            jax.ShapeDtypeStruct((n_g, TILE, TILE), F32),
            jax.ShapeDtypeStruct((n_g * SUBLANE, TILE), F32),
            jax.ShapeDtypeStruct((1, d_a), F32),
        ],
        scratch_shapes=[pltpu.VMEM((TILE, d_a), F32)])


def _sigmoid(x):
    return 1.0 / (1.0 + jnp.exp(-x))


def _conv3(ext, w, b):
    return b + ((w[0:1] * pltpu.roll(ext, 2, 0) + w[1:2] * pltpu.roll(ext, 1, 0)) + w[2:3] * ext)


def _act_blocks(s, ns):
    return _pick(s, 512, ROWS), _pick(ns, 256)


def _ffn_act_fwd(a, cw, cb, name, rides=()):
    _, s, ns = a.shape
    bs, cb_w = _act_blocks(s, ns)
    hb = bs // SUBLANE

    def body(a_ref, prev_ref, cw_ref, cb_ref, o_ref):
        first = pl.program_id(0) == 0

        def conv(comp):
            prev = jnp.where(first, 0.0, prev_ref[comp])
            ext = jnp.concatenate([prev, a_ref[comp]], axis=0)
            return _conv3(ext, cw_ref[comp], cb_ref[comp])[SUBLANE:]

        for p in range(2):
            cg = conv(p)
            o_ref[p] = (cg * _sigmoid(cg) * conv(2 + p)).astype(BF16)

    hm, rode = _hosted_call(
        body, [a, a, cw, cb], name=name, grid=(s // bs, ns // cb_w),
        in_specs=[
            pl.BlockSpec((4, bs, cb_w), lambda i, j: (0, i, j)),
            pl.BlockSpec((4, SUBLANE, cb_w), lambda i, j: (0, jnp.maximum(i * hb - 1, 0), j)),
            pl.BlockSpec((4, 3, cb_w), lambda i, j: (0, 0, j)),
            pl.BlockSpec((4, 1, cb_w), lambda i, j: (0, 0, j)),
        ],
        out_specs=pl.BlockSpec((2, bs, cb_w), lambda i, j: (0, i, j)),
        out_shape=jax.ShapeDtypeStruct((2, s, ns), BF16),
        semantics=("parallel", "parallel"), rides=rides)
    return (hm, rode) if rides else hm


def _ffn_act_bwd(a, dhm, cw, cb, name, rides=()):
    _, s, ns = a.shape
    bs, cb_w = _act_blocks(s, ns)
    hb = bs // SUBLANE
    n_i = s // bs
    n_ext = bs + 2 * SUBLANE
    cur = slice(SUBLANE, SUBLANE + bs)

    def body(a_ref, prev_ref, next_ref, d_ref, dnext_ref, cw_ref, cb_ref, da_ref, dcw_ref, dcb_ref):
        i = pl.program_id(1)
        first, last = i == 0, i == n_i - 1

        @pl.when(first)
        def _():
            dcw_ref[...] = jnp.zeros_like(dcw_ref)
            dcb_ref[...] = jnp.zeros_like(dcb_ref)

        def ext_of(comp):
            return jnp.concatenate([jnp.where(first, 0.0, prev_ref[comp]), a_ref[comp], next_ref[comp]], axis=0)

        def back(comp, a_ext, dc):
            w = cw_ref[comp]
            da = (w[2:3] * dc + w[1:2] * pltpu.roll(dc, n_ext - 1, 0)) + w[0:1] * pltpu.roll(dc, n_ext - 2, 0)
            da_ref[comp] = da[cur].astype(BF16)
            dcc = dc[cur]
            dcw_ref[comp, 0:1, :] += jnp.sum(dcc * pltpu.roll(a_ext, 2, 0)[cur], axis=0, keepdims=True)
            dcw_ref[comp, 1:2, :] += jnp.sum(dcc * pltpu.roll(a_ext, 1, 0)[cur], axis=0, keepdims=True)
            dcw_ref[comp, 2:3, :] += jnp.sum(dcc * a_ext[cur], axis=0, keepdims=True)
            dcb_ref[comp] += jnp.sum(dcc, axis=0, keepdims=True)

        for p in range(2):
            ag, av = ext_of(p), ext_of(2 + p)
            cg = _conv3(ag, cw_ref[p], cb_ref[p])
            cv = _conv3(av, cw_ref[2 + p], cb_ref[2 + p])
            d = jnp.concatenate(
                [jnp.zeros((SUBLANE, cb_w), F32), d_ref[p], jnp.where(last, 0.0, dnext_ref[p])], axis=0)
            sg = _sigmoid(cg)
            back(2 + p, av, d * (cg * sg))
            back(p, ag, d * cv * (sg * (1.0 + cg * (1.0 - sg))))

    return _hosted_call(
        body, [a, a, a, dhm, dhm, cw, cb], name=name, grid=(ns // cb_w, n_i),
        in_specs=[
            pl.BlockSpec((4, bs, cb_w), lambda j, i: (0, i, j)),
            pl.BlockSpec((4, SUBLANE, cb_w), lambda j, i: (0, jnp.maximum(i * hb - 1, 0), j)),
            pl.BlockSpec((4, SUBLANE, cb_w), lambda j, i: (0, jnp.minimum((i + 1) * hb, n_i * hb - 1), j)),
            pl.BlockSpec((2, bs, cb_w), lambda j, i: (0, i, j)),
            pl.BlockSpec((2, SUBLANE, cb_w), lambda j, i: (0, jnp.minimum((i + 1) * hb, n_i * hb - 1), j)),
            pl.BlockSpec((4, 3, cb_w), lambda j, i: (0, 0, j)),
            pl.BlockSpec((4, 1, cb_w), lambda j, i: (0, 0, j)),
        ],
        out_specs=[
            pl.BlockSpec((4, bs, cb_w), lambda j, i: (0, i, j)),
            pl.BlockSpec((4, 3, cb_w), lambda j, i: (0, 0, j)),
            pl.BlockSpec((4, 1, cb_w), lambda j, i: (0, 0, j)),
        ],
        out_shape=[
            jax.ShapeDtypeStruct((4, s, ns), BF16),
            jax.ShapeDtypeStruct((4, 3, ns), F32),
            jax.ShapeDtypeStruct((4, 1, ns), F32),
        ],
        semantics=("parallel", "arbitrary"), rides=rides)


ATT_BQ_FWD = 2048
ATT_BQ_BWD = 1024
ATT_BK = 256
ATT_UNROLL = 2
ATT_UNROLL_BWD = 4


def _att_blocks(s, bq_pref):
    bq = _pick(s, bq_pref)
    bk = min(ATT_BK, bq)
    assert bq % bk == 0
    return bq, bk


def _dot_sel2(x, sel):
    hi = x.astype(BF16)
    lo = (x - hi.astype(F32)).astype(BF16)
    n = x.shape[0]
    both = jnp.dot(jnp.concatenate([hi, lo], axis=0), sel, preferred_element_type=F32)
    return both[:n] + both[n:]


def _causal_mask(bq, bk, row0, col0):
    rows = row0 + lax.broadcasted_iota(jnp.int32, (bq, bk), 0)
    cols = col0 + lax.broadcasted_iota(jnp.int32, (bq, bk), 1)
    return cols < rows


def _sb_tile(qb, kb, scale, mask):
    z = lax.dot_general(qb, kb, _DIMS["nt"], preferred_element_type=F32) * scale
    e = jnp.exp(-jnp.abs(z))
    lb = jnp.minimum(z, 0.0) - jnp.log(1.0 + e)
    l1m = lb - z
    if mask is not None:
        l1m = jnp.where(mask, l1m, 0.0)
    return z, e, lb, l1m


def _attn_fwd(q, k, v, name, rides=()):
    s, hd = q.shape
    bq, bk = _att_blocks(s, ATT_BQ_FWD)
    r = bq // bk
    unroll = math.gcd(r, ATT_UNROLL)
    n_h, n_q = hd // TILE, s // bq
    scale = 1.0 / math.sqrt(TILE)

    def body(q_ref, k_ref, v_ref, o_ref, l_ref, acc_ref, suf_ref):
        i = pl.program_id(1)
        qb = q_ref[...]
        later = _tri(bk, "gt")
        acc_ref[...] = jnp.zeros_like(acc_ref)
        suf_ref[...] = jnp.zeros_like(suf_ref)

        def tile(j, row0):
            rows = pl.ds(pl.multiple_of(j * bk, bk), bk)
            masked = row0 is not None
            r0 = row0 if masked else 0
            rs = pl.ds(r0, bq - r0)
            mask = _causal_mask(bq - r0, bk, i * bq + r0, j * bk) if masked else None
            _, _, lb, l1m = _sb_tile(qb[r0:], k_ref[rows, :], scale, mask)
            a = jnp.exp(lb + _dot_sel2(l1m, later) + suf_ref[rs, :])
            if masked:
                a = jnp.where(mask, a, 0.0)
            acc_ref[rs, :] += jnp.dot(a.astype(BF16), v_ref[rows, :], preferred_element_type=F32)
            suf_ref[rs, :] += jnp.sum(l1m, axis=1, keepdims=True)

        for dgl in range(r - 1, -1, -1):
            tile(r * i + dgl, dgl * bk)

        def step(t, carry):
            for u in range(unroll):
                tile(r * i - 1 - (unroll * t + u), None)
            return carry

        lax.fori_loop(0, (r * i) // unroll, step, 0)
        o_ref[...] = acc_ref[...].astype(BF16)
        l_ref[...] = jnp.broadcast_to(suf_ref[...], (bq, TILE))

    blk = pl.BlockSpec((bq, TILE), lambda h, i: (i, h))
    head = pl.BlockSpec((s, TILE), lambda h, i: (0, h))
    return _hosted_call(
        body, [q, k, v], name=name, grid=(n_h, n_q), in_specs=[blk, head, head], out_specs=[blk, blk],
        out_shape=[jax.ShapeDtypeStruct((s, hd), BF16), jax.ShapeDtypeStruct((s, hd), F32)],
        scratch_shapes=[pltpu.VMEM((bq, TILE), F32), pltpu.VMEM((bq, 1), F32)],
        semantics=("parallel", "parallel"), rides=rides)


def _attn_bwd(q, k, v, do, lsum, name, rides=()):
    s, hd = q.shape
    bq, bk = _att_blocks(s, ATT_BQ_BWD)
    r = bq // bk
    unroll = math.gcd(r, ATT_UNROLL_BWD)
    n_h, n_q = hd // TILE, s // bq
    scale = 1.0 / math.sqrt(TILE)

    def body(q_ref, k_ref, v_ref, do_ref, l_ref, dq_ref, dk_ref, dv_ref, dq_acc, pre_ref, cp_ref):
        i = pl.program_id(1)

        @pl.when(i == 0)
        def _():
            dk_ref[...] = jnp.zeros_like(dk_ref)
            dv_ref[...] = jnp.zeros_like(dv_ref)

        qb = q_ref[...]
        dob = do_ref[...]
        upto = _tri(bk, "le")
        before = _tri(bk, "lt")
        dq_acc[...] = jnp.zeros_like(dq_acc)
        pre_ref[...] = jnp.zeros_like(pre_ref)
        cp_ref[...] = jnp.zeros_like(cp_ref)

        def tile(j, row0):
            rows = pl.ds(pl.multiple_of(j * bk, bk), bk)
            kb, vb = k_ref[rows, :], v_ref[rows, :]
            masked = row0 is not None
            r0 = row0 if masked else 0
            rs = pl.ds(r0, bq - r0)
            qs, dos = qb[r0:], dob[r0:]
            mask = _causal_mask(bq - r0, bk, i * bq + r0, j * bk) if masked else None
            z, e, lb, l1m = _sb_tile(qs, kb, scale, mask)
            suffix = (l_ref[rs, 0:1] - pre_ref[rs, :]) - _dot_sel2(l1m, upto)
            a = jnp.exp(lb + suffix)
            if masked:
                a = jnp.where(mask, a, 0.0)
            p = a * lax.dot_general(dos, vb, _DIMS["nt"], preferred_element_type=F32)
            both = p + (cp_ref[rs, :] + jnp.dot(p.astype(BF16), before, preferred_element_type=F32))
            sg = jnp.where(z >= 0.0, 1.0, e) * pl.reciprocal(1.0 + e, approx=True)
            dz = p - both * sg
            if masked:
                dz = jnp.where(mask, dz, 0.0)
            dz = (dz * scale).astype(BF16)
            dq_acc[rs, :] += jnp.dot(dz, kb, preferred_element_type=F32)
            dk_ref[rows, :] += lax.dot_general(dz, qs, _DIMS["tn"], preferred_element_type=F32)
            dv_ref[rows, :] += lax.dot_general(a.astype(BF16), dos, _DIMS["tn"], preferred_element_type=F32)
            pre_ref[rs, :] += jnp.sum(l1m, axis=1, keepdims=True)
            cp_ref[rs, :] += jnp.sum(p, axis=1, keepdims=True)

        def step(j, carry):
            for u in range(unroll):
                tile(unroll * j + u, None)
            return carry

        lax.fori_loop(0, (r * i) // unroll, step, 0)
        for dgl in range(r):
            tile(r * i + dgl, dgl * bk)
        dq_ref[...] = dq_acc[...].astype(BF16)

    blk = pl.BlockSpec((bq, TILE), lambda h, i: (i, h))
    head = pl.BlockSpec((s, TILE), lambda h, i: (0, h))
    return _hosted_call(
        body, [q, k, v, do, lsum], name=name, grid=(n_h, n_q), in_specs=[blk, head, head, blk, blk],
        out_specs=[blk, head, head],
        out_shape=[jax.ShapeDtypeStruct((s, hd), BF16), jax.ShapeDtypeStruct((s, hd), F32),
                   jax.ShapeDtypeStruct((s, hd), F32)],
        scratch_shapes=[pltpu.VMEM((bq, TILE), F32), pltpu.VMEM((bq, 1), F32), pltpu.VMEM((bq, 1), F32)],
        semantics=("parallel", "arbitrary"), rides=rides)


EW_BLOCK = 512 * 1024


def _ew_blocks(r, c, elems=EW_BLOCK):
    return _pick(r, max(ROWS, elems // c // ROWS * ROWS), ROWS), c


def _cast_bf16(w, layer, chip_idx, name):
    _, r, c = w.shape
    br, bc = _ew_blocks(r, c)

    def body(chip_ref, w_ref, o_ref):
        o_ref[...] = w_ref[...].astype(BF16)

    return pl.pallas_call(
        body, name=name,
        grid_spec=pltpu.PrefetchScalarGridSpec(
            num_scalar_prefetch=1, grid=(r // br, c // bc),
            in_specs=[pl.BlockSpec((None, br, bc), lambda i, j, chip_ref: (layer, i, j))],
            out_specs=pl.BlockSpec((None, br, bc), lambda i, j, chip_ref: (chip_ref[0], i, j)),
        ),
        out_shape=jax.ShapeDtypeStruct((N_CHIPS, r, c), BF16), compiler_params=_cp("parallel", "parallel"),
    )(chip_idx, w)


def _pair_add(dw, recv, c_idx, name):
    _, r, c = dw.shape
    hr = r // 2
    br, bc = _ew_blocks(hr, c)
    nb = hr // br

    def body(c_ref, a_ref, b_ref, o_ref):
        o_ref[...] = (a_ref[...].astype(F32) + b_ref[...].astype(F32)).astype(BF16)

    return pl.pallas_call(
        body, name=name,
        grid_spec=pltpu.PrefetchScalarGridSpec(
            num_scalar_prefetch=1, grid=(N_CHIPS, nb, c // bc),
            in_specs=[
                pl.BlockSpec((None, br, bc), lambda s, i, j, c_ref: (s, c_ref[0] * nb + i, j)),
                pl.BlockSpec((None, br, bc), lambda s, i, j, c_ref: (s, i, j)),
            ],
            out_specs=pl.BlockSpec((None, br, bc), lambda s, i, j, c_ref: (s, i, j)),
        ),
        out_shape=jax.ShapeDtypeStruct((N_CHIPS, hr, c), BF16),
        compiler_params=_cp("parallel", "parallel", "parallel"),
    )(c_idx, dw, recv)


def _chip_sum(parts, dest, shape, layer, c_idx, name):
    _, hr, c = parts.shape
    br, bc = _ew_blocks(hr, c, EW_BLOCK // 2)
    nb = hr // br

    def body(c_ref, p_ref, *refs):
        o_ref = refs[-1]
        acc = p_ref[0].astype(F32)
        for s in range(1, N_CHIPS):
            acc = acc + p_ref[s].astype(F32)
        o_ref[...] = acc

    in_specs = [pl.BlockSpec((N_CHIPS, br, bc), lambda i, j, c_ref: (0, i, j))]
    operands = [c_idx, parts]
    aliases = {}
    if dest is not None:
        in_specs.append(ANY)
        operands.append(dest)
        aliases = {2: 0}
    return pl.pallas_call(
        body, name=name,
        grid_spec=pltpu.PrefetchScalarGridSpec(
            num_scalar_prefetch=1, grid=(nb, c // bc), in_specs=in_specs,
            out_specs=pl.BlockSpec((None, br, bc), lambda i, j, c_ref: (layer, c_ref[0] * nb + i, j)),
        ),
        out_shape=jax.ShapeDtypeStruct(shape, F32), input_output_aliases=aliases,
        compiler_params=_cp("parallel", "parallel"),
    )(*operands)


def _adamw(w, g, m, v, name, pass_g=False):
    n_l, r, c = w.shape
    br, bc = _ew_blocks(r, c, EW_BLOCK // 2)

    def body(w_ref, g_ref, m_ref, v_ref, *out_refs):
        d_ref, mo_ref, vo_ref = out_refs[-3:]
        g = g_ref[...]
        if pass_g:
            out_refs[0][...] = g
        m = ADAM_B1 * m_ref[...] + (1.0 - ADAM_B1) * g
        v = ADAM_B2 * v_ref[...] + (1.0 - ADAM_B2) * (g * g)
        m_hat = m / (1.0 - ADAM_B1 ** ADAM_STEP)
        v_hat = v / (1.0 - ADAM_B2 ** ADAM_STEP)
        d_ref[...] = -ADAM_LR * (m_hat / (jnp.sqrt(v_hat) + ADAM_EPS) + ADAM_WD * w_ref[...])
        mo_ref[...] = m
        vo_ref[...] = v

    blk = pl.BlockSpec((None, br, bc), lambda l, i, j: (l, i, j))
    n_out = 4 if pass_g else 3
    return pl.pallas_call(
        body, name=name, grid=(n_l, r // br, c // bc), in_specs=[blk] * 4, out_specs=[blk] * n_out,
        out_shape=[jax.ShapeDtypeStruct(w.shape, F32)] * n_out,
        compiler_params=_cp("parallel", "parallel", "parallel"),
    )(w, g, m, v)


def _place():
    x, y, c = lax.axis_index("x"), lax.axis_index("y"), lax.axis_index("c")
    chips = [(1 - x, y), (x, 1 - y), (1 - x, 1 - y)]
    return x, y, c, chips


class _Ride:
    def __init__(self, reads, bufs, new, n_sems, start, finish):
        self.reads, self.bufs, self.new, self.n_sems, self.start, self.finish = reads, bufs, new, n_sems, start, finish


def _hosted_call(body, operands, *, name, grid, in_specs, out_specs, out_shape, scratch_shapes=(), semantics=(), rides=()):
    single = not isinstance(out_shape, (list, tuple))
    out_specs = [out_specs] if single else list(out_specs)
    out_shape = [out_shape] if single else list(out_shape)
    in_specs, scratch_shapes = list(in_specs), list(scratch_shapes)
    n_in, n_out, n_scr = len(in_specs), len(out_shape), len(scratch_shapes)
    extra_in, extra_out, aliases, where = [], [], {}, []
    for ride in rides:
        r0 = len(extra_in)
        extra_in += list(ride.reads)
        b0 = len(extra_in)
        extra_in += list(ride.bufs)
        ob0 = len(extra_out)
        extra_out += [jax.ShapeDtypeStruct(b.shape, b.dtype) for b in ride.bufs]
        for t in range(len(ride.bufs)):
            aliases[n_in + b0 + t] = n_out + ob0 + t
        on0 = len(extra_out)
        extra_out += list(ride.new)
        where.append((r0, len(ride.reads), ob0, len(ride.bufs), on0, len(ride.new)))
    n_ein, n_eout = len(extra_in), len(extra_out)
    sem_shapes = [pltpu.SemaphoreType.DMA((max(1, k),)) for ride in rides for k in ride.n_sems]

    def full_body(*refs):
        ins, outs, scr = refs[:n_in + n_ein], refs[n_in + n_ein:n_in + n_ein + n_out + n_eout], refs[n_in + n_ein + n_out + n_eout:]

        def run(which):
            for idx, (ride, (r0, nr, ob0, nb, on0, nn)) in enumerate(zip(rides, where)):
                fn = ride.start if which == 0 else ride.finish
                fn(ins[n_in + r0:n_in + r0 + nr], outs[n_out + ob0:n_out + ob0 + nb], outs[n_out + on0:n_out + on0 + nn],
                   *scr[n_scr + 3 * idx:n_scr + 3 * idx + 3])

        host = lambda: body(*ins[:n_in], *outs[:n_out], *scr[:n_scr])
        if not rides:
            host()
        elif not grid:
            run(0)
            host()
            run(1)
        else:
            ids = [pl.program_id(ax) for ax in range(len(grid))]
            first = functools.reduce(jnp.logical_and, [i == 0 for i in ids])
            last = functools.reduce(jnp.logical_and, [i == g - 1 for i, g in zip(ids, grid)])
            pl.when(first)(lambda: run(0))
            host()
            pl.when(last)(lambda: run(1))

    if rides:
        params = pltpu.CompilerParams(dimension_semantics=("arbitrary",) * len(grid), vmem_limit_bytes=VMEM_LIMIT)
    else:
        params = _cp(*semantics)
    outs = pl.pallas_call(
        full_body, name=name, grid=grid,
        in_specs=in_specs + [ANY] * n_ein, out_specs=out_specs + [ANY] * n_eout,
        out_shape=out_shape + extra_out, input_output_aliases=aliases,
        scratch_shapes=scratch_shapes + sem_shapes, compiler_params=params,
    )(*operands, *extra_in)
    main = outs[0] if single else list(outs[:n_out])
    rode = [(list(outs[n_out + ob0:n_out + ob0 + nb]), list(outs[n_out + on0:n_out + on0 + nn]))
            for (_, _, ob0, nb, on0, nn) in where]
    return main, rode


def _run_rides(rides, name):
    return _hosted_call(lambda: None, [], name=name, grid=(), in_specs=[], out_specs=[], out_shape=[], rides=rides)[1]


def _ride_gather(slots, part=0, n_parts=1, span=1, stage=None):
    n = len(slots)
    halves = [a.shape[1] // 2 for a in slots]
    sizes = [hr // n_parts for hr in halves]
    assert part + span <= n_parts
    for a, hr, size in zip(slots, halves, sizes):
        assert a.shape[1] == 2 * hr and hr == size * n_parts and size % ROWS == 0, a.shape

    def remote(bufs, send_sems, recv_sems, i, k, slot, core, to):
        rows = bufs[i].at[slot, pl.ds(pl.multiple_of(core * halves[i] + part * sizes[i], ROWS), span * sizes[i])]
        return pltpu.make_async_remote_copy(
            src_ref=rows, dst_ref=rows, send_sem=send_sems.at[i * 6 + k], recv_sem=recv_sems.at[i * 6 + k],
            device_id=to, device_id_type=MESH)

    def each(fn):
        x, y, c, chips = _place()
        for i in range(n):
            for k, (px, py) in enumerate(chips):
                fn(x, y, c, i, k, px, py)

    def start(reads, bufs, new, send_sems, recv_sems, local_sems):
        cp = functools.partial(remote, bufs, send_sems, recv_sems)
        if stage != "d2d":
            each(lambda x, y, c, i, k, px, py: cp(i, k, 2 * x + y, c, (px, py, c)).start())
        else:
            each(lambda x, y, c, i, k, px, py: cp(i, 3 + k, 2 * px + py, c, (x, y, 1 - c)).start())

    def finish(reads, bufs, new, send_sems, recv_sems, local_sems):
        cp = functools.partial(remote, bufs, send_sems, recv_sems)

        def landed_over_ici(x, y, c, i, k, px, py):
            cp(i, k, 2 * px + py, c, (x, y, c)).wait_recv()
            if stage is None:
                cp(i, 3 + k, 2 * px + py, c, (x, y, 1 - c)).start()

        if stage != "d2d":
            each(landed_over_ici)
        if stage != "ici":
            each(lambda x, y, c, i, k, px, py: cp(i, 3 + k, 2 * px + py, 1 - c, (x, y, c)).wait_recv())
        if stage != "d2d":
            each(lambda x, y, c, i, k, px, py: cp(i, k, 2 * x + y, c, (px, py, c)).wait_send())
        if stage != "ici":
            each(lambda x, y, c, i, k, px, py: cp(i, 3 + k, 2 * px + py, c, (x, y, 1 - c)).wait_send())

    return _Ride([], slots, [], (6 * n, 6 * n, 0), start, finish)


def _ride_swap(grads):
    n = len(grads)
    halves = [a.shape[1] // 2 for a in grads]

    def copies(reads, new, send_sems, recv_sems):
        x, y, c, _ = _place()
        out = []
        for i in range(n):
            rows = pl.ds(pl.multiple_of((1 - c) * halves[i], 2 * SUBLANE), halves[i])
            out.append(pltpu.make_async_remote_copy(
                src_ref=reads[i].at[:, rows, :], dst_ref=new[i], send_sem=send_sems.at[i], recv_sem=recv_sems.at[i],
                device_id=(x, y, 1 - c), device_id_type=MESH))
        return out

    def start(reads, bufs, new, send_sems, recv_sems, local_sems):
        for cp in copies(reads, new, send_sems, recv_sems):
            cp.start()

    def finish(reads, bufs, new, send_sems, recv_sems, local_sems):
        for cp in copies(reads, new, send_sems, recv_sems):
            cp.wait()

    shapes = [jax.ShapeDtypeStruct((N_CHIPS, hr, a.shape[2]), a.dtype) for a, hr in zip(grads, halves)]
    return _Ride(grads, [], shapes, (n, n, 0), start, finish)


def _ride_scatter(parts, part=0, n_parts=1, into=None, span=1):
    n = len(parts)
    sizes = [a.shape[1] // n_parts for a in parts]
    assert part + span <= n_parts
    for a, size in zip(parts, sizes):
        assert a.shape[1] == size * n_parts and size % ROWS == 0, a.shape

    def piece(ref, i, slot):
        return ref.at[slot, pl.ds(part * sizes[i], span * sizes[i])]

    def own(reads, land, local_sems, i):
        me = 2 * lax.axis_index("x") + lax.axis_index("y")
        return pltpu.make_async_copy(piece(reads[i], i, me), piece(land[i], i, me), local_sems.at[i])

    def send(reads, land, send_sems, recv_sems, i, k):
        x, y, c, chips = _place()
        px, py = chips[k]
        return pltpu.make_async_remote_copy(
            src_ref=piece(reads[i], i, 2 * px + py), dst_ref=piece(land[i], i, 2 * x + y),
            send_sem=send_sems.at[3 * i + k], recv_sem=recv_sems.at[3 * i + k],
            device_id=(px, py, c), device_id_type=MESH)

    def start(reads, bufs, new, send_sems, recv_sems, local_sems):
        land = new if into is None else bufs
        for i in range(n):
            own(reads, land, local_sems, i).start()
            for k in range(3):
                send(reads, land, send_sems, recv_sems, i, k).start()

    def finish(reads, bufs, new, send_sems, recv_sems, local_sems):
        land = new if into is None else bufs
        x, y, c, chips = _place()
        for i in range(n):
            for k, (px, py) in enumerate(chips):
                slot = piece(land[i], i, 2 * px + py)
                pltpu.make_async_remote_copy(
                    src_ref=slot, dst_ref=slot, send_sem=send_sems.at[3 * i + k], recv_sem=recv_sems.at[3 * i + k],
                    device_id=(x, y, c), device_id_type=MESH).wait_recv()
        for i in range(n):
            for k in range(3):
                send(reads, land, send_sems, recv_sems, i, k).wait_send()
            own(reads, land, local_sems, i).wait()

    shapes = [jax.ShapeDtypeStruct(a.shape, a.dtype) for a in parts]
    if into is None:
        return _Ride(parts, [], shapes, (3 * n, 3 * n, n), start, finish)
    return _Ride(parts, list(into), [], (3 * n, 3 * n, n), start, finish)


def _ride_join(grads):
    n = len(grads)

    def copy(bufs, send_sems, recv_sems, i, core, to):
        hr = grads[i].shape[1] // 2
        rows = bufs[i].at[:, pl.ds(pl.multiple_of(core * hr, SUBLANE), hr), :]
        return pltpu.make_async_remote_copy(
            src_ref=rows, dst_ref=rows, send_sem=send_sems.at[i], recv_sem=recv_sems.at[i],
            device_id=to, device_id_type=MESH)

    def start(reads, bufs, new, send_sems, recv_sems, local_sems):
        x, y, c, _ = _place()
        for i in range(n):
            copy(bufs, send_sems, recv_sems, i, c, (x, y, 1 - c)).start()

    def finish(reads, bufs, new, send_sems, recv_sems, local_sems):
        x, y, c, _ = _place()
        for i in range(n):
            copy(bufs, send_sems, recv_sems, i, 1 - c, (x, y, c)).wait_recv()
        for i in range(n):
            copy(bufs, send_sems, recv_sems, i, c, (x, y, 1 - c)).wait_send()

    return _Ride([], grads, [], (n, n, 0), start, finish)


def _all_reduce_small(packed, name, rides=()):
    r, c = packed.shape
    chunk = _pick(r, 256, ROWS)

    def body(x_ref, out_ref, gath, send_sems, recv_sems, local_sem):
        x, y, cc, chips = _place()
        me, sibling = (x, y, cc), (x, y, 1 - cc)

        def slot(px, py, pc):
            return gath.at[4 * px + 2 * py + pc]

        def copy(k, block, to, src=None):
            return pltpu.make_async_remote_copy(
                src_ref=slot(*block) if src is None else src, dst_ref=slot(*block),
                send_sem=send_sems.at[k], recv_sem=recv_sems.at[k], device_id=to, device_id_type=MESH)

        mine = pltpu.make_async_copy(x_ref, slot(*me), local_sem)
        mine.start()
        first = [copy(0, me, sibling, src=x_ref)]
        first += [copy(1 + j, me, (*chip, cc), src=x_ref) for j, chip in enumerate(chips)]
        for cp in first:
            cp.start()
        passed = [copy(4 + j, (*chip, cc), sibling) for j, chip in enumerate(chips)]
        for j, chip in enumerate(chips):
            copy(1 + j, (*chip, cc), me).wait_recv()
            passed[j].start()
        copy(0, sibling, me).wait_recv()
        for j, chip in enumerate(chips):
            copy(4 + j, (*chip, 1 - cc), me).wait_recv()
        for cp in first + passed:
            cp.wait_send()
        mine.wait()

        def add(i, carry):
            rows = pl.ds(pl.multiple_of(i * chunk, SUBLANE), chunk)
            acc = gath[0, rows, :]
            for dev in range(1, N_DEV):
                acc = acc + gath[dev, rows, :]
            out_ref[rows, :] = acc
            return carry

        lax.fori_loop(0, r // chunk, add, 0)

    return _hosted_call(
        body, [packed], name=name, grid=(), in_specs=[VMEM_SPEC], out_specs=VMEM_SPEC,
        out_shape=jax.ShapeDtypeStruct((r, c), F32),
        scratch_shapes=[pltpu.VMEM((N_DEV, r, c), F32), pltpu.SemaphoreType.DMA((7,)),
                        pltpu.SemaphoreType.DMA((7,)), pltpu.SemaphoreType.DMA],
        rides=rides)


_PACK_ROWS = 256


def _pack(arrays):
    flat = jnp.concatenate([a.reshape(-1).astype(F32) for a in arrays])
    unit = _PACK_ROWS * LANE
    total = -(-flat.shape[0] // unit) * unit
    return jnp.pad(flat, (0, total - flat.shape[0])).reshape(-1, LANE)


def _unpack(packed, shapes, lead=()):
    flat = packed.reshape(lead + (-1,))
    out, at = [], 0
    for s in shapes:
        size = math.prod(s)
        out.append(flat[..., at:at + size].reshape(lead + tuple(s)))
        at += size
    return out


def kernel(x, pre_mix_g, post_mix_g, pre_ffn_g, post_ffn_g, a_w_in, a_v_norm_g, a_w_spatial, a_b_spatial, a_w_out, kv_norm_g, w_k, w_v, b_w_q, b_w_o, ffn_w_up, ffn_conv_w, ffn_conv_b, ffn_w_down, loss_target, m_pre_mix_g, m_post_mix_g, m_pre_ffn_g, m_post_ffn_g, m_a_w_in, m_a_v_norm_g, m_a_w_spatial, m_a_b_spatial, m_a_w_out, m_kv_norm_g, m_w_k, m_w_v, m_b_w_q, m_b_w_o, m_ffn_w_up, m_ffn_conv_w, m_ffn_conv_b, m_ffn_w_down, v_pre_mix_g, v_post_mix_g, v_pre_ffn_g, v_post_ffn_g, v_a_w_in, v_a_v_norm_g, v_a_w_spatial, v_a_b_spatial, v_a_w_out, v_kv_norm_g, v_w_k, v_w_v, v_b_w_q, v_b_w_o, v_ffn_w_up, v_ffn_conv_w, v_ffn_conv_b, v_ffn_w_down):
    xi, yi, ci = lax.axis_index("x"), lax.axis_index("y"), lax.axis_index("c")
    chip = 2 * xi + yi
    c_idx = jnp.reshape(ci, (1,)).astype(jnp.int32)
    _, s, d = x.shape
    n_layers = pre_mix_g.shape[0]
    assert n_layers == 2 and a_w_in.shape[0] == 1 and b_w_q.shape[0] == 1
    d_a = a_w_out.shape[1] * N_CHIPS
    n_g = a_w_spatial.shape[1]
    ns = ffn_w_up.shape[2]
    assert a_w_spatial.shape[2] == TILE and d_a == n_g * TILE and s % TILE == 0
    h0 = x[0]
    target = loss_target[0]

    big = {
        "win": (a_w_in, m_a_w_in, v_a_w_in),
        "wout": (a_w_out, m_a_w_out, v_a_w_out),
        "wk": (w_k[None], m_w_k[None], v_w_k[None]),
        "wv": (w_v[None], m_w_v[None], v_w_v[None]),
        "wq": (b_w_q, m_b_w_q, v_b_w_q),
        "wo": (b_w_o, m_b_w_o, v_b_w_o),
        "wup": (ffn_w_up, m_ffn_w_up, v_ffn_w_up),
        "wdn": (ffn_w_down, m_ffn_w_down, v_ffn_w_down),
    }
    units = [(nm, layer) for nm in big for layer in range(big[nm][0].shape[0])]
    chip_idx = jnp.reshape(chip, (1,)).astype(jnp.int32)
    shards = [_cast_bf16(big[nm][0], layer, chip_idx, f"cast_{nm}{layer}") for nm, layer in units]
    small_sharded = _pack([a_v_norm_g, ffn_conv_w])
    small_sharded = lax.dynamic_update_index_in_dim(
        jnp.zeros((N_CHIPS,) + small_sharded.shape, F32), small_sharded, chip, 0)
    own = dict(zip(units, shards))
    full = {}

    def gather_ride(keys):
        return _ride_gather([own[key] for key in keys])

    def gathered(keys, rode):
        full.update(zip(keys, rode[0]))

    first_keys = [("win", 0)]
    (first_bufs, _), = _run_rides([_ride_gather([own[key] for key in first_keys] + [small_sharded])], "gather_first")
    full.update(zip(first_keys, first_bufs[:-1]))
    vg_parts, cw_parts = _unpack(first_bufs[-1], [a_v_norm_g.shape, ffn_conv_w.shape], lead=(N_CHIPS,))
    v_g = jnp.transpose(vg_parts, (1, 0, 2)).reshape(1, d_a)

    def rows(nm, layer=0):
        w = full[(nm, layer)]
        return w.reshape(w.shape[0] * w.shape[1], w.shape[2])

    gains = lambda g, layer: g[layer:layer + 1]
    bias = jnp.repeat(a_b_spatial[0].T, TILE, axis=1)
    w_s = a_w_spatial[0]
    kv_g = kv_norm_g[None]
    conv_w = [cw_parts[:, layer] for layer in range(n_layers)]
    conv_b = [ffn_conv_b[layer].reshape(N_CHIPS, 1, ns) for layer in range(n_layers)]

    up0 = own[("wup", 0)]
    pieces = lambda p, span: _ride_gather([up0], part=p, n_parts=8, span=span)
    hn0 = _rms_fwd(h0, gains(pre_mix_g, 0), "norm_in")
    uv, ((out_bufs, _), ((up0,), _)) = _mm(
        hn0, full[("win", 0)], "nn", "gmlp_in", out_split=N_CHIPS, rides=[gather_ride([("wout", 0)]), pieces(0, 1)])
    full[("wout", 0)] = out_bufs[0]
    gm, (((up0,), _),) = _gmlp_fwd(uv, v_g, w_s, bias, "gmlp_gate", rides=[pieces(1, 2)])
    mix0, (((up0,), _),) = _mm(gm, rows("wout"), "nn", "gmlp_out", rides=[pieces(3, 2)])
    mix0 = mix0[0]
    (h1, hn1), (((up0,), _),) = _resid_rms(
        h0, mix0, gains(post_mix_g, 0), [gains(pre_ffn_g, 0)], "resid_mix0", rides=[pieces(5, 3)])
    full[("wup", 0)] = up0
    def leg(keys, stage):
        return _ride_gather([own[key] for key in keys], stage=stage)

    def first_leg_done(keys, rode):
        own.update(zip(keys, rode[0]))

    down0, qk, vo = [("wdn", 0)], [("wq", 0), ("wk", 0)], [("wv", 0), ("wo", 0)]
    a0, (rode,) = _mm(hn1, full[("wup", 0)], "nn", "ffn_up0", out_split=N_CHIPS, rides=[leg(down0, "ici")])
    first_leg_done(down0, rode)
    hm0, (rode, rode_qk) = _ffn_act_fwd(
        a0, conv_w[0], conv_b[0], "ffn_act0", rides=[leg(down0, "d2d"), leg(qk, "ici")])
    gathered(down0, rode)
    first_leg_done(qk, rode_qk)
    f0, (rode, rode_vo) = _mm(hm0, rows("wdn", 0), "nn", "ffn_down0", rides=[leg(qk, "d2d"), leg(vo, "ici")])
    gathered(qk, rode)
    first_leg_done(vo, rode_vo)
    f0 = f0[0]
    (h2, hn2, kvn), (rode,) = _resid_rms(
        h1, f0, gains(post_ffn_g, 0), [gains(pre_mix_g, 1), kv_g], "resid_ffn0", rides=[leg(vo, "d2d")])
    gathered(vo, rode)
    q = _mm(hn2, rows("wq"), "nn", "proj_q", out_dtype=BF16)[0]
    k = _mm(kvn, rows("wk"), "nn", "proj_k", out_dtype=BF16)[0]
    v = _mm(kvn, rows("wv"), "nn", "proj_v", out_dtype=BF16)[0]
    last_keys = [("wup", 1), ("wdn", 1)]
    (att, lsum), (rode,) = _attn_fwd(q, k, v, "attn_fwd", rides=[leg(last_keys, "ici")])
    first_leg_done(last_keys, rode)
    mix1, (rode,) = _mm(att, rows("wo"), "nn", "proj_o", rides=[leg(last_keys, "d2d")])
    gathered(last_keys, rode)
    mix1 = mix1[0]
    h3, hn3 = _resid_rms(h2, mix1, gains(post_mix_g, 1), [gains(pre_ffn_g, 1)], "resid_mix1")
    a1 = _mm(hn3, full[("wup", 1)], "nn", "ffn_up1", out_split=N_CHIPS)
    hm1 = _ffn_act_fwd(a1, conv_w[1], conv_b[1], "ffn_act1")
    f1 = _mm(hm1, rows("wdn", 1), "nn", "ffn_down1")[0]
    dh4, loss_tile = _loss_head(h3, f1, gains(post_ffn_g, 1), target, "loss_head")
    loss = lax.psum(loss_tile[0, 0], ("x", "y", "c"))

    dw = {}
    dg = {}

    pair = {}
    half_done = {nm: None for nm in big}

    def swap_ride(keys):
        return _ride_swap([dw[key] for key in keys])

    def swapped(keys, rode):
        for (nm, layer), got in zip(keys, rode[1]):
            pair[(nm, layer)] = _pair_add(dw[(nm, layer)], got, c_idx, f"pair_add_{nm}{layer}")

    def scatter_ride(keys):
        return _ride_scatter([pair[key] for key in keys])

    def scattered(keys, rode):
        for (nm, layer), got in zip(keys, rode[1]):
            half_done[nm] = _chip_sum(got, half_done[nm], big[nm][0].shape, layer, c_idx, f"chip_sum_{nm}{layer}")

    def ffn_bwd(dh_out, h_in, hn, a, hm, f, layer, act_rides=()):
        df, dg[("post_ffn", layer)] = _rms_bwd_out(dh_out, f, gains(post_ffn_g, layer), f"d_norm_ffn_out{layer}")
        dwd = _mm(hm, df, "tn", f"d_w_down{layer}", out_dtype=BF16)[0]
        down, up = [("wdn", layer)], [("wup", layer)]
        dw[down[0]] = dwd.reshape(N_CHIPS, dwd.shape[0] // N_CHIPS, d)
        dhm, (rode,) = _mm(df, rows("wdn", layer), "nt", f"d_ffn_mid{layer}", out_split=2, rides=[swap_ride(down)])
        swapped(down, rode)
        (da, dg[("conv_w", layer)], dg[("conv_b", layer)]), act_rode = _ffn_act_bwd(
            a, dhm, conv_w[layer], conv_b[layer], f"d_ffn_act{layer}", rides=act_rides)
        dw[up[0]], (rode,) = _mm(hn, da, "tn", f"d_w_up{layer}", out_dtype=BF16, out_split=N_CHIPS,
                                 rides=[scatter_ride(down)])
        scattered(down, rode)
        dhn, (rode,) = _mm(da, full[("wup", layer)], "nt", f"d_ffn_in{layer}", rides=[swap_ride(up)])
        swapped(up, rode)
        return dhn[0], act_rode

    dhn3, _ = ffn_bwd(dh4, h3, hn3, a1, hm1, f1, 1)
    dh3, (dg[("pre_ffn", 1)],) = _rms_bwd_in(dh4, h3, [([dhn3], gains(pre_ffn_g, 1))], "d_norm_ffn_in1")
    dmix1, dg[("post_mix", 1)] = _rms_bwd_out(dh3, mix1, gains(post_mix_g, 1), "d_norm_mix_out1")
    dwo = _mm(att, dmix1, "tn", "d_w_o", out_dtype=BF16)[0]
    dw[("wo", 0)] = dwo.reshape(N_CHIPS, dwo.shape[0] // N_CHIPS, d)
    datt = _mm(dmix1, rows("wo"), "nt", "d_attn_out", out_dtype=BF16)[0]
    ffn1_keys = [("wup", 1)]
    (dq, dk, dv), (rode,) = _attn_bwd(q, k, v, datt, lsum, "attn_bwd", rides=[scatter_ride(ffn1_keys)])
    scattered(ffn1_keys, rode)
    for nm, act, dact in (("wq", hn2, dq), ("wk", kvn, dk), ("wv", kvn, dv)):
        g = _mm(act, dact, "tn", f"d_{nm}", out_dtype=BF16)[0]
        dw[(nm, 0)] = g.reshape(N_CHIPS, g.shape[0] // N_CHIPS, g.shape[1])
    dhn2 = _mm(dq, rows("wq"), "nt", "d_q_in")[0]
    dkvn_k = _mm(dk, rows("wk"), "nt", "d_k_in")[0]
    attn_keys = [("wo", 0), ("wq", 0), ("wk", 0), ("wv", 0)]
    dkvn_v, (rode,) = _mm(dv, rows("wv"), "nt", "d_v_in", rides=[swap_ride(attn_keys)])
    swapped(attn_keys, rode)
    dh2, (dg[("pre_mix", 1)], dg["kv"]) = _rms_bwd_in(
        dh3, h2, [([dhn2], gains(pre_mix_g, 1)), ([dkvn_k, dkvn_v[0]], kv_g)], "d_norm_mix_in1")
    dhn1, (rode,) = ffn_bwd(dh2, h1, hn1, a0, hm0, f0, 0, act_rides=[scatter_ride(attn_keys)])
    scattered(attn_keys, rode)
    up0_pair = [pair[("wup", 0)]]
    up0_landed = [None]

    def up0_piece(part, span):
        return _ride_scatter(up0_pair, part, 8, into=up0_landed[0], span=span)

    def up0_rode(rode):
        up0_landed[0] = rode[1] if up0_landed[0] is None else rode[0]

    dh1, (dg[("pre_ffn", 0)],), (rode,) = _rms_bwd_in(
        dh2, h1, [([dhn1], gains(pre_ffn_g, 0))], "d_norm_ffn_in0", rides=[up0_piece(0, 1)])
    up0_rode(rode)
    dmix0, dg[("post_mix", 0)], (rode,) = _rms_bwd_out(
        dh1, mix0, gains(post_mix_g, 0), "d_norm_mix_out0", rides=[up0_piece(1, 1)])
    up0_rode(rode)
    early = ["wq", "wk", "wv", "wo", "wdn"]
    dwout, (((joined_early, _)),) = _mm(
        gm, dmix0, "tn", "d_w_out", out_dtype=BF16, rides=[_ride_join([half_done[nm] for nm in early])])
    grads_big = dict(zip(early, joined_early))
    w_out_key, w_in_key = [("wout", 0)], [("win", 0)]
    dw[w_out_key[0]] = dwout[0].reshape(N_CHIPS, dwout.shape[1] // N_CHIPS, d)
    dgm, (rode, up0) = _mm(dmix0, rows("wout"), "nt", "d_gmlp_gate", rides=[swap_ride(w_out_key), up0_piece(2, 1)])
    swapped(w_out_key, rode)
    up0_rode(up0)
    (duv, d_ws, d_bs, d_vg), (rode,) = _gmlp_bwd(uv, dgm[0], v_g, w_s, bias, "d_gmlp", rides=[up0_piece(3, 2)])
    up0_rode(rode)
    dw[w_in_key[0]], (rode, up0) = _mm(
        hn0, duv, "tn", "d_w_in", out_dtype=BF16, out_split=N_CHIPS, rides=[scatter_ride(w_out_key), up0_piece(5, 1)])
    scattered(w_out_key, rode)
    up0_rode(up0)
    dhn0, (rode, up0) = _mm(
        duv, full[("win", 0)], "nt", "d_gmlp_in", rides=[swap_ride(w_in_key), up0_piece(6, 2)])
    swapped(w_in_key, rode)
    up0_rode(up0)
    scattered([("wup", 0)], (None, up0_landed[0]))
    dx, (dg[("pre_mix", 0)],), (rode,) = _rms_bwd_in(
        dh1, h0, [([dhn0[0]], gains(pre_mix_g, 0))], "d_norm_in", rides=[scatter_ride(w_in_key)])
    scattered(w_in_key, rode)

    stack = lambda key: jnp.concatenate([dg[(key, layer)] for layer in range(n_layers)], axis=0)
    small_parts = [
        stack("pre_mix"), stack("post_mix"), stack("pre_ffn"), stack("post_ffn"),
        d_vg, d_ws, d_bs[::SUBLANE], dg["kv"],
        jnp.stack([dg[("conv_w", layer)] for layer in range(n_layers)]),
        jnp.stack([dg[("conv_b", layer)] for layer in range(n_layers)]),
    ]
    late = [nm for nm in big if nm not in early]
    summed, ((joined_late, _),) = _all_reduce_small(
        _pack(small_parts), "small_grads_sum", rides=[_ride_join([half_done[nm] for nm in late])])
    grads_big.update(zip(late, joined_late))
    (g_pre_mix, g_post_mix, g_pre_ffn, g_post_ffn, g_vg, g_ws, g_bs, g_kv, g_cw, g_cb) = _unpack(
        summed, [p.shape for p in small_parts])
    g_vg = lax.dynamic_index_in_dim(g_vg.reshape(N_CHIPS, 1, d_a // N_CHIPS), chip, 0, keepdims=False)
    g_cw = lax.dynamic_index_in_dim(g_cw, chip, 1, keepdims=False)
    g_cb = g_cb.reshape(n_layers, N_CHIPS * ns)
    small = [
        (pre_mix_g, g_pre_mix, m_pre_mix_g, v_pre_mix_g),
        (post_mix_g, g_post_mix, m_post_mix_g, v_post_mix_g),
        (pre_ffn_g, g_pre_ffn, m_pre_ffn_g, v_pre_ffn_g),
        (post_ffn_g, g_post_ffn, m_post_ffn_g, v_post_ffn_g),
        (a_v_norm_g, g_vg, m_a_v_norm_g, v_a_v_norm_g),
        (a_w_spatial, g_ws[None], m_a_w_spatial, v_a_w_spatial),
        (a_b_spatial, g_bs[None], m_a_b_spatial, v_a_b_spatial),
        (kv_norm_g, g_kv.reshape(d), m_kv_norm_g, v_kv_norm_g),
        (ffn_conv_w, g_cw, m_ffn_conv_w, v_ffn_conv_w),
        (ffn_conv_b, g_cb, m_ffn_conv_b, v_ffn_conv_b),
    ]
    small = [(w, g.reshape(w.shape), m, v) for w, g, m, v in small]
    packed = [_pack([t[i] for t in small])[None] for i in range(4)]
    small_new = [_unpack(p[0], [t[0].shape for t in small]) for p in _adamw(*packed, "adamw_small")]

    new_big = {nm: _adamw(big[nm][0], grads_big[nm], big[nm][1], big[nm][2], f"adamw_{nm}", pass_g=True)
               for nm in big}

    def big_out(nm, which):
        ref_shape = {"wk": w_k.shape, "wv": w_v.shape}.get(nm, big[nm][0].shape)
        return new_big[nm][which].reshape(ref_shape)

    order = ["pre_mix", "post_mix", "pre_ffn", "post_ffn", "win", "vg", "ws", "bs", "wout", "kv", "wk", "wv", "wq",
             "wo", "wup", "cw", "cb", "wdn"]
    small_at = {"pre_mix": 0, "post_mix": 1, "pre_ffn": 2, "post_ffn": 3, "vg": 4, "ws": 5, "bs": 6, "kv": 7,
                "cw": 8, "cb": 9}
    outs = [loss, dx[None]]
    for which in range(4):
        for nm in order:
            if nm in small_at:
                outs.append(small[small_at[nm]][1] if which == 0 else small_new[which - 1][small_at[nm]])
            else:
                outs.append(big_out(nm, which))
    return tuple(outs)
```

```python
import functools
import math

import jax
import jax.numpy as jnp
from jax import lax
from jax.experimental import pallas as pl
from jax.experimental.pallas import tpu as pltpu

F32 = jnp.float32
BF16 = jnp.bfloat16
EPS = 1e-6
ADAM_LR = 0.001
ADAM_B1 = 0.9
ADAM_B2 = 0.999
ADAM_EPS = 1e-08
ADAM_WD = 0.01
ADAM_STEP = 10

LANE = 128
SUBLANE = 8
ROWS = 16
TILE = 128
N_CHIPS = 4
N_DEV = 8
VMEM_LIMIT = 56 * 1024 * 1024
MM_VMEM = 46 * 1024 * 1024
MXU_WIDTH = 256
MESH = pl.DeviceIdType.MESH
ANY = pl.BlockSpec(memory_space=pl.ANY)
VMEM_SPEC = pl.BlockSpec(memory_space=pltpu.VMEM)


def _cp(*sem):
    return pltpu.CompilerParams(dimension_semantics=sem, vmem_limit_bytes=VMEM_LIMIT)


def _pick(dim, pref, align=LANE):
    if dim <= pref:
        return dim
    best = None
    for d in range(align, pref + 1, align):
        if dim % d == 0:
            best = d
    assert best is not None, (dim, pref)
    return best


_DIMS = {
    "nn": (((1,), (0,)), ((), ())),
    "nt": (((1,), (1,)), ((), ())),
    "tn": (((0,), (0,)), ((), ())),
}


def _as3(a):
    return a if a.ndim == 3 else a[None]


def _spec3(br, bc, cols_j, rc):
    per = cols_j // bc

    def imap(m, n, k):
        r, c = rc(m, n, k)
        return (c // per, r, c % per)

    return pl.BlockSpec((None, br, bc), imap)


def _mm(a, b, mode, name, out_dtype=F32, out_split=1, rides=()):
    a, b = _as3(a), _as3(b)
    ja, ra, caj = a.shape
    jb, rb, cbj = b.shape
    if mode == "nn":
        m, k, n = ra, ja * caj, jb * cbj
        assert rb == k
        m_ext, k_ext, n_ext = [ra], [caj, rb], [cbj]
    elif mode == "nt":
        m, k, n = ra, ja * caj, rb
        assert jb * cbj == k
        m_ext, k_ext, n_ext = [ra], [caj, cbj], [rb]
    else:
        m, k, n = ja * caj, ra, jb * cbj
        assert rb == k
        m_ext, k_ext, n_ext = [caj], [ra], [cbj]
    assert n % out_split == 0
    n_ext.append(n // out_split)
    bm = _pick(math.gcd(*m_ext), 1536)
    n_unit = math.gcd(*n_ext)
    bn = _pick(n_unit, 1536)
    k_unit = math.gcd(*k_ext)
    o_bytes = jnp.dtype(out_dtype).itemsize

    def vmem_need(bm, bn, bk):
        tiles = bm * bk * a.dtype.itemsize + bk * bn * b.dtype.itemsize + bm * bn * o_bytes
        return 2 * tiles + bm * bn * 4 * (2 if bk < k else 1)

    def deepest(bm, bn):
        return max(d for d in range(LANE, k_unit + 1, LANE)
                   if k_unit % d == 0 and (d == LANE or vmem_need(bm, bn, d) <= MM_VMEM))

    bk = deepest(bm, bn)
    if bk < k_unit and k_unit == k:
        if bm % (2 * LANE) == 0 and deepest(bm // 2, bn) == k:
            bm, bk = bm // 2, k
        elif bn % (2 * LANE) == 0 and deepest(bm, bn // 2) == k:
            bn, bk = bn // 2, k
    n_outer = False
    if bn % MXU_WIDTH and n_unit % MXU_WIDTH == 0 and k_unit == k:
        for rows in (bm, bm // 2, bm // 4):
            if rows % LANE == 0 and vmem_need(rows, n_unit, k) <= MM_VMEM:
                bm, bn, bk, n_outer = rows, n_unit, k, True
                break
    nk = k // bk
    order = (lambda f: lambda ni, mi, ki: f(mi, ni, ki)) if n_outer else (lambda f: f)
    if mode == "nn":
        a_spec = _spec3(bm, bk, caj, order(lambda mi, ni, ki: (mi, ki)))
        b_spec = _spec3(bk, bn, cbj, order(lambda mi, ni, ki: (ki, ni)))
    elif mode == "nt":
        a_spec = _spec3(bm, bk, caj, order(lambda mi, ni, ki: (mi, ki)))
        b_spec = _spec3(bn, bk, cbj, order(lambda mi, ni, ki: (ni, ki)))
    else:
        a_spec = _spec3(bk, bm, caj, order(lambda mi, ni, ki: (ki, mi)))
        b_spec = _spec3(bk, bn, cbj, order(lambda mi, ni, ki: (ki, ni)))
    o_spec = _spec3(bm, bn, n // out_split, order(lambda mi, ni, ki: (mi, ni)))
    dims = _DIMS[mode]

    def body(a_ref, b_ref, o_ref, *acc):
        def part():
            return lax.dot_general(a_ref[...].astype(BF16), b_ref[...].astype(BF16), dims, preferred_element_type=F32)

        if nk == 1:
            o_ref[...] = part().astype(o_ref.dtype)
            return
        acc_ref, = acc
        ki = pl.program_id(2)

        @pl.when(ki == 0)
        def _():
            acc_ref[...] = part()

        @pl.when(jnp.logical_and(ki > 0, ki < nk - 1))
        def _():
            acc_ref[...] += part()

        @pl.when(ki == nk - 1)
        def _():
            o_ref[...] = (acc_ref[...] + part()).astype(o_ref.dtype)

    grid = (n // bn, m // bm, nk) if n_outer else (m // bm, n // bn, nk)
    out, rode = _hosted_call(
        body, [a, b], name=name, grid=grid, in_specs=[a_spec, b_spec], out_specs=o_spec,
        out_shape=jax.ShapeDtypeStruct((out_split, m, n // out_split), out_dtype),
        scratch_shapes=[pltpu.VMEM((bm, bn), F32)] if nk > 1 else [],
        semantics=("parallel", "parallel", "arbitrary"), rides=rides)
    return (out, rode) if rides else out


def _rms(x, g):
    r = lax.rsqrt(jnp.mean(x * x, axis=-1, keepdims=True) + EPS)
    return x * r * g


def _rms_bwd(x, g, dy):
    r = lax.rsqrt(jnp.mean(x * x, axis=-1, keepdims=True) + EPS)
    xh = x * r
    gy = dy * g
    dx = r * (gy - xh * jnp.mean(gy * xh, axis=-1, keepdims=True))
    return dx, jnp.sum(dy * xh, axis=0, keepdims=True)


def _row_block(s, streams):
    return _pick(s, 512 if streams <= 4 else 256, ROWS)


def _rms_fwd(h, g, name):
    s, d = h.shape
    br = _row_block(s, 2)

    def body(h_ref, g_ref, o_ref):
        o_ref[...] = _rms(h_ref[...], g_ref[...]).astype(BF16)

    row = pl.BlockSpec((br, d), lambda i: (i, 0))
    vec = pl.BlockSpec((1, d), lambda i: (0, 0))
    return pl.pallas_call(
        body, name=name, grid=(s // br,), in_specs=[row, vec], out_specs=row,
        out_shape=jax.ShapeDtypeStruct((s, d), BF16), compiler_params=_cp("parallel"),
    )(h, g)


def _resid_rms(h_in, f, g_post, g_next, name, rides=()):
    s, d = h_in.shape
    br = _row_block(s, 3 + (len(g_next) + 1) // 2)
    n_next = len(g_next)

    def body(h_ref, f_ref, gp_ref, *refs):
        gn_refs, ho_ref, hn_refs = refs[:n_next], refs[n_next], refs[n_next + 1:]
        h = h_ref[...] + _rms(f_ref[...], gp_ref[...])
        ho_ref[...] = h
        for gn_ref, hn_ref in zip(gn_refs, hn_refs):
            hn_ref[...] = _rms(h, gn_ref[...]).astype(BF16)

    row = pl.BlockSpec((br, d), lambda i: (i, 0))
    vec = pl.BlockSpec((1, d), lambda i: (0, 0))
    outs, rode = _hosted_call(
        body, [h_in, f, g_post, *g_next], name=name, grid=(s // br,),
        in_specs=[row, row, vec] + [vec] * n_next,
        out_specs=[row] * (1 + n_next),
        out_shape=[jax.ShapeDtypeStruct((s, d), F32)] + [jax.ShapeDtypeStruct((s, d), BF16)] * n_next,
        semantics=("parallel",), rides=rides)
    return (outs, rode) if rides else outs


def _loss_head(h_in, f, g_post, target, name):
    s, d = h_in.shape
    br = _row_block(s, 4)

    def body(h_ref, f_ref, gp_ref, t_ref, dh_ref, loss_ref):
        @pl.when(pl.program_id(0) == 0)
        def _():
            loss_ref[...] = jnp.zeros_like(loss_ref)

        diff = h_ref[...] + _rms(f_ref[...], gp_ref[...]) - t_ref[...]
        dh_ref[...] = diff * (1.0 / d)
        loss_ref[...] += 0.5 * jnp.sum(jnp.mean(diff * diff, axis=-1, keepdims=True))

    row = pl.BlockSpec((br, d), lambda i: (i, 0))
    vec = pl.BlockSpec((1, d), lambda i: (0, 0))
    return pl.pallas_call(
        body, name=name, grid=(s // br,),
        in_specs=[row, row, vec, row],
        out_specs=[row, pl.BlockSpec((SUBLANE, LANE), lambda i: (0, 0))],
        out_shape=[jax.ShapeDtypeStruct((s, d), F32), jax.ShapeDtypeStruct((SUBLANE, LANE), F32)],
        compiler_params=_cp("arbitrary"),
    )(h_in, f, g_post, target)


def _rms_bwd_out(dy, f, g, name, rides=()):
    s, d = f.shape
    br = _row_block(s, 3)

    def body(dy_ref, f_ref, g_ref, df_ref, dg_ref):
        @pl.when(pl.program_id(0) == 0)
        def _():
            dg_ref[...] = jnp.zeros_like(dg_ref)

        dx, dg = _rms_bwd(f_ref[...], g_ref[...], dy_ref[...])
        df_ref[...] = dx.astype(BF16)
        dg_ref[...] += dg

    row = pl.BlockSpec((br, d), lambda i: (i, 0))
    vec = pl.BlockSpec((1, d), lambda i: (0, 0))
    (df, dg), rode = _hosted_call(
        body, [dy, f, g], name=name, grid=(s // br,), in_specs=[row, row, vec], out_specs=[row, vec],
        out_shape=[jax.ShapeDtypeStruct((s, d), BF16), jax.ShapeDtypeStruct((1, d), F32)],
        semantics=("arbitrary",), rides=rides)
    return (df, dg, rode) if rides else (df, dg)


def _rms_bwd_in(dh_out, h_in, branches, name, rides=()):
    s, d = h_in.shape
    br = _row_block(s, 3 + sum(len(ds) for ds, _ in branches))
    counts = [len(ds) for ds, _ in branches]
    n_d = sum(counts)
    n_b = len(branches)

    def body(dho_ref, h_ref, *refs):
        d_refs, g_refs = refs[:n_d], refs[n_d:n_d + n_b]
        dh_ref, dg_refs = refs[n_d + n_b], refs[n_d + n_b + 1:]

        @pl.when(pl.program_id(0) == 0)
        def _():
            for r in dg_refs:
                r[...] = jnp.zeros_like(r)

        h = h_ref[...]
        acc = dho_ref[...]
        at = 0
        for bi, cnt in enumerate(counts):
            dn = d_refs[at][...]
            for r in d_refs[at + 1:at + cnt]:
                dn = dn + r[...]
            at += cnt
            dx, dg = _rms_bwd(h, g_refs[bi][...], dn)
            acc = acc + dx
            dg_refs[bi][...] += dg
        dh_ref[...] = acc

    row = pl.BlockSpec((br, d), lambda i: (i, 0))
    vec = pl.BlockSpec((1, d), lambda i: (0, 0))
    flat_d = [x for ds, _ in branches for x in ds]
    outs, rode = _hosted_call(
        body, [dh_out, h_in, *flat_d, *[g for _, g in branches]], name=name, grid=(s // br,),
        in_specs=[row, row] + [row] * n_d + [vec] * n_b,
        out_specs=[row] + [vec] * n_b,
        out_shape=[jax.ShapeDtypeStruct((s, d), F32)] + [jax.ShapeDtypeStruct((1, d), F32)] * n_b,
        semantics=("arbitrary",), rides=rides)
    return (outs[0], list(outs[1:]), rode) if rides else (outs[0], list(outs[1:]))


def _split3(x):
    x0 = x.astype(BF16)
    r1 = x - x0.astype(F32)
    x1 = r1.astype(BF16)
    x2 = (r1 - x1.astype(F32)).astype(BF16)
    return x0, x1, x2


def _tri(n, kind):
    r = lax.broadcasted_iota(jnp.int32, (n, n), 0)
    c = lax.broadcasted_iota(jnp.int32, (n, n), 1)
    m = {"lt": r < c, "le": r <= c, "gt": r > c}[kind]
    return jnp.where(m, 1.0, 0.0).astype(BF16)


_GELU_C = math.sqrt(2.0 / math.pi)
_GELU_A = 0.044715


def _gelu(x):
    return 0.5 * x * (1.0 + jnp.tanh(_GELU_C * (x + _GELU_A * (x * x * x))))


def _gelu_grad(x):
    t = jnp.tanh(_GELU_C * (x + _GELU_A * (x * x * x)))
    return 0.5 * (1.0 + t) + 0.5 * x * (1.0 - t * t) * (_GELU_C * (1.0 + 3.0 * _GELU_A * (x * x)))


def _causal_w(w):
    r = lax.broadcasted_iota(jnp.int32, (TILE, TILE), 0)
    c = lax.broadcasted_iota(jnp.int32, (TILE, TILE), 1)
    return jnp.where(c <= r, w, 0.0)


def _uv_tiles(uv_ref, g, d_a, dq):
    cu, cv = g * TILE, d_a + g * TILE
    u = uv_ref[cu // dq, :, pl.ds(cu % dq, TILE)]
    v = uv_ref[cv // dq, :, pl.ds(cv % dq, TILE)]
    return u, v


def _gmlp_fwd(uv, v_g, w_s, bias, name, rides=()):
    _, s, dq = uv.shape
    d_a = 2 * dq
    n_g = d_a // TILE

    def body(uv_ref, vg_ref, ws_ref, b_ref, o_ref):
        for g in range(n_g):
            up, vp = _uv_tiles(uv_ref, g, d_a, dq)
            cols = pl.ds(g * TILE, TILE)
            vn = _rms(_gelu(vp), vg_ref[:, cols])
            mixed = jnp.dot(_causal_w(ws_ref[g]).astype(BF16), vn.astype(BF16), preferred_element_type=F32) + b_ref[:, cols]
            o_ref[:, cols] = (_gelu(up) * mixed).astype(BF16)

    return _hosted_call(
        body, [uv, v_g, w_s, bias], name=name, grid=(s // TILE,),
        in_specs=[
            pl.BlockSpec((4, TILE, dq), lambda i: (0, i, 0)),
            pl.BlockSpec((1, d_a), lambda i: (0, 0)),
            pl.BlockSpec((n_g, TILE, TILE), lambda i: (0, 0, 0)),
            pl.BlockSpec((TILE, d_a), lambda i: (0, 0)),
        ],
        out_specs=pl.BlockSpec((TILE, d_a), lambda i: (i, 0)),
        out_shape=jax.ShapeDtypeStruct((s, d_a), BF16),
        semantics=("parallel",), rides=rides)


def _gmlp_bwd(uv, dgm, v_g, w_s, bias, name, rides=()):
    _, s, dq = uv.shape
    d_a = 2 * dq
    n_g = d_a // TILE
    n_c = s // TILE

    def body(uv_ref, d_ref, vg_ref, ws_ref, b_ref, duv_ref, dws_ref, dbs_ref, dvg_ref, dbias_acc):
        i = pl.program_id(0)

        @pl.when(i == 0)
        def _():
            dws_ref[...] = jnp.zeros_like(dws_ref)
            dvg_ref[...] = jnp.zeros_like(dvg_ref)
            dbias_acc[...] = jnp.zeros_like(dbias_acc)

        for g in range(n_g):
            up, vp = _uv_tiles(uv_ref, g, d_a, dq)
            cols = pl.ds(g * TILE, TILE)
            vg = vg_ref[:, cols]
            u = _gelu(up)
            v = _gelu(vp)
            r = lax.rsqrt(jnp.mean(v * v, axis=-1, keepdims=True) + EPS)
            vh = v * r
            vn = (vh * vg).astype(BF16)
            wc = _causal_w(ws_ref[g]).astype(BF16)
            mixed = jnp.dot(wc, vn, preferred_element_type=F32) + b_ref[:, cols]
            d_out = d_ref[:, cols]
            du = d_out * mixed
            dmixed = d_out * u
            dmb = dmixed.astype(BF16)
            dvn = lax.dot_general(wc, dmb, _DIMS["tn"], preferred_element_type=F32)
            dws_ref[g] += lax.dot_general(dmb, vn, _DIMS["nt"], preferred_element_type=F32)
            dbias_acc[:, cols] += dmixed
            dvg_ref[:, cols] += jnp.sum(dvn * vh, axis=0, keepdims=True)
            gv = dvn * vg
            dv = r * (gv - vh * jnp.mean(gv * vh, axis=-1, keepdims=True))
            cu, cv = g * TILE, d_a + g * TILE
            duv_ref[cu // dq, :, pl.ds(cu % dq, TILE)] = (du * _gelu_grad(up)).astype(BF16)
            duv_ref[cv // dq, :, pl.ds(cv % dq, TILE)] = (dv * _gelu_grad(vp)).astype(BF16)

        @pl.when(i == n_c - 1)
        def _():
            ones = jnp.ones((SUBLANE, TILE), BF16)
            for g in range(n_g):
                dws_ref[g] = _causal_w(dws_ref[g])
                cols = pl.ds(g * TILE, TILE)
                out = None
                for t in _split3(dbias_acc[:, cols]):
                    p = lax.dot_general(ones, t, _DIMS["nt"], preferred_element_type=F32)
                    out = p if out is None else out + p
                dbs_ref[pl.ds(g * SUBLANE, SUBLANE), :] = out

    return _hosted_call(
        body, [uv, dgm, v_g, w_s, bias], name=name, grid=(n_c,), semantics=("arbitrary",), rides=rides,
        in_specs=[
            pl.BlockSpec((4, TILE, dq), lambda i: (0, i, 0)),
            pl.BlockSpec((TILE, d_a), lambda i: (i, 0)),
            pl.BlockSpec((1, d_a), lambda i: (0, 0)),
            pl.BlockSpec((n_g, TILE, TILE), lambda i: (0, 0, 0)),
            pl.BlockSpec((TILE, d_a), lambda i: (0, 0)),
        ],
        out_specs=[
            pl.BlockSpec((4, TILE, dq), lambda i: (0, i, 0)),
            pl.BlockSpec((n_g, TILE, TILE), lambda i: (0, 0, 0)),
            pl.BlockSpec((n_g * SUBLANE, TILE), lambda i: (0, 0)),
            pl.BlockSpec((1, d_a), lambda i: (0, 0)),
        ],
        out_shape=[
            jax.ShapeDtypeStruct((4, s, dq), BF16),
            jax.ShapeDtypeStruct((n_g, TILE, TILE), F32),
            jax.ShapeDtypeStruct((n_g * SUBLANE, TILE), F32),
            jax.ShapeDtypeStruct((1, d_a), F32),
        ],
        scratch_shapes=[pltpu.VMEM((TILE, d_a), F32)])


def _sigmoid(x):
    return 1.0 / (1.0 + jnp.exp(-x))


def _conv3(ext, w, b):
    return b + ((w[0:1] * pltpu.roll(ext, 2, 0) + w[1:2] * pltpu.roll(ext, 1, 0)) + w[2:3] * ext)


def _act_blocks(s, ns):
    return _pick(s, 512, ROWS), _pick(ns, 256)


def _ffn_act_fwd(a, cw, cb, name, rides=()):
    _, s, ns = a.shape
    bs, cb_w = _act_blocks(s, ns)
    hb = bs // SUBLANE

    def body(a_ref, prev_ref, cw_ref, cb_ref, o_ref):
        first = pl.program_id(0) == 0

        def conv(comp):
            prev = jnp.where(first, 0.0, prev_ref[comp])
            ext = jnp.concatenate([prev, a_ref[comp]], axis=0)
            return _conv3(ext, cw_ref[comp], cb_ref[comp])[SUBLANE:]

        for p in range(2):
            cg = conv(p)
            o_ref[p] = (cg * _sigmoid(cg) * conv(2 + p)).astype(BF16)

    hm, rode = _hosted_call(
        body, [a, a, cw, cb], name=name, grid=(s // bs, ns // cb_w),
        in_specs=[
            pl.BlockSpec((4, bs, cb_w), lambda i, j: (0, i, j)),
            pl.BlockSpec((4, SUBLANE, cb_w), lambda i, j: (0, jnp.maximum(i * hb - 1, 0), j)),
            pl.BlockSpec((4, 3, cb_w), lambda i, j: (0, 0, j)),
            pl.BlockSpec((4, 1, cb_w), lambda i, j: (0, 0, j)),
        ],
        out_specs=pl.BlockSpec((2, bs, cb_w), lambda i, j: (0, i, j)),
        out_shape=jax.ShapeDtypeStruct((2, s, ns), BF16),
        semantics=("parallel", "parallel"), rides=rides)
    return (hm, rode) if rides else hm


def _ffn_act_bwd(a, dhm, cw, cb, name, rides=()):
    _, s, ns = a.shape
    bs, cb_w = _act_blocks(s, ns)
    hb = bs // SUBLANE
    n_i = s // bs
    n_ext = bs + 2 * SUBLANE
    cur = slice(SUBLANE, SUBLANE + bs)

    def body(a_ref, prev_ref, next_ref, d_ref, dnext_ref, cw_ref, cb_ref, da_ref, dcw_ref, dcb_ref):
        i = pl.program_id(1)
        first, last = i == 0, i == n_i - 1

        @pl.when(first)
        def _():
            dcw_ref[...] = jnp.zeros_like(dcw_ref)
            dcb_ref[...] = jnp.zeros_like(dcb_ref)

        def ext_of(comp):
            return jnp.concatenate([jnp.where(first, 0.0, prev_ref[comp]), a_ref[comp], next_ref[comp]], axis=0)

        def back(comp, a_ext, dc):
            w = cw_ref[comp]
            da = (w[2:3] * dc + w[1:2] * pltpu.roll(dc, n_ext - 1, 0)) + w[0:1] * pltpu.roll(dc, n_ext - 2, 0)
            da_ref[comp] = da[cur].astype(BF16)
            dcc = dc[cur]
            dcw_ref[comp, 0:1, :] += jnp.sum(dcc * pltpu.roll(a_ext, 2, 0)[cur], axis=0, keepdims=True)
            dcw_ref[comp, 1:2, :] += jnp.sum(dcc * pltpu.roll(a_ext, 1, 0)[cur], axis=0, keepdims=True)
            dcw_ref[comp, 2:3, :] += jnp.sum(dcc * a_ext[cur], axis=0, keepdims=True)
            dcb_ref[comp] += jnp.sum(dcc, axis=0, keepdims=True)

        for p in range(2):
            ag, av = ext_of(p), ext_of(2 + p)
            cg = _conv3(ag, cw_ref[p], cb_ref[p])
            cv = _conv3(av, cw_ref[2 + p], cb_ref[2 + p])
            d = jnp.concatenate(
                [jnp.zeros((SUBLANE, cb_w), F32), d_ref[p], jnp.where(last, 0.0, dnext_ref[p])], axis=0)
            sg = _sigmoid(cg)
            back(2 + p, av, d * (cg * sg))
            back(p, ag, d * cv * (sg * (1.0 + cg * (1.0 - sg))))

    return _hosted_call(
        body, [a, a, a, dhm, dhm, cw, cb], name=name, grid=(ns // cb_w, n_i),
        in_specs=[
            pl.BlockSpec((4, bs, cb_w), lambda j, i: (0, i, j)),
            pl.BlockSpec((4, SUBLANE, cb_w), lambda j, i: (0, jnp.maximum(i * hb - 1, 0), j)),
            pl.BlockSpec((4, SUBLANE, cb_w), lambda j, i: (0, jnp.minimum((i + 1) * hb, n_i * hb - 1), j)),
            pl.BlockSpec((2, bs, cb_w), lambda j, i: (0, i, j)),
            pl.BlockSpec((2, SUBLANE, cb_w), lambda j, i: (0, jnp.minimum((i + 1) * hb, n_i * hb - 1), j)),
            pl.BlockSpec((4, 3, cb_w), lambda j, i: (0, 0, j)),
            pl.BlockSpec((4, 1, cb_w), lambda j, i: (0, 0, j)),
        ],
        out_specs=[
            pl.BlockSpec((4, bs, cb_w), lambda j, i: (0, i, j)),
            pl.BlockSpec((4, 3, cb_w), lambda j, i: (0, 0, j)),
            pl.BlockSpec((4, 1, cb_w), lambda j, i: (0, 0, j)),
        ],
        out_shape=[
            jax.ShapeDtypeStruct((4, s, ns), BF16),
            jax.ShapeDtypeStruct((4, 3, ns), F32),
            jax.ShapeDtypeStruct((4, 1, ns), F32),
        ],
        semantics=("parallel", "arbitrary"), rides=rides)


ATT_BQ_FWD = 2048
ATT_BQ_BWD = 1024
ATT_BK = 256
ATT_UNROLL = 2
ATT_UNROLL_BWD = 4


def _att_blocks(s, bq_pref):
    bq = _pick(s, bq_pref)
    bk = min(ATT_BK, bq)
    assert bq % bk == 0
    return bq, bk


def _dot_sel2(x, sel):
    hi = x.astype(BF16)
    lo = (x - hi.astype(F32)).astype(BF16)
    n = x.shape[0]
    both = jnp.dot(jnp.concatenate([hi, lo], axis=0), sel, preferred_element_type=F32)
    return both[:n] + both[n:]


def _causal_mask(bq, bk, row0, col0):
    rows = row0 + lax.broadcasted_iota(jnp.int32, (bq, bk), 0)
    cols = col0 + lax.broadcasted_iota(jnp.int32, (bq, bk), 1)
    return cols < rows


def _sb_tile(qb, kb, scale, mask):
    z = lax.dot_general(qb, kb, _DIMS["nt"], preferred_element_type=F32) * scale
    e = jnp.exp(-jnp.abs(z))
    lb = jnp.minimum(z, 0.0) - jnp.log(1.0 + e)
    l1m = lb - z
    if mask is not None:
        l1m = jnp.where(mask, l1m, 0.0)
    return z, e, lb, l1m


def _attn_fwd(q, k, v, name, rides=()):
    s, hd = q.shape
    bq, bk = _att_blocks(s, ATT_BQ_FWD)
    r = bq // bk
    unroll = math.gcd(r, ATT_UNROLL)
    n_h, n_q = hd // TILE, s // bq
    scale = 1.0 / math.sqrt(TILE)

    def body(q_ref, k_ref, v_ref, o_ref, l_ref, acc_ref, suf_ref):
        i = pl.program_id(1)
        qb = q_ref[...]
        later = _tri(bk, "gt")
        acc_ref[...] = jnp.zeros_like(acc_ref)
        suf_ref[...] = jnp.zeros_like(suf_ref)

        def tile(j, row0):
            rows = pl.ds(pl.multiple_of(j * bk, bk), bk)
            masked = row0 is not None
            r0 = row0 if masked else 0
            rs = pl.ds(r0, bq - r0)
            mask = _causal_mask(bq - r0, bk, i * bq + r0, j * bk) if masked else None
            _, _, lb, l1m = _sb_tile(qb[r0:], k_ref[rows, :], scale, mask)
            a = jnp.exp(lb + _dot_sel2(l1m, later) + suf_ref[rs, :])
            if masked:
                a = jnp.where(mask, a, 0.0)
            acc_ref[rs, :] += jnp.dot(a.astype(BF16), v_ref[rows, :], preferred_element_type=F32)
            suf_ref[rs, :] += jnp.sum(l1m, axis=1, keepdims=True)

        for dgl in range(r - 1, -1, -1):
            tile(r * i + dgl, dgl * bk)

        def step(t, carry):
            for u in range(unroll):
                tile(r * i - 1 - (unroll * t + u), None)
            return carry

        lax.fori_loop(0, (r * i) // unroll, step, 0)
        o_ref[...] = acc_ref[...].astype(BF16)
        l_ref[...] = jnp.broadcast_to(suf_ref[...], (bq, TILE))

    blk = pl.BlockSpec((bq, TILE), lambda h, i: (i, h))
    head = pl.BlockSpec((s, TILE), lambda h, i: (0, h))
    return _hosted_call(
        body, [q, k, v], name=name, grid=(n_h, n_q), in_specs=[blk, head, head], out_specs=[blk, blk],
        out_shape=[jax.ShapeDtypeStruct((s, hd), BF16), jax.ShapeDtypeStruct((s, hd), F32)],
        scratch_shapes=[pltpu.VMEM((bq, TILE), F32), pltpu.VMEM((bq, 1), F32)],
        semantics=("parallel", "parallel"), rides=rides)


def _attn_bwd(q, k, v, do, lsum, name, rides=()):
    s, hd = q.shape
    bq, bk = _att_blocks(s, ATT_BQ_BWD)
    r = bq // bk
    unroll = math.gcd(r, ATT_UNROLL_BWD)
    n_h, n_q = hd // TILE, s // bq
    scale = 1.0 / math.sqrt(TILE)

    def body(q_ref, k_ref, v_ref, do_ref, l_ref, dq_ref, dk_ref, dv_ref, dq_acc, pre_ref, cp_ref):
        i = pl.program_id(1)

        @pl.when(i == 0)
        def _():
            dk_ref[...] = jnp.zeros_like(dk_ref)
            dv_ref[...] = jnp.zeros_like(dv_ref)

        qb = q_ref[...]
        dob = do_ref[...]
        upto = _tri(bk, "le")
        before = _tri(bk, "lt")
        dq_acc[...] = jnp.zeros_like(dq_acc)
        pre_ref[...] = jnp.zeros_like(pre_ref)
        cp_ref[...] = jnp.zeros_like(cp_ref)

        def tile(j, row0):
            rows = pl.ds(pl.multiple_of(j * bk, bk), bk)
            kb, vb = k_ref[rows, :], v_ref[rows, :]
            masked = row0 is not None
            r0 = row0 if masked else 0
            rs = pl.ds(r0, bq - r0)
            qs, dos = qb[r0:], dob[r0:]
            mask = _causal_mask(bq - r0, bk, i * bq + r0, j * bk) if masked else None
            z, e, lb, l1m = _sb_tile(qs, kb, scale, mask)
            suffix = (l_ref[rs, 0:1] - pre_ref[rs, :]) - _dot_sel2(l1m, upto)
            a = jnp.exp(lb + suffix)
            if masked:
                a = jnp.where(mask, a, 0.0)
            p = a * lax.dot_general(dos, vb, _DIMS["nt"], preferred_element_type=F32)
            both = p + (cp_ref[rs, :] + jnp.dot(p.astype(BF16), before, preferred_element_type=F32))
            sg = jnp.where(z >= 0.0, 1.0, e) * pl.reciprocal(1.0 + e, approx=True)
            dz = p - both * sg
            if masked:
                dz = jnp.where(mask, dz, 0.0)
            dz = (dz * scale).astype(BF16)
            dq_acc[rs, :] += jnp.dot(dz, kb, preferred_element_type=F32)
            dk_ref[rows, :] += lax.dot_general(dz, qs, _DIMS["tn"], preferred_element_type=F32)
            dv_ref[rows, :] += lax.dot_general(a.astype(BF16), dos, _DIMS["tn"], preferred_element_type=F32)
            pre_ref[rs, :] += jnp.sum(l1m, axis=1, keepdims=True)
            cp_ref[rs, :] += jnp.sum(p, axis=1, keepdims=True)

        def step(j, carry):
            for u in range(unroll):
                tile(unroll * j + u, None)
            return carry

        lax.fori_loop(0, (r * i) // unroll, step, 0)
        for dgl in range(r):
            tile(r * i + dgl, dgl * bk)
        dq_ref[...] = dq_acc[...].astype(BF16)

    blk = pl.BlockSpec((bq, TILE), lambda h, i: (i, h))
    head = pl.BlockSpec((s, TILE), lambda h, i: (0, h))
    return _hosted_call(
        body, [q, k, v, do, lsum], name=name, grid=(n_h, n_q), in_specs=[blk, head, head, blk, blk],
        out_specs=[blk, head, head],
        out_shape=[jax.ShapeDtypeStruct((s, hd), BF16), jax.ShapeDtypeStruct((s, hd), F32),
                   jax.ShapeDtypeStruct((s, hd), F32)],
        scratch_shapes=[pltpu.VMEM((bq, TILE), F32), pltpu.VMEM((bq, 1), F32), pltpu.VMEM((bq, 1), F32)],
        semantics=("parallel", "arbitrary"), rides=rides)


EW_BLOCK = 512 * 1024


def _ew_blocks(r, c, elems=EW_BLOCK):
    return _pick(r, max(ROWS, elems // c // ROWS * ROWS), ROWS), c


def _cast_bf16(w, layer, chip_idx, name):
    _, r, c = w.shape
    br, bc = _ew_blocks(r, c)

    def body(chip_ref, w_ref, o_ref):
        o_ref[...] = w_ref[...].astype(BF16)

    return pl.pallas_call(
        body, name=name,
        grid_spec=pltpu.PrefetchScalarGridSpec(
            num_scalar_prefetch=1, grid=(r // br, c // bc),
            in_specs=[pl.BlockSpec((None, br, bc), lambda i, j, chip_ref: (layer, i, j))],
            out_specs=pl.BlockSpec((None, br, bc), lambda i, j, chip_ref: (chip_ref[0], i, j)),
        ),
        out_shape=jax.ShapeDtypeStruct((N_CHIPS, r, c), BF16), compiler_params=_cp("parallel", "parallel"),
    )(chip_idx, w)


def _pair_add(dw, recv, c_idx, name):
    _, r, c = dw.shape
    hr = r // 2
    br, bc = _ew_blocks(hr, c)
    nb = hr // br

    def body(c_ref, a_ref, b_ref, o_ref):
        o_ref[...] = (a_ref[...].astype(F32) + b_ref[...].astype(F32)).astype(BF16)

    return pl.pallas_call(
        body, name=name,
        grid_spec=pltpu.PrefetchScalarGridSpec(
            num_scalar_prefetch=1, grid=(N_CHIPS, nb, c // bc),
            in_specs=[
                pl.BlockSpec((None, br, bc), lambda s, i, j, c_ref: (s, c_ref[0] * nb + i, j)),
                pl.BlockSpec((None, br, bc), lambda s, i, j, c_ref: (s, i, j)),
            ],
            out_specs=pl.BlockSpec((None, br, bc), lambda s, i, j, c_ref: (s, i, j)),
        ),
        out_shape=jax.ShapeDtypeStruct((N_CHIPS, hr, c), BF16),
        compiler_params=_cp("parallel", "parallel", "parallel"),
    )(c_idx, dw, recv)


def _chip_sum(parts, dest, shape, layer, c_idx, name):
    _, hr, c = parts.shape
    br, bc = _ew_blocks(hr, c, EW_BLOCK // 2)
    nb = hr // br

    def body(c_ref, p_ref, *refs):
        o_ref = refs[-1]
        acc = p_ref[0].astype(F32)
        for s in range(1, N_CHIPS):
            acc = acc + p_ref[s].astype(F32)
        o_ref[...] = acc

    in_specs = [pl.BlockSpec((N_CHIPS, br, bc), lambda i, j, c_ref: (0, i, j))]
    operands = [c_idx, parts]
    aliases = {}
    if dest is not None:
        in_specs.append(ANY)
        operands.append(dest)
        aliases = {2: 0}
    return pl.pallas_call(
        body, name=name,
        grid_spec=pltpu.PrefetchScalarGridSpec(
            num_scalar_prefetch=1, grid=(nb, c // bc), in_specs=in_specs,
            out_specs=pl.BlockSpec((None, br, bc), lambda i, j, c_ref: (layer, c_ref[0] * nb + i, j)),
        ),
        out_shape=jax.ShapeDtypeStruct(shape, F32), input_output_aliases=aliases,
        compiler_params=_cp("parallel", "parallel"),
    )(*operands)


def _adamw(w, g, m, v, name, pass_g=False):
    n_l, r, c = w.shape
    br, bc = _ew_blocks(r, c, EW_BLOCK // 2)

    def body(w_ref, g_ref, m_ref, v_ref, *out_refs):
        d_ref, mo_ref, vo_ref = out_refs[-3:]
        g = g_ref[...]
        if pass_g:
            out_refs[0][...] = g
        m = ADAM_B1 * m_ref[...] + (1.0 - ADAM_B1) * g
        v = ADAM_B2 * v_ref[...] + (1.0 - ADAM_B2) * (g * g)
        m_hat = m / (1.0 - ADAM_B1 ** ADAM_STEP)
        v_hat = v / (1.0 - ADAM_B2 ** ADAM_STEP)
        d_ref[...] = -ADAM_LR * (m_hat / (jnp.sqrt(v_hat) + ADAM_EPS) + ADAM_WD * w_ref[...])
        mo_ref[...] = m
        vo_ref[...] = v

    blk = pl.BlockSpec((None, br, bc), lambda l, i, j: (l, i, j))
    n_out = 4 if pass_g else 3
    return pl.pallas_call(
        body, name=name, grid=(n_l, r // br, c // bc), in_specs=[blk] * 4, out_specs=[blk] * n_out,
        out_shape=[jax.ShapeDtypeStruct(w.shape, F32)] * n_out,
        compiler_params=_cp("parallel", "parallel", "parallel"),
    )(w, g, m, v)


def _place():
    x, y, c = lax.axis_index("x"), lax.axis_index("y"), lax.axis_index("c")
    chips = [(1 - x, y), (x, 1 - y), (1 - x, 1 - y)]
    return x, y, c, chips


class _Ride:
    def __init__(self, reads, bufs, new, n_sems, start, finish):
        self.reads, self.bufs, self.new, self.n_sems, self.start, self.finish = reads, bufs, new, n_sems, start, finish


def _hosted_call(body, operands, *, name, grid, in_specs, out_specs, out_shape, scratch_shapes=(), semantics=(), rides=()):
    single = not isinstance(out_shape, (list, tuple))
    out_specs = [out_specs] if single else list(out_specs)
    out_shape = [out_shape] if single else list(out_shape)
    in_specs, scratch_shapes = list(in_specs), list(scratch_shapes)
    n_in, n_out, n_scr = len(in_specs), len(out_shape), len(scratch_shapes)
    extra_in, extra_out, aliases, where = [], [], {}, []
    for ride in rides:
        r0 = len(extra_in)
        extra_in += list(ride.reads)
        b0 = len(extra_in)
        extra_in += list(ride.bufs)
        ob0 = len(extra_out)
        extra_out += [jax.ShapeDtypeStruct(b.shape, b.dtype) for b in ride.bufs]
        for t in range(len(ride.bufs)):
            aliases[n_in + b0 + t] = n_out + ob0 + t
        on0 = len(extra_out)
        extra_out += list(ride.new)
        where.append((r0, len(ride.reads), ob0, len(ride.bufs), on0, len(ride.new)))
    n_ein, n_eout = len(extra_in), len(extra_out)
    sem_shapes = [pltpu.SemaphoreType.DMA((max(1, k),)) for ride in rides for k in ride.n_sems]

    def full_body(*refs):
        ins, outs, scr = refs[:n_in + n_ein], refs[n_in + n_ein:n_in + n_ein + n_out + n_eout], refs[n_in + n_ein + n_out + n_eout:]

        def run(which):
            for idx, (ride, (r0, nr, ob0, nb, on0, nn)) in enumerate(zip(rides, where)):
                fn = ride.start if which == 0 else ride.finish
                fn(ins[n_in + r0:n_in + r0 + nr], outs[n_out + ob0:n_out + ob0 + nb], outs[n_out + on0:n_out + on0 + nn],
                   *scr[n_scr + 3 * idx:n_scr + 3 * idx + 3])

        host = lambda: body(*ins[:n_in], *outs[:n_out], *scr[:n_scr])
        if not rides:
            host()
        elif not grid:
            run(0)
            host()
            run(1)
        else:
            ids = [pl.program_id(ax) for ax in range(len(grid))]
            first = functools.reduce(jnp.logical_and, [i == 0 for i in ids])
            last = functools.reduce(jnp.logical_and, [i == g - 1 for i, g in zip(ids, grid)])
            pl.when(first)(lambda: run(0))
            host()
            pl.when(last)(lambda: run(1))

    if rides:
        params = pltpu.CompilerParams(dimension_semantics=("arbitrary",) * len(grid), vmem_limit_bytes=VMEM_LIMIT)
    else:
        params = _cp(*semantics)
    outs = pl.pallas_call(
        full_body, name=name, grid=grid,
        in_specs=in_specs + [ANY] * n_ein, out_specs=out_specs + [ANY] * n_eout,
        out_shape=out_shape + extra_out, input_output_aliases=aliases,
        scratch_shapes=scratch_shapes + sem_shapes, compiler_params=params,
    )(*operands, *extra_in)
    main = outs[0] if single else list(outs[:n_out])
    rode = [(list(outs[n_out + ob0:n_out + ob0 + nb]), list(outs[n_out + on0:n_out + on0 + nn]))
            for (_, _, ob0, nb, on0, nn) in where]
    return main, rode


def _run_rides(rides, name):
    return _hosted_call(lambda: None, [], name=name, grid=(), in_specs=[], out_specs=[], out_shape=[], rides=rides)[1]


def _ride_gather(slots, part=0, n_parts=1, span=1, stage=None):
    n = len(slots)
    halves = [a.shape[1] // 2 for a in slots]
    sizes = [hr // n_parts for hr in halves]
    assert part + span <= n_parts
    for a, hr, size in zip(slots, halves, sizes):
        assert a.shape[1] == 2 * hr and hr == size * n_parts and size % ROWS == 0, a.shape

    def remote(bufs, send_sems, recv_sems, i, k, slot, core, to):
        rows = bufs[i].at[slot, pl.ds(pl.multiple_of(core * halves[i] + part * sizes[i], ROWS), span * sizes[i])]
        return pltpu.make_async_remote_copy(
            src_ref=rows, dst_ref=rows, send_sem=send_sems.at[i * 6 + k], recv_sem=recv_sems.at[i * 6 + k],
            device_id=to, device_id_type=MESH)

    def each(fn):
        x, y, c, chips = _place()
        for i in range(n):
            for k, (px, py) in enumerate(chips):
                fn(x, y, c, i, k, px, py)

    def start(reads, bufs, new, send_sems, recv_sems, local_sems):
        cp = functools.partial(remote, bufs, send_sems, recv_sems)
        if stage != "d2d":
            each(lambda x, y, c, i, k, px, py: cp(i, k, 2 * x + y, c, (px, py, c)).start())
        else:
            each(lambda x, y, c, i, k, px, py: cp(i, 3 + k, 2 * px + py, c, (x, y, 1 - c)).start())

    def finish(reads, bufs, new, send_sems, recv_sems, local_sems):
        cp = functools.partial(remote, bufs, send_sems, recv_sems)

        def landed_over_ici(x, y, c, i, k, px, py):
            cp(i, k, 2 * px + py, c, (x, y, c)).wait_recv()
            if stage is None:
                cp(i, 3 + k, 2 * px + py, c, (x, y, 1 - c)).start()

        if stage != "d2d":
            each(landed_over_ici)
        if stage != "ici":
            each(lambda x, y, c, i, k, px, py: cp(i, 3 + k, 2 * px + py, 1 - c, (x, y, c)).wait_recv())
        if stage != "d2d":
            each(lambda x, y, c, i, k, px, py: cp(i, k, 2 * x + y, c, (px, py, c)).wait_send())
        if stage != "ici":
            each(lambda x, y, c, i, k, px, py: cp(i, 3 + k, 2 * px + py, c, (x, y, 1 - c)).wait_send())

    return _Ride([], slots, [], (6 * n, 6 * n, 0), start, finish)


def _ride_swap(grads):
    n = len(grads)
    halves = [a.shape[1] // 2 for a in grads]

    def copies(reads, new, send_sems, recv_sems):
        x, y, c, _ = _place()
        out = []
        for i in range(n):
            rows = pl.ds(pl.multiple_of((1 - c) * halves[i], 2 * SUBLANE), halves[i])
            out.append(pltpu.make_async_remote_copy(
                src_ref=reads[i].at[:, rows, :], dst_ref=new[i], send_sem=send_sems.at[i], recv_sem=recv_sems.at[i],
                device_id=(x, y, 1 - c), device_id_type=MESH))
        return out

    def start(reads, bufs, new, send_sems, recv_sems, local_sems):
        for cp in copies(reads, new, send_sems, recv_sems):
            cp.start()

    def finish(reads, bufs, new, send_sems, recv_sems, local_sems):
        for cp in copies(reads, new, send_sems, recv_sems):
            cp.wait()

    shapes = [jax.ShapeDtypeStruct((N_CHIPS, hr, a.shape[2]), a.dtype) for a, hr in zip(grads, halves)]
    return _Ride(grads, [], shapes, (n, n, 0), start, finish)


def _ride_scatter(parts, part=0, n_parts=1, into=None, span=1):
    n = len(parts)
    sizes = [a.shape[1] // n_parts for a in parts]
    assert part + span <= n_parts
    for a, size in zip(parts, sizes):
        assert a.shape[1] == size * n_parts and size % ROWS == 0, a.shape

    def piece(ref, i, slot):
        return ref.at[slot, pl.ds(part * sizes[i], span * sizes[i])]

    def own(reads, land, local_sems, i):
        me = 2 * lax.axis_index("x") + lax.axis_index("y")
        return pltpu.make_async_copy(piece(reads[i], i, me), piece(land[i], i, me), local_sems.at[i])

    def send(reads, land, send_sems, recv_sems, i, k):
        x, y, c, chips = _place()
        px, py = chips[k]
        return pltpu.make_async_remote_copy(
            src_ref=piece(reads[i], i, 2 * px + py), dst_ref=piece(land[i], i, 2 * x + y),
            send_sem=send_sems.at[3 * i + k], recv_sem=recv_sems.at[3 * i + k],
            device_id=(px, py, c), device_id_type=MESH)

    def start(reads, bufs, new, send_sems, recv_sems, local_sems):
        land = new if into is None else bufs
        for i in range(n):
            own(reads, land, local_sems, i).start()
            for k in range(3):
                send(reads, land, send_sems, recv_sems, i, k).start()

    def finish(reads, bufs, new, send_sems, recv_sems, local_sems):
        land = new if into is None else bufs
        x, y, c, chips = _place()
        for i in range(n):
            for k, (px, py) in enumerate(chips):
                slot = piece(land[i], i, 2 * px + py)
                pltpu.make_async_remote_copy(
                    src_ref=slot, dst_ref=slot, send_sem=send_sems.at[3 * i + k], recv_sem=recv_sems.at[3 * i + k],
                    device_id=(x, y, c), device_id_type=MESH).wait_recv()
        for i in range(n):
            for k in range(3):
                send(reads, land, send_sems, recv_sems, i, k).wait_send()
            own(reads, land, local_sems, i).wait()

    shapes = [jax.ShapeDtypeStruct(a.shape, a.dtype) for a in parts]
    if into is None:
        return _Ride(parts, [], shapes, (3 * n, 3 * n, n), start, finish)
    return _Ride(parts, list(into), [], (3 * n, 3 * n, n), start, finish)


def _ride_join(grads):
    n = len(grads)

    def copy(bufs, send_sems, recv_sems, i, core, to):
        hr = grads[i].shape[1] // 2
        rows = bufs[i].at[:, pl.ds(pl.multiple_of(core * hr, SUBLANE), hr), :]
        return pltpu.make_async_remote_copy(
            src_ref=rows, dst_ref=rows, send_sem=send_sems.at[i], recv_sem=recv_sems.at[i],
            device_id=to, device_id_type=MESH)

    def start(reads, bufs, new, send_sems, recv_sems, local_sems):
        x, y, c, _ = _place()
        for i in range(n):
            copy(bufs, send_sems, recv_sems, i, c, (x, y, 1 - c)).start()

    def finish(reads, bufs, new, send_sems, recv_sems, local_sems):
        x, y, c, _ = _place()
        for i in range(n):
            copy(bufs, send_sems, recv_sems, i, 1 - c, (x, y, c)).wait_recv()
        for i in range(n):
            copy(bufs, send_sems, recv_sems, i, c, (x, y, 1 - c)).wait_send()

    return _Ride([], grads, [], (n, n, 0), start, finish)


def _all_reduce_small(packed, name, rides=()):
    r, c = packed.shape
    chunk = _pick(r, 256, ROWS)

    def body(x_ref, out_ref, gath, send_sems, recv_sems, local_sem):
        x, y, cc, chips = _place()
        me, sibling = (x, y, cc), (x, y, 1 - cc)

        def slot(px, py, pc):
            return gath.at[4 * px + 2 * py + pc]

        def copy(k, block, to, src=None):
            return pltpu.make_async_remote_copy(
                src_ref=slot(*block) if src is None else src, dst_ref=slot(*block),
                send_sem=send_sems.at[k], recv_sem=recv_sems.at[k], device_id=to, device_id_type=MESH)

        mine = pltpu.make_async_copy(x_ref, slot(*me), local_sem)
        mine.start()
        first = [copy(0, me, sibling, src=x_ref)]
        first += [copy(1 + j, me, (*chip, cc), src=x_ref) for j, chip in enumerate(chips)]
        for cp in first:
            cp.start()
        passed = [copy(4 + j, (*chip, cc), sibling) for j, chip in enumerate(chips)]
        for j, chip in enumerate(chips):
            copy(1 + j, (*chip, cc), me).wait_recv()
            passed[j].start()
        copy(0, sibling, me).wait_recv()
        for j, chip in enumerate(chips):
            copy(4 + j, (*chip, 1 - cc), me).wait_recv()
        for cp in first + passed:
            cp.wait_send()
        mine.wait()

        def add(i, carry):
            rows = pl.ds(pl.multiple_of(i * chunk, SUBLANE), chunk)
            acc = gath[0, rows, :]
            for dev in range(1, N_DEV):
                acc = acc + gath[dev, rows, :]
            out_ref[rows, :] = acc
            return carry

        lax.fori_loop(0, r // chunk, add, 0)

    return _hosted_call(
        body, [packed], name=name, grid=(), in_specs=[VMEM_SPEC], out_specs=VMEM_SPEC,
        out_shape=jax.ShapeDtypeStruct((r, c), F32),
        scratch_shapes=[pltpu.VMEM((N_DEV, r, c), F32), pltpu.SemaphoreType.DMA((7,)),
                        pltpu.SemaphoreType.DMA((7,)), pltpu.SemaphoreType.DMA],
        rides=rides)


_PACK_ROWS = 256


def _pack(arrays):
    flat = jnp.concatenate([a.reshape(-1).astype(F32) for a in arrays])
    unit = _PACK_ROWS * LANE
    total = -(-flat.shape[0] // unit) * unit
    return jnp.pad(flat, (0, total - flat.shape[0])).reshape(-1, LANE)


def _unpack(packed, shapes, lead=()):
    flat = packed.reshape(lead + (-1,))
    out, at = [], 0
    for s in shapes:
        size = math.prod(s)
        out.append(flat[..., at:at + size].reshape(lead + tuple(s)))
        at += size
    return out


def kernel(x, pre_mix_g, post_mix_g, pre_ffn_g, post_ffn_g, a_w_in, a_v_norm_g, a_w_spatial, a_b_spatial, a_w_out, kv_norm_g, w_k, w_v, b_w_q, b_w_o, ffn_w_up, ffn_conv_w, ffn_conv_b, ffn_w_down, loss_target, m_pre_mix_g, m_post_mix_g, m_pre_ffn_g, m_post_ffn_g, m_a_w_in, m_a_v_norm_g, m_a_w_spatial, m_a_b_spatial, m_a_w_out, m_kv_norm_g, m_w_k, m_w_v, m_b_w_q, m_b_w_o, m_ffn_w_up, m_ffn_conv_w, m_ffn_conv_b, m_ffn_w_down, v_pre_mix_g, v_post_mix_g, v_pre_ffn_g, v_post_ffn_g, v_a_w_in, v_a_v_norm_g, v_a_w_spatial, v_a_b_spatial, v_a_w_out, v_kv_norm_g, v_w_k, v_w_v, v_b_w_q, v_b_w_o, v_ffn_w_up, v_ffn_conv_w, v_ffn_conv_b, v_ffn_w_down):
    xi, yi, ci = lax.axis_index("x"), lax.axis_index("y"), lax.axis_index("c")
    chip = 2 * xi + yi
    c_idx = jnp.reshape(ci, (1,)).astype(jnp.int32)
    _, s, d = x.shape
    n_layers = pre_mix_g.shape[0]
    assert n_layers == 2 and a_w_in.shape[0] == 1 and b_w_q.shape[0] == 1
    d_a = a_w_out.shape[1] * N_CHIPS
    n_g = a_w_spatial.shape[1]
    ns = ffn_w_up.shape[2]
    assert a_w_spatial.shape[2] == TILE and d_a == n_g * TILE and s % TILE == 0
    h0 = x[0]
    target = loss_target[0]

    big = {
        "win": (a_w_in, m_a_w_in, v_a_w_in),
        "wout": (a_w_out, m_a_w_out, v_a_w_out),
        "wk": (w_k[None], m_w_k[None], v_w_k[None]),
        "wv": (w_v[None], m_w_v[None], v_w_v[None]),
        "wq": (b_w_q, m_b_w_q, v_b_w_q),
        "wo": (b_w_o, m_b_w_o, v_b_w_o),
        "wup": (ffn_w_up, m_ffn_w_up, v_ffn_w_up),
        "wdn": (ffn_w_down, m_ffn_w_down, v_ffn_w_down),
    }
    units = [(nm, layer) for nm in big for layer in range(big[nm][0].shape[0])]
    chip_idx = jnp.reshape(chip, (1,)).astype(jnp.int32)
    shards = [_cast_bf16(big[nm][0], layer, chip_idx, f"cast_{nm}{layer}") for nm, layer in units]
    small_sharded = _pack([a_v_norm_g, ffn_conv_w])
    small_sharded = lax.dynamic_update_index_in_dim(
        jnp.zeros((N_CHIPS,) + small_sharded.shape, F32), small_sharded, chip, 0)
    own = dict(zip(units, shards))
    full = {}

    def gather_ride(keys):
        return _ride_gather([own[key] for key in keys])

    def gathered(keys, rode):
        full.update(zip(keys, rode[0]))

    first_keys = [("win", 0)]
    (first_bufs, _), = _run_rides([_ride_gather([own[key] for key in first_keys] + [small_sharded])], "gather_first")
    full.update(zip(first_keys, first_bufs[:-1]))
    vg_parts, cw_parts = _unpack(first_bufs[-1], [a_v_norm_g.shape, ffn_conv_w.shape], lead=(N_CHIPS,))
    v_g = jnp.transpose(vg_parts, (1, 0, 2)).reshape(1, d_a)

    def rows(nm, layer=0):
        w = full[(nm, layer)]
        return w.reshape(w.shape[0] * w.shape[1], w.shape[2])

    gains = lambda g, layer: g[layer:layer + 1]
    bias = jnp.repeat(a_b_spatial[0].T, TILE, axis=1)
    w_s = a_w_spatial[0]
    kv_g = kv_norm_g[None]
    conv_w = [cw_parts[:, layer] for layer in range(n_layers)]
    conv_b = [ffn_conv_b[layer].reshape(N_CHIPS, 1, ns) for layer in range(n_layers)]

    up0 = own[("wup", 0)]
    pieces = lambda p, span: _ride_gather([up0], part=p, n_parts=8, span=span)
    hn0 = _rms_fwd(h0, gains(pre_mix_g, 0), "norm_in")
    uv, ((out_bufs, _), ((up0,), _)) = _mm(
        hn0, full[("win", 0)], "nn", "gmlp_in", out_split=N_CHIPS, rides=[gather_ride([("wout", 0)]), pieces(0, 1)])
    full[("wout", 0)] = out_bufs[0]
    gm, (((up0,), _),) = _gmlp_fwd(uv, v_g, w_s, bias, "gmlp_gate", rides=[pieces(1, 2)])
    mix0, (((up0,), _),) = _mm(gm, rows("wout"), "nn", "gmlp_out", rides=[pieces(3, 2)])
    mix0 = mix0[0]
    (h1, hn1), (((up0,), _),) = _resid_rms(
        h0, mix0, gains(post_mix_g, 0), [gains(pre_ffn_g, 0)], "resid_mix0", rides=[pieces(5, 3)])
    full[("wup", 0)] = up0
    def leg(keys, stage):
        return _ride_gather([own[key] for key in keys], stage=stage)

    def first_leg_done(keys, rode):
        own.update(zip(keys, rode[0]))

    down0, qk, vo = [("wdn", 0)], [("wq", 0), ("wk", 0)], [("wv", 0), ("wo", 0)]
    a0, (rode,) = _mm(hn1, full[("wup", 0)], "nn", "ffn_up0", out_split=N_CHIPS, rides=[leg(down0, "ici")])
    first_leg_done(down0, rode)
    hm0, (rode, rode_qk) = _ffn_act_fwd(
        a0, conv_w[0], conv_b[0], "ffn_act0", rides=[leg(down0, "d2d"), leg(qk, "ici")])
    gathered(down0, rode)
    first_leg_done(qk, rode_qk)
    f0, (rode, rode_vo) = _mm(hm0, rows("wdn", 0), "nn", "ffn_down0", rides=[leg(qk, "d2d"), leg(vo, "ici")])
    gathered(qk, rode)
    first_leg_done(vo, rode_vo)
    f0 = f0[0]
    (h2, hn2, kvn), (rode,) = _resid_rms(
        h1, f0, gains(post_ffn_g, 0), [gains(pre_mix_g, 1), kv_g], "resid_ffn0", rides=[leg(vo, "d2d")])
    gathered(vo, rode)
    q = _mm(hn2, rows("wq"), "nn", "proj_q", out_dtype=BF16)[0]
    k = _mm(kvn, rows("wk"), "nn", "proj_k", out_dtype=BF16)[0]
    v = _mm(kvn, rows("wv"), "nn", "proj_v", out_dtype=BF16)[0]
    last_keys = [("wup", 1), ("wdn", 1)]
    (att, lsum), (rode,) = _attn_fwd(q, k, v, "attn_fwd", rides=[leg(last_keys, "ici")])
    first_leg_done(last_keys, rode)
    mix1, (rode,) = _mm(att, rows("wo"), "nn", "proj_o", rides=[leg(last_keys, "d2d")])
    gathered(last_keys, rode)
    mix1 = mix1[0]
    h3, hn3 = _resid_rms(h2, mix1, gains(post_mix_g, 1), [gains(pre_ffn_g, 1)], "resid_mix1")
    a1 = _mm(hn3, full[("wup", 1)], "nn", "ffn_up1", out_split=N_CHIPS)
    hm1 = _ffn_act_fwd(a1, conv_w[1], conv_b[1], "ffn_act1")
    f1 = _mm(hm1, rows("wdn", 1), "nn", "ffn_down1")[0]
    dh4, loss_tile = _loss_head(h3, f1, gains(post_ffn_g, 1), target, "loss_head")
    loss = lax.psum(loss_tile[0, 0], ("x", "y", "c"))

    dw = {}
    dg = {}

    pair = {}
    half_done = {nm: None for nm in big}

    def swap_ride(keys):
        return _ride_swap([dw[key] for key in keys])

    def swapped(keys, rode):
        for (nm, layer), got in zip(keys, rode[1]):
            pair[(nm, layer)] = _pair_add(dw[(nm, layer)], got, c_idx, f"pair_add_{nm}{layer}")

    def scatter_ride(keys):
        return _ride_scatter([pair[key] for key in keys])

    def scattered(keys, rode):
        for (nm, layer), got in zip(keys, rode[1]):
            half_done[nm] = _chip_sum(got, half_done[nm], big[nm][0].shape, layer, c_idx, f"chip_sum_{nm}{layer}")

    def ffn_bwd(dh_out, h_in, hn, a, hm, f, layer, act_rides=()):
        df, dg[("post_ffn", layer)] = _rms_bwd_out(dh_out, f, gains(post_ffn_g, layer), f"d_norm_ffn_out{layer}")
        dwd = _mm(hm, df, "tn", f"d_w_down{layer}", out_dtype=BF16)[0]
        down, up = [("wdn", layer)], [("wup", layer)]
        dw[down[0]] = dwd.reshape(N_CHIPS, dwd.shape[0] // N_CHIPS, d)
        dhm, (rode,) = _mm(df, rows("wdn", layer), "nt", f"d_ffn_mid{layer}", out_split=2, rides=[swap_ride(down)])
        swapped(down, rode)
        (da, dg[("conv_w", layer)], dg[("conv_b", layer)]), act_rode = _ffn_act_bwd(
            a, dhm, conv_w[layer], conv_b[layer], f"d_ffn_act{layer}", rides=act_rides)
        dw[up[0]], (rode,) = _mm(hn, da, "tn", f"d_w_up{layer}", out_dtype=BF16, out_split=N_CHIPS,
                                 rides=[scatter_ride(down)])
        scattered(down, rode)
        dhn, (rode,) = _mm(da, full[("wup", layer)], "nt", f"d_ffn_in{layer}", rides=[swap_ride(up)])
        swapped(up, rode)
        return dhn[0], act_rode

    dhn3, _ = ffn_bwd(dh4, h3, hn3, a1, hm1, f1, 1)
    dh3, (dg[("pre_ffn", 1)],) = _rms_bwd_in(dh4, h3, [([dhn3], gains(pre_ffn_g, 1))], "d_norm_ffn_in1")
    dmix1, dg[("post_mix", 1)] = _rms_bwd_out(dh3, mix1, gains(post_mix_g, 1), "d_norm_mix_out1")
    dwo = _mm(att, dmix1, "tn", "d_w_o", out_dtype=BF16)[0]
    dw[("wo", 0)] = dwo.reshape(N_CHIPS, dwo.shape[0] // N_CHIPS, d)
    datt = _mm(dmix1, rows("wo"), "nt", "d_attn_out", out_dtype=BF16)[0]
    ffn1_keys = [("wup", 1)]
    (dq, dk, dv), (rode,) = _attn_bwd(q, k, v, datt, lsum, "attn_bwd", rides=[scatter_ride(ffn1_keys)])
    scattered(ffn1_keys, rode)
    for nm, act, dact in (("wq", hn2, dq), ("wk", kvn, dk), ("wv", kvn, dv)):
        g = _mm(act, dact, "tn", f"d_{nm}", out_dtype=BF16)[0]
        dw[(nm, 0)] = g.reshape(N_CHIPS, g.shape[0] // N_CHIPS, g.shape[1])
    dhn2 = _mm(dq, rows("wq"), "nt", "d_q_in")[0]
    dkvn_k = _mm(dk, rows("wk"), "nt", "d_k_in")[0]
    attn_keys = [("wo", 0), ("wq", 0), ("wk", 0), ("wv", 0)]
    dkvn_v, (rode,) = _mm(dv, rows("wv"), "nt", "d_v_in", rides=[swap_ride(attn_keys)])
    swapped(attn_keys, rode)
    dh2, (dg[("pre_mix", 1)], dg["kv"]) = _rms_bwd_in(
        dh3, h2, [([dhn2], gains(pre_mix_g, 1)), ([dkvn_k, dkvn_v[0]], kv_g)], "d_norm_mix_in1")
    dhn1, (rode,) = ffn_bwd(dh2, h1, hn1, a0, hm0, f0, 0, act_rides=[scatter_ride(attn_keys)])
    scattered(attn_keys, rode)
    up0_pair = [pair[("wup", 0)]]
    up0_landed = [None]

    def up0_piece(part, span):
        return _ride_scatter(up0_pair, part, 8, into=up0_landed[0], span=span)

    def up0_rode(rode):
        up0_landed[0] = rode[1] if up0_landed[0] is None else rode[0]

    dh1, (dg[("pre_ffn", 0)],), (rode,) = _rms_bwd_in(
        dh2, h1, [([dhn1], gains(pre_ffn_g, 0))], "d_norm_ffn_in0", rides=[up0_piece(0, 1)])
    up0_rode(rode)
    dmix0, dg[("post_mix", 0)], (rode,) = _rms_bwd_out(
        dh1, mix0, gains(post_mix_g, 0), "d_norm_mix_out0", rides=[up0_piece(1, 1)])
    up0_rode(rode)
    early = ["wq", "wk", "wv", "wo", "wdn"]
    dwout, (((joined_early, _)),) = _mm(
        gm, dmix0, "tn", "d_w_out", out_dtype=BF16, rides=[_ride_join([half_done[nm] for nm in early])])
    grads_big = dict(zip(early, joined_early))
    w_out_key, w_in_key = [("wout", 0)], [("win", 0)]
    dw[w_out_key[0]] = dwout[0].reshape(N_CHIPS, dwout.shape[1] // N_CHIPS, d)
    dgm, (rode, up0) = _mm(dmix0, rows("wout"), "nt", "d_gmlp_gate", rides=[swap_ride(w_out_key), up0_piece(2, 1)])
    swapped(w_out_key, rode)
    up0_rode(up0)
    (duv, d_ws, d_bs, d_vg), (rode,) = _gmlp_bwd(uv, dgm[0], v_g, w_s, bias, "d_gmlp", rides=[up0_piece(3, 2)])
    up0_rode(rode)
    dw[w_in_key[0]], (rode, up0) = _mm(
        hn0, duv, "tn", "d_w_in", out_dtype=BF16, out_split=N_CHIPS, rides=[scatter_ride(w_out_key), up0_piece(5, 1)])
    scattered(w_out_key, rode)
    up0_rode(up0)
    dhn0, (rode, up0) = _mm(
        duv, full[("win", 0)], "nt", "d_gmlp_in", rides=[swap_ride(w_in_key), up0_piece(6, 2)])
    swapped(w_in_key, rode)
    up0_rode(up0)
    scattered([("wup", 0)], (None, up0_landed[0]))
    dx, (dg[("pre_mix", 0)],), (rode,) = _rms_bwd_in(
        dh1, h0, [([dhn0[0]], gains(pre_mix_g, 0))], "d_norm_in", rides=[scatter_ride(w_in_key)])
    scattered(w_in_key, rode)

    stack = lambda key: jnp.concatenate([dg[(key, layer)] for layer in range(n_layers)], axis=0)
    small_parts = [
        stack("pre_mix"), stack("post_mix"), stack("pre_ffn"), stack("post_ffn"),
        d_vg, d_ws, d_bs[::SUBLANE], dg["kv"],
        jnp.stack([dg[("conv_w", layer)] for layer in range(n_layers)]),
        jnp.stack([dg[("conv_b", layer)] for layer in range(n_layers)]),
    ]
    late = [nm for nm in big if nm not in early]
    summed, ((joined_late, _),) = _all_reduce_small(
        _pack(small_parts), "small_grads_sum", rides=[_ride_join([half_done[nm] for nm in late])])
    grads_big.update(zip(late, joined_late))
    (g_pre_mix, g_post_mix, g_pre_ffn, g_post_ffn, g_vg, g_ws, g_bs, g_kv, g_cw, g_cb) = _unpack(
        summed, [p.shape for p in small_parts])
    g_vg = lax.dynamic_index_in_dim(g_vg.reshape(N_CHIPS, 1, d_a // N_CHIPS), chip, 0, keepdims=False)
    g_cw = lax.dynamic_index_in_dim(g_cw, chip, 1, keepdims=False)
    g_cb = g_cb.reshape(n_layers, N_CHIPS * ns)
    small = [
        (pre_mix_g, g_pre_mix, m_pre_mix_g, v_pre_mix_g),
        (post_mix_g, g_post_mix, m_post_mix_g, v_post_mix_g),
        (pre_ffn_g, g_pre_ffn, m_pre_ffn_g, v_pre_ffn_g),
        (post_ffn_g, g_post_ffn, m_post_ffn_g, v_post_ffn_g),
        (a_v_norm_g, g_vg, m_a_v_norm_g, v_a_v_norm_g),
        (a_w_spatial, g_ws[None], m_a_w_spatial, v_a_w_spatial),
        (a_b_spatial, g_bs[None], m_a_b_spatial, v_a_b_spatial),
        (kv_norm_g, g_kv.reshape(d), m_kv_norm_g, v_kv_norm_g),
        (ffn_conv_w, g_cw, m_ffn_conv_w, v_ffn_conv_w),
        (ffn_conv_b, g_cb, m_ffn_conv_b, v_ffn_conv_b),
    ]
    small = [(w, g.reshape(w.shape), m, v) for w, g, m, v in small]
    packed = [_pack([t[i] for t in small])[None] for i in range(4)]
    small_new = [_unpack(p[0], [t[0].shape for t in small]) for p in _adamw(*packed, "adamw_small")]

    new_big = {nm: _adamw(big[nm][0], grads_big[nm], big[nm][1], big[nm][2], f"adamw_{nm}", pass_g=True)
               for nm in big}

    def big_out(nm, which):
        ref_shape = {"wk": w_k.shape, "wv": w_v.shape}.get(nm, big[nm][0].shape)
        return new_big[nm][which].reshape(ref_shape)

    order = ["pre_mix", "post_mix", "pre_ffn", "post_ffn", "win", "vg", "ws", "bs", "wout", "kv", "wk", "wv", "wq",
             "wo", "wup", "cw", "cb", "wdn"]
    small_at = {"pre_mix": 0, "post_mix": 1, "pre_ffn": 2, "post_ffn": 3, "vg": 4, "ws": 5, "bs": 6, "kv": 7,
                "cw": 8, "cb": 9}
    outs = [loss, dx[None]]
    for which in range(4):
        for nm in order:
            if nm in small_at:
                outs.append(small[small_at[nm]][1] if which == 0 else small_new[which - 1][small_at[nm]])
            else:
                outs.append(big_out(nm, which))
    return tuple(outs)
```

```python
import functools
import math

import jax
import jax.numpy as jnp
from jax import lax
from jax.experimental import pallas as pl
from jax.experimental.pallas import tpu as pltpu

F32 = jnp.float32
BF16 = jnp.bfloat16
EPS = 1e-6
ADAM_LR = 0.001
ADAM_B1 = 0.9
ADAM_B2 = 0.999
ADAM_EPS = 1e-08
ADAM_WD = 0.01
ADAM_STEP = 10

LANE = 128
SUBLANE = 8
ROWS = 16
TILE = 128
N_CHIPS = 4
N_DEV = 8
VMEM_LIMIT = 56 * 1024 * 1024
MM_VMEM = 46 * 1024 * 1024
MXU_WIDTH = 256
MESH = pl.DeviceIdType.MESH
ANY = pl.BlockSpec(memory_space=pl.ANY)
VMEM_SPEC = pl.BlockSpec(memory_space=pltpu.VMEM)


def _cp(*sem):
    return pltpu.CompilerParams(dimension_semantics=sem, vmem_limit_bytes=VMEM_LIMIT)


def _pick(dim, pref, align=LANE):
    if dim <= pref:
        return dim
    best = None
    for d in range(align, pref + 1, align):
        if dim % d == 0:
            best = d
    assert best is not None, (dim, pref)
    return best


_DIMS = {
    "nn": (((1,), (0,)), ((), ())),
    "nt": (((1,), (1,)), ((), ())),
    "tn": (((0,), (0,)), ((), ())),
}


def _as3(a):
    return a if a.ndim == 3 else a[None]


def _spec3(br, bc, cols_j, rc):
    per = cols_j // bc

    def imap(m, n, k):
        r, c = rc(m, n, k)
        return (c // per, r, c % per)

    return pl.BlockSpec((None, br, bc), imap)


def _mm(a, b, mode, name, out_dtype=F32, out_split=1, rides=()):
    a, b = _as3(a), _as3(b)
    ja, ra, caj = a.shape
    jb, rb, cbj = b.shape
    if mode == "nn":
        m, k, n = ra, ja * caj, jb * cbj
        assert rb == k
        m_ext, k_ext, n_ext = [ra], [caj, rb], [cbj]
    elif mode == "nt":
        m, k, n = ra, ja * caj, rb
        assert jb * cbj == k
        m_ext, k_ext, n_ext = [ra], [caj, cbj], [rb]
    else:
        m, k, n = ja * caj, ra, jb * cbj
        assert rb == k
        m_ext, k_ext, n_ext = [caj], [ra], [cbj]
    assert n % out_split == 0
    n_ext.append(n // out_split)
    bm = _pick(math.gcd(*m_ext), 1536)
    n_unit = math.gcd(*n_ext)
    bn = _pick(n_unit, 1536)
    k_unit = math.gcd(*k_ext)
    o_bytes = jnp.dtype(out_dtype).itemsize

    def vmem_need(bm, bn, bk):
        tiles = bm * bk * a.dtype.itemsize + bk * bn * b.dtype.itemsize + bm * bn * o_bytes
        return 2 * tiles + bm * bn * 4 * (2 if bk < k else 1)

    def deepest(bm, bn):
        return max(d for d in range(LANE, k_unit + 1, LANE)
                   if k_unit % d == 0 and (d == LANE or vmem_need(bm, bn, d) <= MM_VMEM))

    bk = deepest(bm, bn)
    if bk < k_unit and k_unit == k:
        if bm % (2 * LANE) == 0 and deepest(bm // 2, bn) == k:
            bm, bk = bm // 2, k
        elif bn % (2 * LANE) == 0 and deepest(bm, bn // 2) == k:
            bn, bk = bn // 2, k
    n_outer = False
    if bn % MXU_WIDTH and n_unit % MXU_WIDTH == 0 and k_unit == k:
        for rows in (bm, bm // 2, bm // 4):
            if rows % LANE == 0 and vmem_need(rows, n_unit, k) <= MM_VMEM:
                bm, bn, bk, n_outer = rows, n_unit, k, True
                break
        else:
            if k % (2 * LANE) == 0 and vmem_need(bm, n_unit, k // 2) <= MM_VMEM:
                bn, bk, n_outer = n_unit, k // 2, True
    nk = k // bk
    order = (lambda f: lambda ni, mi, ki: f(mi, ni, ki)) if n_outer else (lambda f: f)
    if mode == "nn":
        a_spec = _spec3(bm, bk, caj, order(lambda mi, ni, ki: (mi, ki)))
        b_spec = _spec3(bk, bn, cbj, order(lambda mi, ni, ki: (ki, ni)))
    elif mode == "nt":
        a_spec = _spec3(bm, bk, caj, order(lambda mi, ni, ki: (mi, ki)))
        b_spec = _spec3(bn, bk, cbj, order(lambda mi, ni, ki: (ni, ki)))
    else:
        a_spec = _spec3(bk, bm, caj, order(lambda mi, ni, ki: (ki, mi)))
        b_spec = _spec3(bk, bn, cbj, order(lambda mi, ni, ki: (ki, ni)))
    o_spec = _spec3(bm, bn, n // out_split, order(lambda mi, ni, ki: (mi, ni)))
    dims = _DIMS[mode]

    def body(a_ref, b_ref, o_ref, *acc):
        def part():
            return lax.dot_general(a_ref[...].astype(BF16), b_ref[...].astype(BF16), dims, preferred_element_type=F32)

        if nk == 1:
            o_ref[...] = part().astype(o_ref.dtype)
            return
        acc_ref, = acc
        ki = pl.program_id(2)

        @pl.when(ki == 0)
        def _():
            acc_ref[...] = part()

        @pl.when(jnp.logical_and(ki > 0, ki < nk - 1))
        def _():
            acc_ref[...] += part()

        @pl.when(ki == nk - 1)
        def _():
            o_ref[...] = (acc_ref[...] + part()).astype(o_ref.dtype)

    grid = (n // bn, m // bm, nk) if n_outer else (m // bm, n // bn, nk)
    out, rode = _hosted_call(
        body, [a, b], name=name, grid=grid, in_specs=[a_spec, b_spec], out_specs=o_spec,
        out_shape=jax.ShapeDtypeStruct((out_split, m, n // out_split), out_dtype),
        scratch_shapes=[pltpu.VMEM((bm, bn), F32)] if nk > 1 else [],
        semantics=("parallel", "parallel", "arbitrary"), rides=rides)
    return (out, rode) if rides else out


def _rms(x, g):
    r = lax.rsqrt(jnp.mean(x * x, axis=-1, keepdims=True) + EPS)
    return x * r * g


def _rms_bwd(x, g, dy):
    r = lax.rsqrt(jnp.mean(x * x, axis=-1, keepdims=True) + EPS)
    xh = x * r
    gy = dy * g
    dx = r * (gy - xh * jnp.mean(gy * xh, axis=-1, keepdims=True))
    return dx, jnp.sum(dy * xh, axis=0, keepdims=True)


def _row_block(s, streams):
    return _pick(s, 512 if streams <= 4 else 256, ROWS)


def _rms_fwd(h, g, name):
    s, d = h.shape
    br = _row_block(s, 2)

    def body(h_ref, g_ref, o_ref):
        o_ref[...] = _rms(h_ref[...], g_ref[...]).astype(BF16)

    row = pl.BlockSpec((br, d), lambda i: (i, 0))
    vec = pl.BlockSpec((1, d), lambda i: (0, 0))
    return pl.pallas_call(
        body, name=name, grid=(s // br,), in_specs=[row, vec], out_specs=row,
        out_shape=jax.ShapeDtypeStruct((s, d), BF16), compiler_params=_cp("parallel"),
    )(h, g)


def _resid_rms(h_in, f, g_post, g_next, name, rides=()):
    s, d = h_in.shape
    br = _row_block(s, 3 + (len(g_next) + 1) // 2)
    n_next = len(g_next)

    def body(h_ref, f_ref, gp_ref, *refs):
        gn_refs, ho_ref, hn_refs = refs[:n_next], refs[n_next], refs[n_next + 1:]
        h = h_ref[...] + _rms(f_ref[...], gp_ref[...])
        ho_ref[...] = h
        for gn_ref, hn_ref in zip(gn_refs, hn_refs):
            hn_ref[...] = _rms(h, gn_ref[...]).astype(BF16)

    row = pl.BlockSpec((br, d), lambda i: (i, 0))
    vec = pl.BlockSpec((1, d), lambda i: (0, 0))
    outs, rode = _hosted_call(
        body, [h_in, f, g_post, *g_next], name=name, grid=(s // br,),
        in_specs=[row, row, vec] + [vec] * n_next,
        out_specs=[row] * (1 + n_next),
        out_shape=[jax.ShapeDtypeStruct((s, d), F32)] + [jax.ShapeDtypeStruct((s, d), BF16)] * n_next,
        semantics=("parallel",), rides=rides)
    return (outs, rode) if rides else outs


def _loss_head(h_in, f, g_post, target, name):
    s, d = h_in.shape
    br = _row_block(s, 4)

    def body(h_ref, f_ref, gp_ref, t_ref, dh_ref, loss_ref):
        @pl.when(pl.program_id(0) == 0)
        def _():
            loss_ref[...] = jnp.zeros_like(loss_ref)

        diff = h_ref[...] + _rms(f_ref[...], gp_ref[...]) - t_ref[...]
        dh_ref[...] = diff * (1.0 / d)
        loss_ref[...] += 0.5 * jnp.sum(jnp.mean(diff * diff, axis=-1, keepdims=True))

    row = pl.BlockSpec((br, d), lambda i: (i, 0))
    vec = pl.BlockSpec((1, d), lambda i: (0, 0))
    return pl.pallas_call(
        body, name=name, grid=(s // br,),
        in_specs=[row, row, vec, row],
        out_specs=[row, pl.BlockSpec((SUBLANE, LANE), lambda i: (0, 0))],
        out_shape=[jax.ShapeDtypeStruct((s, d), F32), jax.ShapeDtypeStruct((SUBLANE, LANE), F32)],
        compiler_params=_cp("arbitrary"),
    )(h_in, f, g_post, target)


def _rms_bwd_out(dy, f, g, name, rides=()):
    s, d = f.shape
    br = _row_block(s, 3)

    def body(dy_ref, f_ref, g_ref, df_ref, dg_ref):
        @pl.when(pl.program_id(0) == 0)
        def _():
            dg_ref[...] = jnp.zeros_like(dg_ref)

        dx, dg = _rms_bwd(f_ref[...], g_ref[...], dy_ref[...])
        df_ref[...] = dx.astype(BF16)
        dg_ref[...] += dg

    row = pl.BlockSpec((br, d), lambda i: (i, 0))
    vec = pl.BlockSpec((1, d), lambda i: (0, 0))
    (df, dg), rode = _hosted_call(
        body, [dy, f, g], name=name, grid=(s // br,), in_specs=[row, row, vec], out_specs=[row, vec],
        out_shape=[jax.ShapeDtypeStruct((s, d), BF16), jax.ShapeDtypeStruct((1, d), F32)],
        semantics=("arbitrary",), rides=rides)
    return (df, dg, rode) if rides else (df, dg)


def _rms_bwd_in(dh_out, h_in, branches, name, rides=()):
    s, d = h_in.shape
    br = _row_block(s, 3 + sum(len(ds) for ds, _ in branches))
    counts = [len(ds) for ds, _ in branches]
    n_d = sum(counts)
    n_b = len(branches)

    def body(dho_ref, h_ref, *refs):
        d_refs, g_refs = refs[:n_d], refs[n_d:n_d + n_b]
        dh_ref, dg_refs = refs[n_d + n_b], refs[n_d + n_b + 1:]

        @pl.when(pl.program_id(0) == 0)
        def _():
            for r in dg_refs:
                r[...] = jnp.zeros_like(r)

        h = h_ref[...]
        acc = dho_ref[...]
        at = 0
        for bi, cnt in enumerate(counts):
            dn = d_refs[at][...]
            for r in d_refs[at + 1:at + cnt]:
                dn = dn + r[...]
            at += cnt
            dx, dg = _rms_bwd(h, g_refs[bi][...], dn)
            acc = acc + dx
            dg_refs[bi][...] += dg
        dh_ref[...] = acc

    row = pl.BlockSpec((br, d), lambda i: (i, 0))
    vec = pl.BlockSpec((1, d), lambda i: (0, 0))
    flat_d = [x for ds, _ in branches for x in ds]
    outs, rode = _hosted_call(
        body, [dh_out, h_in, *flat_d, *[g for _, g in branches]], name=name, grid=(s // br,),
        in_specs=[row, row] + [row] * n_d + [vec] * n_b,
        out_specs=[row] + [vec] * n_b,
        out_shape=[jax.ShapeDtypeStruct((s, d), F32)] + [jax.ShapeDtypeStruct((1, d), F32)] * n_b,
        semantics=("arbitrary",), rides=rides)
    return (outs[0], list(outs[1:]), rode) if rides else (outs[0], list(outs[1:]))


def _split3(x):
    x0 = x.astype(BF16)
    r1 = x - x0.astype(F32)
    x1 = r1.astype(BF16)
    x2 = (r1 - x1.astype(F32)).astype(BF16)
    return x0, x1, x2


def _tri(n, kind):
    r = lax.broadcasted_iota(jnp.int32, (n, n), 0)
    c = lax.broadcasted_iota(jnp.int32, (n, n), 1)
    m = {"lt": r < c, "le": r <= c, "gt": r > c}[kind]
    return jnp.where(m, 1.0, 0.0).astype(BF16)


_GELU_C = math.sqrt(2.0 / math.pi)
_GELU_A = 0.044715


def _gelu(x):
    return 0.5 * x * (1.0 + jnp.tanh(_GELU_C * (x + _GELU_A * (x * x * x))))


def _gelu_grad(x):
    t = jnp.tanh(_GELU_C * (x + _GELU_A * (x * x * x)))
    return 0.5 * (1.0 + t) + 0.5 * x * (1.0 - t * t) * (_GELU_C * (1.0 + 3.0 * _GELU_A * (x * x)))


def _causal_w(w):
    r = lax.broadcasted_iota(jnp.int32, (TILE, TILE), 0)
    c = lax.broadcasted_iota(jnp.int32, (TILE, TILE), 1)
    return jnp.where(c <= r, w, 0.0)


def _uv_tiles(uv_ref, g, d_a, dq):
    cu, cv = g * TILE, d_a + g * TILE
    u = uv_ref[cu // dq, :, pl.ds(cu % dq, TILE)]
    v = uv_ref[cv // dq, :, pl.ds(cv % dq, TILE)]
    return u, v


def _gmlp_fwd(uv, v_g, w_s, bias, name, rides=()):
    _, s, dq = uv.shape
    d_a = 2 * dq
    n_g = d_a // TILE

    def body(uv_ref, vg_ref, ws_ref, b_ref, o_ref):
        for g in range(n_g):
            up, vp = _uv_tiles(uv_ref, g, d_a, dq)
            cols = pl.ds(g * TILE, TILE)
            vn = _rms(_gelu(vp), vg_ref[:, cols])
            mixed = jnp.dot(_causal_w(ws_ref[g]).astype(BF16), vn.astype(BF16), preferred_element_type=F32) + b_ref[:, cols]
            o_ref[:, cols] = (_gelu(up) * mixed).astype(BF16)

    return _hosted_call(
        body, [uv, v_g, w_s, bias], name=name, grid=(s // TILE,),
        in_specs=[
            pl.BlockSpec((4, TILE, dq), lambda i: (0, i, 0)),
            pl.BlockSpec((1, d_a), lambda i: (0, 0)),
            pl.BlockSpec((n_g, TILE, TILE), lambda i: (0, 0, 0)),
            pl.BlockSpec((TILE, d_a), lambda i: (0, 0)),
        ],
        out_specs=pl.BlockSpec((TILE, d_a), lambda i: (i, 0)),
        out_shape=jax.ShapeDtypeStruct((s, d_a), BF16),
        semantics=("parallel",), rides=rides)


def _gmlp_bwd(uv, dgm, v_g, w_s, bias, name, rides=()):
    _, s, dq = uv.shape
    d_a = 2 * dq
    n_g = d_a // TILE
    n_c = s // TILE

    def body(uv_ref, d_ref, vg_ref, ws_ref, b_ref, duv_ref, dws_ref, dbs_ref, dvg_ref, dbias_acc):
        i = pl.program_id(0)

        @pl.when(i == 0)
        def _():
            dws_ref[...] = jnp.zeros_like(dws_ref)
            dvg_ref[...] = jnp.zeros_like(dvg_ref)
            dbias_acc[...] = jnp.zeros_like(dbias_acc)

        for g in range(n_g):
            up, vp = _uv_tiles(uv_ref, g, d_a, dq)
            cols = pl.ds(g * TILE, TILE)
            vg = vg_ref[:, cols]
            u = _gelu(up)
            v = _gelu(vp)
            r = lax.rsqrt(jnp.mean(v * v, axis=-1, keepdims=True) + EPS)
            vh = v * r
            vn = (vh * vg).astype(BF16)
            wc = _causal_w(ws_ref[g]).astype(BF16)
            mixed = jnp.dot(wc, vn, preferred_element_type=F32) + b_ref[:, cols]
            d_out = d_ref[:, cols]
            du = d_out * mixed
            dmixed = d_out * u
            dmb = dmixed.astype(BF16)
            dvn = lax.dot_general(wc, dmb, _DIMS["tn"], preferred_element_type=F32)
            dws_ref[g] += lax.dot_general(dmb, vn, _DIMS["nt"], preferred_element_type=F32)
            dbias_acc[:, cols] += dmixed
            dvg_ref[:, cols] += jnp.sum(dvn * vh, axis=0, keepdims=True)
            gv = dvn * vg
            dv = r * (gv - vh * jnp.mean(gv * vh, axis=-1, keepdims=True))
            cu, cv = g * TILE, d_a + g * TILE
            duv_ref[cu // dq, :, pl.ds(cu % dq, TILE)] = (du * _gelu_grad(up)).astype(BF16)
            duv_ref[cv // dq, :, pl.ds(cv % dq, TILE)] = (dv * _gelu_grad(vp)).astype(BF16)

        @pl.when(i == n_c - 1)
        def _():
            ones = jnp.ones((SUBLANE, TILE), BF16)
            for g in range(n_g):
                dws_ref[g] = _causal_w(dws_ref[g])
                cols = pl.ds(g * TILE, TILE)
                out = None
                for t in _split3(dbias_acc[:, cols]):
                    p = lax.dot_general(ones, t, _DIMS["nt"], preferred_element_type=F32)
                    out = p if out is None else out + p
                dbs_ref[pl.ds(g * SUBLANE, SUBLANE), :] = out

    return _hosted_call(
        body, [uv, dgm, v_g, w_s, bias], name=name, grid=(n_c,), semantics=("arbitrary",), rides=rides,
        in_specs=[
            pl.BlockSpec((4, TILE, dq), lambda i: (0, i, 0)),
            pl.BlockSpec((TILE, d_a), lambda i: (i, 0)),
            pl.BlockSpec((1, d_a), lambda i: (0, 0)),
            pl.BlockSpec((n_g, TILE, TILE), lambda i: (0, 0, 0)),
            pl.BlockSpec((TILE, d_a), lambda i: (0, 0)),
        ],
        out_specs=[
            pl.BlockSpec((4, TILE, dq), lambda i: (0, i, 0)),
            pl.BlockSpec((n_g, TILE, TILE), lambda i: (0, 0, 0)),
            pl.BlockSpec((n_g * SUBLANE, TILE), lambda i: (0, 0)),
            pl.BlockSpec((1, d_a), lambda i: (0, 0)),
        ],
        out_shape=[
            jax.ShapeDtypeStruct((4, s, dq), BF16),
            jax.ShapeDtypeStruct((n_g, TILE, TILE), F32),
            jax.ShapeDtypeStruct((n_g * SUBLANE, TILE), F32),
            jax.ShapeDtypeStruct((1, d_a), F32),
        ],
        scratch_shapes=[pltpu.VMEM((TILE, d_a), F32)])


def _sigmoid(x):
    return 1.0 / (1.0 + jnp.exp(-x))


def _conv3(ext, w, b):
    return b + ((w[0:1] * pltpu.roll(ext, 2, 0) + w[1:2] * pltpu.roll(ext, 1, 0)) + w[2:3] * ext)


def _act_blocks(s, ns):
    return _pick(s, 512, ROWS), _pick(ns, 256)


def _ffn_act_fwd(a, cw, cb, name, rides=()):
    _, s, ns = a.shape
    bs, cb_w = _act_blocks(s, ns)
    hb = bs // SUBLANE

    def body(a_ref, prev_ref, cw_ref, cb_ref, o_ref):
        first = pl.program_id(0) == 0

        def conv(comp):
            prev = jnp.where(first, 0.0, prev_ref[comp])
            ext = jnp.concatenate([prev, a_ref[comp]], axis=0)
            return _conv3(ext, cw_ref[comp], cb_ref[comp])[SUBLANE:]

        for p in range(2):
            cg = conv(p)
            o_ref[p] = (cg * _sigmoid(cg) * conv(2 + p)).astype(BF16)

    hm, rode = _hosted_call(
        body, [a, a, cw, cb], name=name, grid=(s // bs, ns // cb_w),
        in_specs=[
            pl.BlockSpec((4, bs, cb_w), lambda i, j: (0, i, j)),
            pl.BlockSpec((4, SUBLANE, cb_w), lambda i, j: (0, jnp.maximum(i * hb - 1, 0), j)),
            pl.BlockSpec((4, 3, cb_w), lambda i, j: (0, 0, j)),
            pl.BlockSpec((4, 1, cb_w), lambda i, j: (0, 0, j)),
        ],
        out_specs=pl.BlockSpec((2, bs, cb_w), lambda i, j: (0, i, j)),
        out_shape=jax.ShapeDtypeStruct((2, s, ns), BF16),
        semantics=("parallel", "parallel"), rides=rides)
    return (hm, rode) if rides else hm


def _ffn_act_bwd(a, dhm, cw, cb, name, rides=()):
    _, s, ns = a.shape
    bs, cb_w = _act_blocks(s, ns)
    hb = bs // SUBLANE
    n_i = s // bs
    n_ext = bs + 2 * SUBLANE
    cur = slice(SUBLANE, SUBLANE + bs)

    def body(a_ref, prev_ref, next_ref, d_ref, dnext_ref, cw_ref, cb_ref, da_ref, dcw_ref, dcb_ref):
        i = pl.program_id(1)
        first, last = i == 0, i == n_i - 1

        @pl.when(first)
        def _():
            dcw_ref[...] = jnp.zeros_like(dcw_ref)
            dcb_ref[...] = jnp.zeros_like(dcb_ref)

        def ext_of(comp):
            return jnp.concatenate([jnp.where(first, 0.0, prev_ref[comp]), a_ref[comp], next_ref[comp]], axis=0)

        def back(comp, a_ext, dc):
            w = cw_ref[comp]
            da = (w[2:3] * dc + w[1:2] * pltpu.roll(dc, n_ext - 1, 0)) + w[0:1] * pltpu.roll(dc, n_ext - 2, 0)
            da_ref[comp] = da[cur].astype(BF16)
            dcc = dc[cur]
            dcw_ref[comp, 0:1, :] += jnp.sum(dcc * pltpu.roll(a_ext, 2, 0)[cur], axis=0, keepdims=True)
            dcw_ref[comp, 1:2, :] += jnp.sum(dcc * pltpu.roll(a_ext, 1, 0)[cur], axis=0, keepdims=True)
            dcw_ref[comp, 2:3, :] += jnp.sum(dcc * a_ext[cur], axis=0, keepdims=True)
            dcb_ref[comp] += jnp.sum(dcc, axis=0, keepdims=True)

        for p in range(2):
            ag, av = ext_of(p), ext_of(2 + p)
            cg = _conv3(ag, cw_ref[p], cb_ref[p])
            cv = _conv3(av, cw_ref[2 + p], cb_ref[2 + p])
            d = jnp.concatenate(
                [jnp.zeros((SUBLANE, cb_w), F32), d_ref[p], jnp.where(last, 0.0, dnext_ref[p])], axis=0)
            sg = _sigmoid(cg)
            back(2 + p, av, d * (cg * sg))
            back(p, ag, d * cv * (sg * (1.0 + cg * (1.0 - sg))))

    return _hosted_call(
        body, [a, a, a, dhm, dhm, cw, cb], name=name, grid=(ns // cb_w, n_i),
        in_specs=[
            pl.BlockSpec((4, bs, cb_w), lambda j, i: (0, i, j)),
            pl.BlockSpec((4, SUBLANE, cb_w), lambda j, i: (0, jnp.maximum(i * hb - 1, 0), j)),
            pl.BlockSpec((4, SUBLANE, cb_w), lambda j, i: (0, jnp.minimum((i + 1) * hb, n_i * hb - 1), j)),
            pl.BlockSpec((2, bs, cb_w), lambda j, i: (0, i, j)),
            pl.BlockSpec((2, SUBLANE, cb_w), lambda j, i: (0, jnp.minimum((i + 1) * hb, n_i * hb - 1), j)),
            pl.BlockSpec((4, 3, cb_w), lambda j, i: (0, 0, j)),
            pl.BlockSpec((4, 1, cb_w), lambda j, i: (0, 0, j)),
        ],
        out_specs=[
            pl.BlockSpec((4, bs, cb_w), lambda j, i: (0, i, j)),
            pl.BlockSpec((4, 3, cb_w), lambda j, i: (0, 0, j)),
            pl.BlockSpec((4, 1, cb_w), lambda j, i: (0, 0, j)),
        ],
        out_shape=[
            jax.ShapeDtypeStruct((4, s, ns), BF16),
            jax.ShapeDtypeStruct((4, 3, ns), F32),
            jax.ShapeDtypeStruct((4, 1, ns), F32),
        ],
        semantics=("parallel", "arbitrary"), rides=rides)


ATT_BQ_FWD = 2048
ATT_BQ_BWD = 1024
ATT_BK = 256
ATT_UNROLL = 2
ATT_UNROLL_BWD = 4


def _att_blocks(s, bq_pref):
    bq = _pick(s, bq_pref)
    bk = min(ATT_BK, bq)
    assert bq % bk == 0
    return bq, bk


def _dot_sel2(x, sel):
    hi = x.astype(BF16)
    lo = (x - hi.astype(F32)).astype(BF16)
    n = x.shape[0]
    both = jnp.dot(jnp.concatenate([hi, lo], axis=0), sel, preferred_element_type=F32)
    return both[:n] + both[n:]


def _causal_mask(bq, bk, row0, col0):
    rows = row0 + lax.broadcasted_iota(jnp.int32, (bq, bk), 0)
    cols = col0 + lax.broadcasted_iota(jnp.int32, (bq, bk), 1)
    return cols < rows


def _sb_tile(qb, kb, scale, mask):
    z = lax.dot_general(qb, kb, _DIMS["nt"], preferred_element_type=F32) * scale
    e = jnp.exp(-jnp.abs(z))
    lb = jnp.minimum(z, 0.0) - jnp.log(1.0 + e)
    l1m = lb - z
    if mask is not None:
        l1m = jnp.where(mask, l1m, 0.0)
    return z, e, lb, l1m


def _attn_fwd(q, k, v, name, rides=()):
    s, hd = q.shape
    bq, bk = _att_blocks(s, ATT_BQ_FWD)
    r = bq // bk
    unroll = math.gcd(r, ATT_UNROLL)
    n_h, n_q = hd // TILE, s // bq
    scale = 1.0 / math.sqrt(TILE)

    def body(q_ref, k_ref, v_ref, o_ref, l_ref, acc_ref, suf_ref):
        i = pl.program_id(1)
        qb = q_ref[...]
        later = _tri(bk, "gt")
        acc_ref[...] = jnp.zeros_like(acc_ref)
        suf_ref[...] = jnp.zeros_like(suf_ref)

        def tile(j, row0):
            rows = pl.ds(pl.multiple_of(j * bk, bk), bk)
            masked = row0 is not None
            r0 = row0 if masked else 0
            rs = pl.ds(r0, bq - r0)
            mask = _causal_mask(bq - r0, bk, i * bq + r0, j * bk) if masked else None
            _, _, lb, l1m = _sb_tile(qb[r0:], k_ref[rows, :], scale, mask)
            a = jnp.exp(lb + _dot_sel2(l1m, later) + suf_ref[rs, :])
            if masked:
                a = jnp.where(mask, a, 0.0)
            acc_ref[rs, :] += jnp.dot(a.astype(BF16), v_ref[rows, :], preferred_element_type=F32)
            suf_ref[rs, :] += jnp.sum(l1m, axis=1, keepdims=True)

        for dgl in range(r - 1, -1, -1):
            tile(r * i + dgl, dgl * bk)

        def step(t, carry):
            for u in range(unroll):
                tile(r * i - 1 - (unroll * t + u), None)
            return carry

        lax.fori_loop(0, (r * i) // unroll, step, 0)
        o_ref[...] = acc_ref[...].astype(BF16)
        l_ref[...] = jnp.broadcast_to(suf_ref[...], (bq, TILE))

    blk = pl.BlockSpec((bq, TILE), lambda h, i: (i, h))
    head = pl.BlockSpec((s, TILE), lambda h, i: (0, h))
    return _hosted_call(
        body, [q, k, v], name=name, grid=(n_h, n_q), in_specs=[blk, head, head], out_specs=[blk, blk],
        out_shape=[jax.ShapeDtypeStruct((s, hd), BF16), jax.ShapeDtypeStruct((s, hd), F32)],
        scratch_shapes=[pltpu.VMEM((bq, TILE), F32), pltpu.VMEM((bq, 1), F32)],
        semantics=("parallel", "parallel"), rides=rides)


def _attn_bwd(q, k, v, do, lsum, name, rides=()):
    s, hd = q.shape
    bq, bk = _att_blocks(s, ATT_BQ_BWD)
    r = bq // bk
    unroll = math.gcd(r, ATT_UNROLL_BWD)
    n_h, n_q = hd // TILE, s // bq
    scale = 1.0 / math.sqrt(TILE)

    def body(q_ref, k_ref, v_ref, do_ref, l_ref, dq_ref, dk_ref, dv_ref, dq_acc, pre_ref, cp_ref):
        i = pl.program_id(1)

        @pl.when(i == 0)
        def _():
            dk_ref[...] = jnp.zeros_like(dk_ref)
            dv_ref[...] = jnp.zeros_like(dv_ref)

        qb = q_ref[...]
        dob = do_ref[...]
        upto = _tri(bk, "le")
        before = _tri(bk, "lt")
        dq_acc[...] = jnp.zeros_like(dq_acc)
        pre_ref[...] = jnp.zeros_like(pre_ref)
        cp_ref[...] = jnp.zeros_like(cp_ref)

        def tile(j, row0):
            rows = pl.ds(pl.multiple_of(j * bk, bk), bk)
            kb, vb = k_ref[rows, :], v_ref[rows, :]
            masked = row0 is not None
            r0 = row0 if masked else 0
            rs = pl.ds(r0, bq - r0)
            qs, dos = qb[r0:], dob[r0:]
            mask = _causal_mask(bq - r0, bk, i * bq + r0, j * bk) if masked else None
            z, e, lb, l1m = _sb_tile(qs, kb, scale, mask)
            suffix = (l_ref[rs, 0:1] - pre_ref[rs, :]) - _dot_sel2(l1m, upto)
            a = jnp.exp(lb + suffix)
            if masked:
                a = jnp.where(mask, a, 0.0)
            p = a * lax.dot_general(dos, vb, _DIMS["nt"], preferred_element_type=F32)
            both = p + (cp_ref[rs, :] + jnp.dot(p.astype(BF16), before, preferred_element_type=F32))
            sg = jnp.where(z >= 0.0, 1.0, e) * pl.reciprocal(1.0 + e, approx=True)
            dz = p - both * sg
            if masked:
                dz = jnp.where(mask, dz, 0.0)
            dz = (dz * scale).astype(BF16)
            dq_acc[rs, :] += jnp.dot(dz, kb, preferred_element_type=F32)
            dk_ref[rows, :] += lax.dot_general(dz, qs, _DIMS["tn"], preferred_element_type=F32)
            dv_ref[rows, :] += lax.dot_general(a.astype(BF16), dos, _DIMS["tn"], preferred_element_type=F32)
            pre_ref[rs, :] += jnp.sum(l1m, axis=1, keepdims=True)
            cp_ref[rs, :] += jnp.sum(p, axis=1, keepdims=True)

        def step(j, carry):
            for u in range(unroll):
                tile(unroll * j + u, None)
            return carry

        lax.fori_loop(0, (r * i) // unroll, step, 0)
        for dgl in range(r):
            tile(r * i + dgl, dgl * bk)
        dq_ref[...] = dq_acc[...].astype(BF16)

    blk = pl.BlockSpec((bq, TILE), lambda h, i: (i, h))
    head = pl.BlockSpec((s, TILE), lambda h, i: (0, h))
    return _hosted_call(
        body, [q, k, v, do, lsum], name=name, grid=(n_h, n_q), in_specs=[blk, head, head, blk, blk],
        out_specs=[blk, head, head],
        out_shape=[jax.ShapeDtypeStruct((s, hd), BF16), jax.ShapeDtypeStruct((s, hd), F32),
                   jax.ShapeDtypeStruct((s, hd), F32)],
        scratch_shapes=[pltpu.VMEM((bq, TILE), F32), pltpu.VMEM((bq, 1), F32), pltpu.VMEM((bq, 1), F32)],
        semantics=("parallel", "arbitrary"), rides=rides)


EW_BLOCK = 512 * 1024


def _ew_blocks(r, c, elems=EW_BLOCK):
    return _pick(r, max(ROWS, elems // c // ROWS * ROWS), ROWS), c


def _cast_bf16(w, layer, chip_idx, name):
    _, r, c = w.shape
    br, bc = _ew_blocks(r, c)

    def body(chip_ref, w_ref, o_ref):
        o_ref[...] = w_ref[...].astype(BF16)

    return pl.pallas_call(
        body, name=name,
        grid_spec=pltpu.PrefetchScalarGridSpec(
            num_scalar_prefetch=1, grid=(r // br, c // bc),
            in_specs=[pl.BlockSpec((None, br, bc), lambda i, j, chip_ref: (layer, i, j))],
            out_specs=pl.BlockSpec((None, br, bc), lambda i, j, chip_ref: (chip_ref[0], i, j)),
        ),
        out_shape=jax.ShapeDtypeStruct((N_CHIPS, r, c), BF16), compiler_params=_cp("parallel", "parallel"),
    )(chip_idx, w)


def _pair_add(dw, recv, c_idx, name):
    _, r, c = dw.shape
    hr = r // 2
    br, bc = _ew_blocks(hr, c)
    nb = hr // br

    def body(c_ref, a_ref, b_ref, o_ref):
        o_ref[...] = (a_ref[...].astype(F32) + b_ref[...].astype(F32)).astype(BF16)

    return pl.pallas_call(
        body, name=name,
        grid_spec=pltpu.PrefetchScalarGridSpec(
            num_scalar_prefetch=1, grid=(N_CHIPS, nb, c // bc),
            in_specs=[
                pl.BlockSpec((None, br, bc), lambda s, i, j, c_ref: (s, c_ref[0] * nb + i, j)),
                pl.BlockSpec((None, br, bc), lambda s, i, j, c_ref: (s, i, j)),
            ],
            out_specs=pl.BlockSpec((None, br, bc), lambda s, i, j, c_ref: (s, i, j)),
        ),
        out_shape=jax.ShapeDtypeStruct((N_CHIPS, hr, c), BF16),
        compiler_params=_cp("parallel", "parallel", "parallel"),
    )(c_idx, dw, recv)


def _chip_sum(parts, dest, shape, layer, c_idx, name):
    _, hr, c = parts.shape
    br, bc = _ew_blocks(hr, c, EW_BLOCK // 2)
    nb = hr // br

    def body(c_ref, p_ref, *refs):
        o_ref = refs[-1]
        acc = p_ref[0].astype(F32)
        for s in range(1, N_CHIPS):
            acc = acc + p_ref[s].astype(F32)
        o_ref[...] = acc

    in_specs = [pl.BlockSpec((N_CHIPS, br, bc), lambda i, j, c_ref: (0, i, j))]
    operands = [c_idx, parts]
    aliases = {}
    if dest is not None:
        in_specs.append(ANY)
        operands.append(dest)
        aliases = {2: 0}
    return pl.pallas_call(
        body, name=name,
        grid_spec=pltpu.PrefetchScalarGridSpec(
            num_scalar_prefetch=1, grid=(nb, c // bc), in_specs=in_specs,
            out_specs=pl.BlockSpec((None, br, bc), lambda i, j, c_ref: (layer, c_ref[0] * nb + i, j)),
        ),
        out_shape=jax.ShapeDtypeStruct(shape, F32), input_output_aliases=aliases,
        compiler_params=_cp("parallel", "parallel"),
    )(*operands)


def _adamw(w, g, m, v, name, pass_g=False):
    n_l, r, c = w.shape
    br, bc = _ew_blocks(r, c, EW_BLOCK // 2)

    def body(w_ref, g_ref, m_ref, v_ref, *out_refs):
        d_ref, mo_ref, vo_ref = out_refs[-3:]
        g = g_ref[...]
        if pass_g:
            out_refs[0][...] = g
        m = ADAM_B1 * m_ref[...] + (1.0 - ADAM_B1) * g
        v = ADAM_B2 * v_ref[...] + (1.0 - ADAM_B2) * (g * g)
        m_hat = m / (1.0 - ADAM_B1 ** ADAM_STEP)
        v_hat = v / (1.0 - ADAM_B2 ** ADAM_STEP)
        d_ref[...] = -ADAM_LR * (m_hat / (jnp.sqrt(v_hat) + ADAM_EPS) + ADAM_WD * w_ref[...])
        mo_ref[...] = m
        vo_ref[...] = v

    blk = pl.BlockSpec((None, br, bc), lambda l, i, j: (l, i, j))
    n_out = 4 if pass_g else 3
    return pl.pallas_call(
        body, name=name, grid=(n_l, r // br, c // bc), in_specs=[blk] * 4, out_specs=[blk] * n_out,
        out_shape=[jax.ShapeDtypeStruct(w.shape, F32)] * n_out,
        compiler_params=_cp("parallel", "parallel", "parallel"),
    )(w, g, m, v)


def _place():
    x, y, c = lax.axis_index("x"), lax.axis_index("y"), lax.axis_index("c")
    chips = [(1 - x, y), (x, 1 - y), (1 - x, 1 - y)]
    return x, y, c, chips


class _Ride:
    def __init__(self, reads, bufs, new, n_sems, start, finish):
        self.reads, self.bufs, self.new, self.n_sems, self.start, self.finish = reads, bufs, new, n_sems, start, finish


def _hosted_call(body, operands, *, name, grid, in_specs, out_specs, out_shape, scratch_shapes=(), semantics=(), rides=()):
    single = not isinstance(out_shape, (list, tuple))
    out_specs = [out_specs] if single else list(out_specs)
    out_shape = [out_shape] if single else list(out_shape)
    in_specs, scratch_shapes = list(in_specs), list(scratch_shapes)
    n_in, n_out, n_scr = len(in_specs), len(out_shape), len(scratch_shapes)
    extra_in, extra_out, aliases, where = [], [], {}, []
    for ride in rides:
        r0 = len(extra_in)
        extra_in += list(ride.reads)
        b0 = len(extra_in)
        extra_in += list(ride.bufs)
        ob0 = len(extra_out)
        extra_out += [jax.ShapeDtypeStruct(b.shape, b.dtype) for b in ride.bufs]
        for t in range(len(ride.bufs)):
            aliases[n_in + b0 + t] = n_out + ob0 + t
        on0 = len(extra_out)
        extra_out += list(ride.new)
        where.append((r0, len(ride.reads), ob0, len(ride.bufs), on0, len(ride.new)))
    n_ein, n_eout = len(extra_in), len(extra_out)
    sem_shapes = [pltpu.SemaphoreType.DMA((max(1, k),)) for ride in rides for k in ride.n_sems]

    def full_body(*refs):
        ins, outs, scr = refs[:n_in + n_ein], refs[n_in + n_ein:n_in + n_ein + n_out + n_eout], refs[n_in + n_ein + n_out + n_eout:]

        def run(which):
            for idx, (ride, (r0, nr, ob0, nb, on0, nn)) in enumerate(zip(rides, where)):
                fn = ride.start if which == 0 else ride.finish
                fn(ins[n_in + r0:n_in + r0 + nr], outs[n_out + ob0:n_out + ob0 + nb], outs[n_out + on0:n_out + on0 + nn],
                   *scr[n_scr + 3 * idx:n_scr + 3 * idx + 3])

        host = lambda: body(*ins[:n_in], *outs[:n_out], *scr[:n_scr])
        if not rides:
            host()
        elif not grid:
            run(0)
            host()
            run(1)
        else:
            ids = [pl.program_id(ax) for ax in range(len(grid))]
            first = functools.reduce(jnp.logical_and, [i == 0 for i in ids])
            last = functools.reduce(jnp.logical_and, [i == g - 1 for i, g in zip(ids, grid)])
            pl.when(first)(lambda: run(0))
            host()
            pl.when(last)(lambda: run(1))

    if rides:
        params = pltpu.CompilerParams(dimension_semantics=("arbitrary",) * len(grid), vmem_limit_bytes=VMEM_LIMIT)
    else:
        params = _cp(*semantics)
    outs = pl.pallas_call(
        full_body, name=name, grid=grid,
        in_specs=in_specs + [ANY] * n_ein, out_specs=out_specs + [ANY] * n_eout,
        out_shape=out_shape + extra_out, input_output_aliases=aliases,
        scratch_shapes=scratch_shapes + sem_shapes, compiler_params=params,
    )(*operands, *extra_in)
    main = outs[0] if single else list(outs[:n_out])
    rode = [(list(outs[n_out + ob0:n_out + ob0 + nb]), list(outs[n_out + on0:n_out + on0 + nn]))
            for (_, _, ob0, nb, on0, nn) in where]
    return main, rode


def _run_rides(rides, name):
    return _hosted_call(lambda: None, [], name=name, grid=(), in_specs=[], out_specs=[], out_shape=[], rides=rides)[1]


def _ride_gather(slots, part=0, n_parts=1, span=1, stage=None):
    n = len(slots)
    halves = [a.shape[1] // 2 for a in slots]
    sizes = [hr // n_parts for hr in halves]
    assert part + span <= n_parts
    for a, hr, size in zip(slots, halves, sizes):
        assert a.shape[1] == 2 * hr and hr == size * n_parts and size % ROWS == 0, a.shape

    def remote(bufs, send_sems, recv_sems, i, k, slot, core, to):
        rows = bufs[i].at[slot, pl.ds(pl.multiple_of(core * halves[i] + part * sizes[i], ROWS), span * sizes[i])]
        return pltpu.make_async_remote_copy(
            src_ref=rows, dst_ref=rows, send_sem=send_sems.at[i * 6 + k], recv_sem=recv_sems.at[i * 6 + k],
            device_id=to, device_id_type=MESH)

    def each(fn):
        x, y, c, chips = _place()
        for i in range(n):
            for k, (px, py) in enumerate(chips):
                fn(x, y, c, i, k, px, py)

    def start(reads, bufs, new, send_sems, recv_sems, local_sems):
        cp = functools.partial(remote, bufs, send_sems, recv_sems)
        if stage != "d2d":
            each(lambda x, y, c, i, k, px, py: cp(i, k, 2 * x + y, c, (px, py, c)).start())
        else:
            each(lambda x, y, c, i, k, px, py: cp(i, 3 + k, 2 * px + py, c, (x, y, 1 - c)).start())

    def finish(reads, bufs, new, send_sems, recv_sems, local_sems):
        cp = functools.partial(remote, bufs, send_sems, recv_sems)

        def landed_over_ici(x, y, c, i, k, px, py):
            cp(i, k, 2 * px + py, c, (x, y, c)).wait_recv()
            if stage is None:
                cp(i, 3 + k, 2 * px + py, c, (x, y, 1 - c)).start()

        if stage != "d2d":
            each(landed_over_ici)
        if stage != "ici":
            each(lambda x, y, c, i, k, px, py: cp(i, 3 + k, 2 * px + py, 1 - c, (x, y, c)).wait_recv())
        if stage != "d2d":
            each(lambda x, y, c, i, k, px, py: cp(i, k, 2 * x + y, c, (px, py, c)).wait_send())
        if stage != "ici":
            each(lambda x, y, c, i, k, px, py: cp(i, 3 + k, 2 * px + py, c, (x, y, 1 - c)).wait_send())

    return _Ride([], slots, [], (6 * n, 6 * n, 0), start, finish)


def _ride_swap(grads):
    n = len(grads)
    halves = [a.shape[1] // 2 for a in grads]

    def copies(reads, new, send_sems, recv_sems):
        x, y, c, _ = _place()
        out = []
        for i in range(n):
            rows = pl.ds(pl.multiple_of((1 - c) * halves[i], 2 * SUBLANE), halves[i])
            out.append(pltpu.make_async_remote_copy(
                src_ref=reads[i].at[:, rows, :], dst_ref=new[i], send_sem=send_sems.at[i], recv_sem=recv_sems.at[i],
                device_id=(x, y, 1 - c), device_id_type=MESH))
        return out

    def start(reads, bufs, new, send_sems, recv_sems, local_sems):
        for cp in copies(reads, new, send_sems, recv_sems):
            cp.start()

    def finish(reads, bufs, new, send_sems, recv_sems, local_sems):
        for cp in copies(reads, new, send_sems, recv_sems):
            cp.wait()

    shapes = [jax.ShapeDtypeStruct((N_CHIPS, hr, a.shape[2]), a.dtype) for a, hr in zip(grads, halves)]
    return _Ride(grads, [], shapes, (n, n, 0), start, finish)


def _ride_scatter(parts, part=0, n_parts=1, into=None, span=1):
    n = len(parts)
    sizes = [a.shape[1] // n_parts for a in parts]
    assert part + span <= n_parts
    for a, size in zip(parts, sizes):
        assert a.shape[1] == size * n_parts and size % ROWS == 0, a.shape

    def piece(ref, i, slot):
        return ref.at[slot, pl.ds(part * sizes[i], span * sizes[i])]

    def own(reads, land, local_sems, i):
        me = 2 * lax.axis_index("x") + lax.axis_index("y")
        return pltpu.make_async_copy(piece(reads[i], i, me), piece(land[i], i, me), local_sems.at[i])

    def send(reads, land, send_sems, recv_sems, i, k):
        x, y, c, chips = _place()
        px, py = chips[k]
        return pltpu.make_async_remote_copy(
            src_ref=piece(reads[i], i, 2 * px + py), dst_ref=piece(land[i], i, 2 * x + y),
            send_sem=send_sems.at[3 * i + k], recv_sem=recv_sems.at[3 * i + k],
            device_id=(px, py, c), device_id_type=MESH)

    def start(reads, bufs, new, send_sems, recv_sems, local_sems):
        land = new if into is None else bufs
        for i in range(n):
            own(reads, land, local_sems, i).start()
            for k in range(3):
                send(reads, land, send_sems, recv_sems, i, k).start()

    def finish(reads, bufs, new, send_sems, recv_sems, local_sems):
        land = new if into is None else bufs
        x, y, c, chips = _place()
        for i in range(n):
            for k, (px, py) in enumerate(chips):
                slot = piece(land[i], i, 2 * px + py)
                pltpu.make_async_remote_copy(
                    src_ref=slot, dst_ref=slot, send_sem=send_sems.at[3 * i + k], recv_sem=recv_sems.at[3 * i + k],
                    device_id=(x, y, c), device_id_type=MESH).wait_recv()
        for i in range(n):
            for k in range(3):
                send(reads, land, send_sems, recv_sems, i, k).wait_send()
            own(reads, land, local_sems, i).wait()

    shapes = [jax.ShapeDtypeStruct(a.shape, a.dtype) for a in parts]
    if into is None:
        return _Ride(parts, [], shapes, (3 * n, 3 * n, n), start, finish)
    return _Ride(parts, list(into), [], (3 * n, 3 * n, n), start, finish)


def _ride_join(grads):
    n = len(grads)

    def copy(bufs, send_sems, recv_sems, i, core, to):
        hr = grads[i].shape[1] // 2
        rows = bufs[i].at[:, pl.ds(pl.multiple_of(core * hr, SUBLANE), hr), :]
        return pltpu.make_async_remote_copy(
            src_ref=rows, dst_ref=rows, send_sem=send_sems.at[i], recv_sem=recv_sems.at[i],
            device_id=to, device_id_type=MESH)

    def start(reads, bufs, new, send_sems, recv_sems, local_sems):
        x, y, c, _ = _place()
        for i in range(n):
            copy(bufs, send_sems, recv_sems, i, c, (x, y, 1 - c)).start()

    def finish(reads, bufs, new, send_sems, recv_sems, local_sems):
        x, y, c, _ = _place()
        for i in range(n):
            copy(bufs, send_sems, recv_sems, i, 1 - c, (x, y, c)).wait_recv()
        for i in range(n):
            copy(bufs, send_sems, recv_sems, i, c, (x, y, 1 - c)).wait_send()

    return _Ride([], grads, [], (n, n, 0), start, finish)


def _all_reduce_small(packed, name, rides=()):
    r, c = packed.shape
    chunk = _pick(r, 256, ROWS)

    def body(x_ref, out_ref, gath, send_sems, recv_sems, local_sem):
        x, y, cc, chips = _place()
        me, sibling = (x, y, cc), (x, y, 1 - cc)

        def slot(px, py, pc):
            return gath.at[4 * px + 2 * py + pc]

        def copy(k, block, to, src=None):
            return pltpu.make_async_remote_copy(
                src_ref=slot(*block) if src is None else src, dst_ref=slot(*block),
                send_sem=send_sems.at[k], recv_sem=recv_sems.at[k], device_id=to, device_id_type=MESH)

        mine = pltpu.make_async_copy(x_ref, slot(*me), local_sem)
        mine.start()
        first = [copy(0, me, sibling, src=x_ref)]
        first += [copy(1 + j, me, (*chip, cc), src=x_ref) for j, chip in enumerate(chips)]
        for cp in first:
            cp.start()
        passed = [copy(4 + j, (*chip, cc), sibling) for j, chip in enumerate(chips)]
        for j, chip in enumerate(chips):
            copy(1 + j, (*chip, cc), me).wait_recv()
            passed[j].start()
        copy(0, sibling, me).wait_recv()
        for j, chip in enumerate(chips):
            copy(4 + j, (*chip, 1 - cc), me).wait_recv()
        for cp in first + passed:
            cp.wait_send()
        mine.wait()

        def add(i, carry):
            rows = pl.ds(pl.multiple_of(i * chunk, SUBLANE), chunk)
            acc = gath[0, rows, :]
            for dev in range(1, N_DEV):
                acc = acc + gath[dev, rows, :]
            out_ref[rows, :] = acc
            return carry

        lax.fori_loop(0, r // chunk, add, 0)

    return _hosted_call(
        body, [packed], name=name, grid=(), in_specs=[VMEM_SPEC], out_specs=VMEM_SPEC,
        out_shape=jax.ShapeDtypeStruct((r, c), F32),
        scratch_shapes=[pltpu.VMEM((N_DEV, r, c), F32), pltpu.SemaphoreType.DMA((7,)),
                        pltpu.SemaphoreType.DMA((7,)), pltpu.SemaphoreType.DMA],
        rides=rides)


_PACK_ROWS = 256


def _pack(arrays):
    flat = jnp.concatenate([a.reshape(-1).astype(F32) for a in arrays])
    unit = _PACK_ROWS * LANE
    total = -(-flat.shape[0] // unit) * unit
    return jnp.pad(flat, (0, total - flat.shape[0])).reshape(-1, LANE)


def _unpack(packed, shapes, lead=()):
    flat = packed.reshape(lead + (-1,))
    out, at = [], 0
    for s in shapes:
        size = math.prod(s)
        out.append(flat[..., at:at + size].reshape(lead + tuple(s)))
        at += size
    return out


def kernel(x, pre_mix_g, post_mix_g, pre_ffn_g, post_ffn_g, a_w_in, a_v_norm_g, a_w_spatial, a_b_spatial, a_w_out, kv_norm_g, w_k, w_v, b_w_q, b_w_o, ffn_w_up, ffn_conv_w, ffn_conv_b, ffn_w_down, loss_target, m_pre_mix_g, m_post_mix_g, m_pre_ffn_g, m_post_ffn_g, m_a_w_in, m_a_v_norm_g, m_a_w_spatial, m_a_b_spatial, m_a_w_out, m_kv_norm_g, m_w_k, m_w_v, m_b_w_q, m_b_w_o, m_ffn_w_up, m_ffn_conv_w, m_ffn_conv_b, m_ffn_w_down, v_pre_mix_g, v_post_mix_g, v_pre_ffn_g, v_post_ffn_g, v_a_w_in, v_a_v_norm_g, v_a_w_spatial, v_a_b_spatial, v_a_w_out, v_kv_norm_g, v_w_k, v_w_v, v_b_w_q, v_b_w_o, v_ffn_w_up, v_ffn_conv_w, v_ffn_conv_b, v_ffn_w_down):
    xi, yi, ci = lax.axis_index("x"), lax.axis_index("y"), lax.axis_index("c")
    chip = 2 * xi + yi
    c_idx = jnp.reshape(ci, (1,)).astype(jnp.int32)
    _, s, d = x.shape
    n_layers = pre_mix_g.shape[0]
    assert n_layers == 2 and a_w_in.shape[0] == 1 and b_w_q.shape[0] == 1
    d_a = a_w_out.shape[1] * N_CHIPS
    n_g = a_w_spatial.shape[1]
    ns = ffn_w_up.shape[2]
    assert a_w_spatial.shape[2] == TILE and d_a == n_g * TILE and s % TILE == 0
    h0 = x[0]
    target = loss_target[0]

    big = {
        "win": (a_w_in, m_a_w_in, v_a_w_in),
        "wout": (a_w_out, m_a_w_out, v_a_w_out),
        "wk": (w_k[None], m_w_k[None], v_w_k[None]),
        "wv": (w_v[None], m_w_v[None], v_w_v[None]),
        "wq": (b_w_q, m_b_w_q, v_b_w_q),
        "wo": (b_w_o, m_b_w_o, v_b_w_o),
        "wup": (ffn_w_up, m_ffn_w_up, v_ffn_w_up),
        "wdn": (ffn_w_down, m_ffn_w_down, v_ffn_w_down),
    }
    units = [(nm, layer) for nm in big for layer in range(big[nm][0].shape[0])]
    chip_idx = jnp.reshape(chip, (1,)).astype(jnp.int32)
    shards = [_cast_bf16(big[nm][0], layer, chip_idx, f"cast_{nm}{layer}") for nm, layer in units]
    small_sharded = _pack([a_v_norm_g, ffn_conv_w])
    small_sharded = lax.dynamic_update_index_in_dim(
        jnp.zeros((N_CHIPS,) + small_sharded.shape, F32), small_sharded, chip, 0)
    own = dict(zip(units, shards))
    full = {}

    def gather_ride(keys):
        return _ride_gather([own[key] for key in keys])

    def gathered(keys, rode):
        full.update(zip(keys, rode[0]))

    first_keys = [("win", 0)]
    (first_bufs, _), = _run_rides([_ride_gather([own[key] for key in first_keys] + [small_sharded])], "gather_first")
    full.update(zip(first_keys, first_bufs[:-1]))
    vg_parts, cw_parts = _unpack(first_bufs[-1], [a_v_norm_g.shape, ffn_conv_w.shape], lead=(N_CHIPS,))
    v_g = jnp.transpose(vg_parts, (1, 0, 2)).reshape(1, d_a)

    def rows(nm, layer=0):
        w = full[(nm, layer)]
        return w.reshape(w.shape[0] * w.shape[1], w.shape[2])

    gains = lambda g, layer: g[layer:layer + 1]
    bias = jnp.repeat(a_b_spatial[0].T, TILE, axis=1)
    w_s = a_w_spatial[0]
    kv_g = kv_norm_g[None]
    conv_w = [cw_parts[:, layer] for layer in range(n_layers)]
    conv_b = [ffn_conv_b[layer].reshape(N_CHIPS, 1, ns) for layer in range(n_layers)]

    up0 = own[("wup", 0)]
    pieces = lambda p, span: _ride_gather([up0], part=p, n_parts=8, span=span)
    hn0 = _rms_fwd(h0, gains(pre_mix_g, 0), "norm_in")
    uv, ((out_bufs, _), ((up0,), _)) = _mm(
        hn0, full[("win", 0)], "nn", "gmlp_in", out_split=N_CHIPS, rides=[gather_ride([("wout", 0)]), pieces(0, 1)])
    full[("wout", 0)] = out_bufs[0]
    gm, (((up0,), _),) = _gmlp_fwd(uv, v_g, w_s, bias, "gmlp_gate", rides=[pieces(1, 2)])
    mix0, (((up0,), _),) = _mm(gm, rows("wout"), "nn", "gmlp_out", rides=[pieces(3, 2)])
    mix0 = mix0[0]
    (h1, hn1), (((up0,), _),) = _resid_rms(
        h0, mix0, gains(post_mix_g, 0), [gains(pre_ffn_g, 0)], "resid_mix0", rides=[pieces(5, 3)])
    full[("wup", 0)] = up0
    def leg(keys, stage):
        return _ride_gather([own[key] for key in keys], stage=stage)

    def first_leg_done(keys, rode):
        own.update(zip(keys, rode[0]))

    down0, qk, vo = [("wdn", 0)], [("wq", 0), ("wk", 0)], [("wv", 0), ("wo", 0)]
    a0, (rode,) = _mm(hn1, full[("wup", 0)], "nn", "ffn_up0", out_split=N_CHIPS, rides=[leg(down0, "ici")])
    first_leg_done(down0, rode)
    hm0, (rode, rode_qk) = _ffn_act_fwd(
        a0, conv_w[0], conv_b[0], "ffn_act0", rides=[leg(down0, "d2d"), leg(qk, "ici")])
    gathered(down0, rode)
    first_leg_done(qk, rode_qk)
    f0, (rode, rode_vo) = _mm(hm0, rows("wdn", 0), "nn", "ffn_down0", rides=[leg(qk, "d2d"), leg(vo, "ici")])
    gathered(qk, rode)
    first_leg_done(vo, rode_vo)
    f0 = f0[0]
    (h2, hn2, kvn), (rode,) = _resid_rms(
        h1, f0, gains(post_ffn_g, 0), [gains(pre_mix_g, 1), kv_g], "resid_ffn0", rides=[leg(vo, "d2d")])
    gathered(vo, rode)
    q = _mm(hn2, rows("wq"), "nn", "proj_q", out_dtype=BF16)[0]
    k = _mm(kvn, rows("wk"), "nn", "proj_k", out_dtype=BF16)[0]
    v = _mm(kvn, rows("wv"), "nn", "proj_v", out_dtype=BF16)[0]
    last_keys = [("wup", 1), ("wdn", 1)]
    (att, lsum), (rode,) = _attn_fwd(q, k, v, "attn_fwd", rides=[leg(last_keys, "ici")])
    first_leg_done(last_keys, rode)
    mix1, (rode,) = _mm(att, rows("wo"), "nn", "proj_o", rides=[leg(last_keys, "d2d")])
    gathered(last_keys, rode)
    mix1 = mix1[0]
    h3, hn3 = _resid_rms(h2, mix1, gains(post_mix_g, 1), [gains(pre_ffn_g, 1)], "resid_mix1")
    a1 = _mm(hn3, full[("wup", 1)], "nn", "ffn_up1", out_split=N_CHIPS)
    hm1 = _ffn_act_fwd(a1, conv_w[1], conv_b[1], "ffn_act1")
    f1 = _mm(hm1, rows("wdn", 1), "nn", "ffn_down1")[0]
    dh4, loss_tile = _loss_head(h3, f1, gains(post_ffn_g, 1), target, "loss_head")
    loss = lax.psum(loss_tile[0, 0], ("x", "y", "c"))

    dw = {}
    dg = {}

    pair = {}
    half_done = {nm: None for nm in big}

    def swap_ride(keys):
        return _ride_swap([dw[key] for key in keys])

    def swapped(keys, rode):
        for (nm, layer), got in zip(keys, rode[1]):
            pair[(nm, layer)] = _pair_add(dw[(nm, layer)], got, c_idx, f"pair_add_{nm}{layer}")

    def scatter_ride(keys):
        return _ride_scatter([pair[key] for key in keys])

    def scattered(keys, rode):
        for (nm, layer), got in zip(keys, rode[1]):
            half_done[nm] = _chip_sum(got, half_done[nm], big[nm][0].shape, layer, c_idx, f"chip_sum_{nm}{layer}")

    def ffn_bwd(dh_out, h_in, hn, a, hm, f, layer, act_rides=()):
        df, dg[("post_ffn", layer)] = _rms_bwd_out(dh_out, f, gains(post_ffn_g, layer), f"d_norm_ffn_out{layer}")
        dwd = _mm(hm, df, "tn", f"d_w_down{layer}", out_dtype=BF16)[0]
        down, up = [("wdn", layer)], [("wup", layer)]
        dw[down[0]] = dwd.reshape(N_CHIPS, dwd.shape[0] // N_CHIPS, d)
        dhm, (rode,) = _mm(df, rows("wdn", layer), "nt", f"d_ffn_mid{layer}", out_split=2, rides=[swap_ride(down)])
        swapped(down, rode)
        (da, dg[("conv_w", layer)], dg[("conv_b", layer)]), act_rode = _ffn_act_bwd(
            a, dhm, conv_w[layer], conv_b[layer], f"d_ffn_act{layer}", rides=act_rides)
        dw[up[0]], (rode,) = _mm(hn, da, "tn", f"d_w_up{layer}", out_dtype=BF16, out_split=N_CHIPS,
                                 rides=[scatter_ride(down)])
        scattered(down, rode)
        dhn, (rode,) = _mm(da, full[("wup", layer)], "nt", f"d_ffn_in{layer}", rides=[swap_ride(up)])
        swapped(up, rode)
        return dhn[0], act_rode

    dhn3, _ = ffn_bwd(dh4, h3, hn3, a1, hm1, f1, 1)
    dh3, (dg[("pre_ffn", 1)],) = _rms_bwd_in(dh4, h3, [([dhn3], gains(pre_ffn_g, 1))], "d_norm_ffn_in1")
    dmix1, dg[("post_mix", 1)] = _rms_bwd_out(dh3, mix1, gains(post_mix_g, 1), "d_norm_mix_out1")
    dwo = _mm(att, dmix1, "tn", "d_w_o", out_dtype=BF16)[0]
    dw[("wo", 0)] = dwo.reshape(N_CHIPS, dwo.shape[0] // N_CHIPS, d)
    datt = _mm(dmix1, rows("wo"), "nt", "d_attn_out", out_dtype=BF16)[0]
    ffn1_keys = [("wup", 1)]
    (dq, dk, dv), (rode,) = _attn_bwd(q, k, v, datt, lsum, "attn_bwd", rides=[scatter_ride(ffn1_keys)])
    scattered(ffn1_keys, rode)
    for nm, act, dact in (("wq", hn2, dq), ("wk", kvn, dk), ("wv", kvn, dv)):
        g = _mm(act, dact, "tn", f"d_{nm}", out_dtype=BF16)[0]
        dw[(nm, 0)] = g.reshape(N_CHIPS, g.shape[0] // N_CHIPS, g.shape[1])
    dhn2 = _mm(dq, rows("wq"), "nt", "d_q_in")[0]
    dkvn_k = _mm(dk, rows("wk"), "nt", "d_k_in")[0]
    attn_keys = [("wo", 0), ("wq", 0), ("wk", 0), ("wv", 0)]
    dkvn_v, (rode,) = _mm(dv, rows("wv"), "nt", "d_v_in", rides=[swap_ride(attn_keys)])
    swapped(attn_keys, rode)
    dh2, (dg[("pre_mix", 1)], dg["kv"]) = _rms_bwd_in(
        dh3, h2, [([dhn2], gains(pre_mix_g, 1)), ([dkvn_k, dkvn_v[0]], kv_g)], "d_norm_mix_in1")
    dhn1, (rode,) = ffn_bwd(dh2, h1, hn1, a0, hm0, f0, 0, act_rides=[scatter_ride(attn_keys)])
    scattered(attn_keys, rode)
    up0_pair = [pair[("wup", 0)]]
    up0_landed = [None]

    def up0_piece(part, span):
        return _ride_scatter(up0_pair, part, 8, into=up0_landed[0], span=span)

    def up0_rode(rode):
        up0_landed[0] = rode[1] if up0_landed[0] is None else rode[0]

    dh1, (dg[("pre_ffn", 0)],), (rode,) = _rms_bwd_in(
        dh2, h1, [([dhn1], gains(pre_ffn_g, 0))], "d_norm_ffn_in0", rides=[up0_piece(0, 1)])
    up0_rode(rode)
    dmix0, dg[("post_mix", 0)], (rode,) = _rms_bwd_out(
        dh1, mix0, gains(post_mix_g, 0), "d_norm_mix_out0", rides=[up0_piece(1, 1)])
    up0_rode(rode)
    early = ["wq", "wk", "wv", "wo", "wdn"]
    dwout, (((joined_early, _)),) = _mm(
        gm, dmix0, "tn", "d_w_out", out_dtype=BF16, rides=[_ride_join([half_done[nm] for nm in early])])
    grads_big = dict(zip(early, joined_early))
    w_out_key, w_in_key = [("wout", 0)], [("win", 0)]
    dw[w_out_key[0]] = dwout[0].reshape(N_CHIPS, dwout.shape[1] // N_CHIPS, d)
    dgm, (rode, up0) = _mm(dmix0, rows("wout"), "nt", "d_gmlp_gate", rides=[swap_ride(w_out_key), up0_piece(2, 1)])
    swapped(w_out_key, rode)
    up0_rode(up0)
    (duv, d_ws, d_bs, d_vg), (rode,) = _gmlp_bwd(uv, dgm[0], v_g, w_s, bias, "d_gmlp", rides=[up0_piece(3, 2)])
    up0_rode(rode)
    dw[w_in_key[0]], (rode, up0) = _mm(
        hn0, duv, "tn", "d_w_in", out_dtype=BF16, out_split=N_CHIPS, rides=[scatter_ride(w_out_key), up0_piece(5, 1)])
    scattered(w_out_key, rode)
    up0_rode(up0)
    dhn0, (rode, up0) = _mm(
        duv, full[("win", 0)], "nt", "d_gmlp_in", rides=[swap_ride(w_in_key), up0_piece(6, 2)])
    swapped(w_in_key, rode)
    up0_rode(up0)
    scattered([("wup", 0)], (None, up0_landed[0]))
    dx, (dg[("pre_mix", 0)],), (rode,) = _rms_bwd_in(
        dh1, h0, [([dhn0[0]], gains(pre_mix_g, 0))], "d_norm_in", rides=[scatter_ride(w_in_key)])
    scattered(w_in_key, rode)

    stack = lambda key: jnp.concatenate([dg[(key, layer)] for layer in range(n_layers)], axis=0)
    small_parts = [
        stack("pre_mix"), stack("post_mix"), stack("pre_ffn"), stack("post_ffn"),
        d_vg, d_ws, d_bs[::SUBLANE], dg["kv"],
        jnp.stack([dg[("conv_w", layer)] for layer in range(n_layers)]),
        jnp.stack([dg[("conv_b", layer)] for layer in range(n_layers)]),
    ]
    late = [nm for nm in big if nm not in early]
    summed, ((joined_late, _),) = _all_reduce_small(
        _pack(small_parts), "small_grads_sum", rides=[_ride_join([half_done[nm] for nm in late])])
    grads_big.update(zip(late, joined_late))
    (g_pre_mix, g_post_mix, g_pre_ffn, g_post_ffn, g_vg, g_ws, g_bs, g_kv, g_cw, g_cb) = _unpack(
        summed, [p.shape for p in small_parts])
    g_vg = lax.dynamic_index_in_dim(g_vg.reshape(N_CHIPS, 1, d_a // N_CHIPS), chip, 0, keepdims=False)
    g_cw = lax.dynamic_index_in_dim(g_cw, chip, 1, keepdims=False)
    g_cb = g_cb.reshape(n_layers, N_CHIPS * ns)
    small = [
        (pre_mix_g, g_pre_mix, m_pre_mix_g, v_pre_mix_g),
        (post_mix_g, g_post_mix, m_post_mix_g, v_post_mix_g),
        (pre_ffn_g, g_pre_ffn, m_pre_ffn_g, v_pre_ffn_g),
        (post_ffn_g, g_post_ffn, m_post_ffn_g, v_post_ffn_g),
        (a_v_norm_g, g_vg, m_a_v_norm_g, v_a_v_norm_g),
        (a_w_spatial, g_ws[None], m_a_w_spatial, v_a_w_spatial),
        (a_b_spatial, g_bs[None], m_a_b_spatial, v_a_b_spatial),
        (kv_norm_g, g_kv.reshape(d), m_kv_norm_g, v_kv_norm_g),
        (ffn_conv_w, g_cw, m_ffn_conv_w, v_ffn_conv_w),
        (ffn_conv_b, g_cb, m_ffn_conv_b, v_ffn_conv_b),
    ]
    small = [(w, g.reshape(w.shape), m, v) for w, g, m, v in small]
    packed = [_pack([t[i] for t in small])[None] for i in range(4)]
    small_new = [_unpack(p[0], [t[0].shape for t in small]) for p in _adamw(*packed, "adamw_small")]

    new_big = {nm: _adamw(big[nm][0], grads_big[nm], big[nm][1], big[nm][2], f"adamw_{nm}", pass_g=True)
               for nm in big}

    def big_out(nm, which):
        ref_shape = {"wk": w_k.shape, "wv": w_v.shape}.get(nm, big[nm][0].shape)
        return new_big[nm][which].reshape(ref_shape)

    order = ["pre_mix", "post_mix", "pre_ffn", "post_ffn", "win", "vg", "ws", "bs", "wout", "kv", "wk", "wv", "wq",
             "wo", "wup", "cw", "cb", "wdn"]
    small_at = {"pre_mix": 0, "post_mix": 1, "pre_ffn": 2, "post_ffn": 3, "vg": 4, "ws": 5, "bs": 6, "kv": 7,
                "cw": 8, "cb": 9}
    outs = [loss, dx[None]]
    for which in range(4):
        for nm in order:
            if nm in small_at:
                outs.append(small[small_at[nm]][1] if which == 0 else small_new[which - 1][small_at[nm]])
            else:
                outs.append(big_out(nm, which))
    return tuple(outs)
```

```python
import functools
import math

import jax
import jax.numpy as jnp
from jax import lax
from jax.experimental import pallas as pl
from jax.experimental.pallas import tpu as pltpu

F32 = jnp.float32
BF16 = jnp.bfloat16
EPS = 1e-6
ADAM_LR = 0.001
ADAM_B1 = 0.9
ADAM_B2 = 0.999
ADAM_EPS = 1e-08
ADAM_WD = 0.01
ADAM_STEP = 10

LANE = 128
SUBLANE = 8
ROWS = 16
TILE = 128
N_CHIPS = 4
N_DEV = 8
VMEM_LIMIT = 56 * 1024 * 1024
MM_VMEM = 46 * 1024 * 1024
MXU_WIDTH = 256
MESH = pl.DeviceIdType.MESH
ANY = pl.BlockSpec(memory_space=pl.ANY)
VMEM_SPEC = pl.BlockSpec(memory_space=pltpu.VMEM)


def _cp(*sem):
    return pltpu.CompilerParams(dimension_semantics=sem, vmem_limit_bytes=VMEM_LIMIT)


def _pick(dim, pref, align=LANE):
    if dim <= pref:
        return dim
    best = None
    for d in range(align, pref + 1, align):
        if dim % d == 0:
            best = d
    assert best is not None, (dim, pref)
    return best


_DIMS = {
    "nn": (((1,), (0,)), ((), ())),
    "nt": (((1,), (1,)), ((), ())),
    "tn": (((0,), (0,)), ((), ())),
}


def _as3(a):
    return a if a.ndim == 3 else a[None]


def _spec3(br, bc, cols_j, rc):
    per = cols_j // bc

    def imap(m, n, k):
        r, c = rc(m, n, k)
        return (c // per, r, c % per)

    return pl.BlockSpec((None, br, bc), imap)


def _mm(a, b, mode, name, out_dtype=F32, out_split=1, rides=()):
    a, b = _as3(a), _as3(b)
    ja, ra, caj = a.shape
    jb, rb, cbj = b.shape
    if mode == "nn":
        m, k, n = ra, ja * caj, jb * cbj
        assert rb == k
        m_ext, k_ext, n_ext = [ra], [caj, rb], [cbj]
    elif mode == "nt":
        m, k, n = ra, ja * caj, rb
        assert jb * cbj == k
        m_ext, k_ext, n_ext = [ra], [caj, cbj], [rb]
    else:
        m, k, n = ja * caj, ra, jb * cbj
        assert rb == k
        m_ext, k_ext, n_ext = [caj], [ra], [cbj]
    assert n % out_split == 0
    n_ext.append(n // out_split)
    bm = _pick(math.gcd(*m_ext), 1536)
    n_unit = math.gcd(*n_ext)
    bn = _pick(n_unit, 1536)
    k_unit = math.gcd(*k_ext)
    o_bytes = jnp.dtype(out_dtype).itemsize

    def vmem_need(bm, bn, bk):
        tiles = bm * bk * a.dtype.itemsize + bk * bn * b.dtype.itemsize + bm * bn * o_bytes
        return 2 * tiles + bm * bn * 4 * (2 if bk < k else 1)

    def deepest(bm, bn):
        return max(d for d in range(LANE, k_unit + 1, LANE)
                   if k_unit % d == 0 and (d == LANE or vmem_need(bm, bn, d) <= MM_VMEM))

    bk = deepest(bm, bn)
    if bk < k_unit and k_unit == k:
        if bm % (2 * LANE) == 0 and deepest(bm // 2, bn) == k:
            bm, bk = bm // 2, k
        elif bn % (2 * LANE) == 0 and deepest(bm, bn // 2) == k:
            bn, bk = bn // 2, k
    n_outer = False
    if bn % MXU_WIDTH and n_unit % MXU_WIDTH == 0 and k_unit == k:
        for rows in (bm, bm // 2, bm // 4):
            if rows % LANE == 0 and vmem_need(rows, n_unit, k) <= MM_VMEM:
                bm, bn, bk, n_outer = rows, n_unit, k, True
                break
        else:
            if k % (2 * LANE) == 0 and vmem_need(bm, n_unit, k // 2) <= MM_VMEM:
                bn, bk, n_outer = n_unit, k // 2, True
    nk = k // bk
    order = (lambda f: lambda ni, mi, ki: f(mi, ni, ki)) if n_outer else (lambda f: f)
    if mode == "nn":
        a_spec = _spec3(bm, bk, caj, order(lambda mi, ni, ki: (mi, ki)))
        b_spec = _spec3(bk, bn, cbj, order(lambda mi, ni, ki: (ki, ni)))
    elif mode == "nt":
        a_spec = _spec3(bm, bk, caj, order(lambda mi, ni, ki: (mi, ki)))
        b_spec = _spec3(bn, bk, cbj, order(lambda mi, ni, ki: (ni, ki)))
    else:
        a_spec = _spec3(bk, bm, caj, order(lambda mi, ni, ki: (ki, mi)))
        b_spec = _spec3(bk, bn, cbj, order(lambda mi, ni, ki: (ki, ni)))
    o_spec = _spec3(bm, bn, n // out_split, order(lambda mi, ni, ki: (mi, ni)))
    dims = _DIMS[mode]

    def body(a_ref, b_ref, o_ref, *acc):
        def part():
            return lax.dot_general(a_ref[...].astype(BF16), b_ref[...].astype(BF16), dims, preferred_element_type=F32)

        if nk == 1:
            o_ref[...] = part().astype(o_ref.dtype)
            return
        acc_ref, = acc
        ki = pl.program_id(2)

        @pl.when(ki == 0)
        def _():
            acc_ref[...] = part()

        @pl.when(jnp.logical_and(ki > 0, ki < nk - 1))
        def _():
            acc_ref[...] += part()

        @pl.when(ki == nk - 1)
        def _():
            o_ref[...] = (acc_ref[...] + part()).astype(o_ref.dtype)

    grid = (n // bn, m // bm, nk) if n_outer else (m // bm, n // bn, nk)
    out, rode = _hosted_call(
        body, [a, b], name=name, grid=grid, in_specs=[a_spec, b_spec], out_specs=o_spec,
        out_shape=jax.ShapeDtypeStruct((out_split, m, n // out_split), out_dtype),
        scratch_shapes=[pltpu.VMEM((bm, bn), F32)] if nk > 1 else [],
        semantics=("parallel", "parallel", "arbitrary"), rides=rides)
    return (out, rode) if rides else out


def _rms(x, g):
    r = lax.rsqrt(jnp.mean(x * x, axis=-1, keepdims=True) + EPS)
    return x * r * g


def _rms_bwd(x, g, dy):
    r = lax.rsqrt(jnp.mean(x * x, axis=-1, keepdims=True) + EPS)
    xh = x * r
    gy = dy * g
    dx = r * (gy - xh * jnp.mean(gy * xh, axis=-1, keepdims=True))
    return dx, jnp.sum(dy * xh, axis=0, keepdims=True)


def _row_block(s, streams):
    return _pick(s, 512 if streams <= 4 else 256, ROWS)


def _rms_fwd(h, g, name, rides=()):
    s, d = h.shape
    br = _row_block(s, 2)

    def body(h_ref, g_ref, o_ref):
        o_ref[...] = _rms(h_ref[...], g_ref[...]).astype(BF16)

    row = pl.BlockSpec((br, d), lambda i: (i, 0))
    vec = pl.BlockSpec((1, d), lambda i: (0, 0))
    out, rode = _hosted_call(
        body, [h, g], name=name, grid=(s // br,), in_specs=[row, vec], out_specs=row,
        out_shape=jax.ShapeDtypeStruct((s, d), BF16), semantics=("parallel",), rides=rides)
    return (out, rode) if rides else out


def _resid_rms(h_in, f, g_post, g_next, name, rides=()):
    s, d = h_in.shape
    br = _row_block(s, 3 + (len(g_next) + 1) // 2)
    n_next = len(g_next)

    def body(h_ref, f_ref, gp_ref, *refs):
        gn_refs, ho_ref, hn_refs = refs[:n_next], refs[n_next], refs[n_next + 1:]
        h = h_ref[...] + _rms(f_ref[...], gp_ref[...])
        ho_ref[...] = h
        for gn_ref, hn_ref in zip(gn_refs, hn_refs):
            hn_ref[...] = _rms(h, gn_ref[...]).astype(BF16)

    row = pl.BlockSpec((br, d), lambda i: (i, 0))
    vec = pl.BlockSpec((1, d), lambda i: (0, 0))
    outs, rode = _hosted_call(
        body, [h_in, f, g_post, *g_next], name=name, grid=(s // br,),
        in_specs=[row, row, vec] + [vec] * n_next,
        out_specs=[row] * (1 + n_next),
        out_shape=[jax.ShapeDtypeStruct((s, d), F32)] + [jax.ShapeDtypeStruct((s, d), BF16)] * n_next,
        semantics=("parallel",), rides=rides)
    return (outs, rode) if rides else outs


def _loss_head(h_in, f, g_post, target, name):
    s, d = h_in.shape
    br = _row_block(s, 4)

    def body(h_ref, f_ref, gp_ref, t_ref, dh_ref, loss_ref):
        @pl.when(pl.program_id(0) == 0)
        def _():
            loss_ref[...] = jnp.zeros_like(loss_ref)

        diff = h_ref[...] + _rms(f_ref[...], gp_ref[...]) - t_ref[...]
        dh_ref[...] = diff * (1.0 / d)
        loss_ref[...] += 0.5 * jnp.sum(jnp.mean(diff * diff, axis=-1, keepdims=True))

    row = pl.BlockSpec((br, d), lambda i: (i, 0))
    vec = pl.BlockSpec((1, d), lambda i: (0, 0))
    return pl.pallas_call(
        body, name=name, grid=(s // br,),
        in_specs=[row, row, vec, row],
        out_specs=[row, pl.BlockSpec((SUBLANE, LANE), lambda i: (0, 0))],
        out_shape=[jax.ShapeDtypeStruct((s, d), F32), jax.ShapeDtypeStruct((SUBLANE, LANE), F32)],
        compiler_params=_cp("arbitrary"),
    )(h_in, f, g_post, target)


def _rms_bwd_out(dy, f, g, name, rides=()):
    s, d = f.shape
    br = _row_block(s, 3)

    def body(dy_ref, f_ref, g_ref, df_ref, dg_ref):
        @pl.when(pl.program_id(0) == 0)
        def _():
            dg_ref[...] = jnp.zeros_like(dg_ref)

        dx, dg = _rms_bwd(f_ref[...], g_ref[...], dy_ref[...])
        df_ref[...] = dx.astype(BF16)
        dg_ref[...] += dg

    row = pl.BlockSpec((br, d), lambda i: (i, 0))
    vec = pl.BlockSpec((1, d), lambda i: (0, 0))
    (df, dg), rode = _hosted_call(
        body, [dy, f, g], name=name, grid=(s // br,), in_specs=[row, row, vec], out_specs=[row, vec],
        out_shape=[jax.ShapeDtypeStruct((s, d), BF16), jax.ShapeDtypeStruct((1, d), F32)],
        semantics=("arbitrary",), rides=rides)
    return (df, dg, rode) if rides else (df, dg)


def _rms_bwd_in(dh_out, h_in, branches, name, rides=()):
    s, d = h_in.shape
    br = _row_block(s, 3 + sum(len(ds) for ds, _ in branches))
    counts = [len(ds) for ds, _ in branches]
    n_d = sum(counts)
    n_b = len(branches)

    def body(dho_ref, h_ref, *refs):
        d_refs, g_refs = refs[:n_d], refs[n_d:n_d + n_b]
        dh_ref, dg_refs = refs[n_d + n_b], refs[n_d + n_b + 1:]

        @pl.when(pl.program_id(0) == 0)
        def _():
            for r in dg_refs:
                r[...] = jnp.zeros_like(r)

        h = h_ref[...]
        acc = dho_ref[...]
        at = 0
        for bi, cnt in enumerate(counts):
            dn = d_refs[at][...]
            for r in d_refs[at + 1:at + cnt]:
                dn = dn + r[...]
            at += cnt
            dx, dg = _rms_bwd(h, g_refs[bi][...], dn)
            acc = acc + dx
            dg_refs[bi][...] += dg
        dh_ref[...] = acc

    row = pl.BlockSpec((br, d), lambda i: (i, 0))
    vec = pl.BlockSpec((1, d), lambda i: (0, 0))
    flat_d = [x for ds, _ in branches for x in ds]
    outs, rode = _hosted_call(
        body, [dh_out, h_in, *flat_d, *[g for _, g in branches]], name=name, grid=(s // br,),
        in_specs=[row, row] + [row] * n_d + [vec] * n_b,
        out_specs=[row] + [vec] * n_b,
        out_shape=[jax.ShapeDtypeStruct((s, d), F32)] + [jax.ShapeDtypeStruct((1, d), F32)] * n_b,
        semantics=("arbitrary",), rides=rides)
    return (outs[0], list(outs[1:]), rode) if rides else (outs[0], list(outs[1:]))


def _split3(x):
    x0 = x.astype(BF16)
    r1 = x - x0.astype(F32)
    x1 = r1.astype(BF16)
    x2 = (r1 - x1.astype(F32)).astype(BF16)
    return x0, x1, x2


def _tri(n, kind):
    r = lax.broadcasted_iota(jnp.int32, (n, n), 0)
    c = lax.broadcasted_iota(jnp.int32, (n, n), 1)
    m = {"lt": r < c, "le": r <= c, "gt": r > c}[kind]
    return jnp.where(m, 1.0, 0.0).astype(BF16)


_GELU_C = math.sqrt(2.0 / math.pi)
_GELU_A = 0.044715


def _gelu(x):
    return 0.5 * x * (1.0 + jnp.tanh(_GELU_C * (x + _GELU_A * (x * x * x))))


def _gelu_grad(x):
    t = jnp.tanh(_GELU_C * (x + _GELU_A * (x * x * x)))
    return 0.5 * (1.0 + t) + 0.5 * x * (1.0 - t * t) * (_GELU_C * (1.0 + 3.0 * _GELU_A * (x * x)))


def _causal_w(w):
    r = lax.broadcasted_iota(jnp.int32, (TILE, TILE), 0)
    c = lax.broadcasted_iota(jnp.int32, (TILE, TILE), 1)
    return jnp.where(c <= r, w, 0.0)


def _uv_tiles(uv_ref, g, d_a, dq):
    cu, cv = g * TILE, d_a + g * TILE
    u = uv_ref[cu // dq, :, pl.ds(cu % dq, TILE)]
    v = uv_ref[cv // dq, :, pl.ds(cv % dq, TILE)]
    return u, v


def _gmlp_fwd(uv, v_g, w_s, bias, name, rides=()):
    _, s, dq = uv.shape
    d_a = 2 * dq
    n_g = d_a // TILE

    def body(uv_ref, vg_ref, ws_ref, b_ref, o_ref):
        for g in range(n_g):
            up, vp = _uv_tiles(uv_ref, g, d_a, dq)
            cols = pl.ds(g * TILE, TILE)
            vn = _rms(_gelu(vp), vg_ref[:, cols])
            mixed = jnp.dot(_causal_w(ws_ref[g]).astype(BF16), vn.astype(BF16), preferred_element_type=F32) + b_ref[:, cols]
            o_ref[:, cols] = (_gelu(up) * mixed).astype(BF16)

    return _hosted_call(
        body, [uv, v_g, w_s, bias], name=name, grid=(s // TILE,),
        in_specs=[
            pl.BlockSpec((4, TILE, dq), lambda i: (0, i, 0)),
            pl.BlockSpec((1, d_a), lambda i: (0, 0)),
            pl.BlockSpec((n_g, TILE, TILE), lambda i: (0, 0, 0)),
            pl.BlockSpec((TILE, d_a), lambda i: (0, 0)),
        ],
        out_specs=pl.BlockSpec((TILE, d_a), lambda i: (i, 0)),
        out_shape=jax.ShapeDtypeStruct((s, d_a), BF16),
        semantics=("parallel",), rides=rides)


def _gmlp_bwd(uv, dgm, v_g, w_s, bias, name, rides=()):
    _, s, dq = uv.shape
    d_a = 2 * dq
    n_g = d_a // TILE
    n_c = s // TILE

    def body(uv_ref, d_ref, vg_ref, ws_ref, b_ref, duv_ref, dws_ref, dbs_ref, dvg_ref, dbias_acc):
        i = pl.program_id(0)

        @pl.when(i == 0)
        def _():
            dws_ref[...] = jnp.zeros_like(dws_ref)
            dvg_ref[...] = jnp.zeros_like(dvg_ref)
            dbias_acc[...] = jnp.zeros_like(dbias_acc)

        for g in range(n_g):
            up, vp = _uv_tiles(uv_ref, g, d_a, dq)
            cols = pl.ds(g * TILE, TILE)
            vg = vg_ref[:, cols]
            u = _gelu(up)
            v = _gelu(vp)
            r = lax.rsqrt(jnp.mean(v * v, axis=-1, keepdims=True) + EPS)
            vh = v * r
            vn = (vh * vg).astype(BF16)
            wc = _causal_w(ws_ref[g]).astype(BF16)
            mixed = jnp.dot(wc, vn, preferred_element_type=F32) + b_ref[:, cols]
            d_out = d_ref[:, cols]
            du = d_out * mixed
            dmixed = d_out * u
            dmb = dmixed.astype(BF16)
            dvn = lax.dot_general(wc, dmb, _DIMS["tn"], preferred_element_type=F32)
            dws_ref[g] += lax.dot_general(dmb, vn, _DIMS["nt"], preferred_element_type=F32)
            dbias_acc[:, cols] += dmixed
            dvg_ref[:, cols] += jnp.sum(dvn * vh, axis=0, keepdims=True)
            gv = dvn * vg
            dv = r * (gv - vh * jnp.mean(gv * vh, axis=-1, keepdims=True))
            cu, cv = g * TILE, d_a + g * TILE
            duv_ref[cu // dq, :, pl.ds(cu % dq, TILE)] = (du * _gelu_grad(up)).astype(BF16)
            duv_ref[cv // dq, :, pl.ds(cv % dq, TILE)] = (dv * _gelu_grad(vp)).astype(BF16)

        @pl.when(i == n_c - 1)
        def _():
            ones = jnp.ones((SUBLANE, TILE), BF16)
            for g in range(n_g):
                dws_ref[g] = _causal_w(dws_ref[g])
                cols = pl.ds(g * TILE, TILE)
                out = None
                for t in _split3(dbias_acc[:, cols]):
                    p = lax.dot_general(ones, t, _DIMS["nt"], preferred_element_type=F32)
                    out = p if out is None else out + p
                dbs_ref[pl.ds(g * SUBLANE, SUBLANE), :] = out

    return _hosted_call(
        body, [uv, dgm, v_g, w_s, bias], name=name, grid=(n_c,), semantics=("arbitrary",), rides=rides,
        in_specs=[
            pl.BlockSpec((4, TILE, dq), lambda i: (0, i, 0)),
            pl.BlockSpec((TILE, d_a), lambda i: (i, 0)),
            pl.BlockSpec((1, d_a), lambda i: (0, 0)),
            pl.BlockSpec((n_g, TILE, TILE), lambda i: (0, 0, 0)),
            pl.BlockSpec((TILE, d_a), lambda i: (0, 0)),
        ],
        out_specs=[
            pl.BlockSpec((4, TILE, dq), lambda i: (0, i, 0)),
            pl.BlockSpec((n_g, TILE, TILE), lambda i: (0, 0, 0)),
            pl.BlockSpec((n_g * SUBLANE, TILE), lambda i: (0, 0)),
            pl.BlockSpec((1, d_a), lambda i: (0, 0)),
        ],
        out_shape=[
            jax.ShapeDtypeStruct((4, s, dq), BF16),
            jax.ShapeDtypeStruct((n_g, TILE, TILE), F32),
            jax.ShapeDtypeStruct((n_g * SUBLANE, TILE), F32),
            jax.ShapeDtypeStruct((1, d_a), F32),
        ],
        scratch_shapes=[pltpu.VMEM((TILE, d_a), F32)])


def _sigmoid(x):
    return 1.0 / (1.0 + jnp.exp(-x))


def _conv3(ext, w, b):
    return b + ((w[0:1] * pltpu.roll(ext, 2, 0) + w[1:2] * pltpu.roll(ext, 1, 0)) + w[2:3] * ext)


def _act_blocks(s, ns):
    return _pick(s, 512, ROWS), _pick(ns, 256)


def _ffn_act_fwd(a, cw, cb, name, rides=()):
    _, s, ns = a.shape
    bs, cb_w = _act_blocks(s, ns)
    hb = bs // SUBLANE

    def body(a_ref, prev_ref, cw_ref, cb_ref, o_ref):
        first = pl.program_id(0) == 0

        def conv(comp):
            prev = jnp.where(first, 0.0, prev_ref[comp])
            ext = jnp.concatenate([prev, a_ref[comp]], axis=0)
            return _conv3(ext, cw_ref[comp], cb_ref[comp])[SUBLANE:]

        for p in range(2):
            cg = conv(p)
            o_ref[p] = (cg * _sigmoid(cg) * conv(2 + p)).astype(BF16)

    hm, rode = _hosted_call(
        body, [a, a, cw, cb], name=name, grid=(s // bs, ns // cb_w),
        in_specs=[
            pl.BlockSpec((4, bs, cb_w), lambda i, j: (0, i, j)),
            pl.BlockSpec((4, SUBLANE, cb_w), lambda i, j: (0, jnp.maximum(i * hb - 1, 0), j)),
            pl.BlockSpec((4, 3, cb_w), lambda i, j: (0, 0, j)),
            pl.BlockSpec((4, 1, cb_w), lambda i, j: (0, 0, j)),
        ],
        out_specs=pl.BlockSpec((2, bs, cb_w), lambda i, j: (0, i, j)),
        out_shape=jax.ShapeDtypeStruct((2, s, ns), BF16),
        semantics=("parallel", "parallel"), rides=rides)
    return (hm, rode) if rides else hm


def _ffn_act_bwd(a, dhm, cw, cb, name, rides=()):
    _, s, ns = a.shape
    bs, cb_w = _act_blocks(s, ns)
    hb = bs // SUBLANE
    n_i = s // bs
    n_ext = bs + 2 * SUBLANE
    cur = slice(SUBLANE, SUBLANE + bs)

    def body(a_ref, prev_ref, next_ref, d_ref, dnext_ref, cw_ref, cb_ref, da_ref, dcw_ref, dcb_ref):
        i = pl.program_id(1)
        first, last = i == 0, i == n_i - 1

        @pl.when(first)
        def _():
            dcw_ref[...] = jnp.zeros_like(dcw_ref)
            dcb_ref[...] = jnp.zeros_like(dcb_ref)

        def ext_of(comp):
            return jnp.concatenate([jnp.where(first, 0.0, prev_ref[comp]), a_ref[comp], next_ref[comp]], axis=0)

        def back(comp, a_ext, dc):
            w = cw_ref[comp]
            da = (w[2:3] * dc + w[1:2] * pltpu.roll(dc, n_ext - 1, 0)) + w[0:1] * pltpu.roll(dc, n_ext - 2, 0)
            da_ref[comp] = da[cur].astype(BF16)
            dcc = dc[cur]
            dcw_ref[comp, 0:1, :] += jnp.sum(dcc * pltpu.roll(a_ext, 2, 0)[cur], axis=0, keepdims=True)
            dcw_ref[comp, 1:2, :] += jnp.sum(dcc * pltpu.roll(a_ext, 1, 0)[cur], axis=0, keepdims=True)
            dcw_ref[comp, 2:3, :] += jnp.sum(dcc * a_ext[cur], axis=0, keepdims=True)
            dcb_ref[comp] += jnp.sum(dcc, axis=0, keepdims=True)

        for p in range(2):
            ag, av = ext_of(p), ext_of(2 + p)
            cg = _conv3(ag, cw_ref[p], cb_ref[p])
            cv = _conv3(av, cw_ref[2 + p], cb_ref[2 + p])
            d = jnp.concatenate(
                [jnp.zeros((SUBLANE, cb_w), F32), d_ref[p], jnp.where(last, 0.0, dnext_ref[p])], axis=0)
            sg = _sigmoid(cg)
            back(2 + p, av, d * (cg * sg))
            back(p, ag, d * cv * (sg * (1.0 + cg * (1.0 - sg))))

    return _hosted_call(
        body, [a, a, a, dhm, dhm, cw, cb], name=name, grid=(ns // cb_w, n_i),
        in_specs=[
            pl.BlockSpec((4, bs, cb_w), lambda j, i: (0, i, j)),
            pl.BlockSpec((4, SUBLANE, cb_w), lambda j, i: (0, jnp.maximum(i * hb - 1, 0), j)),
            pl.BlockSpec((4, SUBLANE, cb_w), lambda j, i: (0, jnp.minimum((i + 1) * hb, n_i * hb - 1), j)),
            pl.BlockSpec((2, bs, cb_w), lambda j, i: (0, i, j)),
            pl.BlockSpec((2, SUBLANE, cb_w), lambda j, i: (0, jnp.minimum((i + 1) * hb, n_i * hb - 1), j)),
            pl.BlockSpec((4, 3, cb_w), lambda j, i: (0, 0, j)),
            pl.BlockSpec((4, 1, cb_w), lambda j, i: (0, 0, j)),
        ],
        out_specs=[
            pl.BlockSpec((4, bs, cb_w), lambda j, i: (0, i, j)),
            pl.BlockSpec((4, 3, cb_w), lambda j, i: (0, 0, j)),
            pl.BlockSpec((4, 1, cb_w), lambda j, i: (0, 0, j)),
        ],
        out_shape=[
            jax.ShapeDtypeStruct((4, s, ns), BF16),
            jax.ShapeDtypeStruct((4, 3, ns), F32),
            jax.ShapeDtypeStruct((4, 1, ns), F32),
        ],
        semantics=("parallel", "arbitrary"), rides=rides)


ATT_BQ_FWD = 2048
ATT_BQ_BWD = 1024
ATT_BK = 256
ATT_UNROLL = 2
ATT_UNROLL_BWD = 4


def _att_blocks(s, bq_pref):
    bq = _pick(s, bq_pref)
    bk = min(ATT_BK, bq)
    assert bq % bk == 0
    return bq, bk


def _dot_sel2(x, sel):
    hi = x.astype(BF16)
    lo = (x - hi.astype(F32)).astype(BF16)
    n = x.shape[0]
    both = jnp.dot(jnp.concatenate([hi, lo], axis=0), sel, preferred_element_type=F32)
    return both[:n] + both[n:]


def _causal_mask(bq, bk, row0, col0):
    rows = row0 + lax.broadcasted_iota(jnp.int32, (bq, bk), 0)
    cols = col0 + lax.broadcasted_iota(jnp.int32, (bq, bk), 1)
    return cols < rows


def _sb_tile(qb, kb, scale, mask):
    z = lax.dot_general(qb, kb, _DIMS["nt"], preferred_element_type=F32) * scale
    e = jnp.exp(-jnp.abs(z))
    lb = jnp.minimum(z, 0.0) - jnp.log(1.0 + e)
    l1m = lb - z
    if mask is not None:
        l1m = jnp.where(mask, l1m, 0.0)
    return z, e, lb, l1m


def _attn_fwd(q, k, v, name, rides=()):
    s, hd = q.shape
    bq, bk = _att_blocks(s, ATT_BQ_FWD)
    r = bq // bk
    unroll = math.gcd(r, ATT_UNROLL)
    n_h, n_q = hd // TILE, s // bq
    scale = 1.0 / math.sqrt(TILE)

    def body(q_ref, k_ref, v_ref, o_ref, l_ref, acc_ref, suf_ref):
        i = pl.program_id(1)
        qb = q_ref[...]
        later = _tri(bk, "gt")
        acc_ref[...] = jnp.zeros_like(acc_ref)
        suf_ref[...] = jnp.zeros_like(suf_ref)

        def tile(j, row0):
            rows = pl.ds(pl.multiple_of(j * bk, bk), bk)
            masked = row0 is not None
            r0 = row0 if masked else 0
            rs = pl.ds(r0, bq - r0)
            mask = _causal_mask(bq - r0, bk, i * bq + r0, j * bk) if masked else None
            _, _, lb, l1m = _sb_tile(qb[r0:], k_ref[rows, :], scale, mask)
            a = jnp.exp(lb + _dot_sel2(l1m, later) + suf_ref[rs, :])
            if masked:
                a = jnp.where(mask, a, 0.0)
            acc_ref[rs, :] += jnp.dot(a.astype(BF16), v_ref[rows, :], preferred_element_type=F32)
            suf_ref[rs, :] += jnp.sum(l1m, axis=1, keepdims=True)

        for dgl in range(r - 1, -1, -1):
            tile(r * i + dgl, dgl * bk)

        def step(t, carry):
            for u in range(unroll):
                tile(r * i - 1 - (unroll * t + u), None)
            return carry

        lax.fori_loop(0, (r * i) // unroll, step, 0)
        o_ref[...] = acc_ref[...].astype(BF16)
        l_ref[...] = jnp.broadcast_to(suf_ref[...], (bq, TILE))

    blk = pl.BlockSpec((bq, TILE), lambda h, i: (i, h))
    head = pl.BlockSpec((s, TILE), lambda h, i: (0, h))
    return _hosted_call(
        body, [q, k, v], name=name, grid=(n_h, n_q), in_specs=[blk, head, head], out_specs=[blk, blk],
        out_shape=[jax.ShapeDtypeStruct((s, hd), BF16), jax.ShapeDtypeStruct((s, hd), F32)],
        scratch_shapes=[pltpu.VMEM((bq, TILE), F32), pltpu.VMEM((bq, 1), F32)],
        semantics=("parallel", "parallel"), rides=rides)


def _attn_bwd(q, k, v, do, lsum, name, rides=()):
    s, hd = q.shape
    bq, bk = _att_blocks(s, ATT_BQ_BWD)
    r = bq // bk
    unroll = math.gcd(r, ATT_UNROLL_BWD)
    n_h, n_q = hd // TILE, s // bq
    scale = 1.0 / math.sqrt(TILE)

    def body(q_ref, k_ref, v_ref, do_ref, l_ref, dq_ref, dk_ref, dv_ref, dq_acc, pre_ref, cp_ref):
        i = pl.program_id(1)

        @pl.when(i == 0)
        def _():
            dk_ref[...] = jnp.zeros_like(dk_ref)
            dv_ref[...] = jnp.zeros_like(dv_ref)

        qb = q_ref[...]
        dob = do_ref[...]
        upto = _tri(bk, "le")
        before = _tri(bk, "lt")
        dq_acc[...] = jnp.zeros_like(dq_acc)
        pre_ref[...] = jnp.zeros_like(pre_ref)
        cp_ref[...] = jnp.zeros_like(cp_ref)

        def tile(j, row0):
            rows = pl.ds(pl.multiple_of(j * bk, bk), bk)
            kb, vb = k_ref[rows, :], v_ref[rows, :]
            masked = row0 is not None
            r0 = row0 if masked else 0
            rs = pl.ds(r0, bq - r0)
            qs, dos = qb[r0:], dob[r0:]
            mask = _causal_mask(bq - r0, bk, i * bq + r0, j * bk) if masked else None
            z, e, lb, l1m = _sb_tile(qs, kb, scale, mask)
            suffix = (l_ref[rs, 0:1] - pre_ref[rs, :]) - _dot_sel2(l1m, upto)
            a = jnp.exp(lb + suffix)
            if masked:
                a = jnp.where(mask, a, 0.0)
            p = a * lax.dot_general(dos, vb, _DIMS["nt"], preferred_element_type=F32)
            both = p + (cp_ref[rs, :] + jnp.dot(p.astype(BF16), before, preferred_element_type=F32))
            sg = jnp.where(z >= 0.0, 1.0, e) * pl.reciprocal(1.0 + e, approx=True)
            dz = p - both * sg
            if masked:
                dz = jnp.where(mask, dz, 0.0)
            dz = (dz * scale).astype(BF16)
            dq_acc[rs, :] += jnp.dot(dz, kb, preferred_element_type=F32)
            dk_ref[rows, :] += lax.dot_general(dz, qs, _DIMS["tn"], preferred_element_type=F32)
            dv_ref[rows, :] += lax.dot_general(a.astype(BF16), dos, _DIMS["tn"], preferred_element_type=F32)
            pre_ref[rs, :] += jnp.sum(l1m, axis=1, keepdims=True)
            cp_ref[rs, :] += jnp.sum(p, axis=1, keepdims=True)

        def step(j, carry):
            for u in range(unroll):
                tile(unroll * j + u, None)
            return carry

        lax.fori_loop(0, (r * i) // unroll, step, 0)
        for dgl in range(r):
            tile(r * i + dgl, dgl * bk)
        dq_ref[...] = dq_acc[...].astype(BF16)

    blk = pl.BlockSpec((bq, TILE), lambda h, i: (i, h))
    head = pl.BlockSpec((s, TILE), lambda h, i: (0, h))
    return _hosted_call(
        body, [q, k, v, do, lsum], name=name, grid=(n_h, n_q), in_specs=[blk, head, head, blk, blk],
        out_specs=[blk, head, head],
        out_shape=[jax.ShapeDtypeStruct((s, hd), BF16), jax.ShapeDtypeStruct((s, hd), F32),
                   jax.ShapeDtypeStruct((s, hd), F32)],
        scratch_shapes=[pltpu.VMEM((bq, TILE), F32), pltpu.VMEM((bq, 1), F32), pltpu.VMEM((bq, 1), F32)],
        semantics=("parallel", "arbitrary"), rides=rides)


EW_BLOCK = 512 * 1024


def _ew_blocks(r, c, elems=EW_BLOCK):
    return _pick(r, max(ROWS, elems // c // ROWS * ROWS), ROWS), c


def _cast_bf16(w, layer, chip_idx, name):
    _, r, c = w.shape
    br, bc = _ew_blocks(r, c)

    def body(chip_ref, w_ref, o_ref):
        o_ref[...] = w_ref[...].astype(BF16)

    return pl.pallas_call(
        body, name=name,
        grid_spec=pltpu.PrefetchScalarGridSpec(
            num_scalar_prefetch=1, grid=(r // br, c // bc),
            in_specs=[pl.BlockSpec((None, br, bc), lambda i, j, chip_ref: (layer, i, j))],
            out_specs=pl.BlockSpec((None, br, bc), lambda i, j, chip_ref: (chip_ref[0], i, j)),
        ),
        out_shape=jax.ShapeDtypeStruct((N_CHIPS, r, c), BF16), compiler_params=_cp("parallel", "parallel"),
    )(chip_idx, w)


def _pair_add(dw, recv, c_idx, name):
    _, r, c = dw.shape
    hr = r // 2
    br, bc = _ew_blocks(hr, c)
    nb = hr // br

    def body(c_ref, a_ref, b_ref, o_ref):
        o_ref[...] = (a_ref[...].astype(F32) + b_ref[...].astype(F32)).astype(BF16)

    return pl.pallas_call(
        body, name=name,
        grid_spec=pltpu.PrefetchScalarGridSpec(
            num_scalar_prefetch=1, grid=(N_CHIPS, nb, c // bc),
            in_specs=[
                pl.BlockSpec((None, br, bc), lambda s, i, j, c_ref: (s, c_ref[0] * nb + i, j)),
                pl.BlockSpec((None, br, bc), lambda s, i, j, c_ref: (s, i, j)),
            ],
            out_specs=pl.BlockSpec((None, br, bc), lambda s, i, j, c_ref: (s, i, j)),
        ),
        out_shape=jax.ShapeDtypeStruct((N_CHIPS, hr, c), BF16),
        compiler_params=_cp("parallel", "parallel", "parallel"),
    )(c_idx, dw, recv)


def _chip_sum(parts, dest, shape, layer, c_idx, name):
    _, hr, c = parts.shape
    br, bc = _ew_blocks(hr, c, EW_BLOCK // 2)
    nb = hr // br

    def body(c_ref, p_ref, *refs):
        o_ref = refs[-1]
        acc = p_ref[0].astype(F32)
        for s in range(1, N_CHIPS):
            acc = acc + p_ref[s].astype(F32)
        o_ref[...] = acc

    in_specs = [pl.BlockSpec((N_CHIPS, br, bc), lambda i, j, c_ref: (0, i, j))]
    operands = [c_idx, parts]
    aliases = {}
    if dest is not None:
        in_specs.append(ANY)
        operands.append(dest)
        aliases = {2: 0}
    return pl.pallas_call(
        body, name=name,
        grid_spec=pltpu.PrefetchScalarGridSpec(
            num_scalar_prefetch=1, grid=(nb, c // bc), in_specs=in_specs,
            out_specs=pl.BlockSpec((None, br, bc), lambda i, j, c_ref: (layer, c_ref[0] * nb + i, j)),
        ),
        out_shape=jax.ShapeDtypeStruct(shape, F32), input_output_aliases=aliases,
        compiler_params=_cp("parallel", "parallel"),
    )(*operands)


def _adamw(w, g, m, v, name, pass_g=False):
    n_l, r, c = w.shape
    br, bc = _ew_blocks(r, c, EW_BLOCK // 2)

    def body(w_ref, g_ref, m_ref, v_ref, *out_refs):
        d_ref, mo_ref, vo_ref = out_refs[-3:]
        g = g_ref[...]
        if pass_g:
            out_refs[0][...] = g
        m = ADAM_B1 * m_ref[...] + (1.0 - ADAM_B1) * g
        v = ADAM_B2 * v_ref[...] + (1.0 - ADAM_B2) * (g * g)
        m_hat = m / (1.0 - ADAM_B1 ** ADAM_STEP)
        v_hat = v / (1.0 - ADAM_B2 ** ADAM_STEP)
        d_ref[...] = -ADAM_LR * (m_hat / (jnp.sqrt(v_hat) + ADAM_EPS) + ADAM_WD * w_ref[...])
        mo_ref[...] = m
        vo_ref[...] = v

    blk = pl.BlockSpec((None, br, bc), lambda l, i, j: (l, i, j))
    n_out = 4 if pass_g else 3
    return pl.pallas_call(
        body, name=name, grid=(n_l, r // br, c // bc), in_specs=[blk] * 4, out_specs=[blk] * n_out,
        out_shape=[jax.ShapeDtypeStruct(w.shape, F32)] * n_out,
        compiler_params=_cp("parallel", "parallel", "parallel"),
    )(w, g, m, v)


def _place():
    x, y, c = lax.axis_index("x"), lax.axis_index("y"), lax.axis_index("c")
    chips = [(1 - x, y), (x, 1 - y), (1 - x, 1 - y)]
    return x, y, c, chips


class _Ride:
    def __init__(self, reads, bufs, new, n_sems, start, finish):
        self.reads, self.bufs, self.new, self.n_sems, self.start, self.finish = reads, bufs, new, n_sems, start, finish


def _hosted_call(body, operands, *, name, grid, in_specs, out_specs, out_shape, scratch_shapes=(), semantics=(), rides=()):
    single = not isinstance(out_shape, (list, tuple))
    out_specs = [out_specs] if single else list(out_specs)
    out_shape = [out_shape] if single else list(out_shape)
    in_specs, scratch_shapes = list(in_specs), list(scratch_shapes)
    n_in, n_out, n_scr = len(in_specs), len(out_shape), len(scratch_shapes)
    extra_in, extra_out, aliases, where = [], [], {}, []
    for ride in rides:
        r0 = len(extra_in)
        extra_in += list(ride.reads)
        b0 = len(extra_in)
        extra_in += list(ride.bufs)
        ob0 = len(extra_out)
        extra_out += [jax.ShapeDtypeStruct(b.shape, b.dtype) for b in ride.bufs]
        for t in range(len(ride.bufs)):
            aliases[n_in + b0 + t] = n_out + ob0 + t
        on0 = len(extra_out)
        extra_out += list(ride.new)
        where.append((r0, len(ride.reads), ob0, len(ride.bufs), on0, len(ride.new)))
    n_ein, n_eout = len(extra_in), len(extra_out)
    sem_shapes = [pltpu.SemaphoreType.DMA((max(1, k),)) for ride in rides for k in ride.n_sems]

    def full_body(*refs):
        ins, outs, scr = refs[:n_in + n_ein], refs[n_in + n_ein:n_in + n_ein + n_out + n_eout], refs[n_in + n_ein + n_out + n_eout:]

        def run(which):
            for idx, (ride, (r0, nr, ob0, nb, on0, nn)) in enumerate(zip(rides, where)):
                fn = ride.start if which == 0 else ride.finish
                fn(ins[n_in + r0:n_in + r0 + nr], outs[n_out + ob0:n_out + ob0 + nb], outs[n_out + on0:n_out + on0 + nn],
                   *scr[n_scr + 3 * idx:n_scr + 3 * idx + 3])

        host = lambda: body(*ins[:n_in], *outs[:n_out], *scr[:n_scr])
        if not rides:
            host()
        elif not grid:
            run(0)
            host()
            run(1)
        else:
            ids = [pl.program_id(ax) for ax in range(len(grid))]
            first = functools.reduce(jnp.logical_and, [i == 0 for i in ids])
            last = functools.reduce(jnp.logical_and, [i == g - 1 for i, g in zip(ids, grid)])
            pl.when(first)(lambda: run(0))
            host()
            pl.when(last)(lambda: run(1))

    if rides:
        params = pltpu.CompilerParams(dimension_semantics=("arbitrary",) * len(grid), vmem_limit_bytes=VMEM_LIMIT)
    else:
        params = _cp(*semantics)
    outs = pl.pallas_call(
        full_body, name=name, grid=grid,
        in_specs=in_specs + [ANY] * n_ein, out_specs=out_specs + [ANY] * n_eout,
        out_shape=out_shape + extra_out, input_output_aliases=aliases,
        scratch_shapes=scratch_shapes + sem_shapes, compiler_params=params,
    )(*operands, *extra_in)
    main = outs[0] if single else list(outs[:n_out])
    rode = [(list(outs[n_out + ob0:n_out + ob0 + nb]), list(outs[n_out + on0:n_out + on0 + nn]))
            for (_, _, ob0, nb, on0, nn) in where]
    return main, rode


def _run_rides(rides, name):
    return _hosted_call(lambda: None, [], name=name, grid=(), in_specs=[], out_specs=[], out_shape=[], rides=rides)[1]


def _ride_gather(slots, part=0, n_parts=1, span=1, stage=None):
    n = len(slots)
    halves = [a.shape[1] // 2 for a in slots]
    sizes = [hr // n_parts for hr in halves]
    assert part + span <= n_parts
    for a, hr, size in zip(slots, halves, sizes):
        assert a.shape[1] == 2 * hr and hr == size * n_parts and size % ROWS == 0, a.shape

    def remote(bufs, send_sems, recv_sems, i, k, slot, core, to):
        rows = bufs[i].at[slot, pl.ds(pl.multiple_of(core * halves[i] + part * sizes[i], ROWS), span * sizes[i])]
        return pltpu.make_async_remote_copy(
            src_ref=rows, dst_ref=rows, send_sem=send_sems.at[i * 6 + k], recv_sem=recv_sems.at[i * 6 + k],
            device_id=to, device_id_type=MESH)

    def each(fn):
        x, y, c, chips = _place()
        for i in range(n):
            for k, (px, py) in enumerate(chips):
                fn(x, y, c, i, k, px, py)

    def start(reads, bufs, new, send_sems, recv_sems, local_sems):
        cp = functools.partial(remote, bufs, send_sems, recv_sems)
        if stage != "d2d":
            each(lambda x, y, c, i, k, px, py: cp(i, k, 2 * x + y, c, (px, py, c)).start())
        else:
            each(lambda x, y, c, i, k, px, py: cp(i, 3 + k, 2 * px + py, c, (x, y, 1 - c)).start())

    def finish(reads, bufs, new, send_sems, recv_sems, local_sems):
        cp = functools.partial(remote, bufs, send_sems, recv_sems)

        def landed_over_ici(x, y, c, i, k, px, py):
            cp(i, k, 2 * px + py, c, (x, y, c)).wait_recv()
            if stage is None:
                cp(i, 3 + k, 2 * px + py, c, (x, y, 1 - c)).start()

        if stage != "d2d":
            each(landed_over_ici)
        if stage != "ici":
            each(lambda x, y, c, i, k, px, py: cp(i, 3 + k, 2 * px + py, 1 - c, (x, y, c)).wait_recv())
        if stage != "d2d":
            each(lambda x, y, c, i, k, px, py: cp(i, k, 2 * x + y, c, (px, py, c)).wait_send())
        if stage != "ici":
            each(lambda x, y, c, i, k, px, py: cp(i, 3 + k, 2 * px + py, c, (x, y, 1 - c)).wait_send())

    return _Ride([], slots, [], (6 * n, 6 * n, 0), start, finish)


def _ride_swap(grads):
    n = len(grads)
    halves = [a.shape[1] // 2 for a in grads]

    def copies(reads, new, send_sems, recv_sems):
        x, y, c, _ = _place()
        out = []
        for i in range(n):
            rows = pl.ds(pl.multiple_of((1 - c) * halves[i], 2 * SUBLANE), halves[i])
            out.append(pltpu.make_async_remote_copy(
                src_ref=reads[i].at[:, rows, :], dst_ref=new[i], send_sem=send_sems.at[i], recv_sem=recv_sems.at[i],
                device_id=(x, y, 1 - c), device_id_type=MESH))
        return out

    def start(reads, bufs, new, send_sems, recv_sems, local_sems):
        for cp in copies(reads, new, send_sems, recv_sems):
            cp.start()

    def finish(reads, bufs, new, send_sems, recv_sems, local_sems):
        for cp in copies(reads, new, send_sems, recv_sems):
            cp.wait()

    shapes = [jax.ShapeDtypeStruct((N_CHIPS, hr, a.shape[2]), a.dtype) for a, hr in zip(grads, halves)]
    return _Ride(grads, [], shapes, (n, n, 0), start, finish)


def _ride_scatter(parts, part=0, n_parts=1, into=None, span=1):
    n = len(parts)
    sizes = [a.shape[1] // n_parts for a in parts]
    assert part + span <= n_parts
    for a, size in zip(parts, sizes):
        assert a.shape[1] == size * n_parts and size % ROWS == 0, a.shape

    def piece(ref, i, slot):
        return ref.at[slot, pl.ds(part * sizes[i], span * sizes[i])]

    def own(reads, land, local_sems, i):
        me = 2 * lax.axis_index("x") + lax.axis_index("y")
        return pltpu.make_async_copy(piece(reads[i], i, me), piece(land[i], i, me), local_sems.at[i])

    def send(reads, land, send_sems, recv_sems, i, k):
        x, y, c, chips = _place()
        px, py = chips[k]
        return pltpu.make_async_remote_copy(
            src_ref=piece(reads[i], i, 2 * px + py), dst_ref=piece(land[i], i, 2 * x + y),
            send_sem=send_sems.at[3 * i + k], recv_sem=recv_sems.at[3 * i + k],
            device_id=(px, py, c), device_id_type=MESH)

    def start(reads, bufs, new, send_sems, recv_sems, local_sems):
        land = new if into is None else bufs
        for i in range(n):
            own(reads, land, local_sems, i).start()
            for k in range(3):
                send(reads, land, send_sems, recv_sems, i, k).start()

    def finish(reads, bufs, new, send_sems, recv_sems, local_sems):
        land = new if into is None else bufs
        x, y, c, chips = _place()
        for i in range(n):
            for k, (px, py) in enumerate(chips):
                slot = piece(land[i], i, 2 * px + py)
                pltpu.make_async_remote_copy(
                    src_ref=slot, dst_ref=slot, send_sem=send_sems.at[3 * i + k], recv_sem=recv_sems.at[3 * i + k],
                    device_id=(x, y, c), device_id_type=MESH).wait_recv()
        for i in range(n):
            for k in range(3):
                send(reads, land, send_sems, recv_sems, i, k).wait_send()
            own(reads, land, local_sems, i).wait()

    shapes = [jax.ShapeDtypeStruct(a.shape, a.dtype) for a in parts]
    if into is None:
        return _Ride(parts, [], shapes, (3 * n, 3 * n, n), start, finish)
    return _Ride(parts, list(into), [], (3 * n, 3 * n, n), start, finish)


def _ride_join(grads):
    n = len(grads)

    def copy(bufs, send_sems, recv_sems, i, core, to):
        hr = grads[i].shape[1] // 2
        rows = bufs[i].at[:, pl.ds(pl.multiple_of(core * hr, SUBLANE), hr), :]
        return pltpu.make_async_remote_copy(
            src_ref=rows, dst_ref=rows, send_sem=send_sems.at[i], recv_sem=recv_sems.at[i],
            device_id=to, device_id_type=MESH)

    def start(reads, bufs, new, send_sems, recv_sems, local_sems):
        x, y, c, _ = _place()
        for i in range(n):
            copy(bufs, send_sems, recv_sems, i, c, (x, y, 1 - c)).start()

    def finish(reads, bufs, new, send_sems, recv_sems, local_sems):
        x, y, c, _ = _place()
        for i in range(n):
            copy(bufs, send_sems, recv_sems, i, 1 - c, (x, y, c)).wait_recv()
        for i in range(n):
            copy(bufs, send_sems, recv_sems, i, c, (x, y, 1 - c)).wait_send()

    return _Ride([], grads, [], (n, n, 0), start, finish)


def _all_reduce_small(packed, name, rides=()):
    r, c = packed.shape
    chunk = _pick(r, 256, ROWS)

    def body(x_ref, out_ref, gath, send_sems, recv_sems, local_sem):
        x, y, cc, chips = _place()
        me, sibling = (x, y, cc), (x, y, 1 - cc)

        def slot(px, py, pc):
            return gath.at[4 * px + 2 * py + pc]

        def copy(k, block, to, src=None):
            return pltpu.make_async_remote_copy(
                src_ref=slot(*block) if src is None else src, dst_ref=slot(*block),
                send_sem=send_sems.at[k], recv_sem=recv_sems.at[k], device_id=to, device_id_type=MESH)

        mine = pltpu.make_async_copy(x_ref, slot(*me), local_sem)
        mine.start()
        first = [copy(0, me, sibling, src=x_ref)]
        first += [copy(1 + j, me, (*chip, cc), src=x_ref) for j, chip in enumerate(chips)]
        for cp in first:
            cp.start()
        passed = [copy(4 + j, (*chip, cc), sibling) for j, chip in enumerate(chips)]
        for j, chip in enumerate(chips):
            copy(1 + j, (*chip, cc), me).wait_recv()
            passed[j].start()
        copy(0, sibling, me).wait_recv()
        for j, chip in enumerate(chips):
            copy(4 + j, (*chip, 1 - cc), me).wait_recv()
        for cp in first + passed:
            cp.wait_send()
        mine.wait()

        def add(i, carry):
            rows = pl.ds(pl.multiple_of(i * chunk, SUBLANE), chunk)
            acc = gath[0, rows, :]
            for dev in range(1, N_DEV):
                acc = acc + gath[dev, rows, :]
            out_ref[rows, :] = acc
            return carry

        lax.fori_loop(0, r // chunk, add, 0)

    return _hosted_call(
        body, [packed], name=name, grid=(), in_specs=[VMEM_SPEC], out_specs=VMEM_SPEC,
        out_shape=jax.ShapeDtypeStruct((r, c), F32),
        scratch_shapes=[pltpu.VMEM((N_DEV, r, c), F32), pltpu.SemaphoreType.DMA((7,)),
                        pltpu.SemaphoreType.DMA((7,)), pltpu.SemaphoreType.DMA],
        rides=rides)


_PACK_ROWS = 256


def _pack(arrays):
    flat = jnp.concatenate([a.reshape(-1).astype(F32) for a in arrays])
    unit = _PACK_ROWS * LANE
    total = -(-flat.shape[0] // unit) * unit
    return jnp.pad(flat, (0, total - flat.shape[0])).reshape(-1, LANE)


def _unpack(packed, shapes, lead=()):
    flat = packed.reshape(lead + (-1,))
    out, at = [], 0
    for s in shapes:
        size = math.prod(s)
        out.append(flat[..., at:at + size].reshape(lead + tuple(s)))
        at += size
    return out


def kernel(x, pre_mix_g, post_mix_g, pre_ffn_g, post_ffn_g, a_w_in, a_v_norm_g, a_w_spatial, a_b_spatial, a_w_out, kv_norm_g, w_k, w_v, b_w_q, b_w_o, ffn_w_up, ffn_conv_w, ffn_conv_b, ffn_w_down, loss_target, m_pre_mix_g, m_post_mix_g, m_pre_ffn_g, m_post_ffn_g, m_a_w_in, m_a_v_norm_g, m_a_w_spatial, m_a_b_spatial, m_a_w_out, m_kv_norm_g, m_w_k, m_w_v, m_b_w_q, m_b_w_o, m_ffn_w_up, m_ffn_conv_w, m_ffn_conv_b, m_ffn_w_down, v_pre_mix_g, v_post_mix_g, v_pre_ffn_g, v_post_ffn_g, v_a_w_in, v_a_v_norm_g, v_a_w_spatial, v_a_b_spatial, v_a_w_out, v_kv_norm_g, v_w_k, v_w_v, v_b_w_q, v_b_w_o, v_ffn_w_up, v_ffn_conv_w, v_ffn_conv_b, v_ffn_w_down):
    xi, yi, ci = lax.axis_index("x"), lax.axis_index("y"), lax.axis_index("c")
    chip = 2 * xi + yi
    c_idx = jnp.reshape(ci, (1,)).astype(jnp.int32)
    _, s, d = x.shape
    n_layers = pre_mix_g.shape[0]
    assert n_layers == 2 and a_w_in.shape[0] == 1 and b_w_q.shape[0] == 1
    d_a = a_w_out.shape[1] * N_CHIPS
    n_g = a_w_spatial.shape[1]
    ns = ffn_w_up.shape[2]
    assert a_w_spatial.shape[2] == TILE and d_a == n_g * TILE and s % TILE == 0
    h0 = x[0]
    target = loss_target[0]

    big = {
        "win": (a_w_in, m_a_w_in, v_a_w_in),
        "wout": (a_w_out, m_a_w_out, v_a_w_out),
        "wk": (w_k[None], m_w_k[None], v_w_k[None]),
        "wv": (w_v[None], m_w_v[None], v_w_v[None]),
        "wq": (b_w_q, m_b_w_q, v_b_w_q),
        "wo": (b_w_o, m_b_w_o, v_b_w_o),
        "wup": (ffn_w_up, m_ffn_w_up, v_ffn_w_up),
        "wdn": (ffn_w_down, m_ffn_w_down, v_ffn_w_down),
    }
    units = [(nm, layer) for nm in big for layer in range(big[nm][0].shape[0])]
    chip_idx = jnp.reshape(chip, (1,)).astype(jnp.int32)
    shards = [_cast_bf16(big[nm][0], layer, chip_idx, f"cast_{nm}{layer}") for nm, layer in units]
    small_sharded = _pack([a_v_norm_g, ffn_conv_w])
    small_sharded = lax.dynamic_update_index_in_dim(
        jnp.zeros((N_CHIPS,) + small_sharded.shape, F32), small_sharded, chip, 0)
    own = dict(zip(units, shards))
    full = {}

    def gather_ride(keys):
        return _ride_gather([own[key] for key in keys])

    def gathered(keys, rode):
        full.update(zip(keys, rode[0]))

    gains = lambda g, layer: g[layer:layer + 1]
    first_keys = [("win", 0)]
    (first_bufs, _), = _run_rides(
        [_ride_gather([own[key] for key in first_keys] + [small_sharded], stage="ici")], "gather_first")
    hn0, ((first_bufs, _),) = _rms_fwd(
        h0, gains(pre_mix_g, 0), "norm_in", rides=[_ride_gather(first_bufs, stage="d2d")])
    full.update(zip(first_keys, first_bufs[:-1]))
    vg_parts, cw_parts = _unpack(first_bufs[-1], [a_v_norm_g.shape, ffn_conv_w.shape], lead=(N_CHIPS,))
    v_g = jnp.transpose(vg_parts, (1, 0, 2)).reshape(1, d_a)

    def rows(nm, layer=0):
        w = full[(nm, layer)]
        return w.reshape(w.shape[0] * w.shape[1], w.shape[2])

    bias = jnp.repeat(a_b_spatial[0].T, TILE, axis=1)
    w_s = a_w_spatial[0]
    kv_g = kv_norm_g[None]
    conv_w = [cw_parts[:, layer] for layer in range(n_layers)]
    conv_b = [ffn_conv_b[layer].reshape(N_CHIPS, 1, ns) for layer in range(n_layers)]

    up0 = own[("wup", 0)]
    pieces = lambda p, span: _ride_gather([up0], part=p, n_parts=8, span=span)
    uv, ((out_bufs, _), ((up0,), _)) = _mm(
        hn0, full[("win", 0)], "nn", "gmlp_in", out_split=N_CHIPS, rides=[gather_ride([("wout", 0)]), pieces(0, 1)])
    full[("wout", 0)] = out_bufs[0]
    gm, (((up0,), _),) = _gmlp_fwd(uv, v_g, w_s, bias, "gmlp_gate", rides=[pieces(1, 2)])
    mix0, (((up0,), _),) = _mm(gm, rows("wout"), "nn", "gmlp_out", rides=[pieces(3, 2)])
    mix0 = mix0[0]
    (h1, hn1), (((up0,), _),) = _resid_rms(
        h0, mix0, gains(post_mix_g, 0), [gains(pre_ffn_g, 0)], "resid_mix0", rides=[pieces(5, 3)])
    full[("wup", 0)] = up0
    def leg(keys, stage):
        return _ride_gather([own[key] for key in keys], stage=stage)

    def first_leg_done(keys, rode):
        own.update(zip(keys, rode[0]))

    down0, qk, vo = [("wdn", 0)], [("wq", 0), ("wk", 0)], [("wv", 0), ("wo", 0)]
    a0, (rode,) = _mm(hn1, full[("wup", 0)], "nn", "ffn_up0", out_split=N_CHIPS, rides=[leg(down0, "ici")])
    first_leg_done(down0, rode)
    hm0, (rode, rode_qk) = _ffn_act_fwd(
        a0, conv_w[0], conv_b[0], "ffn_act0", rides=[leg(down0, "d2d"), leg(qk, "ici")])
    gathered(down0, rode)
    first_leg_done(qk, rode_qk)
    f0, (rode, rode_vo) = _mm(hm0, rows("wdn", 0), "nn", "ffn_down0", rides=[leg(qk, "d2d"), leg(vo, "ici")])
    gathered(qk, rode)
    first_leg_done(vo, rode_vo)
    f0 = f0[0]
    (h2, hn2, kvn), (rode,) = _resid_rms(
        h1, f0, gains(post_ffn_g, 0), [gains(pre_mix_g, 1), kv_g], "resid_ffn0", rides=[leg(vo, "d2d")])
    gathered(vo, rode)
    q = _mm(hn2, rows("wq"), "nn", "proj_q", out_dtype=BF16)[0]
    k = _mm(kvn, rows("wk"), "nn", "proj_k", out_dtype=BF16)[0]
    v = _mm(kvn, rows("wv"), "nn", "proj_v", out_dtype=BF16)[0]
    last_keys = [("wup", 1), ("wdn", 1)]
    (att, lsum), (rode,) = _attn_fwd(q, k, v, "attn_fwd", rides=[leg(last_keys, "ici")])
    first_leg_done(last_keys, rode)
    mix1, (rode,) = _mm(att, rows("wo"), "nn", "proj_o", rides=[leg(last_keys, "d2d")])
    gathered(last_keys, rode)
    mix1 = mix1[0]
    h3, hn3 = _resid_rms(h2, mix1, gains(post_mix_g, 1), [gains(pre_ffn_g, 1)], "resid_mix1")
    a1 = _mm(hn3, full[("wup", 1)], "nn", "ffn_up1", out_split=N_CHIPS)
    hm1 = _ffn_act_fwd(a1, conv_w[1], conv_b[1], "ffn_act1")
    f1 = _mm(hm1, rows("wdn", 1), "nn", "ffn_down1")[0]
    dh4, loss_tile = _loss_head(h3, f1, gains(post_ffn_g, 1), target, "loss_head")
    loss = lax.psum(loss_tile[0, 0], ("x", "y", "c"))

    dw = {}
    dg = {}

    pair = {}
    half_done = {nm: None for nm in big}

    def swap_ride(keys):
        return _ride_swap([dw[key] for key in keys])

    def swapped(keys, rode):
        for (nm, layer), got in zip(keys, rode[1]):
            pair[(nm, layer)] = _pair_add(dw[(nm, layer)], got, c_idx, f"pair_add_{nm}{layer}")

    def scatter_ride(keys):
        return _ride_scatter([pair[key] for key in keys])

    def scattered(keys, rode):
        for (nm, layer), got in zip(keys, rode[1]):
            half_done[nm] = _chip_sum(got, half_done[nm], big[nm][0].shape, layer, c_idx, f"chip_sum_{nm}{layer}")

    def ffn_bwd(dh_out, h_in, hn, a, hm, f, layer, act_rides=()):
        df, dg[("post_ffn", layer)] = _rms_bwd_out(dh_out, f, gains(post_ffn_g, layer), f"d_norm_ffn_out{layer}")
        dwd = _mm(hm, df, "tn", f"d_w_down{layer}", out_dtype=BF16)[0]
        down, up = [("wdn", layer)], [("wup", layer)]
        dw[down[0]] = dwd.reshape(N_CHIPS, dwd.shape[0] // N_CHIPS, d)
        dhm, (rode,) = _mm(df, rows("wdn", layer), "nt", f"d_ffn_mid{layer}", out_split=2, rides=[swap_ride(down)])
        swapped(down, rode)
        (da, dg[("conv_w", layer)], dg[("conv_b", layer)]), act_rode = _ffn_act_bwd(
            a, dhm, conv_w[layer], conv_b[layer], f"d_ffn_act{layer}", rides=act_rides)
        dw[up[0]], (rode,) = _mm(hn, da, "tn", f"d_w_up{layer}", out_dtype=BF16, out_split=N_CHIPS,
                                 rides=[scatter_ride(down)])
        scattered(down, rode)
        dhn, (rode,) = _mm(da, full[("wup", layer)], "nt", f"d_ffn_in{layer}", rides=[swap_ride(up)])
        swapped(up, rode)
        return dhn[0], act_rode

    dhn3, _ = ffn_bwd(dh4, h3, hn3, a1, hm1, f1, 1)
    dh3, (dg[("pre_ffn", 1)],) = _rms_bwd_in(dh4, h3, [([dhn3], gains(pre_ffn_g, 1))], "d_norm_ffn_in1")
    dmix1, dg[("post_mix", 1)] = _rms_bwd_out(dh3, mix1, gains(post_mix_g, 1), "d_norm_mix_out1")
    dwo = _mm(att, dmix1, "tn", "d_w_o", out_dtype=BF16)[0]
    dw[("wo", 0)] = dwo.reshape(N_CHIPS, dwo.shape[0] // N_CHIPS, d)
    datt = _mm(dmix1, rows("wo"), "nt", "d_attn_out", out_dtype=BF16)[0]
    ffn1_keys = [("wup", 1)]
    (dq, dk, dv), (rode,) = _attn_bwd(q, k, v, datt, lsum, "attn_bwd", rides=[scatter_ride(ffn1_keys)])
    scattered(ffn1_keys, rode)
    for nm, act, dact in (("wq", hn2, dq), ("wk", kvn, dk), ("wv", kvn, dv)):
        g = _mm(act, dact, "tn", f"d_{nm}", out_dtype=BF16)[0]
        dw[(nm, 0)] = g.reshape(N_CHIPS, g.shape[0] // N_CHIPS, g.shape[1])
    dhn2 = _mm(dq, rows("wq"), "nt", "d_q_in")[0]
    dkvn_k = _mm(dk, rows("wk"), "nt", "d_k_in")[0]
    attn_keys = [("wo", 0), ("wq", 0), ("wk", 0), ("wv", 0)]
    dkvn_v, (rode,) = _mm(dv, rows("wv"), "nt", "d_v_in", rides=[swap_ride(attn_keys)])
    swapped(attn_keys, rode)
    dh2, (dg[("pre_mix", 1)], dg["kv"]) = _rms_bwd_in(
        dh3, h2, [([dhn2], gains(pre_mix_g, 1)), ([dkvn_k, dkvn_v[0]], kv_g)], "d_norm_mix_in1")
    dhn1, (rode,) = ffn_bwd(dh2, h1, hn1, a0, hm0, f0, 0, act_rides=[scatter_ride(attn_keys)])
    scattered(attn_keys, rode)
    up0_pair = [pair[("wup", 0)]]
    up0_landed = [None]

    def up0_piece(part, span):
        return _ride_scatter(up0_pair, part, 8, into=up0_landed[0], span=span)

    def up0_rode(rode):
        up0_landed[0] = rode[1] if up0_landed[0] is None else rode[0]

    dh1, (dg[("pre_ffn", 0)],), (rode,) = _rms_bwd_in(
        dh2, h1, [([dhn1], gains(pre_ffn_g, 0))], "d_norm_ffn_in0", rides=[up0_piece(0, 1)])
    up0_rode(rode)
    dmix0, dg[("post_mix", 0)], (rode,) = _rms_bwd_out(
        dh1, mix0, gains(post_mix_g, 0), "d_norm_mix_out0", rides=[up0_piece(1, 1)])
    up0_rode(rode)
    early = ["wq", "wk", "wv", "wo", "wdn"]
    dwout, (((joined_early, _)),) = _mm(
        gm, dmix0, "tn", "d_w_out", out_dtype=BF16, rides=[_ride_join([half_done[nm] for nm in early])])
    grads_big = dict(zip(early, joined_early))
    w_out_key, w_in_key = [("wout", 0)], [("win", 0)]
    dw[w_out_key[0]] = dwout[0].reshape(N_CHIPS, dwout.shape[1] // N_CHIPS, d)
    dgm, (rode, up0) = _mm(dmix0, rows("wout"), "nt", "d_gmlp_gate", rides=[swap_ride(w_out_key), up0_piece(2, 1)])
    swapped(w_out_key, rode)
    up0_rode(up0)
    (duv, d_ws, d_bs, d_vg), (rode,) = _gmlp_bwd(uv, dgm[0], v_g, w_s, bias, "d_gmlp", rides=[up0_piece(3, 2)])
    up0_rode(rode)
    dw[w_in_key[0]], (rode, up0) = _mm(
        hn0, duv, "tn", "d_w_in", out_dtype=BF16, out_split=N_CHIPS, rides=[scatter_ride(w_out_key), up0_piece(5, 1)])
    scattered(w_out_key, rode)
    up0_rode(up0)
    dhn0, (rode, up0) = _mm(
        duv, full[("win", 0)], "nt", "d_gmlp_in", rides=[swap_ride(w_in_key), up0_piece(6, 2)])
    swapped(w_in_key, rode)
    up0_rode(up0)
    scattered([("wup", 0)], (None, up0_landed[0]))
    dx, (dg[("pre_mix", 0)],), (rode,) = _rms_bwd_in(
        dh1, h0, [([dhn0[0]], gains(pre_mix_g, 0))], "d_norm_in", rides=[scatter_ride(w_in_key)])
    scattered(w_in_key, rode)

    stack = lambda key: jnp.concatenate([dg[(key, layer)] for layer in range(n_layers)], axis=0)
    small_parts = [
        stack("pre_mix"), stack("post_mix"), stack("pre_ffn"), stack("post_ffn"),
        d_vg, d_ws, d_bs[::SUBLANE], dg["kv"],
        jnp.stack([dg[("conv_w", layer)] for layer in range(n_layers)]),
        jnp.stack([dg[("conv_b", layer)] for layer in range(n_layers)]),
    ]
    late = [nm for nm in big if nm not in early]
    summed, ((joined_late, _),) = _all_reduce_small(
        _pack(small_parts), "small_grads_sum", rides=[_ride_join([half_done[nm] for nm in late])])
    grads_big.update(zip(late, joined_late))
    (g_pre_mix, g_post_mix, g_pre_ffn, g_post_ffn, g_vg, g_ws, g_bs, g_kv, g_cw, g_cb) = _unpack(
        summed, [p.shape for p in small_parts])
    g_vg = lax.dynamic_index_in_dim(g_vg.reshape(N_CHIPS, 1, d_a // N_CHIPS), chip, 0, keepdims=False)
    g_cw = lax.dynamic_index_in_dim(g_cw, chip, 1, keepdims=False)
    g_cb = g_cb.reshape(n_layers, N_CHIPS * ns)
    small = [
        (pre_mix_g, g_pre_mix, m_pre_mix_g, v_pre_mix_g),
        (post_mix_g, g_post_mix, m_post_mix_g, v_post_mix_g),
        (pre_ffn_g, g_pre_ffn, m_pre_ffn_g, v_pre_ffn_g),
        (post_ffn_g, g_post_ffn, m_post_ffn_g, v_post_ffn_g),
        (a_v_norm_g, g_vg, m_a_v_norm_g, v_a_v_norm_g),
        (a_w_spatial, g_ws[None], m_a_w_spatial, v_a_w_spatial),
        (a_b_spatial, g_bs[None], m_a_b_spatial, v_a_b_spatial),
        (kv_norm_g, g_kv.reshape(d), m_kv_norm_g, v_kv_norm_g),
        (ffn_conv_w, g_cw, m_ffn_conv_w, v_ffn_conv_w),
        (ffn_conv_b, g_cb, m_ffn_conv_b, v_ffn_conv_b),
    ]
    small = [(w, g.reshape(w.shape), m, v) for w, g, m, v in small]
    packed = [_pack([t[i] for t in small])[None] for i in range(4)]
    small_new = [_unpack(p[0], [t[0].shape for t in small]) for p in _adamw(*packed, "adamw_small")]

    new_big = {nm: _adamw(big[nm][0], grads_big[nm], big[nm][1], big[nm][2], f"adamw_{nm}", pass_g=True)
               for nm in big}

    def big_out(nm, which):
        ref_shape = {"wk": w_k.shape, "wv": w_v.shape}.get(nm, big[nm][0].shape)
        return new_big[nm][which].reshape(ref_shape)

    order = ["pre_mix", "post_mix", "pre_ffn", "post_ffn", "win", "vg", "ws", "bs", "wout", "kv", "wk", "wv", "wq",
             "wo", "wup", "cw", "cb", "wdn"]
    small_at = {"pre_mix": 0, "post_mix": 1, "pre_ffn": 2, "post_ffn": 3, "vg": 4, "ws": 5, "bs": 6, "kv": 7,
                "cw": 8, "cb": 9}
    outs = [loss, dx[None]]
    for which in range(4):
        for nm in order:
            if nm in small_at:
                outs.append(small[small_at[nm]][1] if which == 0 else small_new[which - 1][small_at[nm]])
            else:
                outs.append(big_out(nm, which))
    return tuple(outs)
```

```python
import functools
import math

import jax
import jax.numpy as jnp
from jax import lax
from jax.experimental import pallas as pl
from jax.experimental.pallas import tpu as pltpu

F32 = jnp.float32
BF16 = jnp.bfloat16
EPS = 1e-6
ADAM_LR = 0.001
ADAM_B1 = 0.9
ADAM_B2 = 0.999
ADAM_EPS = 1e-08
ADAM_WD = 0.01
ADAM_STEP = 10

LANE = 128
SUBLANE = 8
ROWS = 16
TILE = 128
N_CHIPS = 4
N_DEV = 8
VMEM_LIMIT = 56 * 1024 * 1024
MM_VMEM = 46 * 1024 * 1024
MXU_WIDTH = 256
MESH = pl.DeviceIdType.MESH
ANY = pl.BlockSpec(memory_space=pl.ANY)
VMEM_SPEC = pl.BlockSpec(memory_space=pltpu.VMEM)


def _cp(*sem):
    return pltpu.CompilerParams(dimension_semantics=sem, vmem_limit_bytes=VMEM_LIMIT)


def _pick(dim, pref, align=LANE):
    if dim <= pref:
        return dim
    best = None
    for d in range(align, pref + 1, align):
        if dim % d == 0:
            best = d
    assert best is not None, (dim, pref)
    return best


_DIMS = {
    "nn": (((1,), (0,)), ((), ())),
    "nt": (((1,), (1,)), ((), ())),
    "tn": (((0,), (0,)), ((), ())),
}


def _as3(a):
    return a if a.ndim == 3 else a[None]


def _spec3(br, bc, cols_j, rc):
    per = cols_j // bc

    def imap(m, n, k):
        r, c = rc(m, n, k)
        return (c // per, r, c % per)

    return pl.BlockSpec((None, br, bc), imap)


def _mm(a, b, mode, name, out_dtype=F32, out_split=1, rides=()):
    a, b = _as3(a), _as3(b)
    ja, ra, caj = a.shape
    jb, rb, cbj = b.shape
    if mode == "nn":
        m, k, n = ra, ja * caj, jb * cbj
        assert rb == k
        m_ext, k_ext, n_ext = [ra], [caj, rb], [cbj]
    elif mode == "nt":
        m, k, n = ra, ja * caj, rb
        assert jb * cbj == k
        m_ext, k_ext, n_ext = [ra], [caj, cbj], [rb]
    else:
        m, k, n = ja * caj, ra, jb * cbj
        assert rb == k
        m_ext, k_ext, n_ext = [caj], [ra], [cbj]
    assert n % out_split == 0
    n_ext.append(n // out_split)
    bm = _pick(math.gcd(*m_ext), 1536)
    n_unit = math.gcd(*n_ext)
    bn = _pick(n_unit, 1536)
    k_unit = math.gcd(*k_ext)
    o_bytes = jnp.dtype(out_dtype).itemsize

    def vmem_need(bm, bn, bk):
        tiles = bm * bk * a.dtype.itemsize + bk * bn * b.dtype.itemsize + bm * bn * o_bytes
        return 2 * tiles + bm * bn * 4 * (2 if bk < k else 1)

    def deepest(bm, bn):
        return max(d for d in range(LANE, k_unit + 1, LANE)
                   if k_unit % d == 0 and (d == LANE or vmem_need(bm, bn, d) <= MM_VMEM))

    bk = deepest(bm, bn)
    if bk < k_unit and k_unit == k:
        if bm % (2 * LANE) == 0 and deepest(bm // 2, bn) == k:
            bm, bk = bm // 2, k
        elif bn % (2 * LANE) == 0 and deepest(bm, bn // 2) == k:
            bn, bk = bn // 2, k
    n_outer = False
    if bn % MXU_WIDTH and n_unit % MXU_WIDTH == 0 and k_unit == k:
        for rows in (bm, bm // 2, bm // 4):
            if rows % LANE == 0 and vmem_need(rows, n_unit, k) <= MM_VMEM:
                bm, bn, bk, n_outer = rows, n_unit, k, True
                break
        else:
            if k % (2 * LANE) == 0 and vmem_need(bm, n_unit, k // 2) <= MM_VMEM:
                bn, bk, n_outer = n_unit, k // 2, True
    nk = k // bk
    order = (lambda f: lambda ni, mi, ki: f(mi, ni, ki)) if n_outer else (lambda f: f)
    if mode == "nn":
        a_spec = _spec3(bm, bk, caj, order(lambda mi, ni, ki: (mi, ki)))
        b_spec = _spec3(bk, bn, cbj, order(lambda mi, ni, ki: (ki, ni)))
    elif mode == "nt":
        a_spec = _spec3(bm, bk, caj, order(lambda mi, ni, ki: (mi, ki)))
        b_spec = _spec3(bn, bk, cbj, order(lambda mi, ni, ki: (ni, ki)))
    else:
        a_spec = _spec3(bk, bm, caj, order(lambda mi, ni, ki: (ki, mi)))
        b_spec = _spec3(bk, bn, cbj, order(lambda mi, ni, ki: (ki, ni)))
    o_spec = _spec3(bm, bn, n // out_split, order(lambda mi, ni, ki: (mi, ni)))
    dims = _DIMS[mode]

    def body(a_ref, b_ref, o_ref, *acc):
        def part():
            return lax.dot_general(a_ref[...].astype(BF16), b_ref[...].astype(BF16), dims, preferred_element_type=F32)

        if nk == 1:
            o_ref[...] = part().astype(o_ref.dtype)
            return
        acc_ref, = acc
        ki = pl.program_id(2)

        @pl.when(ki == 0)
        def _():
            acc_ref[...] = part()

        @pl.when(jnp.logical_and(ki > 0, ki < nk - 1))
        def _():
            acc_ref[...] += part()

        @pl.when(ki == nk - 1)
        def _():
            o_ref[...] = (acc_ref[...] + part()).astype(o_ref.dtype)

    grid = (n // bn, m // bm, nk) if n_outer else (m // bm, n // bn, nk)
    out, rode = _hosted_call(
        body, [a, b], name=name, grid=grid, in_specs=[a_spec, b_spec], out_specs=o_spec,
        out_shape=jax.ShapeDtypeStruct((out_split, m, n // out_split), out_dtype),
        scratch_shapes=[pltpu.VMEM((bm, bn), F32)] if nk > 1 else [],
        semantics=("parallel", "parallel", "arbitrary"), rides=rides)
    return (out, rode) if rides else out


def _rms(x, g):
    r = lax.rsqrt(jnp.mean(x * x, axis=-1, keepdims=True) + EPS)
    return x * r * g


def _rms_bwd(x, g, dy):
    r = lax.rsqrt(jnp.mean(x * x, axis=-1, keepdims=True) + EPS)
    xh = x * r
    gy = dy * g
    dx = r * (gy - xh * jnp.mean(gy * xh, axis=-1, keepdims=True))
    return dx, jnp.sum(dy * xh, axis=0, keepdims=True)


def _row_block(s, streams):
    return _pick(s, 512 if streams <= 4 else 256, ROWS)


def _rms_fwd(h, g, name, rides=()):
    s, d = h.shape
    br = _row_block(s, 2)

    def body(h_ref, g_ref, o_ref):
        o_ref[...] = _rms(h_ref[...], g_ref[...]).astype(BF16)

    row = pl.BlockSpec((br, d), lambda i: (i, 0))
    vec = pl.BlockSpec((1, d), lambda i: (0, 0))
    out, rode = _hosted_call(
        body, [h, g], name=name, grid=(s // br,), in_specs=[row, vec], out_specs=row,
        out_shape=jax.ShapeDtypeStruct((s, d), BF16), semantics=("parallel",), rides=rides)
    return (out, rode) if rides else out


def _resid_rms(h_in, f, g_post, g_next, name, rides=()):
    s, d = h_in.shape
    br = _row_block(s, 3 + (len(g_next) + 1) // 2)
    n_next = len(g_next)

    def body(h_ref, f_ref, gp_ref, *refs):
        gn_refs, ho_ref, hn_refs = refs[:n_next], refs[n_next], refs[n_next + 1:]
        h = h_ref[...] + _rms(f_ref[...], gp_ref[...])
        ho_ref[...] = h
        for gn_ref, hn_ref in zip(gn_refs, hn_refs):
            hn_ref[...] = _rms(h, gn_ref[...]).astype(BF16)

    row = pl.BlockSpec((br, d), lambda i: (i, 0))
    vec = pl.BlockSpec((1, d), lambda i: (0, 0))
    outs, rode = _hosted_call(
        body, [h_in, f, g_post, *g_next], name=name, grid=(s // br,),
        in_specs=[row, row, vec] + [vec] * n_next,
        out_specs=[row] * (1 + n_next),
        out_shape=[jax.ShapeDtypeStruct((s, d), F32)] + [jax.ShapeDtypeStruct((s, d), BF16)] * n_next,
        semantics=("parallel",), rides=rides)
    return (outs, rode) if rides else outs


def _loss_head(h_in, f, g_post, target, name):
    s, d = h_in.shape
    br = _row_block(s, 4)

    def body(h_ref, f_ref, gp_ref, t_ref, dh_ref, loss_ref):
        @pl.when(pl.program_id(0) == 0)
        def _():
            loss_ref[...] = jnp.zeros_like(loss_ref)

        diff = h_ref[...] + _rms(f_ref[...], gp_ref[...]) - t_ref[...]
        dh_ref[...] = diff * (1.0 / d)
        loss_ref[...] += 0.5 * jnp.sum(jnp.mean(diff * diff, axis=-1, keepdims=True))

    row = pl.BlockSpec((br, d), lambda i: (i, 0))
    vec = pl.BlockSpec((1, d), lambda i: (0, 0))
    return pl.pallas_call(
        body, name=name, grid=(s // br,),
        in_specs=[row, row, vec, row],
        out_specs=[row, pl.BlockSpec((SUBLANE, LANE), lambda i: (0, 0))],
        out_shape=[jax.ShapeDtypeStruct((s, d), F32), jax.ShapeDtypeStruct((SUBLANE, LANE), F32)],
        compiler_params=_cp("arbitrary"),
    )(h_in, f, g_post, target)


def _rms_bwd_out(dy, f, g, name, rides=()):
    s, d = f.shape
    br = _row_block(s, 3)

    def body(dy_ref, f_ref, g_ref, df_ref, dg_ref):
        @pl.when(pl.program_id(0) == 0)
        def _():
            dg_ref[...] = jnp.zeros_like(dg_ref)

        dx, dg = _rms_bwd(f_ref[...], g_ref[...], dy_ref[...])
        df_ref[...] = dx.astype(BF16)
        dg_ref[...] += dg

    row = pl.BlockSpec((br, d), lambda i: (i, 0))
    vec = pl.BlockSpec((1, d), lambda i: (0, 0))
    (df, dg), rode = _hosted_call(
        body, [dy, f, g], name=name, grid=(s // br,), in_specs=[row, row, vec], out_specs=[row, vec],
        out_shape=[jax.ShapeDtypeStruct((s, d), BF16), jax.ShapeDtypeStruct((1, d), F32)],
        semantics=("arbitrary",), rides=rides)
    return (df, dg, rode) if rides else (df, dg)


def _rms_bwd_in(dh_out, h_in, branches, name, rides=()):
    s, d = h_in.shape
    br = _row_block(s, 3 + sum(len(ds) for ds, _ in branches))
    counts = [len(ds) for ds, _ in branches]
    n_d = sum(counts)
    n_b = len(branches)

    def body(dho_ref, h_ref, *refs):
        d_refs, g_refs = refs[:n_d], refs[n_d:n_d + n_b]
        dh_ref, dg_refs = refs[n_d + n_b], refs[n_d + n_b + 1:]

        @pl.when(pl.program_id(0) == 0)
        def _():
            for r in dg_refs:
                r[...] = jnp.zeros_like(r)

        h = h_ref[...]
        acc = dho_ref[...]
        at = 0
        for bi, cnt in enumerate(counts):
            dn = d_refs[at][...]
            for r in d_refs[at + 1:at + cnt]:
                dn = dn + r[...]
            at += cnt
            dx, dg = _rms_bwd(h, g_refs[bi][...], dn)
            acc = acc + dx
            dg_refs[bi][...] += dg
        dh_ref[...] = acc

    row = pl.BlockSpec((br, d), lambda i: (i, 0))
    vec = pl.BlockSpec((1, d), lambda i: (0, 0))
    flat_d = [x for ds, _ in branches for x in ds]
    outs, rode = _hosted_call(
        body, [dh_out, h_in, *flat_d, *[g for _, g in branches]], name=name, grid=(s // br,),
        in_specs=[row, row] + [row] * n_d + [vec] * n_b,
        out_specs=[row] + [vec] * n_b,
        out_shape=[jax.ShapeDtypeStruct((s, d), F32)] + [jax.ShapeDtypeStruct((1, d), F32)] * n_b,
        semantics=("arbitrary",), rides=rides)
    return (outs[0], list(outs[1:]), rode) if rides else (outs[0], list(outs[1:]))


def _split3(x):
    x0 = x.astype(BF16)
    r1 = x - x0.astype(F32)
    x1 = r1.astype(BF16)
    x2 = (r1 - x1.astype(F32)).astype(BF16)
    return x0, x1, x2


def _tri(n, kind):
    r = lax.broadcasted_iota(jnp.int32, (n, n), 0)
    c = lax.broadcasted_iota(jnp.int32, (n, n), 1)
    m = {"lt": r < c, "le": r <= c, "gt": r > c}[kind]
    return jnp.where(m, 1.0, 0.0).astype(BF16)


_GELU_C = math.sqrt(2.0 / math.pi)
_GELU_A = 0.044715


def _gelu(x):
    return 0.5 * x * (1.0 + jnp.tanh(_GELU_C * (x + _GELU_A * (x * x * x))))


def _gelu_grad(x):
    t = jnp.tanh(_GELU_C * (x + _GELU_A * (x * x * x)))
    return 0.5 * (1.0 + t) + 0.5 * x * (1.0 - t * t) * (_GELU_C * (1.0 + 3.0 * _GELU_A * (x * x)))


def _causal_w(w):
    r = lax.broadcasted_iota(jnp.int32, (TILE, TILE), 0)
    c = lax.broadcasted_iota(jnp.int32, (TILE, TILE), 1)
    return jnp.where(c <= r, w, 0.0)


def _uv_tiles(uv_ref, g, d_a, dq):
    cu, cv = g * TILE, d_a + g * TILE
    u = uv_ref[cu // dq, :, pl.ds(cu % dq, TILE)]
    v = uv_ref[cv // dq, :, pl.ds(cv % dq, TILE)]
    return u, v


def _gmlp_fwd(uv, v_g, w_s, bias, name, rides=()):
    _, s, dq = uv.shape
    d_a = 2 * dq
    n_g = d_a // TILE

    def body(uv_ref, vg_ref, ws_ref, b_ref, o_ref):
        for g in range(n_g):
            up, vp = _uv_tiles(uv_ref, g, d_a, dq)
            cols = pl.ds(g * TILE, TILE)
            vn = _rms(_gelu(vp), vg_ref[:, cols])
            mixed = jnp.dot(_causal_w(ws_ref[g]).astype(BF16), vn.astype(BF16), preferred_element_type=F32) + b_ref[:, cols]
            o_ref[:, cols] = (_gelu(up) * mixed).astype(BF16)

    return _hosted_call(
        body, [uv, v_g, w_s, bias], name=name, grid=(s // TILE,),
        in_specs=[
            pl.BlockSpec((4, TILE, dq), lambda i: (0, i, 0)),
            pl.BlockSpec((1, d_a), lambda i: (0, 0)),
            pl.BlockSpec((n_g, TILE, TILE), lambda i: (0, 0, 0)),
            pl.BlockSpec((TILE, d_a), lambda i: (0, 0)),
        ],
        out_specs=pl.BlockSpec((TILE, d_a), lambda i: (i, 0)),
        out_shape=jax.ShapeDtypeStruct((s, d_a), BF16),
        semantics=("parallel",), rides=rides)


def _gmlp_bwd(uv, dgm, v_g, w_s, bias, name, rides=()):
    _, s, dq = uv.shape
    d_a = 2 * dq
    n_g = d_a // TILE
    n_c = s // TILE

    def body(uv_ref, d_ref, vg_ref, ws_ref, b_ref, duv_ref, dws_ref, dbs_ref, dvg_ref, dbias_acc):
        i = pl.program_id(0)

        @pl.when(i == 0)
        def _():
            dws_ref[...] = jnp.zeros_like(dws_ref)
            dvg_ref[...] = jnp.zeros_like(dvg_ref)
            dbias_acc[...] = jnp.zeros_like(dbias_acc)

        for g in range(n_g):
            up, vp = _uv_tiles(uv_ref, g, d_a, dq)
            cols = pl.ds(g * TILE, TILE)
            vg = vg_ref[:, cols]
            u = _gelu(up)
            v = _gelu(vp)
            r = lax.rsqrt(jnp.mean(v * v, axis=-1, keepdims=True) + EPS)
            vh = v * r
            vn = (vh * vg).astype(BF16)
            wc = _causal_w(ws_ref[g]).astype(BF16)
            mixed = jnp.dot(wc, vn, preferred_element_type=F32) + b_ref[:, cols]
            d_out = d_ref[:, cols]
            du = d_out * mixed
            dmixed = d_out * u
            dmb = dmixed.astype(BF16)
            dvn = lax.dot_general(wc, dmb, _DIMS["tn"], preferred_element_type=F32)
            dws_ref[g] += lax.dot_general(dmb, vn, _DIMS["nt"], preferred_element_type=F32)
            dbias_acc[:, cols] += dmixed
            dvg_ref[:, cols] += jnp.sum(dvn * vh, axis=0, keepdims=True)
            gv = dvn * vg
            dv = r * (gv - vh * jnp.mean(gv * vh, axis=-1, keepdims=True))
            cu, cv = g * TILE, d_a + g * TILE
            duv_ref[cu // dq, :, pl.ds(cu % dq, TILE)] = (du * _gelu_grad(up)).astype(BF16)
            duv_ref[cv // dq, :, pl.ds(cv % dq, TILE)] = (dv * _gelu_grad(vp)).astype(BF16)

        @pl.when(i == n_c - 1)
        def _():
            ones = jnp.ones((SUBLANE, TILE), BF16)
            for g in range(n_g):
                dws_ref[g] = _causal_w(dws_ref[g])
                cols = pl.ds(g * TILE, TILE)
                out = None
                for t in _split3(dbias_acc[:, cols]):
                    p = lax.dot_general(ones, t, _DIMS["nt"], preferred_element_type=F32)
                    out = p if out is None else out + p
                dbs_ref[pl.ds(g * SUBLANE, SUBLANE), :] = out

    return _hosted_call(
        body, [uv, dgm, v_g, w_s, bias], name=name, grid=(n_c,), semantics=("arbitrary",), rides=rides,
        in_specs=[
            pl.BlockSpec((4, TILE, dq), lambda i: (0, i, 0)),
            pl.BlockSpec((TILE, d_a), lambda i: (i, 0)),
            pl.BlockSpec((1, d_a), lambda i: (0, 0)),
            pl.BlockSpec((n_g, TILE, TILE), lambda i: (0, 0, 0)),
            pl.BlockSpec((TILE, d_a), lambda i: (0, 0)),
        ],
        out_specs=[
            pl.BlockSpec((4, TILE, dq), lambda i: (0, i, 0)),
            pl.BlockSpec((n_g, TILE, TILE), lambda i: (0, 0, 0)),
            pl.BlockSpec((n_g * SUBLANE, TILE), lambda i: (0, 0)),
            pl.BlockSpec((1, d_a), lambda i: (0, 0)),
        ],
        out_shape=[
            jax.ShapeDtypeStruct((4, s, dq), BF16),
            jax.ShapeDtypeStruct((n_g, TILE, TILE), F32),
            jax.ShapeDtypeStruct((n_g * SUBLANE, TILE), F32),
            jax.ShapeDtypeStruct((1, d_a), F32),
        ],
        scratch_shapes=[pltpu.VMEM((TILE, d_a), F32)])


def _sigmoid(x):
    return 1.0 / (1.0 + jnp.exp(-x))


def _conv3(ext, w, b):
    return b + ((w[0:1] * pltpu.roll(ext, 2, 0) + w[1:2] * pltpu.roll(ext, 1, 0)) + w[2:3] * ext)


def _act_blocks(s, ns):
    return _pick(s, 512, ROWS), _pick(ns, 256)


def _ffn_act_fwd(a, cw, cb, name, rides=()):
    _, s, ns = a.shape
    bs, cb_w = _act_blocks(s, ns)
    hb = bs // SUBLANE

    def body(a_ref, prev_ref, cw_ref, cb_ref, o_ref):
        first = pl.program_id(0) == 0

        def conv(comp):
            prev = jnp.where(first, 0.0, prev_ref[comp])
            ext = jnp.concatenate([prev, a_ref[comp]], axis=0)
            return _conv3(ext, cw_ref[comp], cb_ref[comp])[SUBLANE:]

        for p in range(2):
            cg = conv(p)
            o_ref[p] = (cg * _sigmoid(cg) * conv(2 + p)).astype(BF16)

    hm, rode = _hosted_call(
        body, [a, a, cw, cb], name=name, grid=(s // bs, ns // cb_w),
        in_specs=[
            pl.BlockSpec((4, bs, cb_w), lambda i, j: (0, i, j)),
            pl.BlockSpec((4, SUBLANE, cb_w), lambda i, j: (0, jnp.maximum(i * hb - 1, 0), j)),
            pl.BlockSpec((4, 3, cb_w), lambda i, j: (0, 0, j)),
            pl.BlockSpec((4, 1, cb_w), lambda i, j: (0, 0, j)),
        ],
        out_specs=pl.BlockSpec((2, bs, cb_w), lambda i, j: (0, i, j)),
        out_shape=jax.ShapeDtypeStruct((2, s, ns), BF16),
        semantics=("parallel", "parallel"), rides=rides)
    return (hm, rode) if rides else hm


def _ffn_act_bwd(a, dhm, cw, cb, name, rides=()):
    _, s, ns = a.shape
    bs, cb_w = _act_blocks(s, ns)
    hb = bs // SUBLANE
    n_i = s // bs
    n_ext = bs + 2 * SUBLANE
    cur = slice(SUBLANE, SUBLANE + bs)

    def body(a_ref, prev_ref, next_ref, d_ref, dnext_ref, cw_ref, cb_ref, da_ref, dcw_ref, dcb_ref):
        i = pl.program_id(1)
        first, last = i == 0, i == n_i - 1

        @pl.when(first)
        def _():
            dcw_ref[...] = jnp.zeros_like(dcw_ref)
            dcb_ref[...] = jnp.zeros_like(dcb_ref)

        def ext_of(comp):
            return jnp.concatenate([jnp.where(first, 0.0, prev_ref[comp]), a_ref[comp], next_ref[comp]], axis=0)

        def back(comp, a_ext, dc):
            w = cw_ref[comp]
            da = (w[2:3] * dc + w[1:2] * pltpu.roll(dc, n_ext - 1, 0)) + w[0:1] * pltpu.roll(dc, n_ext - 2, 0)
            da_ref[comp] = da[cur].astype(BF16)
            dcc = dc[cur]
            dcw_ref[comp, 0:1, :] += jnp.sum(dcc * pltpu.roll(a_ext, 2, 0)[cur], axis=0, keepdims=True)
            dcw_ref[comp, 1:2, :] += jnp.sum(dcc * pltpu.roll(a_ext, 1, 0)[cur], axis=0, keepdims=True)
            dcw_ref[comp, 2:3, :] += jnp.sum(dcc * a_ext[cur], axis=0, keepdims=True)
            dcb_ref[comp] += jnp.sum(dcc, axis=0, keepdims=True)

        for p in range(2):
            ag, av = ext_of(p), ext_of(2 + p)
            cg = _conv3(ag, cw_ref[p], cb_ref[p])
            cv = _conv3(av, cw_ref[2 + p], cb_ref[2 + p])
            d = jnp.concatenate(
                [jnp.zeros((SUBLANE, cb_w), F32), d_ref[p], jnp.where(last, 0.0, dnext_ref[p])], axis=0)
            sg = _sigmoid(cg)
            back(2 + p, av, d * (cg * sg))
            back(p, ag, d * cv * (sg * (1.0 + cg * (1.0 - sg))))

    return _hosted_call(
        body, [a, a, a, dhm, dhm, cw, cb], name=name, grid=(ns // cb_w, n_i),
        in_specs=[
            pl.BlockSpec((4, bs, cb_w), lambda j, i: (0, i, j)),
            pl.BlockSpec((4, SUBLANE, cb_w), lambda j, i: (0, jnp.maximum(i * hb - 1, 0), j)),
            pl.BlockSpec((4, SUBLANE, cb_w), lambda j, i: (0, jnp.minimum((i + 1) * hb, n_i * hb - 1), j)),
            pl.BlockSpec((2, bs, cb_w), lambda j, i: (0, i, j)),
            pl.BlockSpec((2, SUBLANE, cb_w), lambda j, i: (0, jnp.minimum((i + 1) * hb, n_i * hb - 1), j)),
            pl.BlockSpec((4, 3, cb_w), lambda j, i: (0, 0, j)),
            pl.BlockSpec((4, 1, cb_w), lambda j, i: (0, 0, j)),
        ],
        out_specs=[
            pl.BlockSpec((4, bs, cb_w), lambda j, i: (0, i, j)),
            pl.BlockSpec((4, 3, cb_w), lambda j, i: (0, 0, j)),
            pl.BlockSpec((4, 1, cb_w), lambda j, i: (0, 0, j)),
        ],
        out_shape=[
            jax.ShapeDtypeStruct((4, s, ns), BF16),
            jax.ShapeDtypeStruct((4, 3, ns), F32),
            jax.ShapeDtypeStruct((4, 1, ns), F32),
        ],
        semantics=("parallel", "arbitrary"), rides=rides)


ATT_BQ_FWD = 2048
ATT_BQ_BWD = 1024
ATT_BK = 256
ATT_UNROLL = 2
ATT_UNROLL_BWD = 4


def _att_blocks(s, bq_pref):
    bq = _pick(s, bq_pref)
    bk = min(ATT_BK, bq)
    assert bq % bk == 0
    return bq, bk


def _dot_sel2(x, sel):
    hi = x.astype(BF16)
    lo = (x - hi.astype(F32)).astype(BF16)
    n = x.shape[0]
    both = jnp.dot(jnp.concatenate([hi, lo], axis=0), sel, preferred_element_type=F32)
    return both[:n] + both[n:]


def _causal_mask(bq, bk, row0, col0):
    rows = row0 + lax.broadcasted_iota(jnp.int32, (bq, bk), 0)
    cols = col0 + lax.broadcasted_iota(jnp.int32, (bq, bk), 1)
    return cols < rows


def _sb_tile(qb, kb, scale, mask):
    z = lax.dot_general(qb, kb, _DIMS["nt"], preferred_element_type=F32) * scale
    e = jnp.exp(-jnp.abs(z))
    lb = jnp.minimum(z, 0.0) - jnp.log(1.0 + e)
    l1m = lb - z
    if mask is not None:
        l1m = jnp.where(mask, l1m, 0.0)
    return z, e, lb, l1m


def _attn_fwd(q, k, v, name, rides=()):
    s, hd = q.shape
    bq, bk = _att_blocks(s, ATT_BQ_FWD)
    r = bq // bk
    unroll = math.gcd(r, ATT_UNROLL)
    n_h, n_q = hd // TILE, s // bq
    scale = 1.0 / math.sqrt(TILE)

    def body(q_ref, k_ref, v_ref, o_ref, l_ref, acc_ref, suf_ref):
        i = pl.program_id(1)
        qb = q_ref[...]
        later = _tri(bk, "gt")
        acc_ref[...] = jnp.zeros_like(acc_ref)
        suf_ref[...] = jnp.zeros_like(suf_ref)

        def tile(j, row0):
            rows = pl.ds(pl.multiple_of(j * bk, bk), bk)
            masked = row0 is not None
            r0 = row0 if masked else 0
            rs = pl.ds(r0, bq - r0)
            mask = _causal_mask(bq - r0, bk, i * bq + r0, j * bk) if masked else None
            _, _, lb, l1m = _sb_tile(qb[r0:], k_ref[rows, :], scale, mask)
            a = jnp.exp(lb + _dot_sel2(l1m, later) + suf_ref[rs, :])
            if masked:
                a = jnp.where(mask, a, 0.0)
            acc_ref[rs, :] += jnp.dot(a.astype(BF16), v_ref[rows, :], preferred_element_type=F32)
            suf_ref[rs, :] += jnp.sum(l1m, axis=1, keepdims=True)

        for dgl in range(r - 1, -1, -1):
            tile(r * i + dgl, dgl * bk)

        def step(t, carry):
            for u in range(unroll):
                tile(r * i - 1 - (unroll * t + u), None)
            return carry

        lax.fori_loop(0, (r * i) // unroll, step, 0)
        o_ref[...] = acc_ref[...].astype(BF16)
        l_ref[...] = jnp.broadcast_to(suf_ref[...], (bq, TILE))

    blk = pl.BlockSpec((bq, TILE), lambda h, i: (i, h))
    head = pl.BlockSpec((s, TILE), lambda h, i: (0, h))
    return _hosted_call(
        body, [q, k, v], name=name, grid=(n_h, n_q), in_specs=[blk, head, head], out_specs=[blk, blk],
        out_shape=[jax.ShapeDtypeStruct((s, hd), BF16), jax.ShapeDtypeStruct((s, hd), F32)],
        scratch_shapes=[pltpu.VMEM((bq, TILE), F32), pltpu.VMEM((bq, 1), F32)],
        semantics=("parallel", "parallel"), rides=rides)


def _attn_bwd(q, k, v, do, lsum, name, rides=()):
    s, hd = q.shape
    bq, bk = _att_blocks(s, ATT_BQ_BWD)
    r = bq // bk
    unroll = math.gcd(r, ATT_UNROLL_BWD)
    n_h, n_q = hd // TILE, s // bq
    scale = 1.0 / math.sqrt(TILE)

    def body(q_ref, k_ref, v_ref, do_ref, l_ref, dq_ref, dk_ref, dv_ref, dq_acc, pre_ref, cp_ref):
        i = pl.program_id(1)

        @pl.when(i == 0)
        def _():
            dk_ref[...] = jnp.zeros_like(dk_ref)
            dv_ref[...] = jnp.zeros_like(dv_ref)

        qb = q_ref[...]
        dob = do_ref[...]
        upto = _tri(bk, "le")
        before = _tri(bk, "lt")
        dq_acc[...] = jnp.zeros_like(dq_acc)
        pre_ref[...] = jnp.zeros_like(pre_ref)
        cp_ref[...] = jnp.zeros_like(cp_ref)

        def tile(j, row0):
            rows = pl.ds(pl.multiple_of(j * bk, bk), bk)
            kb, vb = k_ref[rows, :], v_ref[rows, :]
            masked = row0 is not None
            r0 = row0 if masked else 0
            rs = pl.ds(r0, bq - r0)
            qs, dos = qb[r0:], dob[r0:]
            mask = _causal_mask(bq - r0, bk, i * bq + r0, j * bk) if masked else None
            z, e, lb, l1m = _sb_tile(qs, kb, scale, mask)
            suffix = (l_ref[rs, 0:1] - pre_ref[rs, :]) - _dot_sel2(l1m, upto)
            a = jnp.exp(lb + suffix)
            if masked:
                a = jnp.where(mask, a, 0.0)
            p = a * lax.dot_general(dos, vb, _DIMS["nt"], preferred_element_type=F32)
            both = p + (cp_ref[rs, :] + jnp.dot(p.astype(BF16), before, preferred_element_type=F32))
            sg = jnp.where(z >= 0.0, 1.0, e) * pl.reciprocal(1.0 + e, approx=True)
            dz = p - both * sg
            if masked:
                dz = jnp.where(mask, dz, 0.0)
            dz = (dz * scale).astype(BF16)
            dq_acc[rs, :] += jnp.dot(dz, kb, preferred_element_type=F32)
            dk_ref[rows, :] += lax.dot_general(dz, qs, _DIMS["tn"], preferred_element_type=F32)
            dv_ref[rows, :] += lax.dot_general(a.astype(BF16), dos, _DIMS["tn"], preferred_element_type=F32)
            pre_ref[rs, :] += jnp.sum(l1m, axis=1, keepdims=True)
            cp_ref[rs, :] += jnp.sum(p, axis=1, keepdims=True)

        def step(j, carry):
            for u in range(unroll):
                tile(unroll * j + u, None)
            return carry

        lax.fori_loop(0, (r * i) // unroll, step, 0)
        for dgl in range(r):
            tile(r * i + dgl, dgl * bk)
        dq_ref[...] = dq_acc[...].astype(BF16)

    blk = pl.BlockSpec((bq, TILE), lambda h, i: (i, h))
    head = pl.BlockSpec((s, TILE), lambda h, i: (0, h))
    return _hosted_call(
        body, [q, k, v, do, lsum], name=name, grid=(n_h, n_q), in_specs=[blk, head, head, blk, blk],
        out_specs=[blk, head, head],
        out_shape=[jax.ShapeDtypeStruct((s, hd), BF16), jax.ShapeDtypeStruct((s, hd), F32),
                   jax.ShapeDtypeStruct((s, hd), F32)],
        scratch_shapes=[pltpu.VMEM((bq, TILE), F32), pltpu.VMEM((bq, 1), F32), pltpu.VMEM((bq, 1), F32)],
        semantics=("parallel", "arbitrary"), rides=rides)


EW_BLOCK = 512 * 1024


def _ew_blocks(r, c, elems=EW_BLOCK):
    return _pick(r, max(ROWS, elems // c // ROWS * ROWS), ROWS), c


def _cast_bf16(w, layer, chip_idx, name):
    _, r, c = w.shape
    br, bc = _ew_blocks(r, c)

    def body(chip_ref, w_ref, o_ref):
        o_ref[...] = w_ref[...].astype(BF16)

    return pl.pallas_call(
        body, name=name,
        grid_spec=pltpu.PrefetchScalarGridSpec(
            num_scalar_prefetch=1, grid=(r // br, c // bc),
            in_specs=[pl.BlockSpec((None, br, bc), lambda i, j, chip_ref: (layer, i, j))],
            out_specs=pl.BlockSpec((None, br, bc), lambda i, j, chip_ref: (chip_ref[0], i, j)),
        ),
        out_shape=jax.ShapeDtypeStruct((N_CHIPS, r, c), BF16), compiler_params=_cp("parallel", "parallel"),
    )(chip_idx, w)


def _pair_add(dw, recv, c_idx, name):
    _, r, c = dw.shape
    hr = r // 2
    br, bc = _ew_blocks(hr, c)
    nb = hr // br

    def body(c_ref, a_ref, b_ref, o_ref):
        o_ref[...] = (a_ref[...].astype(F32) + b_ref[...].astype(F32)).astype(BF16)

    return pl.pallas_call(
        body, name=name,
        grid_spec=pltpu.PrefetchScalarGridSpec(
            num_scalar_prefetch=1, grid=(N_CHIPS, nb, c // bc),
            in_specs=[
                pl.BlockSpec((None, br, bc), lambda s, i, j, c_ref: (s, c_ref[0] * nb + i, j)),
                pl.BlockSpec((None, br, bc), lambda s, i, j, c_ref: (s, i, j)),
            ],
            out_specs=pl.BlockSpec((None, br, bc), lambda s, i, j, c_ref: (s, i, j)),
        ),
        out_shape=jax.ShapeDtypeStruct((N_CHIPS, hr, c), BF16),
        compiler_params=_cp("parallel", "parallel", "parallel"),
    )(c_idx, dw, recv)


def _chip_sum(parts, dest, shape, layer, c_idx, name):
    _, hr, c = parts.shape
    br, bc = _ew_blocks(hr, c, EW_BLOCK // 2)
    nb = hr // br

    def body(c_ref, p_ref, *refs):
        o_ref = refs[-1]
        acc = p_ref[0].astype(F32)
        for s in range(1, N_CHIPS):
            acc = acc + p_ref[s].astype(F32)
        o_ref[...] = acc

    in_specs = [pl.BlockSpec((N_CHIPS, br, bc), lambda i, j, c_ref: (0, i, j))]
    operands = [c_idx, parts]
    aliases = {}
    if dest is not None:
        in_specs.append(ANY)
        operands.append(dest)
        aliases = {2: 0}
    return pl.pallas_call(
        body, name=name,
        grid_spec=pltpu.PrefetchScalarGridSpec(
            num_scalar_prefetch=1, grid=(nb, c // bc), in_specs=in_specs,
            out_specs=pl.BlockSpec((None, br, bc), lambda i, j, c_ref: (layer, c_ref[0] * nb + i, j)),
        ),
        out_shape=jax.ShapeDtypeStruct(shape, F32), input_output_aliases=aliases,
        compiler_params=_cp("parallel", "parallel"),
    )(*operands)


def _adamw(w, g, m, v, name, pass_g=False):
    n_l, r, c = w.shape
    br, bc = _ew_blocks(r, c, EW_BLOCK // 2)

    def body(w_ref, g_ref, m_ref, v_ref, *out_refs):
        d_ref, mo_ref, vo_ref = out_refs[-3:]
        g = g_ref[...]
        if pass_g:
            out_refs[0][...] = g
        m = ADAM_B1 * m_ref[...] + (1.0 - ADAM_B1) * g
        v = ADAM_B2 * v_ref[...] + (1.0 - ADAM_B2) * (g * g)
        m_hat = m / (1.0 - ADAM_B1 ** ADAM_STEP)
        v_hat = v / (1.0 - ADAM_B2 ** ADAM_STEP)
        d_ref[...] = -ADAM_LR * (m_hat / (jnp.sqrt(v_hat) + ADAM_EPS) + ADAM_WD * w_ref[...])
        mo_ref[...] = m
        vo_ref[...] = v

    blk = pl.BlockSpec((None, br, bc), lambda l, i, j: (l, i, j))
    n_out = 4 if pass_g else 3
    return pl.pallas_call(
        body, name=name, grid=(n_l, r // br, c // bc), in_specs=[blk] * 4, out_specs=[blk] * n_out,
        out_shape=[jax.ShapeDtypeStruct(w.shape, F32)] * n_out,
        compiler_params=_cp("parallel", "parallel", "parallel"),
    )(w, g, m, v)


def _place():
    x, y, c = lax.axis_index("x"), lax.axis_index("y"), lax.axis_index("c")
    chips = [(1 - x, y), (x, 1 - y), (1 - x, 1 - y)]
    return x, y, c, chips


class _Ride:
    def __init__(self, reads, bufs, new, n_sems, start, finish):
        self.reads, self.bufs, self.new, self.n_sems, self.start, self.finish = reads, bufs, new, n_sems, start, finish


def _hosted_call(body, operands, *, name, grid, in_specs, out_specs, out_shape, scratch_shapes=(), semantics=(), rides=()):
    single = not isinstance(out_shape, (list, tuple))
    out_specs = [out_specs] if single else list(out_specs)
    out_shape = [out_shape] if single else list(out_shape)
    in_specs, scratch_shapes = list(in_specs), list(scratch_shapes)
    n_in, n_out, n_scr = len(in_specs), len(out_shape), len(scratch_shapes)
    extra_in, extra_out, aliases, where = [], [], {}, []
    for ride in rides:
        r0 = len(extra_in)
        extra_in += list(ride.reads)
        b0 = len(extra_in)
        extra_in += list(ride.bufs)
        ob0 = len(extra_out)
        extra_out += [jax.ShapeDtypeStruct(b.shape, b.dtype) for b in ride.bufs]
        for t in range(len(ride.bufs)):
            aliases[n_in + b0 + t] = n_out + ob0 + t
        on0 = len(extra_out)
        extra_out += list(ride.new)
        where.append((r0, len(ride.reads), ob0, len(ride.bufs), on0, len(ride.new)))
    n_ein, n_eout = len(extra_in), len(extra_out)
    sem_shapes = [pltpu.SemaphoreType.DMA((max(1, k),)) for ride in rides for k in ride.n_sems]

    def full_body(*refs):
        ins, outs, scr = refs[:n_in + n_ein], refs[n_in + n_ein:n_in + n_ein + n_out + n_eout], refs[n_in + n_ein + n_out + n_eout:]

        def run(which):
            for idx, (ride, (r0, nr, ob0, nb, on0, nn)) in enumerate(zip(rides, where)):
                fn = ride.start if which == 0 else ride.finish
                fn(ins[n_in + r0:n_in + r0 + nr], outs[n_out + ob0:n_out + ob0 + nb], outs[n_out + on0:n_out + on0 + nn],
                   *scr[n_scr + 3 * idx:n_scr + 3 * idx + 3])

        host = lambda: body(*ins[:n_in], *outs[:n_out], *scr[:n_scr])
        if not rides:
            host()
        elif not grid:
            run(0)
            host()
            run(1)
        else:
            ids = [pl.program_id(ax) for ax in range(len(grid))]
            first = functools.reduce(jnp.logical_and, [i == 0 for i in ids])
            last = functools.reduce(jnp.logical_and, [i == g - 1 for i, g in zip(ids, grid)])
            pl.when(first)(lambda: run(0))
            host()
            pl.when(last)(lambda: run(1))

    if rides:
        params = pltpu.CompilerParams(dimension_semantics=("arbitrary",) * len(grid), vmem_limit_bytes=VMEM_LIMIT)
    else:
        params = _cp(*semantics)
    outs = pl.pallas_call(
        full_body, name=name, grid=grid,
        in_specs=in_specs + [ANY] * n_ein, out_specs=out_specs + [ANY] * n_eout,
        out_shape=out_shape + extra_out, input_output_aliases=aliases,
        scratch_shapes=scratch_shapes + sem_shapes, compiler_params=params,
    )(*operands, *extra_in)
    main = outs[0] if single else list(outs[:n_out])
    rode = [(list(outs[n_out + ob0:n_out + ob0 + nb]), list(outs[n_out + on0:n_out + on0 + nn]))
            for (_, _, ob0, nb, on0, nn) in where]
    return main, rode


def _run_rides(rides, name):
    return _hosted_call(lambda: None, [], name=name, grid=(), in_specs=[], out_specs=[], out_shape=[], rides=rides)[1]


def _ride_gather(slots, part=0, n_parts=1, span=1, stage=None):
    n = len(slots)
    halves = [a.shape[1] // 2 for a in slots]
    sizes = [hr // n_parts for hr in halves]
    assert part + span <= n_parts
    for a, hr, size in zip(slots, halves, sizes):
        assert a.shape[1] == 2 * hr and hr == size * n_parts and size % ROWS == 0, a.shape

    def remote(bufs, send_sems, recv_sems, i, k, slot, core, to):
        rows = bufs[i].at[slot, pl.ds(pl.multiple_of(core * halves[i] + part * sizes[i], ROWS), span * sizes[i])]
        return pltpu.make_async_remote_copy(
            src_ref=rows, dst_ref=rows, send_sem=send_sems.at[i * 6 + k], recv_sem=recv_sems.at[i * 6 + k],
            device_id=to, device_id_type=MESH)

    def each(fn):
        x, y, c, chips = _place()
        for i in range(n):
            for k, (px, py) in enumerate(chips):
                fn(x, y, c, i, k, px, py)

    def start(reads, bufs, new, send_sems, recv_sems, local_sems):
        cp = functools.partial(remote, bufs, send_sems, recv_sems)
        if stage != "d2d":
            each(lambda x, y, c, i, k, px, py: cp(i, k, 2 * x + y, c, (px, py, c)).start())
        else:
            each(lambda x, y, c, i, k, px, py: cp(i, 3 + k, 2 * px + py, c, (x, y, 1 - c)).start())

    def finish(reads, bufs, new, send_sems, recv_sems, local_sems):
        cp = functools.partial(remote, bufs, send_sems, recv_sems)

        def landed_over_ici(x, y, c, i, k, px, py):
            cp(i, k, 2 * px + py, c, (x, y, c)).wait_recv()
            if stage is None:
                cp(i, 3 + k, 2 * px + py, c, (x, y, 1 - c)).start()

        if stage != "d2d":
            each(landed_over_ici)
        if stage != "ici":
            each(lambda x, y, c, i, k, px, py: cp(i, 3 + k, 2 * px + py, 1 - c, (x, y, c)).wait_recv())
        if stage != "d2d":
            each(lambda x, y, c, i, k, px, py: cp(i, k, 2 * x + y, c, (px, py, c)).wait_send())
        if stage != "ici":
            each(lambda x, y, c, i, k, px, py: cp(i, 3 + k, 2 * px + py, c, (x, y, 1 - c)).wait_send())

    return _Ride([], slots, [], (6 * n, 6 * n, 0), start, finish)


def _ride_swap(grads):
    n = len(grads)
    halves = [a.shape[1] // 2 for a in grads]

    def copies(reads, new, send_sems, recv_sems):
        x, y, c, _ = _place()
        out = []
        for i in range(n):
            rows = pl.ds(pl.multiple_of((1 - c) * halves[i], 2 * SUBLANE), halves[i])
            out.append(pltpu.make_async_remote_copy(
                src_ref=reads[i].at[:, rows, :], dst_ref=new[i], send_sem=send_sems.at[i], recv_sem=recv_sems.at[i],
                device_id=(x, y, 1 - c), device_id_type=MESH))
        return out

    def start(reads, bufs, new, send_sems, recv_sems, local_sems):
        for cp in copies(reads, new, send_sems, recv_sems):
            cp.start()

    def finish(reads, bufs, new, send_sems, recv_sems, local_sems):
        for cp in copies(reads, new, send_sems, recv_sems):
            cp.wait()

    shapes = [jax.ShapeDtypeStruct((N_CHIPS, hr, a.shape[2]), a.dtype) for a, hr in zip(grads, halves)]
    return _Ride(grads, [], shapes, (n, n, 0), start, finish)


def _ride_scatter(parts, part=0, n_parts=1, into=None, span=1):
    n = len(parts)
    sizes = [a.shape[1] // n_parts for a in parts]
    assert part + span <= n_parts
    for a, size in zip(parts, sizes):
        assert a.shape[1] == size * n_parts and size % ROWS == 0, a.shape

    def piece(ref, i, slot):
        return ref.at[slot, pl.ds(part * sizes[i], span * sizes[i])]

    def own(reads, land, local_sems, i):
        me = 2 * lax.axis_index("x") + lax.axis_index("y")
        return pltpu.make_async_copy(piece(reads[i], i, me), piece(land[i], i, me), local_sems.at[i])

    def send(reads, land, send_sems, recv_sems, i, k):
        x, y, c, chips = _place()
        px, py = chips[k]
        return pltpu.make_async_remote_copy(
            src_ref=piece(reads[i], i, 2 * px + py), dst_ref=piece(land[i], i, 2 * x + y),
            send_sem=send_sems.at[3 * i + k], recv_sem=recv_sems.at[3 * i + k],
            device_id=(px, py, c), device_id_type=MESH)

    def start(reads, bufs, new, send_sems, recv_sems, local_sems):
        land = new if into is None else bufs
        for i in range(n):
            own(reads, land, local_sems, i).start()
            for k in range(3):
                send(reads, land, send_sems, recv_sems, i, k).start()

    def finish(reads, bufs, new, send_sems, recv_sems, local_sems):
        land = new if into is None else bufs
        x, y, c, chips = _place()
        for i in range(n):
            for k, (px, py) in enumerate(chips):
                slot = piece(land[i], i, 2 * px + py)
                pltpu.make_async_remote_copy(
                    src_ref=slot, dst_ref=slot, send_sem=send_sems.at[3 * i + k], recv_sem=recv_sems.at[3 * i + k],
                    device_id=(x, y, c), device_id_type=MESH).wait_recv()
        for i in range(n):
            for k in range(3):
                send(reads, land, send_sems, recv_sems, i, k).wait_send()
            own(reads, land, local_sems, i).wait()

    shapes = [jax.ShapeDtypeStruct(a.shape, a.dtype) for a in parts]
    if into is None:
        return _Ride(parts, [], shapes, (3 * n, 3 * n, n), start, finish)
    return _Ride(parts, list(into), [], (3 * n, 3 * n, n), start, finish)


def _ride_join(grads):
    n = len(grads)

    def copy(bufs, send_sems, recv_sems, i, core, to):
        hr = grads[i].shape[1] // 2
        rows = bufs[i].at[:, pl.ds(pl.multiple_of(core * hr, SUBLANE), hr), :]
        return pltpu.make_async_remote_copy(
            src_ref=rows, dst_ref=rows, send_sem=send_sems.at[i], recv_sem=recv_sems.at[i],
            device_id=to, device_id_type=MESH)

    def start(reads, bufs, new, send_sems, recv_sems, local_sems):
        x, y, c, _ = _place()
        for i in range(n):
            copy(bufs, send_sems, recv_sems, i, c, (x, y, 1 - c)).start()

    def finish(reads, bufs, new, send_sems, recv_sems, local_sems):
        x, y, c, _ = _place()
        for i in range(n):
            copy(bufs, send_sems, recv_sems, i, 1 - c, (x, y, c)).wait_recv()
        for i in range(n):
            copy(bufs, send_sems, recv_sems, i, c, (x, y, 1 - c)).wait_send()

    return _Ride([], grads, [], (n, n, 0), start, finish)


def _all_reduce_small(packed, name, rides=()):
    r, c = packed.shape
    pr = r // N_DEV
    assert r == pr * N_DEV and pr % ROWS == 0

    def body(x_ref, out_ref, parts, send1, recv1, send2, recv2):
        x, y, cc, _ = _place()
        mine = 4 * x + 2 * y + cc
        flip = lambda v, bit: 1 - v if bit else v
        peers = [(flip(x, k & 4), flip(y, k & 2), flip(cc, k & 1)) for k in range(1, N_DEV)]

        def piece(ref, dev):
            return ref.at[pl.ds(pl.multiple_of(dev * pr, ROWS), pr), :]

        def to_owner(k, peer):
            px, py, pc = peer
            return pltpu.make_async_remote_copy(
                src_ref=piece(x_ref, 4 * px + 2 * py + pc), dst_ref=parts.at[mine],
                send_sem=send1.at[k], recv_sem=recv1.at[k], device_id=peer, device_id_type=MESH)

        def to_all(k, peer):
            return pltpu.make_async_remote_copy(
                src_ref=piece(out_ref, mine), dst_ref=piece(out_ref, mine),
                send_sem=send2.at[k], recv_sem=recv2.at[k], device_id=peer, device_id_type=MESH)

        for k, peer in enumerate(peers):
            to_owner(k, peer).start()
        parts[mine] = x_ref[pl.ds(pl.multiple_of(mine * pr, ROWS), pr), :]
        for k, (px, py, pc) in enumerate(peers):
            landed = parts.at[4 * px + 2 * py + pc]
            pltpu.make_async_remote_copy(
                src_ref=landed, dst_ref=landed, send_sem=send1.at[k], recv_sem=recv1.at[k],
                device_id=(x, y, cc), device_id_type=MESH).wait_recv()
        acc = parts[0]
        for dev in range(1, N_DEV):
            acc = acc + parts[dev]
        out_ref[pl.ds(pl.multiple_of(mine * pr, ROWS), pr), :] = acc
        for k, peer in enumerate(peers):
            to_all(k, peer).start()
        for k, (px, py, pc) in enumerate(peers):
            theirs = piece(out_ref, 4 * px + 2 * py + pc)
            pltpu.make_async_remote_copy(
                src_ref=theirs, dst_ref=theirs, send_sem=send2.at[k], recv_sem=recv2.at[k],
                device_id=(x, y, cc), device_id_type=MESH).wait_recv()
        for k, peer in enumerate(peers):
            to_owner(k, peer).wait_send()
            to_all(k, peer).wait_send()

    return _hosted_call(
        body, [packed], name=name, grid=(), in_specs=[VMEM_SPEC], out_specs=VMEM_SPEC,
        out_shape=jax.ShapeDtypeStruct((r, c), F32),
        scratch_shapes=[pltpu.VMEM((N_DEV, pr, c), F32)] + [pltpu.SemaphoreType.DMA((N_DEV - 1,))] * 4,
        rides=rides)


_PACK_ROWS = 256


def _pack(arrays):
    flat = jnp.concatenate([a.reshape(-1).astype(F32) for a in arrays])
    unit = _PACK_ROWS * LANE
    total = -(-flat.shape[0] // unit) * unit
    return jnp.pad(flat, (0, total - flat.shape[0])).reshape(-1, LANE)


def _unpack(packed, shapes, lead=()):
    flat = packed.reshape(lead + (-1,))
    out, at = [], 0
    for s in shapes:
        size = math.prod(s)
        out.append(flat[..., at:at + size].reshape(lead + tuple(s)))
        at += size
    return out


def kernel(x, pre_mix_g, post_mix_g, pre_ffn_g, post_ffn_g, a_w_in, a_v_norm_g, a_w_spatial, a_b_spatial, a_w_out, kv_norm_g, w_k, w_v, b_w_q, b_w_o, ffn_w_up, ffn_conv_w, ffn_conv_b, ffn_w_down, loss_target, m_pre_mix_g, m_post_mix_g, m_pre_ffn_g, m_post_ffn_g, m_a_w_in, m_a_v_norm_g, m_a_w_spatial, m_a_b_spatial, m_a_w_out, m_kv_norm_g, m_w_k, m_w_v, m_b_w_q, m_b_w_o, m_ffn_w_up, m_ffn_conv_w, m_ffn_conv_b, m_ffn_w_down, v_pre_mix_g, v_post_mix_g, v_pre_ffn_g, v_post_ffn_g, v_a_w_in, v_a_v_norm_g, v_a_w_spatial, v_a_b_spatial, v_a_w_out, v_kv_norm_g, v_w_k, v_w_v, v_b_w_q, v_b_w_o, v_ffn_w_up, v_ffn_conv_w, v_ffn_conv_b, v_ffn_w_down):
    xi, yi, ci = lax.axis_index("x"), lax.axis_index("y"), lax.axis_index("c")
    chip = 2 * xi + yi
    c_idx = jnp.reshape(ci, (1,)).astype(jnp.int32)
    _, s, d = x.shape
    n_layers = pre_mix_g.shape[0]
    assert n_layers == 2 and a_w_in.shape[0] == 1 and b_w_q.shape[0] == 1
    d_a = a_w_out.shape[1] * N_CHIPS
    n_g = a_w_spatial.shape[1]
    ns = ffn_w_up.shape[2]
    assert a_w_spatial.shape[2] == TILE and d_a == n_g * TILE and s % TILE == 0
    h0 = x[0]
    target = loss_target[0]

    big = {
        "win": (a_w_in, m_a_w_in, v_a_w_in),
        "wout": (a_w_out, m_a_w_out, v_a_w_out),
        "wk": (w_k[None], m_w_k[None], v_w_k[None]),
        "wv": (w_v[None], m_w_v[None], v_w_v[None]),
        "wq": (b_w_q, m_b_w_q, v_b_w_q),
        "wo": (b_w_o, m_b_w_o, v_b_w_o),
        "wup": (ffn_w_up, m_ffn_w_up, v_ffn_w_up),
        "wdn": (ffn_w_down, m_ffn_w_down, v_ffn_w_down),
    }
    units = [(nm, layer) for nm in big for layer in range(big[nm][0].shape[0])]
    chip_idx = jnp.reshape(chip, (1,)).astype(jnp.int32)
    shards = [_cast_bf16(big[nm][0], layer, chip_idx, f"cast_{nm}{layer}") for nm, layer in units]
    small_sharded = _pack([a_v_norm_g, ffn_conv_w])
    small_sharded = lax.dynamic_update_index_in_dim(
        jnp.zeros((N_CHIPS,) + small_sharded.shape, F32), small_sharded, chip, 0)
    own = dict(zip(units, shards))
    full = {}

    def gather_ride(keys):
        return _ride_gather([own[key] for key in keys])

    def gathered(keys, rode):
        full.update(zip(keys, rode[0]))

    gains = lambda g, layer: g[layer:layer + 1]
    first_keys = [("win", 0)]
    (first_bufs, _), = _run_rides(
        [_ride_gather([own[key] for key in first_keys] + [small_sharded], stage="ici")], "gather_first")
    hn0, ((first_bufs, _),) = _rms_fwd(
        h0, gains(pre_mix_g, 0), "norm_in", rides=[_ride_gather(first_bufs, stage="d2d")])
    full.update(zip(first_keys, first_bufs[:-1]))
    vg_parts, cw_parts = _unpack(first_bufs[-1], [a_v_norm_g.shape, ffn_conv_w.shape], lead=(N_CHIPS,))
    v_g = jnp.transpose(vg_parts, (1, 0, 2)).reshape(1, d_a)

    def rows(nm, layer=0):
        w = full[(nm, layer)]
        return w.reshape(w.shape[0] * w.shape[1], w.shape[2])

    bias = jnp.repeat(a_b_spatial[0].T, TILE, axis=1)
    w_s = a_w_spatial[0]
    kv_g = kv_norm_g[None]
    conv_w = [cw_parts[:, layer] for layer in range(n_layers)]
    conv_b = [ffn_conv_b[layer].reshape(N_CHIPS, 1, ns) for layer in range(n_layers)]

    up0 = own[("wup", 0)]
    pieces = lambda p, span: _ride_gather([up0], part=p, n_parts=8, span=span)
    uv, ((out_bufs, _), ((up0,), _)) = _mm(
        hn0, full[("win", 0)], "nn", "gmlp_in", out_split=N_CHIPS, rides=[gather_ride([("wout", 0)]), pieces(0, 1)])
    full[("wout", 0)] = out_bufs[0]
    gm, (((up0,), _),) = _gmlp_fwd(uv, v_g, w_s, bias, "gmlp_gate", rides=[pieces(1, 2)])
    mix0, (((up0,), _),) = _mm(gm, rows("wout"), "nn", "gmlp_out", rides=[pieces(3, 2)])
    mix0 = mix0[0]
    (h1, hn1), (((up0,), _),) = _resid_rms(
        h0, mix0, gains(post_mix_g, 0), [gains(pre_ffn_g, 0)], "resid_mix0", rides=[pieces(5, 3)])
    full[("wup", 0)] = up0
    def leg(keys, stage):
        return _ride_gather([own[key] for key in keys], stage=stage)

    def first_leg_done(keys, rode):
        own.update(zip(keys, rode[0]))

    down0, qk, vo = [("wdn", 0)], [("wq", 0), ("wk", 0)], [("wv", 0), ("wo", 0)]
    a0, (rode,) = _mm(hn1, full[("wup", 0)], "nn", "ffn_up0", out_split=N_CHIPS, rides=[leg(down0, "ici")])
    first_leg_done(down0, rode)
    hm0, (rode, rode_qk) = _ffn_act_fwd(
        a0, conv_w[0], conv_b[0], "ffn_act0", rides=[leg(down0, "d2d"), leg(qk, "ici")])
    gathered(down0, rode)
    first_leg_done(qk, rode_qk)
    f0, (rode, rode_vo) = _mm(hm0, rows("wdn", 0), "nn", "ffn_down0", rides=[leg(qk, "d2d"), leg(vo, "ici")])
    gathered(qk, rode)
    first_leg_done(vo, rode_vo)
    f0 = f0[0]
    (h2, hn2, kvn), (rode,) = _resid_rms(
        h1, f0, gains(post_ffn_g, 0), [gains(pre_mix_g, 1), kv_g], "resid_ffn0", rides=[leg(vo, "d2d")])
    gathered(vo, rode)
    q = _mm(hn2, rows("wq"), "nn", "proj_q", out_dtype=BF16)[0]
    k = _mm(kvn, rows("wk"), "nn", "proj_k", out_dtype=BF16)[0]
    v = _mm(kvn, rows("wv"), "nn", "proj_v", out_dtype=BF16)[0]
    last_keys = [("wup", 1), ("wdn", 1)]
    (att, lsum), (rode,) = _attn_fwd(q, k, v, "attn_fwd", rides=[leg(last_keys, "ici")])
    first_leg_done(last_keys, rode)
    mix1, (rode,) = _mm(att, rows("wo"), "nn", "proj_o", rides=[leg(last_keys, "d2d")])
    gathered(last_keys, rode)
    mix1 = mix1[0]
    h3, hn3 = _resid_rms(h2, mix1, gains(post_mix_g, 1), [gains(pre_ffn_g, 1)], "resid_mix1")
    a1 = _mm(hn3, full[("wup", 1)], "nn", "ffn_up1", out_split=N_CHIPS)
    hm1 = _ffn_act_fwd(a1, conv_w[1], conv_b[1], "ffn_act1")
    f1 = _mm(hm1, rows("wdn", 1), "nn", "ffn_down1")[0]
    dh4, loss_tile = _loss_head(h3, f1, gains(post_ffn_g, 1), target, "loss_head")
    loss = lax.psum(loss_tile[0, 0], ("x", "y", "c"))

    dw = {}
    dg = {}

    pair = {}
    half_done = {nm: None for nm in big}

    def swap_ride(keys):
        return _ride_swap([dw[key] for key in keys])

    def swapped(keys, rode):
        for (nm, layer), got in zip(keys, rode[1]):
            pair[(nm, layer)] = _pair_add(dw[(nm, layer)], got, c_idx, f"pair_add_{nm}{layer}")

    def scatter_ride(keys):
        return _ride_scatter([pair[key] for key in keys])

    def scattered(keys, rode):
        for (nm, layer), got in zip(keys, rode[1]):
            half_done[nm] = _chip_sum(got, half_done[nm], big[nm][0].shape, layer, c_idx, f"chip_sum_{nm}{layer}")

    def ffn_bwd(dh_out, h_in, hn, a, hm, f, layer, act_rides=()):
        df, dg[("post_ffn", layer)] = _rms_bwd_out(dh_out, f, gains(post_ffn_g, layer), f"d_norm_ffn_out{layer}")
        dwd = _mm(hm, df, "tn", f"d_w_down{layer}", out_dtype=BF16)[0]
        down, up = [("wdn", layer)], [("wup", layer)]
        dw[down[0]] = dwd.reshape(N_CHIPS, dwd.shape[0] // N_CHIPS, d)
        dhm, (rode,) = _mm(df, rows("wdn", layer), "nt", f"d_ffn_mid{layer}", out_split=2, rides=[swap_ride(down)])
        swapped(down, rode)
        (da, dg[("conv_w", layer)], dg[("conv_b", layer)]), act_rode = _ffn_act_bwd(
            a, dhm, conv_w[layer], conv_b[layer], f"d_ffn_act{layer}", rides=act_rides)
        dw[up[0]], (rode,) = _mm(hn, da, "tn", f"d_w_up{layer}", out_dtype=BF16, out_split=N_CHIPS,
                                 rides=[scatter_ride(down)])
        scattered(down, rode)
        dhn, (rode,) = _mm(da, full[("wup", layer)], "nt", f"d_ffn_in{layer}", rides=[swap_ride(up)])
        swapped(up, rode)
        return dhn[0], act_rode

    dhn3, _ = ffn_bwd(dh4, h3, hn3, a1, hm1, f1, 1)
    dh3, (dg[("pre_ffn", 1)],) = _rms_bwd_in(dh4, h3, [([dhn3], gains(pre_ffn_g, 1))], "d_norm_ffn_in1")
    dmix1, dg[("post_mix", 1)] = _rms_bwd_out(dh3, mix1, gains(post_mix_g, 1), "d_norm_mix_out1")
    dwo = _mm(att, dmix1, "tn", "d_w_o", out_dtype=BF16)[0]
    dw[("wo", 0)] = dwo.reshape(N_CHIPS, dwo.shape[0] // N_CHIPS, d)
    datt = _mm(dmix1, rows("wo"), "nt", "d_attn_out", out_dtype=BF16)[0]
    ffn1_keys = [("wup", 1)]
    (dq, dk, dv), (rode,) = _attn_bwd(q, k, v, datt, lsum, "attn_bwd", rides=[scatter_ride(ffn1_keys)])
    scattered(ffn1_keys, rode)
    for nm, act, dact in (("wq", hn2, dq), ("wk", kvn, dk), ("wv", kvn, dv)):
        g = _mm(act, dact, "tn", f"d_{nm}", out_dtype=BF16)[0]
        dw[(nm, 0)] = g.reshape(N_CHIPS, g.shape[0] // N_CHIPS, g.shape[1])
    dhn2 = _mm(dq, rows("wq"), "nt", "d_q_in")[0]
    dkvn_k = _mm(dk, rows("wk"), "nt", "d_k_in")[0]
    attn_keys = [("wo", 0), ("wq", 0), ("wk", 0), ("wv", 0)]
    dkvn_v, (rode,) = _mm(dv, rows("wv"), "nt", "d_v_in", rides=[swap_ride(attn_keys)])
    swapped(attn_keys, rode)
    dh2, (dg[("pre_mix", 1)], dg["kv"]) = _rms_bwd_in(
        dh3, h2, [([dhn2], gains(pre_mix_g, 1)), ([dkvn_k, dkvn_v[0]], kv_g)], "d_norm_mix_in1")
    dhn1, (rode,) = ffn_bwd(dh2, h1, hn1, a0, hm0, f0, 0, act_rides=[scatter_ride(attn_keys)])
    scattered(attn_keys, rode)
    up0_pair = [pair[("wup", 0)]]
    up0_landed = [None]

    def up0_piece(part, span):
        return _ride_scatter(up0_pair, part, 8, into=up0_landed[0], span=span)

    def up0_rode(rode):
        up0_landed[0] = rode[1] if up0_landed[0] is None else rode[0]

    dh1, (dg[("pre_ffn", 0)],), (rode,) = _rms_bwd_in(
        dh2, h1, [([dhn1], gains(pre_ffn_g, 0))], "d_norm_ffn_in0", rides=[up0_piece(0, 1)])
    up0_rode(rode)
    dmix0, dg[("post_mix", 0)], (rode,) = _rms_bwd_out(
        dh1, mix0, gains(post_mix_g, 0), "d_norm_mix_out0", rides=[up0_piece(1, 1)])
    up0_rode(rode)
    early = ["wq", "wk", "wv", "wo", "wdn"]
    dwout, (((joined_early, _)),) = _mm(
        gm, dmix0, "tn", "d_w_out", out_dtype=BF16, rides=[_ride_join([half_done[nm] for nm in early])])
    grads_big = dict(zip(early, joined_early))
    w_out_key, w_in_key = [("wout", 0)], [("win", 0)]
    dw[w_out_key[0]] = dwout[0].reshape(N_CHIPS, dwout.shape[1] // N_CHIPS, d)
    dgm, (rode, up0) = _mm(dmix0, rows("wout"), "nt", "d_gmlp_gate", rides=[swap_ride(w_out_key), up0_piece(2, 1)])
    swapped(w_out_key, rode)
    up0_rode(up0)
    (duv, d_ws, d_bs, d_vg), (rode,) = _gmlp_bwd(uv, dgm[0], v_g, w_s, bias, "d_gmlp", rides=[up0_piece(3, 2)])
    up0_rode(rode)
    dw[w_in_key[0]], (rode, up0) = _mm(
        hn0, duv, "tn", "d_w_in", out_dtype=BF16, out_split=N_CHIPS, rides=[scatter_ride(w_out_key), up0_piece(5, 1)])
    scattered(w_out_key, rode)
    up0_rode(up0)
    dhn0, (rode, up0) = _mm(
        duv, full[("win", 0)], "nt", "d_gmlp_in", rides=[swap_ride(w_in_key), up0_piece(6, 2)])
    swapped(w_in_key, rode)
    up0_rode(up0)
    scattered([("wup", 0)], (None, up0_landed[0]))
    dx, (dg[("pre_mix", 0)],), (rode,) = _rms_bwd_in(
        dh1, h0, [([dhn0[0]], gains(pre_mix_g, 0))], "d_norm_in", rides=[scatter_ride(w_in_key)])
    scattered(w_in_key, rode)

    stack = lambda key: jnp.concatenate([dg[(key, layer)] for layer in range(n_layers)], axis=0)
    small_parts = [
        stack("pre_mix"), stack("post_mix"), stack("pre_ffn"), stack("post_ffn"),
        d_vg, d_ws, d_bs[::SUBLANE], dg["kv"],
        jnp.stack([dg[("conv_w", layer)] for layer in range(n_layers)]),
        jnp.stack([dg[("conv_b", layer)] for layer in range(n_layers)]),
    ]
    late = [nm for nm in big if nm not in early]
    summed, ((joined_late, _),) = _all_reduce_small(
        _pack(small_parts), "small_grads_sum", rides=[_ride_join([half_done[nm] for nm in late])])
    grads_big.update(zip(late, joined_late))
    (g_pre_mix, g_post_mix, g_pre_ffn, g_post_ffn, g_vg, g_ws, g_bs, g_kv, g_cw, g_cb) = _unpack(
        summed, [p.shape for p in small_parts])
    g_vg = lax.dynamic_index_in_dim(g_vg.reshape(N_CHIPS, 1, d_a // N_CHIPS), chip, 0, keepdims=False)
    g_cw = lax.dynamic_index_in_dim(g_cw, chip, 1, keepdims=False)
    g_cb = g_cb.reshape(n_layers, N_CHIPS * ns)
    small = [
        (pre_mix_g, g_pre_mix, m_pre_mix_g, v_pre_mix_g),
        (post_mix_g, g_post_mix, m_post_mix_g, v_post_mix_g),
        (pre_ffn_g, g_pre_ffn, m_pre_ffn_g, v_pre_ffn_g),
        (post_ffn_g, g_post_ffn, m_post_ffn_g, v_post_ffn_g),
        (a_v_norm_g, g_vg, m_a_v_norm_g, v_a_v_norm_g),
        (a_w_spatial, g_ws[None], m_a_w_spatial, v_a_w_spatial),
        (a_b_spatial, g_bs[None], m_a_b_spatial, v_a_b_spatial),
        (kv_norm_g, g_kv.reshape(d), m_kv_norm_g, v_kv_norm_g),
        (ffn_conv_w, g_cw, m_ffn_conv_w, v_ffn_conv_w),
        (ffn_conv_b, g_cb, m_ffn_conv_b, v_ffn_conv_b),
    ]
    small = [(w, g.reshape(w.shape), m, v) for w, g, m, v in small]
    packed = [_pack([t[i] for t in small])[None] for i in range(4)]
    small_new = [_unpack(p[0], [t[0].shape for t in small]) for p in _adamw(*packed, "adamw_small")]

    new_big = {nm: _adamw(big[nm][0], grads_big[nm], big[nm][1], big[nm][2], f"adamw_{nm}", pass_g=True)
               for nm in big}

    def big_out(nm, which):
        ref_shape = {"wk": w_k.shape, "wv": w_v.shape}.get(nm, big[nm][0].shape)
        return new_big[nm][which].reshape(ref_shape)

    order = ["pre_mix", "post_mix", "pre_ffn", "post_ffn", "win", "vg", "ws", "bs", "wout", "kv", "wk", "wv", "wq",
             "wo", "wup", "cw", "cb", "wdn"]
    small_at = {"pre_mix": 0, "post_mix": 1, "pre_ffn": 2, "post_ffn": 3, "vg": 4, "ws": 5, "bs": 6, "kv": 7,
                "cw": 8, "cb": 9}
    outs = [loss, dx[None]]
    for which in range(4):
        for nm in order:
            if nm in small_at:
                outs.append(small[small_at[nm]][1] if which == 0 else small_new[which - 1][small_at[nm]])
            else:
                outs.append(big_out(nm, which))
    return tuple(outs)
```

```python
import functools
import math

import jax
import jax.numpy as jnp
from jax import lax
from jax.experimental import pallas as pl
from jax.experimental.pallas import tpu as pltpu

F32 = jnp.float32
BF16 = jnp.bfloat16
EPS = 1e-6
ADAM_LR = 0.001
ADAM_B1 = 0.9
ADAM_B2 = 0.999
ADAM_EPS = 1e-08
ADAM_WD = 0.01
ADAM_STEP = 10

LANE = 128
SUBLANE = 8
ROWS = 16
TILE = 128
N_CHIPS = 4
N_DEV = 8
VMEM_LIMIT = 56 * 1024 * 1024
MM_VMEM = 46 * 1024 * 1024
MXU_WIDTH = 256
MESH = pl.DeviceIdType.MESH
ANY = pl.BlockSpec(memory_space=pl.ANY)
VMEM_SPEC = pl.BlockSpec(memory_space=pltpu.VMEM)


def _cp(*sem):
    return pltpu.CompilerParams(dimension_semantics=sem, vmem_limit_bytes=VMEM_LIMIT)


def _pick(dim, pref, align=LANE):
    if dim <= pref:
        return dim
    best = None
    for d in range(align, pref + 1, align):
        if dim % d == 0:
            best = d
    assert best is not None, (dim, pref)
    return best


_DIMS = {
    "nn": (((1,), (0,)), ((), ())),
    "nt": (((1,), (1,)), ((), ())),
    "tn": (((0,), (0,)), ((), ())),
}


def _as3(a):
    return a if a.ndim == 3 else a[None]


def _spec3(br, bc, cols_j, rc):
    per = cols_j // bc

    def imap(m, n, k):
        r, c = rc(m, n, k)
        return (c // per, r, c % per)

    return pl.BlockSpec((None, br, bc), imap)


def _mm(a, b, mode, name, out_dtype=F32, out_split=1, rides=()):
    a, b = _as3(a), _as3(b)
    ja, ra, caj = a.shape
    jb, rb, cbj = b.shape
    if mode == "nn":
        m, k, n = ra, ja * caj, jb * cbj
        assert rb == k
        m_ext, k_ext, n_ext = [ra], [caj, rb], [cbj]
    elif mode == "nt":
        m, k, n = ra, ja * caj, rb
        assert jb * cbj == k
        m_ext, k_ext, n_ext = [ra], [caj, cbj], [rb]
    else:
        m, k, n = ja * caj, ra, jb * cbj
        assert rb == k
        m_ext, k_ext, n_ext = [caj], [ra], [cbj]
    assert n % out_split == 0
    n_ext.append(n // out_split)
    bm = _pick(math.gcd(*m_ext), 1536)
    n_unit = math.gcd(*n_ext)
    bn = _pick(n_unit, 1536)
    k_unit = math.gcd(*k_ext)
    o_bytes = jnp.dtype(out_dtype).itemsize

    def vmem_need(bm, bn, bk):
        tiles = bm * bk * a.dtype.itemsize + bk * bn * b.dtype.itemsize + bm * bn * o_bytes
        return 2 * tiles + bm * bn * 4 * (2 if bk < k else 1)

    def deepest(bm, bn):
        return max(d for d in range(LANE, k_unit + 1, LANE)
                   if k_unit % d == 0 and (d == LANE or vmem_need(bm, bn, d) <= MM_VMEM))

    bk = deepest(bm, bn)
    if bk < k_unit and k_unit == k:
        if bm % (2 * LANE) == 0 and deepest(bm // 2, bn) == k:
            bm, bk = bm // 2, k
        elif bn % (2 * LANE) == 0 and deepest(bm, bn // 2) == k:
            bn, bk = bn // 2, k
    n_outer = False
    if bn % MXU_WIDTH and n_unit % MXU_WIDTH == 0 and k_unit == k:
        for rows in (bm, bm // 2, bm // 4):
            if rows % LANE == 0 and vmem_need(rows, n_unit, k) <= MM_VMEM:
                bm, bn, bk, n_outer = rows, n_unit, k, True
                break
        else:
            if k % (2 * LANE) == 0 and vmem_need(bm, n_unit, k // 2) <= MM_VMEM:
                bn, bk, n_outer = n_unit, k // 2, True
    nk = k // bk
    order = (lambda f: lambda ni, mi, ki: f(mi, ni, ki)) if n_outer else (lambda f: f)
    if mode == "nn":
        a_spec = _spec3(bm, bk, caj, order(lambda mi, ni, ki: (mi, ki)))
        b_spec = _spec3(bk, bn, cbj, order(lambda mi, ni, ki: (ki, ni)))
    elif mode == "nt":
        a_spec = _spec3(bm, bk, caj, order(lambda mi, ni, ki: (mi, ki)))
        b_spec = _spec3(bn, bk, cbj, order(lambda mi, ni, ki: (ni, ki)))
    else:
        a_spec = _spec3(bk, bm, caj, order(lambda mi, ni, ki: (ki, mi)))
        b_spec = _spec3(bk, bn, cbj, order(lambda mi, ni, ki: (ki, ni)))
    o_spec = _spec3(bm, bn, n // out_split, order(lambda mi, ni, ki: (mi, ni)))
    dims = _DIMS[mode]

    def body(a_ref, b_ref, o_ref, *acc):
        def part():
            return lax.dot_general(a_ref[...].astype(BF16), b_ref[...].astype(BF16), dims, preferred_element_type=F32)

        if nk == 1:
            o_ref[...] = part().astype(o_ref.dtype)
            return
        acc_ref, = acc
        ki = pl.program_id(2)

        @pl.when(ki == 0)
        def _():
            acc_ref[...] = part()

        @pl.when(jnp.logical_and(ki > 0, ki < nk - 1))
        def _():
            acc_ref[...] += part()

        @pl.when(ki == nk - 1)
        def _():
            o_ref[...] = (acc_ref[...] + part()).astype(o_ref.dtype)

    grid = (n // bn, m // bm, nk) if n_outer else (m // bm, n // bn, nk)
    out, rode = _hosted_call(
        body, [a, b], name=name, grid=grid, in_specs=[a_spec, b_spec], out_specs=o_spec,
        out_shape=jax.ShapeDtypeStruct((out_split, m, n // out_split), out_dtype),
        scratch_shapes=[pltpu.VMEM((bm, bn), F32)] if nk > 1 else [],
        semantics=("parallel", "parallel", "arbitrary"), rides=rides)
    return (out, rode) if rides else out


def _rms(x, g):
    r = lax.rsqrt(jnp.mean(x * x, axis=-1, keepdims=True) + EPS)
    return x * r * g


def _rms_bwd(x, g, dy):
    r = lax.rsqrt(jnp.mean(x * x, axis=-1, keepdims=True) + EPS)
    xh = x * r
    gy = dy * g
    dx = r * (gy - xh * jnp.mean(gy * xh, axis=-1, keepdims=True))
    return dx, jnp.sum(dy * xh, axis=0, keepdims=True)


def _row_block(s, streams):
    return _pick(s, 512 if streams <= 4 else 256, ROWS)


def _rms_fwd(h, g, name, rides=()):
    s, d = h.shape
    br = _row_block(s, 2)

    def body(h_ref, g_ref, o_ref):
        o_ref[...] = _rms(h_ref[...], g_ref[...]).astype(BF16)

    row = pl.BlockSpec((br, d), lambda i: (i, 0))
    vec = pl.BlockSpec((1, d), lambda i: (0, 0))
    out, rode = _hosted_call(
        body, [h, g], name=name, grid=(s // br,), in_specs=[row, vec], out_specs=row,
        out_shape=jax.ShapeDtypeStruct((s, d), BF16), semantics=("parallel",), rides=rides)
    return (out, rode) if rides else out


def _resid_rms(h_in, f, g_post, g_next, name, rides=()):
    s, d = h_in.shape
    br = _row_block(s, 3 + (len(g_next) + 1) // 2)
    n_next = len(g_next)

    def body(h_ref, f_ref, gp_ref, *refs):
        gn_refs, ho_ref, hn_refs = refs[:n_next], refs[n_next], refs[n_next + 1:]
        h = h_ref[...] + _rms(f_ref[...], gp_ref[...])
        ho_ref[...] = h
        for gn_ref, hn_ref in zip(gn_refs, hn_refs):
            hn_ref[...] = _rms(h, gn_ref[...]).astype(BF16)

    row = pl.BlockSpec((br, d), lambda i: (i, 0))
    vec = pl.BlockSpec((1, d), lambda i: (0, 0))
    outs, rode = _hosted_call(
        body, [h_in, f, g_post, *g_next], name=name, grid=(s // br,),
        in_specs=[row, row, vec] + [vec] * n_next,
        out_specs=[row] * (1 + n_next),
        out_shape=[jax.ShapeDtypeStruct((s, d), F32)] + [jax.ShapeDtypeStruct((s, d), BF16)] * n_next,
        semantics=("parallel",), rides=rides)
    return (outs, rode) if rides else outs


def _loss_head(h_in, f, g_post, target, name):
    s, d = h_in.shape
    br = _row_block(s, 4)

    def body(h_ref, f_ref, gp_ref, t_ref, dh_ref, loss_ref):
        @pl.when(pl.program_id(0) == 0)
        def _():
            loss_ref[...] = jnp.zeros_like(loss_ref)

        diff = h_ref[...] + _rms(f_ref[...], gp_ref[...]) - t_ref[...]
        dh_ref[...] = diff * (1.0 / d)
        loss_ref[...] += 0.5 * jnp.sum(jnp.mean(diff * diff, axis=-1, keepdims=True))

    row = pl.BlockSpec((br, d), lambda i: (i, 0))
    vec = pl.BlockSpec((1, d), lambda i: (0, 0))
    return pl.pallas_call(
        body, name=name, grid=(s // br,),
        in_specs=[row, row, vec, row],
        out_specs=[row, pl.BlockSpec((SUBLANE, LANE), lambda i: (0, 0))],
        out_shape=[jax.ShapeDtypeStruct((s, d), F32), jax.ShapeDtypeStruct((SUBLANE, LANE), F32)],
        compiler_params=_cp("arbitrary"),
    )(h_in, f, g_post, target)


def _rms_bwd_out(dy, f, g, name, rides=()):
    s, d = f.shape
    br = _row_block(s, 3)

    def body(dy_ref, f_ref, g_ref, df_ref, dg_ref):
        @pl.when(pl.program_id(0) == 0)
        def _():
            dg_ref[...] = jnp.zeros_like(dg_ref)

        dx, dg = _rms_bwd(f_ref[...], g_ref[...], dy_ref[...])
        df_ref[...] = dx.astype(BF16)
        dg_ref[...] += dg

    row = pl.BlockSpec((br, d), lambda i: (i, 0))
    vec = pl.BlockSpec((1, d), lambda i: (0, 0))
    (df, dg), rode = _hosted_call(
        body, [dy, f, g], name=name, grid=(s // br,), in_specs=[row, row, vec], out_specs=[row, vec],
        out_shape=[jax.ShapeDtypeStruct((s, d), BF16), jax.ShapeDtypeStruct((1, d), F32)],
        semantics=("arbitrary",), rides=rides)
    return (df, dg, rode) if rides else (df, dg)


def _rms_bwd_in(dh_out, h_in, branches, name, rides=()):
    s, d = h_in.shape
    br = _row_block(s, 3 + sum(len(ds) for ds, _ in branches))
    counts = [len(ds) for ds, _ in branches]
    n_d = sum(counts)
    n_b = len(branches)

    def body(dho_ref, h_ref, *refs):
        d_refs, g_refs = refs[:n_d], refs[n_d:n_d + n_b]
        dh_ref, dg_refs = refs[n_d + n_b], refs[n_d + n_b + 1:]

        @pl.when(pl.program_id(0) == 0)
        def _():
            for r in dg_refs:
                r[...] = jnp.zeros_like(r)

        h = h_ref[...]
        acc = dho_ref[...]
        at = 0
        for bi, cnt in enumerate(counts):
            dn = d_refs[at][...]
            for r in d_refs[at + 1:at + cnt]:
                dn = dn + r[...]
            at += cnt
            dx, dg = _rms_bwd(h, g_refs[bi][...], dn)
            acc = acc + dx
            dg_refs[bi][...] += dg
        dh_ref[...] = acc

    row = pl.BlockSpec((br, d), lambda i: (i, 0))
    vec = pl.BlockSpec((1, d), lambda i: (0, 0))
    flat_d = [x for ds, _ in branches for x in ds]
    outs, rode = _hosted_call(
        body, [dh_out, h_in, *flat_d, *[g for _, g in branches]], name=name, grid=(s // br,),
        in_specs=[row, row] + [row] * n_d + [vec] * n_b,
        out_specs=[row] + [vec] * n_b,
        out_shape=[jax.ShapeDtypeStruct((s, d), F32)] + [jax.ShapeDtypeStruct((1, d), F32)] * n_b,
        semantics=("arbitrary",), rides=rides)
    return (outs[0], list(outs[1:]), rode) if rides else (outs[0], list(outs[1:]))


def _split3(x):
    x0 = x.astype(BF16)
    r1 = x - x0.astype(F32)
    x1 = r1.astype(BF16)
    x2 = (r1 - x1.astype(F32)).astype(BF16)
    return x0, x1, x2


def _tri(n, kind):
    r = lax.broadcasted_iota(jnp.int32, (n, n), 0)
    c = lax.broadcasted_iota(jnp.int32, (n, n), 1)
    m = {"lt": r < c, "le": r <= c, "gt": r > c}[kind]
    return jnp.where(m, 1.0, 0.0).astype(BF16)


_GELU_C = math.sqrt(2.0 / math.pi)
_GELU_A = 0.044715


def _gelu(x):
    return 0.5 * x * (1.0 + jnp.tanh(_GELU_C * (x + _GELU_A * (x * x * x))))


def _gelu_grad(x):
    t = jnp.tanh(_GELU_C * (x + _GELU_A * (x * x * x)))
    return 0.5 * (1.0 + t) + 0.5 * x * (1.0 - t * t) * (_GELU_C * (1.0 + 3.0 * _GELU_A * (x * x)))


def _causal_w(w):
    r = lax.broadcasted_iota(jnp.int32, (TILE, TILE), 0)
    c = lax.broadcasted_iota(jnp.int32, (TILE, TILE), 1)
    return jnp.where(c <= r, w, 0.0)


def _uv_tiles(uv_ref, g, d_a, dq):
    cu, cv = g * TILE, d_a + g * TILE
    u = uv_ref[cu // dq, :, pl.ds(cu % dq, TILE)]
    v = uv_ref[cv // dq, :, pl.ds(cv % dq, TILE)]
    return u, v


def _gmlp_fwd(uv, v_g, w_s, bias, name, rides=()):
    _, s, dq = uv.shape
    d_a = 2 * dq
    n_g = d_a // TILE

    def body(uv_ref, vg_ref, ws_ref, b_ref, o_ref):
        for g in range(n_g):
            up, vp = _uv_tiles(uv_ref, g, d_a, dq)
            cols = pl.ds(g * TILE, TILE)
            vn = _rms(_gelu(vp), vg_ref[:, cols])
            mixed = jnp.dot(_causal_w(ws_ref[g]).astype(BF16), vn.astype(BF16), preferred_element_type=F32) + b_ref[:, cols]
            o_ref[:, cols] = (_gelu(up) * mixed).astype(BF16)

    return _hosted_call(
        body, [uv, v_g, w_s, bias], name=name, grid=(s // TILE,),
        in_specs=[
            pl.BlockSpec((4, TILE, dq), lambda i: (0, i, 0)),
            pl.BlockSpec((1, d_a), lambda i: (0, 0)),
            pl.BlockSpec((n_g, TILE, TILE), lambda i: (0, 0, 0)),
            pl.BlockSpec((TILE, d_a), lambda i: (0, 0)),
        ],
        out_specs=pl.BlockSpec((TILE, d_a), lambda i: (i, 0)),
        out_shape=jax.ShapeDtypeStruct((s, d_a), BF16),
        semantics=("parallel",), rides=rides)


def _gmlp_bwd(uv, dgm, v_g, w_s, bias, name, rides=()):
    _, s, dq = uv.shape
    d_a = 2 * dq
    n_g = d_a // TILE
    n_c = s // TILE

    def body(uv_ref, d_ref, vg_ref, ws_ref, b_ref, duv_ref, dws_ref, dbs_ref, dvg_ref, dbias_acc):
        i = pl.program_id(0)

        @pl.when(i == 0)
        def _():
            dws_ref[...] = jnp.zeros_like(dws_ref)
            dvg_ref[...] = jnp.zeros_like(dvg_ref)
            dbias_acc[...] = jnp.zeros_like(dbias_acc)

        for g in range(n_g):
            up, vp = _uv_tiles(uv_ref, g, d_a, dq)
            cols = pl.ds(g * TILE, TILE)
            vg = vg_ref[:, cols]
            u = _gelu(up)
            v = _gelu(vp)
            r = lax.rsqrt(jnp.mean(v * v, axis=-1, keepdims=True) + EPS)
            vh = v * r
            vn = (vh * vg).astype(BF16)
            wc = _causal_w(ws_ref[g]).astype(BF16)
            mixed = jnp.dot(wc, vn, preferred_element_type=F32) + b_ref[:, cols]
            d_out = d_ref[:, cols]
            du = d_out * mixed
            dmixed = d_out * u
            dmb = dmixed.astype(BF16)
            dvn = lax.dot_general(wc, dmb, _DIMS["tn"], preferred_element_type=F32)
            dws_ref[g] += lax.dot_general(dmb, vn, _DIMS["nt"], preferred_element_type=F32)
            dbias_acc[:, cols] += dmixed
            dvg_ref[:, cols] += jnp.sum(dvn * vh, axis=0, keepdims=True)
            gv = dvn * vg
            dv = r * (gv - vh * jnp.mean(gv * vh, axis=-1, keepdims=True))
            cu, cv = g * TILE, d_a + g * TILE
            duv_ref[cu // dq, :, pl.ds(cu % dq, TILE)] = (du * _gelu_grad(up)).astype(BF16)
            duv_ref[cv // dq, :, pl.ds(cv % dq, TILE)] = (dv * _gelu_grad(vp)).astype(BF16)

        @pl.when(i == n_c - 1)
        def _():
            ones = jnp.ones((SUBLANE, TILE), BF16)
            for g in range(n_g):
                dws_ref[g] = _causal_w(dws_ref[g])
                cols = pl.ds(g * TILE, TILE)
                out = None
                for t in _split3(dbias_acc[:, cols]):
                    p = lax.dot_general(ones, t, _DIMS["nt"], preferred_element_type=F32)
                    out = p if out is None else out + p
                dbs_ref[pl.ds(g * SUBLANE, SUBLANE), :] = out

    return _hosted_call(
        body, [uv, dgm, v_g, w_s, bias], name=name, grid=(n_c,), semantics=("arbitrary",), rides=rides,
        in_specs=[
            pl.BlockSpec((4, TILE, dq), lambda i: (0, i, 0)),
            pl.BlockSpec((TILE, d_a), lambda i: (i, 0)),
            pl.BlockSpec((1, d_a), lambda i: (0, 0)),
            pl.BlockSpec((n_g, TILE, TILE), lambda i: (0, 0, 0)),
            pl.BlockSpec((TILE, d_a), lambda i: (0, 0)),
        ],
        out_specs=[
            pl.BlockSpec((4, TILE, dq), lambda i: (0, i, 0)),
            pl.BlockSpec((n_g, TILE, TILE), lambda i: (0, 0, 0)),
            pl.BlockSpec((n_g * SUBLANE, TILE), lambda i: (0, 0)),
            pl.BlockSpec((1, d_a), lambda i: (0, 0)),
        ],
        out_shape=[
            jax.ShapeDtypeStruct((4, s, dq), BF16),
            jax.ShapeDtypeStruct((n_g, TILE, TILE), F32),
            jax.ShapeDtypeStruct((n_g * SUBLANE, TILE), F32),
            jax.ShapeDtypeStruct((1, d_a), F32),
        ],
        scratch_shapes=[pltpu.VMEM((TILE, d_a), F32)])


def _sigmoid(x):
    return 1.0 / (1.0 + jnp.exp(-x))


def _conv3(ext, w, b):
    return b + ((w[0:1] * pltpu.roll(ext, 2, 0) + w[1:2] * pltpu.roll(ext, 1, 0)) + w[2:3] * ext)


def _act_blocks(s, ns):
    return _pick(s, 512, ROWS), _pick(ns, 256)


def _ffn_act_fwd(a, cw, cb, name, rides=()):
    _, s, ns = a.shape
    bs, cb_w = _act_blocks(s, ns)
    hb = bs // SUBLANE

    def body(a_ref, prev_ref, cw_ref, cb_ref, o_ref):
        first = pl.program_id(0) == 0

        def conv(comp):
            prev = jnp.where(first, 0.0, prev_ref[comp])
            ext = jnp.concatenate([prev, a_ref[comp]], axis=0)
            return _conv3(ext, cw_ref[comp], cb_ref[comp])[SUBLANE:]

        for p in range(2):
            cg = conv(p)
            o_ref[p] = (cg * _sigmoid(cg) * conv(2 + p)).astype(BF16)

    hm, rode = _hosted_call(
        body, [a, a, cw, cb], name=name, grid=(s // bs, ns // cb_w),
        in_specs=[
            pl.BlockSpec((4, bs, cb_w), lambda i, j: (0, i, j)),
            pl.BlockSpec((4, SUBLANE, cb_w), lambda i, j: (0, jnp.maximum(i * hb - 1, 0), j)),
            pl.BlockSpec((4, 3, cb_w), lambda i, j: (0, 0, j)),
            pl.BlockSpec((4, 1, cb_w), lambda i, j: (0, 0, j)),
        ],
        out_specs=pl.BlockSpec((2, bs, cb_w), lambda i, j: (0, i, j)),
        out_shape=jax.ShapeDtypeStruct((2, s, ns), BF16),
        semantics=("parallel", "parallel"), rides=rides)
    return (hm, rode) if rides else hm


def _ffn_act_bwd(a, dhm, cw, cb, name, rides=()):
    _, s, ns = a.shape
    bs, cb_w = _act_blocks(s, ns)
    hb = bs // SUBLANE
    n_i = s // bs
    n_ext = bs + 2 * SUBLANE
    cur = slice(SUBLANE, SUBLANE + bs)

    def body(a_ref, prev_ref, next_ref, d_ref, dnext_ref, cw_ref, cb_ref, da_ref, dcw_ref, dcb_ref):
        i = pl.program_id(1)
        first, last = i == 0, i == n_i - 1

        @pl.when(first)
        def _():
            dcw_ref[...] = jnp.zeros_like(dcw_ref)
            dcb_ref[...] = jnp.zeros_like(dcb_ref)

        def ext_of(comp):
            return jnp.concatenate([jnp.where(first, 0.0, prev_ref[comp]), a_ref[comp], next_ref[comp]], axis=0)

        def back(comp, a_ext, dc):
            w = cw_ref[comp]
            da = (w[2:3] * dc + w[1:2] * pltpu.roll(dc, n_ext - 1, 0)) + w[0:1] * pltpu.roll(dc, n_ext - 2, 0)
            da_ref[comp] = da[cur].astype(BF16)
            dcc = dc[cur]
            dcw_ref[comp, 0:1, :] += jnp.sum(dcc * pltpu.roll(a_ext, 2, 0)[cur], axis=0, keepdims=True)
            dcw_ref[comp, 1:2, :] += jnp.sum(dcc * pltpu.roll(a_ext, 1, 0)[cur], axis=0, keepdims=True)
            dcw_ref[comp, 2:3, :] += jnp.sum(dcc * a_ext[cur], axis=0, keepdims=True)
            dcb_ref[comp] += jnp.sum(dcc, axis=0, keepdims=True)

        for p in range(2):
            ag, av = ext_of(p), ext_of(2 + p)
            cg = _conv3(ag, cw_ref[p], cb_ref[p])
            cv = _conv3(av, cw_ref[2 + p], cb_ref[2 + p])
            d = jnp.concatenate(
                [jnp.zeros((SUBLANE, cb_w), F32), d_ref[p], jnp.where(last, 0.0, dnext_ref[p])], axis=0)
            sg = _sigmoid(cg)
            back(2 + p, av, d * (cg * sg))
            back(p, ag, d * cv * (sg * (1.0 + cg * (1.0 - sg))))

    return _hosted_call(
        body, [a, a, a, dhm, dhm, cw, cb], name=name, grid=(ns // cb_w, n_i),
        in_specs=[
            pl.BlockSpec((4, bs, cb_w), lambda j, i: (0, i, j)),
            pl.BlockSpec((4, SUBLANE, cb_w), lambda j, i: (0, jnp.maximum(i * hb - 1, 0), j)),
            pl.BlockSpec((4, SUBLANE, cb_w), lambda j, i: (0, jnp.minimum((i + 1) * hb, n_i * hb - 1), j)),
            pl.BlockSpec((2, bs, cb_w), lambda j, i: (0, i, j)),
            pl.BlockSpec((2, SUBLANE, cb_w), lambda j, i: (0, jnp.minimum((i + 1) * hb, n_i * hb - 1), j)),
            pl.BlockSpec((4, 3, cb_w), lambda j, i: (0, 0, j)),
            pl.BlockSpec((4, 1, cb_w), lambda j, i: (0, 0, j)),
        ],
        out_specs=[
            pl.BlockSpec((4, bs, cb_w), lambda j, i: (0, i, j)),
            pl.BlockSpec((4, 3, cb_w), lambda j, i: (0, 0, j)),
            pl.BlockSpec((4, 1, cb_w), lambda j, i: (0, 0, j)),
        ],
        out_shape=[
            jax.ShapeDtypeStruct((4, s, ns), BF16),
            jax.ShapeDtypeStruct((4, 3, ns), F32),
            jax.ShapeDtypeStruct((4, 1, ns), F32),
        ],
        semantics=("parallel", "arbitrary"), rides=rides)


ATT_BQ_FWD = 2048
ATT_BQ_BWD = 1024
ATT_BK = 256
ATT_UNROLL = 2
ATT_UNROLL_BWD = 4


def _att_blocks(s, bq_pref):
    bq = _pick(s, bq_pref)
    bk = min(ATT_BK, bq)
    assert bq % bk == 0
    return bq, bk


def _dot_sel2(x, sel):
    hi = x.astype(BF16)
    lo = (x - hi.astype(F32)).astype(BF16)
    n = x.shape[0]
    both = jnp.dot(jnp.concatenate([hi, lo], axis=0), sel, preferred_element_type=F32)
    return both[:n] + both[n:]


def _causal_mask(bq, bk, row0, col0):
    rows = row0 + lax.broadcasted_iota(jnp.int32, (bq, bk), 0)
    cols = col0 + lax.broadcasted_iota(jnp.int32, (bq, bk), 1)
    return cols < rows


def _sb_tile(qb, kb, scale, mask):
    z = lax.dot_general(qb, kb, _DIMS["nt"], preferred_element_type=F32) * scale
    e = jnp.exp(-jnp.abs(z))
    lb = jnp.minimum(z, 0.0) - jnp.log(1.0 + e)
    l1m = lb - z
    if mask is not None:
        l1m = jnp.where(mask, l1m, 0.0)
    return z, e, lb, l1m


def _attn_fwd(q, k, v, name, rides=()):
    s, hd = q.shape
    bq, bk = _att_blocks(s, ATT_BQ_FWD)
    r = bq // bk
    unroll = math.gcd(r, ATT_UNROLL)
    n_h, n_q = hd // TILE, s // bq
    scale = 1.0 / math.sqrt(TILE)

    def body(q_ref, k_ref, v_ref, o_ref, l_ref, acc_ref, suf_ref):
        i = pl.program_id(1)
        qb = q_ref[...]
        later = _tri(bk, "gt")
        acc_ref[...] = jnp.zeros_like(acc_ref)
        suf_ref[...] = jnp.zeros_like(suf_ref)

        def tile(j, row0):
            rows = pl.ds(pl.multiple_of(j * bk, bk), bk)
            masked = row0 is not None
            r0 = row0 if masked else 0
            rs = pl.ds(r0, bq - r0)
            mask = _causal_mask(bq - r0, bk, i * bq + r0, j * bk) if masked else None
            _, _, lb, l1m = _sb_tile(qb[r0:], k_ref[rows, :], scale, mask)
            a = jnp.exp(lb + _dot_sel2(l1m, later) + suf_ref[rs, :])
            if masked:
                a = jnp.where(mask, a, 0.0)
            acc_ref[rs, :] += jnp.dot(a.astype(BF16), v_ref[rows, :], preferred_element_type=F32)
            suf_ref[rs, :] += jnp.sum(l1m, axis=1, keepdims=True)

        for dgl in range(r - 1, -1, -1):
            tile(r * i + dgl, dgl * bk)

        def step(t, carry):
            for u in range(unroll):
                tile(r * i - 1 - (unroll * t + u), None)
            return carry

        lax.fori_loop(0, (r * i) // unroll, step, 0)
        o_ref[...] = acc_ref[...].astype(BF16)
        l_ref[...] = jnp.broadcast_to(suf_ref[...], (bq, TILE))

    blk = pl.BlockSpec((bq, TILE), lambda h, i: (i, h))
    head = pl.BlockSpec((s, TILE), lambda h, i: (0, h))
    return _hosted_call(
        body, [q, k, v], name=name, grid=(n_h, n_q), in_specs=[blk, head, head], out_specs=[blk, blk],
        out_shape=[jax.ShapeDtypeStruct((s, hd), BF16), jax.ShapeDtypeStruct((s, hd), F32)],
        scratch_shapes=[pltpu.VMEM((bq, TILE), F32), pltpu.VMEM((bq, 1), F32)],
        semantics=("parallel", "parallel"), rides=rides)


def _attn_bwd(q, k, v, do, lsum, name, rides=()):
    s, hd = q.shape
    bq, bk = _att_blocks(s, ATT_BQ_BWD)
    r = bq // bk
    unroll = math.gcd(r, ATT_UNROLL_BWD)
    n_h, n_q = hd // TILE, s // bq
    scale = 1.0 / math.sqrt(TILE)

    def body(q_ref, k_ref, v_ref, do_ref, l_ref, dq_ref, dk_ref, dv_ref, dq_acc, pre_ref, cp_ref):
        i = pl.program_id(1)

        @pl.when(i == 0)
        def _():
            dk_ref[...] = jnp.zeros_like(dk_ref)
            dv_ref[...] = jnp.zeros_like(dv_ref)

        qb = q_ref[...]
        dob = do_ref[...]
        upto = _tri(bk, "le")
        before = _tri(bk, "lt")
        dq_acc[...] = jnp.zeros_like(dq_acc)
        pre_ref[...] = jnp.zeros_like(pre_ref)
        cp_ref[...] = jnp.zeros_like(cp_ref)

        def tile(j, row0):
            rows = pl.ds(pl.multiple_of(j * bk, bk), bk)
            kb, vb = k_ref[rows, :], v_ref[rows, :]
            masked = row0 is not None
            r0 = row0 if masked else 0
            rs = pl.ds(r0, bq - r0)
            qs, dos = qb[r0:], dob[r0:]
            mask = _causal_mask(bq - r0, bk, i * bq + r0, j * bk) if masked else None
            z, e, lb, l1m = _sb_tile(qs, kb, scale, mask)
            suffix = (l_ref[rs, 0:1] - pre_ref[rs, :]) - _dot_sel2(l1m, upto)
            a = jnp.exp(lb + suffix)
            if masked:
                a = jnp.where(mask, a, 0.0)
            p = a * lax.dot_general(dos, vb, _DIMS["nt"], preferred_element_type=F32)
            both = p + (cp_ref[rs, :] + jnp.dot(p.astype(BF16), before, preferred_element_type=F32))
            sg = jnp.where(z >= 0.0, 1.0, e) * pl.reciprocal(1.0 + e, approx=True)
            dz = p - both * sg
            if masked:
                dz = jnp.where(mask, dz, 0.0)
            dz = (dz * scale).astype(BF16)
            dq_acc[rs, :] += jnp.dot(dz, kb, preferred_element_type=F32)
            dk_ref[rows, :] += lax.dot_general(dz, qs, _DIMS["tn"], preferred_element_type=F32)
            dv_ref[rows, :] += lax.dot_general(a.astype(BF16), dos, _DIMS["tn"], preferred_element_type=F32)
            pre_ref[rs, :] += jnp.sum(l1m, axis=1, keepdims=True)
            cp_ref[rs, :] += jnp.sum(p, axis=1, keepdims=True)

        def step(j, carry):
            for u in range(unroll):
                tile(unroll * j + u, None)
            return carry

        lax.fori_loop(0, (r * i) // unroll, step, 0)
        for dgl in range(r):
            tile(r * i + dgl, dgl * bk)
        dq_ref[...] = dq_acc[...].astype(BF16)

    blk = pl.BlockSpec((bq, TILE), lambda h, i: (i, h))
    head = pl.BlockSpec((s, TILE), lambda h, i: (0, h))
    return _hosted_call(
        body, [q, k, v, do, lsum], name=name, grid=(n_h, n_q), in_specs=[blk, head, head, blk, blk],
        out_specs=[blk, head, head],
        out_shape=[jax.ShapeDtypeStruct((s, hd), BF16), jax.ShapeDtypeStruct((s, hd), F32),
                   jax.ShapeDtypeStruct((s, hd), F32)],
        scratch_shapes=[pltpu.VMEM((bq, TILE), F32), pltpu.VMEM((bq, 1), F32), pltpu.VMEM((bq, 1), F32)],
        semantics=("parallel", "arbitrary"), rides=rides)


EW_BLOCK = 512 * 1024


def _ew_blocks(r, c, elems=EW_BLOCK):
    return _pick(r, max(ROWS, elems // c // ROWS * ROWS), ROWS), c


def _cast_bf16(w, layer, chip_idx, name):
    _, r, c = w.shape
    br, bc = _ew_blocks(r, c)

    def body(chip_ref, w_ref, o_ref):
        o_ref[...] = w_ref[...].astype(BF16)

    return pl.pallas_call(
        body, name=name,
        grid_spec=pltpu.PrefetchScalarGridSpec(
            num_scalar_prefetch=1, grid=(r // br, c // bc),
            in_specs=[pl.BlockSpec((None, br, bc), lambda i, j, chip_ref: (layer, i, j))],
            out_specs=pl.BlockSpec((None, br, bc), lambda i, j, chip_ref: (chip_ref[0], i, j)),
        ),
        out_shape=jax.ShapeDtypeStruct((N_CHIPS, r, c), BF16), compiler_params=_cp("parallel", "parallel"),
    )(chip_idx, w)


def _pair_add(dw, recv, c_idx, name):
    _, r, c = dw.shape
    hr = r // 2
    br, bc = _ew_blocks(hr, c)
    nb = hr // br

    def body(c_ref, a_ref, b_ref, o_ref):
        o_ref[...] = (a_ref[...].astype(F32) + b_ref[...].astype(F32)).astype(BF16)

    return pl.pallas_call(
        body, name=name,
        grid_spec=pltpu.PrefetchScalarGridSpec(
            num_scalar_prefetch=1, grid=(N_CHIPS, nb, c // bc),
            in_specs=[
                pl.BlockSpec((None, br, bc), lambda s, i, j, c_ref: (s, c_ref[0] * nb + i, j)),
                pl.BlockSpec((None, br, bc), lambda s, i, j, c_ref: (s, i, j)),
            ],
            out_specs=pl.BlockSpec((None, br, bc), lambda s, i, j, c_ref: (s, i, j)),
        ),
        out_shape=jax.ShapeDtypeStruct((N_CHIPS, hr, c), BF16),
        compiler_params=_cp("parallel", "parallel", "parallel"),
    )(c_idx, dw, recv)


def _chip_sum(parts, dest, shape, layer, c_idx, name):
    _, hr, c = parts.shape
    br, bc = _ew_blocks(hr, c, EW_BLOCK // 2)
    nb = hr // br

    def body(c_ref, p_ref, *refs):
        o_ref = refs[-1]
        acc = p_ref[0].astype(F32)
        for s in range(1, N_CHIPS):
            acc = acc + p_ref[s].astype(F32)
        o_ref[...] = acc

    in_specs = [pl.BlockSpec((N_CHIPS, br, bc), lambda i, j, c_ref: (0, i, j))]
    operands = [c_idx, parts]
    aliases = {}
    if dest is not None:
        in_specs.append(ANY)
        operands.append(dest)
        aliases = {2: 0}
    return pl.pallas_call(
        body, name=name,
        grid_spec=pltpu.PrefetchScalarGridSpec(
            num_scalar_prefetch=1, grid=(nb, c // bc), in_specs=in_specs,
            out_specs=pl.BlockSpec((None, br, bc), lambda i, j, c_ref: (layer, c_ref[0] * nb + i, j)),
        ),
        out_shape=jax.ShapeDtypeStruct(shape, F32), input_output_aliases=aliases,
        compiler_params=_cp("parallel", "parallel"),
    )(*operands)


def _adamw(w, g, m, v, name, pass_g=False):
    n_l, r, c = w.shape
    br, bc = _ew_blocks(r, c, EW_BLOCK // 2)

    def body(w_ref, g_ref, m_ref, v_ref, *out_refs):
        d_ref, mo_ref, vo_ref = out_refs[-3:]
        g = g_ref[...]
        if pass_g:
            out_refs[0][...] = g
        m = ADAM_B1 * m_ref[...] + (1.0 - ADAM_B1) * g
        v = ADAM_B2 * v_ref[...] + (1.0 - ADAM_B2) * (g * g)
        m_hat = m / (1.0 - ADAM_B1 ** ADAM_STEP)
        v_hat = v / (1.0 - ADAM_B2 ** ADAM_STEP)
        d_ref[...] = -ADAM_LR * (m_hat / (jnp.sqrt(v_hat) + ADAM_EPS) + ADAM_WD * w_ref[...])
        mo_ref[...] = m
        vo_ref[...] = v

    blk = pl.BlockSpec((None, br, bc), lambda l, i, j: (l, i, j))
    n_out = 4 if pass_g else 3
    return pl.pallas_call(
        body, name=name, grid=(n_l, r // br, c // bc), in_specs=[blk] * 4, out_specs=[blk] * n_out,
        out_shape=[jax.ShapeDtypeStruct(w.shape, F32)] * n_out,
        compiler_params=_cp("parallel", "parallel", "parallel"),
    )(w, g, m, v)


def _place():
    x, y, c = lax.axis_index("x"), lax.axis_index("y"), lax.axis_index("c")
    chips = [(1 - x, y), (x, 1 - y), (1 - x, 1 - y)]
    return x, y, c, chips


class _Ride:
    def __init__(self, reads, bufs, new, n_sems, start, finish):
        self.reads, self.bufs, self.new, self.n_sems, self.start, self.finish = reads, bufs, new, n_sems, start, finish


def _hosted_call(body, operands, *, name, grid, in_specs, out_specs, out_shape, scratch_shapes=(), semantics=(), rides=()):
    single = not isinstance(out_shape, (list, tuple))
    out_specs = [out_specs] if single else list(out_specs)
    out_shape = [out_shape] if single else list(out_shape)
    in_specs, scratch_shapes = list(in_specs), list(scratch_shapes)
    n_in, n_out, n_scr = len(in_specs), len(out_shape), len(scratch_shapes)
    extra_in, extra_out, aliases, where = [], [], {}, []
    for ride in rides:
        r0 = len(extra_in)
        extra_in += list(ride.reads)
        b0 = len(extra_in)
        extra_in += list(ride.bufs)
        ob0 = len(extra_out)
        extra_out += [jax.ShapeDtypeStruct(b.shape, b.dtype) for b in ride.bufs]
        for t in range(len(ride.bufs)):
            aliases[n_in + b0 + t] = n_out + ob0 + t
        on0 = len(extra_out)
        extra_out += list(ride.new)
        where.append((r0, len(ride.reads), ob0, len(ride.bufs), on0, len(ride.new)))
    n_ein, n_eout = len(extra_in), len(extra_out)
    sem_shapes = [pltpu.SemaphoreType.DMA((max(1, k),)) for ride in rides for k in ride.n_sems]

    def full_body(*refs):
        ins, outs, scr = refs[:n_in + n_ein], refs[n_in + n_ein:n_in + n_ein + n_out + n_eout], refs[n_in + n_ein + n_out + n_eout:]

        def run(which):
            for idx, (ride, (r0, nr, ob0, nb, on0, nn)) in enumerate(zip(rides, where)):
                fn = ride.start if which == 0 else ride.finish
                fn(ins[n_in + r0:n_in + r0 + nr], outs[n_out + ob0:n_out + ob0 + nb], outs[n_out + on0:n_out + on0 + nn],
                   *scr[n_scr + 3 * idx:n_scr + 3 * idx + 3])

        host = lambda: body(*ins[:n_in], *outs[:n_out], *scr[:n_scr])
        if not rides:
            host()
        elif not grid:
            run(0)
            host()
            run(1)
        else:
            ids = [pl.program_id(ax) for ax in range(len(grid))]
            first = functools.reduce(jnp.logical_and, [i == 0 for i in ids])
            last = functools.reduce(jnp.logical_and, [i == g - 1 for i, g in zip(ids, grid)])
            pl.when(first)(lambda: run(0))
            host()
            pl.when(last)(lambda: run(1))

    if rides:
        params = pltpu.CompilerParams(dimension_semantics=("arbitrary",) * len(grid), vmem_limit_bytes=VMEM_LIMIT)
    else:
        params = _cp(*semantics)
    outs = pl.pallas_call(
        full_body, name=name, grid=grid,
        in_specs=in_specs + [ANY] * n_ein, out_specs=out_specs + [ANY] * n_eout,
        out_shape=out_shape + extra_out, input_output_aliases=aliases,
        scratch_shapes=scratch_shapes + sem_shapes, compiler_params=params,
    )(*operands, *extra_in)
    main = outs[0] if single else list(outs[:n_out])
    rode = [(list(outs[n_out + ob0:n_out + ob0 + nb]), list(outs[n_out + on0:n_out + on0 + nn]))
            for (_, _, ob0, nb, on0, nn) in where]
    return main, rode


def _run_rides(rides, name):
    return _hosted_call(lambda: None, [], name=name, grid=(), in_specs=[], out_specs=[], out_shape=[], rides=rides)[1]


def _ride_gather(slots, part=0, n_parts=1, span=1, stage=None):
    n = len(slots)
    halves = [a.shape[1] // 2 for a in slots]
    sizes = [hr // n_parts for hr in halves]
    assert part + span <= n_parts
    for a, hr, size in zip(slots, halves, sizes):
        assert a.shape[1] == 2 * hr and hr == size * n_parts and size % ROWS == 0, a.shape

    def remote(bufs, send_sems, recv_sems, i, k, slot, core, to):
        rows = bufs[i].at[slot, pl.ds(pl.multiple_of(core * halves[i] + part * sizes[i], ROWS), span * sizes[i])]
        return pltpu.make_async_remote_copy(
            src_ref=rows, dst_ref=rows, send_sem=send_sems.at[i * 6 + k], recv_sem=recv_sems.at[i * 6 + k],
            device_id=to, device_id_type=MESH)

    def each(fn):
        x, y, c, chips = _place()
        for i in range(n):
            for k, (px, py) in enumerate(chips):
                fn(x, y, c, i, k, px, py)

    def start(reads, bufs, new, send_sems, recv_sems, local_sems):
        cp = functools.partial(remote, bufs, send_sems, recv_sems)
        if stage != "d2d":
            each(lambda x, y, c, i, k, px, py: cp(i, k, 2 * x + y, c, (px, py, c)).start())
        else:
            each(lambda x, y, c, i, k, px, py: cp(i, 3 + k, 2 * px + py, c, (x, y, 1 - c)).start())

    def finish(reads, bufs, new, send_sems, recv_sems, local_sems):
        cp = functools.partial(remote, bufs, send_sems, recv_sems)

        def landed_over_ici(x, y, c, i, k, px, py):
            cp(i, k, 2 * px + py, c, (x, y, c)).wait_recv()
            if stage is None:
                cp(i, 3 + k, 2 * px + py, c, (x, y, 1 - c)).start()

        if stage != "d2d":
            each(landed_over_ici)
        if stage != "ici":
            each(lambda x, y, c, i, k, px, py: cp(i, 3 + k, 2 * px + py, 1 - c, (x, y, c)).wait_recv())
        if stage != "d2d":
            each(lambda x, y, c, i, k, px, py: cp(i, k, 2 * x + y, c, (px, py, c)).wait_send())
        if stage != "ici":
            each(lambda x, y, c, i, k, px, py: cp(i, 3 + k, 2 * px + py, c, (x, y, 1 - c)).wait_send())

    return _Ride([], slots, [], (6 * n, 6 * n, 0), start, finish)


def _ride_swap(grads):
    n = len(grads)
    halves = [a.shape[1] // 2 for a in grads]

    def copies(reads, new, send_sems, recv_sems):
        x, y, c, _ = _place()
        out = []
        for i in range(n):
            rows = pl.ds(pl.multiple_of((1 - c) * halves[i], 2 * SUBLANE), halves[i])
            out.append(pltpu.make_async_remote_copy(
                src_ref=reads[i].at[:, rows, :], dst_ref=new[i], send_sem=send_sems.at[i], recv_sem=recv_sems.at[i],
                device_id=(x, y, 1 - c), device_id_type=MESH))
        return out

    def start(reads, bufs, new, send_sems, recv_sems, local_sems):
        for cp in copies(reads, new, send_sems, recv_sems):
            cp.start()

    def finish(reads, bufs, new, send_sems, recv_sems, local_sems):
        for cp in copies(reads, new, send_sems, recv_sems):
            cp.wait()

    shapes = [jax.ShapeDtypeStruct((N_CHIPS, hr, a.shape[2]), a.dtype) for a, hr in zip(grads, halves)]
    return _Ride(grads, [], shapes, (n, n, 0), start, finish)


def _ride_scatter(parts, part=0, n_parts=1, into=None, span=1):
    n = len(parts)
    sizes = [a.shape[1] // n_parts for a in parts]
    assert part + span <= n_parts
    for a, size in zip(parts, sizes):
        assert a.shape[1] == size * n_parts and size % ROWS == 0, a.shape

    def piece(ref, i, slot):
        return ref.at[slot, pl.ds(part * sizes[i], span * sizes[i])]

    def own(reads, land, local_sems, i):
        me = 2 * lax.axis_index("x") + lax.axis_index("y")
        return pltpu.make_async_copy(piece(reads[i], i, me), piece(land[i], i, me), local_sems.at[i])

    def send(reads, land, send_sems, recv_sems, i, k):
        x, y, c, chips = _place()
        px, py = chips[k]
        return pltpu.make_async_remote_copy(
            src_ref=piece(reads[i], i, 2 * px + py), dst_ref=piece(land[i], i, 2 * x + y),
            send_sem=send_sems.at[3 * i + k], recv_sem=recv_sems.at[3 * i + k],
            device_id=(px, py, c), device_id_type=MESH)

    def start(reads, bufs, new, send_sems, recv_sems, local_sems):
        land = new if into is None else bufs
        for i in range(n):
            own(reads, land, local_sems, i).start()
            for k in range(3):
                send(reads, land, send_sems, recv_sems, i, k).start()

    def finish(reads, bufs, new, send_sems, recv_sems, local_sems):
        land = new if into is None else bufs
        x, y, c, chips = _place()
        for i in range(n):
            for k, (px, py) in enumerate(chips):
                slot = piece(land[i], i, 2 * px + py)
                pltpu.make_async_remote_copy(
                    src_ref=slot, dst_ref=slot, send_sem=send_sems.at[3 * i + k], recv_sem=recv_sems.at[3 * i + k],
                    device_id=(x, y, c), device_id_type=MESH).wait_recv()
        for i in range(n):
            for k in range(3):
                send(reads, land, send_sems, recv_sems, i, k).wait_send()
            own(reads, land, local_sems, i).wait()

    shapes = [jax.ShapeDtypeStruct(a.shape, a.dtype) for a in parts]
    if into is None:
        return _Ride(parts, [], shapes, (3 * n, 3 * n, n), start, finish)
    return _Ride(parts, list(into), [], (3 * n, 3 * n, n), start, finish)


def _ride_join(grads):
    n = len(grads)

    def copy(bufs, send_sems, recv_sems, i, core, to):
        hr = grads[i].shape[1] // 2
        rows = bufs[i].at[:, pl.ds(pl.multiple_of(core * hr, SUBLANE), hr), :]
        return pltpu.make_async_remote_copy(
            src_ref=rows, dst_ref=rows, send_sem=send_sems.at[i], recv_sem=recv_sems.at[i],
            device_id=to, device_id_type=MESH)

    def start(reads, bufs, new, send_sems, recv_sems, local_sems):
        x, y, c, _ = _place()
        for i in range(n):
            copy(bufs, send_sems, recv_sems, i, c, (x, y, 1 - c)).start()

    def finish(reads, bufs, new, send_sems, recv_sems, local_sems):
        x, y, c, _ = _place()
        for i in range(n):
            copy(bufs, send_sems, recv_sems, i, 1 - c, (x, y, c)).wait_recv()
        for i in range(n):
            copy(bufs, send_sems, recv_sems, i, c, (x, y, 1 - c)).wait_send()

    return _Ride([], grads, [], (n, n, 0), start, finish)


def _all_reduce_small(packed, name, rides=()):
    r, c = packed.shape
    pr = r // N_DEV
    assert r == pr * N_DEV and pr % ROWS == 0

    def body(x_ref, out_ref, parts, send1, recv1, send2, recv2):
        x, y, cc, _ = _place()
        mine = 4 * x + 2 * y + cc
        flip = lambda v, bit: 1 - v if bit else v
        peers = [(flip(x, k & 4), flip(y, k & 2), flip(cc, k & 1)) for k in range(1, N_DEV)]

        def piece(ref, dev):
            return ref.at[pl.ds(pl.multiple_of(dev * pr, ROWS), pr), :]

        def to_owner(k, peer):
            px, py, pc = peer
            return pltpu.make_async_remote_copy(
                src_ref=piece(x_ref, 4 * px + 2 * py + pc), dst_ref=parts.at[mine],
                send_sem=send1.at[k], recv_sem=recv1.at[k], device_id=peer, device_id_type=MESH)

        def to_all(k, peer):
            return pltpu.make_async_remote_copy(
                src_ref=piece(out_ref, mine), dst_ref=piece(out_ref, mine),
                send_sem=send2.at[k], recv_sem=recv2.at[k], device_id=peer, device_id_type=MESH)

        for k, peer in enumerate(peers):
            to_owner(k, peer).start()
        parts[mine] = x_ref[pl.ds(pl.multiple_of(mine * pr, ROWS), pr), :]
        for k, (px, py, pc) in enumerate(peers):
            landed = parts.at[4 * px + 2 * py + pc]
            pltpu.make_async_remote_copy(
                src_ref=landed, dst_ref=landed, send_sem=send1.at[k], recv_sem=recv1.at[k],
                device_id=(x, y, cc), device_id_type=MESH).wait_recv()
        acc = parts[0]
        for dev in range(1, N_DEV):
            acc = acc + parts[dev]
        out_ref[pl.ds(pl.multiple_of(mine * pr, ROWS), pr), :] = acc
        for k, peer in enumerate(peers):
            to_all(k, peer).start()
        for k, (px, py, pc) in enumerate(peers):
            theirs = piece(out_ref, 4 * px + 2 * py + pc)
            pltpu.make_async_remote_copy(
                src_ref=theirs, dst_ref=theirs, send_sem=send2.at[k], recv_sem=recv2.at[k],
                device_id=(x, y, cc), device_id_type=MESH).wait_recv()
        for k, peer in enumerate(peers):
            to_owner(k, peer).wait_send()
            to_all(k, peer).wait_send()

    return _hosted_call(
        body, [packed], name=name, grid=(), in_specs=[VMEM_SPEC], out_specs=VMEM_SPEC,
        out_shape=jax.ShapeDtypeStruct((r, c), F32),
        scratch_shapes=[pltpu.VMEM((N_DEV, pr, c), F32)] + [pltpu.SemaphoreType.DMA((N_DEV - 1,))] * 4,
        rides=rides)


_PACK_ROWS = 256


def _pack(arrays):
    flat = jnp.concatenate([a.reshape(-1).astype(F32) for a in arrays])
    unit = _PACK_ROWS * LANE
    total = -(-flat.shape[0] // unit) * unit
    return jnp.pad(flat, (0, total - flat.shape[0])).reshape(-1, LANE)


def _unpack(packed, shapes, lead=()):
    flat = packed.reshape(lead + (-1,))
    out, at = [], 0
    for s in shapes:
        size = math.prod(s)
        out.append(flat[..., at:at + size].reshape(lead + tuple(s)))
        at += size
    return out


def kernel(x, pre_mix_g, post_mix_g, pre_ffn_g, post_ffn_g, a_w_in, a_v_norm_g, a_w_spatial, a_b_spatial, a_w_out, kv_norm_g, w_k, w_v, b_w_q, b_w_o, ffn_w_up, ffn_conv_w, ffn_conv_b, ffn_w_down, loss_target, m_pre_mix_g, m_post_mix_g, m_pre_ffn_g, m_post_ffn_g, m_a_w_in, m_a_v_norm_g, m_a_w_spatial, m_a_b_spatial, m_a_w_out, m_kv_norm_g, m_w_k, m_w_v, m_b_w_q, m_b_w_o, m_ffn_w_up, m_ffn_conv_w, m_ffn_conv_b, m_ffn_w_down, v_pre_mix_g, v_post_mix_g, v_pre_ffn_g, v_post_ffn_g, v_a_w_in, v_a_v_norm_g, v_a_w_spatial, v_a_b_spatial, v_a_w_out, v_kv_norm_g, v_w_k, v_w_v, v_b_w_q, v_b_w_o, v_ffn_w_up, v_ffn_conv_w, v_ffn_conv_b, v_ffn_w_down):
    xi, yi, ci = lax.axis_index("x"), lax.axis_index("y"), lax.axis_index("c")
    chip = 2 * xi + yi
    c_idx = jnp.reshape(ci, (1,)).astype(jnp.int32)
    _, s, d = x.shape
    n_layers = pre_mix_g.shape[0]
    assert n_layers == 2 and a_w_in.shape[0] == 1 and b_w_q.shape[0] == 1
    d_a = a_w_out.shape[1] * N_CHIPS
    n_g = a_w_spatial.shape[1]
    ns = ffn_w_up.shape[2]
    assert a_w_spatial.shape[2] == TILE and d_a == n_g * TILE and s % TILE == 0
    h0 = x[0]
    target = loss_target[0]

    big = {
        "win": (a_w_in, m_a_w_in, v_a_w_in),
        "wout": (a_w_out, m_a_w_out, v_a_w_out),
        "wk": (w_k[None], m_w_k[None], v_w_k[None]),
        "wv": (w_v[None], m_w_v[None], v_w_v[None]),
        "wq": (b_w_q, m_b_w_q, v_b_w_q),
        "wo": (b_w_o, m_b_w_o, v_b_w_o),
        "wup": (ffn_w_up, m_ffn_w_up, v_ffn_w_up),
        "wdn": (ffn_w_down, m_ffn_w_down, v_ffn_w_down),
    }
    units = [(nm, layer) for nm in big for layer in range(big[nm][0].shape[0])]
    chip_idx = jnp.reshape(chip, (1,)).astype(jnp.int32)
    shards = [_cast_bf16(big[nm][0], layer, chip_idx, f"cast_{nm}{layer}") for nm, layer in units]
    small_sharded = _pack([a_v_norm_g, ffn_conv_w])
    small_sharded = lax.dynamic_update_index_in_dim(
        jnp.zeros((N_CHIPS,) + small_sharded.shape, F32), small_sharded, chip, 0)
    own = dict(zip(units, shards))
    full = {}

    def gather_ride(keys):
        return _ride_gather([own[key] for key in keys])

    def gathered(keys, rode):
        full.update(zip(keys, rode[0]))

    gains = lambda g, layer: g[layer:layer + 1]
    first_keys = [("win", 0)]
    (first_bufs, _), = _run_rides(
        [_ride_gather([own[key] for key in first_keys] + [small_sharded], stage="ici")], "gather_first")
    hn0, ((first_bufs, _),) = _rms_fwd(
        h0, gains(pre_mix_g, 0), "norm_in", rides=[_ride_gather(first_bufs, stage="d2d")])
    full.update(zip(first_keys, first_bufs[:-1]))
    vg_parts, cw_parts = _unpack(first_bufs[-1], [a_v_norm_g.shape, ffn_conv_w.shape], lead=(N_CHIPS,))
    v_g = jnp.transpose(vg_parts, (1, 0, 2)).reshape(1, d_a)

    def rows(nm, layer=0):
        w = full[(nm, layer)]
        return w.reshape(w.shape[0] * w.shape[1], w.shape[2])

    bias = jnp.repeat(a_b_spatial[0].T, TILE, axis=1)
    w_s = a_w_spatial[0]
    kv_g = kv_norm_g[None]
    conv_w = [cw_parts[:, layer] for layer in range(n_layers)]
    conv_b = [ffn_conv_b[layer].reshape(N_CHIPS, 1, ns) for layer in range(n_layers)]

    up0 = own[("wup", 0)]
    pieces = lambda p, span: _ride_gather([up0], part=p, n_parts=8, span=span)
    uv, ((out_bufs, _), ((up0,), _)) = _mm(
        hn0, full[("win", 0)], "nn", "gmlp_in", out_split=N_CHIPS, rides=[gather_ride([("wout", 0)]), pieces(0, 1)])
    full[("wout", 0)] = out_bufs[0]
    gm, (((up0,), _),) = _gmlp_fwd(uv, v_g, w_s, bias, "gmlp_gate", rides=[pieces(1, 2)])
    mix0, (((up0,), _),) = _mm(gm, rows("wout"), "nn", "gmlp_out", rides=[pieces(3, 2)])
    mix0 = mix0[0]
    (h1, hn1), (((up0,), _),) = _resid_rms(
        h0, mix0, gains(post_mix_g, 0), [gains(pre_ffn_g, 0)], "resid_mix0", rides=[pieces(5, 3)])
    full[("wup", 0)] = up0
    def leg(keys, stage):
        return _ride_gather([own[key] for key in keys], stage=stage)

    def first_leg_done(keys, rode):
        own.update(zip(keys, rode[0]))

    down0, qk, vo = [("wdn", 0)], [("wq", 0), ("wk", 0)], [("wv", 0), ("wo", 0)]
    a0, (rode,) = _mm(hn1, full[("wup", 0)], "nn", "ffn_up0", out_split=N_CHIPS, rides=[leg(down0, "ici")])
    first_leg_done(down0, rode)
    hm0, (rode, rode_qk) = _ffn_act_fwd(
        a0, conv_w[0], conv_b[0], "ffn_act0", rides=[leg(down0, "d2d"), leg(qk, "ici")])
    gathered(down0, rode)
    first_leg_done(qk, rode_qk)
    f0, (rode, rode_vo) = _mm(hm0, rows("wdn", 0), "nn", "ffn_down0", rides=[leg(qk, "d2d"), leg(vo, "ici")])
    gathered(qk, rode)
    first_leg_done(vo, rode_vo)
    f0 = f0[0]
    (h2, hn2, kvn), (rode,) = _resid_rms(
        h1, f0, gains(post_ffn_g, 0), [gains(pre_mix_g, 1), kv_g], "resid_ffn0", rides=[leg(vo, "d2d")])
    gathered(vo, rode)
    q = _mm(hn2, rows("wq"), "nn", "proj_q", out_dtype=BF16)[0]
    k = _mm(kvn, rows("wk"), "nn", "proj_k", out_dtype=BF16)[0]
    v = _mm(kvn, rows("wv"), "nn", "proj_v", out_dtype=BF16)[0]
    last_keys = [("wup", 1), ("wdn", 1)]
    (att, lsum), (rode,) = _attn_fwd(q, k, v, "attn_fwd", rides=[leg(last_keys, "ici")])
    first_leg_done(last_keys, rode)
    mix1, (rode,) = _mm(att, rows("wo"), "nn", "proj_o", rides=[leg(last_keys, "d2d")])
    gathered(last_keys, rode)
    mix1 = mix1[0]
    h3, hn3 = _resid_rms(h2, mix1, gains(post_mix_g, 1), [gains(pre_ffn_g, 1)], "resid_mix1")
    a1 = _mm(hn3, full[("wup", 1)], "nn", "ffn_up1", out_split=N_CHIPS)
    hm1 = _ffn_act_fwd(a1, conv_w[1], conv_b[1], "ffn_act1")
    f1 = _mm(hm1, rows("wdn", 1), "nn", "ffn_down1")[0]
    dh4, loss_tile = _loss_head(h3, f1, gains(post_ffn_g, 1), target, "loss_head")
    loss = lax.psum(loss_tile[0, 0], ("x", "y", "c"))

    dw = {}
    dg = {}

    pair = {}
    half_done = {nm: None for nm in big}

    def swap_ride(keys):
        return _ride_swap([dw[key] for key in keys])

    def swapped(keys, rode):
        for (nm, layer), got in zip(keys, rode[1]):
            pair[(nm, layer)] = _pair_add(dw[(nm, layer)], got, c_idx, f"pair_add_{nm}{layer}")

    def scatter_ride(keys):
        return _ride_scatter([pair[key] for key in keys])

    def scattered(keys, rode):
        for (nm, layer), got in zip(keys, rode[1]):
            half_done[nm] = _chip_sum(got, half_done[nm], big[nm][0].shape, layer, c_idx, f"chip_sum_{nm}{layer}")

    def ffn_bwd(dh_out, h_in, hn, a, hm, f, layer, act_rides=()):
        df, dg[("post_ffn", layer)] = _rms_bwd_out(dh_out, f, gains(post_ffn_g, layer), f"d_norm_ffn_out{layer}")
        dwd = _mm(hm, df, "tn", f"d_w_down{layer}", out_dtype=BF16)[0]
        down, up = [("wdn", layer)], [("wup", layer)]
        dw[down[0]] = dwd.reshape(N_CHIPS, dwd.shape[0] // N_CHIPS, d)
        dhm, (rode,) = _mm(df, rows("wdn", layer), "nt", f"d_ffn_mid{layer}", out_split=2, rides=[swap_ride(down)])
        swapped(down, rode)
        (da, dg[("conv_w", layer)], dg[("conv_b", layer)]), act_rode = _ffn_act_bwd(
            a, dhm, conv_w[layer], conv_b[layer], f"d_ffn_act{layer}", rides=act_rides)
        dw[up[0]], (rode,) = _mm(hn, da, "tn", f"d_w_up{layer}", out_dtype=BF16, out_split=N_CHIPS,
                                 rides=[scatter_ride(down)])
        scattered(down, rode)
        dhn, (rode,) = _mm(da, full[("wup", layer)], "nt", f"d_ffn_in{layer}", rides=[swap_ride(up)])
        swapped(up, rode)
        return dhn[0], act_rode

    dhn3, _ = ffn_bwd(dh4, h3, hn3, a1, hm1, f1, 1)
    dh3, (dg[("pre_ffn", 1)],) = _rms_bwd_in(dh4, h3, [([dhn3], gains(pre_ffn_g, 1))], "d_norm_ffn_in1")
    dmix1, dg[("post_mix", 1)] = _rms_bwd_out(dh3, mix1, gains(post_mix_g, 1), "d_norm_mix_out1")
    dwo = _mm(att, dmix1, "tn", "d_w_o", out_dtype=BF16)[0]
    dw[("wo", 0)] = dwo.reshape(N_CHIPS, dwo.shape[0] // N_CHIPS, d)
    datt = _mm(dmix1, rows("wo"), "nt", "d_attn_out", out_dtype=BF16)[0]
    ffn1_keys = [("wup", 1)]
    (dq, dk, dv), (rode,) = _attn_bwd(q, k, v, datt, lsum, "attn_bwd", rides=[scatter_ride(ffn1_keys)])
    scattered(ffn1_keys, rode)
    for nm, act, dact in (("wq", hn2, dq), ("wk", kvn, dk), ("wv", kvn, dv)):
        g = _mm(act, dact, "tn", f"d_{nm}", out_dtype=BF16)[0]
        dw[(nm, 0)] = g.reshape(N_CHIPS, g.shape[0] // N_CHIPS, g.shape[1])
    dhn2 = _mm(dq, rows("wq"), "nt", "d_q_in")[0]
    dkvn_k = _mm(dk, rows("wk"), "nt", "d_k_in")[0]
    attn_keys = [("wo", 0), ("wq", 0), ("wk", 0), ("wv", 0)]
    dkvn_v, (rode,) = _mm(dv, rows("wv"), "nt", "d_v_in", rides=[swap_ride(attn_keys)])
    swapped(attn_keys, rode)
    dh2, (dg[("pre_mix", 1)], dg["kv"]) = _rms_bwd_in(
        dh3, h2, [([dhn2], gains(pre_mix_g, 1)), ([dkvn_k, dkvn_v[0]], kv_g)], "d_norm_mix_in1")
    dhn1, (rode,) = ffn_bwd(dh2, h1, hn1, a0, hm0, f0, 0, act_rides=[scatter_ride(attn_keys)])
    scattered(attn_keys, rode)
    up0_pair = [pair[("wup", 0)]]
    up0_landed = [None]

    def up0_piece(part, span):
        return _ride_scatter(up0_pair, part, 8, into=up0_landed[0], span=span)

    def up0_rode(rode):
        up0_landed[0] = rode[1] if up0_landed[0] is None else rode[0]

    dh1, (dg[("pre_ffn", 0)],), (rode,) = _rms_bwd_in(
        dh2, h1, [([dhn1], gains(pre_ffn_g, 0))], "d_norm_ffn_in0", rides=[up0_piece(0, 1)])
    up0_rode(rode)
    dmix0, dg[("post_mix", 0)], (rode,) = _rms_bwd_out(
        dh1, mix0, gains(post_mix_g, 0), "d_norm_mix_out0", rides=[up0_piece(1, 1)])
    up0_rode(rode)
    early = ["wq", "wk", "wv", "wo", "wdn"]
    dwout, (((joined_early, _)),) = _mm(
        gm, dmix0, "tn", "d_w_out", out_dtype=BF16, rides=[_ride_join([half_done[nm] for nm in early])])
    grads_big = dict(zip(early, joined_early))
    w_out_key, w_in_key = [("wout", 0)], [("win", 0)]
    dw[w_out_key[0]] = dwout[0].reshape(N_CHIPS, dwout.shape[1] // N_CHIPS, d)
    dgm, (rode, up0) = _mm(dmix0, rows("wout"), "nt", "d_gmlp_gate", rides=[swap_ride(w_out_key), up0_piece(2, 1)])
    swapped(w_out_key, rode)
    up0_rode(up0)
    (duv, d_ws, d_bs, d_vg), (rode,) = _gmlp_bwd(uv, dgm[0], v_g, w_s, bias, "d_gmlp", rides=[up0_piece(3, 2)])
    up0_rode(rode)
    dw[w_in_key[0]], (rode, up0) = _mm(
        hn0, duv, "tn", "d_w_in", out_dtype=BF16, out_split=N_CHIPS, rides=[scatter_ride(w_out_key), up0_piece(5, 1)])
    scattered(w_out_key, rode)
    up0_rode(up0)
    dhn0, (rode, up0) = _mm(
        duv, full[("win", 0)], "nt", "d_gmlp_in", rides=[swap_ride(w_in_key), up0_piece(6, 2)])
    swapped(w_in_key, rode)
    up0_rode(up0)
    scattered([("wup", 0)], (None, up0_landed[0]))
    middle = ["wout", "wup"]
    dx, (dg[("pre_mix", 0)],), (rode, (joined_middle, _)) = _rms_bwd_in(
        dh1, h0, [([dhn0[0]], gains(pre_mix_g, 0))], "d_norm_in",
        rides=[scatter_ride(w_in_key), _ride_join([half_done[nm] for nm in middle])])
    scattered(w_in_key, rode)
    grads_big.update(zip(middle, joined_middle))

    stack = lambda key: jnp.concatenate([dg[(key, layer)] for layer in range(n_layers)], axis=0)
    small_parts = [
        stack("pre_mix"), stack("post_mix"), stack("pre_ffn"), stack("post_ffn"),
        d_vg, d_ws, d_bs[::SUBLANE], dg["kv"],
        jnp.stack([dg[("conv_w", layer)] for layer in range(n_layers)]),
        jnp.stack([dg[("conv_b", layer)] for layer in range(n_layers)]),
    ]
    late = [nm for nm in big if nm not in early + middle]
    summed, ((joined_late, _),) = _all_reduce_small(
        _pack(small_parts), "small_grads_sum", rides=[_ride_join([half_done[nm] for nm in late])])
    grads_big.update(zip(late, joined_late))
    (g_pre_mix, g_post_mix, g_pre_ffn, g_post_ffn, g_vg, g_ws, g_bs, g_kv, g_cw, g_cb) = _unpack(
        summed, [p.shape for p in small_parts])
    g_vg = lax.dynamic_index_in_dim(g_vg.reshape(N_CHIPS, 1, d_a // N_CHIPS), chip, 0, keepdims=False)
    g_cw = lax.dynamic_index_in_dim(g_cw, chip, 1, keepdims=False)
    g_cb = g_cb.reshape(n_layers, N_CHIPS * ns)
    small = [
        (pre_mix_g, g_pre_mix, m_pre_mix_g, v_pre_mix_g),
        (post_mix_g, g_post_mix, m_post_mix_g, v_post_mix_g),
        (pre_ffn_g, g_pre_ffn, m_pre_ffn_g, v_pre_ffn_g),
        (post_ffn_g, g_post_ffn, m_post_ffn_g, v_post_ffn_g),
        (a_v_norm_g, g_vg, m_a_v_norm_g, v_a_v_norm_g),
        (a_w_spatial, g_ws[None], m_a_w_spatial, v_a_w_spatial),
        (a_b_spatial, g_bs[None], m_a_b_spatial, v_a_b_spatial),
        (kv_norm_g, g_kv.reshape(d), m_kv_norm_g, v_kv_norm_g),
        (ffn_conv_w, g_cw, m_ffn_conv_w, v_ffn_conv_w),
        (ffn_conv_b, g_cb, m_ffn_conv_b, v_ffn_conv_b),
    ]
    small = [(w, g.reshape(w.shape), m, v) for w, g, m, v in small]
    packed = [_pack([t[i] for t in small])[None] for i in range(4)]
    small_new = [_unpack(p[0], [t[0].shape for t in small]) for p in _adamw(*packed, "adamw_small")]

    new_big = {nm: _adamw(big[nm][0], grads_big[nm], big[nm][1], big[nm][2], f"adamw_{nm}", pass_g=True)
               for nm in big}

    def big_out(nm, which):
        ref_shape = {"wk": w_k.shape, "wv": w_v.shape}.get(nm, big[nm][0].shape)
        return new_big[nm][which].reshape(ref_shape)

    order = ["pre_mix", "post_mix", "pre_ffn", "post_ffn", "win", "vg", "ws", "bs", "wout", "kv", "wk", "wv", "wq",
             "wo", "wup", "cw", "cb", "wdn"]
    small_at = {"pre_mix": 0, "post_mix": 1, "pre_ffn": 2, "post_ffn": 3, "vg": 4, "ws": 5, "bs": 6, "kv": 7,
                "cw": 8, "cb": 9}
    outs = [loss, dx[None]]
    for which in range(4):
        for nm in order:
            if nm in small_at:
                outs.append(small[small_at[nm]][1] if which == 0 else small_new[which - 1][small_at[nm]])
            else:
                outs.append(big_out(nm, which))
    return tuple(outs)
```
